```python
import jax, jax.numpy as jnp
from jax import lax
import numpy as np

D_MODEL = 1024
BATCH = 16
SEQ = 4096
DEPTH = 2

GRID_W = 64
HEAD_DIM = 64
ROPE_THETA = 10000.0
EPS = 1e-6
Q_BLOCK = 128
NEG = -1e30

MLA_HEADS = 8
MLA_Q_RANK = 256
MLA_KV_RANK = 128
MLA_NOPE = 64
MLA_ROPE = 32
MLA_V = 64
GQA_HEADS = 8
GQA_KV_HEADS = 2
EVEN_SPLITS = tuple(int(v) for v in np.cumsum([MLA_Q_RANK, MLA_KV_RANK, MLA_ROPE,
                                                GQA_HEADS * HEAD_DIM, GQA_KV_HEADS * HEAD_DIM]))
EVEN_IN = EVEN_SPLITS[-1] + GQA_KV_HEADS * HEAD_DIM
EVEN_MIX = MLA_HEADS * MLA_V + GQA_HEADS * HEAD_DIM
DIL_HEADS = D_MODEL // HEAD_DIM
DIL_PATTERNS = ((128, 1), (512, 4), (2048, 16))
DIL_BLOCK = 64
D_FF = 2816
N_EXPERTS = 8
TOP_K = 2
D_FF_EXPERT = 3584
MOE_BLOCK = 256
N_EVEN = (DEPTH + 1) // 2
N_ODD = DEPTH // 2

kernel_name = "hybrid_mla_axialgqa_dilated_moe_encoder"


def rmsnorm(x, g):
    x32 = x.astype(jnp.float32)
    y = x32 * lax.rsqrt(jnp.mean(x32 * x32, axis=-1, keepdims=True) + EPS)
    return (y * g.astype(jnp.float32)).astype(x.dtype)


def modulate(x, g, shift, scale):
    return rmsnorm(x, g) * (1.0 + scale[:, None, :]) + shift[:, None, :]


def rope(x, pos):
    dh = x.shape[-1]
    inv = ROPE_THETA ** (-jnp.arange(0, dh, 2, dtype=jnp.float32) / dh)
    ang = pos.astype(jnp.float32)[:, None] * inv[None, :]
    cos = jnp.cos(ang)[None, :, None, :]
    sin = jnp.sin(ang)[None, :, None, :]
    x1, x2 = jnp.split(x.astype(jnp.float32), 2, axis=-1)
    return jnp.concatenate([x1 * cos - x2 * sin, x1 * sin + x2 * cos], axis=-1).astype(x.dtype)


def axial_rope(x, row_idx, col_idx):
    half = x.shape[-1] // 2
    return jnp.concatenate([rope(x[..., :half], row_idx), rope(x[..., half:], col_idx)], axis=-1)


def blocked_attention(q, k, v):
    B, S, H, dq = q.shape
    Hk = k.shape[2]
    R = H // Hk
    dv = v.shape[-1]
    nblk = S // Q_BLOCK
    qb = q.reshape(B, nblk, Q_BLOCK, Hk, R, dq).transpose(1, 0, 2, 3, 4, 5)
    scale = dq ** -0.5

    def one_block(qblk):
        s = jnp.einsum('bqgrd,bkgd->bgrqk', qblk, k, preferred_element_type=jnp.float32) * scale
        p = jax.nn.softmax(s, axis=-1).astype(v.dtype)
        return jnp.einsum('bgrqk,bkgd->bqgrd', p, v)

    out = lax.map(one_block, qb)
    return out.transpose(1, 0, 2, 3, 4, 5).reshape(B, S, H, dv)


def mixer_even(h, w_in, q_norm, w_uq, kv_norm, w_ukv, mla_q_gain, mla_k_gain,
               gqa_q_gain, gqa_k_gain, w_out):
    B, S, _ = h.shape
    pos = jnp.arange(S, dtype=jnp.int32)
    rows = S // GRID_W
    row_idx = jnp.repeat(jnp.arange(rows, dtype=jnp.int32), GRID_W)
    col_idx = jnp.tile(jnp.arange(GRID_W, dtype=jnp.int32), rows)
    c_q, c_kv, k_pe, q_b, k_b, v_b = jnp.split(h @ w_in, EVEN_SPLITS, axis=-1)

    q_a = (rmsnorm(c_q, q_norm) @ w_uq).reshape(B, S, MLA_HEADS, MLA_NOPE + MLA_ROPE)
    kv = (rmsnorm(c_kv, kv_norm) @ w_ukv).reshape(B, S, MLA_HEADS, MLA_NOPE + MLA_V)
    k_nope, v_a = jnp.split(kv, [MLA_NOPE], axis=-1)
    k_a = jnp.concatenate(
        [k_nope, jnp.broadcast_to(k_pe[:, :, None, :], (B, S, MLA_HEADS, MLA_ROPE))], axis=-1)
    q_a = rmsnorm(q_a, mla_q_gain)
    k_a = rmsnorm(k_a, mla_k_gain)
    q_a = jnp.concatenate([q_a[..., :MLA_NOPE], rope(q_a[..., MLA_NOPE:], pos)], axis=-1)
    k_a = jnp.concatenate([k_a[..., :MLA_NOPE], rope(k_a[..., MLA_NOPE:], pos)], axis=-1)
    o_a = blocked_attention(q_a, k_a, v_a).reshape(B, S, MLA_HEADS * MLA_V)

    q_b = rmsnorm(q_b.reshape(B, S, GQA_HEADS, HEAD_DIM), gqa_q_gain)
    k_b = rmsnorm(k_b.reshape(B, S, GQA_KV_HEADS, HEAD_DIM), gqa_k_gain)
    v_b = v_b.reshape(B, S, GQA_KV_HEADS, HEAD_DIM)
    q_b = axial_rope(q_b, row_idx, col_idx)
    k_b = axial_rope(k_b, row_idx, col_idx)
    o_b = blocked_attention(q_b, k_b, v_b).reshape(B, S, GQA_HEADS * HEAD_DIM)

    return jnp.concatenate([o_a, o_b], axis=-1) @ w_out


def dilated_branch(q, k, v, dilation, half):
    B, S, H, dh = q.shape
    L = S // dilation
    nb = -(-L // DIL_BLOCK)
    Lp = nb * DIL_BLOCK
    qs = jnp.pad(q.reshape(B, L, dilation, H, dh), ((0, 0), (0, Lp - L), (0, 0), (0, 0), (0, 0)))
    pad_kv = ((0, 0), (DIL_BLOCK, Lp - L + DIL_BLOCK), (0, 0), (0, 0), (0, 0))
    ks = jnp.pad(k.reshape(B, L, dilation, H, dh), pad_kv)
    vs = jnp.pad(v.reshape(B, L, dilation, H, dh), pad_kv)
    a = jnp.arange(DIL_BLOCK)[:, None]
    b = jnp.arange(3 * DIL_BLOCK)[None, :]
    in_band = jnp.abs(b - DIL_BLOCK - a) <= half
    scale = dh ** -0.5

    def one_block(i):
        qb = lax.dynamic_slice_in_dim(qs, i * DIL_BLOCK, DIL_BLOCK, axis=1)
        kb = lax.dynamic_slice_in_dim(ks, i * DIL_BLOCK, 3 * DIL_BLOCK, axis=1)
        vb = lax.dynamic_slice_in_dim(vs, i * DIL_BLOCK, 3 * DIL_BLOCK, axis=1)
        kj = (i - 1) * DIL_BLOCK + b
        valid = in_band & (kj >= 0) & (kj < L)
        s = jnp.einsum('bqrhd,bkrhd->brhqk', qb, kb, preferred_element_type=jnp.float32) * scale
        s = jnp.where(valid, s, NEG)
        m = jnp.max(s, axis=-1, keepdims=True)
        p = jnp.exp(s - m)
        den = jnp.sum(p, axis=-1, keepdims=True)
        o = jnp.einsum('brhqk,bkrhd->bqrhd', (p / den).astype(v.dtype), vb)
        lse = (m + jnp.log(den))[..., 0]
        return o, lse.transpose(0, 3, 1, 2)

    o, lse = lax.map(one_block, jnp.arange(nb))
    o = o.transpose(1, 0, 2, 3, 4, 5).reshape(B, Lp, dilation, H, dh)[:, :L].reshape(B, S, H, dh)
    lse = lse.transpose(1, 0, 2, 3, 4).reshape(B, Lp, dilation, H)[:, :L].reshape(B, S, H)
    return o, lse


def mixer_odd(h, w_qkv, q_gain, k_gain, w_out):
    B, S, D = h.shape
    pos = jnp.arange(S, dtype=jnp.int32)
    q, k, v = jnp.split(h @ w_qkv, 3, axis=-1)
    q = rope(rmsnorm(q.reshape(B, S, DIL_HEADS, HEAD_DIM), q_gain), pos)
    k = rope(rmsnorm(k.reshape(B, S, DIL_HEADS, HEAD_DIM), k_gain), pos)
    v = v.reshape(B, S, DIL_HEADS, HEAD_DIM)
    outs, lses = [], []
    for window, dilation in DIL_PATTERNS:
        o, l = dilated_branch(q, k, v, dilation, window // (2 * dilation))
        outs.append(o)
        lses.append(l)
    w = jax.nn.softmax(jnp.stack(lses, axis=0), axis=0)
    o = sum(w[g][..., None] * outs[g].astype(jnp.float32) for g in range(len(DIL_PATTERNS)))
    return o.astype(h.dtype).reshape(B, S, D) @ w_out


def swiglu(h, wg, wu, wd):
    return (jax.nn.silu(h @ wg) * (h @ wu)) @ wd


def moe_swiglu(h, router, wg, wu, wd):
    B, S, D = h.shape
    T = B * S
    xt = h.reshape(T, D)
    logits = jnp.einsum('td,de->te', xt, router, preferred_element_type=jnp.float32)
    top_val, top_idx = lax.top_k(logits, TOP_K)
    gates = jax.nn.softmax(top_val, axis=-1)
    A = T * TOP_K
    e_flat = top_idx.reshape(A).astype(jnp.int32)
    g_flat = gates.reshape(A)
    tok_flat = jnp.repeat(jnp.arange(T, dtype=jnp.int32), TOP_K)
    order = jnp.argsort(e_flat)
    e_s, tok_s, g_s = e_flat[order], tok_flat[order], g_flat[order]
    counts = jnp.bincount(e_flat, length=N_EXPERTS)
    starts = jnp.cumsum(counts) - counts
    pcounts = (counts + MOE_BLOCK - 1) // MOE_BLOCK * MOE_BLOCK
    pends = jnp.cumsum(pcounts)
    pstarts = pends - pcounts
    dest = pstarts[e_s] + jnp.arange(A, dtype=jnp.int32) - starts[e_s]
    n_blk = -(-A // MOE_BLOCK) + N_EXPERTS
    P = n_blk * MOE_BLOCK
    tok_pad = jnp.zeros((P,), jnp.int32).at[dest].set(tok_s)
    g_pad = jnp.zeros((P,), jnp.float32).at[dest].set(g_s)
    blk_e = jnp.minimum(jnp.searchsorted(pends, jnp.arange(n_blk, dtype=jnp.int32) * MOE_BLOCK,
                                         side='right'), N_EXPERTS - 1)

    def one_block(args):
        tok, g, e = args
        xb = xt[tok]
        yb = (jax.nn.silu(xb @ wg[e]) * (xb @ wu[e])) @ wd[e]
        return (yb.astype(jnp.float32) * g[:, None]).astype(h.dtype)

    yb = lax.map(one_block, (tok_pad.reshape(n_blk, MOE_BLOCK), g_pad.reshape(n_blk, MOE_BLOCK), blk_e))
    y = jnp.zeros_like(xt).at[tok_pad].add(yb.reshape(P, D))
    return y.reshape(B, S, D)


def setup_inputs(seed: int = 0) -> dict:
    key = jax.random.key(seed)
    ks = iter(jax.random.split(key, 64))
    D = D_MODEL

    def w(shape, fan_in, s=1.0):
        return s * fan_in ** -0.5 * jax.random.normal(next(ks), shape, jnp.float32)

    def gain(shape):
        return 1.0 + 0.1 * jax.random.normal(next(ks), shape, jnp.float32)

    def bias(shape):
        return 0.02 * jax.random.normal(next(ks), shape, jnp.float32)

    return {
        "x": jax.random.normal(next(ks), (BATCH, SEQ, D), jnp.float32),
        "c": jax.random.normal(next(ks), (BATCH, D), jnp.float32),
        "ada_even_w": w((N_EVEN, D, 6 * D), D, 0.5),
        "ada_even_b": bias((N_EVEN, 6 * D)),
        "norm_even_mix": gain((N_EVEN, D)),
        "norm_even_ffn": gain((N_EVEN, D)),
        "even_w_in": w((N_EVEN, D, EVEN_IN), D),
        "mla_q_norm": gain((N_EVEN, MLA_Q_RANK)),
        "mla_w_uq": w((N_EVEN, MLA_Q_RANK, MLA_HEADS * (MLA_NOPE + MLA_ROPE)), MLA_Q_RANK),
        "mla_kv_norm": gain((N_EVEN, MLA_KV_RANK)),
        "mla_w_ukv": w((N_EVEN, MLA_KV_RANK, MLA_HEADS * (MLA_NOPE + MLA_V)), MLA_KV_RANK),
        "mla_q_gain": gain((N_EVEN, MLA_NOPE + MLA_ROPE)),
        "mla_k_gain": gain((N_EVEN, MLA_NOPE + MLA_ROPE)),
        "gqa_q_gain": gain((N_EVEN, HEAD_DIM)),
        "gqa_k_gain": gain((N_EVEN, HEAD_DIM)),
        "even_w_out": w((N_EVEN, EVEN_MIX, D), EVEN_MIX),
        "ffn_w_gate": w((N_EVEN, D, D_FF), D),
        "ffn_w_up": w((N_EVEN, D, D_FF), D),
        "ffn_w_down": w((N_EVEN, D_FF, D), D_FF),
        "ada_odd_w": w((N_ODD, D, 6 * D), D, 0.5),
        "ada_odd_b": bias((N_ODD, 6 * D)),
        "norm_odd_mix": gain((N_ODD, D)),
        "norm_odd_ffn": gain((N_ODD, D)),
        "dil_w_qkv": w((N_ODD, D, 3 * DIL_HEADS * HEAD_DIM), D),
        "dil_q_gain": gain((N_ODD, HEAD_DIM)),
        "dil_k_gain": gain((N_ODD, HEAD_DIM)),
        "dil_w_out": w((N_ODD, DIL_HEADS * HEAD_DIM, D), DIL_HEADS * HEAD_DIM),
        "moe_router": w((N_ODD, D, N_EXPERTS), D),
        "moe_w_gate": w((N_ODD, N_EXPERTS, D, D_FF_EXPERT), D),
        "moe_w_up": w((N_ODD, N_EXPERTS, D, D_FF_EXPERT), D),
        "moe_w_down": w((N_ODD, N_EXPERTS, D_FF_EXPERT, D), D_FF_EXPERT),
    }


def reference(x, c, ada_even_w, ada_even_b, norm_even_mix, norm_even_ffn, even_w_in,
              mla_q_norm, mla_w_uq, mla_kv_norm, mla_w_ukv, mla_q_gain, mla_k_gain,
              gqa_q_gain, gqa_k_gain, even_w_out, ffn_w_gate, ffn_w_up, ffn_w_down,
              ada_odd_w, ada_odd_b, norm_odd_mix, norm_odd_ffn, dil_w_qkv, dil_q_gain,
              dil_k_gain, dil_w_out, moe_router, moe_w_gate, moe_w_up, moe_w_down):
    B, S, D = x.shape
    sc = jax.nn.silu(c)
    for layer in range(DEPTH):
        i = layer // 2
        if layer % 2 == 0:
            mod = (sc @ ada_even_w[i] + ada_even_b[i]).reshape(B, 6, D)
            h = modulate(x, norm_even_mix[i], mod[:, 0], mod[:, 1])
            x = x + mod[:, 2][:, None, :] * mixer_even(
                h, even_w_in[i], mla_q_norm[i], mla_w_uq[i], mla_kv_norm[i], mla_w_ukv[i],
                mla_q_gain[i], mla_k_gain[i], gqa_q_gain[i], gqa_k_gain[i], even_w_out[i])
            h = modulate(x, norm_even_ffn[i], mod[:, 3], mod[:, 4])
            x = x + mod[:, 5][:, None, :] * swiglu(h, ffn_w_gate[i], ffn_w_up[i], ffn_w_down[i])
        else:
            mod = (sc @ ada_odd_w[i] + ada_odd_b[i]).reshape(B, 6, D)
            h = modulate(x, norm_odd_mix[i], mod[:, 0], mod[:, 1])
            x = x + mod[:, 2][:, None, :] * mixer_odd(
                h, dil_w_qkv[i], dil_q_gain[i], dil_k_gain[i], dil_w_out[i])
            h = modulate(x, norm_odd_ffn[i], mod[:, 3], mod[:, 4])
            x = x + mod[:, 5][:, None, :] * moe_swiglu(
                h, moe_router[i], moe_w_gate[i], moe_w_up[i], moe_w_down[i])
    return x
```

```python
import functools
import math

import jax
import jax.numpy as jnp
from jax import lax
from jax.experimental import pallas as pl
from jax.experimental.pallas import tpu as pltpu

_BF = jnp.bfloat16
_F32 = jnp.float32

GRID_W = 64
HEAD_DIM = 64
ROPE_THETA = 10000.0
EPS = 1e-6
MLA_HEADS = 8
MLA_Q_RANK = 256
MLA_KV_RANK = 128
MLA_NOPE = 64
MLA_ROPE = 32
MLA_V = 64
GQA_HEADS = 8
GQA_KV_HEADS = 2
DIL_PATTERNS = ((128, 1), (512, 4), (2048, 16))
N_EXPERTS = 8
TOP_K = 2
NEG = -1e30
LOG2E = math.log2(math.e)

LANES = 128
VMEM_LIMIT = 56 * 1024 * 1024


def _cparams(sem, vmem=VMEM_LIMIT):
    return pltpu.CompilerParams(dimension_semantics=sem, vmem_limit_bytes=vmem)


def _silu(x):
    return x / (1.0 + jnp.exp(-x))


def _modulate(x, g, shift, scale):
    ms = jnp.mean(x * x, axis=-1, keepdims=True)
    return x * lax.rsqrt(ms + EPS) * g * (1.0 + scale) + shift


def _norm(v, n, g):
    return v * lax.rsqrt(jnp.sum(v * v, axis=-1, keepdims=True) * (1.0 / n) + EPS) * g


def _rope(x, c, sa, sb, k):
    return x * c + pltpu.roll(x, LANES - k, 1) * sa + pltpu.roll(x, k, 1) * sb


def _mod_kernel(c_ref, w_ref, b_ref, o_ref):
    sc = _silu(c_ref[...])
    o_ref[...] = jnp.dot(sc.astype(_BF), w_ref[...].astype(_BF),
                         preferred_element_type=_F32) + b_ref[...]


def _ada_mod(c, w, b):
    B, D = c.shape
    N = w.shape[1]
    tn = min(N, 1536)
    return pl.pallas_call(
        _mod_kernel,
        grid=(N // tn,),
        in_specs=[pl.BlockSpec((B, D), lambda j: (0, 0)),
                  pl.BlockSpec((D, tn), lambda j: (0, j)),
                  pl.BlockSpec((1, tn), lambda j: (0, j))],
        out_specs=pl.BlockSpec((B, tn), lambda j: (0, j)),
        out_shape=jax.ShapeDtypeStruct((B, N), _F32),
        compiler_params=_cparams(("arbitrary",)),
        name="ada_mod",
    )(c, w, b.reshape(1, N))


def _pre_even_kernel(x_ref, mod_ref, g_ref, w_in_ref, qn_ref, w_uq_ref, kvn_ref, w_ukv_ref,
                     gains_ref, rope_ref, q_ref, k_ref, kv_ref, vb_ref):
    tm = x_ref.shape[0]
    h = _modulate(x_ref[...], g_ref[...], mod_ref[0:1, :], mod_ref[1:2, :]).astype(_BF)
    y = jnp.dot(h, w_in_ref[...], preferred_element_type=_F32)
    cqn = _norm(y[:, 0:256], MLA_Q_RANK, qn_ref[...]).astype(_BF)
    qa = jnp.dot(cqn, w_uq_ref[...], preferred_element_type=_F32)
    ckvn = _norm(y[:, 256:384], MLA_KV_RANK, kvn_ref[...]).astype(_BF)
    kv = jnp.dot(ckvn, w_ukv_ref[...], preferred_element_type=_F32)
    kv_ref[...] = kv.astype(_BF)
    kpe = y[:, 384:512]
    lo = lax.broadcasted_iota(jnp.int32, (tm, LANES), 1) < 64
    ca, saa, sba = rope_ref[:, 0:128], rope_ref[:, 128:256], rope_ref[:, 256:384]
    cb, sab, sbb = rope_ref[:, 384:512], rope_ref[:, 512:640], rope_ref[:, 640:768]
    gqa, gka = gains_ref[0:1, :], gains_ref[1:2, :]
    gqb, gkb = gains_ref[2:3, :], gains_ref[3:4, :]
    na = MLA_NOPE + MLA_ROPE
    for hh in range(MLA_HEADS):
        sl = slice(hh * LANES, (hh + 1) * LANES)
        q_ref[:, sl] = _rope(_norm(qa[:, sl], na, gqa), ca, saa, sba, 16).astype(_BF)
        kh = jnp.where(lo, kv[:, sl], kpe)
        k_ref[:, sl] = _rope(_norm(kh, na, gka), ca, saa, sba, 16).astype(_BF)
    for hh in range(GQA_HEADS):
        src = slice(512 + hh * LANES, 512 + (hh + 1) * LANES)
        dst = slice((MLA_HEADS + hh) * LANES, (MLA_HEADS + hh + 1) * LANES)
        q_ref[:, dst] = _rope(_norm(y[:, src], HEAD_DIM, gqb), cb, sab, sbb, 16).astype(_BF)
    for g in range(GQA_KV_HEADS):
        src = slice(1536 + g * LANES, 1536 + (g + 1) * LANES)
        dst = slice((MLA_HEADS + g) * LANES, (MLA_HEADS + g + 1) * LANES)
        k_ref[:, dst] = _rope(_norm(y[:, src], HEAD_DIM, gkb), cb, sab, sbb, 16).astype(_BF)
    vb_ref[...] = y[:, 1792:1920].astype(_BF)


def _pre_even(x, mod, g, w_in, qn, w_uq, kvn, w_ukv, gains, rope, tm):
    B, S, D = x.shape
    nin = w_in.shape[1]
    nq = (MLA_HEADS + GQA_HEADS) * LANES
    nk = (MLA_HEADS + GQA_KV_HEADS) * LANES
    nkv = MLA_HEADS * LANES
    full = lambda shp: pl.BlockSpec(shp, lambda b, i: (0,) * len(shp))
    tok = lambda w: pl.BlockSpec((None, tm, w), lambda b, i: (b, i, 0))
    return pl.pallas_call(
        _pre_even_kernel,
        grid=(B, S // tm),
        in_specs=[tok(D),
                  pl.BlockSpec((None, 6, D), lambda b, i: (b, 0, 0)),
                  full((1, D)), full((D, nin)), full((1, MLA_Q_RANK)), full((MLA_Q_RANK, nkv)),
                  full((1, MLA_KV_RANK)), full((MLA_KV_RANK, nkv)), full((4, LANES)),
                  pl.BlockSpec((tm, 6 * LANES), lambda b, i: (i, 0))],
        out_specs=[tok(nq), tok(nk), tok(nkv), tok(LANES)],
        out_shape=[jax.ShapeDtypeStruct((B, S, nq), _BF), jax.ShapeDtypeStruct((B, S, nk), _BF),
                   jax.ShapeDtypeStruct((B, S, nkv), _BF), jax.ShapeDtypeStruct((B, S, LANES), _BF)],
        compiler_params=_cparams(("parallel", "parallel")),
        name="pre_even",
    )(x, mod, g, w_in, qn, w_uq, kvn, w_ukv, gains, rope)


def _attn_kernel(q_ref, k_ref, v_ref, o_ref, *, tk, shared_kv):
    tq = q_ref.shape[0]
    S = k_ref.shape[0]
    accs = []
    for hh in range(2):
        q = q_ref[:, hh * LANES:(hh + 1) * LANES]
        kc = 0 if shared_kv else hh * LANES

        def body(j, carry, q=q, kc=kc):
            m, l, acc = carry
            r0 = pl.multiple_of(j * tk, tk)
            k = k_ref[pl.ds(r0, tk), kc:kc + LANES]
            v = v_ref[pl.ds(r0, tk), kc:kc + LANES]
            s = lax.dot_general(q, k, (((1,), (1,)), ((), ())), preferred_element_type=_F32)
            m_new = jnp.maximum(m, jnp.max(s, axis=-1, keepdims=True))
            alpha = jnp.exp2(m - m_new)
            p = jnp.exp2(s - m_new)
            l = alpha * l + jnp.sum(p, axis=-1, keepdims=True)
            acc = alpha * acc + jnp.dot(p.astype(_BF), v, preferred_element_type=_F32)
            return m_new, l, acc

        init = (jnp.full((tq, 1), NEG, _F32), jnp.zeros((tq, 1), _F32), jnp.zeros((tq, LANES), _F32))
        m, l, acc = lax.fori_loop(0, S // tk, body, init)
        accs.append(acc / l)
    if shared_kv:
        hi = (pl.program_id(1) // 2) == 1
    else:
        hi = True
    a0 = jnp.where(hi, pltpu.roll(accs[0], 64, 1), accs[0])
    a1 = jnp.where(hi, accs[1], pltpu.roll(accs[1], 64, 1))
    lane = lax.broadcasted_iota(jnp.int32, (tq, LANES), 1)
    o_ref[...] = jnp.where(lane < 64, a0, a1).astype(o_ref.dtype)


def _attention(q, k, v, *, q_off, k_off, shared_kv, tq, tk):
    B, S, _ = q.shape
    npairs = 4
    if shared_kv:
        kspec = pl.BlockSpec((None, S, LANES), lambda b, p, i: (b, 0, k_off + p // 2))
        vspec = pl.BlockSpec((None, S, LANES), lambda b, p, i: (b, 0, 0))
    else:
        kspec = pl.BlockSpec((None, S, 2 * LANES), lambda b, p, i: (b, 0, k_off // 2 + p))
        vspec = pl.BlockSpec((None, S, 2 * LANES), lambda b, p, i: (b, 0, p))
    return pl.pallas_call(
        functools.partial(_attn_kernel, tk=tk, shared_kv=shared_kv),
        grid=(B, npairs, S // tq),
        in_specs=[pl.BlockSpec((None, tq, 2 * LANES), lambda b, p, i: (b, i, q_off // 2 + p)),
                  kspec, vspec],
        out_specs=pl.BlockSpec((None, tq, LANES), lambda b, p, i: (b, i, p)),
        out_shape=jax.ShapeDtypeStruct((B, S, npairs * LANES), _BF),
        compiler_params=_cparams(("parallel", "parallel", "parallel")),
        name="attn_gqa" if shared_kv else "attn_mla",
    )(q, k, v)


def _oproj_kernel(*refs, n_in, route):
    x_ref, mod_ref, g_ref, w_ref = refs[0], refs[1], refs[2], refs[3]
    o_refs = refs[4:4 + n_in]
    pos = 4 + n_in
    if route:
        r_ref = refs[pos]
        pos += 1
    x_out, h_out = refs[pos], refs[pos + 1]
    y = None
    off = 0
    for o_ref in o_refs:
        w = o_ref.shape[1]
        t = jnp.dot(o_ref[...], w_ref[off:off + w, :], preferred_element_type=_F32)
        y = t if y is None else y + t
        off += w
    x1 = x_ref[...] + mod_ref[2:3, :] * y
    x_out[...] = x1
    h = _modulate(x1, g_ref[...], mod_ref[3:4, :], mod_ref[4:5, :])
    h_out[...] = h.astype(h_out.dtype)
    if route:
        route_out = refs[pos + 2]
        hh = h.astype(_BF)
        hl = (h - hh.astype(_F32)).astype(_BF)
        rh, rl = r_ref[0], r_ref[1]
        logits = (jnp.dot(hh, rh, preferred_element_type=_F32)
                  + jnp.dot(hh, rl, preferred_element_type=_F32)
                  + jnp.dot(hl, rh, preferred_element_type=_F32))
        tm = logits.shape[0]
        lane = lax.broadcasted_iota(jnp.int32, (tm, LANES), 1)
        lg = jnp.where(lane < N_EXPERTS, logits, NEG)
        m1 = jnp.max(lg, axis=-1, keepdims=True)
        lanef = lane.astype(_F32)
        i1 = jnp.min(jnp.where(lg == m1, lanef, float(LANES)), axis=-1, keepdims=True)
        lg2 = jnp.where(lanef == i1, NEG, lg)
        m2 = jnp.max(lg2, axis=-1, keepdims=True)
        i2 = jnp.min(jnp.where(lg2 == m2, lanef, float(LANES)), axis=-1, keepdims=True)
        e = jnp.exp(m2 - m1)
        g1 = 1.0 / (1.0 + e)
        g2 = e / (1.0 + e)
        route_out[...] = jnp.where(lane == 0, i1,
                                   jnp.where(lane == 1, i2,
                                             jnp.where(lane == 2, g1, jnp.where(lane == 3, g2, 0.0))))


def _oproj(x, mod, g, w, o_list, tm, h_dtype, router=None):
    B, S, D = x.shape
    n_in = len(o_list)
    route = router is not None
    tok = lambda wd: pl.BlockSpec((None, tm, wd), lambda b, i: (b, i, 0))
    in_specs = [tok(D), pl.BlockSpec((None, 6, D), lambda b, i: (b, 0, 0)),
                pl.BlockSpec((1, D), lambda b, i: (0, 0)),
                pl.BlockSpec(w.shape, lambda b, i: (0, 0))]
    in_specs += [tok(o.shape[2]) for o in o_list]
    args = [x, mod, g, w] + list(o_list)
    out_specs = [tok(D), tok(D)]
    out_shape = [jax.ShapeDtypeStruct((B, S, D), _F32), jax.ShapeDtypeStruct((B, S, D), h_dtype)]
    if route:
        in_specs.append(pl.BlockSpec(router.shape, lambda b, i: (0, 0, 0)))
        args.append(router)
        out_specs.append(tok(LANES))
        out_shape.append(jax.ShapeDtypeStruct((B, S, LANES), _F32))
    return pl.pallas_call(
        functools.partial(_oproj_kernel, n_in=n_in, route=route),
        grid=(B, S // tm),
        in_specs=in_specs, out_specs=out_specs, out_shape=out_shape,
        compiler_params=_cparams(("parallel", "parallel")),
        name="oproj_route" if route else "oproj",
    )(*args)


def _ffn_kernel(x_ref, h_ref, mod_ref, modn_ref, gn_ref, wg_ref, wu_ref, wd_ref, x_out, h_out, acc):
    j = pl.program_id(1)
    nf = pl.num_programs(1)
    h = h_ref[...]
    a = _silu(jnp.dot(h, wg_ref[...], preferred_element_type=_F32)) * jnp.dot(
        h, wu_ref[...], preferred_element_type=_F32)
    t = jnp.dot(a.astype(_BF), wd_ref[...], preferred_element_type=_F32)

    @pl.when(j == 0)
    def _():
        acc[...] = t

    @pl.when(jnp.logical_and(j > 0, j < nf - 1))
    def _():
        acc[...] += t

    @pl.when(j == nf - 1)
    def _():
        x2 = x_ref[...] + mod_ref[5:6, :] * (acc[...] + t)
        x_out[...] = x2
        h_out[...] = _modulate(x2, gn_ref[...], modn_ref[0:1, :], modn_ref[1:2, :]).astype(h_out.dtype)


def _ffn(x, h, mod, modn, gn, wg, wu, wd, tm, tf):
    B, S, D = x.shape
    T = B * S
    F = wg.shape[1]
    per_b = S // tm
    x2 = x.reshape(T, D)
    h2 = h.reshape(T, D)
    tok = pl.BlockSpec((tm, D), lambda i, j: (i, 0))
    modspec = pl.BlockSpec((None, 6, D), lambda i, j: (i // per_b, 0, 0))
    xo, ho = pl.pallas_call(
        _ffn_kernel,
        grid=(T // tm, F // tf),
        in_specs=[tok, tok, modspec, modspec, pl.BlockSpec((1, D), lambda i, j: (0, 0)),
                  pl.BlockSpec((D, tf), lambda i, j: (0, j)),
                  pl.BlockSpec((D, tf), lambda i, j: (0, j)),
                  pl.BlockSpec((tf, D), lambda i, j: (j, 0))],
        out_specs=[tok, tok],
        out_shape=[jax.ShapeDtypeStruct((T, D), _F32), jax.ShapeDtypeStruct((T, D), _BF)],
        scratch_shapes=[pltpu.VMEM((tm, D), _F32)],
        compiler_params=_cparams(("parallel", "arbitrary")),
        name="ffn_dense",
    )(x2, h2, mod, modn, gn, wg, wu, wd)
    return xo.reshape(B, S, D), ho.reshape(B, S, D)


def _qkv_kernel(h_ref, w_ref, gains_ref, rope_ref, q_ref, k_ref, v_ref):
    tm, D = h_ref.shape
    y = jnp.dot(h_ref[...], w_ref[...], preferred_element_type=_F32)
    lane = lax.broadcasted_iota(jnp.int32, (tm, LANES), 1)
    lo = lane < 64
    c, sa, sb = rope_ref[:, 0:128], rope_ref[:, 128:256], rope_ref[:, 256:384]

    def head_pair(v, g):
        sq = v * v
        s_lo = jnp.sum(jnp.where(lo, sq, 0.0), axis=-1, keepdims=True)
        s_hi = jnp.sum(jnp.where(lo, 0.0, sq), axis=-1, keepdims=True)
        r = lax.rsqrt(jnp.where(lo, s_lo, s_hi) * (1.0 / HEAD_DIM) + EPS)
        return _rope(v * r * g, c, sa, sb, 32)

    for p in range(D // LANES):
        sl = slice(p * LANES, (p + 1) * LANES)
        q_ref[:, sl] = head_pair(y[:, sl], gains_ref[0:1, :]).astype(_BF)
        k_ref[:, sl] = head_pair(y[:, D + p * LANES:D + (p + 1) * LANES], gains_ref[1:2, :]).astype(_BF)
    v_ref[...] = y[:, 2 * D:3 * D].astype(_BF)


def _qkv(h, w, gains, rope, tm):
    B, S, D = h.shape
    tok = pl.BlockSpec((None, tm, D), lambda b, i: (b, i, 0))
    sds = jax.ShapeDtypeStruct((B, S, D), _BF)
    return pl.pallas_call(
        _qkv_kernel,
        grid=(B, S // tm),
        in_specs=[tok, pl.BlockSpec(w.shape, lambda b, i: (0, 0)),
                  pl.BlockSpec((2, LANES), lambda b, i: (0, 0)),
                  pl.BlockSpec((tm, 3 * LANES), lambda b, i: (i, 0))],
        out_specs=[tok, tok, tok],
        out_shape=[sds, sds, sds],
        compiler_params=_cparams(("parallel", "parallel")),
        name="qkv_dil",
    )(h, w, gains, rope)


def _dil_kernel(*refs, tl, half, has_prev, last):
    q_ref, k_ref, v_ref = refs[0], refs[1], refs[2]
    pos = 3
    if has_prev:
        op_ref, sp_ref = refs[pos], refs[pos + 1]
        pos += 2
    o_ref = refs[pos]
    s_ref = None if last else refs[pos + 1]
    L = k_ref.shape[0]
    D = q_ref.shape[1]
    W = min(tl + 2 * half, L)
    li = pl.program_id(2)
    start = jnp.clip(li * tl - half, 0, L - W)
    start = pl.multiple_of(start, 64)
    qpos = li * tl + lax.broadcasted_iota(jnp.int32, (tl, W), 0)
    kpos = start + lax.broadcasted_iota(jnp.int32, (tl, W), 1)
    valid = jnp.abs(kpos - qpos) <= half
    lane = lax.broadcasted_iota(jnp.int32, (tl, LANES), 1)
    lo = lane < 64
    stats = jnp.zeros((tl, LANES), _F32)
    prev = sp_ref[...] if has_prev else None
    for p in range(D // LANES):
        sl = slice(p * LANES, (p + 1) * LANES)
        qp = q_ref[:, sl]
        kw = k_ref[pl.ds(start, W), sl]
        vw = v_ref[pl.ds(start, W), sl]
        outs = []
        for hh in range(2):
            qm = jnp.where(lo if hh == 0 else jnp.logical_not(lo), qp, jnp.zeros_like(qp))
            s = lax.dot_general(qm, kw, (((1,), (1,)), ((), ())), preferred_element_type=_F32)
            s = jnp.where(valid, s, NEG)
            m = jnp.max(s, axis=-1, keepdims=True)
            pr = jnp.exp2(s - m)
            den = jnp.sum(pr, axis=-1, keepdims=True)
            o = jnp.dot(pr.astype(_BF), vw, preferred_element_type=_F32) / den
            lse = m + jnp.log2(den)
            hidx = 2 * p + hh
            if has_prev:
                lp = prev[:, hidx:hidx + 1]
                mx = jnp.maximum(lp, lse)
                wp = jnp.exp2(lp - mx)
                wc = jnp.exp2(lse - mx)
                tot = wp + wc
                o = (op_ref[:, sl] * wp + o * wc) / tot
                lse = mx + jnp.log2(tot)
            outs.append(o)
            stats = jnp.where(lane == hidx, lse, stats)
        o_ref[:, sl] = jnp.where(lo, outs[0], outs[1]).astype(o_ref.dtype)
    if not last:
        s_ref[...] = stats


def _dil_branch(q, k, v, prev, *, window, dil, tl, last):
    B, S, D = q.shape
    L = S // dil
    half = window // (2 * dil)
    tl = min(tl, L)
    view = lambda a: a.reshape(B, L, dil * a.shape[2])
    qspec = pl.BlockSpec((None, tl, D), lambda b, r, i: (b, i, r))
    kspec = pl.BlockSpec((None, L, D), lambda b, r, i: (b, 0, r))
    sspec = pl.BlockSpec((None, tl, LANES), lambda b, r, i: (b, i, r))
    in_specs = [qspec, kspec, kspec]
    args = [view(q), view(k), view(v)]
    if prev is not None:
        in_specs += [qspec, sspec]
        args += [view(prev[0]), view(prev[1])]
    if last:
        out_specs = [qspec]
        out_shape = [jax.ShapeDtypeStruct((B, L, dil * D), _BF)]
    else:
        out_specs = [qspec, sspec]
        out_shape = [jax.ShapeDtypeStruct((B, L, dil * D), _F32),
                     jax.ShapeDtypeStruct((B, L, dil * LANES), _F32)]
    res = pl.pallas_call(
        functools.partial(_dil_kernel, tl=tl, half=half, has_prev=prev is not None, last=last),
        grid=(B, dil, L // tl),
        in_specs=in_specs, out_specs=out_specs, out_shape=out_shape,
        compiler_params=_cparams(("parallel", "parallel", "parallel")),
        name="dil_attn_%d" % dil,
    )(*args)
    if last:
        return res[0].reshape(B, S, D)
    return res[0].reshape(B, S, D), res[1].reshape(B, S, LANES)


def _moe_kernel(te_ref, tv_ref, src_hbm, dst_hbm, h_hbm, wg_ref, wu_ref, wd_ref, out_hbm,
                xbuf, xbf, acc, ybuf, src_s, dst_s, gsem, ssem, isem):
    i = pl.program_id(0)
    j = pl.program_id(1)
    nt = pl.num_programs(0)
    nf = pl.num_programs(1)
    tm = xbf.shape[0]
    valid = tv_ref[i] == 1
    slot = i % 2
    nxt_valid = jnp.logical_and(i + 1 < nt, tv_ref[jnp.minimum(i + 1, nt - 1)] == 1)

    def src_copy(tile):
        return pltpu.make_async_copy(src_hbm.at[tile], src_s, isem.at[0])

    def dst_copy(tile):
        return pltpu.make_async_copy(dst_hbm.at[tile], dst_s, isem.at[1])

    def gather_row(r, slot_):
        return pltpu.make_async_copy(h_hbm.at[pl.ds(src_s[r], 1)], xbuf.at[slot_, pl.ds(r, 1)],
                                     gsem.at[slot_])

    def scatter_row(r):
        return pltpu.make_async_copy(ybuf.at[pl.ds(r, 1)], out_hbm.at[pl.ds(dst_s[r], 1)], ssem.at[0])

    def for_rows(fn):
        def body(r, c):
            fn(r)
            return c
        lax.fori_loop(0, tm, body, 0)

    @pl.when(jnp.logical_and(jnp.logical_and(i == 0, j == 0), valid))
    def _():
        src_copy(0).start()
        src_copy(0).wait()
        for_rows(lambda r: gather_row(r, 0).start())

    def gather_wait(slot_):
        return pltpu.make_async_copy(h_hbm.at[pl.ds(0, 1)], xbuf.at[slot_, pl.ds(0, 1)], gsem.at[slot_])

    def scatter_wait():
        return pltpu.make_async_copy(ybuf.at[pl.ds(0, 1)], out_hbm.at[pl.ds(0, 1)], ssem.at[0])

    @pl.when(jnp.logical_and(j == 0, valid))
    def _():
        for_rows(lambda r: gather_wait(slot).wait())
        xbf[...] = xbuf[slot].astype(_BF)
        dst_copy(i).start()

    @pl.when(jnp.logical_and(j == 0, nxt_valid))
    def _():
        src_copy(i + 1).start()

    @pl.when(jnp.logical_and(j == 1, valid))
    def _():
        dst_copy(i).wait()

    @pl.when(jnp.logical_and(j == 1, nxt_valid))
    def _():
        src_copy(i + 1).wait()
        for_rows(lambda r: gather_row(r, 1 - slot).start())

    @pl.when(valid)
    def _():
        x = xbf[...]
        a = _silu(jnp.dot(x, wg_ref[...], preferred_element_type=_F32)) * jnp.dot(
            x, wu_ref[...], preferred_element_type=_F32)
        t = jnp.dot(a.astype(_BF), wd_ref[...], preferred_element_type=_F32)

        @pl.when(j == 0)
        def _():
            acc[...] = t

        @pl.when(j > 0)
        def _():
            acc[...] += t

    prev_valid = jnp.logical_and(i >= 1, tv_ref[jnp.maximum(i - 1, 0)] == 1)

    @pl.when(jnp.logical_and(j == nf - 1, prev_valid))
    def _():
        for_rows(lambda r: scatter_wait().wait())

    @pl.when(jnp.logical_and(j == nf - 1, valid))
    def _():
        ybuf[...] = acc[...]
        for_rows(lambda r: scatter_row(r).start())

    @pl.when(jnp.logical_and(jnp.logical_and(j == nf - 1, i == nt - 1), valid))
    def _():
        for_rows(lambda r: scatter_wait().wait())


def _moe(h, tile_e, tile_v, src, dst, wg, wu, wd, n_out_rows, tm, tf):
    T, D = h.shape
    nt = tile_e.shape[0]
    F = wg.shape[2]
    nf = F // tf
    assert nf >= 2

    def wcol(i, j, te, tv):
        return (te[i], 0, jnp.where(tv[i] == 1, j, nf - 1))

    def wrow(i, j, te, tv):
        return (te[i], jnp.where(tv[i] == 1, j, nf - 1), 0)

    grid_spec = pltpu.PrefetchScalarGridSpec(
        num_scalar_prefetch=2,
        grid=(nt, nf),
        in_specs=[pl.BlockSpec(memory_space=pl.ANY), pl.BlockSpec(memory_space=pl.ANY),
                  pl.BlockSpec(memory_space=pl.ANY),
                  pl.BlockSpec((None, D, tf), wcol), pl.BlockSpec((None, D, tf), wcol),
                  pl.BlockSpec((None, tf, D), wrow)],
        out_specs=pl.BlockSpec(memory_space=pl.ANY),
        scratch_shapes=[pltpu.VMEM((2, tm, D), _F32), pltpu.VMEM((tm, D), _BF),
                        pltpu.VMEM((tm, D), _F32), pltpu.VMEM((tm, D), _F32),
                        pltpu.SMEM((tm,), jnp.int32), pltpu.SMEM((tm,), jnp.int32),
                        pltpu.SemaphoreType.DMA((2,)), pltpu.SemaphoreType.DMA((1,)),
                        pltpu.SemaphoreType.DMA((2,))],
    )
    return pl.pallas_call(
        _moe_kernel,
        grid_spec=grid_spec,
        out_shape=jax.ShapeDtypeStruct((n_out_rows, D), _F32),
        compiler_params=_cparams(("arbitrary", "arbitrary")),
        name="moe_experts",
    )(tile_e, tile_v, src, dst, h, wg, wu, wd)


def _route_plan(route, T, tm):
    A = T * TOP_K
    e_flat = route[:, :TOP_K].astype(jnp.int32).reshape(A)
    order = jnp.argsort(e_flat, stable=True).astype(jnp.int32)
    counts = jnp.sum((e_flat[:, None] == jnp.arange(N_EXPERTS, dtype=jnp.int32)[None, :]).astype(jnp.int32), axis=0)
    starts = jnp.cumsum(counts) - counts
    pcounts = (counts + tm - 1) // tm * tm
    pends = jnp.cumsum(pcounts)
    pstarts = pends - pcounts
    nt = A // tm + N_EXPERTS
    tile0 = jnp.arange(nt, dtype=jnp.int32) * tm
    tile_v = (tile0 < pends[-1]).astype(jnp.int32)
    last_valid = jnp.maximum(pends[-1] // tm - 1, 0)
    tile_e_raw = jnp.minimum(jnp.searchsorted(pends, tile0, side="right"), N_EXPERTS - 1).astype(jnp.int32)
    tile_e = jnp.where(tile_v == 1, tile_e_raw, tile_e_raw[last_valid])
    r = jnp.arange(nt * tm, dtype=jnp.int32)
    e_r = jnp.repeat(tile_e, tm)
    within = r - pstarts[e_r]
    ok = jnp.logical_and(within < counts[e_r], jnp.repeat(tile_v, tm) == 1)
    a = order[jnp.clip(starts[e_r] + within, 0, A - 1)]
    src = jnp.where(ok, a // TOP_K, 0).astype(jnp.int32)
    dump = A + e_r * tm + jnp.clip(within - counts[e_r], 0, tm - 1)
    dst = jnp.where(ok, a, dump).astype(jnp.int32)
    return tile_e, tile_v, src.reshape(nt, tm), dst.reshape(nt, tm), A + N_EXPERTS * tm


def _combine_kernel(x_ref, y_ref, r_ref, mod_ref, o_ref):
    D = x_ref.shape[1]
    g1 = r_ref[:, 2:3]
    g2 = r_ref[:, 3:4]
    y = y_ref[:, 0:D] * g1 + y_ref[:, D:2 * D] * g2
    o_ref[...] = x_ref[...] + mod_ref[5:6, :] * y


def _combine(x, y2, route, mod, tm):
    B, S, D = x.shape
    T = B * S
    per_b = S // tm
    out = pl.pallas_call(
        _combine_kernel,
        grid=(T // tm,),
        in_specs=[pl.BlockSpec((tm, D), lambda i: (i, 0)),
                  pl.BlockSpec((tm, 2 * D), lambda i: (i, 0)),
                  pl.BlockSpec((tm, LANES), lambda i: (i, 0)),
                  pl.BlockSpec((None, 6, D), lambda i: (i // per_b, 0, 0))],
        out_specs=pl.BlockSpec((tm, D), lambda i: (i, 0)),
        out_shape=jax.ShapeDtypeStruct((T, D), _F32),
        compiler_params=_cparams(("parallel",)),
        name="moe_combine",
    )(x.reshape(T, D), y2, route.reshape(T, LANES), mod)
    return out.reshape(B, S, D)


def _pad_cols(a, w):
    return jnp.pad(a, ((0, 0), (0, w - a.shape[1])))


def _rope_tables_even(S):
    pos = jnp.arange(S, dtype=jnp.int32)
    inv = ROPE_THETA ** (-jnp.arange(0, 32, 2, dtype=_F32) / 32)
    def cs(p):
        ang = p.astype(_F32)[:, None] * inv[None, :]
        return jnp.cos(ang), jnp.sin(ang)
    one = lambda w: jnp.ones((S, w), _F32)
    zero = lambda w: jnp.zeros((S, w), _F32)
    c, s = cs(pos)
    ca = jnp.concatenate([one(64), c, c, one(32)], 1)
    saa = jnp.concatenate([zero(64), -s, zero(16), zero(32)], 1)
    sba = jnp.concatenate([zero(64), zero(16), s, zero(32)], 1)
    cr, sr = cs(pos // GRID_W)
    cc, sc = cs(pos % GRID_W)
    cb = jnp.concatenate([cr, cr, cc, cc, one(64)], 1)
    sab = jnp.concatenate([-sr, zero(16), -sc, zero(16), zero(64)], 1)
    sbb = jnp.concatenate([zero(16), sr, zero(16), sc, zero(64)], 1)
    return jnp.concatenate([ca, saa, sba, cb, sab, sbb], 1)


def _rope_tables_odd(S):
    pos = jnp.arange(S, dtype=_F32)
    inv = ROPE_THETA ** (-jnp.arange(0, HEAD_DIM, 2, dtype=_F32) / HEAD_DIM)
    ang = pos[:, None] * inv[None, :]
    c, s = jnp.cos(ang), jnp.sin(ang)
    z = jnp.zeros_like(s)
    return jnp.concatenate([c, c, c, c, -s, z, -s, z, z, s, z, s], 1)


def _tiles(S):
    return dict(tm_pre=min(S, 512), tq=min(S, 512), tk=min(S, 512), tm_o=min(S, 512),
                tm_ffn=min(S, 1024), tf_ffn=256 if S < 4096 else 256, tm_qkv=min(S, 512),
                tl=128, tm_moe=min(S, 1024), tf_moe=512, tm_c=min(S, 512))


def kernel(x, c, ada_even_w, ada_even_b, norm_even_mix, norm_even_ffn, even_w_in, mla_q_norm, mla_w_uq, mla_kv_norm, mla_w_ukv, mla_q_gain, mla_k_gain, gqa_q_gain, gqa_k_gain, even_w_out, ffn_w_gate, ffn_w_up, ffn_w_down, ada_odd_w, ada_odd_b, norm_odd_mix, norm_odd_ffn, dil_w_qkv, dil_q_gain, dil_k_gain, dil_w_out, moe_router, moe_w_gate, moe_w_up, moe_w_down):
    B, S, D = x.shape
    T = B * S
    cfg = _tiles(S)

    mod_e = _ada_mod(c, ada_even_w[0], ada_even_b[0]).reshape(B, 6, D)
    mod_o = _ada_mod(c, ada_odd_w[0], ada_odd_b[0]).reshape(B, 6, D)

    w = even_w_in[0]
    sp = [MLA_Q_RANK, MLA_Q_RANK + MLA_KV_RANK, MLA_Q_RANK + MLA_KV_RANK + MLA_ROPE]
    sp.append(sp[-1] + GQA_HEADS * HEAD_DIM)
    sp.append(sp[-1] + GQA_KV_HEADS * HEAD_DIM)
    w_cq, w_ckv, w_kpe = w[:, :sp[0]], w[:, sp[0]:sp[1]], w[:, sp[1]:sp[2]]
    w_qb, w_kb, w_vb = w[:, sp[2]:sp[3]], w[:, sp[3]:sp[4]], w[:, sp[4]:]
    zc = lambda n: jnp.zeros((D, n), _F32)
    slot64 = lambda a, n: jnp.pad(a.reshape(D, n, HEAD_DIM), ((0, 0), (0, 0), (0, LANES - HEAD_DIM))).reshape(D, n * LANES)
    w_in = jnp.concatenate([w_cq, w_ckv, zc(64), w_kpe, zc(32), slot64(w_qb, GQA_HEADS),
                            slot64(w_kb, GQA_KV_HEADS), w_vb], axis=1).astype(_BF)
    na = MLA_NOPE + MLA_ROPE
    w_uq = jnp.pad(mla_w_uq[0].reshape(MLA_Q_RANK, MLA_HEADS, na),
                   ((0, 0), (0, 0), (0, LANES - na))).reshape(MLA_Q_RANK, MLA_HEADS * LANES).astype(_BF)
    w_ukv = mla_w_ukv[0].astype(_BF)
    pad128 = lambda v: jnp.pad(v, (0, LANES - v.shape[0]))
    gains_e = jnp.stack([pad128(mla_q_gain[0]) * (na ** -0.5 * LOG2E), pad128(mla_k_gain[0]),
                         pad128(gqa_q_gain[0]) * (HEAD_DIM ** -0.5 * LOG2E), pad128(gqa_k_gain[0])])
    rope_e = _rope_tables_even(S)
    q_all, k_all, kv_a, v_b = _pre_even(
        x, mod_e, norm_even_mix[0].reshape(1, D), w_in, mla_q_norm[0].reshape(1, -1), w_uq,
        mla_kv_norm[0].reshape(1, -1), w_ukv, gains_e, rope_e, cfg["tm_pre"])
    o_a = _attention(q_all, k_all, kv_a, q_off=0, k_off=0, shared_kv=False, tq=cfg["tq"], tk=cfg["tk"])
    o_b = _attention(q_all, k_all, v_b, q_off=MLA_HEADS, k_off=MLA_HEADS, shared_kv=True,
                     tq=cfg["tq"], tk=cfg["tk"])
    x1, h1 = _oproj(x, mod_e, norm_even_ffn[0].reshape(1, D), even_w_out[0].astype(_BF), [o_a, o_b],
                    cfg["tm_o"], _BF)
    x2, h2 = _ffn(x1, h1, mod_e, mod_o, norm_odd_mix[0].reshape(1, D), ffn_w_gate[0].astype(_BF),
                  ffn_w_up[0].astype(_BF), ffn_w_down[0].astype(_BF), cfg["tm_ffn"], cfg["tf_ffn"])

    pair = lambda v: jnp.concatenate([v, v])
    gains_o = jnp.stack([pair(dil_q_gain[0]) * (HEAD_DIM ** -0.5 * LOG2E), pair(dil_k_gain[0])])
    qd, kd, vd = _qkv(h2, dil_w_qkv[0].astype(_BF), gains_o, _rope_tables_odd(S), cfg["tm_qkv"])
    prev = None
    for bi, (window, dil) in enumerate(DIL_PATTERNS):
        last = bi == len(DIL_PATTERNS) - 1
        prev = _dil_branch(qd, kd, vd, prev, window=window, dil=dil, tl=cfg["tl"], last=last)
    o_d = prev
    r32 = _pad_cols(moe_router[0], LANES)
    r_hi = r32.astype(_BF)
    r_lo = (r32 - r_hi.astype(_F32)).astype(_BF)
    x3, h3, route = _oproj(x2, mod_o, norm_odd_ffn[0].reshape(1, D), dil_w_out[0].astype(_BF), [o_d],
                           cfg["tm_o"], _F32, router=jnp.stack([r_hi, r_lo]))
    tm = cfg["tm_moe"]
    tile_e, tile_v, src, dst, n_rows = _route_plan(route.reshape(T, LANES), T, tm)
    y2 = _moe(h3.reshape(T, D), tile_e, tile_v, src, dst, moe_w_gate[0].astype(_BF),
              moe_w_up[0].astype(_BF), moe_w_down[0].astype(_BF), n_rows, tm, cfg["tf_moe"])
    y2 = y2.reshape(n_rows // TOP_K, TOP_K * D)
    return _combine(x3, y2, route, mod_o, cfg["tm_c"])
```

```python
import functools
import math

import jax
import jax.numpy as jnp
from jax import lax
from jax.experimental import pallas as pl
from jax.experimental.pallas import tpu as pltpu

_BF = jnp.bfloat16
_F32 = jnp.float32

GRID_W = 64
HEAD_DIM = 64
ROPE_THETA = 10000.0
EPS = 1e-6
MLA_HEADS = 8
MLA_Q_RANK = 256
MLA_KV_RANK = 128
MLA_NOPE = 64
MLA_ROPE = 32
MLA_V = 64
GQA_HEADS = 8
GQA_KV_HEADS = 2
DIL_PATTERNS = ((128, 1), (512, 4), (2048, 16))
N_EXPERTS = 8
TOP_K = 2
NEG = -1e30
LOG2E = math.log2(math.e)

LANES = 128
VMEM_LIMIT = 56 * 1024 * 1024


def _cparams(sem, vmem=VMEM_LIMIT):
    return pltpu.CompilerParams(dimension_semantics=sem, vmem_limit_bytes=vmem)


def _silu(x):
    return x / (1.0 + jnp.exp(-x))


def _modulate(x, g, shift, scale):
    ms = jnp.mean(x * x, axis=-1, keepdims=True)
    return x * lax.rsqrt(ms + EPS) * g * (1.0 + scale) + shift


def _norm(v, n, g):
    return v * lax.rsqrt(jnp.sum(v * v, axis=-1, keepdims=True) * (1.0 / n) + EPS) * g


def _rope(x, c, sa, sb, k):
    return x * c + pltpu.roll(x, LANES - k, 1) * sa + pltpu.roll(x, k, 1) * sb


def _mod_kernel(c_ref, w_ref, b_ref, o_ref):
    sc = _silu(c_ref[...])
    o_ref[...] = jnp.dot(sc.astype(_BF), w_ref[...].astype(_BF),
                         preferred_element_type=_F32) + b_ref[...]


def _ada_mod(c, w, b):
    B, D = c.shape
    N = w.shape[1]
    tn = min(N, 1536)
    return pl.pallas_call(
        _mod_kernel,
        grid=(N // tn,),
        in_specs=[pl.BlockSpec((B, D), lambda j: (0, 0)),
                  pl.BlockSpec((D, tn), lambda j: (0, j)),
                  pl.BlockSpec((1, tn), lambda j: (0, j))],
        out_specs=pl.BlockSpec((B, tn), lambda j: (0, j)),
        out_shape=jax.ShapeDtypeStruct((B, N), _F32),
        compiler_params=_cparams(("arbitrary",)),
        name="ada_mod",
    )(c, w, b.reshape(1, N))


def _pre_even_kernel(x_ref, mod_ref, g_ref, w_in_ref, qn_ref, w_uq_ref, kvn_ref, w_ukv_ref,
                     gains_ref, rope_ref, q_ref, k_ref, vt_ref):
    tm = x_ref.shape[0]
    h = _modulate(x_ref[...], g_ref[...], mod_ref[0:1, :], mod_ref[1:2, :]).astype(_BF)
    y = jnp.dot(h, w_in_ref[...], preferred_element_type=_F32)
    cqn = _norm(y[:, 0:256], MLA_Q_RANK, qn_ref[...]).astype(_BF)
    qa = jnp.dot(cqn, w_uq_ref[...], preferred_element_type=_F32)
    ckvn = _norm(y[:, 256:384], MLA_KV_RANK, kvn_ref[...]).astype(_BF)
    kv = jnp.dot(ckvn, w_ukv_ref[...], preferred_element_type=_F32)
    kpe = y[:, 384:512]
    lo = lax.broadcasted_iota(jnp.int32, (tm, LANES), 1) < 64
    ca, saa, sba = rope_ref[:, 0:128], rope_ref[:, 128:256], rope_ref[:, 256:384]
    cb, sab, sbb = rope_ref[:, 384:512], rope_ref[:, 512:640], rope_ref[:, 640:768]
    gqa, gka = gains_ref[0:1, :], gains_ref[1:2, :]
    gqb, gkb = gains_ref[2:3, :], gains_ref[3:4, :]
    na = MLA_NOPE + MLA_ROPE
    for hh in range(MLA_HEADS):
        sl = slice(hh * LANES, (hh + 1) * LANES)
        q_ref[:, sl] = _rope(_norm(qa[:, sl], na, gqa), ca, saa, sba, 16).astype(_BF)
        kh = jnp.where(lo, kv[:, sl], kpe)
        k_ref[:, sl] = _rope(_norm(kh, na, gka), ca, saa, sba, 16).astype(_BF)
        vt_ref[sl, :] = kv[:, sl].T.astype(_BF)
    for hh in range(GQA_HEADS):
        src = slice(512 + hh * LANES, 512 + (hh + 1) * LANES)
        dst = slice((MLA_HEADS + hh) * LANES, (MLA_HEADS + hh + 1) * LANES)
        q_ref[:, dst] = _rope(_norm(y[:, src], HEAD_DIM, gqb), cb, sab, sbb, 16).astype(_BF)
    for g in range(GQA_KV_HEADS):
        src = slice(1536 + g * LANES, 1536 + (g + 1) * LANES)
        dst = slice((MLA_HEADS + g) * LANES, (MLA_HEADS + g + 1) * LANES)
        k_ref[:, dst] = _rope(_norm(y[:, src], HEAD_DIM, gkb), cb, sab, sbb, 16).astype(_BF)
    vt_ref[MLA_HEADS * LANES:(MLA_HEADS + 1) * LANES, :] = y[:, 1792:1920].T.astype(_BF)


def _pre_even(x, mod, g, w_in, qn, w_uq, kvn, w_ukv, gains, rope, tm):
    B, S, D = x.shape
    nin = w_in.shape[1]
    nq = (MLA_HEADS + GQA_HEADS) * LANES
    nk = (MLA_HEADS + GQA_KV_HEADS) * LANES
    nkv = MLA_HEADS * LANES
    nvt = (MLA_HEADS + 1) * LANES
    full = lambda shp: pl.BlockSpec(shp, lambda b, i: (0,) * len(shp))
    tok = lambda w: pl.BlockSpec((None, tm, w), lambda b, i: (b, i, 0))
    return pl.pallas_call(
        _pre_even_kernel,
        grid=(B, S // tm),
        in_specs=[tok(D),
                  pl.BlockSpec((None, 6, D), lambda b, i: (b, 0, 0)),
                  full((1, D)), full((D, nin)), full((1, MLA_Q_RANK)), full((MLA_Q_RANK, nkv)),
                  full((1, MLA_KV_RANK)), full((MLA_KV_RANK, nkv)), full((4, LANES)),
                  pl.BlockSpec((tm, 6 * LANES), lambda b, i: (i, 0))],
        out_specs=[tok(nq), tok(nk), pl.BlockSpec((None, None, nvt, tm), lambda b, i: (b, i, 0, 0))],
        out_shape=[jax.ShapeDtypeStruct((B, S, nq), _BF), jax.ShapeDtypeStruct((B, S, nk), _BF),
                   jax.ShapeDtypeStruct((B, S // tm, nvt, tm), _BF)],
        compiler_params=_cparams(("parallel", "parallel")),
        name="pre_even",
    )(x, mod, g, w_in, qn, w_uq, kvn, w_ukv, gains, rope)


def _attn_kernel(q_ref, k_ref, vt_ref, o_ref, s_scr, *, shared_kv):
    tq = q_ref.shape[0]
    nk, _, tk = vt_ref.shape
    qs = [q_ref[:, hh * LANES:(hh + 1) * LANES] for hh in range(2)]

    def scores(j, hh):
        r0 = pl.multiple_of(j * tk, tk)
        kc = 0 if shared_kv else hh * LANES
        k = k_ref[pl.ds(r0, tk), kc:kc + LANES]
        return lax.dot_general(k, qs[hh], (((1,), (1,)), ((), ())), preferred_element_type=_F32)

    def step(j, hh, carry, cur):
        m, l, acc = carry
        s_scr[hh, 1 - cur] = scores(jnp.minimum(j + 1, nk - 1), hh)
        st = s_scr[hh, cur]
        kc = 0 if shared_kv else hh * LANES
        m_new = jnp.maximum(m, jnp.max(st, axis=0, keepdims=True))
        alpha = jnp.exp2(m - m_new)
        p = jnp.exp2(st - m_new)
        l = alpha * l + jnp.sum(p, axis=0, keepdims=True)
        vt = vt_ref[j, kc:kc + LANES, :]
        acc = alpha * acc + jnp.dot(vt, p.astype(_BF), preferred_element_type=_F32)
        return m_new, l, acc

    def body(jj, carry):
        c = list(carry)
        for sub in range(2):
            for hh in range(2):
                c[hh] = step(2 * jj + sub, hh, c[hh], sub)
        return tuple(c)

    for hh in range(2):
        s_scr[hh, 0] = scores(0, hh)
    init1 = (jnp.full((1, tq), NEG, _F32), jnp.zeros((1, tq), _F32), jnp.zeros((LANES, tq), _F32))
    res = lax.fori_loop(0, nk // 2, body, (init1, init1))
    halves = []
    for (_, l, acc) in res:
        o = acc / l
        if shared_kv:
            hi = (pl.program_id(1) // 2) == 1
            halves.append(jnp.where(hi, o[64:, :], o[:64, :]))
        else:
            halves.append(o[64:, :])
    o_ref[...] = jnp.concatenate(halves, axis=0).T.astype(o_ref.dtype)


def _attention(q, k, vt, *, q_off, k_off, shared_kv, tq):
    B, S, _ = q.shape
    nk, _, tk = vt.shape[1:]
    assert nk % 2 == 0
    npairs = 4
    if shared_kv:
        kspec = pl.BlockSpec((None, S, LANES), lambda b, p, i: (b, 0, k_off + p // 2))
        vspec = pl.BlockSpec((None, nk, LANES, tk), lambda b, p, i: (b, 0, k_off, 0))
    else:
        kspec = pl.BlockSpec((None, S, 2 * LANES), lambda b, p, i: (b, 0, k_off // 2 + p))
        vspec = pl.BlockSpec((None, nk, 2 * LANES, tk), lambda b, p, i: (b, 0, k_off // 2 + p, 0))
    return pl.pallas_call(
        functools.partial(_attn_kernel, shared_kv=shared_kv),
        grid=(B, npairs, S // tq),
        in_specs=[pl.BlockSpec((None, tq, 2 * LANES), lambda b, p, i: (b, i, q_off // 2 + p)),
                  kspec, vspec],
        out_specs=pl.BlockSpec((None, tq, LANES), lambda b, p, i: (b, i, p)),
        out_shape=jax.ShapeDtypeStruct((B, S, npairs * LANES), _BF),
        scratch_shapes=[pltpu.VMEM((2, 2, tk, tq), _F32)],
        compiler_params=_cparams(("parallel", "parallel", "parallel")),
        name="attn_gqa" if shared_kv else "attn_mla",
    )(q, k, vt)


def _oproj_kernel(*refs, n_in, route):
    x_ref, mod_ref, g_ref, w_ref = refs[0], refs[1], refs[2], refs[3]
    o_refs = refs[4:4 + n_in]
    pos = 4 + n_in
    if route:
        r_ref = refs[pos]
        pos += 1
    x_out, h_out = refs[pos], refs[pos + 1]
    y = None
    off = 0
    for o_ref in o_refs:
        if len(o_ref.shape) == 3:
            o = jnp.concatenate([o_ref[p] for p in range(o_ref.shape[0])], axis=1)
        else:
            o = o_ref[...]
        w = o.shape[1]
        t = jnp.dot(o, w_ref[off:off + w, :], preferred_element_type=_F32)
        y = t if y is None else y + t
        off += w
    x1 = x_ref[...] + mod_ref[2:3, :] * y
    x_out[...] = x1
    h = _modulate(x1, g_ref[...], mod_ref[3:4, :], mod_ref[4:5, :])
    h_out[...] = h.astype(h_out.dtype)
    if route:
        route_out = refs[pos + 2]
        hh = h.astype(_BF)
        hl = (h - hh.astype(_F32)).astype(_BF)
        rh, rl = r_ref[0], r_ref[1]
        logits = (jnp.dot(hh, rh, preferred_element_type=_F32)
                  + jnp.dot(hh, rl, preferred_element_type=_F32)
                  + jnp.dot(hl, rh, preferred_element_type=_F32))
        tm = logits.shape[0]
        lane = lax.broadcasted_iota(jnp.int32, (tm, LANES), 1)
        lg = jnp.where(lane < N_EXPERTS, logits, NEG)
        m1 = jnp.max(lg, axis=-1, keepdims=True)
        lanef = lane.astype(_F32)
        i1 = jnp.min(jnp.where(lg == m1, lanef, float(LANES)), axis=-1, keepdims=True)
        lg2 = jnp.where(lanef == i1, NEG, lg)
        m2 = jnp.max(lg2, axis=-1, keepdims=True)
        i2 = jnp.min(jnp.where(lg2 == m2, lanef, float(LANES)), axis=-1, keepdims=True)
        e = jnp.exp(m2 - m1)
        g1 = 1.0 / (1.0 + e)
        g2 = e / (1.0 + e)
        route_out[...] = jnp.where(lane == 0, i1,
                                   jnp.where(lane == 1, i2,
                                             jnp.where(lane == 2, g1, jnp.where(lane == 3, g2, 0.0))))


def _oproj(x, mod, g, w, o_list, tm, h_dtype, router=None):
    B, S, D = x.shape
    n_in = len(o_list)
    route = router is not None
    tok = lambda wd: pl.BlockSpec((None, tm, wd), lambda b, i: (b, i, 0))
    in_specs = [tok(D), pl.BlockSpec((None, 6, D), lambda b, i: (b, 0, 0)),
                pl.BlockSpec((1, D), lambda b, i: (0, 0)),
                pl.BlockSpec(w.shape, lambda b, i: (0, 0))]
    for o in o_list:
        if o.ndim == 4:
            in_specs.append(pl.BlockSpec((None, o.shape[1], tm, LANES), lambda b, i: (b, 0, i, 0)))
        else:
            in_specs.append(tok(o.shape[2]))
    args = [x, mod, g, w] + list(o_list)
    out_specs = [tok(D), tok(D)]
    out_shape = [jax.ShapeDtypeStruct((B, S, D), _F32), jax.ShapeDtypeStruct((B, S, D), h_dtype)]
    if route:
        in_specs.append(pl.BlockSpec(router.shape, lambda b, i: (0, 0, 0)))
        args.append(router)
        out_specs.append(tok(LANES))
        out_shape.append(jax.ShapeDtypeStruct((B, S, LANES), _F32))
    return pl.pallas_call(
        functools.partial(_oproj_kernel, n_in=n_in, route=route),
        grid=(B, S // tm),
        in_specs=in_specs, out_specs=out_specs, out_shape=out_shape,
        compiler_params=_cparams(("parallel", "parallel")),
        name="oproj_route" if route else "oproj",
    )(*args)


def _ffn_kernel(x_ref, h_ref, mod_ref, modn_ref, gn_ref, wg_ref, wu_ref, wd_ref, x_out, h_out, acc):
    j = pl.program_id(1)
    nf = pl.num_programs(1)
    h = h_ref[...]
    a = _silu(jnp.dot(h, wg_ref[...], preferred_element_type=_F32)) * jnp.dot(
        h, wu_ref[...], preferred_element_type=_F32)
    t = jnp.dot(a.astype(_BF), wd_ref[...], preferred_element_type=_F32)

    @pl.when(j == 0)
    def _():
        acc[...] = t

    @pl.when(jnp.logical_and(j > 0, j < nf - 1))
    def _():
        acc[...] += t

    @pl.when(j == nf - 1)
    def _():
        x2 = x_ref[...] + mod_ref[5:6, :] * (acc[...] + t)
        x_out[...] = x2
        h_out[...] = _modulate(x2, gn_ref[...], modn_ref[0:1, :], modn_ref[1:2, :]).astype(h_out.dtype)


def _ffn(x, h, mod, modn, gn, wg, wu, wd, tm, tf):
    B, S, D = x.shape
    T = B * S
    F = wg.shape[1]
    per_b = S // tm
    x2 = x.reshape(T, D)
    h2 = h.reshape(T, D)
    tok = pl.BlockSpec((tm, D), lambda i, j: (i, 0))
    modspec = pl.BlockSpec((None, 6, D), lambda i, j: (i // per_b, 0, 0))
    xo, ho = pl.pallas_call(
        _ffn_kernel,
        grid=(T // tm, F // tf),
        in_specs=[tok, tok, modspec, modspec, pl.BlockSpec((1, D), lambda i, j: (0, 0)),
                  pl.BlockSpec((D, tf), lambda i, j: (0, j)),
                  pl.BlockSpec((D, tf), lambda i, j: (0, j)),
                  pl.BlockSpec((tf, D), lambda i, j: (j, 0))],
        out_specs=[tok, tok],
        out_shape=[jax.ShapeDtypeStruct((T, D), _F32), jax.ShapeDtypeStruct((T, D), _BF)],
        scratch_shapes=[pltpu.VMEM((tm, D), _F32)],
        compiler_params=_cparams(("parallel", "arbitrary")),
        name="ffn_dense",
    )(x2, h2, mod, modn, gn, wg, wu, wd)
    return xo.reshape(B, S, D), ho.reshape(B, S, D)


def _qkv_kernel(h_ref, w_ref, gains_ref, rope_ref, q_ref, k_ref, v_ref):
    tm, D = h_ref.shape
    y = jnp.dot(h_ref[...], w_ref[...], preferred_element_type=_F32)
    lane = lax.broadcasted_iota(jnp.int32, (tm, LANES), 1)
    lo = lane < 64
    c, sa, sb = rope_ref[:, 0:128], rope_ref[:, 128:256], rope_ref[:, 256:384]

    def head_pair(v, g):
        sq = v * v
        s_lo = jnp.sum(jnp.where(lo, sq, 0.0), axis=-1, keepdims=True)
        s_hi = jnp.sum(jnp.where(lo, 0.0, sq), axis=-1, keepdims=True)
        r = lax.rsqrt(jnp.where(lo, s_lo, s_hi) * (1.0 / HEAD_DIM) + EPS)
        return _rope(v * r * g, c, sa, sb, 32)

    for p in range(D // LANES):
        sl = slice(p * LANES, (p + 1) * LANES)
        q_ref[p] = head_pair(y[:, sl], gains_ref[0:1, :]).astype(_BF)
        k_ref[p] = head_pair(y[:, D + p * LANES:D + (p + 1) * LANES], gains_ref[1:2, :]).astype(_BF)
        v_ref[p] = y[:, 2 * D + p * LANES:2 * D + (p + 1) * LANES].astype(_BF)


def _qkv(h, w, gains, rope, tm):
    B, S, D = h.shape
    tok = pl.BlockSpec((None, tm, D), lambda b, i: (b, i, 0))
    hp = pl.BlockSpec((None, D // LANES, tm, LANES), lambda b, i: (b, 0, i, 0))
    sds = jax.ShapeDtypeStruct((B, D // LANES, S, LANES), _BF)
    return pl.pallas_call(
        _qkv_kernel,
        grid=(B, S // tm),
        in_specs=[tok, pl.BlockSpec(w.shape, lambda b, i: (0, 0)),
                  pl.BlockSpec((2, LANES), lambda b, i: (0, 0)),
                  pl.BlockSpec((tm, 3 * LANES), lambda b, i: (i, 0))],
        out_specs=[hp, hp, hp],
        out_shape=[sds, sds, sds],
        compiler_params=_cparams(("parallel", "parallel")),
        name="qkv_dil",
    )(h, w, gains, rope)


def _dil_kernel(q_ref, k_ref, v_ref, o_ref, qf, kf, vf, acc, mm, ll, *, tl):
    S = q_ref.shape[0]
    qf[...] = q_ref[...].astype(_F32)
    kf[...] = k_ref[...].astype(_F32)
    vf[...] = v_ref[...].astype(_F32)
    lo = lax.broadcasted_iota(jnp.int32, (tl, LANES), 1) < 64
    for bi, (window, d) in enumerate(DIL_PATTERNS):
        L = S // d
        half = window // (2 * d)
        t = min(tl, L)
        W = min(t + 2 * half, L)
        nt = L // t

        def rows(first, n, d=d):
            return pl.ds(first, n) if d == 1 else pl.ds(first, n, stride=d)

        def body(idx, carry, bi=bi, d=d, L=L, half=half, t=t, W=W, nt=nt, rows=rows):
            r = idx // nt
            l0 = (idx % nt) * t
            start = jnp.clip(l0 - half, 0, L - W)
            qt = qf[rows(r + d * l0, t), :].astype(_BF)
            kw = kf[rows(r + d * start, W), :].astype(_BF)
            vw = vf[rows(r + d * start, W), :].astype(_BF)
            lo_t = lo[:t]
            zq = jnp.zeros_like(qt)
            q2 = jnp.concatenate([jnp.where(lo_t, qt, zq), jnp.where(lo_t, zq, qt)], axis=0)
            s = lax.dot_general(q2, kw, (((1,), (1,)), ((), ())), preferred_element_type=_F32)
            qpos = l0 + lax.broadcasted_iota(jnp.int32, (t, W), 0)
            kpos = start + lax.broadcasted_iota(jnp.int32, (t, W), 1)
            ok = jnp.abs(kpos - qpos) <= half
            s = jnp.where(jnp.concatenate([ok, ok], axis=0), s, NEG)
            m = jnp.max(s, axis=-1, keepdims=True)
            p = jnp.exp2(s - m)
            den = jnp.sum(p, axis=-1, keepdims=True)
            o2 = jnp.dot(p.astype(_BF), vw, preferred_element_type=_F32)
            o_new = jnp.where(lo_t, o2[:t], o2[t:])
            m_new = jnp.where(lo_t, m[:t], m[t:])
            l_new = jnp.where(lo_t, den[:t], den[t:])
            tok = rows(r + d * l0, t)
            if bi == 0:
                acc[tok, :] = o_new
                mm[tok, :] = m_new
                ll[tok, :] = l_new
            else:
                m_old = mm[tok, :]
                mx = jnp.maximum(m_old, m_new)
                a_old = jnp.exp2(m_old - mx)
                a_new = jnp.exp2(m_new - mx)
                acc[tok, :] = acc[tok, :] * a_old + o_new * a_new
                ll[tok, :] = ll[tok, :] * a_old + l_new * a_new
                mm[tok, :] = mx
            return carry

        lax.fori_loop(0, d * nt, body, 0, unroll=min(4, d * nt))
    o_ref[...] = (acc[...] / ll[...]).astype(o_ref.dtype)


def _dilated(q, k, v, tl):
    B, P, S, _ = q.shape
    spec = pl.BlockSpec((None, None, S, LANES), lambda b, p: (b, p, 0, 0))
    scr = pltpu.VMEM((S, LANES), _F32)
    return pl.pallas_call(
        functools.partial(_dil_kernel, tl=tl),
        grid=(B, P),
        in_specs=[spec, spec, spec],
        out_specs=spec,
        out_shape=jax.ShapeDtypeStruct((B, P, S, LANES), _BF),
        scratch_shapes=[scr] * 6,
        compiler_params=_cparams(("parallel", "parallel")),
        name="dil_attn",
    )(q, k, v)


def _moe_kernel(te_ref, tv_ref, src_hbm, dst_hbm, h_hbm, wg_ref, wu_ref, wd_ref, out_hbm,
                xbuf, xbf, acc, ybuf, src_s, dst_s, gsem, ssem, isem):
    i = pl.program_id(0)
    j = pl.program_id(1)
    nt = pl.num_programs(0)
    nf = pl.num_programs(1)
    tm = xbf.shape[0]
    valid = tv_ref[i] == 1
    slot = i % 2
    nxt_valid = jnp.logical_and(i + 1 < nt, tv_ref[jnp.minimum(i + 1, nt - 1)] == 1)

    def src_copy(tile):
        return pltpu.make_async_copy(src_hbm.at[tile], src_s, isem.at[0])

    def dst_copy(tile):
        return pltpu.make_async_copy(dst_hbm.at[tile], dst_s, isem.at[1])

    def gather_row(r, slot_):
        return pltpu.make_async_copy(h_hbm.at[pl.ds(src_s[r], 1)], xbuf.at[slot_, pl.ds(r, 1)],
                                     gsem.at[slot_])

    def scatter_row(r):
        return pltpu.make_async_copy(ybuf.at[pl.ds(r, 1)], out_hbm.at[pl.ds(dst_s[r], 1)], ssem.at[0])

    def for_rows(fn):
        def body(r, c):
            fn(r)
            return c
        lax.fori_loop(0, tm, body, 0, unroll=8)

    @pl.when(jnp.logical_and(jnp.logical_and(i == 0, j == 0), valid))
    def _():
        src_copy(0).start()
        src_copy(0).wait()
        for_rows(lambda r: gather_row(r, 0).start())

    def gather_wait(slot_):
        return pltpu.make_async_copy(h_hbm.at[pl.ds(0, tm)], xbuf.at[slot_], gsem.at[slot_])

    def scatter_wait():
        return pltpu.make_async_copy(ybuf, out_hbm.at[pl.ds(0, tm)], ssem.at[0])

    @pl.when(jnp.logical_and(j == 0, valid))
    def _():
        gather_wait(slot).wait()
        xbf[...] = xbuf[slot].astype(_BF)
        dst_copy(i).start()

    @pl.when(jnp.logical_and(j == 0, nxt_valid))
    def _():
        src_copy(i + 1).start()

    @pl.when(jnp.logical_and(j == 1, valid))
    def _():
        dst_copy(i).wait()

    @pl.when(jnp.logical_and(j == 1, nxt_valid))
    def _():
        src_copy(i + 1).wait()
        for_rows(lambda r: gather_row(r, 1 - slot).start())

    @pl.when(valid)
    def _():
        x = xbf[...]
        a = _silu(jnp.dot(x, wg_ref[...], preferred_element_type=_F32)) * jnp.dot(
            x, wu_ref[...], preferred_element_type=_F32)
        t = jnp.dot(a.astype(_BF), wd_ref[...], preferred_element_type=_F32)

        @pl.when(j == 0)
        def _():
            acc[...] = t

        @pl.when(j > 0)
        def _():
            acc[...] += t

    prev_valid = jnp.logical_and(i >= 1, tv_ref[jnp.maximum(i - 1, 0)] == 1)

    @pl.when(jnp.logical_and(j == nf - 1, prev_valid))
    def _():
        scatter_wait().wait()

    @pl.when(jnp.logical_and(j == nf - 1, valid))
    def _():
        ybuf[...] = acc[...]
        for_rows(lambda r: scatter_row(r).start())

    @pl.when(jnp.logical_and(jnp.logical_and(j == nf - 1, i == nt - 1), valid))
    def _():
        scatter_wait().wait()


def _moe(h, tile_e, tile_v, src, dst, wg, wu, wd, n_out_rows, tm, tf):
    T, D = h.shape
    nt = tile_e.shape[0]
    F = wg.shape[2]
    nf = F // tf
    assert nf >= 2

    def wcol(i, j, te, tv):
        return (te[i], 0, jnp.where(tv[i] == 1, j, nf - 1))

    def wrow(i, j, te, tv):
        return (te[i], jnp.where(tv[i] == 1, j, nf - 1), 0)

    grid_spec = pltpu.PrefetchScalarGridSpec(
        num_scalar_prefetch=2,
        grid=(nt, nf),
        in_specs=[pl.BlockSpec(memory_space=pl.ANY), pl.BlockSpec(memory_space=pl.ANY),
                  pl.BlockSpec(memory_space=pl.ANY),
                  pl.BlockSpec((None, D, tf), wcol), pl.BlockSpec((None, D, tf), wcol),
                  pl.BlockSpec((None, tf, D), wrow)],
        out_specs=pl.BlockSpec(memory_space=pl.ANY),
        scratch_shapes=[pltpu.VMEM((2, tm, D), _F32), pltpu.VMEM((tm, D), _BF),
                        pltpu.VMEM((tm, D), _F32), pltpu.VMEM((tm, D), _F32),
                        pltpu.SMEM((tm,), jnp.int32), pltpu.SMEM((tm,), jnp.int32),
                        pltpu.SemaphoreType.DMA((2,)), pltpu.SemaphoreType.DMA((1,)),
                        pltpu.SemaphoreType.DMA((2,))],
    )
    return pl.pallas_call(
        _moe_kernel,
        grid_spec=grid_spec,
        out_shape=jax.ShapeDtypeStruct((n_out_rows, D), _F32),
        compiler_params=_cparams(("arbitrary", "arbitrary")),
        name="moe_experts",
    )(tile_e, tile_v, src, dst, h, wg, wu, wd)


def _route_plan(route, T, tm):
    A = T * TOP_K
    e_flat = route[:, :TOP_K].astype(jnp.int32).reshape(A)
    order = jnp.argsort(e_flat, stable=True).astype(jnp.int32)
    counts = jnp.sum((e_flat[:, None] == jnp.arange(N_EXPERTS, dtype=jnp.int32)[None, :]).astype(jnp.int32), axis=0)
    starts = jnp.cumsum(counts) - counts
    pcounts = (counts + tm - 1) // tm * tm
    pends = jnp.cumsum(pcounts)
    pstarts = pends - pcounts
    nt = A // tm + N_EXPERTS
    tile0 = jnp.arange(nt, dtype=jnp.int32) * tm
    tile_v = (tile0 < pends[-1]).astype(jnp.int32)
    last_valid = jnp.maximum(pends[-1] // tm - 1, 0)
    tile_e_raw = jnp.minimum(jnp.searchsorted(pends, tile0, side="right"), N_EXPERTS - 1).astype(jnp.int32)
    tile_e = jnp.where(tile_v == 1, tile_e_raw, tile_e_raw[last_valid])
    r = jnp.arange(nt * tm, dtype=jnp.int32)
    e_r = jnp.repeat(tile_e, tm)
    within = r - pstarts[e_r]
    ok = jnp.logical_and(within < counts[e_r], jnp.repeat(tile_v, tm) == 1)
    a = order[jnp.clip(starts[e_r] + within, 0, A - 1)]
    src = jnp.where(ok, a // TOP_K, 0).astype(jnp.int32)
    dump = A + e_r * tm + jnp.clip(within - counts[e_r], 0, tm - 1)
    dst = jnp.where(ok, (a % TOP_K) * T + a // TOP_K, dump).astype(jnp.int32)
    return tile_e, tile_v, src.reshape(nt, tm), dst.reshape(nt, tm), A + N_EXPERTS * tm


def _combine_kernel(x_ref, y1_ref, y2_ref, r_ref, mod_ref, o_ref):
    y = y1_ref[...] * r_ref[:, 2:3] + y2_ref[...] * r_ref[:, 3:4]
    o_ref[...] = x_ref[...] + mod_ref[5:6, :] * y


def _combine(x, y2, route, mod, tm):
    B, S, D = x.shape
    T = B * S
    per_b = S // tm
    out = pl.pallas_call(
        _combine_kernel,
        grid=(T // tm,),
        in_specs=[pl.BlockSpec((tm, D), lambda i: (i, 0)),
                  pl.BlockSpec((tm, D), lambda i: (i, 0)),
                  pl.BlockSpec((tm, D), lambda i: (T // tm + i, 0)),
                  pl.BlockSpec((tm, LANES), lambda i: (i, 0)),
                  pl.BlockSpec((None, 6, D), lambda i: (i // per_b, 0, 0))],
        out_specs=pl.BlockSpec((tm, D), lambda i: (i, 0)),
        out_shape=jax.ShapeDtypeStruct((T, D), _F32),
        compiler_params=_cparams(("parallel",)),
        name="moe_combine",
    )(x.reshape(T, D), y2, y2, route.reshape(T, LANES), mod)
    return out.reshape(B, S, D)


def _pad_cols(a, w):
    return jnp.pad(a, ((0, 0), (0, w - a.shape[1])))


def _rope_tables_even(S):
    pos = jnp.arange(S, dtype=jnp.int32)
    inv = ROPE_THETA ** (-jnp.arange(0, 32, 2, dtype=_F32) / 32)
    def cs(p):
        ang = p.astype(_F32)[:, None] * inv[None, :]
        return jnp.cos(ang), jnp.sin(ang)
    one = lambda w: jnp.ones((S, w), _F32)
    zero = lambda w: jnp.zeros((S, w), _F32)
    c, s = cs(pos)
    ca = jnp.concatenate([one(64), c, c, one(32)], 1)
    saa = jnp.concatenate([zero(64), -s, zero(16), zero(32)], 1)
    sba = jnp.concatenate([zero(64), zero(16), s, zero(32)], 1)
    cr, sr = cs(pos // GRID_W)
    cc, sc = cs(pos % GRID_W)
    cb = jnp.concatenate([cr, cr, cc, cc, one(64)], 1)
    sab = jnp.concatenate([-sr, zero(16), -sc, zero(16), zero(64)], 1)
    sbb = jnp.concatenate([zero(16), sr, zero(16), sc, zero(64)], 1)
    return jnp.concatenate([ca, saa, sba, cb, sab, sbb], 1)


def _rope_tables_odd(S):
    pos = jnp.arange(S, dtype=_F32)
    inv = ROPE_THETA ** (-jnp.arange(0, HEAD_DIM, 2, dtype=_F32) / HEAD_DIM)
    ang = pos[:, None] * inv[None, :]
    c, s = jnp.cos(ang), jnp.sin(ang)
    z = jnp.zeros_like(s)
    return jnp.concatenate([c, c, c, c, -s, z, -s, z, z, s, z, s], 1)


def _tiles(S):
    return dict(tm_pre=min(S, 512), tq=min(S, 1024), tm_o=min(S, 512),
                tm_ffn=min(S, 1024), tf_ffn=256, tm_qkv=min(S, 512),
                tl=128, tm_moe=min(S, 1024), tf_moe=512, tm_c=min(S, 512))


def kernel(x, c, ada_even_w, ada_even_b, norm_even_mix, norm_even_ffn, even_w_in, mla_q_norm, mla_w_uq, mla_kv_norm, mla_w_ukv, mla_q_gain, mla_k_gain, gqa_q_gain, gqa_k_gain, even_w_out, ffn_w_gate, ffn_w_up, ffn_w_down, ada_odd_w, ada_odd_b, norm_odd_mix, norm_odd_ffn, dil_w_qkv, dil_q_gain, dil_k_gain, dil_w_out, moe_router, moe_w_gate, moe_w_up, moe_w_down):
    B, S, D = x.shape
    T = B * S
    cfg = _tiles(S)

    mod_e = _ada_mod(c, ada_even_w[0], ada_even_b[0]).reshape(B, 6, D)
    mod_o = _ada_mod(c, ada_odd_w[0], ada_odd_b[0]).reshape(B, 6, D)

    w = even_w_in[0]
    sp = [MLA_Q_RANK, MLA_Q_RANK + MLA_KV_RANK, MLA_Q_RANK + MLA_KV_RANK + MLA_ROPE]
    sp.append(sp[-1] + GQA_HEADS * HEAD_DIM)
    sp.append(sp[-1] + GQA_KV_HEADS * HEAD_DIM)
    w_cq, w_ckv, w_kpe = w[:, :sp[0]], w[:, sp[0]:sp[1]], w[:, sp[1]:sp[2]]
    w_qb, w_kb, w_vb = w[:, sp[2]:sp[3]], w[:, sp[3]:sp[4]], w[:, sp[4]:]
    zc = lambda n: jnp.zeros((D, n), _F32)
    slot64 = lambda a, n: jnp.pad(a.reshape(D, n, HEAD_DIM), ((0, 0), (0, 0), (0, LANES - HEAD_DIM))).reshape(D, n * LANES)
    w_in = jnp.concatenate([w_cq, w_ckv, zc(64), w_kpe, zc(32), slot64(w_qb, GQA_HEADS),
                            slot64(w_kb, GQA_KV_HEADS), w_vb], axis=1).astype(_BF)
    na = MLA_NOPE + MLA_ROPE
    w_uq = jnp.pad(mla_w_uq[0].reshape(MLA_Q_RANK, MLA_HEADS, na),
                   ((0, 0), (0, 0), (0, LANES - na))).reshape(MLA_Q_RANK, MLA_HEADS * LANES).astype(_BF)
    w_ukv = mla_w_ukv[0].astype(_BF)
    pad128 = lambda v: jnp.pad(v, (0, LANES - v.shape[0]))
    gains_e = jnp.stack([pad128(mla_q_gain[0]) * (na ** -0.5 * LOG2E), pad128(mla_k_gain[0]),
                         pad128(gqa_q_gain[0]) * (HEAD_DIM ** -0.5 * LOG2E), pad128(gqa_k_gain[0])])
    rope_e = _rope_tables_even(S)
    q_all, k_all, vt_all = _pre_even(
        x, mod_e, norm_even_mix[0].reshape(1, D), w_in, mla_q_norm[0].reshape(1, -1), w_uq,
        mla_kv_norm[0].reshape(1, -1), w_ukv, gains_e, rope_e, cfg["tm_pre"])
    o_a = _attention(q_all, k_all, vt_all, q_off=0, k_off=0, shared_kv=False, tq=cfg["tq"])
    o_b = _attention(q_all, k_all, vt_all, q_off=MLA_HEADS, k_off=MLA_HEADS, shared_kv=True, tq=cfg["tq"])
    x1, h1 = _oproj(x, mod_e, norm_even_ffn[0].reshape(1, D), even_w_out[0].astype(_BF), [o_a, o_b],
                    cfg["tm_o"], _BF)
    x2, h2 = _ffn(x1, h1, mod_e, mod_o, norm_odd_mix[0].reshape(1, D), ffn_w_gate[0].astype(_BF),
                  ffn_w_up[0].astype(_BF), ffn_w_down[0].astype(_BF), cfg["tm_ffn"], cfg["tf_ffn"])

    pair = lambda v: jnp.concatenate([v, v])
    gains_o = jnp.stack([pair(dil_q_gain[0]) * (HEAD_DIM ** -0.5 * LOG2E), pair(dil_k_gain[0])])
    qd, kd, vd = _qkv(h2, dil_w_qkv[0].astype(_BF), gains_o, _rope_tables_odd(S), cfg["tm_qkv"])
    o_d = _dilated(qd, kd, vd, cfg["tl"])
    r32 = _pad_cols(moe_router[0], LANES)
    r_hi = r32.astype(_BF)
    r_lo = (r32 - r_hi.astype(_F32)).astype(_BF)
    x3, h3, route = _oproj(x2, mod_o, norm_odd_ffn[0].reshape(1, D), dil_w_out[0].astype(_BF), [o_d],
                           cfg["tm_o"], _F32, router=jnp.stack([r_hi, r_lo]))
    tm = cfg["tm_moe"]
    tile_e, tile_v, src, dst, n_rows = _route_plan(route.reshape(T, LANES), T, tm)
    y2 = _moe(h3.reshape(T, D), tile_e, tile_v, src, dst, moe_w_gate[0].astype(_BF),
              moe_w_up[0].astype(_BF), moe_w_down[0].astype(_BF), n_rows, tm, cfg["tf_moe"])
    return _combine(x3, y2, route, mod_o, cfg["tm_c"])
```

```python
import functools
import math

import jax
import jax.numpy as jnp
from jax import lax
from jax.experimental import pallas as pl
from jax.experimental.pallas import tpu as pltpu

_BF = jnp.bfloat16
_F32 = jnp.float32

GRID_W = 64
HEAD_DIM = 64
ROPE_THETA = 10000.0
EPS = 1e-6
MLA_HEADS = 8
MLA_Q_RANK = 256
MLA_KV_RANK = 128
MLA_NOPE = 64
MLA_ROPE = 32
MLA_V = 64
GQA_HEADS = 8
GQA_KV_HEADS = 2
DIL_PATTERNS = ((128, 1), (512, 4), (2048, 16))
N_EXPERTS = 8
TOP_K = 2
NEG = -1e30
LOG2E = math.log2(math.e)

LANES = 128
VMEM_LIMIT = 56 * 1024 * 1024


def _cparams(sem, vmem=VMEM_LIMIT):
    return pltpu.CompilerParams(dimension_semantics=sem, vmem_limit_bytes=vmem)


def _silu(x):
    return x / (1.0 + jnp.exp(-x))


def _modulate(x, g, shift, scale):
    ms = jnp.mean(x * x, axis=-1, keepdims=True)
    return x * lax.rsqrt(ms + EPS) * g * (1.0 + scale) + shift


def _norm(v, n, g):
    return v * lax.rsqrt(jnp.sum(v * v, axis=-1, keepdims=True) * (1.0 / n) + EPS) * g


SUBLANES = 8


def _rows_to_tiles(ref, x):
    tm = x.shape[0]
    for c in range(SUBLANES):
        ref[pl.ds(c, tm, stride=SUBLANES), :] = x[:, c * LANES:(c + 1) * LANES]


def _tiles_to_rows(ref, tm):
    return jnp.concatenate([ref[pl.ds(c, tm, stride=SUBLANES), :] for c in range(SUBLANES)], axis=1)


def _rope(x, c, sa, sb, k):
    return x * c + pltpu.roll(x, LANES - k, 1) * sa + pltpu.roll(x, k, 1) * sb


def _mod_kernel(c_ref, w_ref, b_ref, o_ref):
    sc = _silu(c_ref[...])
    o_ref[...] = jnp.dot(sc.astype(_BF), w_ref[...].astype(_BF),
                         preferred_element_type=_F32) + b_ref[...]


def _ada_mod(c, w, b):
    B, D = c.shape
    N = w.shape[1]
    tn = min(N, 1536)
    return pl.pallas_call(
        _mod_kernel,
        grid=(N // tn,),
        in_specs=[pl.BlockSpec((B, D), lambda j: (0, 0)),
                  pl.BlockSpec((D, tn), lambda j: (0, j)),
                  pl.BlockSpec((1, tn), lambda j: (0, j))],
        out_specs=pl.BlockSpec((B, tn), lambda j: (0, j)),
        out_shape=jax.ShapeDtypeStruct((B, N), _F32),
        compiler_params=_cparams(("arbitrary",)),
        name="ada_mod",
    )(c, w, b.reshape(1, N))


def _pre_even_kernel(x_ref, mod_ref, g_ref, w_in_ref, qn_ref, w_uq_ref, kvn_ref, w_ukv_ref,
                     gains_ref, rope_ref, q_ref, k_ref, vt_ref):
    tm = x_ref.shape[0]
    h = _modulate(x_ref[...], g_ref[...], mod_ref[0:1, :], mod_ref[1:2, :]).astype(_BF)
    y = jnp.dot(h, w_in_ref[...], preferred_element_type=_F32)
    cqn = _norm(y[:, 0:256], MLA_Q_RANK, qn_ref[...]).astype(_BF)
    qa = jnp.dot(cqn, w_uq_ref[...], preferred_element_type=_F32)
    ckvn = _norm(y[:, 256:384], MLA_KV_RANK, kvn_ref[...]).astype(_BF)
    kv = jnp.dot(ckvn, w_ukv_ref[...], preferred_element_type=_F32)
    kpe = y[:, 384:512]
    lo = lax.broadcasted_iota(jnp.int32, (tm, LANES), 1) < 64
    ca, saa, sba = rope_ref[:, 0:128], rope_ref[:, 128:256], rope_ref[:, 256:384]
    cb, sab, sbb = rope_ref[:, 384:512], rope_ref[:, 512:640], rope_ref[:, 640:768]
    gqa, gka = gains_ref[0:1, :], gains_ref[1:2, :]
    gqb, gkb = gains_ref[2:3, :], gains_ref[3:4, :]
    na = MLA_NOPE + MLA_ROPE
    for hh in range(MLA_HEADS):
        sl = slice(hh * LANES, (hh + 1) * LANES)
        q_ref[:, sl] = _rope(_norm(qa[:, sl], na, gqa), ca, saa, sba, 16).astype(_BF)
        kh = jnp.where(lo, kv[:, sl], kpe)
        k_ref[:, sl] = _rope(_norm(kh, na, gka), ca, saa, sba, 16).astype(_BF)
        vt_ref[sl, :] = kv[:, sl].T.astype(_BF)
    for hh in range(GQA_HEADS):
        src = slice(512 + hh * LANES, 512 + (hh + 1) * LANES)
        dst = slice((MLA_HEADS + hh) * LANES, (MLA_HEADS + hh + 1) * LANES)
        q_ref[:, dst] = _rope(_norm(y[:, src], HEAD_DIM, gqb), cb, sab, sbb, 16).astype(_BF)
    for g in range(GQA_KV_HEADS):
        src = slice(1536 + g * LANES, 1536 + (g + 1) * LANES)
        dst = slice((MLA_HEADS + g) * LANES, (MLA_HEADS + g + 1) * LANES)
        k_ref[:, dst] = _rope(_norm(y[:, src], HEAD_DIM, gkb), cb, sab, sbb, 16).astype(_BF)
    vt_ref[MLA_HEADS * LANES:(MLA_HEADS + 1) * LANES, :] = y[:, 1792:1920].T.astype(_BF)


def _pre_even(x, mod, g, w_in, qn, w_uq, kvn, w_ukv, gains, rope, tm):
    B, S, D = x.shape
    nin = w_in.shape[1]
    nq = (MLA_HEADS + GQA_HEADS) * LANES
    nk = (MLA_HEADS + GQA_KV_HEADS) * LANES
    nkv = MLA_HEADS * LANES
    nvt = (MLA_HEADS + 1) * LANES
    full = lambda shp: pl.BlockSpec(shp, lambda b, i: (0,) * len(shp))
    tok = lambda w: pl.BlockSpec((None, tm, w), lambda b, i: (b, i, 0))
    return pl.pallas_call(
        _pre_even_kernel,
        grid=(B, S // tm),
        in_specs=[tok(D),
                  pl.BlockSpec((None, 6, D), lambda b, i: (b, 0, 0)),
                  full((1, D)), full((D, nin)), full((1, MLA_Q_RANK)), full((MLA_Q_RANK, nkv)),
                  full((1, MLA_KV_RANK)), full((MLA_KV_RANK, nkv)), full((4, LANES)),
                  pl.BlockSpec((tm, 6 * LANES), lambda b, i: (i, 0))],
        out_specs=[tok(nq), tok(nk), pl.BlockSpec((None, None, nvt, tm), lambda b, i: (b, i, 0, 0))],
        out_shape=[jax.ShapeDtypeStruct((B, S, nq), _BF), jax.ShapeDtypeStruct((B, S, nk), _BF),
                   jax.ShapeDtypeStruct((B, S // tm, nvt, tm), _BF)],
        compiler_params=_cparams(("parallel", "parallel")),
        name="pre_even",
    )(x, mod, g, w_in, qn, w_uq, kvn, w_ukv, gains, rope)


def _attn_kernel(q_ref, k_ref, vt_ref, o_ref, s_scr, *, shared_kv):
    tq = q_ref.shape[0]
    nk, _, tk = vt_ref.shape
    qs = [q_ref[:, hh * LANES:(hh + 1) * LANES] for hh in range(2)]

    def scores(j, hh):
        r0 = pl.multiple_of(j * tk, tk)
        kc = 0 if shared_kv else hh * LANES
        k = k_ref[pl.ds(r0, tk), kc:kc + LANES]
        return lax.dot_general(k, qs[hh], (((1,), (1,)), ((), ())), preferred_element_type=_F32)

    def step(j, hh, carry, cur):
        m, l, acc = carry
        s_scr[hh, 1 - cur] = scores(jnp.minimum(j + 1, nk - 1), hh)
        st = s_scr[hh, cur]
        kc = 0 if shared_kv else hh * LANES
        m_new = jnp.maximum(m, jnp.max(st, axis=0, keepdims=True))
        alpha = jnp.exp2(m - m_new)
        p = jnp.exp2(st - m_new)
        l = alpha * l + jnp.sum(p, axis=0, keepdims=True)
        vt = vt_ref[j, kc:kc + LANES, :]
        acc = alpha * acc + jnp.dot(vt, p.astype(_BF), preferred_element_type=_F32)
        return m_new, l, acc

    def body(jj, carry):
        c = list(carry)
        for sub in range(2):
            for hh in range(2):
                c[hh] = step(2 * jj + sub, hh, c[hh], sub)
        return tuple(c)

    for hh in range(2):
        s_scr[hh, 0] = scores(0, hh)
    init1 = (jnp.full((1, tq), NEG, _F32), jnp.zeros((1, tq), _F32), jnp.zeros((LANES, tq), _F32))
    res = lax.fori_loop(0, nk // 2, body, (init1, init1))
    halves = []
    for (_, l, acc) in res:
        o = acc / l
        if shared_kv:
            hi = (pl.program_id(1) // 2) == 1
            halves.append(jnp.where(hi, o[64:, :], o[:64, :]))
        else:
            halves.append(o[64:, :])
    o_ref[...] = jnp.concatenate(halves, axis=0).T.astype(o_ref.dtype)


def _attention(q, k, vt, *, q_off, k_off, shared_kv, tq):
    B, S, _ = q.shape
    nk, _, tk = vt.shape[1:]
    assert nk % 2 == 0
    npairs = 4
    if shared_kv:
        kspec = pl.BlockSpec((None, S, LANES), lambda b, p, i: (b, 0, k_off + p // 2))
        vspec = pl.BlockSpec((None, nk, LANES, tk), lambda b, p, i: (b, 0, k_off, 0))
    else:
        kspec = pl.BlockSpec((None, S, 2 * LANES), lambda b, p, i: (b, 0, k_off // 2 + p))
        vspec = pl.BlockSpec((None, nk, 2 * LANES, tk), lambda b, p, i: (b, 0, k_off // 2 + p, 0))
    return pl.pallas_call(
        functools.partial(_attn_kernel, shared_kv=shared_kv),
        grid=(B, npairs, S // tq),
        in_specs=[pl.BlockSpec((None, tq, 2 * LANES), lambda b, p, i: (b, i, q_off // 2 + p)),
                  kspec, vspec],
        out_specs=pl.BlockSpec((None, tq, LANES), lambda b, p, i: (b, i, p)),
        out_shape=jax.ShapeDtypeStruct((B, S, npairs * LANES), _BF),
        scratch_shapes=[pltpu.VMEM((2, 2, tk, tq), _F32)],
        compiler_params=_cparams(("parallel", "parallel", "parallel")),
        name="attn_gqa" if shared_kv else "attn_mla",
    )(q, k, vt)


def _oproj_kernel(*refs, n_in, route):
    x_ref, mod_ref, g_ref, w_ref = refs[0], refs[1], refs[2], refs[3]
    o_refs = refs[4:4 + n_in]
    pos = 4 + n_in
    if route:
        r_ref = refs[pos]
        pos += 1
    x_out, h_out = refs[pos], refs[pos + 1]
    y = None
    off = 0
    for o_ref in o_refs:
        if len(o_ref.shape) == 3:
            o = jnp.concatenate([o_ref[p] for p in range(o_ref.shape[0])], axis=1)
        else:
            o = o_ref[...]
        w = o.shape[1]
        t = jnp.dot(o, w_ref[off:off + w, :], preferred_element_type=_F32)
        y = t if y is None else y + t
        off += w
    x1 = x_ref[...] + mod_ref[2:3, :] * y
    x_out[...] = x1
    h = _modulate(x1, g_ref[...], mod_ref[3:4, :], mod_ref[4:5, :])
    if route:
        _rows_to_tiles(h_out, h)
    else:
        h_out[...] = h.astype(h_out.dtype)
    if route:
        route_out = refs[pos + 2]
        hh = h.astype(_BF)
        hl = (h - hh.astype(_F32)).astype(_BF)
        rh, rl = r_ref[0], r_ref[1]
        logits = (jnp.dot(hh, rh, preferred_element_type=_F32)
                  + jnp.dot(hh, rl, preferred_element_type=_F32)
                  + jnp.dot(hl, rh, preferred_element_type=_F32))
        tm = logits.shape[0]
        lane = lax.broadcasted_iota(jnp.int32, (tm, LANES), 1)
        lg = jnp.where(lane < N_EXPERTS, logits, NEG)
        m1 = jnp.max(lg, axis=-1, keepdims=True)
        lanef = lane.astype(_F32)
        i1 = jnp.min(jnp.where(lg == m1, lanef, float(LANES)), axis=-1, keepdims=True)
        lg2 = jnp.where(lanef == i1, NEG, lg)
        m2 = jnp.max(lg2, axis=-1, keepdims=True)
        i2 = jnp.min(jnp.where(lg2 == m2, lanef, float(LANES)), axis=-1, keepdims=True)
        e = jnp.exp(m2 - m1)
        g1 = 1.0 / (1.0 + e)
        g2 = e / (1.0 + e)
        route_out[...] = jnp.where(lane == 0, i1,
                                   jnp.where(lane == 1, i2,
                                             jnp.where(lane == 2, g1, jnp.where(lane == 3, g2, 0.0))))


def _oproj(x, mod, g, w, o_list, tm, h_dtype, router=None):
    B, S, D = x.shape
    n_in = len(o_list)
    route = router is not None
    tok = lambda wd: pl.BlockSpec((None, tm, wd), lambda b, i: (b, i, 0))
    in_specs = [tok(D), pl.BlockSpec((None, 6, D), lambda b, i: (b, 0, 0)),
                pl.BlockSpec((1, D), lambda b, i: (0, 0)),
                pl.BlockSpec(w.shape, lambda b, i: (0, 0))]
    for o in o_list:
        if o.ndim == 4:
            in_specs.append(pl.BlockSpec((None, o.shape[1], tm, LANES), lambda b, i: (b, 0, i, 0)))
        else:
            in_specs.append(tok(o.shape[2]))
    args = [x, mod, g, w] + list(o_list)
    out_specs = [tok(D), tok(D)]
    out_shape = [jax.ShapeDtypeStruct((B, S, D), _F32), jax.ShapeDtypeStruct((B, S, D), h_dtype)]
    if route:
        assert D == SUBLANES * LANES
        per_b = S // tm
        out_specs[1] = pl.BlockSpec((tm * SUBLANES, LANES), lambda b, i: (b * per_b + i, 0))
        out_shape[1] = jax.ShapeDtypeStruct((B * S * SUBLANES, LANES), _F32)
        in_specs.append(pl.BlockSpec(router.shape, lambda b, i: (0, 0, 0)))
        args.append(router)
        out_specs.append(tok(LANES))
        out_shape.append(jax.ShapeDtypeStruct((B, S, LANES), _F32))
    return pl.pallas_call(
        functools.partial(_oproj_kernel, n_in=n_in, route=route),
        grid=(B, S // tm),
        in_specs=in_specs, out_specs=out_specs, out_shape=out_shape,
        compiler_params=_cparams(("parallel", "parallel")),
        name="oproj_route" if route else "oproj",
    )(*args)


def _swiglu_hidden(x, wg_ref, wu_ref, a_scr, tf):
    F = wg_ref.shape[1]
    for c0 in range(0, F, tf):
        c1 = min(c0 + tf, F)
        g = jnp.dot(x, wg_ref[:, c0:c1], preferred_element_type=_F32)
        u = jnp.dot(x, wu_ref[:, c0:c1], preferred_element_type=_F32)
        a_scr[:, c0:c1] = (_silu(g) * u).astype(_BF)


def _ffn_kernel(x_ref, h_ref, mod_ref, modn_ref, gn_ref, wg_ref, wu_ref, wd_ref, x_out, h_out, a_scr, *, tf):
    _swiglu_hidden(h_ref[...], wg_ref, wu_ref, a_scr, tf)
    y = jnp.dot(a_scr[...], wd_ref[...], preferred_element_type=_F32)
    x2 = x_ref[...] + mod_ref[5:6, :] * y
    x_out[...] = x2
    h_out[...] = _modulate(x2, gn_ref[...], modn_ref[0:1, :], modn_ref[1:2, :]).astype(h_out.dtype)


def _ffn(x, h, mod, modn, gn, wg, wu, wd, tm, tf):
    B, S, D = x.shape
    T = B * S
    F = wg.shape[1]
    per_b = S // tm
    x2 = x.reshape(T, D)
    h2 = h.reshape(T, D)
    tok = pl.BlockSpec((tm, D), lambda i: (i, 0))
    modspec = pl.BlockSpec((None, 6, D), lambda i: (i // per_b, 0, 0))
    once = pl.Buffered(1)
    xo, ho = pl.pallas_call(
        functools.partial(_ffn_kernel, tf=tf),
        grid=(T // tm,),
        in_specs=[tok, tok, modspec, modspec, pl.BlockSpec((1, D), lambda i: (0, 0)),
                  pl.BlockSpec((D, F), lambda i: (0, 0), pipeline_mode=once),
                  pl.BlockSpec((D, F), lambda i: (0, 0), pipeline_mode=once),
                  pl.BlockSpec((F, D), lambda i: (0, 0), pipeline_mode=once)],
        out_specs=[tok, tok],
        out_shape=[jax.ShapeDtypeStruct((T, D), _F32), jax.ShapeDtypeStruct((T, D), _BF)],
        scratch_shapes=[pltpu.VMEM((tm, F), _BF)],
        compiler_params=_cparams(("parallel",)),
        name="ffn_dense",
    )(x2, h2, mod, modn, gn, wg, wu, wd)
    return xo.reshape(B, S, D), ho.reshape(B, S, D)


def _qkv_kernel(h_ref, w_ref, gains_ref, rope_ref, q_ref, k_ref, v_ref):
    tm, D = h_ref.shape
    y = jnp.dot(h_ref[...], w_ref[...], preferred_element_type=_F32)
    lane = lax.broadcasted_iota(jnp.int32, (tm, LANES), 1)
    lo = lane < 64
    c, sa, sb = rope_ref[:, 0:128], rope_ref[:, 128:256], rope_ref[:, 256:384]

    def head_pair(v, g):
        sq = v * v
        s_lo = jnp.sum(jnp.where(lo, sq, 0.0), axis=-1, keepdims=True)
        s_hi = jnp.sum(jnp.where(lo, 0.0, sq), axis=-1, keepdims=True)
        r = lax.rsqrt(jnp.where(lo, s_lo, s_hi) * (1.0 / HEAD_DIM) + EPS)
        return _rope(v * r * g, c, sa, sb, 32)

    for p in range(D // LANES):
        sl = slice(p * LANES, (p + 1) * LANES)
        q_ref[p] = head_pair(y[:, sl], gains_ref[0:1, :]).astype(_BF)
        k_ref[p] = head_pair(y[:, D + p * LANES:D + (p + 1) * LANES], gains_ref[1:2, :]).astype(_BF)
        v_ref[p] = y[:, 2 * D + p * LANES:2 * D + (p + 1) * LANES].astype(_BF)


def _qkv(h, w, gains, rope, tm):
    B, S, D = h.shape
    tok = pl.BlockSpec((None, tm, D), lambda b, i: (b, i, 0))
    hp = pl.BlockSpec((None, D // LANES, tm, LANES), lambda b, i: (b, 0, i, 0))
    sds = jax.ShapeDtypeStruct((B, D // LANES, S, LANES), _BF)
    return pl.pallas_call(
        _qkv_kernel,
        grid=(B, S // tm),
        in_specs=[tok, pl.BlockSpec(w.shape, lambda b, i: (0, 0)),
                  pl.BlockSpec((2, LANES), lambda b, i: (0, 0)),
                  pl.BlockSpec((tm, 3 * LANES), lambda b, i: (i, 0))],
        out_specs=[hp, hp, hp],
        out_shape=[sds, sds, sds],
        compiler_params=_cparams(("parallel", "parallel")),
        name="qkv_dil",
    )(h, w, gains, rope)


DIL_GROUP = 4


def _dil_kernel(q_ref, k_ref, v_ref, o_ref, tmp, qp, kp, vp, acc, mm, ll, accp, mmp, llp, *, tl):
    S = q_ref.shape[0]
    ng = S // DIL_GROUP
    pitch = DIL_GROUP + 1

    def spread(dst, src):
        for j in range(DIL_GROUP):
            dst[pl.ds(j, ng, stride=pitch), :] = src[pl.ds(j, ng, stride=DIL_GROUP), :]

    for src, dst in ((q_ref, qp), (k_ref, kp), (v_ref, vp)):
        tmp[...] = src[...].astype(_F32)
        spread(dst, tmp)
    lo = lax.broadcasted_iota(jnp.int32, (tl, LANES), 1) < 64
    for bi, (window, d) in enumerate(DIL_PATTERNS):
        L = S // d
        half = window // (2 * d)
        t = min(tl, L)
        W = min(t + 2 * half, L)
        nt = L // t
        if d > 1:
            assert d % DIL_GROUP == 0
        sd = d * pitch // DIL_GROUP

        def rows(r, first, n, d=d, sd=sd):
            if d == 1:
                return pl.ds(first, n)
            return pl.ds(r + r // DIL_GROUP + sd * first, n, stride=sd)

        def body(idx, carry, bi=bi, d=d, L=L, half=half, t=t, W=W, nt=nt, rows=rows):
            r = idx // nt
            l0 = (idx % nt) * t
            start = jnp.clip(l0 - half, 0, L - W)
            if d == 1:
                qt = q_ref[pl.ds(pl.multiple_of(l0, t), t), :]
                kw = k_ref[pl.ds(pl.multiple_of(start, 64), W), :]
                vw = v_ref[pl.ds(pl.multiple_of(start, 64), W), :]
            else:
                qt = qp[rows(r, l0, t), :].astype(_BF)
                kw = kp[rows(r, start, W), :].astype(_BF)
                vw = vp[rows(r, start, W), :].astype(_BF)
            lo_t = lo[:t]
            zq = jnp.zeros_like(qt)
            q2 = jnp.concatenate([jnp.where(lo_t, qt, zq), jnp.where(lo_t, zq, qt)], axis=0)
            s = lax.dot_general(q2, kw, (((1,), (1,)), ((), ())), preferred_element_type=_F32)
            qpos = l0 + lax.broadcasted_iota(jnp.int32, (t, W), 0)
            kpos = start + lax.broadcasted_iota(jnp.int32, (t, W), 1)
            ok = jnp.abs(kpos - qpos) <= half
            s = jnp.where(jnp.concatenate([ok, ok], axis=0), s, NEG)
            m = jnp.max(s, axis=-1, keepdims=True)
            p = jnp.exp2(s - m)
            den = jnp.sum(p, axis=-1, keepdims=True)
            o2 = jnp.dot(p.astype(_BF), vw, preferred_element_type=_F32)
            o_new = jnp.where(lo_t, o2[:t], o2[t:])
            m_new = jnp.where(lo_t, m[:t], m[t:])
            l_new = jnp.where(lo_t, den[:t], den[t:])
            tok = rows(r, l0, t)
            if d == 1:
                assert bi == 0
                acc[tok, :] = o_new
                mm[tok, :] = m_new
                ll[tok, :] = l_new
            else:
                m_old = mmp[tok, :]
                mx = jnp.maximum(m_old, m_new)
                a_old = jnp.exp2(m_old - mx)
                a_new = jnp.exp2(m_new - mx)
                accp[tok, :] = accp[tok, :] * a_old + o_new * a_new
                llp[tok, :] = llp[tok, :] * a_old + l_new * a_new
                mmp[tok, :] = mx
            return carry

        lax.fori_loop(0, d * nt, body, 0, unroll=min(4, d * nt))
        if bi == 0:
            spread(accp, acc)
            spread(mmp, mm)
            spread(llp, ll)
    for j in range(DIL_GROUP):
        grp = pl.ds(j, ng, stride=pitch)
        tmp[pl.ds(j, ng, stride=DIL_GROUP), :] = accp[grp, :] / llp[grp, :]
    o_ref[...] = tmp[...].astype(o_ref.dtype)


def _dilated(q, k, v, tl):
    B, P, S, _ = q.shape
    spec = pl.BlockSpec((None, None, S, LANES), lambda b, p: (b, p, 0, 0))
    scr = pltpu.VMEM((S, LANES), _F32)
    scrp = pltpu.VMEM((S // DIL_GROUP * (DIL_GROUP + 1), LANES), _F32)
    return pl.pallas_call(
        functools.partial(_dil_kernel, tl=tl),
        grid=(B, P),
        in_specs=[spec, spec, spec],
        out_specs=spec,
        out_shape=jax.ShapeDtypeStruct((B, P, S, LANES), _BF),
        scratch_shapes=[scr, scrp, scrp, scrp, scr, scr, scr, scrp, scrp, scrp],
        compiler_params=_cparams(("parallel", "parallel")),
        name="dil_attn",
    )(q, k, v)


def _moe_kernel(te_ref, tv_ref, src_hbm, dst_hbm, h_hbm, wg_ref, wu_ref, wd_ref, out_hbm,
                xbuf, xbf, a_scr, ybuf, src_s, dst_s, gsem, ssem, isem, *, tf, dump0, n_dump):
    i = pl.program_id(0)
    nt = pl.num_programs(0)
    tm = xbf.shape[0]
    slot = i % 2

    def is_valid(t):
        return jnp.logical_and(jnp.logical_and(t >= 0, t < nt), tv_ref[jnp.clip(t, 0, nt - 1)] == 1)

    valid = is_valid(i)
    prev_valid = is_valid(i - 1)
    nxt_valid = is_valid(i + 1)

    def src_copy(tile):
        return pltpu.make_async_copy(src_hbm.at[tile], src_s.at[tile % 2], isem.at[tile % 2])

    def dst_copy(tile):
        return pltpu.make_async_copy(dst_hbm.at[tile], dst_s, isem.at[2])

    def tile_at(first):
        return pl.ds(pl.multiple_of(first, SUBLANES), SUBLANES)

    def gather_row(r, slot_):
        return pltpu.make_async_copy(h_hbm.at[tile_at(src_s[slot_, r])],
                                     xbuf.at[slot_, tile_at(r * SUBLANES)], gsem.at[slot_])

    def scatter_row(r):
        return pltpu.make_async_copy(ybuf.at[tile_at(r * SUBLANES)], out_hbm.at[tile_at(dst_s[r])],
                                     ssem.at[0])

    def gather_wait(slot_):
        return pltpu.make_async_copy(h_hbm.at[pl.ds(0, tm * SUBLANES)], xbuf.at[slot_], gsem.at[slot_])

    def scatter_wait():
        return pltpu.make_async_copy(ybuf, out_hbm.at[pl.ds(0, tm * SUBLANES)], ssem.at[0])

    def for_rows(fn):
        def body(r, c):
            fn(r)
            return c
        lax.fori_loop(0, tm, body, 0, unroll=8)

    @pl.when(i == 0)
    def _():
        ybuf[...] = jnp.zeros_like(ybuf)
        for k in range(n_dump):
            pltpu.make_async_copy(ybuf, out_hbm.at[pl.ds((dump0 + k * tm) * SUBLANES, tm * SUBLANES)],
                                  ssem.at[0]).start()
        for k in range(n_dump):
            scatter_wait().wait()

    @pl.when(jnp.logical_and(i == 0, valid))
    def _():
        src_copy(0).start()
        src_copy(0).wait()
        for_rows(lambda r: gather_row(r, 0).start())

    @pl.when(jnp.logical_and(i == 0, nxt_valid))
    def _():
        src_copy(1).start()

    @pl.when(valid)
    def _():
        dst_copy(i).start()

    @pl.when(nxt_valid)
    def _():
        src_copy(i + 1).wait()
        for_rows(lambda r: gather_row(r, 1 - slot).start())

    for par in range(2):
        @pl.when(jnp.logical_and(valid, slot == par))
        def _(par=par):
            gather_wait(par).wait()
            xbf[...] = _tiles_to_rows(xbuf.at[par], tm).astype(_BF)

    @pl.when(valid)
    def _():
        _swiglu_hidden(xbf[...], wg_ref, wu_ref, a_scr, tf)

    @pl.when(prev_valid)
    def _():
        scatter_wait().wait()

    @pl.when(valid)
    def _():
        _rows_to_tiles(ybuf, jnp.dot(a_scr[...], wd_ref[...], preferred_element_type=_F32))
        dst_copy(i).wait()
        for_rows(lambda r: scatter_row(r).start())

    @pl.when(is_valid(i + 2))
    def _():
        src_copy(i + 2).start()

    @pl.when(jnp.logical_and(i == nt - 1, valid))
    def _():
        scatter_wait().wait()


def _moe(h, tile_e, tile_v, src, dst, wg, wu, wd, n_out_rows, tm, tf):
    T = h.shape[0] // SUBLANES
    D = wg.shape[1]
    nt = tile_e.shape[0]
    F = wg.shape[2]
    dump0 = T * TOP_K
    once = pl.Buffered(1)
    grid_spec = pltpu.PrefetchScalarGridSpec(
        num_scalar_prefetch=2,
        grid=(nt,),
        in_specs=[pl.BlockSpec(memory_space=pl.ANY), pl.BlockSpec(memory_space=pl.ANY),
                  pl.BlockSpec(memory_space=pl.ANY),
                  pl.BlockSpec((None, D, F), lambda i, te, tv: (te[i], 0, 0), pipeline_mode=once),
                  pl.BlockSpec((None, D, F), lambda i, te, tv: (te[i], 0, 0), pipeline_mode=once),
                  pl.BlockSpec((None, F, D), lambda i, te, tv: (te[i], 0, 0), pipeline_mode=once)],
        out_specs=pl.BlockSpec(memory_space=pl.ANY),
        scratch_shapes=[pltpu.VMEM((2, tm * SUBLANES, LANES), _F32), pltpu.VMEM((tm, D), _BF),
                        pltpu.VMEM((tm, F), _BF), pltpu.VMEM((tm * SUBLANES, LANES), _F32),
                        pltpu.SMEM((2, tm), jnp.int32), pltpu.SMEM((tm,), jnp.int32),
                        pltpu.SemaphoreType.DMA((2,)), pltpu.SemaphoreType.DMA((1,)),
                        pltpu.SemaphoreType.DMA((3,))],
    )
    return pl.pallas_call(
        functools.partial(_moe_kernel, tf=tf, dump0=dump0, n_dump=(n_out_rows - dump0) // tm),
        grid_spec=grid_spec,
        out_shape=jax.ShapeDtypeStruct((n_out_rows * SUBLANES, LANES), _F32),
        compiler_params=_cparams(("arbitrary",)),
        name="moe_experts",
    )(tile_e, tile_v, src, dst, h, wg, wu, wd)


def _route_plan(route, T, tm):
    A = T * TOP_K
    e_flat = route[:, :TOP_K].astype(jnp.int32).reshape(A)
    order = jnp.argsort(e_flat, stable=True).astype(jnp.int32)
    counts = jnp.sum((e_flat[:, None] == jnp.arange(N_EXPERTS, dtype=jnp.int32)[None, :]).astype(jnp.int32), axis=0)
    starts = jnp.cumsum(counts) - counts
    pcounts = (counts + tm - 1) // tm * tm
    pends = jnp.cumsum(pcounts)
    pstarts = pends - pcounts
    nt = A // tm + N_EXPERTS
    tile0 = jnp.arange(nt, dtype=jnp.int32) * tm
    tile_v = (tile0 < pends[-1]).astype(jnp.int32)
    last_valid = jnp.maximum(pends[-1] // tm - 1, 0)
    tile_e_raw = jnp.minimum(jnp.searchsorted(pends, tile0, side="right"), N_EXPERTS - 1).astype(jnp.int32)
    tile_e = jnp.where(tile_v == 1, tile_e_raw, tile_e_raw[last_valid])
    r = jnp.arange(nt * tm, dtype=jnp.int32)
    e_r = jnp.repeat(tile_e, tm)
    within = r - pstarts[e_r]
    ok = jnp.logical_and(within < counts[e_r], jnp.repeat(tile_v, tm) == 1)
    a = order[jnp.clip(starts[e_r] + within, 0, A - 1)]
    src = jnp.where(ok, a // TOP_K, 0).astype(jnp.int32)
    dump = A + e_r * tm + jnp.clip(within - counts[e_r], 0, tm - 1)
    dst = jnp.where(ok, (a % TOP_K) * T + a // TOP_K, dump).astype(jnp.int32)
    return (tile_e, tile_v, (src * SUBLANES).reshape(nt, tm), (dst * SUBLANES).reshape(nt, tm),
            A + N_EXPERTS * tm)


def _combine_kernel(x_ref, y1_ref, y2_ref, r_ref, mod_ref, o_ref):
    tm = x_ref.shape[0]
    y = _tiles_to_rows(y1_ref, tm) * r_ref[:, 2:3] + _tiles_to_rows(y2_ref, tm) * r_ref[:, 3:4]
    o_ref[...] = x_ref[...] + mod_ref[5:6, :] * y


def _combine(x, y2, route, mod, tm):
    B, S, D = x.shape
    T = B * S
    per_b = S // tm
    out = pl.pallas_call(
        _combine_kernel,
        grid=(T // tm,),
        in_specs=[pl.BlockSpec((tm, D), lambda i: (i, 0)),
                  pl.BlockSpec((tm * SUBLANES, LANES), lambda i: (i, 0)),
                  pl.BlockSpec((tm * SUBLANES, LANES), lambda i: (T // tm + i, 0)),
                  pl.BlockSpec((tm, LANES), lambda i: (i, 0)),
                  pl.BlockSpec((None, 6, D), lambda i: (i // per_b, 0, 0))],
        out_specs=pl.BlockSpec((tm, D), lambda i: (i, 0)),
        out_shape=jax.ShapeDtypeStruct((T, D), _F32),
        compiler_params=_cparams(("parallel",)),
        name="moe_combine",
    )(x.reshape(T, D), y2, y2, route.reshape(T, LANES), mod)
    return out.reshape(B, S, D)


def _pad_cols(a, w):
    return jnp.pad(a, ((0, 0), (0, w - a.shape[1])))


def _rope_tables_even(S):
    pos = jnp.arange(S, dtype=jnp.int32)
    inv = ROPE_THETA ** (-jnp.arange(0, 32, 2, dtype=_F32) / 32)
    def cs(p):
        ang = p.astype(_F32)[:, None] * inv[None, :]
        return jnp.cos(ang), jnp.sin(ang)
    one = lambda w: jnp.ones((S, w), _F32)
    zero = lambda w: jnp.zeros((S, w), _F32)
    c, s = cs(pos)
    ca = jnp.concatenate([one(64), c, c, one(32)], 1)
    saa = jnp.concatenate([zero(64), -s, zero(16), zero(32)], 1)
    sba = jnp.concatenate([zero(64), zero(16), s, zero(32)], 1)
    cr, sr = cs(pos // GRID_W)
    cc, sc = cs(pos % GRID_W)
    cb = jnp.concatenate([cr, cr, cc, cc, one(64)], 1)
    sab = jnp.concatenate([-sr, zero(16), -sc, zero(16), zero(64)], 1)
    sbb = jnp.concatenate([zero(16), sr, zero(16), sc, zero(64)], 1)
    return jnp.concatenate([ca, saa, sba, cb, sab, sbb], 1)


def _rope_tables_odd(S):
    pos = jnp.arange(S, dtype=_F32)
    inv = ROPE_THETA ** (-jnp.arange(0, HEAD_DIM, 2, dtype=_F32) / HEAD_DIM)
    ang = pos[:, None] * inv[None, :]
    c, s = jnp.cos(ang), jnp.sin(ang)
    z = jnp.zeros_like(s)
    return jnp.concatenate([c, c, c, c, -s, z, -s, z, z, s, z, s], 1)


def _tiles(S):
    return dict(tm_pre=min(S, 512), tq=min(S, 1024), tm_o=min(S, 512),
                tm_ffn=min(S, 512), tf_ffn=512, tm_qkv=min(S, 512),
                tl=128, tm_moe=min(S, 512), tf_moe=512, tm_c=min(S, 512))


def kernel(x, c, ada_even_w, ada_even_b, norm_even_mix, norm_even_ffn, even_w_in, mla_q_norm, mla_w_uq, mla_kv_norm, mla_w_ukv, mla_q_gain, mla_k_gain, gqa_q_gain, gqa_k_gain, even_w_out, ffn_w_gate, ffn_w_up, ffn_w_down, ada_odd_w, ada_odd_b, norm_odd_mix, norm_odd_ffn, dil_w_qkv, dil_q_gain, dil_k_gain, dil_w_out, moe_router, moe_w_gate, moe_w_up, moe_w_down):
    B, S, D = x.shape
    T = B * S
    cfg = _tiles(S)

    mod_e = _ada_mod(c, ada_even_w[0], ada_even_b[0]).reshape(B, 6, D)
    mod_o = _ada_mod(c, ada_odd_w[0], ada_odd_b[0]).reshape(B, 6, D)

    w = even_w_in[0]
    sp = [MLA_Q_RANK, MLA_Q_RANK + MLA_KV_RANK, MLA_Q_RANK + MLA_KV_RANK + MLA_ROPE]
    sp.append(sp[-1] + GQA_HEADS * HEAD_DIM)
    sp.append(sp[-1] + GQA_KV_HEADS * HEAD_DIM)
    w_cq, w_ckv, w_kpe = w[:, :sp[0]], w[:, sp[0]:sp[1]], w[:, sp[1]:sp[2]]
    w_qb, w_kb, w_vb = w[:, sp[2]:sp[3]], w[:, sp[3]:sp[4]], w[:, sp[4]:]
    zc = lambda n: jnp.zeros((D, n), _F32)
    slot64 = lambda a, n: jnp.pad(a.reshape(D, n, HEAD_DIM), ((0, 0), (0, 0), (0, LANES - HEAD_DIM))).reshape(D, n * LANES)
    w_in = jnp.concatenate([w_cq, w_ckv, zc(64), w_kpe, zc(32), slot64(w_qb, GQA_HEADS),
                            slot64(w_kb, GQA_KV_HEADS), w_vb], axis=1).astype(_BF)
    na = MLA_NOPE + MLA_ROPE
    w_uq = jnp.pad(mla_w_uq[0].reshape(MLA_Q_RANK, MLA_HEADS, na),
                   ((0, 0), (0, 0), (0, LANES - na))).reshape(MLA_Q_RANK, MLA_HEADS * LANES).astype(_BF)
    w_ukv = mla_w_ukv[0].astype(_BF)
    pad128 = lambda v: jnp.pad(v, (0, LANES - v.shape[0]))
    gains_e = jnp.stack([pad128(mla_q_gain[0]) * (na ** -0.5 * LOG2E), pad128(mla_k_gain[0]),
                         pad128(gqa_q_gain[0]) * (HEAD_DIM ** -0.5 * LOG2E), pad128(gqa_k_gain[0])])
    rope_e = _rope_tables_even(S)
    q_all, k_all, vt_all = _pre_even(
        x, mod_e, norm_even_mix[0].reshape(1, D), w_in, mla_q_norm[0].reshape(1, -1), w_uq,
        mla_kv_norm[0].reshape(1, -1), w_ukv, gains_e, rope_e, cfg["tm_pre"])
    o_a = _attention(q_all, k_all, vt_all, q_off=0, k_off=0, shared_kv=False, tq=cfg["tq"])
    o_b = _attention(q_all, k_all, vt_all, q_off=MLA_HEADS, k_off=MLA_HEADS, shared_kv=True, tq=cfg["tq"])
    x1, h1 = _oproj(x, mod_e, norm_even_ffn[0].reshape(1, D), even_w_out[0].astype(_BF), [o_a, o_b],
                    cfg["tm_o"], _BF)
    x2, h2 = _ffn(x1, h1, mod_e, mod_o, norm_odd_mix[0].reshape(1, D), ffn_w_gate[0].astype(_BF),
                  ffn_w_up[0].astype(_BF), ffn_w_down[0].astype(_BF), cfg["tm_ffn"], cfg["tf_ffn"])

    pair = lambda v: jnp.concatenate([v, v])
    gains_o = jnp.stack([pair(dil_q_gain[0]) * (HEAD_DIM ** -0.5 * LOG2E), pair(dil_k_gain[0])])
    qd, kd, vd = _qkv(h2, dil_w_qkv[0].astype(_BF), gains_o, _rope_tables_odd(S), cfg["tm_qkv"])
    o_d = _dilated(qd, kd, vd, cfg["tl"])
    r32 = _pad_cols(moe_router[0], LANES)
    r_hi = r32.astype(_BF)
    r_lo = (r32 - r_hi.astype(_F32)).astype(_BF)
    x3, h3, route = _oproj(x2, mod_o, norm_odd_ffn[0].reshape(1, D), dil_w_out[0].astype(_BF), [o_d],
                           cfg["tm_o"], _F32, router=jnp.stack([r_hi, r_lo]))
    tm = cfg["tm_moe"]
    tile_e, tile_v, src, dst, n_rows = _route_plan(route.reshape(T, LANES), T, tm)
    y2 = _moe(h3, tile_e, tile_v, src, dst, moe_w_gate[0].astype(_BF),
              moe_w_up[0].astype(_BF), moe_w_down[0].astype(_BF), n_rows, tm, cfg["tf_moe"])
    return _combine(x3, y2, route, mod_o, cfg["tm_c"])
```

```python
import functools
import math

import jax
import jax.numpy as jnp
from jax import lax
from jax.experimental import pallas as pl
from jax.experimental.pallas import tpu as pltpu

_BF = jnp.bfloat16
_F32 = jnp.float32

GRID_W = 64
HEAD_DIM = 64
ROPE_THETA = 10000.0
EPS = 1e-6
MLA_HEADS = 8
MLA_Q_RANK = 256
MLA_KV_RANK = 128
MLA_NOPE = 64
MLA_ROPE = 32
MLA_V = 64
GQA_HEADS = 8
GQA_KV_HEADS = 2
DIL_PATTERNS = ((128, 1), (512, 4), (2048, 16))
N_EXPERTS = 8
TOP_K = 2
NEG = -1e30
LOG2E = math.log2(math.e)

LANES = 128
VMEM_LIMIT = 56 * 1024 * 1024


def _cparams(sem, vmem=VMEM_LIMIT):
    return pltpu.CompilerParams(dimension_semantics=sem, vmem_limit_bytes=vmem)


def _silu(x):
    return x / (1.0 + jnp.exp(-x))


def _modulate(x, g, shift, scale):
    ms = jnp.mean(x * x, axis=-1, keepdims=True)
    return x * lax.rsqrt(ms + EPS) * g * (1.0 + scale) + shift


def _norm(v, n, g):
    return v * lax.rsqrt(jnp.sum(v * v, axis=-1, keepdims=True) * (1.0 / n) + EPS) * g


SUBLANES = 8


def _rows_to_tiles(ref, x):
    tm = x.shape[0]
    for c in range(SUBLANES):
        ref[pl.ds(c, tm, stride=SUBLANES), :] = x[:, c * LANES:(c + 1) * LANES]


def _tiles_to_rows(ref, tm):
    return jnp.concatenate([ref[pl.ds(c, tm, stride=SUBLANES), :] for c in range(SUBLANES)], axis=1)


def _rope(x, c, sa, sb, k):
    return x * c + pltpu.roll(x, LANES - k, 1) * sa + pltpu.roll(x, k, 1) * sb


def _mod_kernel(c_ref, w_ref, b_ref, o_ref):
    sc = _silu(c_ref[...])
    o_ref[...] = jnp.dot(sc.astype(_BF), w_ref[...].astype(_BF),
                         preferred_element_type=_F32) + b_ref[...]


def _ada_mod(c, w, b):
    B, D = c.shape
    N = w.shape[1]
    tn = min(N, 1536)
    return pl.pallas_call(
        _mod_kernel,
        grid=(N // tn,),
        in_specs=[pl.BlockSpec((B, D), lambda j: (0, 0)),
                  pl.BlockSpec((D, tn), lambda j: (0, j)),
                  pl.BlockSpec((1, tn), lambda j: (0, j))],
        out_specs=pl.BlockSpec((B, tn), lambda j: (0, j)),
        out_shape=jax.ShapeDtypeStruct((B, N), _F32),
        compiler_params=_cparams(("arbitrary",)),
        name="ada_mod",
    )(c, w, b.reshape(1, N))


def _pre_even_kernel(x_ref, mod_ref, g_ref, w_in_ref, qn_ref, w_uq_ref, kvn_ref, w_ukv_ref,
                     gains_ref, rope_ref, q_ref, k_ref, vt_ref):
    tm = x_ref.shape[0]
    h = _modulate(x_ref[...], g_ref[...], mod_ref[0:1, :], mod_ref[1:2, :]).astype(_BF)
    y = jnp.dot(h, w_in_ref[...], preferred_element_type=_F32)
    cqn = _norm(y[:, 0:256], MLA_Q_RANK, qn_ref[...]).astype(_BF)
    qa = jnp.dot(cqn, w_uq_ref[...], preferred_element_type=_F32)
    ckvn = _norm(y[:, 256:384], MLA_KV_RANK, kvn_ref[...]).astype(_BF)
    kv = jnp.dot(ckvn, w_ukv_ref[...], preferred_element_type=_F32)
    kpe = y[:, 384:512]
    lo = lax.broadcasted_iota(jnp.int32, (tm, LANES), 1) < 64
    ca, saa, sba = rope_ref[:, 0:128], rope_ref[:, 128:256], rope_ref[:, 256:384]
    cb, sab, sbb = rope_ref[:, 384:512], rope_ref[:, 512:640], rope_ref[:, 640:768]
    gqa, gka = gains_ref[0:1, :], gains_ref[1:2, :]
    gqb, gkb = gains_ref[2:3, :], gains_ref[3:4, :]
    na = MLA_NOPE + MLA_ROPE
    for hh in range(MLA_HEADS):
        sl = slice(hh * LANES, (hh + 1) * LANES)
        q_ref[:, sl] = _rope(_norm(qa[:, sl], na, gqa), ca, saa, sba, 16).astype(_BF)
        kh = jnp.where(lo, kv[:, sl], kpe)
        k_ref[:, sl] = _rope(_norm(kh, na, gka), ca, saa, sba, 16).astype(_BF)
        vt_ref[sl, :] = kv[:, sl].T.astype(_BF)
    for hh in range(GQA_HEADS):
        src = slice(512 + hh * LANES, 512 + (hh + 1) * LANES)
        dst = slice((MLA_HEADS + hh) * LANES, (MLA_HEADS + hh + 1) * LANES)
        q_ref[:, dst] = _rope(_norm(y[:, src], HEAD_DIM, gqb), cb, sab, sbb, 16).astype(_BF)
    for g in range(GQA_KV_HEADS):
        src = slice(1536 + g * LANES, 1536 + (g + 1) * LANES)
        dst = slice((MLA_HEADS + g) * LANES, (MLA_HEADS + g + 1) * LANES)
        k_ref[:, dst] = _rope(_norm(y[:, src], HEAD_DIM, gkb), cb, sab, sbb, 16).astype(_BF)
    vt_ref[MLA_HEADS * LANES:(MLA_HEADS + 1) * LANES, :] = y[:, 1792:1920].T.astype(_BF)


def _pre_even(x, mod, g, w_in, qn, w_uq, kvn, w_ukv, gains, rope, tm):
    B, S, D = x.shape
    nin = w_in.shape[1]
    nq = (MLA_HEADS + GQA_HEADS) * LANES
    nk = (MLA_HEADS + GQA_KV_HEADS) * LANES
    nkv = MLA_HEADS * LANES
    nvt = (MLA_HEADS + 1) * LANES
    full = lambda shp: pl.BlockSpec(shp, lambda b, i: (0,) * len(shp))
    tok = lambda w: pl.BlockSpec((None, tm, w), lambda b, i: (b, i, 0))
    return pl.pallas_call(
        _pre_even_kernel,
        grid=(B, S // tm),
        in_specs=[tok(D),
                  pl.BlockSpec((None, 6, D), lambda b, i: (b, 0, 0)),
                  full((1, D)), full((D, nin)), full((1, MLA_Q_RANK)), full((MLA_Q_RANK, nkv)),
                  full((1, MLA_KV_RANK)), full((MLA_KV_RANK, nkv)), full((4, LANES)),
                  pl.BlockSpec((tm, 6 * LANES), lambda b, i: (i, 0))],
        out_specs=[tok(nq), tok(nk), pl.BlockSpec((None, None, nvt, tm), lambda b, i: (b, i, 0, 0))],
        out_shape=[jax.ShapeDtypeStruct((B, S, nq), _BF), jax.ShapeDtypeStruct((B, S, nk), _BF),
                   jax.ShapeDtypeStruct((B, S // tm, nvt, tm), _BF)],
        compiler_params=_cparams(("parallel", "parallel")),
        name="pre_even",
    )(x, mod, g, w_in, qn, w_uq, kvn, w_ukv, gains, rope)


def _attn_kernel(q_ref, k_ref, vt_ref, o_ref, s_scr, *, shared_kv):
    tq = q_ref.shape[0]
    nk, _, tk = vt_ref.shape
    qs = [q_ref[:, hh * LANES:(hh + 1) * LANES] for hh in range(2)]

    def scores(j, hh):
        r0 = pl.multiple_of(j * tk, tk)
        kc = 0 if shared_kv else hh * LANES
        k = k_ref[pl.ds(r0, tk), kc:kc + LANES]
        return lax.dot_general(k, qs[hh], (((1,), (1,)), ((), ())), preferred_element_type=_F32)

    def step(j, hh, carry, cur):
        m, l, acc = carry
        s_scr[hh, 1 - cur] = scores(jnp.minimum(j + 1, nk - 1), hh)
        st = s_scr[hh, cur]
        kc = 0 if shared_kv else hh * LANES
        m_new = jnp.maximum(m, jnp.max(st, axis=0, keepdims=True))
        alpha = jnp.exp2(m - m_new)
        p = jnp.exp2(st - m_new)
        l = alpha * l + jnp.sum(p, axis=0, keepdims=True)
        vt = vt_ref[j, kc:kc + LANES, :]
        acc = alpha * acc + jnp.dot(vt, p.astype(_BF), preferred_element_type=_F32)
        return m_new, l, acc

    def body(jj, carry):
        c = list(carry)
        for sub in range(2):
            for hh in range(2):
                c[hh] = step(2 * jj + sub, hh, c[hh], sub)
        return tuple(c)

    for hh in range(2):
        s_scr[hh, 0] = scores(0, hh)
    init1 = (jnp.full((1, tq), NEG, _F32), jnp.zeros((1, tq), _F32), jnp.zeros((LANES, tq), _F32))
    res = lax.fori_loop(0, nk // 2, body, (init1, init1))
    halves = []
    for (_, l, acc) in res:
        o = acc / l
        if shared_kv:
            hi = (pl.program_id(1) // 2) == 1
            halves.append(jnp.where(hi, o[64:, :], o[:64, :]))
        else:
            halves.append(o[64:, :])
    o_ref[...] = jnp.concatenate(halves, axis=0).T.astype(o_ref.dtype)


def _attention(q, k, vt, *, q_off, k_off, shared_kv, tq):
    B, S, _ = q.shape
    nk, _, tk = vt.shape[1:]
    assert nk % 2 == 0
    npairs = 4
    if shared_kv:
        kspec = pl.BlockSpec((None, S, LANES), lambda b, p, i: (b, 0, k_off + p // 2))
        vspec = pl.BlockSpec((None, nk, LANES, tk), lambda b, p, i: (b, 0, k_off, 0))
    else:
        kspec = pl.BlockSpec((None, S, 2 * LANES), lambda b, p, i: (b, 0, k_off // 2 + p))
        vspec = pl.BlockSpec((None, nk, 2 * LANES, tk), lambda b, p, i: (b, 0, k_off // 2 + p, 0))
    return pl.pallas_call(
        functools.partial(_attn_kernel, shared_kv=shared_kv),
        grid=(B, npairs, S // tq),
        in_specs=[pl.BlockSpec((None, tq, 2 * LANES), lambda b, p, i: (b, i, q_off // 2 + p)),
                  kspec, vspec],
        out_specs=pl.BlockSpec((None, tq, LANES), lambda b, p, i: (b, i, p)),
        out_shape=jax.ShapeDtypeStruct((B, S, npairs * LANES), _BF),
        scratch_shapes=[pltpu.VMEM((2, 2, tk, tq), _F32)],
        compiler_params=_cparams(("parallel", "parallel", "parallel")),
        name="attn_gqa" if shared_kv else "attn_mla",
    )(q, k, vt)


def _oproj_kernel(*refs, n_in, route):
    x_ref, mod_ref, g_ref, w_ref = refs[0], refs[1], refs[2], refs[3]
    o_refs = refs[4:4 + n_in]
    pos = 4 + n_in
    if route:
        r_ref = refs[pos]
        pos += 1
    x_out, h_out = refs[pos], refs[pos + 1]
    y = None
    off = 0
    for o_ref in o_refs:
        if len(o_ref.shape) == 3:
            o = jnp.concatenate([o_ref[p] for p in range(o_ref.shape[0])], axis=1)
        else:
            o = o_ref[...]
        w = o.shape[1]
        t = jnp.dot(o, w_ref[off:off + w, :], preferred_element_type=_F32)
        y = t if y is None else y + t
        off += w
    x1 = x_ref[...] + mod_ref[2:3, :] * y
    x_out[...] = x1
    h = _modulate(x1, g_ref[...], mod_ref[3:4, :], mod_ref[4:5, :])
    if route:
        _rows_to_tiles(h_out, h)
    else:
        h_out[...] = h.astype(h_out.dtype)
    if route:
        route_out = refs[pos + 2]
        hh = h.astype(_BF)
        hl = (h - hh.astype(_F32)).astype(_BF)
        rh, rl = r_ref[0], r_ref[1]
        logits = (jnp.dot(hh, rh, preferred_element_type=_F32)
                  + jnp.dot(hh, rl, preferred_element_type=_F32)
                  + jnp.dot(hl, rh, preferred_element_type=_F32))
        tm = logits.shape[0]
        lane = lax.broadcasted_iota(jnp.int32, (tm, LANES), 1)
        lg = jnp.where(lane < N_EXPERTS, logits, NEG)
        m1 = jnp.max(lg, axis=-1, keepdims=True)
        lanef = lane.astype(_F32)
        i1 = jnp.min(jnp.where(lg == m1, lanef, float(LANES)), axis=-1, keepdims=True)
        lg2 = jnp.where(lanef == i1, NEG, lg)
        m2 = jnp.max(lg2, axis=-1, keepdims=True)
        i2 = jnp.min(jnp.where(lg2 == m2, lanef, float(LANES)), axis=-1, keepdims=True)
        e = jnp.exp(m2 - m1)
        g1 = 1.0 / (1.0 + e)
        g2 = e / (1.0 + e)
        route_out[...] = jnp.where(lane == 0, i1,
                                   jnp.where(lane == 1, i2,
                                             jnp.where(lane == 2, g1, jnp.where(lane == 3, g2, 0.0))))


def _oproj(x, mod, g, w, o_list, tm, h_dtype, router=None):
    B, S, D = x.shape
    n_in = len(o_list)
    route = router is not None
    tok = lambda wd: pl.BlockSpec((None, tm, wd), lambda b, i: (b, i, 0))
    in_specs = [tok(D), pl.BlockSpec((None, 6, D), lambda b, i: (b, 0, 0)),
                pl.BlockSpec((1, D), lambda b, i: (0, 0)),
                pl.BlockSpec(w.shape, lambda b, i: (0, 0))]
    for o in o_list:
        if o.ndim == 4:
            in_specs.append(pl.BlockSpec((None, o.shape[1], tm, LANES), lambda b, i: (b, 0, i, 0)))
        else:
            in_specs.append(tok(o.shape[2]))
    args = [x, mod, g, w] + list(o_list)
    out_specs = [tok(D), tok(D)]
    out_shape = [jax.ShapeDtypeStruct((B, S, D), _F32), jax.ShapeDtypeStruct((B, S, D), h_dtype)]
    if route:
        assert D == SUBLANES * LANES
        per_b = S // tm
        out_specs[1] = pl.BlockSpec((tm * SUBLANES, LANES), lambda b, i: (b * per_b + i, 0))
        out_shape[1] = jax.ShapeDtypeStruct((B * S * SUBLANES, LANES), _F32)
        in_specs.append(pl.BlockSpec(router.shape, lambda b, i: (0, 0, 0)))
        args.append(router)
        out_specs.append(tok(LANES))
        out_shape.append(jax.ShapeDtypeStruct((B, S, LANES), _F32))
    return pl.pallas_call(
        functools.partial(_oproj_kernel, n_in=n_in, route=route),
        grid=(B, S // tm),
        in_specs=in_specs, out_specs=out_specs, out_shape=out_shape,
        compiler_params=_cparams(("parallel", "parallel")),
        name="oproj_route" if route else "oproj",
    )(*args)


def _swiglu_hidden(x, wg_ref, wu_ref, a_scr, tf):
    F = wg_ref.shape[1]
    for c0 in range(0, F, tf):
        c1 = min(c0 + tf, F)
        g = jnp.dot(x, wg_ref[:, c0:c1], preferred_element_type=_F32)
        u = jnp.dot(x, wu_ref[:, c0:c1], preferred_element_type=_F32)
        a_scr[:, c0:c1] = (_silu(g) * u).astype(_BF)


def _ffn_kernel(x_ref, h_ref, mod_ref, modn_ref, gn_ref, wg_ref, wu_ref, wd_ref, x_out, h_out, a_scr, *, tf):
    _swiglu_hidden(h_ref[...], wg_ref, wu_ref, a_scr, tf)
    y = jnp.dot(a_scr[...], wd_ref[...], preferred_element_type=_F32)
    x2 = x_ref[...] + mod_ref[5:6, :] * y
    x_out[...] = x2
    h_out[...] = _modulate(x2, gn_ref[...], modn_ref[0:1, :], modn_ref[1:2, :]).astype(h_out.dtype)


def _ffn(x, h, mod, modn, gn, wg, wu, wd, tm, tf):
    B, S, D = x.shape
    T = B * S
    F = wg.shape[1]
    per_b = S // tm
    x2 = x.reshape(T, D)
    h2 = h.reshape(T, D)
    tok = pl.BlockSpec((tm, D), lambda i: (i, 0))
    modspec = pl.BlockSpec((None, 6, D), lambda i: (i // per_b, 0, 0))
    once = pl.Buffered(1)
    xo, ho = pl.pallas_call(
        functools.partial(_ffn_kernel, tf=tf),
        grid=(T // tm,),
        in_specs=[tok, tok, modspec, modspec, pl.BlockSpec((1, D), lambda i: (0, 0)),
                  pl.BlockSpec((D, F), lambda i: (0, 0), pipeline_mode=once),
                  pl.BlockSpec((D, F), lambda i: (0, 0), pipeline_mode=once),
                  pl.BlockSpec((F, D), lambda i: (0, 0), pipeline_mode=once)],
        out_specs=[tok, tok],
        out_shape=[jax.ShapeDtypeStruct((T, D), _F32), jax.ShapeDtypeStruct((T, D), _BF)],
        scratch_shapes=[pltpu.VMEM((tm, F), _BF)],
        compiler_params=_cparams(("parallel",)),
        name="ffn_dense",
    )(x2, h2, mod, modn, gn, wg, wu, wd)
    return xo.reshape(B, S, D), ho.reshape(B, S, D)


def _qkv_kernel(h_ref, w_ref, gains_ref, rope_ref, q_ref, k_ref, v_ref):
    tm, D = h_ref.shape
    h = h_ref[...]
    lane = lax.broadcasted_iota(jnp.int32, (tm, LANES), 1)
    head_a = (lane & 63) < 32
    c, s = rope_ref[:, 0:128], rope_ref[:, 128:256]

    def head_pair(v, g):
        sq = v * v
        s_a = jnp.sum(jnp.where(head_a, sq, 0.0), axis=-1, keepdims=True)
        s_b = jnp.sum(jnp.where(head_a, 0.0, sq), axis=-1, keepdims=True)
        r = lax.rsqrt(jnp.where(head_a, s_a, s_b) * (1.0 / HEAD_DIM) + EPS)
        vn = v * r * g
        return vn * c + pltpu.roll(vn, 64, 1) * s

    y = jnp.dot(h, w_ref[...], preferred_element_type=_F32)
    for p in range(D // LANES):
        sl = slice(p * LANES, (p + 1) * LANES)
        q_ref[p] = head_pair(y[:, sl], gains_ref[0:1, :]).astype(_BF)
        k_ref[p] = head_pair(y[:, D + p * LANES:D + (p + 1) * LANES], gains_ref[1:2, :]).astype(_BF)
        v_ref[p] = y[:, 2 * D + p * LANES:2 * D + (p + 1) * LANES].astype(_BF)


def _qkv(h, w, gains, rope, tm):
    B, S, D = h.shape
    tok = pl.BlockSpec((None, tm, D), lambda b, i: (b, i, 0))
    hp = pl.BlockSpec((None, D // LANES, tm, LANES), lambda b, i: (b, 0, i, 0))
    sds = jax.ShapeDtypeStruct((B, D // LANES, S, LANES), _BF)
    return pl.pallas_call(
        _qkv_kernel,
        grid=(B, S // tm),
        in_specs=[tok, pl.BlockSpec(w.shape, lambda b, i: (0, 0)),
                  pl.BlockSpec((2, LANES), lambda b, i: (0, 0)),
                  pl.BlockSpec((tm, 2 * LANES), lambda b, i: (i, 0))],
        out_specs=[hp, hp, hp],
        out_shape=[sds, sds, sds],
        compiler_params=_cparams(("parallel", "parallel")),
        name="qkv_dil",
    )(h, w, gains, rope)


DIL_GROUP = 4


def _dil_kernel(q_ref, k_ref, v_ref, o_ref, tmp, qp, kp, vp, acc, mm, ll, accp, mmp, llp, *, tl):
    S = q_ref.shape[0]
    ng = S // DIL_GROUP
    pitch = DIL_GROUP + 1

    def spread(dst, src):
        for j in range(DIL_GROUP):
            dst[pl.ds(j, ng, stride=pitch), :] = src[pl.ds(j, ng, stride=DIL_GROUP), :]

    for src, dst in ((q_ref, qp), (k_ref, kp), (v_ref, vp)):
        tmp[...] = src[...].astype(_F32)
        spread(dst, tmp)
    lane = lax.broadcasted_iota(jnp.int32, (tl, LANES), 1)
    lo = lane < 64
    head_a = (lane & 63) < 32
    for bi, (window, d) in enumerate(DIL_PATTERNS):
        L = S // d
        half = window // (2 * d)
        t = min(tl, L)
        W = min(t + 2 * half, L)
        nt = L // t
        if d > 1:
            assert d % DIL_GROUP == 0
        sd = d * pitch // DIL_GROUP

        def rows(r, first, n, d=d, sd=sd):
            if d == 1:
                return pl.ds(first, n)
            return pl.ds(r + r // DIL_GROUP + sd * first, n, stride=sd)

        def body(idx, carry, bi=bi, d=d, L=L, half=half, t=t, W=W, nt=nt, rows=rows):
            r = idx // nt
            l0 = (idx % nt) * t
            start = jnp.clip(l0 - half, 0, L - W)
            if d == 1:
                qt = q_ref[pl.ds(pl.multiple_of(l0, t), t), :]
                kw = k_ref[pl.ds(pl.multiple_of(start, 64), W), :]
                vw = v_ref[pl.ds(pl.multiple_of(start, 64), W), :]
            else:
                qt = qp[rows(r, l0, t), :].astype(_BF)
                kw = kp[rows(r, start, W), :].astype(_BF)
                vw = vp[rows(r, start, W), :].astype(_BF)
            lo_t = lo[:t]
            qa_t = head_a[:t]
            zq = jnp.zeros_like(qt)
            q2 = jnp.concatenate([jnp.where(qa_t, qt, zq), jnp.where(qa_t, zq, qt)], axis=0)
            s = lax.dot_general(q2, kw, (((1,), (1,)), ((), ())), preferred_element_type=_F32)
            qpos = l0 + lax.broadcasted_iota(jnp.int32, (t, W), 0)
            kpos = start + lax.broadcasted_iota(jnp.int32, (t, W), 1)
            ok = jnp.abs(kpos - qpos) <= half
            s = jnp.where(jnp.concatenate([ok, ok], axis=0), s, NEG)
            m = jnp.max(s, axis=-1, keepdims=True)
            p = jnp.exp2(s - m)
            den = jnp.sum(p, axis=-1, keepdims=True)
            o2 = jnp.dot(p.astype(_BF), vw, preferred_element_type=_F32)
            o_new = jnp.where(lo_t, o2[:t], o2[t:])
            m_new = jnp.where(lo_t, m[:t], m[t:])
            l_new = jnp.where(lo_t, den[:t], den[t:])
            tok = rows(r, l0, t)
            if d == 1:
                assert bi == 0
                acc[tok, :] = o_new
                mm[tok, :] = m_new
                ll[tok, :] = l_new
            else:
                m_old = mmp[tok, :]
                mx = jnp.maximum(m_old, m_new)
                a_old = jnp.exp2(m_old - mx)
                a_new = jnp.exp2(m_new - mx)
                accp[tok, :] = accp[tok, :] * a_old + o_new * a_new
                llp[tok, :] = llp[tok, :] * a_old + l_new * a_new
                mmp[tok, :] = mx
            return carry

        lax.fori_loop(0, d * nt, body, 0, unroll=min(8, d * nt))
        if bi == 0:
            spread(accp, acc)
            spread(mmp, mm)
            spread(llp, ll)
    for j in range(DIL_GROUP):
        grp = pl.ds(j, ng, stride=pitch)
        tmp[pl.ds(j, ng, stride=DIL_GROUP), :] = accp[grp, :] / llp[grp, :]
    o_ref[...] = tmp[...].astype(o_ref.dtype)


def _dilated(q, k, v, tl):
    B, P, S, _ = q.shape
    spec = pl.BlockSpec((None, None, S, LANES), lambda b, p: (b, p, 0, 0))
    scr = pltpu.VMEM((S, LANES), _F32)
    scrp = pltpu.VMEM((S // DIL_GROUP * (DIL_GROUP + 1), LANES), _F32)
    return pl.pallas_call(
        functools.partial(_dil_kernel, tl=tl),
        grid=(B, P),
        in_specs=[spec, spec, spec],
        out_specs=spec,
        out_shape=jax.ShapeDtypeStruct((B, P, S, LANES), _BF),
        scratch_shapes=[scr, scrp, scrp, scrp, scr, scr, scr, scrp, scrp, scrp],
        compiler_params=_cparams(("parallel", "parallel")),
        name="dil_attn",
    )(q, k, v)


def _moe_kernel(te_ref, tv_ref, src_hbm, dst_hbm, h_hbm, wg_ref, wu_ref, wd_ref, out_hbm,
                xbuf, xbf, a_scr, ybuf, src_s0, src_s1, dst_s, gsem, ssem, isem, *, tf, dump0, n_dump):
    i = pl.program_id(0)
    nt = pl.num_programs(0)
    tm = xbf.shape[0]
    slot = i % 2

    def is_valid(t):
        return jnp.logical_and(jnp.logical_and(t >= 0, t < nt), tv_ref[jnp.clip(t, 0, nt - 1)] == 1)

    valid = is_valid(i)
    prev_valid = is_valid(i - 1)
    nxt_valid = is_valid(i + 1)

    src_bufs = (src_s0, src_s1)

    def src_copy(tile, par):
        return pltpu.make_async_copy(src_hbm.at[tile], src_bufs[par], isem.at[par])

    def dst_copy(tile):
        return pltpu.make_async_copy(dst_hbm.at[tile], dst_s, isem.at[2])

    def tile_at(first):
        return pl.ds(pl.multiple_of(first, SUBLANES), SUBLANES)

    def gather_row(r, par):
        return pltpu.make_async_copy(h_hbm.at[tile_at(src_bufs[par][r])],
                                     xbuf.at[par, tile_at(r * SUBLANES)], gsem.at[par])

    def scatter_row(r):
        return pltpu.make_async_copy(ybuf.at[tile_at(r * SUBLANES)], out_hbm.at[tile_at(dst_s[r])],
                                     ssem.at[0])

    def gather_wait(slot_):
        return pltpu.make_async_copy(h_hbm.at[pl.ds(0, tm * SUBLANES)], xbuf.at[slot_], gsem.at[slot_])

    def scatter_wait():
        return pltpu.make_async_copy(ybuf, out_hbm.at[pl.ds(0, tm * SUBLANES)], ssem.at[0])

    def for_rows(fn):
        def body(r, c):
            fn(r)
            return c
        lax.fori_loop(0, tm, body, 0, unroll=8)

    @pl.when(i == 0)
    def _():
        ybuf[...] = jnp.zeros_like(ybuf)
        for k in range(n_dump):
            pltpu.make_async_copy(ybuf, out_hbm.at[pl.ds((dump0 + k * tm) * SUBLANES, tm * SUBLANES)],
                                  ssem.at[0]).start()
        for k in range(n_dump):
            scatter_wait().wait()

    @pl.when(jnp.logical_and(i == 0, valid))
    def _():
        src_copy(0, 0).start()
        src_copy(0, 0).wait()
        for_rows(lambda r: gather_row(r, 0).start())

    @pl.when(jnp.logical_and(i == 0, nxt_valid))
    def _():
        src_copy(1, 1).start()

    @pl.when(valid)
    def _():
        dst_copy(i).start()

    for par in range(2):
        @pl.when(jnp.logical_and(nxt_valid, slot == par))
        def _(par=par):
            src_copy(i + 1, 1 - par).wait()
            for_rows(lambda r: gather_row(r, 1 - par).start())

    for par in range(2):
        @pl.when(jnp.logical_and(valid, slot == par))
        def _(par=par):
            gather_wait(par).wait()
            xbf[...] = _tiles_to_rows(xbuf.at[par], tm).astype(_BF)

    @pl.when(valid)
    def _():
        _swiglu_hidden(xbf[...], wg_ref, wu_ref, a_scr, tf)

    @pl.when(prev_valid)
    def _():
        scatter_wait().wait()

    @pl.when(valid)
    def _():
        _rows_to_tiles(ybuf, jnp.dot(a_scr[...], wd_ref[...], preferred_element_type=_F32))
        dst_copy(i).wait()
        for_rows(lambda r: scatter_row(r).start())

    for par in range(2):
        @pl.when(jnp.logical_and(is_valid(i + 2), slot == par))
        def _(par=par):
            src_copy(i + 2, par).start()

    @pl.when(jnp.logical_and(i == nt - 1, valid))
    def _():
        scatter_wait().wait()


def _moe(h, tile_e, tile_v, src, dst, wg, wu, wd, n_out_rows, tm, tf):
    T = h.shape[0] // SUBLANES
    D = wg.shape[1]
    nt = tile_e.shape[0]
    F = wg.shape[2]
    dump0 = T * TOP_K
    once = pl.Buffered(1)
    grid_spec = pltpu.PrefetchScalarGridSpec(
        num_scalar_prefetch=2,
        grid=(nt,),
        in_specs=[pl.BlockSpec(memory_space=pl.ANY), pl.BlockSpec(memory_space=pl.ANY),
                  pl.BlockSpec(memory_space=pl.ANY),
                  pl.BlockSpec((None, D, F), lambda i, te, tv: (te[i], 0, 0), pipeline_mode=once),
                  pl.BlockSpec((None, D, F), lambda i, te, tv: (te[i], 0, 0), pipeline_mode=once),
                  pl.BlockSpec((None, F, D), lambda i, te, tv: (te[i], 0, 0), pipeline_mode=once)],
        out_specs=pl.BlockSpec(memory_space=pl.ANY),
        scratch_shapes=[pltpu.VMEM((2, tm * SUBLANES, LANES), _F32), pltpu.VMEM((tm, D), _BF),
                        pltpu.VMEM((tm, F), _BF), pltpu.VMEM((tm * SUBLANES, LANES), _F32),
                        pltpu.SMEM((tm,), jnp.int32), pltpu.SMEM((tm,), jnp.int32),
                        pltpu.SMEM((tm,), jnp.int32),
                        pltpu.SemaphoreType.DMA((2,)), pltpu.SemaphoreType.DMA((1,)),
                        pltpu.SemaphoreType.DMA((3,))],
    )
    return pl.pallas_call(
        functools.partial(_moe_kernel, tf=tf, dump0=dump0, n_dump=(n_out_rows - dump0) // tm),
        grid_spec=grid_spec,
        out_shape=jax.ShapeDtypeStruct((n_out_rows * SUBLANES, LANES), _F32),
        compiler_params=_cparams(("arbitrary",)),
        name="moe_experts",
    )(tile_e, tile_v, src, dst, h, wg, wu, wd)


def _route_plan(route, T, tm):
    A = T * TOP_K
    e_flat = route[:, :TOP_K].astype(jnp.int32).reshape(A)
    order = jnp.argsort(e_flat, stable=True).astype(jnp.int32)
    counts = jnp.sum((e_flat[:, None] == jnp.arange(N_EXPERTS, dtype=jnp.int32)[None, :]).astype(jnp.int32), axis=0)
    starts = jnp.cumsum(counts) - counts
    pcounts = (counts + tm - 1) // tm * tm
    pends = jnp.cumsum(pcounts)
    pstarts = pends - pcounts
    nt = A // tm + N_EXPERTS
    tile0 = jnp.arange(nt, dtype=jnp.int32) * tm
    tile_v = (tile0 < pends[-1]).astype(jnp.int32)
    last_valid = jnp.maximum(pends[-1] // tm - 1, 0)
    tile_e_raw = jnp.minimum(jnp.searchsorted(pends, tile0, side="right"), N_EXPERTS - 1).astype(jnp.int32)
    tile_e = jnp.where(tile_v == 1, tile_e_raw, tile_e_raw[last_valid])
    r = jnp.arange(nt * tm, dtype=jnp.int32)
    e_r = jnp.repeat(tile_e, tm)
    within = r - pstarts[e_r]
    ok = jnp.logical_and(within < counts[e_r], jnp.repeat(tile_v, tm) == 1)
    a = order[jnp.clip(starts[e_r] + within, 0, A - 1)]
    src = jnp.where(ok, a // TOP_K, 0).astype(jnp.int32)
    dump = A + e_r * tm + jnp.clip(within - counts[e_r], 0, tm - 1)
    dst = jnp.where(ok, (a % TOP_K) * T + a // TOP_K, dump).astype(jnp.int32)
    return (tile_e, tile_v, (src * SUBLANES).reshape(nt, tm), (dst * SUBLANES).reshape(nt, tm),
            A + N_EXPERTS * tm)


def _combine_kernel(x_ref, y1_ref, y2_ref, r_ref, mod_ref, o_ref):
    tm = x_ref.shape[0]
    y = _tiles_to_rows(y1_ref, tm) * r_ref[:, 2:3] + _tiles_to_rows(y2_ref, tm) * r_ref[:, 3:4]
    o_ref[...] = x_ref[...] + mod_ref[5:6, :] * y


def _combine(x, y2, route, mod, tm):
    B, S, D = x.shape
    T = B * S
    per_b = S // tm
    out = pl.pallas_call(
        _combine_kernel,
        grid=(T // tm,),
        in_specs=[pl.BlockSpec((tm, D), lambda i: (i, 0)),
                  pl.BlockSpec((tm * SUBLANES, LANES), lambda i: (i, 0)),
                  pl.BlockSpec((tm * SUBLANES, LANES), lambda i: (T // tm + i, 0)),
                  pl.BlockSpec((tm, LANES), lambda i: (i, 0)),
                  pl.BlockSpec((None, 6, D), lambda i: (i // per_b, 0, 0))],
        out_specs=pl.BlockSpec((tm, D), lambda i: (i, 0)),
        out_shape=jax.ShapeDtypeStruct((T, D), _F32),
        compiler_params=_cparams(("parallel",)),
        name="moe_combine",
    )(x.reshape(T, D), y2, y2, route.reshape(T, LANES), mod)
    return out.reshape(B, S, D)


def _pad_cols(a, w):
    return jnp.pad(a, ((0, 0), (0, w - a.shape[1])))


def _rope_tables_even(S):
    pos = jnp.arange(S, dtype=jnp.int32)
    inv = ROPE_THETA ** (-jnp.arange(0, 32, 2, dtype=_F32) / 32)
    def cs(p):
        ang = p.astype(_F32)[:, None] * inv[None, :]
        return jnp.cos(ang), jnp.sin(ang)
    one = lambda w: jnp.ones((S, w), _F32)
    zero = lambda w: jnp.zeros((S, w), _F32)
    c, s = cs(pos)
    ca = jnp.concatenate([one(64), c, c, one(32)], 1)
    saa = jnp.concatenate([zero(64), -s, zero(16), zero(32)], 1)
    sba = jnp.concatenate([zero(64), zero(16), s, zero(32)], 1)
    cr, sr = cs(pos // GRID_W)
    cc, sc = cs(pos % GRID_W)
    cb = jnp.concatenate([cr, cr, cc, cc, one(64)], 1)
    sab = jnp.concatenate([-sr, zero(16), -sc, zero(16), zero(64)], 1)
    sbb = jnp.concatenate([zero(16), sr, zero(16), sc, zero(64)], 1)
    return jnp.concatenate([ca, saa, sba, cb, sab, sbb], 1)


def _rope_tables_odd(S):
    pos = jnp.arange(S, dtype=_F32)
    inv = ROPE_THETA ** (-jnp.arange(0, HEAD_DIM, 2, dtype=_F32) / HEAD_DIM)
    ang = pos[:, None] * inv[None, :]
    c, s = jnp.cos(ang), jnp.sin(ang)
    return jnp.concatenate([c, c, c, c, -s, -s, s, s], 1)


def _tiles(S):
    return dict(tm_pre=min(S, 512), tq=min(S, 2048), tm_o=min(S, 512),
                tm_ffn=min(S, 512), tf_ffn=512, tm_qkv=min(S, 512),
                tl=128, tm_moe=min(S, 512), tf_moe=512, tm_c=min(S, 512))


def kernel(x, c, ada_even_w, ada_even_b, norm_even_mix, norm_even_ffn, even_w_in, mla_q_norm, mla_w_uq, mla_kv_norm, mla_w_ukv, mla_q_gain, mla_k_gain, gqa_q_gain, gqa_k_gain, even_w_out, ffn_w_gate, ffn_w_up, ffn_w_down, ada_odd_w, ada_odd_b, norm_odd_mix, norm_odd_ffn, dil_w_qkv, dil_q_gain, dil_k_gain, dil_w_out, moe_router, moe_w_gate, moe_w_up, moe_w_down):
    B, S, D = x.shape
    T = B * S
    cfg = _tiles(S)

    mod_e = _ada_mod(c, ada_even_w[0], ada_even_b[0]).reshape(B, 6, D)
    mod_o = _ada_mod(c, ada_odd_w[0], ada_odd_b[0]).reshape(B, 6, D)

    w = even_w_in[0]
    sp = [MLA_Q_RANK, MLA_Q_RANK + MLA_KV_RANK, MLA_Q_RANK + MLA_KV_RANK + MLA_ROPE]
    sp.append(sp[-1] + GQA_HEADS * HEAD_DIM)
    sp.append(sp[-1] + GQA_KV_HEADS * HEAD_DIM)
    w_cq, w_ckv, w_kpe = w[:, :sp[0]], w[:, sp[0]:sp[1]], w[:, sp[1]:sp[2]]
    w_qb, w_kb, w_vb = w[:, sp[2]:sp[3]], w[:, sp[3]:sp[4]], w[:, sp[4]:]
    zc = lambda n: jnp.zeros((D, n), _F32)
    slot64 = lambda a, n: jnp.pad(a.reshape(D, n, HEAD_DIM), ((0, 0), (0, 0), (0, LANES - HEAD_DIM))).reshape(D, n * LANES)
    w_in = jnp.concatenate([w_cq, w_ckv, zc(64), w_kpe, zc(32), slot64(w_qb, GQA_HEADS),
                            slot64(w_kb, GQA_KV_HEADS), w_vb], axis=1).astype(_BF)
    na = MLA_NOPE + MLA_ROPE
    w_uq = jnp.pad(mla_w_uq[0].reshape(MLA_Q_RANK, MLA_HEADS, na),
                   ((0, 0), (0, 0), (0, LANES - na))).reshape(MLA_Q_RANK, MLA_HEADS * LANES).astype(_BF)
    w_ukv = mla_w_ukv[0].astype(_BF)
    pad128 = lambda v: jnp.pad(v, (0, LANES - v.shape[0]))
    gains_e = jnp.stack([pad128(mla_q_gain[0]) * (na ** -0.5 * LOG2E), pad128(mla_k_gain[0]),
                         pad128(gqa_q_gain[0]) * (HEAD_DIM ** -0.5 * LOG2E), pad128(gqa_k_gain[0])])
    rope_e = _rope_tables_even(S)
    q_all, k_all, vt_all = _pre_even(
        x, mod_e, norm_even_mix[0].reshape(1, D), w_in, mla_q_norm[0].reshape(1, -1), w_uq,
        mla_kv_norm[0].reshape(1, -1), w_ukv, gains_e, rope_e, cfg["tm_pre"])
    o_a = _attention(q_all, k_all, vt_all, q_off=0, k_off=0, shared_kv=False, tq=cfg["tq"])
    o_b = _attention(q_all, k_all, vt_all, q_off=MLA_HEADS, k_off=MLA_HEADS, shared_kv=True, tq=cfg["tq"])
    x1, h1 = _oproj(x, mod_e, norm_even_ffn[0].reshape(1, D), even_w_out[0].astype(_BF), [o_a, o_b],
                    cfg["tm_o"], _BF)
    x2, h2 = _ffn(x1, h1, mod_e, mod_o, norm_odd_mix[0].reshape(1, D), ffn_w_gate[0].astype(_BF),
                  ffn_w_up[0].astype(_BF), ffn_w_down[0].astype(_BF), cfg["tm_ffn"], cfg["tf_ffn"])

    hh = HEAD_DIM // 2
    pair = lambda v: jnp.concatenate([v[:hh], v[:hh], v[hh:], v[hh:]])
    gains_o = jnp.stack([pair(dil_q_gain[0]) * (HEAD_DIM ** -0.5 * LOG2E), pair(dil_k_gain[0])])
    wq, wk, wv = jnp.split(dil_w_qkv[0], 3, axis=1)
    perm = lambda w: w.reshape(D, D // LANES, 2, 2, hh).transpose(0, 1, 3, 2, 4).reshape(D, D)
    w_qkv = jnp.concatenate([perm(wq), perm(wk), wv], axis=1).astype(_BF)
    qd, kd, vd = _qkv(h2, w_qkv, gains_o, _rope_tables_odd(S), cfg["tm_qkv"])
    o_d = _dilated(qd, kd, vd, cfg["tl"])
    r32 = _pad_cols(moe_router[0], LANES)
    r_hi = r32.astype(_BF)
    r_lo = (r32 - r_hi.astype(_F32)).astype(_BF)
    x3, h3, route = _oproj(x2, mod_o, norm_odd_ffn[0].reshape(1, D), dil_w_out[0].astype(_BF), [o_d],
                           cfg["tm_o"], _F32, router=jnp.stack([r_hi, r_lo]))
    tm = cfg["tm_moe"]
    tile_e, tile_v, src, dst, n_rows = _route_plan(route.reshape(T, LANES), T, tm)
    y2 = _moe(h3, tile_e, tile_v, src, dst, moe_w_gate[0].astype(_BF),
              moe_w_up[0].astype(_BF), moe_w_down[0].astype(_BF), n_rows, tm, cfg["tf_moe"])
    return _combine(x3, y2, route, mod_o, cfg["tm_c"])
```

```python
import functools
import math

import jax
import jax.numpy as jnp
from jax import lax
from jax.experimental import pallas as pl
from jax.experimental.pallas import tpu as pltpu

_BF = jnp.bfloat16
_F32 = jnp.float32

GRID_W = 64
HEAD_DIM = 64
ROPE_THETA = 10000.0
EPS = 1e-6
MLA_HEADS = 8
MLA_Q_RANK = 256
MLA_KV_RANK = 128
MLA_NOPE = 64
MLA_ROPE = 32
MLA_V = 64
GQA_HEADS = 8
GQA_KV_HEADS = 2
DIL_PATTERNS = ((128, 1), (512, 4), (2048, 16))
N_EXPERTS = 8
TOP_K = 2
NEG = -1e30
LOG2E = math.log2(math.e)

LANES = 128
VMEM_LIMIT = 56 * 1024 * 1024


def _cparams(sem, vmem=VMEM_LIMIT):
    return pltpu.CompilerParams(dimension_semantics=sem, vmem_limit_bytes=vmem)


def _silu(x):
    return x / (1.0 + jnp.exp(-x))


def _modulate(x, g, shift, scale):
    ms = jnp.mean(x * x, axis=-1, keepdims=True)
    return x * lax.rsqrt(ms + EPS) * g * (1.0 + scale) + shift


def _norm(v, n, g):
    return v * lax.rsqrt(jnp.sum(v * v, axis=-1, keepdims=True) * (1.0 / n) + EPS) * g


SUBLANES = 8


def _rows_to_tiles(ref, x):
    tm = x.shape[0]
    for c in range(SUBLANES):
        ref[pl.ds(c, tm, stride=SUBLANES), :] = x[:, c * LANES:(c + 1) * LANES]


def _tiles_to_rows(ref, tm):
    return jnp.concatenate([ref[pl.ds(c, tm, stride=SUBLANES), :] for c in range(SUBLANES)], axis=1)


def _rope(x, c, sa, sb, k):
    return x * c + pltpu.roll(x, LANES - k, 1) * sa + pltpu.roll(x, k, 1) * sb


def _mod_kernel(c_ref, w_ref, b_ref, o_ref):
    sc = _silu(c_ref[...])
    o_ref[...] = jnp.dot(sc.astype(_BF), w_ref[...].astype(_BF),
                         preferred_element_type=_F32) + b_ref[...]


def _ada_mod(c, w, b):
    B, D = c.shape
    N = w.shape[1]
    tn = min(N, 1536)
    return pl.pallas_call(
        _mod_kernel,
        grid=(N // tn,),
        in_specs=[pl.BlockSpec((B, D), lambda j: (0, 0)),
                  pl.BlockSpec((D, tn), lambda j: (0, j)),
                  pl.BlockSpec((1, tn), lambda j: (0, j))],
        out_specs=pl.BlockSpec((B, tn), lambda j: (0, j)),
        out_shape=jax.ShapeDtypeStruct((B, N), _F32),
        compiler_params=_cparams(("arbitrary",)),
        name="ada_mod",
    )(c, w, b.reshape(1, N))


def _pre_even_kernel(x_ref, mod_ref, g_ref, w_in_ref, qn_ref, w_uq_ref, kvn_ref, w_uk_ref, w_uvt_ref,
                     w_vbt_ref, gains_ref, rope_ref, q_ref, k_ref, vt_ref, nq_scr, nk_scr):
    h = _modulate(x_ref[...], g_ref[...], mod_ref[0:1, :], mod_ref[1:2, :]).astype(_BF)
    y = jnp.dot(h, w_in_ref[...], preferred_element_type=_F32)
    cqn = _norm(y[:, 0:256], MLA_Q_RANK, qn_ref[...]).astype(_BF)
    qa = jnp.dot(cqn, w_uq_ref[...], preferred_element_type=_F32)
    ckvn = _norm(y[:, 256:384], MLA_KV_RANK, kvn_ref[...]).astype(_BF)
    kn = jnp.dot(ckvn, w_uk_ref[...], preferred_element_type=_F32)
    kpe = y[:, 384:512]
    nt_dims = (((1,), (1,)), ((), ()))
    nv = MLA_HEADS * MLA_V
    vt_ref[0:nv, :] = lax.dot_general(w_uvt_ref[...], ckvn, nt_dims,
                                      preferred_element_type=_F32).astype(_BF)
    vt_ref[nv:nv + LANES, :] = lax.dot_general(w_vbt_ref[...], h, nt_dims,
                                               preferred_element_type=_F32).astype(_BF)
    ca, sa = rope_ref[:, 0:128], rope_ref[:, 128:256]
    cb, sb = rope_ref[:, 256:384], rope_ref[:, 384:512]
    gqa, gka = gains_ref[0:1, :], gains_ref[1:2, :]
    gqb, gkb = gains_ref[2:3, :], gains_ref[3:4, :]
    na = MLA_NOPE + MLA_ROPE

    for hh in range(MLA_HEADS):
        sl = slice(hh * LANES, (hh + 1) * LANES)
        nq_scr[:, sl] = _norm(qa[:, sl], na, gqa)
        nk_scr[:, sl] = _norm(kn[:, sl] + kpe, na, gka)
    for hh in range(GQA_HEADS):
        src = slice(512 + hh * LANES, 512 + (hh + 1) * LANES)
        dst = slice((MLA_HEADS + hh) * LANES, (MLA_HEADS + hh + 1) * LANES)
        nq_scr[:, dst] = _norm(y[:, src], HEAD_DIM, gqb)
    for g in range(GQA_KV_HEADS):
        src = slice(1536 + g * LANES, 1536 + (g + 1) * LANES)
        dst = slice((MLA_HEADS + g) * LANES, (MLA_HEADS + g + 1) * LANES)
        nk_scr[:, dst] = _norm(y[:, src], HEAD_DIM, gkb)

    @pl.when(pl.program_id(1) >= 0)
    def _():
        def rope(ref, out, n_a, n_all):
            for hh in range(n_all):
                sl = slice(hh * LANES, (hh + 1) * LANES)
                c, s = (ca, sa) if hh < n_a else (cb, sb)
                v = ref[:, sl]
                out[:, sl] = (v * c + pltpu.roll(v, 64, 1) * s).astype(_BF)

        rope(nq_scr, q_ref, MLA_HEADS, MLA_HEADS + GQA_HEADS)
        rope(nk_scr, k_ref, MLA_HEADS, MLA_HEADS + GQA_KV_HEADS)


def _pre_even(x, mod, g, w_in, qn, w_uq, kvn, w_uk, w_uvt, w_vbt, gains, rope, tm):
    B, S, D = x.shape
    nq = (MLA_HEADS + GQA_HEADS) * LANES
    nk = (MLA_HEADS + GQA_KV_HEADS) * LANES
    nvt = MLA_HEADS * MLA_V + LANES
    full = lambda a: pl.BlockSpec(a.shape, lambda b, i: (0,) * a.ndim)
    tok = lambda w: pl.BlockSpec((None, tm, w), lambda b, i: (b, i, 0))
    return pl.pallas_call(
        _pre_even_kernel,
        grid=(B, S // tm),
        in_specs=[tok(D),
                  pl.BlockSpec((None, 6, D), lambda b, i: (b, 0, 0)),
                  full(g), full(w_in), full(qn), full(w_uq), full(kvn), full(w_uk), full(w_uvt),
                  full(w_vbt), full(gains),
                  pl.BlockSpec((tm, rope.shape[1]), lambda b, i: (i, 0))],
        out_specs=[tok(nq), tok(nk), pl.BlockSpec((None, None, nvt, tm), lambda b, i: (b, i, 0, 0))],
        out_shape=[jax.ShapeDtypeStruct((B, S, nq), _BF), jax.ShapeDtypeStruct((B, S, nk), _BF),
                   jax.ShapeDtypeStruct((B, S // tm, nvt, tm), _BF)],
        scratch_shapes=[pltpu.VMEM((tm, nq), _F32), pltpu.VMEM((tm, nk), _F32)],
        compiler_params=_cparams(("parallel", "parallel")),
        name="pre_even",
    )(x, mod, g, w_in, qn, w_uq, kvn, w_uk, w_uvt, w_vbt, gains, rope)


def _attn_kernel(q_ref, k_ref, vt_ref, o_ref, s_scr, *, shared_kv):
    tq = q_ref.shape[0]
    nk, _, tk = vt_ref.shape
    qs = [q_ref[:, hh * LANES:(hh + 1) * LANES] for hh in range(2)]

    def scores(j, hh):
        r0 = pl.multiple_of(j * tk, tk)
        kc = 0 if shared_kv else hh * LANES
        k = k_ref[pl.ds(r0, tk), kc:kc + LANES]
        return lax.dot_general(k, qs[hh], (((1,), (1,)), ((), ())), preferred_element_type=_F32)

    def step(j, hh, carry, cur):
        m, l, acc = carry
        s_scr[hh, 1 - cur] = scores(jnp.minimum(j + 1, nk - 1), hh)
        st = s_scr[hh, cur]
        m_new = jnp.maximum(m, jnp.max(st, axis=0, keepdims=True))
        alpha = jnp.exp2(m - m_new)
        p = jnp.exp2(st - m_new)
        l = alpha * l + jnp.sum(p, axis=0, keepdims=True)
        acc = alpha * acc + jnp.dot(vt_ref[j], p.astype(_BF), preferred_element_type=_F32)
        return m_new, l, acc

    def body(jj, carry):
        c = list(carry)
        for sub in range(2):
            for hh in range(2):
                c[hh] = step(2 * jj + sub, hh, c[hh], sub)
        return tuple(c)

    for hh in range(2):
        s_scr[hh, 0] = scores(0, hh)
    init1 = (jnp.full((1, tq), NEG, _F32), jnp.zeros((1, tq), _F32), jnp.zeros((LANES, tq), _F32))
    res = lax.fori_loop(0, nk // 2, body, (init1, init1))
    halves = []
    for hh, (_, l, acc) in enumerate(res):
        o = acc / l
        if shared_kv:
            hi = (pl.program_id(1) // 2) == 1
            halves.append(jnp.where(hi, o[64:, :], o[:64, :]))
        else:
            halves.append(o[hh * 64:(hh + 1) * 64, :])
    o_ref[...] = jnp.concatenate(halves, axis=0).T.astype(o_ref.dtype)


def _attention(q, k, vt, *, q_off, k_off, shared_kv, tq):
    B, S, _ = q.shape
    nk, _, tk = vt.shape[1:]
    assert nk % 2 == 0
    npairs = 4
    if shared_kv:
        kspec = pl.BlockSpec((None, S, LANES), lambda b, p, i: (b, 0, k_off + p // 2))
        vspec = pl.BlockSpec((None, nk, LANES, tk), lambda b, p, i: (b, 0, npairs, 0))
    else:
        kspec = pl.BlockSpec((None, S, 2 * LANES), lambda b, p, i: (b, 0, k_off // 2 + p))
        vspec = pl.BlockSpec((None, nk, LANES, tk), lambda b, p, i: (b, 0, p, 0))
    return pl.pallas_call(
        functools.partial(_attn_kernel, shared_kv=shared_kv),
        grid=(B, npairs, S // tq),
        in_specs=[pl.BlockSpec((None, tq, 2 * LANES), lambda b, p, i: (b, i, q_off // 2 + p)),
                  kspec, vspec],
        out_specs=pl.BlockSpec((None, tq, LANES), lambda b, p, i: (b, i, p)),
        out_shape=jax.ShapeDtypeStruct((B, S, npairs * LANES), _BF),
        scratch_shapes=[pltpu.VMEM((2, 2, tk, tq), _F32)],
        compiler_params=_cparams(("parallel", "parallel", "parallel")),
        name="attn_gqa" if shared_kv else "attn_mla",
    )(q, k, vt)


def _oproj_kernel(*refs, n_in, route):
    x_ref, mod_ref, g_ref, w_ref = refs[0], refs[1], refs[2], refs[3]
    o_refs = refs[4:4 + n_in]
    pos = 4 + n_in
    if route:
        r_ref = refs[pos]
        pos += 1
    x_out, h_out = refs[pos], refs[pos + 1]
    y = None
    off = 0
    for o_ref in o_refs:
        if len(o_ref.shape) == 3:
            o = jnp.concatenate([o_ref[p] for p in range(o_ref.shape[0])], axis=1)
        else:
            o = o_ref[...]
        w = o.shape[1]
        t = jnp.dot(o, w_ref[off:off + w, :], preferred_element_type=_F32)
        y = t if y is None else y + t
        off += w
    x1 = x_ref[...] + mod_ref[2:3, :] * y
    x_out[...] = x1
    h = _modulate(x1, g_ref[...], mod_ref[3:4, :], mod_ref[4:5, :])
    if route:
        _rows_to_tiles(h_out, h)
    else:
        h_out[...] = h.astype(h_out.dtype)
    if route:
        route_out = refs[pos + 2]
        hh = h.astype(_BF)
        hl = (h - hh.astype(_F32)).astype(_BF)
        rh, rl = r_ref[0], r_ref[1]
        logits = (jnp.dot(hh, rh, preferred_element_type=_F32)
                  + jnp.dot(hh, rl, preferred_element_type=_F32)
                  + jnp.dot(hl, rh, preferred_element_type=_F32))
        tm = logits.shape[0]
        lane = lax.broadcasted_iota(jnp.int32, (tm, LANES), 1)
        lg = jnp.where(lane < N_EXPERTS, logits, NEG)
        m1 = jnp.max(lg, axis=-1, keepdims=True)
        lanef = lane.astype(_F32)
        i1 = jnp.min(jnp.where(lg == m1, lanef, float(LANES)), axis=-1, keepdims=True)
        lg2 = jnp.where(lanef == i1, NEG, lg)
        m2 = jnp.max(lg2, axis=-1, keepdims=True)
        i2 = jnp.min(jnp.where(lg2 == m2, lanef, float(LANES)), axis=-1, keepdims=True)
        e = jnp.exp(m2 - m1)
        g1 = 1.0 / (1.0 + e)
        g2 = e / (1.0 + e)
        route_out[...] = jnp.where(lane == 0, i1,
                                   jnp.where(lane == 1, i2,
                                             jnp.where(lane == 2, g1, jnp.where(lane == 3, g2, 0.0))))


def _oproj(x, mod, g, w, o_list, tm, h_dtype, router=None):
    B, S, D = x.shape
    n_in = len(o_list)
    route = router is not None
    tok = lambda wd: pl.BlockSpec((None, tm, wd), lambda b, i: (b, i, 0))
    in_specs = [tok(D), pl.BlockSpec((None, 6, D), lambda b, i: (b, 0, 0)),
                pl.BlockSpec((1, D), lambda b, i: (0, 0)),
                pl.BlockSpec(w.shape, lambda b, i: (0, 0))]
    for o in o_list:
        if o.ndim == 4:
            in_specs.append(pl.BlockSpec((None, o.shape[1], tm, LANES), lambda b, i: (b, 0, i, 0)))
        else:
            in_specs.append(tok(o.shape[2]))
    args = [x, mod, g, w] + list(o_list)
    out_specs = [tok(D), tok(D)]
    out_shape = [jax.ShapeDtypeStruct((B, S, D), _F32), jax.ShapeDtypeStruct((B, S, D), h_dtype)]
    if route:
        assert D == SUBLANES * LANES
        per_b = S // tm
        out_specs[1] = pl.BlockSpec((tm * SUBLANES, LANES), lambda b, i: (b * per_b + i, 0))
        out_shape[1] = jax.ShapeDtypeStruct((B * S * SUBLANES, LANES), _F32)
        in_specs.append(pl.BlockSpec(router.shape, lambda b, i: (0, 0, 0)))
        args.append(router)
        out_specs.append(tok(LANES))
        out_shape.append(jax.ShapeDtypeStruct((B, S, LANES), _F32))
    return pl.pallas_call(
        functools.partial(_oproj_kernel, n_in=n_in, route=route),
        grid=(B, S // tm),
        in_specs=in_specs, out_specs=out_specs, out_shape=out_shape,
        compiler_params=_cparams(("parallel", "parallel")),
        name="oproj_route" if route else "oproj",
    )(*args)


def _swiglu_hidden(x, wg_ref, wu_ref, a_scr, tf):
    F = wg_ref.shape[1]
    for c0 in range(0, F, tf):
        c1 = min(c0 + tf, F)
        g = jnp.dot(x, wg_ref[:, c0:c1], preferred_element_type=_F32)
        u = jnp.dot(x, wu_ref[:, c0:c1], preferred_element_type=_F32)
        a_scr[:, c0:c1] = (_silu(g) * u).astype(_BF)


def _ffn_kernel(x_ref, h_ref, mod_ref, modn_ref, gn_ref, wg_ref, wu_ref, wd_ref, x_out, h_out, a_scr, *, tf):
    _swiglu_hidden(h_ref[...], wg_ref, wu_ref, a_scr, tf)
    y = jnp.dot(a_scr[...], wd_ref[...], preferred_element_type=_F32)
    x2 = x_ref[...] + mod_ref[5:6, :] * y
    x_out[...] = x2
    h_out[...] = _modulate(x2, gn_ref[...], modn_ref[0:1, :], modn_ref[1:2, :]).astype(h_out.dtype)


def _ffn(x, h, mod, modn, gn, wg, wu, wd, tm, tf):
    B, S, D = x.shape
    T = B * S
    F = wg.shape[1]
    per_b = S // tm
    x2 = x.reshape(T, D)
    h2 = h.reshape(T, D)
    tok = pl.BlockSpec((tm, D), lambda i: (i, 0))
    modspec = pl.BlockSpec((None, 6, D), lambda i: (i // per_b, 0, 0))
    once = pl.Buffered(1)
    xo, ho = pl.pallas_call(
        functools.partial(_ffn_kernel, tf=tf),
        grid=(T // tm,),
        in_specs=[tok, tok, modspec, modspec, pl.BlockSpec((1, D), lambda i: (0, 0)),
                  pl.BlockSpec((D, F), lambda i: (0, 0), pipeline_mode=once),
                  pl.BlockSpec((D, F), lambda i: (0, 0), pipeline_mode=once),
                  pl.BlockSpec((F, D), lambda i: (0, 0), pipeline_mode=once)],
        out_specs=[tok, tok],
        out_shape=[jax.ShapeDtypeStruct((T, D), _F32), jax.ShapeDtypeStruct((T, D), _BF)],
        scratch_shapes=[pltpu.VMEM((tm, F), _BF)],
        compiler_params=_cparams(("parallel",)),
        name="ffn_dense",
    )(x2, h2, mod, modn, gn, wg, wu, wd)
    return xo.reshape(B, S, D), ho.reshape(B, S, D)


def _qkv_kernel(h_ref, w_ref, gains_ref, rope_ref, q_ref, k_ref, v_ref, n_scr):
    tm, D = h_ref.shape
    h = h_ref[...]
    lane = lax.broadcasted_iota(jnp.int32, (tm, LANES), 1)
    head_a = (lane & 63) < 32

    def head_pair_norm(v, g):
        sq = v * v
        s_a = jnp.sum(jnp.where(head_a, sq, 0.0), axis=-1, keepdims=True)
        s_b = jnp.sum(jnp.where(head_a, 0.0, sq), axis=-1, keepdims=True)
        r = lax.rsqrt(jnp.where(head_a, s_a, s_b) * (1.0 / HEAD_DIM) + EPS)
        return v * r * g

    y = jnp.dot(h, w_ref[...], preferred_element_type=_F32)
    for p in range(D // LANES):
        sl = slice(p * LANES, (p + 1) * LANES)
        n_scr[:, sl] = head_pair_norm(y[:, sl], gains_ref[0:1, :])
        n_scr[:, D + p * LANES:D + (p + 1) * LANES] = head_pair_norm(
            y[:, D + p * LANES:D + (p + 1) * LANES], gains_ref[1:2, :])
        v_ref[p] = y[:, 2 * D + p * LANES:2 * D + (p + 1) * LANES].astype(_BF)

    @pl.when(pl.program_id(1) >= 0)
    def _():
        c, s = rope_ref[:, 0:128], rope_ref[:, 128:256]
        for p in range(D // LANES):
            vq = n_scr[:, p * LANES:(p + 1) * LANES]
            vk = n_scr[:, D + p * LANES:D + (p + 1) * LANES]
            q_ref[p] = (vq * c + pltpu.roll(vq, 64, 1) * s).astype(_BF)
            k_ref[p] = (vk * c + pltpu.roll(vk, 64, 1) * s).astype(_BF)


def _qkv(h, w, gains, rope, tm):
    B, S, D = h.shape
    tok = pl.BlockSpec((None, tm, D), lambda b, i: (b, i, 0))
    hp = pl.BlockSpec((None, D // LANES, tm, LANES), lambda b, i: (b, 0, i, 0))
    sds = jax.ShapeDtypeStruct((B, D // LANES, S, LANES), _BF)
    return pl.pallas_call(
        _qkv_kernel,
        grid=(B, S // tm),
        in_specs=[tok, pl.BlockSpec(w.shape, lambda b, i: (0, 0)),
                  pl.BlockSpec((2, LANES), lambda b, i: (0, 0)),
                  pl.BlockSpec((tm, 2 * LANES), lambda b, i: (i, 0))],
        out_specs=[hp, hp, hp],
        out_shape=[sds, sds, sds],
        scratch_shapes=[pltpu.VMEM((tm, 2 * D), _F32)],
        compiler_params=_cparams(("parallel", "parallel")),
        name="qkv_dil",
    )(h, w, gains, rope)


DIL_GROUP = 4


def _dil_kernel(q_ref, k_ref, v_ref, o_ref, tmp, qp, kp, vp, acc, mm, ll, accp, mmp, llp, *, tl):
    S = q_ref.shape[0]
    ng = S // DIL_GROUP
    pitch = DIL_GROUP + 1

    def spread(dst, src):
        for j in range(DIL_GROUP):
            dst[pl.ds(j, ng, stride=pitch), :] = src[pl.ds(j, ng, stride=DIL_GROUP), :]

    for src, dst in ((q_ref, qp), (k_ref, kp), (v_ref, vp)):
        tmp[...] = src[...].astype(_F32)
        spread(dst, tmp)
    lane = lax.broadcasted_iota(jnp.int32, (tl, LANES), 1)
    lo = lane < 64
    head_a = (lane & 63) < 32
    for bi, (window, d) in enumerate(DIL_PATTERNS):
        L = S // d
        half = window // (2 * d)
        t = min(tl, L)
        W = min(t + 2 * half, L)
        nt = L // t
        if d > 1:
            assert d % DIL_GROUP == 0
        sd = d * pitch // DIL_GROUP

        def rows(r, first, n, d=d, sd=sd):
            if d == 1:
                return pl.ds(first, n)
            return pl.ds(r + r // DIL_GROUP + sd * first, n, stride=sd)

        def body(idx, carry, bi=bi, d=d, L=L, half=half, t=t, W=W, nt=nt, rows=rows):
            r = idx // nt
            l0 = (idx % nt) * t
            start = jnp.clip(l0 - half, 0, L - W)
            if d == 1:
                qt = q_ref[pl.ds(pl.multiple_of(l0, t), t), :]
                kw = k_ref[pl.ds(pl.multiple_of(start, 64), W), :]
                vw = v_ref[pl.ds(pl.multiple_of(start, 64), W), :]
            else:
                qt = qp[rows(r, l0, t), :].astype(_BF)
                kw = kp[rows(r, start, W), :].astype(_BF)
                vw = vp[rows(r, start, W), :].astype(_BF)
            lo_t = lo[:t]
            qa_t = head_a[:t]
            zq = jnp.zeros_like(qt)
            q2 = jnp.concatenate([jnp.where(qa_t, qt, zq), jnp.where(qa_t, zq, qt)], axis=0)
            s = lax.dot_general(q2, kw, (((1,), (1,)), ((), ())), preferred_element_type=_F32)
            qpos = l0 + lax.broadcasted_iota(jnp.int32, (t, W), 0)
            kpos = start + lax.broadcasted_iota(jnp.int32, (t, W), 1)
            ok = jnp.abs(kpos - qpos) <= half
            s = jnp.where(jnp.concatenate([ok, ok], axis=0), s, NEG)
            m = jnp.max(s, axis=-1, keepdims=True)
            p = jnp.exp2(s - m)
            den = jnp.sum(p, axis=-1, keepdims=True)
            o2 = jnp.dot(p.astype(_BF), vw, preferred_element_type=_F32)
            o_new = jnp.where(lo_t, o2[:t], o2[t:])
            m_new = jnp.where(lo_t, m[:t], m[t:])
            l_new = jnp.where(lo_t, den[:t], den[t:])
            tok = rows(r, l0, t)
            if d == 1:
                assert bi == 0
                acc[tok, :] = o_new
                mm[tok, :] = m_new
                ll[tok, :] = l_new
            else:
                m_old = mmp[tok, :]
                mx = jnp.maximum(m_old, m_new)
                a_old = jnp.exp2(m_old - mx)
                a_new = jnp.exp2(m_new - mx)
                accp[tok, :] = accp[tok, :] * a_old + o_new * a_new
                llp[tok, :] = llp[tok, :] * a_old + l_new * a_new
                mmp[tok, :] = mx
            return carry

        lax.fori_loop(0, d * nt, body, 0, unroll=min(8, d * nt))
        if bi == 0:
            spread(accp, acc)
            spread(mmp, mm)
            spread(llp, ll)
    for j in range(DIL_GROUP):
        grp = pl.ds(j, ng, stride=pitch)
        tmp[pl.ds(j, ng, stride=DIL_GROUP), :] = accp[grp, :] / llp[grp, :]
    o_ref[...] = tmp[...].astype(o_ref.dtype)


def _dilated(q, k, v, tl):
    B, P, S, _ = q.shape
    spec = pl.BlockSpec((None, None, S, LANES), lambda b, p: (b, p, 0, 0))
    scr = pltpu.VMEM((S, LANES), _F32)
    scrp = pltpu.VMEM((S // DIL_GROUP * (DIL_GROUP + 1), LANES), _F32)
    return pl.pallas_call(
        functools.partial(_dil_kernel, tl=tl),
        grid=(B, P),
        in_specs=[spec, spec, spec],
        out_specs=spec,
        out_shape=jax.ShapeDtypeStruct((B, P, S, LANES), _BF),
        scratch_shapes=[scr, scrp, scrp, scrp, scr, scr, scr, scrp, scrp, scrp],
        compiler_params=_cparams(("parallel", "parallel")),
        name="dil_attn",
    )(q, k, v)


def _moe_kernel(te_ref, tv_ref, src_hbm, dst_hbm, h_hbm, wg_ref, wu_ref, wd_ref, out_hbm,
                xbuf, xbf, a_scr, ybuf, src_s0, src_s1, dst_s, gsem, ssem, isem, *, tf, dump0, n_dump):
    i = pl.program_id(0)
    nt = pl.num_programs(0)
    tm = xbf.shape[0]
    slot = i % 2

    def is_valid(t):
        return jnp.logical_and(jnp.logical_and(t >= 0, t < nt), tv_ref[jnp.clip(t, 0, nt - 1)] == 1)

    valid = is_valid(i)
    prev_valid = is_valid(i - 1)
    nxt_valid = is_valid(i + 1)

    src_bufs = (src_s0, src_s1)

    def src_copy(tile, par):
        return pltpu.make_async_copy(src_hbm.at[tile], src_bufs[par], isem.at[par])

    def dst_copy(tile):
        return pltpu.make_async_copy(dst_hbm.at[tile], dst_s, isem.at[2])

    def tile_at(first):
        return pl.ds(pl.multiple_of(first, SUBLANES), SUBLANES)

    def gather_row(r, par):
        return pltpu.make_async_copy(h_hbm.at[tile_at(src_bufs[par][r])],
                                     xbuf.at[par, tile_at(r * SUBLANES)], gsem.at[par])

    def scatter_row(r):
        return pltpu.make_async_copy(ybuf.at[tile_at(r * SUBLANES)], out_hbm.at[tile_at(dst_s[r])],
                                     ssem.at[0])

    def gather_wait(slot_):
        return pltpu.make_async_copy(h_hbm.at[pl.ds(0, tm * SUBLANES)], xbuf.at[slot_], gsem.at[slot_])

    def scatter_wait():
        return pltpu.make_async_copy(ybuf, out_hbm.at[pl.ds(0, tm * SUBLANES)], ssem.at[0])

    def for_rows(fn):
        def body(r, c):
            fn(r)
            return c
        lax.fori_loop(0, tm, body, 0, unroll=8)

    @pl.when(i == 0)
    def _():
        ybuf[...] = jnp.zeros_like(ybuf)
        for k in range(n_dump):
            pltpu.make_async_copy(ybuf, out_hbm.at[pl.ds((dump0 + k * tm) * SUBLANES, tm * SUBLANES)],
                                  ssem.at[0]).start()
        for k in range(n_dump):
            scatter_wait().wait()

    @pl.when(jnp.logical_and(i == 0, valid))
    def _():
        src_copy(0, 0).start()
        src_copy(0, 0).wait()
        for_rows(lambda r: gather_row(r, 0).start())

    @pl.when(jnp.logical_and(i == 0, nxt_valid))
    def _():
        src_copy(1, 1).start()

    @pl.when(valid)
    def _():
        dst_copy(i).start()

    for par in range(2):
        @pl.when(jnp.logical_and(nxt_valid, slot == par))
        def _(par=par):
            src_copy(i + 1, 1 - par).wait()
            for_rows(lambda r: gather_row(r, 1 - par).start())

    for par in range(2):
        @pl.when(jnp.logical_and(valid, slot == par))
        def _(par=par):
            gather_wait(par).wait()
            xbf[...] = _tiles_to_rows(xbuf.at[par], tm).astype(_BF)

    @pl.when(valid)
    def _():
        _swiglu_hidden(xbf[...], wg_ref, wu_ref, a_scr, tf)

    @pl.when(prev_valid)
    def _():
        scatter_wait().wait()

    @pl.when(valid)
    def _():
        _rows_to_tiles(ybuf, jnp.dot(a_scr[...], wd_ref[...], preferred_element_type=_F32))
        dst_copy(i).wait()
        for_rows(lambda r: scatter_row(r).start())

    for par in range(2):
        @pl.when(jnp.logical_and(is_valid(i + 2), slot == par))
        def _(par=par):
            src_copy(i + 2, par).start()

    @pl.when(jnp.logical_and(i == nt - 1, valid))
    def _():
        scatter_wait().wait()


def _moe(h, tile_e, tile_v, src, dst, wg, wu, wd, n_out_rows, tm, tf):
    T = h.shape[0] // SUBLANES
    D = wg.shape[1]
    nt = tile_e.shape[0]
    F = wg.shape[2]
    dump0 = T * TOP_K
    once = pl.Buffered(1)
    grid_spec = pltpu.PrefetchScalarGridSpec(
        num_scalar_prefetch=2,
        grid=(nt,),
        in_specs=[pl.BlockSpec(memory_space=pl.ANY), pl.BlockSpec(memory_space=pl.ANY),
                  pl.BlockSpec(memory_space=pl.ANY),
                  pl.BlockSpec((None, D, F), lambda i, te, tv: (te[i], 0, 0), pipeline_mode=once),
                  pl.BlockSpec((None, D, F), lambda i, te, tv: (te[i], 0, 0), pipeline_mode=once),
                  pl.BlockSpec((None, F, D), lambda i, te, tv: (te[i], 0, 0), pipeline_mode=once)],
        out_specs=pl.BlockSpec(memory_space=pl.ANY),
        scratch_shapes=[pltpu.VMEM((2, tm * SUBLANES, LANES), _F32), pltpu.VMEM((tm, D), _BF),
                        pltpu.VMEM((tm, F), _BF), pltpu.VMEM((tm * SUBLANES, LANES), _F32),
                        pltpu.SMEM((tm,), jnp.int32), pltpu.SMEM((tm,), jnp.int32),
                        pltpu.SMEM((tm,), jnp.int32),
                        pltpu.SemaphoreType.DMA((2,)), pltpu.SemaphoreType.DMA((1,)),
                        pltpu.SemaphoreType.DMA((3,))],
    )
    return pl.pallas_call(
        functools.partial(_moe_kernel, tf=tf, dump0=dump0, n_dump=(n_out_rows - dump0) // tm),
        grid_spec=grid_spec,
        out_shape=jax.ShapeDtypeStruct((n_out_rows * SUBLANES, LANES), _F32),
        compiler_params=_cparams(("arbitrary",)),
        name="moe_experts",
    )(tile_e, tile_v, src, dst, h, wg, wu, wd)


def _route_plan(route, T, tm):
    A = T * TOP_K
    e_flat = route[:, :TOP_K].astype(jnp.int32).reshape(A)
    order = jnp.argsort(e_flat, stable=True).astype(jnp.int32)
    counts = jnp.sum((e_flat[:, None] == jnp.arange(N_EXPERTS, dtype=jnp.int32)[None, :]).astype(jnp.int32), axis=0)
    starts = jnp.cumsum(counts) - counts
    pcounts = (counts + tm - 1) // tm * tm
    pends = jnp.cumsum(pcounts)
    pstarts = pends - pcounts
    nt = A // tm + N_EXPERTS
    tile0 = jnp.arange(nt, dtype=jnp.int32) * tm
    tile_v = (tile0 < pends[-1]).astype(jnp.int32)
    last_valid = jnp.maximum(pends[-1] // tm - 1, 0)
    tile_e_raw = jnp.minimum(jnp.searchsorted(pends, tile0, side="right"), N_EXPERTS - 1).astype(jnp.int32)
    tile_e = jnp.where(tile_v == 1, tile_e_raw, tile_e_raw[last_valid])
    r = jnp.arange(nt * tm, dtype=jnp.int32)
    e_r = jnp.repeat(tile_e, tm)
    within = r - pstarts[e_r]
    ok = jnp.logical_and(within < counts[e_r], jnp.repeat(tile_v, tm) == 1)
    a = order[jnp.clip(starts[e_r] + within, 0, A - 1)]
    src = jnp.where(ok, a // TOP_K, 0).astype(jnp.int32)
    dump = A + e_r * tm + jnp.clip(within - counts[e_r], 0, tm - 1)
    dst = jnp.where(ok, (a % TOP_K) * T + a // TOP_K, dump).astype(jnp.int32)
    return (tile_e, tile_v, (src * SUBLANES).reshape(nt, tm), (dst * SUBLANES).reshape(nt, tm),
            A + N_EXPERTS * tm)


def _combine_kernel(x_ref, y1_ref, y2_ref, r_ref, mod_ref, o_ref):
    tm = x_ref.shape[0]
    y = _tiles_to_rows(y1_ref, tm) * r_ref[:, 2:3] + _tiles_to_rows(y2_ref, tm) * r_ref[:, 3:4]
    o_ref[...] = x_ref[...] + mod_ref[5:6, :] * y


def _combine(x, y2, route, mod, tm):
    B, S, D = x.shape
    T = B * S
    per_b = S // tm
    out = pl.pallas_call(
        _combine_kernel,
        grid=(T // tm,),
        in_specs=[pl.BlockSpec((tm, D), lambda i: (i, 0)),
                  pl.BlockSpec((tm * SUBLANES, LANES), lambda i: (i, 0)),
                  pl.BlockSpec((tm * SUBLANES, LANES), lambda i: (T // tm + i, 0)),
                  pl.BlockSpec((tm, LANES), lambda i: (i, 0)),
                  pl.BlockSpec((None, 6, D), lambda i: (i // per_b, 0, 0))],
        out_specs=pl.BlockSpec((tm, D), lambda i: (i, 0)),
        out_shape=jax.ShapeDtypeStruct((T, D), _F32),
        compiler_params=_cparams(("parallel",)),
        name="moe_combine",
    )(x.reshape(T, D), y2, y2, route.reshape(T, LANES), mod)
    return out.reshape(B, S, D)


def _pad_cols(a, w):
    return jnp.pad(a, ((0, 0), (0, w - a.shape[1])))


def _rope_tables_even(S):
    pos = jnp.arange(S, dtype=jnp.int32)
    inv = ROPE_THETA ** (-jnp.arange(0, 32, 2, dtype=_F32) / 32)
    def cs(p):
        ang = p.astype(_F32)[:, None] * inv[None, :]
        return jnp.cos(ang), jnp.sin(ang)
    one = lambda w: jnp.ones((S, w), _F32)
    zero = lambda w: jnp.zeros((S, w), _F32)
    c, s = cs(pos)
    ca = jnp.concatenate([c, one(48), c, one(48)], 1)
    sa = jnp.concatenate([-s, zero(48), s, zero(48)], 1)
    cr, sr = cs(pos // GRID_W)
    cc, sc = cs(pos % GRID_W)
    cb = jnp.concatenate([cr, cc, one(32), cr, cc, one(32)], 1)
    sb = jnp.concatenate([-sr, -sc, zero(32), sr, sc, zero(32)], 1)
    return jnp.concatenate([ca, sa, cb, sb], 1)


def _slot_maps():
    r = MLA_ROPE // 2
    mla = ([MLA_NOPE + i for i in range(r)] + list(range(0, 64 - r))
           + [MLA_NOPE + r + i for i in range(r)] + list(range(64 - r, MLA_NOPE)))
    mla += [-1] * (LANES - len(mla))
    q = HEAD_DIM // 4
    gqa = (list(range(0, q)) + list(range(2 * q, 3 * q)) + [-1] * (64 - 2 * q)
           + list(range(q, 2 * q)) + list(range(3 * q, 4 * q)) + [-1] * (64 - 2 * q))
    return mla, gqa


def _to_slots(a, lane_map):
    idx = jnp.asarray([max(i, 0) for i in lane_map], jnp.int32)
    keep = jnp.asarray([1.0 if i >= 0 else 0.0 for i in lane_map], a.dtype)
    return jnp.take(a, idx, axis=-1) * keep


def _rope_tables_odd(S):
    pos = jnp.arange(S, dtype=_F32)
    inv = ROPE_THETA ** (-jnp.arange(0, HEAD_DIM, 2, dtype=_F32) / HEAD_DIM)
    ang = pos[:, None] * inv[None, :]
    c, s = jnp.cos(ang), jnp.sin(ang)
    return jnp.concatenate([c, c, c, c, -s, -s, s, s], 1)


def _tiles(S):
    return dict(tm_pre=min(S, 512), tq=min(S, 2048), tm_o=min(S, 512),
                tm_ffn=min(S, 512), tf_ffn=512, tm_qkv=min(S, 512),
                tl=128, tm_moe=min(S, 512), tf_moe=512, tm_c=min(S, 512))


def kernel(x, c, ada_even_w, ada_even_b, norm_even_mix, norm_even_ffn, even_w_in, mla_q_norm, mla_w_uq, mla_kv_norm, mla_w_ukv, mla_q_gain, mla_k_gain, gqa_q_gain, gqa_k_gain, even_w_out, ffn_w_gate, ffn_w_up, ffn_w_down, ada_odd_w, ada_odd_b, norm_odd_mix, norm_odd_ffn, dil_w_qkv, dil_q_gain, dil_k_gain, dil_w_out, moe_router, moe_w_gate, moe_w_up, moe_w_down):
    B, S, D = x.shape
    T = B * S
    cfg = _tiles(S)

    mod_e = _ada_mod(c, ada_even_w[0], ada_even_b[0]).reshape(B, 6, D)
    mod_o = _ada_mod(c, ada_odd_w[0], ada_odd_b[0]).reshape(B, 6, D)

    w = even_w_in[0]
    sp = [MLA_Q_RANK, MLA_Q_RANK + MLA_KV_RANK, MLA_Q_RANK + MLA_KV_RANK + MLA_ROPE]
    sp.append(sp[-1] + GQA_HEADS * HEAD_DIM)
    sp.append(sp[-1] + GQA_KV_HEADS * HEAD_DIM)
    w_cq, w_ckv, w_kpe = w[:, :sp[0]], w[:, sp[0]:sp[1]], w[:, sp[1]:sp[2]]
    w_qb, w_kb, w_vb = w[:, sp[2]:sp[3]], w[:, sp[3]:sp[4]], w[:, sp[4]:]
    mla_map, gqa_map = _slot_maps()
    na = MLA_NOPE + MLA_ROPE
    nope_only = [i if 0 <= i < MLA_NOPE else -1 for i in mla_map]
    rope_only = [i - MLA_NOPE if i >= MLA_NOPE else -1 for i in mla_map]
    gslots = lambda a, n: _to_slots(a.reshape(D, n, HEAD_DIM), gqa_map).reshape(D, n * LANES)
    w_in = jnp.concatenate([w_cq, w_ckv, _to_slots(w_kpe, rope_only), gslots(w_qb, GQA_HEADS),
                            gslots(w_kb, GQA_KV_HEADS)], axis=1).astype(_BF)
    w_uq = _to_slots(mla_w_uq[0].reshape(MLA_Q_RANK, MLA_HEADS, na), mla_map)
    w_uq = w_uq.reshape(MLA_Q_RANK, MLA_HEADS * LANES).astype(_BF)
    w_ukv = mla_w_ukv[0].reshape(MLA_KV_RANK, MLA_HEADS, MLA_NOPE + MLA_V)
    w_uk = _to_slots(w_ukv[:, :, :MLA_NOPE], nope_only).reshape(MLA_KV_RANK, MLA_HEADS * LANES).astype(_BF)
    w_uvt = w_ukv[:, :, MLA_NOPE:].reshape(MLA_KV_RANK, MLA_HEADS * MLA_V).T.astype(_BF)
    w_vbt = w_vb.T.astype(_BF)
    gains_e = jnp.stack([_to_slots(mla_q_gain[0], mla_map) * (na ** -0.5 * LOG2E),
                         _to_slots(mla_k_gain[0], mla_map),
                         _to_slots(gqa_q_gain[0], gqa_map) * (HEAD_DIM ** -0.5 * LOG2E),
                         _to_slots(gqa_k_gain[0], gqa_map)])
    rope_e = _rope_tables_even(S)
    q_all, k_all, vt_all = _pre_even(
        x, mod_e, norm_even_mix[0].reshape(1, D), w_in, mla_q_norm[0].reshape(1, -1), w_uq,
        mla_kv_norm[0].reshape(1, -1), w_uk, w_uvt, w_vbt, gains_e, rope_e, cfg["tm_pre"])
    o_a = _attention(q_all, k_all, vt_all, q_off=0, k_off=0, shared_kv=False, tq=cfg["tq"])
    o_b = _attention(q_all, k_all, vt_all, q_off=MLA_HEADS, k_off=MLA_HEADS, shared_kv=True, tq=cfg["tq"])
    x1, h1 = _oproj(x, mod_e, norm_even_ffn[0].reshape(1, D), even_w_out[0].astype(_BF), [o_a, o_b],
                    cfg["tm_o"], _BF)
    x2, h2 = _ffn(x1, h1, mod_e, mod_o, norm_odd_mix[0].reshape(1, D), ffn_w_gate[0].astype(_BF),
                  ffn_w_up[0].astype(_BF), ffn_w_down[0].astype(_BF), cfg["tm_ffn"], cfg["tf_ffn"])

    hh = HEAD_DIM // 2
    pair = lambda v: jnp.concatenate([v[:hh], v[:hh], v[hh:], v[hh:]])
    gains_o = jnp.stack([pair(dil_q_gain[0]) * (HEAD_DIM ** -0.5 * LOG2E), pair(dil_k_gain[0])])
    wq, wk, wv = jnp.split(dil_w_qkv[0], 3, axis=1)
    perm = lambda w: w.reshape(D, D // LANES, 2, 2, hh).transpose(0, 1, 3, 2, 4).reshape(D, D)
    w_qkv = jnp.concatenate([perm(wq), perm(wk), wv], axis=1).astype(_BF)
    qd, kd, vd = _qkv(h2, w_qkv, gains_o, _rope_tables_odd(S), cfg["tm_qkv"])
    o_d = _dilated(qd, kd, vd, cfg["tl"])
    r32 = _pad_cols(moe_router[0], LANES)
    r_hi = r32.astype(_BF)
    r_lo = (r32 - r_hi.astype(_F32)).astype(_BF)
    x3, h3, route = _oproj(x2, mod_o, norm_odd_ffn[0].reshape(1, D), dil_w_out[0].astype(_BF), [o_d],
                           cfg["tm_o"], _F32, router=jnp.stack([r_hi, r_lo]))
    tm = cfg["tm_moe"]
    tile_e, tile_v, src, dst, n_rows = _route_plan(route.reshape(T, LANES), T, tm)
    y2 = _moe(h3, tile_e, tile_v, src, dst, moe_w_gate[0].astype(_BF),
              moe_w_up[0].astype(_BF), moe_w_down[0].astype(_BF), n_rows, tm, cfg["tf_moe"])
    return _combine(x3, y2, route, mod_o, cfg["tm_c"])
```

```python
import functools
import math

import jax
import jax.numpy as jnp
from jax import lax
from jax.experimental import pallas as pl
from jax.experimental.pallas import tpu as pltpu

_BF = jnp.bfloat16
_F32 = jnp.float32

GRID_W = 64
HEAD_DIM = 64
ROPE_THETA = 10000.0
EPS = 1e-6
MLA_HEADS = 8
MLA_Q_RANK = 256
MLA_KV_RANK = 128
MLA_NOPE = 64
MLA_ROPE = 32
MLA_V = 64
GQA_HEADS = 8
GQA_KV_HEADS = 2
DIL_PATTERNS = ((128, 1), (512, 4), (2048, 16))
N_EXPERTS = 8
TOP_K = 2
NEG = -1e30
LOG2E = math.log2(math.e)

LANES = 128
VMEM_LIMIT = 56 * 1024 * 1024


def _cparams(sem, vmem=VMEM_LIMIT):
    return pltpu.CompilerParams(dimension_semantics=sem, vmem_limit_bytes=vmem)


def _silu(x):
    return x / (1.0 + jnp.exp(-x))


def _modulate(x, g, shift, scale):
    ms = jnp.mean(x * x, axis=-1, keepdims=True)
    return x * lax.rsqrt(ms + EPS) * g * (1.0 + scale) + shift


def _norm(v, n, g):
    return v * lax.rsqrt(jnp.sum(v * v, axis=-1, keepdims=True) * (1.0 / n) + EPS) * g


SUBLANES = 8


def _rows_to_tiles(ref, x):
    tm = x.shape[0]
    for c in range(SUBLANES):
        ref[pl.ds(c, tm, stride=SUBLANES), :] = x[:, c * LANES:(c + 1) * LANES]


def _tiles_to_rows(ref, tm):
    return jnp.concatenate([ref[pl.ds(c, tm, stride=SUBLANES), :] for c in range(SUBLANES)], axis=1)


def _rope(x, c, sa, sb, k):
    return x * c + pltpu.roll(x, LANES - k, 1) * sa + pltpu.roll(x, k, 1) * sb


def _mod_kernel(c_ref, w_ref, b_ref, o_ref):
    sc = _silu(c_ref[...])
    o_ref[...] = jnp.dot(sc.astype(_BF), w_ref[...].astype(_BF),
                         preferred_element_type=_F32) + b_ref[...]


def _ada_mod(c, w, b):
    B, D = c.shape
    N = w.shape[1]
    tn = min(N, 1536)
    return pl.pallas_call(
        _mod_kernel,
        grid=(N // tn,),
        in_specs=[pl.BlockSpec((B, D), lambda j: (0, 0)),
                  pl.BlockSpec((D, tn), lambda j: (0, j)),
                  pl.BlockSpec((1, tn), lambda j: (0, j))],
        out_specs=pl.BlockSpec((B, tn), lambda j: (0, j)),
        out_shape=jax.ShapeDtypeStruct((B, N), _F32),
        compiler_params=_cparams(("arbitrary",)),
        name="ada_mod",
    )(c, w, b.reshape(1, N))


def _pre_even_kernel(x_ref, mod_ref, g_ref, w_in_ref, qn_ref, w_uq_ref, kvn_ref, w_uk_ref, w_uvt_ref,
                     w_vbt_ref, gains_ref, rope_ref, q_ref, k_ref, vt_ref, nq_scr, nk_scr):
    h = _modulate(x_ref[...], g_ref[...], mod_ref[0:1, :], mod_ref[1:2, :]).astype(_BF)
    y = jnp.dot(h, w_in_ref[...], preferred_element_type=_F32)
    cqn = _norm(y[:, 0:256], MLA_Q_RANK, qn_ref[...]).astype(_BF)
    qa = jnp.dot(cqn, w_uq_ref[...], preferred_element_type=_F32)
    ckvn = _norm(y[:, 256:384], MLA_KV_RANK, kvn_ref[...]).astype(_BF)
    kn = jnp.dot(ckvn, w_uk_ref[...], preferred_element_type=_F32)
    kpe = y[:, 384:512]
    nt_dims = (((1,), (1,)), ((), ()))
    nv = MLA_HEADS * MLA_V
    vt_ref[0:nv, :] = lax.dot_general(w_uvt_ref[...], ckvn, nt_dims,
                                      preferred_element_type=_F32).astype(_BF)
    vt_ref[nv:nv + LANES, :] = lax.dot_general(w_vbt_ref[...], h, nt_dims,
                                               preferred_element_type=_F32).astype(_BF)
    ca, sa = rope_ref[:, 0:128], rope_ref[:, 128:256]
    cb, sb = rope_ref[:, 256:384], rope_ref[:, 384:512]
    gqa, gka = gains_ref[0:1, :], gains_ref[1:2, :]
    gqb, gkb = gains_ref[2:3, :], gains_ref[3:4, :]
    na = MLA_NOPE + MLA_ROPE

    for hh in range(MLA_HEADS):
        sl = slice(hh * LANES, (hh + 1) * LANES)
        nq_scr[:, sl] = _norm(qa[:, sl], na, gqa)
        nk_scr[:, sl] = _norm(kn[:, sl] + kpe, na, gka)
    for hh in range(GQA_HEADS):
        src = slice(512 + hh * LANES, 512 + (hh + 1) * LANES)
        dst = slice((MLA_HEADS + hh) * LANES, (MLA_HEADS + hh + 1) * LANES)
        nq_scr[:, dst] = _norm(y[:, src], HEAD_DIM, gqb)
    for g in range(GQA_KV_HEADS):
        src = slice(1536 + g * LANES, 1536 + (g + 1) * LANES)
        dst = slice((MLA_HEADS + g) * LANES, (MLA_HEADS + g + 1) * LANES)
        nk_scr[:, dst] = _norm(y[:, src], HEAD_DIM, gkb)

    @pl.when(pl.program_id(1) >= 0)
    def _():
        def rope(ref, out, n_a, n_all):
            for hh in range(n_all):
                sl = slice(hh * LANES, (hh + 1) * LANES)
                c, s = (ca, sa) if hh < n_a else (cb, sb)
                v = ref[:, sl]
                out[:, sl] = (v * c + pltpu.roll(v, 64, 1) * s).astype(_BF)

        rope(nq_scr, q_ref, MLA_HEADS, MLA_HEADS + GQA_HEADS)
        rope(nk_scr, k_ref, MLA_HEADS, MLA_HEADS + GQA_KV_HEADS)


def _pre_even(x, mod, g, w_in, qn, w_uq, kvn, w_uk, w_uvt, w_vbt, gains, rope, tm):
    B, S, D = x.shape
    nq = (MLA_HEADS + GQA_HEADS) * LANES
    nk = (MLA_HEADS + GQA_KV_HEADS) * LANES
    nvt = MLA_HEADS * MLA_V + LANES
    full = lambda a: pl.BlockSpec(a.shape, lambda b, i: (0,) * a.ndim)
    tok = lambda w: pl.BlockSpec((None, tm, w), lambda b, i: (b, i, 0))
    return pl.pallas_call(
        _pre_even_kernel,
        grid=(B, S // tm),
        in_specs=[tok(D),
                  pl.BlockSpec((None, 6, D), lambda b, i: (b, 0, 0)),
                  full(g), full(w_in), full(qn), full(w_uq), full(kvn), full(w_uk), full(w_uvt),
                  full(w_vbt), full(gains),
                  pl.BlockSpec((tm, rope.shape[1]), lambda b, i: (i, 0))],
        out_specs=[tok(nq), tok(nk), pl.BlockSpec((None, None, nvt, tm), lambda b, i: (b, i, 0, 0))],
        out_shape=[jax.ShapeDtypeStruct((B, S, nq), _BF), jax.ShapeDtypeStruct((B, S, nk), _BF),
                   jax.ShapeDtypeStruct((B, S // tm, nvt, tm), _BF)],
        scratch_shapes=[pltpu.VMEM((tm, nq), _F32), pltpu.VMEM((tm, nk), _F32)],
        compiler_params=_cparams(("parallel", "parallel")),
        name="pre_even",
    )(x, mod, g, w_in, qn, w_uq, kvn, w_uk, w_uvt, w_vbt, gains, rope)


def _attn_kernel(q_ref, k_ref, vt_ref, o_ref, s_scr, *, shared_kv):
    tq = q_ref.shape[0]
    nk, _, tk = vt_ref.shape
    qs = [q_ref[:, hh * LANES:(hh + 1) * LANES] for hh in range(2)]

    def scores(j, hh):
        r0 = j * tk if isinstance(j, int) else pl.multiple_of(j * tk, tk)
        kc = 0 if shared_kv else hh * LANES
        k = k_ref[pl.ds(r0, tk), kc:kc + LANES]
        return lax.dot_general(k, qs[hh], (((1,), (1,)), ((), ())), preferred_element_type=_F32)

    def step(j, hh, carry, cur, last=False):
        m, l, acc = carry
        if not last:
            s_scr[hh, 1 - cur] = scores(j + 1, hh)
        st = s_scr[hh, cur]
        m_new = jnp.maximum(m, jnp.max(st, axis=0, keepdims=True))
        alpha = jnp.exp2(m - m_new)
        p = jnp.exp2(st - m_new)
        l = alpha * l + jnp.sum(p, axis=0, keepdims=True)
        acc = alpha * acc + jnp.dot(vt_ref[j], p.astype(_BF), preferred_element_type=_F32)
        return m_new, l, acc

    def body(jj, carry):
        c = list(carry)
        for sub in range(2):
            for hh in range(2):
                c[hh] = step(2 * jj + sub, hh, c[hh], sub)
        return tuple(c)

    for hh in range(2):
        s_scr[hh, 0] = scores(0, hh)
    init1 = (jnp.full((1, tq), NEG, _F32), jnp.zeros((1, tq), _F32), jnp.zeros((LANES, tq), _F32))
    res = lax.fori_loop(0, nk // 2 - 1, body, (init1, init1))
    res = list(res)
    for sub in range(2):
        for hh in range(2):
            res[hh] = step(nk - 2 + sub, hh, res[hh], sub, last=(sub == 1))
    halves = []
    for hh, (_, l, acc) in enumerate(res):
        o = acc / l
        if shared_kv:
            hi = (pl.program_id(1) // 2) == 1
            halves.append(jnp.where(hi, o[64:, :], o[:64, :]))
        else:
            halves.append(o[hh * 64:(hh + 1) * 64, :])
    o_ref[...] = jnp.concatenate(halves, axis=0).T.astype(o_ref.dtype)


def _attention(q, k, vt, *, q_off, k_off, shared_kv, tq):
    B, S, _ = q.shape
    nk, _, tk = vt.shape[1:]
    assert nk % 2 == 0
    npairs = 4
    if shared_kv:
        kspec = pl.BlockSpec((None, S, LANES), lambda b, p, i: (b, 0, k_off + p // 2))
        vspec = pl.BlockSpec((None, nk, LANES, tk), lambda b, p, i: (b, 0, npairs, 0))
    else:
        kspec = pl.BlockSpec((None, S, 2 * LANES), lambda b, p, i: (b, 0, k_off // 2 + p))
        vspec = pl.BlockSpec((None, nk, LANES, tk), lambda b, p, i: (b, 0, p, 0))
    return pl.pallas_call(
        functools.partial(_attn_kernel, shared_kv=shared_kv),
        grid=(B, npairs, S // tq),
        in_specs=[pl.BlockSpec((None, tq, 2 * LANES), lambda b, p, i: (b, i, q_off // 2 + p)),
                  kspec, vspec],
        out_specs=pl.BlockSpec((None, tq, LANES), lambda b, p, i: (b, i, p)),
        out_shape=jax.ShapeDtypeStruct((B, S, npairs * LANES), _BF),
        scratch_shapes=[pltpu.VMEM((2, 2, tk, tq), _F32)],
        compiler_params=_cparams(("parallel", "parallel", "parallel")),
        name="attn_gqa" if shared_kv else "attn_mla",
    )(q, k, vt)


def _oproj_kernel(*refs, n_in, route):
    x_ref, mod_ref, g_ref, w_ref = refs[0], refs[1], refs[2], refs[3]
    o_refs = refs[4:4 + n_in]
    pos = 4 + n_in
    if route:
        r_ref = refs[pos]
        pos += 1
    x_out, h_out = refs[pos], refs[pos + 1]
    y = None
    off = 0
    for o_ref in o_refs:
        if len(o_ref.shape) == 3:
            o = jnp.concatenate([o_ref[p] for p in range(o_ref.shape[0])], axis=1)
        else:
            o = o_ref[...]
        w = o.shape[1]
        t = jnp.dot(o, w_ref[off:off + w, :], preferred_element_type=_F32)
        y = t if y is None else y + t
        off += w
    x1 = x_ref[...] + mod_ref[2:3, :] * y
    x_out[...] = x1
    h = _modulate(x1, g_ref[...], mod_ref[3:4, :], mod_ref[4:5, :])
    if route:
        _rows_to_tiles(h_out, h)
    else:
        h_out[...] = h.astype(h_out.dtype)
    if route:
        route_out = refs[pos + 2]
        hh = h.astype(_BF)
        hl = (h - hh.astype(_F32)).astype(_BF)
        rh, rl = r_ref[0], r_ref[1]
        logits = (jnp.dot(hh, rh, preferred_element_type=_F32)
                  + jnp.dot(hh, rl, preferred_element_type=_F32)
                  + jnp.dot(hl, rh, preferred_element_type=_F32))
        tm = logits.shape[0]
        lane = lax.broadcasted_iota(jnp.int32, (tm, LANES), 1)
        lg = jnp.where(lane < N_EXPERTS, logits, NEG)
        m1 = jnp.max(lg, axis=-1, keepdims=True)
        lanef = lane.astype(_F32)
        i1 = jnp.min(jnp.where(lg == m1, lanef, float(LANES)), axis=-1, keepdims=True)
        lg2 = jnp.where(lanef == i1, NEG, lg)
        m2 = jnp.max(lg2, axis=-1, keepdims=True)
        i2 = jnp.min(jnp.where(lg2 == m2, lanef, float(LANES)), axis=-1, keepdims=True)
        e = jnp.exp(m2 - m1)
        g1 = 1.0 / (1.0 + e)
        g2 = e / (1.0 + e)
        route_out[...] = jnp.where(lane == 0, i1,
                                   jnp.where(lane == 1, i2,
                                             jnp.where(lane == 2, g1, jnp.where(lane == 3, g2, 0.0))))


def _oproj(x, mod, g, w, o_list, tm, h_dtype, router=None):
    B, S, D = x.shape
    n_in = len(o_list)
    route = router is not None
    tok = lambda wd: pl.BlockSpec((None, tm, wd), lambda b, i: (b, i, 0))
    in_specs = [tok(D), pl.BlockSpec((None, 6, D), lambda b, i: (b, 0, 0)),
                pl.BlockSpec((1, D), lambda b, i: (0, 0)),
                pl.BlockSpec(w.shape, lambda b, i: (0, 0))]
    for o in o_list:
        if o.ndim == 4:
            in_specs.append(pl.BlockSpec((None, o.shape[1], tm, LANES), lambda b, i: (b, 0, i, 0)))
        else:
            in_specs.append(tok(o.shape[2]))
    args = [x, mod, g, w] + list(o_list)
    out_specs = [tok(D), tok(D)]
    out_shape = [jax.ShapeDtypeStruct((B, S, D), _F32), jax.ShapeDtypeStruct((B, S, D), h_dtype)]
    if route:
        assert D == SUBLANES * LANES
        per_b = S // tm
        out_specs[1] = pl.BlockSpec((tm * SUBLANES, LANES), lambda b, i: (b * per_b + i, 0))
        out_shape[1] = jax.ShapeDtypeStruct((B * S * SUBLANES, LANES), _F32)
        in_specs.append(pl.BlockSpec(router.shape, lambda b, i: (0, 0, 0)))
        args.append(router)
        out_specs.append(tok(LANES))
        out_shape.append(jax.ShapeDtypeStruct((B, S, LANES), _F32))
    return pl.pallas_call(
        functools.partial(_oproj_kernel, n_in=n_in, route=route),
        grid=(B, S // tm),
        in_specs=in_specs, out_specs=out_specs, out_shape=out_shape,
        compiler_params=_cparams(("parallel", "parallel")),
        name="oproj_route" if route else "oproj",
    )(*args)


def _swiglu_hidden(x, wg_ref, wu_ref, a_scr, tf):
    F = wg_ref.shape[1]
    for c0 in range(0, F, tf):
        c1 = min(c0 + tf, F)
        g = jnp.dot(x, wg_ref[:, c0:c1], preferred_element_type=_F32)
        u = jnp.dot(x, wu_ref[:, c0:c1], preferred_element_type=_F32)
        a_scr[:, c0:c1] = (_silu(g) * u).astype(_BF)


def _ffn_kernel(x_ref, h_ref, mod_ref, modn_ref, gn_ref, wg_ref, wu_ref, wd_ref, x_out, h_out, a_scr, *, tf):
    _swiglu_hidden(h_ref[...], wg_ref, wu_ref, a_scr, tf)
    y = jnp.dot(a_scr[...], wd_ref[...], preferred_element_type=_F32)
    x2 = x_ref[...] + mod_ref[5:6, :] * y
    x_out[...] = x2
    h_out[...] = _modulate(x2, gn_ref[...], modn_ref[0:1, :], modn_ref[1:2, :]).astype(h_out.dtype)


def _ffn(x, h, mod, modn, gn, wg, wu, wd, tm, tf):
    B, S, D = x.shape
    T = B * S
    F = wg.shape[1]
    per_b = S // tm
    x2 = x.reshape(T, D)
    h2 = h.reshape(T, D)
    tok = pl.BlockSpec((tm, D), lambda i: (i, 0))
    modspec = pl.BlockSpec((None, 6, D), lambda i: (i // per_b, 0, 0))
    once = pl.Buffered(1)
    xo, ho = pl.pallas_call(
        functools.partial(_ffn_kernel, tf=tf),
        grid=(T // tm,),
        in_specs=[tok, tok, modspec, modspec, pl.BlockSpec((1, D), lambda i: (0, 0)),
                  pl.BlockSpec((D, F), lambda i: (0, 0), pipeline_mode=once),
                  pl.BlockSpec((D, F), lambda i: (0, 0), pipeline_mode=once),
                  pl.BlockSpec((F, D), lambda i: (0, 0), pipeline_mode=once)],
        out_specs=[tok, tok],
        out_shape=[jax.ShapeDtypeStruct((T, D), _F32), jax.ShapeDtypeStruct((T, D), _BF)],
        scratch_shapes=[pltpu.VMEM((tm, F), _BF)],
        compiler_params=_cparams(("parallel",)),
        name="ffn_dense",
    )(x2, h2, mod, modn, gn, wg, wu, wd)
    return xo.reshape(B, S, D), ho.reshape(B, S, D)


def _qkv_kernel(h_ref, w_ref, gains_ref, rope_ref, q_ref, k_ref, v_ref, n_scr):
    tm, D = h_ref.shape
    h = h_ref[...]
    lane = lax.broadcasted_iota(jnp.int32, (tm, LANES), 1)
    head_a = (lane & 63) < 32

    def head_pair_norm(v, g):
        sq = v * v
        s_a = jnp.sum(jnp.where(head_a, sq, 0.0), axis=-1, keepdims=True)
        s_b = jnp.sum(jnp.where(head_a, 0.0, sq), axis=-1, keepdims=True)
        r = lax.rsqrt(jnp.where(head_a, s_a, s_b) * (1.0 / HEAD_DIM) + EPS)
        return v * r * g

    y = jnp.dot(h, w_ref[...], preferred_element_type=_F32)
    for p in range(D // LANES):
        sl = slice(p * LANES, (p + 1) * LANES)
        n_scr[:, sl] = head_pair_norm(y[:, sl], gains_ref[0:1, :])
        n_scr[:, D + p * LANES:D + (p + 1) * LANES] = head_pair_norm(
            y[:, D + p * LANES:D + (p + 1) * LANES], gains_ref[1:2, :])
        v_ref[p] = y[:, 2 * D + p * LANES:2 * D + (p + 1) * LANES].astype(_BF)

    @pl.when(pl.program_id(1) >= 0)
    def _():
        c, s = rope_ref[:, 0:128], rope_ref[:, 128:256]
        for p in range(D // LANES):
            vq = n_scr[:, p * LANES:(p + 1) * LANES]
            vk = n_scr[:, D + p * LANES:D + (p + 1) * LANES]
            q_ref[p] = (vq * c + pltpu.roll(vq, 64, 1) * s).astype(_BF)
            k_ref[p] = (vk * c + pltpu.roll(vk, 64, 1) * s).astype(_BF)


def _qkv(h, w, gains, rope, tm):
    B, S, D = h.shape
    tok = pl.BlockSpec((None, tm, D), lambda b, i: (b, i, 0))
    hp = pl.BlockSpec((None, D // LANES, tm, LANES), lambda b, i: (b, 0, i, 0))
    sds = jax.ShapeDtypeStruct((B, D // LANES, S, LANES), _BF)
    return pl.pallas_call(
        _qkv_kernel,
        grid=(B, S // tm),
        in_specs=[tok, pl.BlockSpec(w.shape, lambda b, i: (0, 0)),
                  pl.BlockSpec((2, LANES), lambda b, i: (0, 0)),
                  pl.BlockSpec((tm, 2 * LANES), lambda b, i: (i, 0))],
        out_specs=[hp, hp, hp],
        out_shape=[sds, sds, sds],
        scratch_shapes=[pltpu.VMEM((tm, 2 * D), _F32)],
        compiler_params=_cparams(("parallel", "parallel")),
        name="qkv_dil",
    )(h, w, gains, rope)


DIL_GROUP = 4


def _dil_kernel(q_ref, k_ref, v_ref, o_ref, tmp, qp, kp, vp, acc, mm, ll, accp, mmp, llp, bias_scr, *, tl):
    S = q_ref.shape[0]
    ng = S // DIL_GROUP
    pitch = DIL_GROUP + 1

    def spread(dst, src):
        for j in range(DIL_GROUP):
            dst[pl.ds(j, ng, stride=pitch), :] = src[pl.ds(j, ng, stride=DIL_GROUP), :]

    for src, dst in ((q_ref, qp), (k_ref, kp), (v_ref, vp)):
        tmp[...] = src[...].astype(_F32)
        spread(dst, tmp)
    lane = lax.broadcasted_iota(jnp.int32, (tl, LANES), 1)
    lo = lane < 64
    head_a = (lane & 63) < 32
    for bi, (window, d) in enumerate(DIL_PATTERNS):
        L = S // d
        half = window // (2 * d)
        t = min(tl, L)
        W = min(t + 2 * half, L)
        nt = L // t
        if d > 1:
            assert d % DIL_GROUP == 0
        sd = d * pitch // DIL_GROUP

        def rows(r, first, n, d=d, sd=sd):
            if d == 1:
                return pl.ds(first, n)
            return pl.ds(r + r // DIL_GROUP + sd * first, n, stride=sd)

        assert t % half == 0 and W <= t + 2 * half
        qa = lax.broadcasted_iota(jnp.int32, (t, W), 0)
        kb = lax.broadcasted_iota(jnp.int32, (t, W), 1)
        for case in range(3):
            ok = jnp.abs(kb - qa - case * half) <= half
            bias_scr[case, 0:t, 0:W] = jnp.where(ok, 0.0, NEG)

        def body(idx, carry, bi=bi, d=d, L=L, half=half, t=t, W=W, nt=nt, rows=rows):
            r = idx // nt
            l0 = (idx % nt) * t
            start = jnp.clip(l0 - half, 0, L - W)
            if d == 1:
                qt = q_ref[pl.ds(pl.multiple_of(l0, t), t), :]
                kw = k_ref[pl.ds(pl.multiple_of(start, 64), W), :]
                vw = v_ref[pl.ds(pl.multiple_of(start, 64), W), :]
            else:
                qt = qp[rows(r, l0, t), :].astype(_BF)
                kw = kp[rows(r, start, W), :].astype(_BF)
                vw = vp[rows(r, start, W), :].astype(_BF)
            lo_t = lo[:t]
            qa_t = head_a[:t]
            zq = jnp.zeros_like(qt)
            q2 = jnp.concatenate([jnp.where(qa_t, qt, zq), jnp.where(qa_t, zq, qt)], axis=0)
            s = lax.dot_general(q2, kw, (((1,), (1,)), ((), ())), preferred_element_type=_F32)
            bias = bias_scr[(l0 - start) // half, 0:t, 0:W]
            s = s + jnp.concatenate([bias, bias], axis=0)
            m = jnp.max(s, axis=-1, keepdims=True)
            p = jnp.exp2(s - m)
            den = jnp.sum(p, axis=-1, keepdims=True)
            o2 = jnp.dot(p.astype(_BF), vw, preferred_element_type=_F32)
            o_new = jnp.where(lo_t, o2[:t], o2[t:])
            m_new = jnp.where(lo_t, m[:t], m[t:])
            l_new = jnp.where(lo_t, den[:t], den[t:])
            tok = rows(r, l0, t)
            if d == 1:
                assert bi == 0
                acc[tok, :] = o_new
                mm[tok, :] = m_new
                ll[tok, :] = l_new
            else:
                m_old = mmp[tok, :]
                mx = jnp.maximum(m_old, m_new)
                a_old = jnp.exp2(m_old - mx)
                a_new = jnp.exp2(m_new - mx)
                accp[tok, :] = accp[tok, :] * a_old + o_new * a_new
                llp[tok, :] = llp[tok, :] * a_old + l_new * a_new
                mmp[tok, :] = mx
            return carry

        lax.fori_loop(0, d * nt, body, 0, unroll=min(8, d * nt))
        if bi == 0:
            spread(accp, acc)
            spread(mmp, mm)
            spread(llp, ll)
    for j in range(DIL_GROUP):
        grp = pl.ds(j, ng, stride=pitch)
        tmp[pl.ds(j, ng, stride=DIL_GROUP), :] = accp[grp, :] / llp[grp, :]
    o_ref[...] = tmp[...].astype(o_ref.dtype)


def _dilated(q, k, v, tl):
    B, P, S, _ = q.shape
    spec = pl.BlockSpec((None, None, S, LANES), lambda b, p: (b, p, 0, 0))
    scr = pltpu.VMEM((S, LANES), _F32)
    scrp = pltpu.VMEM((S // DIL_GROUP * (DIL_GROUP + 1), LANES), _F32)
    return pl.pallas_call(
        functools.partial(_dil_kernel, tl=tl),
        grid=(B, P),
        in_specs=[spec, spec, spec],
        out_specs=spec,
        out_shape=jax.ShapeDtypeStruct((B, P, S, LANES), _BF),
        scratch_shapes=[scr, scrp, scrp, scrp, scr, scr, scr, scrp, scrp, scrp,
                        pltpu.VMEM((3, tl, tl + max(w // d for w, d in DIL_PATTERNS)), _F32)],
        compiler_params=_cparams(("parallel", "parallel")),
        name="dil_attn",
    )(q, k, v)


def _moe_kernel(te_ref, tv_ref, src_hbm, dst_hbm, h_hbm, wg_ref, wu_ref, wd_ref, out_hbm,
                xbuf, xbf, a_scr, ybuf, src_s0, src_s1, dst_s, gsem, ssem, isem, *, tf, dump0, n_dump):
    i = pl.program_id(0)
    nt = pl.num_programs(0)
    tm = xbf.shape[0]
    slot = i % 2

    def is_valid(t):
        return jnp.logical_and(jnp.logical_and(t >= 0, t < nt), tv_ref[jnp.clip(t, 0, nt - 1)] == 1)

    valid = is_valid(i)
    prev_valid = is_valid(i - 1)
    nxt_valid = is_valid(i + 1)

    src_bufs = (src_s0, src_s1)

    def src_copy(tile, par):
        return pltpu.make_async_copy(src_hbm.at[tile], src_bufs[par], isem.at[par])

    def dst_copy(tile):
        return pltpu.make_async_copy(dst_hbm.at[tile], dst_s, isem.at[2])

    def tile_at(first):
        return pl.ds(pl.multiple_of(first, SUBLANES), SUBLANES)

    def gather_row(r, par):
        return pltpu.make_async_copy(h_hbm.at[tile_at(src_bufs[par][r])],
                                     xbuf.at[par, tile_at(r * SUBLANES)], gsem.at[par])

    def scatter_row(r):
        return pltpu.make_async_copy(ybuf.at[tile_at(r * SUBLANES)], out_hbm.at[tile_at(dst_s[r])],
                                     ssem.at[0])

    def gather_wait(slot_):
        return pltpu.make_async_copy(h_hbm.at[pl.ds(0, tm * SUBLANES)], xbuf.at[slot_], gsem.at[slot_])

    def scatter_wait():
        return pltpu.make_async_copy(ybuf, out_hbm.at[pl.ds(0, tm * SUBLANES)], ssem.at[0])

    def for_rows(fn):
        def body(r, c):
            fn(r)
            return c
        lax.fori_loop(0, tm, body, 0, unroll=8)

    @pl.when(i == 0)
    def _():
        ybuf[...] = jnp.zeros_like(ybuf)
        for k in range(n_dump):
            pltpu.make_async_copy(ybuf, out_hbm.at[pl.ds((dump0 + k * tm) * SUBLANES, tm * SUBLANES)],
                                  ssem.at[0]).start()
        for k in range(n_dump):
            scatter_wait().wait()

    @pl.when(jnp.logical_and(i == 0, valid))
    def _():
        src_copy(0, 0).start()
        src_copy(0, 0).wait()
        for_rows(lambda r: gather_row(r, 0).start())

    @pl.when(jnp.logical_and(i == 0, nxt_valid))
    def _():
        src_copy(1, 1).start()

    @pl.when(valid)
    def _():
        dst_copy(i).start()

    for par in range(2):
        @pl.when(jnp.logical_and(nxt_valid, slot == par))
        def _(par=par):
            src_copy(i + 1, 1 - par).wait()
            for_rows(lambda r: gather_row(r, 1 - par).start())

    for par in range(2):
        @pl.when(jnp.logical_and(valid, slot == par))
        def _(par=par):
            gather_wait(par).wait()
            xbf[...] = _tiles_to_rows(xbuf.at[par], tm).astype(_BF)

    @pl.when(valid)
    def _():
        _swiglu_hidden(xbf[...], wg_ref, wu_ref, a_scr, tf)

    @pl.when(prev_valid)
    def _():
        scatter_wait().wait()

    @pl.when(valid)
    def _():
        _rows_to_tiles(ybuf, jnp.dot(a_scr[...], wd_ref[...], preferred_element_type=_F32))
        dst_copy(i).wait()
        for_rows(lambda r: scatter_row(r).start())

    for par in range(2):
        @pl.when(jnp.logical_and(is_valid(i + 2), slot == par))
        def _(par=par):
            src_copy(i + 2, par).start()

    @pl.when(jnp.logical_and(i == nt - 1, valid))
    def _():
        scatter_wait().wait()


def _moe(h, tile_e, tile_v, src, dst, wg, wu, wd, n_out_rows, tm, tf):
    T = h.shape[0] // SUBLANES
    D = wg.shape[1]
    nt = tile_e.shape[0]
    F = wg.shape[2]
    dump0 = T * TOP_K
    once = pl.Buffered(1)
    grid_spec = pltpu.PrefetchScalarGridSpec(
        num_scalar_prefetch=2,
        grid=(nt,),
        in_specs=[pl.BlockSpec(memory_space=pl.ANY), pl.BlockSpec(memory_space=pl.ANY),
                  pl.BlockSpec(memory_space=pl.ANY),
                  pl.BlockSpec((None, D, F), lambda i, te, tv: (te[i], 0, 0), pipeline_mode=once),
                  pl.BlockSpec((None, D, F), lambda i, te, tv: (te[i], 0, 0), pipeline_mode=once),
                  pl.BlockSpec((None, F, D), lambda i, te, tv: (te[i], 0, 0), pipeline_mode=once)],
        out_specs=pl.BlockSpec(memory_space=pl.ANY),
        scratch_shapes=[pltpu.VMEM((2, tm * SUBLANES, LANES), _F32), pltpu.VMEM((tm, D), _BF),
                        pltpu.VMEM((tm, F), _BF), pltpu.VMEM((tm * SUBLANES, LANES), _F32),
                        pltpu.SMEM((tm,), jnp.int32), pltpu.SMEM((tm,), jnp.int32),
                        pltpu.SMEM((tm,), jnp.int32),
                        pltpu.SemaphoreType.DMA((2,)), pltpu.SemaphoreType.DMA((1,)),
                        pltpu.SemaphoreType.DMA((3,))],
    )
    return pl.pallas_call(
        functools.partial(_moe_kernel, tf=tf, dump0=dump0, n_dump=(n_out_rows - dump0) // tm),
        grid_spec=grid_spec,
        out_shape=jax.ShapeDtypeStruct((n_out_rows * SUBLANES, LANES), _F32),
        compiler_params=_cparams(("arbitrary",)),
        name="moe_experts",
    )(tile_e, tile_v, src, dst, h, wg, wu, wd)


def _route_plan(route, T, tm):
    A = T * TOP_K
    e_flat = route[:, :TOP_K].astype(jnp.int32).reshape(A)
    order = jnp.argsort(e_flat, stable=True).astype(jnp.int32)
    counts = jnp.sum((e_flat[:, None] == jnp.arange(N_EXPERTS, dtype=jnp.int32)[None, :]).astype(jnp.int32), axis=0)
    starts = jnp.cumsum(counts) - counts
    pcounts = (counts + tm - 1) // tm * tm
    pends = jnp.cumsum(pcounts)
    pstarts = pends - pcounts
    nt = A // tm + N_EXPERTS
    tile0 = jnp.arange(nt, dtype=jnp.int32) * tm
    tile_v = (tile0 < pends[-1]).astype(jnp.int32)
    last_valid = jnp.maximum(pends[-1] // tm - 1, 0)
    tile_e_raw = jnp.minimum(jnp.searchsorted(pends, tile0, side="right"), N_EXPERTS - 1).astype(jnp.int32)
    tile_e = jnp.where(tile_v == 1, tile_e_raw, tile_e_raw[last_valid])
    r = jnp.arange(nt * tm, dtype=jnp.int32)
    e_r = jnp.repeat(tile_e, tm)
    within = r - pstarts[e_r]
    ok = jnp.logical_and(within < counts[e_r], jnp.repeat(tile_v, tm) == 1)
    a = order[jnp.clip(starts[e_r] + within, 0, A - 1)]
    src = jnp.where(ok, a // TOP_K, 0).astype(jnp.int32)
    dump = A + e_r * tm + jnp.clip(within - counts[e_r], 0, tm - 1)
    dst = jnp.where(ok, (a % TOP_K) * T + a // TOP_K, dump).astype(jnp.int32)
    return (tile_e, tile_v, (src * SUBLANES).reshape(nt, tm), (dst * SUBLANES).reshape(nt, tm),
            A + N_EXPERTS * tm)


def _combine_kernel(x_ref, y1_ref, y2_ref, r_ref, mod_ref, o_ref):
    tm = x_ref.shape[0]
    y = _tiles_to_rows(y1_ref, tm) * r_ref[:, 2:3] + _tiles_to_rows(y2_ref, tm) * r_ref[:, 3:4]
    o_ref[...] = x_ref[...] + mod_ref[5:6, :] * y


def _combine(x, y2, route, mod, tm):
    B, S, D = x.shape
    T = B * S
    per_b = S // tm
    out = pl.pallas_call(
        _combine_kernel,
        grid=(T // tm,),
        in_specs=[pl.BlockSpec((tm, D), lambda i: (i, 0)),
                  pl.BlockSpec((tm * SUBLANES, LANES), lambda i: (i, 0)),
                  pl.BlockSpec((tm * SUBLANES, LANES), lambda i: (T // tm + i, 0)),
                  pl.BlockSpec((tm, LANES), lambda i: (i, 0)),
                  pl.BlockSpec((None, 6, D), lambda i: (i // per_b, 0, 0))],
        out_specs=pl.BlockSpec((tm, D), lambda i: (i, 0)),
        out_shape=jax.ShapeDtypeStruct((T, D), _F32),
        compiler_params=_cparams(("parallel",)),
        name="moe_combine",
    )(x.reshape(T, D), y2, y2, route.reshape(T, LANES), mod)
    return out.reshape(B, S, D)


def _pad_cols(a, w):
    return jnp.pad(a, ((0, 0), (0, w - a.shape[1])))


def _rope_tables_even(S):
    pos = jnp.arange(S, dtype=jnp.int32)
    inv = ROPE_THETA ** (-jnp.arange(0, 32, 2, dtype=_F32) / 32)
    def cs(p):
        ang = p.astype(_F32)[:, None] * inv[None, :]
        return jnp.cos(ang), jnp.sin(ang)
    one = lambda w: jnp.ones((S, w), _F32)
    zero = lambda w: jnp.zeros((S, w), _F32)
    c, s = cs(pos)
    ca = jnp.concatenate([c, one(48), c, one(48)], 1)
    sa = jnp.concatenate([-s, zero(48), s, zero(48)], 1)
    cr, sr = cs(pos // GRID_W)
    cc, sc = cs(pos % GRID_W)
    cb = jnp.concatenate([cr, cc, one(32), cr, cc, one(32)], 1)
    sb = jnp.concatenate([-sr, -sc, zero(32), sr, sc, zero(32)], 1)
    return jnp.concatenate([ca, sa, cb, sb], 1)


def _slot_maps():
    r = MLA_ROPE // 2
    mla = ([MLA_NOPE + i for i in range(r)] + list(range(0, 64 - r))
           + [MLA_NOPE + r + i for i in range(r)] + list(range(64 - r, MLA_NOPE)))
    mla += [-1] * (LANES - len(mla))
    q = HEAD_DIM // 4
    gqa = (list(range(0, q)) + list(range(2 * q, 3 * q)) + [-1] * (64 - 2 * q)
           + list(range(q, 2 * q)) + list(range(3 * q, 4 * q)) + [-1] * (64 - 2 * q))
    return mla, gqa


def _to_slots(a, lane_map):
    idx = jnp.asarray([max(i, 0) for i in lane_map], jnp.int32)
    keep = jnp.asarray([1.0 if i >= 0 else 0.0 for i in lane_map], a.dtype)
    return jnp.take(a, idx, axis=-1) * keep


def _rope_tables_odd(S):
    pos = jnp.arange(S, dtype=_F32)
    inv = ROPE_THETA ** (-jnp.arange(0, HEAD_DIM, 2, dtype=_F32) / HEAD_DIM)
    ang = pos[:, None] * inv[None, :]
    c, s = jnp.cos(ang), jnp.sin(ang)
    return jnp.concatenate([c, c, c, c, -s, -s, s, s], 1)


def _tiles(S):
    return dict(tm_pre=min(S, 512), tq=min(S, 2048), tm_o=min(S, 512),
                tm_ffn=min(S, 512), tf_ffn=512, tm_qkv=min(S, 512),
                tl=128, tm_moe=min(S, 512), tf_moe=512, tm_c=min(S, 512))


def kernel(x, c, ada_even_w, ada_even_b, norm_even_mix, norm_even_ffn, even_w_in, mla_q_norm, mla_w_uq, mla_kv_norm, mla_w_ukv, mla_q_gain, mla_k_gain, gqa_q_gain, gqa_k_gain, even_w_out, ffn_w_gate, ffn_w_up, ffn_w_down, ada_odd_w, ada_odd_b, norm_odd_mix, norm_odd_ffn, dil_w_qkv, dil_q_gain, dil_k_gain, dil_w_out, moe_router, moe_w_gate, moe_w_up, moe_w_down):
    B, S, D = x.shape
    T = B * S
    cfg = _tiles(S)

    mod_e = _ada_mod(c, ada_even_w[0], ada_even_b[0]).reshape(B, 6, D)
    mod_o = _ada_mod(c, ada_odd_w[0], ada_odd_b[0]).reshape(B, 6, D)

    w = even_w_in[0]
    sp = [MLA_Q_RANK, MLA_Q_RANK + MLA_KV_RANK, MLA_Q_RANK + MLA_KV_RANK + MLA_ROPE]
    sp.append(sp[-1] + GQA_HEADS * HEAD_DIM)
    sp.append(sp[-1] + GQA_KV_HEADS * HEAD_DIM)
    w_cq, w_ckv, w_kpe = w[:, :sp[0]], w[:, sp[0]:sp[1]], w[:, sp[1]:sp[2]]
    w_qb, w_kb, w_vb = w[:, sp[2]:sp[3]], w[:, sp[3]:sp[4]], w[:, sp[4]:]
    mla_map, gqa_map = _slot_maps()
    na = MLA_NOPE + MLA_ROPE
    nope_only = [i if 0 <= i < MLA_NOPE else -1 for i in mla_map]
    rope_only = [i - MLA_NOPE if i >= MLA_NOPE else -1 for i in mla_map]
    gslots = lambda a, n: _to_slots(a.reshape(D, n, HEAD_DIM), gqa_map).reshape(D, n * LANES)
    w_in = jnp.concatenate([w_cq, w_ckv, _to_slots(w_kpe, rope_only), gslots(w_qb, GQA_HEADS),
                            gslots(w_kb, GQA_KV_HEADS)], axis=1).astype(_BF)
    w_uq = _to_slots(mla_w_uq[0].reshape(MLA_Q_RANK, MLA_HEADS, na), mla_map)
    w_uq = w_uq.reshape(MLA_Q_RANK, MLA_HEADS * LANES).astype(_BF)
    w_ukv = mla_w_ukv[0].reshape(MLA_KV_RANK, MLA_HEADS, MLA_NOPE + MLA_V)
    w_uk = _to_slots(w_ukv[:, :, :MLA_NOPE], nope_only).reshape(MLA_KV_RANK, MLA_HEADS * LANES).astype(_BF)
    w_uvt = w_ukv[:, :, MLA_NOPE:].reshape(MLA_KV_RANK, MLA_HEADS * MLA_V).T.astype(_BF)
    w_vbt = w_vb.T.astype(_BF)
    gains_e = jnp.stack([_to_slots(mla_q_gain[0], mla_map) * (na ** -0.5 * LOG2E),
                         _to_slots(mla_k_gain[0], mla_map),
                         _to_slots(gqa_q_gain[0], gqa_map) * (HEAD_DIM ** -0.5 * LOG2E),
                         _to_slots(gqa_k_gain[0], gqa_map)])
    rope_e = _rope_tables_even(S)
    q_all, k_all, vt_all = _pre_even(
        x, mod_e, norm_even_mix[0].reshape(1, D), w_in, mla_q_norm[0].reshape(1, -1), w_uq,
        mla_kv_norm[0].reshape(1, -1), w_uk, w_uvt, w_vbt, gains_e, rope_e, cfg["tm_pre"])
    o_a = _attention(q_all, k_all, vt_all, q_off=0, k_off=0, shared_kv=False, tq=cfg["tq"])
    o_b = _attention(q_all, k_all, vt_all, q_off=MLA_HEADS, k_off=MLA_HEADS, shared_kv=True, tq=cfg["tq"])
    x1, h1 = _oproj(x, mod_e, norm_even_ffn[0].reshape(1, D), even_w_out[0].astype(_BF), [o_a, o_b],
                    cfg["tm_o"], _BF)
    x2, h2 = _ffn(x1, h1, mod_e, mod_o, norm_odd_mix[0].reshape(1, D), ffn_w_gate[0].astype(_BF),
                  ffn_w_up[0].astype(_BF), ffn_w_down[0].astype(_BF), cfg["tm_ffn"], cfg["tf_ffn"])

    hh = HEAD_DIM // 2
    pair = lambda v: jnp.concatenate([v[:hh], v[:hh], v[hh:], v[hh:]])
    gains_o = jnp.stack([pair(dil_q_gain[0]) * (HEAD_DIM ** -0.5 * LOG2E), pair(dil_k_gain[0])])
    wq, wk, wv = jnp.split(dil_w_qkv[0], 3, axis=1)
    perm = lambda w: w.reshape(D, D // LANES, 2, 2, hh).transpose(0, 1, 3, 2, 4).reshape(D, D)
    w_qkv = jnp.concatenate([perm(wq), perm(wk), wv], axis=1).astype(_BF)
    qd, kd, vd = _qkv(h2, w_qkv, gains_o, _rope_tables_odd(S), cfg["tm_qkv"])
    o_d = _dilated(qd, kd, vd, cfg["tl"])
    r32 = _pad_cols(moe_router[0], LANES)
    r_hi = r32.astype(_BF)
    r_lo = (r32 - r_hi.astype(_F32)).astype(_BF)
    x3, h3, route = _oproj(x2, mod_o, norm_odd_ffn[0].reshape(1, D), dil_w_out[0].astype(_BF), [o_d],
                           cfg["tm_o"], _F32, router=jnp.stack([r_hi, r_lo]))
    tm = cfg["tm_moe"]
    tile_e, tile_v, src, dst, n_rows = _route_plan(route.reshape(T, LANES), T, tm)
    y2 = _moe(h3, tile_e, tile_v, src, dst, moe_w_gate[0].astype(_BF),
              moe_w_up[0].astype(_BF), moe_w_down[0].astype(_BF), n_rows, tm, cfg["tf_moe"])
    return _combine(x3, y2, route, mod_o, cfg["tm_c"])
```

```python
import functools
import math

import jax
import jax.numpy as jnp
from jax import lax
from jax.experimental import pallas as pl
from jax.experimental.pallas import tpu as pltpu

_BF = jnp.bfloat16
_F32 = jnp.float32

GRID_W = 64
HEAD_DIM = 64
ROPE_THETA = 10000.0
EPS = 1e-6
MLA_HEADS = 8
MLA_Q_RANK = 256
MLA_KV_RANK = 128
MLA_NOPE = 64
MLA_ROPE = 32
MLA_V = 64
GQA_HEADS = 8
GQA_KV_HEADS = 2
DIL_PATTERNS = ((128, 1), (512, 4), (2048, 16))
N_EXPERTS = 8
TOP_K = 2
NEG = -1e30
LOG2E = math.log2(math.e)

LANES = 128
VMEM_LIMIT = 56 * 1024 * 1024


def _cparams(sem, vmem=VMEM_LIMIT):
    return pltpu.CompilerParams(dimension_semantics=sem, vmem_limit_bytes=vmem)


def _silu(x):
    return x / (1.0 + jnp.exp(-x))


def _modulate(x, g, shift, scale):
    ms = jnp.mean(x * x, axis=-1, keepdims=True)
    return x * lax.rsqrt(ms + EPS) * g * (1.0 + scale) + shift


def _norm(v, n, g):
    return v * lax.rsqrt(jnp.sum(v * v, axis=-1, keepdims=True) * (1.0 / n) + EPS) * g


SUBLANES = 8


def _rows_to_tiles(ref, x):
    tm = x.shape[0]
    for c in range(SUBLANES):
        ref[pl.ds(c, tm, stride=SUBLANES), :] = x[:, c * LANES:(c + 1) * LANES]


def _tiles_to_rows(ref, tm):
    return jnp.concatenate([ref[pl.ds(c, tm, stride=SUBLANES), :] for c in range(SUBLANES)], axis=1)


def _rope(x, c, sa, sb, k):
    return x * c + pltpu.roll(x, LANES - k, 1) * sa + pltpu.roll(x, k, 1) * sb


def _mod_kernel(c_ref, w_ref, b_ref, o_ref):
    sc = _silu(c_ref[...])
    o_ref[...] = jnp.dot(sc.astype(_BF), w_ref[...].astype(_BF),
                         preferred_element_type=_F32) + b_ref[...]


def _ada_mod(c, w, b):
    B, D = c.shape
    N = w.shape[1]
    tn = min(N, 1536)
    return pl.pallas_call(
        _mod_kernel,
        grid=(N // tn,),
        in_specs=[pl.BlockSpec((B, D), lambda j: (0, 0)),
                  pl.BlockSpec((D, tn), lambda j: (0, j)),
                  pl.BlockSpec((1, tn), lambda j: (0, j))],
        out_specs=pl.BlockSpec((B, tn), lambda j: (0, j)),
        out_shape=jax.ShapeDtypeStruct((B, N), _F32),
        compiler_params=_cparams(("arbitrary",)),
        name="ada_mod",
    )(c, w, b.reshape(1, N))


def _pre_even_kernel(x_ref, mod_ref, g_ref, w_in_ref, qn_ref, w_uq_ref, kvn_ref, w_uk_ref, w_uvt_ref,
                     w_vbt_ref, gains_ref, rope_ref, q_ref, k_ref, vt_ref, nq_scr, nk_scr):
    h = _modulate(x_ref[...], g_ref[...], mod_ref[0:1, :], mod_ref[1:2, :]).astype(_BF)
    y = jnp.dot(h, w_in_ref[...], preferred_element_type=_F32)
    cqn = _norm(y[:, 0:256], MLA_Q_RANK, qn_ref[...]).astype(_BF)
    qa = jnp.dot(cqn, w_uq_ref[...], preferred_element_type=_F32)
    ckvn = _norm(y[:, 256:384], MLA_KV_RANK, kvn_ref[...]).astype(_BF)
    kn = jnp.dot(ckvn, w_uk_ref[...], preferred_element_type=_F32)
    kpe = y[:, 384:512]
    nt_dims = (((1,), (1,)), ((), ()))
    vt_a = lax.dot_general(w_uvt_ref[...], ckvn, nt_dims, preferred_element_type=_F32).astype(_BF)
    vt_b = lax.dot_general(w_vbt_ref[...], h, nt_dims, preferred_element_type=_F32).astype(_BF)
    tm = x_ref.shape[0]
    ones_blk = jnp.where(lax.broadcasted_iota(jnp.int32, (LANES - MLA_V, tm), 0) == 0, 1.0, 0.0).astype(_BF)
    for hh in range(MLA_HEADS + GQA_KV_HEADS):
        src = vt_a[hh * MLA_V:(hh + 1) * MLA_V] if hh < MLA_HEADS else \
            vt_b[(hh - MLA_HEADS) * HEAD_DIM:(hh - MLA_HEADS + 1) * HEAD_DIM]
        vt_ref[hh * LANES:hh * LANES + MLA_V, :] = src
        vt_ref[hh * LANES + MLA_V:(hh + 1) * LANES, :] = ones_blk
    ca, sa = rope_ref[:, 0:128], rope_ref[:, 128:256]
    cb, sb = rope_ref[:, 256:384], rope_ref[:, 384:512]
    gqa, gka = gains_ref[0:1, :], gains_ref[1:2, :]
    gqb, gkb = gains_ref[2:3, :], gains_ref[3:4, :]
    na = MLA_NOPE + MLA_ROPE

    for hh in range(MLA_HEADS):
        sl = slice(hh * LANES, (hh + 1) * LANES)
        nq_scr[:, sl] = _norm(qa[:, sl], na, gqa)
        nk_scr[:, sl] = _norm(kn[:, sl] + kpe, na, gka)
    for hh in range(GQA_HEADS):
        src = slice(512 + hh * LANES, 512 + (hh + 1) * LANES)
        dst = slice((MLA_HEADS + hh) * LANES, (MLA_HEADS + hh + 1) * LANES)
        nq_scr[:, dst] = _norm(y[:, src], HEAD_DIM, gqb)
    for g in range(GQA_KV_HEADS):
        src = slice(1536 + g * LANES, 1536 + (g + 1) * LANES)
        dst = slice((MLA_HEADS + g) * LANES, (MLA_HEADS + g + 1) * LANES)
        nk_scr[:, dst] = _norm(y[:, src], HEAD_DIM, gkb)

    @pl.when(pl.program_id(1) >= 0)
    def _():
        def rope(ref, out, n_a, n_all):
            for hh in range(n_all):
                sl = slice(hh * LANES, (hh + 1) * LANES)
                c, s = (ca, sa) if hh < n_a else (cb, sb)
                v = ref[:, sl]
                out[:, sl] = (v * c + pltpu.roll(v, 64, 1) * s).astype(_BF)

        rope(nq_scr, q_ref, MLA_HEADS, MLA_HEADS + GQA_HEADS)
        rope(nk_scr, k_ref, MLA_HEADS, MLA_HEADS + GQA_KV_HEADS)


def _pre_even(x, mod, g, w_in, qn, w_uq, kvn, w_uk, w_uvt, w_vbt, gains, rope, tm):
    B, S, D = x.shape
    nq = (MLA_HEADS + GQA_HEADS) * LANES
    nk = (MLA_HEADS + GQA_KV_HEADS) * LANES
    nvt = (MLA_HEADS + GQA_KV_HEADS) * LANES
    full = lambda a: pl.BlockSpec(a.shape, lambda b, i: (0,) * a.ndim)
    tok = lambda w: pl.BlockSpec((None, tm, w), lambda b, i: (b, i, 0))
    return pl.pallas_call(
        _pre_even_kernel,
        grid=(B, S // tm),
        in_specs=[tok(D),
                  pl.BlockSpec((None, 6, D), lambda b, i: (b, 0, 0)),
                  full(g), full(w_in), full(qn), full(w_uq), full(kvn), full(w_uk), full(w_uvt),
                  full(w_vbt), full(gains),
                  pl.BlockSpec((tm, rope.shape[1]), lambda b, i: (i, 0))],
        out_specs=[tok(nq), tok(nk), pl.BlockSpec((None, None, nvt, tm), lambda b, i: (b, i, 0, 0))],
        out_shape=[jax.ShapeDtypeStruct((B, S, nq), _BF), jax.ShapeDtypeStruct((B, S, nk), _BF),
                   jax.ShapeDtypeStruct((B, S // tm, nvt, tm), _BF)],
        scratch_shapes=[pltpu.VMEM((tm, nq), _F32), pltpu.VMEM((tm, nk), _F32)],
        compiler_params=_cparams(("parallel", "parallel")),
        name="pre_even",
    )(x, mod, g, w_in, qn, w_uq, kvn, w_uk, w_uvt, w_vbt, gains, rope)


def _attn_kernel(q_ref, k_ref, vt_ref, o_ref, s_scr, *, shared_kv):
    tq = q_ref.shape[0]
    nk, _, tk = vt_ref.shape
    qs = [q_ref[:, hh * LANES:(hh + 1) * LANES] for hh in range(2)]

    def scores(j, hh):
        r0 = j * tk if isinstance(j, int) else pl.multiple_of(j * tk, tk)
        kc = 0 if shared_kv else hh * LANES
        k = k_ref[pl.ds(r0, tk), kc:kc + LANES]
        return lax.dot_general(k, qs[hh], (((1,), (1,)), ((), ())), preferred_element_type=_F32)

    def step(j, hh, carry, cur, last=False):
        m, acc = carry
        if not last:
            s_scr[hh, 1 - cur] = scores(j + 1, hh)
        st = s_scr[hh, cur]
        m_new = jnp.maximum(m, jnp.max(st, axis=0, keepdims=True))
        alpha = jnp.exp2(m - m_new)
        p = jnp.exp2(st - m_new).astype(_BF)
        vc = 0 if shared_kv else hh * LANES
        acc = alpha * acc + jnp.dot(vt_ref[j, vc:vc + LANES, :], p, preferred_element_type=_F32)
        return m_new, acc

    def body(jj, carry):
        c = list(carry)
        for sub in range(2):
            for hh in range(2):
                c[hh] = step(2 * jj + sub, hh, c[hh], sub)
        return tuple(c)

    for hh in range(2):
        s_scr[hh, 0] = scores(0, hh)
    init1 = (jnp.full((1, tq), NEG, _F32), jnp.zeros((LANES, tq), _F32))
    res = lax.fori_loop(0, nk // 2 - 1, body, (init1, init1))
    res = list(res)
    for sub in range(2):
        for hh in range(2):
            res[hh] = step(nk - 2 + sub, hh, res[hh], sub, last=(sub == 1))
    nv = HEAD_DIM
    halves = [acc[0:nv, :] / acc[nv:nv + 1, :] for (_, acc) in res]
    o_ref[...] = jnp.concatenate(halves, axis=0).T.astype(o_ref.dtype)


def _attention(q, k, vt, *, q_off, k_off, shared_kv, tq):
    B, S, _ = q.shape
    nk, _, tk = vt.shape[1:]
    assert nk % 2 == 0 and nk >= 2
    npairs = 4
    if shared_kv:
        kspec = pl.BlockSpec((None, S, LANES), lambda b, p, i: (b, 0, k_off + p // 2))
        vspec = pl.BlockSpec((None, nk, LANES, tk), lambda b, p, i: (b, 0, k_off + p // 2, 0))
    else:
        kspec = pl.BlockSpec((None, S, 2 * LANES), lambda b, p, i: (b, 0, k_off // 2 + p))
        vspec = pl.BlockSpec((None, nk, 2 * LANES, tk), lambda b, p, i: (b, 0, k_off // 2 + p, 0))
    return pl.pallas_call(
        functools.partial(_attn_kernel, shared_kv=shared_kv),
        grid=(B, npairs, S // tq),
        in_specs=[pl.BlockSpec((None, tq, 2 * LANES), lambda b, p, i: (b, i, q_off // 2 + p)),
                  kspec, vspec],
        out_specs=pl.BlockSpec((None, tq, LANES), lambda b, p, i: (b, i, p)),
        out_shape=jax.ShapeDtypeStruct((B, S, npairs * LANES), _BF),
        scratch_shapes=[pltpu.VMEM((2, 2, tk, tq), _F32)],
        compiler_params=_cparams(("parallel", "parallel", "parallel")),
        name="attn_gqa" if shared_kv else "attn_mla",
    )(q, k, vt)


def _oproj_kernel(*refs, n_in, route):
    x_ref, mod_ref, g_ref, w_ref = refs[0], refs[1], refs[2], refs[3]
    o_refs = refs[4:4 + n_in]
    pos = 4 + n_in
    if route:
        r_ref = refs[pos]
        pos += 1
    x_out, h_out = refs[pos], refs[pos + 1]
    y = None
    off = 0
    for o_ref in o_refs:
        if len(o_ref.shape) == 3:
            o = jnp.concatenate([o_ref[p] for p in range(o_ref.shape[0])], axis=1)
        else:
            o = o_ref[...]
        w = o.shape[1]
        t = jnp.dot(o, w_ref[off:off + w, :], preferred_element_type=_F32)
        y = t if y is None else y + t
        off += w
    x1 = x_ref[...] + mod_ref[2:3, :] * y
    x_out[...] = x1
    h = _modulate(x1, g_ref[...], mod_ref[3:4, :], mod_ref[4:5, :])
    if route:
        _rows_to_tiles(h_out, h)
    else:
        h_out[...] = h.astype(h_out.dtype)
    if route:
        route_out = refs[pos + 2]
        hh = h.astype(_BF)
        hl = (h - hh.astype(_F32)).astype(_BF)
        rh, rl = r_ref[0], r_ref[1]
        logits = (jnp.dot(hh, rh, preferred_element_type=_F32)
                  + jnp.dot(hh, rl, preferred_element_type=_F32)
                  + jnp.dot(hl, rh, preferred_element_type=_F32))
        tm = logits.shape[0]
        lane = lax.broadcasted_iota(jnp.int32, (tm, LANES), 1)
        lg = jnp.where(lane < N_EXPERTS, logits, NEG)
        m1 = jnp.max(lg, axis=-1, keepdims=True)
        lanef = lane.astype(_F32)
        i1 = jnp.min(jnp.where(lg == m1, lanef, float(LANES)), axis=-1, keepdims=True)
        lg2 = jnp.where(lanef == i1, NEG, lg)
        m2 = jnp.max(lg2, axis=-1, keepdims=True)
        i2 = jnp.min(jnp.where(lg2 == m2, lanef, float(LANES)), axis=-1, keepdims=True)
        e = jnp.exp(m2 - m1)
        g1 = 1.0 / (1.0 + e)
        g2 = e / (1.0 + e)
        route_out[...] = jnp.where(lane == 0, i1,
                                   jnp.where(lane == 1, i2,
                                             jnp.where(lane == 2, g1, jnp.where(lane == 3, g2, 0.0))))


def _oproj(x, mod, g, w, o_list, tm, h_dtype, router=None):
    B, S, D = x.shape
    n_in = len(o_list)
    route = router is not None
    tok = lambda wd: pl.BlockSpec((None, tm, wd), lambda b, i: (b, i, 0))
    in_specs = [tok(D), pl.BlockSpec((None, 6, D), lambda b, i: (b, 0, 0)),
                pl.BlockSpec((1, D), lambda b, i: (0, 0)),
                pl.BlockSpec(w.shape, lambda b, i: (0, 0))]
    for o in o_list:
        if o.ndim == 4:
            in_specs.append(pl.BlockSpec((None, o.shape[1], tm, LANES), lambda b, i: (b, 0, i, 0)))
        else:
            in_specs.append(tok(o.shape[2]))
    args = [x, mod, g, w] + list(o_list)
    out_specs = [tok(D), tok(D)]
    out_shape = [jax.ShapeDtypeStruct((B, S, D), _F32), jax.ShapeDtypeStruct((B, S, D), h_dtype)]
    if route:
        assert D == SUBLANES * LANES
        per_b = S // tm
        out_specs[1] = pl.BlockSpec((tm * SUBLANES, LANES), lambda b, i: (b * per_b + i, 0))
        out_shape[1] = jax.ShapeDtypeStruct((B * S * SUBLANES, LANES), _F32)
        in_specs.append(pl.BlockSpec(router.shape, lambda b, i: (0, 0, 0)))
        args.append(router)
        out_specs.append(tok(LANES))
        out_shape.append(jax.ShapeDtypeStruct((B, S, LANES), _F32))
    return pl.pallas_call(
        functools.partial(_oproj_kernel, n_in=n_in, route=route),
        grid=(B, S // tm),
        in_specs=in_specs, out_specs=out_specs, out_shape=out_shape,
        compiler_params=_cparams(("parallel", "parallel")),
        name="oproj_route" if route else "oproj",
    )(*args)


def _swiglu_hidden(x, wg_ref, wu_ref, a_scr, tf):
    F = wg_ref.shape[1]
    for c0 in range(0, F, tf):
        c1 = min(c0 + tf, F)
        g = jnp.dot(x, wg_ref[:, c0:c1], preferred_element_type=_F32)
        u = jnp.dot(x, wu_ref[:, c0:c1], preferred_element_type=_F32)
        a_scr[:, c0:c1] = (_silu(g) * u).astype(_BF)


def _ffn_kernel(x_ref, h_ref, mod_ref, modn_ref, gn_ref, wg_ref, wu_ref, wd_ref, x_out, h_out, a_scr, *, tf):
    _swiglu_hidden(h_ref[...], wg_ref, wu_ref, a_scr, tf)
    y = jnp.dot(a_scr[...], wd_ref[...], preferred_element_type=_F32)
    x2 = x_ref[...] + mod_ref[5:6, :] * y
    x_out[...] = x2
    h_out[...] = _modulate(x2, gn_ref[...], modn_ref[0:1, :], modn_ref[1:2, :]).astype(h_out.dtype)


def _ffn(x, h, mod, modn, gn, wg, wu, wd, tm, tf):
    B, S, D = x.shape
    T = B * S
    F = wg.shape[1]
    per_b = S // tm
    x2 = x.reshape(T, D)
    h2 = h.reshape(T, D)
    tok = pl.BlockSpec((tm, D), lambda i: (i, 0))
    modspec = pl.BlockSpec((None, 6, D), lambda i: (i // per_b, 0, 0))
    once = pl.Buffered(1)
    xo, ho = pl.pallas_call(
        functools.partial(_ffn_kernel, tf=tf),
        grid=(T // tm,),
        in_specs=[tok, tok, modspec, modspec, pl.BlockSpec((1, D), lambda i: (0, 0)),
                  pl.BlockSpec((D, F), lambda i: (0, 0), pipeline_mode=once),
                  pl.BlockSpec((D, F), lambda i: (0, 0), pipeline_mode=once),
                  pl.BlockSpec((F, D), lambda i: (0, 0), pipeline_mode=once)],
        out_specs=[tok, tok],
        out_shape=[jax.ShapeDtypeStruct((T, D), _F32), jax.ShapeDtypeStruct((T, D), _BF)],
        scratch_shapes=[pltpu.VMEM((tm, F), _BF)],
        compiler_params=_cparams(("parallel",)),
        name="ffn_dense",
    )(x2, h2, mod, modn, gn, wg, wu, wd)
    return xo.reshape(B, S, D), ho.reshape(B, S, D)


def _qkv_kernel(h_ref, w_ref, gains_ref, rope_ref, q_ref, k_ref, v_ref, n_scr):
    tm, D = h_ref.shape
    h = h_ref[...]
    lane = lax.broadcasted_iota(jnp.int32, (tm, LANES), 1)
    head_a = (lane & 63) < 32

    def head_pair_norm(v, g):
        sq = v * v
        s_a = jnp.sum(jnp.where(head_a, sq, 0.0), axis=-1, keepdims=True)
        s_b = jnp.sum(jnp.where(head_a, 0.0, sq), axis=-1, keepdims=True)
        r = lax.rsqrt(jnp.where(head_a, s_a, s_b) * (1.0 / HEAD_DIM) + EPS)
        return v * r * g

    y = jnp.dot(h, w_ref[...], preferred_element_type=_F32)
    for p in range(D // LANES):
        sl = slice(p * LANES, (p + 1) * LANES)
        n_scr[:, sl] = head_pair_norm(y[:, sl], gains_ref[0:1, :])
        n_scr[:, D + p * LANES:D + (p + 1) * LANES] = head_pair_norm(
            y[:, D + p * LANES:D + (p + 1) * LANES], gains_ref[1:2, :])
        v_ref[p] = y[:, 2 * D + p * LANES:2 * D + (p + 1) * LANES].astype(_BF)

    @pl.when(pl.program_id(1) >= 0)
    def _():
        c, s = rope_ref[:, 0:128], rope_ref[:, 128:256]
        for p in range(D // LANES):
            vq = n_scr[:, p * LANES:(p + 1) * LANES]
            vk = n_scr[:, D + p * LANES:D + (p + 1) * LANES]
            q_ref[p] = (vq * c + pltpu.roll(vq, 64, 1) * s).astype(_BF)
            k_ref[p] = (vk * c + pltpu.roll(vk, 64, 1) * s).astype(_BF)


def _qkv(h, w, gains, rope, tm):
    B, S, D = h.shape
    tok = pl.BlockSpec((None, tm, D), lambda b, i: (b, i, 0))
    hp = pl.BlockSpec((None, D // LANES, tm, LANES), lambda b, i: (b, 0, i, 0))
    sds = jax.ShapeDtypeStruct((B, D // LANES, S, LANES), _BF)
    return pl.pallas_call(
        _qkv_kernel,
        grid=(B, S // tm),
        in_specs=[tok, pl.BlockSpec(w.shape, lambda b, i: (0, 0)),
                  pl.BlockSpec((2, LANES), lambda b, i: (0, 0)),
                  pl.BlockSpec((tm, 2 * LANES), lambda b, i: (i, 0))],
        out_specs=[hp, hp, hp],
        out_shape=[sds, sds, sds],
        scratch_shapes=[pltpu.VMEM((tm, 2 * D), _F32)],
        compiler_params=_cparams(("parallel", "parallel")),
        name="qkv_dil",
    )(h, w, gains, rope)


DIL_GROUP = 4


def _dil_kernel(q_ref, k_ref, v_ref, o_ref, tmp, qp, kp, vp, acc, mm, ll, accp, mmp, llp, bias_scr, *, tl):
    S = q_ref.shape[0]
    ng = S // DIL_GROUP
    pitch = DIL_GROUP + 1

    def spread(dst, src):
        for j in range(DIL_GROUP):
            dst[pl.ds(j, ng, stride=pitch), :] = src[pl.ds(j, ng, stride=DIL_GROUP), :]

    for src, dst in ((q_ref, qp), (k_ref, kp), (v_ref, vp)):
        tmp[...] = src[...].astype(_F32)
        spread(dst, tmp)
    lane = lax.broadcasted_iota(jnp.int32, (tl, LANES), 1)
    lo = lane < 64
    head_a = (lane & 63) < 32
    for bi, (window, d) in enumerate(DIL_PATTERNS):
        L = S // d
        half = window // (2 * d)
        t = min(tl, L)
        W = min(t + 2 * half, L)
        nt = L // t
        if d > 1:
            assert d % DIL_GROUP == 0
        sd = d * pitch // DIL_GROUP

        def rows(r, first, n, d=d, sd=sd):
            if d == 1:
                return pl.ds(first, n)
            return pl.ds(r + r // DIL_GROUP + sd * first, n, stride=sd)

        assert t % half == 0 and W <= t + 2 * half
        qa = lax.broadcasted_iota(jnp.int32, (t, W), 0)
        kb = lax.broadcasted_iota(jnp.int32, (t, W), 1)
        for case in range(3):
            ok = jnp.abs(kb - qa - case * half) <= half
            bias_scr[case, 0:t, 0:W] = jnp.where(ok, 0.0, NEG)

        def body(idx, carry, bi=bi, d=d, L=L, half=half, t=t, W=W, nt=nt, rows=rows):
            r = idx // nt
            l0 = (idx % nt) * t
            start = jnp.clip(l0 - half, 0, L - W)
            if d == 1:
                qt = q_ref[pl.ds(pl.multiple_of(l0, t), t), :]
                kw = k_ref[pl.ds(pl.multiple_of(start, 64), W), :]
                vw = v_ref[pl.ds(pl.multiple_of(start, 64), W), :]
            else:
                qt = qp[rows(r, l0, t), :].astype(_BF)
                kw = kp[rows(r, start, W), :].astype(_BF)
                vw = vp[rows(r, start, W), :].astype(_BF)
            lo_t = lo[:t]
            qa_t = head_a[:t]
            zq = jnp.zeros_like(qt)
            q2 = jnp.concatenate([jnp.where(qa_t, qt, zq), jnp.where(qa_t, zq, qt)], axis=0)
            s = lax.dot_general(q2, kw, (((1,), (1,)), ((), ())), preferred_element_type=_F32)
            bias = bias_scr[(l0 - start) // half, 0:t, 0:W]
            s = s + jnp.concatenate([bias, bias], axis=0)
            m = jnp.max(s, axis=-1, keepdims=True)
            p = jnp.exp2(s - m)
            den = jnp.sum(p, axis=-1, keepdims=True)
            o2 = jnp.dot(p.astype(_BF), vw, preferred_element_type=_F32)
            o_new = jnp.where(lo_t, o2[:t], o2[t:])
            m_new = jnp.where(lo_t, m[:t], m[t:])
            l_new = jnp.where(lo_t, den[:t], den[t:])
            tok = rows(r, l0, t)
            if d == 1:
                assert bi == 0
                acc[tok, :] = o_new
                mm[tok, :] = m_new
                ll[tok, :] = l_new
            else:
                m_old = mmp[tok, :]
                mx = jnp.maximum(m_old, m_new)
                a_old = jnp.exp2(m_old - mx)
                a_new = jnp.exp2(m_new - mx)
                accp[tok, :] = accp[tok, :] * a_old + o_new * a_new
                llp[tok, :] = llp[tok, :] * a_old + l_new * a_new
                mmp[tok, :] = mx
            return carry

        lax.fori_loop(0, d * nt, body, 0, unroll=min(8, d * nt))
        if bi == 0:
            spread(accp, acc)
            spread(mmp, mm)
            spread(llp, ll)
    for j in range(DIL_GROUP):
        grp = pl.ds(j, ng, stride=pitch)
        tmp[pl.ds(j, ng, stride=DIL_GROUP), :] = accp[grp, :] / llp[grp, :]
    o_ref[...] = tmp[...].astype(o_ref.dtype)


def _dilated(q, k, v, tl):
    B, P, S, _ = q.shape
    spec = pl.BlockSpec((None, None, S, LANES), lambda b, p: (b, p, 0, 0))
    scr = pltpu.VMEM((S, LANES), _F32)
    scrp = pltpu.VMEM((S // DIL_GROUP * (DIL_GROUP + 1), LANES), _F32)
    return pl.pallas_call(
        functools.partial(_dil_kernel, tl=tl),
        grid=(B, P),
        in_specs=[spec, spec, spec],
        out_specs=spec,
        out_shape=jax.ShapeDtypeStruct((B, P, S, LANES), _BF),
        scratch_shapes=[scr, scrp, scrp, scrp, scr, scr, scr, scrp, scrp, scrp,
                        pltpu.VMEM((3, tl, tl + max(w // d for w, d in DIL_PATTERNS)), _F32)],
        compiler_params=_cparams(("parallel", "parallel")),
        name="dil_attn",
    )(q, k, v)


def _moe_kernel(te_ref, tv_ref, src_hbm, dst_hbm, h_hbm, wg_ref, wu_ref, wd_ref, out_hbm,
                xbuf, xbf, a_scr, ybuf, src_s0, src_s1, dst_s, gsem, ssem, isem, *, tf, dump0, n_dump):
    i = pl.program_id(0)
    nt = pl.num_programs(0)
    tm = xbf.shape[0]
    slot = i % 2

    def is_valid(t):
        return jnp.logical_and(jnp.logical_and(t >= 0, t < nt), tv_ref[jnp.clip(t, 0, nt - 1)] == 1)

    valid = is_valid(i)
    prev_valid = is_valid(i - 1)
    nxt_valid = is_valid(i + 1)

    src_bufs = (src_s0, src_s1)

    def src_copy(tile, par):
        return pltpu.make_async_copy(src_hbm.at[tile], src_bufs[par], isem.at[par])

    def dst_copy(tile):
        return pltpu.make_async_copy(dst_hbm.at[tile], dst_s, isem.at[2])

    def tile_at(first):
        return pl.ds(pl.multiple_of(first, SUBLANES), SUBLANES)

    def gather_row(r, par):
        return pltpu.make_async_copy(h_hbm.at[tile_at(src_bufs[par][r])],
                                     xbuf.at[par, tile_at(r * SUBLANES)], gsem.at[par])

    def scatter_row(r):
        return pltpu.make_async_copy(ybuf.at[tile_at(r * SUBLANES)], out_hbm.at[tile_at(dst_s[r])],
                                     ssem.at[0])

    def gather_wait(slot_):
        return pltpu.make_async_copy(h_hbm.at[pl.ds(0, tm * SUBLANES)], xbuf.at[slot_], gsem.at[slot_])

    def scatter_wait():
        return pltpu.make_async_copy(ybuf, out_hbm.at[pl.ds(0, tm * SUBLANES)], ssem.at[0])

    def for_rows(fn):
        def body(r, c):
            fn(r)
            return c
        lax.fori_loop(0, tm, body, 0, unroll=8)

    @pl.when(i == 0)
    def _():
        ybuf[...] = jnp.zeros_like(ybuf)
        for k in range(n_dump):
            pltpu.make_async_copy(ybuf, out_hbm.at[pl.ds((dump0 + k * tm) * SUBLANES, tm * SUBLANES)],
                                  ssem.at[0]).start()
        for k in range(n_dump):
            scatter_wait().wait()

    @pl.when(jnp.logical_and(i == 0, valid))
    def _():
        src_copy(0, 0).start()
        src_copy(0, 0).wait()
        for_rows(lambda r: gather_row(r, 0).start())

    @pl.when(jnp.logical_and(i == 0, nxt_valid))
    def _():
        src_copy(1, 1).start()

    @pl.when(valid)
    def _():
        dst_copy(i).start()

    for par in range(2):
        @pl.when(jnp.logical_and(nxt_valid, slot == par))
        def _(par=par):
            src_copy(i + 1, 1 - par).wait()
            for_rows(lambda r: gather_row(r, 1 - par).start())

    for par in range(2):
        @pl.when(jnp.logical_and(valid, slot == par))
        def _(par=par):
            gather_wait(par).wait()
            xbf[...] = _tiles_to_rows(xbuf.at[par], tm).astype(_BF)

    @pl.when(valid)
    def _():
        _swiglu_hidden(xbf[...], wg_ref, wu_ref, a_scr, tf)

    @pl.when(prev_valid)
    def _():
        scatter_wait().wait()

    @pl.when(valid)
    def _():
        _rows_to_tiles(ybuf, jnp.dot(a_scr[...], wd_ref[...], preferred_element_type=_F32))
        dst_copy(i).wait()
        for_rows(lambda r: scatter_row(r).start())

    for par in range(2):
        @pl.when(jnp.logical_and(is_valid(i + 2), slot == par))
        def _(par=par):
            src_copy(i + 2, par).start()

    @pl.when(jnp.logical_and(i == nt - 1, valid))
    def _():
        scatter_wait().wait()


def _moe(h, tile_e, tile_v, src, dst, wg, wu, wd, n_out_rows, tm, tf):
    T = h.shape[0] // SUBLANES
    D = wg.shape[1]
    nt = tile_e.shape[0]
    F = wg.shape[2]
    dump0 = T * TOP_K
    once = pl.Buffered(1)
    grid_spec = pltpu.PrefetchScalarGridSpec(
        num_scalar_prefetch=2,
        grid=(nt,),
        in_specs=[pl.BlockSpec(memory_space=pl.ANY), pl.BlockSpec(memory_space=pl.ANY),
                  pl.BlockSpec(memory_space=pl.ANY),
                  pl.BlockSpec((None, D, F), lambda i, te, tv: (te[i], 0, 0), pipeline_mode=once),
                  pl.BlockSpec((None, D, F), lambda i, te, tv: (te[i], 0, 0), pipeline_mode=once),
                  pl.BlockSpec((None, F, D), lambda i, te, tv: (te[i], 0, 0), pipeline_mode=once)],
        out_specs=pl.BlockSpec(memory_space=pl.ANY),
        scratch_shapes=[pltpu.VMEM((2, tm * SUBLANES, LANES), _F32), pltpu.VMEM((tm, D), _BF),
                        pltpu.VMEM((tm, F), _BF), pltpu.VMEM((tm * SUBLANES, LANES), _F32),
                        pltpu.SMEM((tm,), jnp.int32), pltpu.SMEM((tm,), jnp.int32),
                        pltpu.SMEM((tm,), jnp.int32),
                        pltpu.SemaphoreType.DMA((2,)), pltpu.SemaphoreType.DMA((1,)),
                        pltpu.SemaphoreType.DMA((3,))],
    )
    return pl.pallas_call(
        functools.partial(_moe_kernel, tf=tf, dump0=dump0, n_dump=(n_out_rows - dump0) // tm),
        grid_spec=grid_spec,
        out_shape=jax.ShapeDtypeStruct((n_out_rows * SUBLANES, LANES), _F32),
        compiler_params=_cparams(("arbitrary",)),
        name="moe_experts",
    )(tile_e, tile_v, src, dst, h, wg, wu, wd)


def _route_plan(route, T, tm):
    A = T * TOP_K
    e_flat = route[:, :TOP_K].astype(jnp.int32).reshape(A)
    order = jnp.argsort(e_flat, stable=True).astype(jnp.int32)
    counts = jnp.sum((e_flat[:, None] == jnp.arange(N_EXPERTS, dtype=jnp.int32)[None, :]).astype(jnp.int32), axis=0)
    starts = jnp.cumsum(counts) - counts
    pcounts = (counts + tm - 1) // tm * tm
    pends = jnp.cumsum(pcounts)
    pstarts = pends - pcounts
    nt = A // tm + N_EXPERTS
    tile0 = jnp.arange(nt, dtype=jnp.int32) * tm
    tile_v = (tile0 < pends[-1]).astype(jnp.int32)
    last_valid = jnp.maximum(pends[-1] // tm - 1, 0)
    tile_e_raw = jnp.minimum(jnp.searchsorted(pends, tile0, side="right"), N_EXPERTS - 1).astype(jnp.int32)
    tile_e = jnp.where(tile_v == 1, tile_e_raw, tile_e_raw[last_valid])
    r = jnp.arange(nt * tm, dtype=jnp.int32)
    e_r = jnp.repeat(tile_e, tm)
    within = r - pstarts[e_r]
    ok = jnp.logical_and(within < counts[e_r], jnp.repeat(tile_v, tm) == 1)
    a = order[jnp.clip(starts[e_r] + within, 0, A - 1)]
    src = jnp.where(ok, a // TOP_K, 0).astype(jnp.int32)
    dump = A + e_r * tm + jnp.clip(within - counts[e_r], 0, tm - 1)
    dst = jnp.where(ok, (a % TOP_K) * T + a // TOP_K, dump).astype(jnp.int32)
    return (tile_e, tile_v, (src * SUBLANES).reshape(nt, tm), (dst * SUBLANES).reshape(nt, tm),
            A + N_EXPERTS * tm)


def _combine_kernel(x_ref, y1_ref, y2_ref, r_ref, mod_ref, o_ref):
    tm = x_ref.shape[0]
    y = _tiles_to_rows(y1_ref, tm) * r_ref[:, 2:3] + _tiles_to_rows(y2_ref, tm) * r_ref[:, 3:4]
    o_ref[...] = x_ref[...] + mod_ref[5:6, :] * y


def _combine(x, y2, route, mod, tm):
    B, S, D = x.shape
    T = B * S
    per_b = S // tm
    out = pl.pallas_call(
        _combine_kernel,
        grid=(T // tm,),
        in_specs=[pl.BlockSpec((tm, D), lambda i: (i, 0)),
                  pl.BlockSpec((tm * SUBLANES, LANES), lambda i: (i, 0)),
                  pl.BlockSpec((tm * SUBLANES, LANES), lambda i: (T // tm + i, 0)),
                  pl.BlockSpec((tm, LANES), lambda i: (i, 0)),
                  pl.BlockSpec((None, 6, D), lambda i: (i // per_b, 0, 0))],
        out_specs=pl.BlockSpec((tm, D), lambda i: (i, 0)),
        out_shape=jax.ShapeDtypeStruct((T, D), _F32),
        compiler_params=_cparams(("parallel",)),
        name="moe_combine",
    )(x.reshape(T, D), y2, y2, route.reshape(T, LANES), mod)
    return out.reshape(B, S, D)


def _pad_cols(a, w):
    return jnp.pad(a, ((0, 0), (0, w - a.shape[1])))


def _rope_tables_even(S):
    pos = jnp.arange(S, dtype=jnp.int32)
    inv = ROPE_THETA ** (-jnp.arange(0, 32, 2, dtype=_F32) / 32)
    def cs(p):
        ang = p.astype(_F32)[:, None] * inv[None, :]
        return jnp.cos(ang), jnp.sin(ang)
    one = lambda w: jnp.ones((S, w), _F32)
    zero = lambda w: jnp.zeros((S, w), _F32)
    c, s = cs(pos)
    ca = jnp.concatenate([c, one(48), c, one(48)], 1)
    sa = jnp.concatenate([-s, zero(48), s, zero(48)], 1)
    cr, sr = cs(pos // GRID_W)
    cc, sc = cs(pos % GRID_W)
    cb = jnp.concatenate([cr, cc, one(32), cr, cc, one(32)], 1)
    sb = jnp.concatenate([-sr, -sc, zero(32), sr, sc, zero(32)], 1)
    return jnp.concatenate([ca, sa, cb, sb], 1)


def _slot_maps():
    r = MLA_ROPE // 2
    mla = ([MLA_NOPE + i for i in range(r)] + list(range(0, 64 - r))
           + [MLA_NOPE + r + i for i in range(r)] + list(range(64 - r, MLA_NOPE)))
    mla += [-1] * (LANES - len(mla))
    q = HEAD_DIM // 4
    gqa = (list(range(0, q)) + list(range(2 * q, 3 * q)) + [-1] * (64 - 2 * q)
           + list(range(q, 2 * q)) + list(range(3 * q, 4 * q)) + [-1] * (64 - 2 * q))
    return mla, gqa


def _to_slots(a, lane_map):
    idx = jnp.asarray([max(i, 0) for i in lane_map], jnp.int32)
    keep = jnp.asarray([1.0 if i >= 0 else 0.0 for i in lane_map], a.dtype)
    return jnp.take(a, idx, axis=-1) * keep


def _rope_tables_odd(S):
    pos = jnp.arange(S, dtype=_F32)
    inv = ROPE_THETA ** (-jnp.arange(0, HEAD_DIM, 2, dtype=_F32) / HEAD_DIM)
    ang = pos[:, None] * inv[None, :]
    c, s = jnp.cos(ang), jnp.sin(ang)
    return jnp.concatenate([c, c, c, c, -s, -s, s, s], 1)


def _tiles(S):
    return dict(tm_pre=min(S, 512), tq=min(S, 2048), tm_o=min(S, 512),
                tm_ffn=min(S, 512), tf_ffn=512, tm_qkv=min(S, 512),
                tl=128, tm_moe=min(S, 512), tf_moe=512, tm_c=min(S, 512))


def kernel(x, c, ada_even_w, ada_even_b, norm_even_mix, norm_even_ffn, even_w_in, mla_q_norm, mla_w_uq, mla_kv_norm, mla_w_ukv, mla_q_gain, mla_k_gain, gqa_q_gain, gqa_k_gain, even_w_out, ffn_w_gate, ffn_w_up, ffn_w_down, ada_odd_w, ada_odd_b, norm_odd_mix, norm_odd_ffn, dil_w_qkv, dil_q_gain, dil_k_gain, dil_w_out, moe_router, moe_w_gate, moe_w_up, moe_w_down):
    B, S, D = x.shape
    T = B * S
    cfg = _tiles(S)

    mod_e = _ada_mod(c, ada_even_w[0], ada_even_b[0]).reshape(B, 6, D)
    mod_o = _ada_mod(c, ada_odd_w[0], ada_odd_b[0]).reshape(B, 6, D)

    w = even_w_in[0]
    sp = [MLA_Q_RANK, MLA_Q_RANK + MLA_KV_RANK, MLA_Q_RANK + MLA_KV_RANK + MLA_ROPE]
    sp.append(sp[-1] + GQA_HEADS * HEAD_DIM)
    sp.append(sp[-1] + GQA_KV_HEADS * HEAD_DIM)
    w_cq, w_ckv, w_kpe = w[:, :sp[0]], w[:, sp[0]:sp[1]], w[:, sp[1]:sp[2]]
    w_qb, w_kb, w_vb = w[:, sp[2]:sp[3]], w[:, sp[3]:sp[4]], w[:, sp[4]:]
    mla_map, gqa_map = _slot_maps()
    na = MLA_NOPE + MLA_ROPE
    nope_only = [i if 0 <= i < MLA_NOPE else -1 for i in mla_map]
    rope_only = [i - MLA_NOPE if i >= MLA_NOPE else -1 for i in mla_map]
    gslots = lambda a, n: _to_slots(a.reshape(D, n, HEAD_DIM), gqa_map).reshape(D, n * LANES)
    w_in = jnp.concatenate([w_cq, w_ckv, _to_slots(w_kpe, rope_only), gslots(w_qb, GQA_HEADS),
                            gslots(w_kb, GQA_KV_HEADS)], axis=1).astype(_BF)
    w_uq = _to_slots(mla_w_uq[0].reshape(MLA_Q_RANK, MLA_HEADS, na), mla_map)
    w_uq = w_uq.reshape(MLA_Q_RANK, MLA_HEADS * LANES).astype(_BF)
    w_ukv = mla_w_ukv[0].reshape(MLA_KV_RANK, MLA_HEADS, MLA_NOPE + MLA_V)
    w_uk = _to_slots(w_ukv[:, :, :MLA_NOPE], nope_only).reshape(MLA_KV_RANK, MLA_HEADS * LANES).astype(_BF)
    w_uvt = w_ukv[:, :, MLA_NOPE:].reshape(MLA_KV_RANK, MLA_HEADS * MLA_V).T.astype(_BF)
    w_vbt = w_vb.T.astype(_BF)
    gains_e = jnp.stack([_to_slots(mla_q_gain[0], mla_map) * (na ** -0.5 * LOG2E),
                         _to_slots(mla_k_gain[0], mla_map),
                         _to_slots(gqa_q_gain[0], gqa_map) * (HEAD_DIM ** -0.5 * LOG2E),
                         _to_slots(gqa_k_gain[0], gqa_map)])
    rope_e = _rope_tables_even(S)
    q_all, k_all, vt_all = _pre_even(
        x, mod_e, norm_even_mix[0].reshape(1, D), w_in, mla_q_norm[0].reshape(1, -1), w_uq,
        mla_kv_norm[0].reshape(1, -1), w_uk, w_uvt, w_vbt, gains_e, rope_e, cfg["tm_pre"])
    o_a = _attention(q_all, k_all, vt_all, q_off=0, k_off=0, shared_kv=False, tq=cfg["tq"])
    o_b = _attention(q_all, k_all, vt_all, q_off=MLA_HEADS, k_off=MLA_HEADS, shared_kv=True, tq=cfg["tq"])
    x1, h1 = _oproj(x, mod_e, norm_even_ffn[0].reshape(1, D), even_w_out[0].astype(_BF), [o_a, o_b],
                    cfg["tm_o"], _BF)
    x2, h2 = _ffn(x1, h1, mod_e, mod_o, norm_odd_mix[0].reshape(1, D), ffn_w_gate[0].astype(_BF),
                  ffn_w_up[0].astype(_BF), ffn_w_down[0].astype(_BF), cfg["tm_ffn"], cfg["tf_ffn"])

    hh = HEAD_DIM // 2
    pair = lambda v: jnp.concatenate([v[:hh], v[:hh], v[hh:], v[hh:]])
    gains_o = jnp.stack([pair(dil_q_gain[0]) * (HEAD_DIM ** -0.5 * LOG2E), pair(dil_k_gain[0])])
    wq, wk, wv = jnp.split(dil_w_qkv[0], 3, axis=1)
    perm = lambda w: w.reshape(D, D // LANES, 2, 2, hh).transpose(0, 1, 3, 2, 4).reshape(D, D)
    w_qkv = jnp.concatenate([perm(wq), perm(wk), wv], axis=1).astype(_BF)
    qd, kd, vd = _qkv(h2, w_qkv, gains_o, _rope_tables_odd(S), cfg["tm_qkv"])
    o_d = _dilated(qd, kd, vd, cfg["tl"])
    r32 = _pad_cols(moe_router[0], LANES)
    r_hi = r32.astype(_BF)
    r_lo = (r32 - r_hi.astype(_F32)).astype(_BF)
    x3, h3, route = _oproj(x2, mod_o, norm_odd_ffn[0].reshape(1, D), dil_w_out[0].astype(_BF), [o_d],
                           cfg["tm_o"], _F32, router=jnp.stack([r_hi, r_lo]))
    tm = cfg["tm_moe"]
    tile_e, tile_v, src, dst, n_rows = _route_plan(route.reshape(T, LANES), T, tm)
    y2 = _moe(h3, tile_e, tile_v, src, dst, moe_w_gate[0].astype(_BF),
              moe_w_up[0].astype(_BF), moe_w_down[0].astype(_BF), n_rows, tm, cfg["tf_moe"])
    return _combine(x3, y2, route, mod_o, cfg["tm_c"])
```

```python
import functools
import math

import jax
import jax.numpy as jnp
from jax import lax
from jax.experimental import pallas as pl
from jax.experimental.pallas import tpu as pltpu

_BF = jnp.bfloat16
_F32 = jnp.float32

GRID_W = 64
HEAD_DIM = 64
ROPE_THETA = 10000.0
EPS = 1e-6
MLA_HEADS = 8
MLA_Q_RANK = 256
MLA_KV_RANK = 128
MLA_NOPE = 64
MLA_ROPE = 32
MLA_V = 64
GQA_HEADS = 8
GQA_KV_HEADS = 2
DIL_PATTERNS = ((128, 1), (512, 4), (2048, 16))
N_EXPERTS = 8
TOP_K = 2
NEG = -1e30
LOG2E = math.log2(math.e)

LANES = 128
VMEM_LIMIT = 56 * 1024 * 1024


def _cparams(sem, vmem=VMEM_LIMIT):
    return pltpu.CompilerParams(dimension_semantics=sem, vmem_limit_bytes=vmem)


def _silu(x):
    return x / (1.0 + jnp.exp(-x))


def _modulate(x, g, shift, scale):
    ms = jnp.mean(x * x, axis=-1, keepdims=True)
    return x * lax.rsqrt(ms + EPS) * g * (1.0 + scale) + shift


def _norm(v, n, g):
    return v * lax.rsqrt(jnp.sum(v * v, axis=-1, keepdims=True) * (1.0 / n) + EPS) * g


SUBLANES = 8


def _rows_to_tiles(ref, x):
    tm = x.shape[0]
    for c in range(SUBLANES):
        ref[pl.ds(c, tm, stride=SUBLANES), :] = x[:, c * LANES:(c + 1) * LANES]


def _tiles_to_rows(ref, tm):
    return jnp.concatenate([ref[pl.ds(c, tm, stride=SUBLANES), :] for c in range(SUBLANES)], axis=1)


def _rope(x, c, sa, sb, k):
    return x * c + pltpu.roll(x, LANES - k, 1) * sa + pltpu.roll(x, k, 1) * sb


def _mod_kernel(c_ref, w_ref, b_ref, o_ref):
    sc = _silu(c_ref[...])
    o_ref[...] = jnp.dot(sc.astype(_BF), w_ref[...].astype(_BF),
                         preferred_element_type=_F32) + b_ref[...]


def _ada_mod(c, w, b):
    B, D = c.shape
    N = w.shape[1]
    tn = min(N, 1536)
    return pl.pallas_call(
        _mod_kernel,
        grid=(N // tn,),
        in_specs=[pl.BlockSpec((B, D), lambda j: (0, 0)),
                  pl.BlockSpec((D, tn), lambda j: (0, j)),
                  pl.BlockSpec((1, tn), lambda j: (0, j))],
        out_specs=pl.BlockSpec((B, tn), lambda j: (0, j)),
        out_shape=jax.ShapeDtypeStruct((B, N), _F32),
        compiler_params=_cparams(("arbitrary",)),
        name="ada_mod",
    )(c, w, b.reshape(1, N))


def _pre_even_kernel(x_ref, mod_ref, g_ref, w_in_ref, qn_ref, w_uq_ref, kvn_ref, w_uk_ref, w_uvt_ref,
                     w_vbt_ref, gains_ref, rope_ref, q_ref, k_ref, vt_ref, nq_scr, nk_scr):
    h = _modulate(x_ref[...], g_ref[...], mod_ref[0:1, :], mod_ref[1:2, :]).astype(_BF)
    y = jnp.dot(h, w_in_ref[...], preferred_element_type=_F32)
    cqn = _norm(y[:, 0:256], MLA_Q_RANK, qn_ref[...]).astype(_BF)
    qa = jnp.dot(cqn, w_uq_ref[...], preferred_element_type=_F32)
    ckvn = _norm(y[:, 256:384], MLA_KV_RANK, kvn_ref[...]).astype(_BF)
    kn = jnp.dot(ckvn, w_uk_ref[...], preferred_element_type=_F32)
    kpe = y[:, 384:512]
    nt_dims = (((1,), (1,)), ((), ()))
    vt_a = lax.dot_general(w_uvt_ref[...], ckvn, nt_dims, preferred_element_type=_F32).astype(_BF)
    vt_b = lax.dot_general(w_vbt_ref[...], h, nt_dims, preferred_element_type=_F32).astype(_BF)
    tm = x_ref.shape[0]
    ones_blk = jnp.where(lax.broadcasted_iota(jnp.int32, (LANES - MLA_V, tm), 0) == 0, 1.0, 0.0).astype(_BF)
    for hh in range(MLA_HEADS + GQA_KV_HEADS):
        src = vt_a[hh * MLA_V:(hh + 1) * MLA_V] if hh < MLA_HEADS else \
            vt_b[(hh - MLA_HEADS) * HEAD_DIM:(hh - MLA_HEADS + 1) * HEAD_DIM]
        vt_ref[hh * LANES:hh * LANES + MLA_V, :] = src
        vt_ref[hh * LANES + MLA_V:(hh + 1) * LANES, :] = ones_blk
    ca, sa = rope_ref[:, 0:128], rope_ref[:, 128:256]
    cb, sb = rope_ref[:, 256:384], rope_ref[:, 384:512]
    gqa, gka = gains_ref[0:1, :], gains_ref[1:2, :]
    gqb, gkb = gains_ref[2:3, :], gains_ref[3:4, :]
    na = MLA_NOPE + MLA_ROPE

    for hh in range(MLA_HEADS):
        sl = slice(hh * LANES, (hh + 1) * LANES)
        nq_scr[:, sl] = _norm(qa[:, sl], na, gqa)
        nk_scr[:, sl] = _norm(kn[:, sl] + kpe, na, gka)
    for hh in range(GQA_HEADS):
        src = slice(512 + hh * LANES, 512 + (hh + 1) * LANES)
        dst = slice((MLA_HEADS + hh) * LANES, (MLA_HEADS + hh + 1) * LANES)
        nq_scr[:, dst] = _norm(y[:, src], HEAD_DIM, gqb)
    for g in range(GQA_KV_HEADS):
        src = slice(1536 + g * LANES, 1536 + (g + 1) * LANES)
        dst = slice((MLA_HEADS + g) * LANES, (MLA_HEADS + g + 1) * LANES)
        nk_scr[:, dst] = _norm(y[:, src], HEAD_DIM, gkb)

    @pl.when(pl.program_id(1) >= 0)
    def _():
        def rope(ref, out, n_a, n_all):
            for hh in range(n_all):
                sl = slice(hh * LANES, (hh + 1) * LANES)
                c, s = (ca, sa) if hh < n_a else (cb, sb)
                v = ref[:, sl]
                out[:, sl] = (v * c + pltpu.roll(v, 64, 1) * s).astype(_BF)

        rope(nq_scr, q_ref, MLA_HEADS, MLA_HEADS + GQA_HEADS)
        rope(nk_scr, k_ref, MLA_HEADS, MLA_HEADS + GQA_KV_HEADS)


def _pre_even(x, mod, g, w_in, qn, w_uq, kvn, w_uk, w_uvt, w_vbt, gains, rope, tm):
    B, S, D = x.shape
    nq = (MLA_HEADS + GQA_HEADS) * LANES
    nk = (MLA_HEADS + GQA_KV_HEADS) * LANES
    nvt = (MLA_HEADS + GQA_KV_HEADS) * LANES
    full = lambda a: pl.BlockSpec(a.shape, lambda b, i: (0,) * a.ndim)
    tok = lambda w: pl.BlockSpec((None, tm, w), lambda b, i: (b, i, 0))
    return pl.pallas_call(
        _pre_even_kernel,
        grid=(B, S // tm),
        in_specs=[tok(D),
                  pl.BlockSpec((None, 6, D), lambda b, i: (b, 0, 0)),
                  full(g), full(w_in), full(qn), full(w_uq), full(kvn), full(w_uk), full(w_uvt),
                  full(w_vbt), full(gains),
                  pl.BlockSpec((tm, rope.shape[1]), lambda b, i: (i, 0))],
        out_specs=[tok(nq), tok(nk), pl.BlockSpec((None, None, nvt, tm), lambda b, i: (b, i, 0, 0))],
        out_shape=[jax.ShapeDtypeStruct((B, S, nq), _BF), jax.ShapeDtypeStruct((B, S, nk), _BF),
                   jax.ShapeDtypeStruct((B, S // tm, nvt, tm), _BF)],
        scratch_shapes=[pltpu.VMEM((tm, nq), _F32), pltpu.VMEM((tm, nk), _F32)],
        compiler_params=_cparams(("parallel", "parallel")),
        name="pre_even",
    )(x, mod, g, w_in, qn, w_uq, kvn, w_uk, w_uvt, w_vbt, gains, rope)


def _attn_kernel(q_ref, k_ref, vt_ref, o_ref, s_scr, *, shared_kv):
    tq = q_ref.shape[0]
    nk, _, tk = vt_ref.shape
    qs = [q_ref[:, hh * LANES:(hh + 1) * LANES] for hh in range(2)]

    def scores(j, hh):
        r0 = j * tk if isinstance(j, int) else pl.multiple_of(j * tk, tk)
        kc = 0 if shared_kv else hh * LANES
        k = k_ref[pl.ds(r0, tk), kc:kc + LANES]
        return lax.dot_general(k, qs[hh], (((1,), (1,)), ((), ())), preferred_element_type=_F32)

    def put_scores(j, hh, slot):
        st = scores(j, hh)
        s_scr[hh, slot] = st
        return jnp.max(st, axis=0, keepdims=True)

    def step(j, hh, carry, cur, last=False):
        m, mc, acc = carry
        mc_next = mc if last else put_scores(j + 1, hh, 1 - cur)
        st = s_scr[hh, cur]
        m_new = jnp.maximum(m, mc)
        alpha = jnp.exp2(m - m_new)
        p = jnp.exp2(st - m_new).astype(_BF)
        vc = 0 if shared_kv else hh * LANES
        acc = alpha * acc + jnp.dot(vt_ref[j, vc:vc + LANES, :], p, preferred_element_type=_F32)
        return m_new, mc_next, acc

    def body(jj, carry):
        c = list(carry)
        for sub in range(2):
            for hh in range(2):
                c[hh] = step(2 * jj + sub, hh, c[hh], sub)
        return tuple(c)

    init = tuple((jnp.full((1, tq), NEG, _F32), put_scores(0, hh, 0), jnp.zeros((LANES, tq), _F32))
                 for hh in range(2))
    res = lax.fori_loop(0, nk // 2 - 1, body, init)
    res = list(res)
    for sub in range(2):
        for hh in range(2):
            res[hh] = step(nk - 2 + sub, hh, res[hh], sub, last=(sub == 1))
    nv = HEAD_DIM
    halves = [acc[0:nv, :] / acc[nv:nv + 1, :] for (_, _, acc) in res]
    o_ref[...] = jnp.concatenate(halves, axis=0).T.astype(o_ref.dtype)


def _attention(q, k, vt, *, q_off, k_off, shared_kv, tq):
    B, S, _ = q.shape
    nk, _, tk = vt.shape[1:]
    assert nk % 2 == 0 and nk >= 2
    npairs = 4
    if shared_kv:
        kspec = pl.BlockSpec((None, S, LANES), lambda b, p, i: (b, 0, k_off + p // 2))
        vspec = pl.BlockSpec((None, nk, LANES, tk), lambda b, p, i: (b, 0, k_off + p // 2, 0))
    else:
        kspec = pl.BlockSpec((None, S, 2 * LANES), lambda b, p, i: (b, 0, k_off // 2 + p))
        vspec = pl.BlockSpec((None, nk, 2 * LANES, tk), lambda b, p, i: (b, 0, k_off // 2 + p, 0))
    return pl.pallas_call(
        functools.partial(_attn_kernel, shared_kv=shared_kv),
        grid=(B, npairs, S // tq),
        in_specs=[pl.BlockSpec((None, tq, 2 * LANES), lambda b, p, i: (b, i, q_off // 2 + p)),
                  kspec, vspec],
        out_specs=pl.BlockSpec((None, tq, LANES), lambda b, p, i: (b, i, p)),
        out_shape=jax.ShapeDtypeStruct((B, S, npairs * LANES), _BF),
        scratch_shapes=[pltpu.VMEM((2, 2, tk, tq), _F32)],
        compiler_params=_cparams(("parallel", "parallel", "parallel")),
        name="attn_gqa" if shared_kv else "attn_mla",
    )(q, k, vt)


def _oproj_kernel(*refs, n_in, route):
    x_ref, mod_ref, g_ref, w_ref = refs[0], refs[1], refs[2], refs[3]
    o_refs = refs[4:4 + n_in]
    pos = 4 + n_in
    if route:
        r_ref = refs[pos]
        pos += 1
    x_out, h_out = refs[pos], refs[pos + 1]
    y = None
    off = 0
    for o_ref in o_refs:
        if len(o_ref.shape) == 3:
            o = jnp.concatenate([o_ref[p] for p in range(o_ref.shape[0])], axis=1)
        else:
            o = o_ref[...]
        w = o.shape[1]
        t = jnp.dot(o, w_ref[off:off + w, :], preferred_element_type=_F32)
        y = t if y is None else y + t
        off += w
    x1 = x_ref[...] + mod_ref[2:3, :] * y
    x_out[...] = x1
    h = _modulate(x1, g_ref[...], mod_ref[3:4, :], mod_ref[4:5, :])
    if route:
        _rows_to_tiles(h_out, h)
    else:
        h_out[...] = h.astype(h_out.dtype)
    if route:
        route_out = refs[pos + 2]
        hh = h.astype(_BF)
        hl = (h - hh.astype(_F32)).astype(_BF)
        rh, rl = r_ref[0], r_ref[1]
        logits = (jnp.dot(hh, rh, preferred_element_type=_F32)
                  + jnp.dot(hh, rl, preferred_element_type=_F32)
                  + jnp.dot(hl, rh, preferred_element_type=_F32))
        tm = logits.shape[0]
        lane = lax.broadcasted_iota(jnp.int32, (tm, LANES), 1)
        lg = jnp.where(lane < N_EXPERTS, logits, NEG)
        m1 = jnp.max(lg, axis=-1, keepdims=True)
        lanef = lane.astype(_F32)
        i1 = jnp.min(jnp.where(lg == m1, lanef, float(LANES)), axis=-1, keepdims=True)
        lg2 = jnp.where(lanef == i1, NEG, lg)
        m2 = jnp.max(lg2, axis=-1, keepdims=True)
        i2 = jnp.min(jnp.where(lg2 == m2, lanef, float(LANES)), axis=-1, keepdims=True)
        e = jnp.exp(m2 - m1)
        g1 = 1.0 / (1.0 + e)
        g2 = e / (1.0 + e)
        route_out[...] = jnp.where(lane == 0, i1,
                                   jnp.where(lane == 1, i2,
                                             jnp.where(lane == 2, g1, jnp.where(lane == 3, g2, 0.0))))


def _oproj(x, mod, g, w, o_list, tm, h_dtype, router=None):
    B, S, D = x.shape
    n_in = len(o_list)
    route = router is not None
    tok = lambda wd: pl.BlockSpec((None, tm, wd), lambda b, i: (b, i, 0))
    in_specs = [tok(D), pl.BlockSpec((None, 6, D), lambda b, i: (b, 0, 0)),
                pl.BlockSpec((1, D), lambda b, i: (0, 0)),
                pl.BlockSpec(w.shape, lambda b, i: (0, 0))]
    for o in o_list:
        if o.ndim == 4:
            in_specs.append(pl.BlockSpec((None, o.shape[1], tm, LANES), lambda b, i: (b, 0, i, 0)))
        else:
            in_specs.append(tok(o.shape[2]))
    args = [x, mod, g, w] + list(o_list)
    out_specs = [tok(D), tok(D)]
    out_shape = [jax.ShapeDtypeStruct((B, S, D), _F32), jax.ShapeDtypeStruct((B, S, D), h_dtype)]
    if route:
        assert D == SUBLANES * LANES
        per_b = S // tm
        out_specs[1] = pl.BlockSpec((tm * SUBLANES, LANES), lambda b, i: (b * per_b + i, 0))
        out_shape[1] = jax.ShapeDtypeStruct((B * S * SUBLANES, LANES), _F32)
        in_specs.append(pl.BlockSpec(router.shape, lambda b, i: (0, 0, 0)))
        args.append(router)
        out_specs.append(tok(LANES))
        out_shape.append(jax.ShapeDtypeStruct((B, S, LANES), _F32))
    return pl.pallas_call(
        functools.partial(_oproj_kernel, n_in=n_in, route=route),
        grid=(B, S // tm),
        in_specs=in_specs, out_specs=out_specs, out_shape=out_shape,
        compiler_params=_cparams(("parallel", "parallel")),
        name="oproj_route" if route else "oproj",
    )(*args)


def _swiglu_hidden(x, wg_ref, wu_ref, a_scr, tf):
    F = wg_ref.shape[1]
    for c0 in range(0, F, tf):
        c1 = min(c0 + tf, F)
        g = jnp.dot(x, wg_ref[:, c0:c1], preferred_element_type=_F32)
        u = jnp.dot(x, wu_ref[:, c0:c1], preferred_element_type=_F32)
        a_scr[:, c0:c1] = (_silu(g) * u).astype(_BF)


def _ffn_kernel(x_ref, h_ref, mod_ref, modn_ref, gn_ref, wg_ref, wu_ref, wd_ref, x_out, h_out, a_scr, *, tf):
    _swiglu_hidden(h_ref[...], wg_ref, wu_ref, a_scr, tf)
    y = jnp.dot(a_scr[...], wd_ref[...], preferred_element_type=_F32)
    x2 = x_ref[...] + mod_ref[5:6, :] * y
    x_out[...] = x2
    h_out[...] = _modulate(x2, gn_ref[...], modn_ref[0:1, :], modn_ref[1:2, :]).astype(h_out.dtype)


def _ffn(x, h, mod, modn, gn, wg, wu, wd, tm, tf):
    B, S, D = x.shape
    T = B * S
    F = wg.shape[1]
    per_b = S // tm
    x2 = x.reshape(T, D)
    h2 = h.reshape(T, D)
    tok = pl.BlockSpec((tm, D), lambda i: (i, 0))
    modspec = pl.BlockSpec((None, 6, D), lambda i: (i // per_b, 0, 0))
    once = pl.Buffered(1)
    xo, ho = pl.pallas_call(
        functools.partial(_ffn_kernel, tf=tf),
        grid=(T // tm,),
        in_specs=[tok, tok, modspec, modspec, pl.BlockSpec((1, D), lambda i: (0, 0)),
                  pl.BlockSpec((D, F), lambda i: (0, 0), pipeline_mode=once),
                  pl.BlockSpec((D, F), lambda i: (0, 0), pipeline_mode=once),
                  pl.BlockSpec((F, D), lambda i: (0, 0), pipeline_mode=once)],
        out_specs=[tok, tok],
        out_shape=[jax.ShapeDtypeStruct((T, D), _F32), jax.ShapeDtypeStruct((T, D), _BF)],
        scratch_shapes=[pltpu.VMEM((tm, F), _BF)],
        compiler_params=_cparams(("parallel",)),
        name="ffn_dense",
    )(x2, h2, mod, modn, gn, wg, wu, wd)
    return xo.reshape(B, S, D), ho.reshape(B, S, D)


def _qkv_kernel(h_ref, w_ref, gains_ref, rope_ref, q_ref, k_ref, v_ref, n_scr):
    tm, D = h_ref.shape
    h = h_ref[...]
    lane = lax.broadcasted_iota(jnp.int32, (tm, LANES), 1)
    head_a = (lane & 63) < 32

    def head_pair_norm(v, g):
        sq = v * v
        s_a = jnp.sum(jnp.where(head_a, sq, 0.0), axis=-1, keepdims=True)
        s_b = jnp.sum(jnp.where(head_a, 0.0, sq), axis=-1, keepdims=True)
        r = lax.rsqrt(jnp.where(head_a, s_a, s_b) * (1.0 / HEAD_DIM) + EPS)
        return v * r * g

    y = jnp.dot(h, w_ref[...], preferred_element_type=_F32)
    for p in range(D // LANES):
        sl = slice(p * LANES, (p + 1) * LANES)
        n_scr[:, sl] = head_pair_norm(y[:, sl], gains_ref[0:1, :])
        n_scr[:, D + p * LANES:D + (p + 1) * LANES] = head_pair_norm(
            y[:, D + p * LANES:D + (p + 1) * LANES], gains_ref[1:2, :])
        v_ref[p] = y[:, 2 * D + p * LANES:2 * D + (p + 1) * LANES].astype(_BF)

    @pl.when(pl.program_id(1) >= 0)
    def _():
        c, s = rope_ref[:, 0:128], rope_ref[:, 128:256]
        for p in range(D // LANES):
            vq = n_scr[:, p * LANES:(p + 1) * LANES]
            vk = n_scr[:, D + p * LANES:D + (p + 1) * LANES]
            q_ref[p] = (vq * c + pltpu.roll(vq, 64, 1) * s).astype(_BF)
            k_ref[p] = (vk * c + pltpu.roll(vk, 64, 1) * s).astype(_BF)


def _qkv(h, w, gains, rope, tm):
    B, S, D = h.shape
    tok = pl.BlockSpec((None, tm, D), lambda b, i: (b, i, 0))
    hp = pl.BlockSpec((None, D // LANES, tm, LANES), lambda b, i: (b, 0, i, 0))
    sds = jax.ShapeDtypeStruct((B, D // LANES, S, LANES), _BF)
    return pl.pallas_call(
        _qkv_kernel,
        grid=(B, S // tm),
        in_specs=[tok, pl.BlockSpec(w.shape, lambda b, i: (0, 0)),
                  pl.BlockSpec((2, LANES), lambda b, i: (0, 0)),
                  pl.BlockSpec((tm, 2 * LANES), lambda b, i: (i, 0))],
        out_specs=[hp, hp, hp],
        out_shape=[sds, sds, sds],
        scratch_shapes=[pltpu.VMEM((tm, 2 * D), _F32)],
        compiler_params=_cparams(("parallel", "parallel")),
        name="qkv_dil",
    )(h, w, gains, rope)


DIL_GROUP = 4


def _dil_kernel(q_ref, k_ref, v_ref, o_ref, tmp, qp, kp, vp, acc, mm, ll, accp, mmp, llp, bias_scr, *, tl):
    S = q_ref.shape[0]
    ng = S // DIL_GROUP
    pitch = DIL_GROUP + 1

    def spread(dst, src):
        for j in range(DIL_GROUP):
            dst[pl.ds(j, ng, stride=pitch), :] = src[pl.ds(j, ng, stride=DIL_GROUP), :]

    for src, dst in ((q_ref, qp), (k_ref, kp), (v_ref, vp)):
        tmp[...] = src[...].astype(_F32)
        spread(dst, tmp)
    lane = lax.broadcasted_iota(jnp.int32, (tl, LANES), 1)
    lo = lane < 64
    head_a = (lane & 63) < 32
    for bi, (window, d) in enumerate(DIL_PATTERNS):
        L = S // d
        half = window // (2 * d)
        t = min(tl, L)
        W = min(t + 2 * half, L)
        nt = L // t
        if d > 1:
            assert d % DIL_GROUP == 0
        sd = d * pitch // DIL_GROUP

        def rows(r, first, n, d=d, sd=sd):
            if d == 1:
                return pl.ds(first, n)
            return pl.ds(r + r // DIL_GROUP + sd * first, n, stride=sd)

        assert t % half == 0 and W <= t + 2 * half
        qa = lax.broadcasted_iota(jnp.int32, (t, W), 0)
        kb = lax.broadcasted_iota(jnp.int32, (t, W), 1)
        for case in range(3):
            ok = jnp.abs(kb - qa - case * half) <= half
            bias_scr[case, 0:t, 0:W] = jnp.where(ok, 0.0, NEG)

        def body(idx, carry, bi=bi, d=d, L=L, half=half, t=t, W=W, nt=nt, rows=rows):
            r = idx // nt
            l0 = (idx % nt) * t
            start = jnp.clip(l0 - half, 0, L - W)
            if d == 1:
                qt = q_ref[pl.ds(pl.multiple_of(l0, t), t), :]
                kw = k_ref[pl.ds(pl.multiple_of(start, 64), W), :]
                vw = v_ref[pl.ds(pl.multiple_of(start, 64), W), :]
            else:
                qt = qp[rows(r, l0, t), :].astype(_BF)
                kw = kp[rows(r, start, W), :].astype(_BF)
                vw = vp[rows(r, start, W), :].astype(_BF)
            lo_t = lo[:t]
            qa_t = head_a[:t]
            zq = jnp.zeros_like(qt)
            q2 = jnp.concatenate([jnp.where(qa_t, qt, zq), jnp.where(qa_t, zq, qt)], axis=0)
            s = lax.dot_general(q2, kw, (((1,), (1,)), ((), ())), preferred_element_type=_F32)
            bias = bias_scr[(l0 - start) // half, 0:t, 0:W]
            s = s + jnp.concatenate([bias, bias], axis=0)
            m = jnp.max(s, axis=-1, keepdims=True)
            p = jnp.exp2(s - m)
            den = jnp.sum(p, axis=-1, keepdims=True)
            o2 = jnp.dot(p.astype(_BF), vw, preferred_element_type=_F32)
            o_new = jnp.where(lo_t, o2[:t], o2[t:])
            m_new = jnp.where(lo_t, m[:t], m[t:])
            l_new = jnp.where(lo_t, den[:t], den[t:])
            tok = rows(r, l0, t)
            if d == 1:
                assert bi == 0
                acc[tok, :] = o_new
                mm[tok, :] = m_new
                ll[tok, :] = l_new
            else:
                m_old = mmp[tok, :]
                mx = jnp.maximum(m_old, m_new)
                a_old = jnp.exp2(m_old - mx)
                a_new = jnp.exp2(m_new - mx)
                accp[tok, :] = accp[tok, :] * a_old + o_new * a_new
                llp[tok, :] = llp[tok, :] * a_old + l_new * a_new
                mmp[tok, :] = mx
            return carry

        lax.fori_loop(0, d * nt, body, 0, unroll=min(8, d * nt))
        if bi == 0:
            spread(accp, acc)
            spread(mmp, mm)
            spread(llp, ll)
    for j in range(DIL_GROUP):
        grp = pl.ds(j, ng, stride=pitch)
        tmp[pl.ds(j, ng, stride=DIL_GROUP), :] = accp[grp, :] / llp[grp, :]
    o_ref[...] = tmp[...].astype(o_ref.dtype)


def _dilated(q, k, v, tl):
    B, P, S, _ = q.shape
    spec = pl.BlockSpec((None, None, S, LANES), lambda b, p: (b, p, 0, 0))
    scr = pltpu.VMEM((S, LANES), _F32)
    scrp = pltpu.VMEM((S // DIL_GROUP * (DIL_GROUP + 1), LANES), _F32)
    return pl.pallas_call(
        functools.partial(_dil_kernel, tl=tl),
        grid=(B, P),
        in_specs=[spec, spec, spec],
        out_specs=spec,
        out_shape=jax.ShapeDtypeStruct((B, P, S, LANES), _BF),
        scratch_shapes=[scr, scrp, scrp, scrp, scr, scr, scr, scrp, scrp, scrp,
                        pltpu.VMEM((3, tl, tl + max(w // d for w, d in DIL_PATTERNS)), _F32)],
        compiler_params=_cparams(("parallel", "parallel")),
        name="dil_attn",
    )(q, k, v)


def _moe_kernel(te_ref, tv_ref, src_hbm, dst_hbm, h_hbm, wg_ref, wu_ref, wd_ref, out_hbm,
                xbuf, xbf, a_scr, ybuf, src_s0, src_s1, dst_s, gsem, ssem, isem, *, tf, dump0, n_dump):
    i = pl.program_id(0)
    nt = pl.num_programs(0)
    tm = xbf.shape[0]
    slot = i % 2

    def is_valid(t):
        return jnp.logical_and(jnp.logical_and(t >= 0, t < nt), tv_ref[jnp.clip(t, 0, nt - 1)] == 1)

    valid = is_valid(i)
    prev_valid = is_valid(i - 1)
    nxt_valid = is_valid(i + 1)

    src_bufs = (src_s0, src_s1)

    def src_copy(tile, par):
        return pltpu.make_async_copy(src_hbm.at[tile], src_bufs[par], isem.at[par])

    def dst_copy(tile):
        return pltpu.make_async_copy(dst_hbm.at[tile], dst_s, isem.at[2])

    def tile_at(first):
        return pl.ds(pl.multiple_of(first, SUBLANES), SUBLANES)

    def gather_row(r, par):
        return pltpu.make_async_copy(h_hbm.at[tile_at(src_bufs[par][r])],
                                     xbuf.at[par, tile_at(r * SUBLANES)], gsem.at[par])

    def scatter_row(r):
        return pltpu.make_async_copy(ybuf.at[tile_at(r * SUBLANES)], out_hbm.at[tile_at(dst_s[r])],
                                     ssem.at[0])

    def gather_wait(slot_):
        return pltpu.make_async_copy(h_hbm.at[pl.ds(0, tm * SUBLANES)], xbuf.at[slot_], gsem.at[slot_])

    def scatter_wait():
        return pltpu.make_async_copy(ybuf, out_hbm.at[pl.ds(0, tm * SUBLANES)], ssem.at[0])

    def for_rows(fn):
        def body(r, c):
            fn(r)
            return c
        lax.fori_loop(0, tm, body, 0, unroll=8)

    @pl.when(i == 0)
    def _():
        ybuf[...] = jnp.zeros_like(ybuf)
        for k in range(n_dump):
            pltpu.make_async_copy(ybuf, out_hbm.at[pl.ds((dump0 + k * tm) * SUBLANES, tm * SUBLANES)],
                                  ssem.at[0]).start()
        for k in range(n_dump):
            scatter_wait().wait()

    @pl.when(jnp.logical_and(i == 0, valid))
    def _():
        src_copy(0, 0).start()
        src_copy(0, 0).wait()
        for_rows(lambda r: gather_row(r, 0).start())

    @pl.when(jnp.logical_and(i == 0, nxt_valid))
    def _():
        src_copy(1, 1).start()

    @pl.when(valid)
    def _():
        dst_copy(i).start()

    for par in range(2):
        @pl.when(jnp.logical_and(nxt_valid, slot == par))
        def _(par=par):
            src_copy(i + 1, 1 - par).wait()
            for_rows(lambda r: gather_row(r, 1 - par).start())

    for par in range(2):
        @pl.when(jnp.logical_and(valid, slot == par))
        def _(par=par):
            gather_wait(par).wait()
            xbf[...] = _tiles_to_rows(xbuf.at[par], tm).astype(_BF)

    @pl.when(valid)
    def _():
        _swiglu_hidden(xbf[...], wg_ref, wu_ref, a_scr, tf)

    @pl.when(prev_valid)
    def _():
        scatter_wait().wait()

    @pl.when(valid)
    def _():
        _rows_to_tiles(ybuf, jnp.dot(a_scr[...], wd_ref[...], preferred_element_type=_F32))
        dst_copy(i).wait()
        for_rows(lambda r: scatter_row(r).start())

    for par in range(2):
        @pl.when(jnp.logical_and(is_valid(i + 2), slot == par))
        def _(par=par):
            src_copy(i + 2, par).start()

    @pl.when(jnp.logical_and(i == nt - 1, valid))
    def _():
        scatter_wait().wait()


def _moe(h, tile_e, tile_v, src, dst, wg, wu, wd, n_out_rows, tm, tf):
    T = h.shape[0] // SUBLANES
    D = wg.shape[1]
    nt = tile_e.shape[0]
    F = wg.shape[2]
    dump0 = T * TOP_K
    once = pl.Buffered(1)
    grid_spec = pltpu.PrefetchScalarGridSpec(
        num_scalar_prefetch=2,
        grid=(nt,),
        in_specs=[pl.BlockSpec(memory_space=pl.ANY), pl.BlockSpec(memory_space=pl.ANY),
                  pl.BlockSpec(memory_space=pl.ANY),
                  pl.BlockSpec((None, D, F), lambda i, te, tv: (te[i], 0, 0), pipeline_mode=once),
                  pl.BlockSpec((None, D, F), lambda i, te, tv: (te[i], 0, 0), pipeline_mode=once),
                  pl.BlockSpec((None, F, D), lambda i, te, tv: (te[i], 0, 0), pipeline_mode=once)],
        out_specs=pl.BlockSpec(memory_space=pl.ANY),
        scratch_shapes=[pltpu.VMEM((2, tm * SUBLANES, LANES), _F32), pltpu.VMEM((tm, D), _BF),
                        pltpu.VMEM((tm, F), _BF), pltpu.VMEM((tm * SUBLANES, LANES), _F32),
                        pltpu.SMEM((tm,), jnp.int32), pltpu.SMEM((tm,), jnp.int32),
                        pltpu.SMEM((tm,), jnp.int32),
                        pltpu.SemaphoreType.DMA((2,)), pltpu.SemaphoreType.DMA((1,)),
                        pltpu.SemaphoreType.DMA((3,))],
    )
    return pl.pallas_call(
        functools.partial(_moe_kernel, tf=tf, dump0=dump0, n_dump=(n_out_rows - dump0) // tm),
        grid_spec=grid_spec,
        out_shape=jax.ShapeDtypeStruct((n_out_rows * SUBLANES, LANES), _F32),
        compiler_params=_cparams(("arbitrary",)),
        name="moe_experts",
    )(tile_e, tile_v, src, dst, h, wg, wu, wd)


def _route_plan(route, T, tm):
    A = T * TOP_K
    e_flat = route[:, :TOP_K].astype(jnp.int32).reshape(A)
    order = jnp.argsort(e_flat, stable=True).astype(jnp.int32)
    counts = jnp.sum((e_flat[:, None] == jnp.arange(N_EXPERTS, dtype=jnp.int32)[None, :]).astype(jnp.int32), axis=0)
    starts = jnp.cumsum(counts) - counts
    pcounts = (counts + tm - 1) // tm * tm
    pends = jnp.cumsum(pcounts)
    pstarts = pends - pcounts
    nt = A // tm + N_EXPERTS
    tile0 = jnp.arange(nt, dtype=jnp.int32) * tm
    tile_v = (tile0 < pends[-1]).astype(jnp.int32)
    last_valid = jnp.maximum(pends[-1] // tm - 1, 0)
    tile_e_raw = jnp.minimum(jnp.searchsorted(pends, tile0, side="right"), N_EXPERTS - 1).astype(jnp.int32)
    tile_e = jnp.where(tile_v == 1, tile_e_raw, tile_e_raw[last_valid])
    r = jnp.arange(nt * tm, dtype=jnp.int32)
    e_r = jnp.repeat(tile_e, tm)
    within = r - pstarts[e_r]
    ok = jnp.logical_and(within < counts[e_r], jnp.repeat(tile_v, tm) == 1)
    a = order[jnp.clip(starts[e_r] + within, 0, A - 1)]
    src = jnp.where(ok, a // TOP_K, 0).astype(jnp.int32)
    dump = A + e_r * tm + jnp.clip(within - counts[e_r], 0, tm - 1)
    dst = jnp.where(ok, (a % TOP_K) * T + a // TOP_K, dump).astype(jnp.int32)
    return (tile_e, tile_v, (src * SUBLANES).reshape(nt, tm), (dst * SUBLANES).reshape(nt, tm),
            A + N_EXPERTS * tm)


def _combine_kernel(x_ref, y1_ref, y2_ref, r_ref, mod_ref, o_ref):
    tm = x_ref.shape[0]
    y = _tiles_to_rows(y1_ref, tm) * r_ref[:, 2:3] + _tiles_to_rows(y2_ref, tm) * r_ref[:, 3:4]
    o_ref[...] = x_ref[...] + mod_ref[5:6, :] * y


def _combine(x, y2, route, mod, tm):
    B, S, D = x.shape
    T = B * S
    per_b = S // tm
    out = pl.pallas_call(
        _combine_kernel,
        grid=(T // tm,),
        in_specs=[pl.BlockSpec((tm, D), lambda i: (i, 0)),
                  pl.BlockSpec((tm * SUBLANES, LANES), lambda i: (i, 0)),
                  pl.BlockSpec((tm * SUBLANES, LANES), lambda i: (T // tm + i, 0)),
                  pl.BlockSpec((tm, LANES), lambda i: (i, 0)),
                  pl.BlockSpec((None, 6, D), lambda i: (i // per_b, 0, 0))],
        out_specs=pl.BlockSpec((tm, D), lambda i: (i, 0)),
        out_shape=jax.ShapeDtypeStruct((T, D), _F32),
        compiler_params=_cparams(("parallel",)),
        name="moe_combine",
    )(x.reshape(T, D), y2, y2, route.reshape(T, LANES), mod)
    return out.reshape(B, S, D)


def _pad_cols(a, w):
    return jnp.pad(a, ((0, 0), (0, w - a.shape[1])))


def _rope_tables_even(S):
    pos = jnp.arange(S, dtype=jnp.int32)
    inv = ROPE_THETA ** (-jnp.arange(0, 32, 2, dtype=_F32) / 32)
    def cs(p):
        ang = p.astype(_F32)[:, None] * inv[None, :]
        return jnp.cos(ang), jnp.sin(ang)
    one = lambda w: jnp.ones((S, w), _F32)
    zero = lambda w: jnp.zeros((S, w), _F32)
    c, s = cs(pos)
    ca = jnp.concatenate([c, one(48), c, one(48)], 1)
    sa = jnp.concatenate([-s, zero(48), s, zero(48)], 1)
    cr, sr = cs(pos // GRID_W)
    cc, sc = cs(pos % GRID_W)
    cb = jnp.concatenate([cr, cc, one(32), cr, cc, one(32)], 1)
    sb = jnp.concatenate([-sr, -sc, zero(32), sr, sc, zero(32)], 1)
    return jnp.concatenate([ca, sa, cb, sb], 1)


def _slot_maps():
    r = MLA_ROPE // 2
    mla = ([MLA_NOPE + i for i in range(r)] + list(range(0, 64 - r))
           + [MLA_NOPE + r + i for i in range(r)] + list(range(64 - r, MLA_NOPE)))
    mla += [-1] * (LANES - len(mla))
    q = HEAD_DIM // 4
    gqa = (list(range(0, q)) + list(range(2 * q, 3 * q)) + [-1] * (64 - 2 * q)
           + list(range(q, 2 * q)) + list(range(3 * q, 4 * q)) + [-1] * (64 - 2 * q))
    return mla, gqa


def _to_slots(a, lane_map):
    idx = jnp.asarray([max(i, 0) for i in lane_map], jnp.int32)
    keep = jnp.asarray([1.0 if i >= 0 else 0.0 for i in lane_map], a.dtype)
    return jnp.take(a, idx, axis=-1) * keep


def _rope_tables_odd(S):
    pos = jnp.arange(S, dtype=_F32)
    inv = ROPE_THETA ** (-jnp.arange(0, HEAD_DIM, 2, dtype=_F32) / HEAD_DIM)
    ang = pos[:, None] * inv[None, :]
    c, s = jnp.cos(ang), jnp.sin(ang)
    return jnp.concatenate([c, c, c, c, -s, -s, s, s], 1)


def _tiles(S):
    return dict(tm_pre=min(S, 512), tq=min(S, 2048), tm_o=min(S, 512),
                tm_ffn=min(S, 512), tf_ffn=512, tm_qkv=min(S, 512),
                tl=128, tm_moe=min(S, 512), tf_moe=512, tm_c=min(S, 512))


def kernel(x, c, ada_even_w, ada_even_b, norm_even_mix, norm_even_ffn, even_w_in, mla_q_norm, mla_w_uq, mla_kv_norm, mla_w_ukv, mla_q_gain, mla_k_gain, gqa_q_gain, gqa_k_gain, even_w_out, ffn_w_gate, ffn_w_up, ffn_w_down, ada_odd_w, ada_odd_b, norm_odd_mix, norm_odd_ffn, dil_w_qkv, dil_q_gain, dil_k_gain, dil_w_out, moe_router, moe_w_gate, moe_w_up, moe_w_down):
    B, S, D = x.shape
    T = B * S
    cfg = _tiles(S)

    mod_e = _ada_mod(c, ada_even_w[0], ada_even_b[0]).reshape(B, 6, D)
    mod_o = _ada_mod(c, ada_odd_w[0], ada_odd_b[0]).reshape(B, 6, D)

    w = even_w_in[0]
    sp = [MLA_Q_RANK, MLA_Q_RANK + MLA_KV_RANK, MLA_Q_RANK + MLA_KV_RANK + MLA_ROPE]
    sp.append(sp[-1] + GQA_HEADS * HEAD_DIM)
    sp.append(sp[-1] + GQA_KV_HEADS * HEAD_DIM)
    w_cq, w_ckv, w_kpe = w[:, :sp[0]], w[:, sp[0]:sp[1]], w[:, sp[1]:sp[2]]
    w_qb, w_kb, w_vb = w[:, sp[2]:sp[3]], w[:, sp[3]:sp[4]], w[:, sp[4]:]
    mla_map, gqa_map = _slot_maps()
    na = MLA_NOPE + MLA_ROPE
    nope_only = [i if 0 <= i < MLA_NOPE else -1 for i in mla_map]
    rope_only = [i - MLA_NOPE if i >= MLA_NOPE else -1 for i in mla_map]
    gslots = lambda a, n: _to_slots(a.reshape(D, n, HEAD_DIM), gqa_map).reshape(D, n * LANES)
    w_in = jnp.concatenate([w_cq, w_ckv, _to_slots(w_kpe, rope_only), gslots(w_qb, GQA_HEADS),
                            gslots(w_kb, GQA_KV_HEADS)], axis=1).astype(_BF)
    w_uq = _to_slots(mla_w_uq[0].reshape(MLA_Q_RANK, MLA_HEADS, na), mla_map)
    w_uq = w_uq.reshape(MLA_Q_RANK, MLA_HEADS * LANES).astype(_BF)
    w_ukv = mla_w_ukv[0].reshape(MLA_KV_RANK, MLA_HEADS, MLA_NOPE + MLA_V)
    w_uk = _to_slots(w_ukv[:, :, :MLA_NOPE], nope_only).reshape(MLA_KV_RANK, MLA_HEADS * LANES).astype(_BF)
    w_uvt = w_ukv[:, :, MLA_NOPE:].reshape(MLA_KV_RANK, MLA_HEADS * MLA_V).T.astype(_BF)
    w_vbt = w_vb.T.astype(_BF)
    gains_e = jnp.stack([_to_slots(mla_q_gain[0], mla_map) * (na ** -0.5 * LOG2E),
                         _to_slots(mla_k_gain[0], mla_map),
                         _to_slots(gqa_q_gain[0], gqa_map) * (HEAD_DIM ** -0.5 * LOG2E),
                         _to_slots(gqa_k_gain[0], gqa_map)])
    rope_e = _rope_tables_even(S)
    q_all, k_all, vt_all = _pre_even(
        x, mod_e, norm_even_mix[0].reshape(1, D), w_in, mla_q_norm[0].reshape(1, -1), w_uq,
        mla_kv_norm[0].reshape(1, -1), w_uk, w_uvt, w_vbt, gains_e, rope_e, cfg["tm_pre"])
    o_a = _attention(q_all, k_all, vt_all, q_off=0, k_off=0, shared_kv=False, tq=cfg["tq"])
    o_b = _attention(q_all, k_all, vt_all, q_off=MLA_HEADS, k_off=MLA_HEADS, shared_kv=True, tq=cfg["tq"])
    x1, h1 = _oproj(x, mod_e, norm_even_ffn[0].reshape(1, D), even_w_out[0].astype(_BF), [o_a, o_b],
                    cfg["tm_o"], _BF)
    x2, h2 = _ffn(x1, h1, mod_e, mod_o, norm_odd_mix[0].reshape(1, D), ffn_w_gate[0].astype(_BF),
                  ffn_w_up[0].astype(_BF), ffn_w_down[0].astype(_BF), cfg["tm_ffn"], cfg["tf_ffn"])

    hh = HEAD_DIM // 2
    pair = lambda v: jnp.concatenate([v[:hh], v[:hh], v[hh:], v[hh:]])
    gains_o = jnp.stack([pair(dil_q_gain[0]) * (HEAD_DIM ** -0.5 * LOG2E), pair(dil_k_gain[0])])
    wq, wk, wv = jnp.split(dil_w_qkv[0], 3, axis=1)
    perm = lambda w: w.reshape(D, D // LANES, 2, 2, hh).transpose(0, 1, 3, 2, 4).reshape(D, D)
    w_qkv = jnp.concatenate([perm(wq), perm(wk), wv], axis=1).astype(_BF)
    qd, kd, vd = _qkv(h2, w_qkv, gains_o, _rope_tables_odd(S), cfg["tm_qkv"])
    o_d = _dilated(qd, kd, vd, cfg["tl"])
    r32 = _pad_cols(moe_router[0], LANES)
    r_hi = r32.astype(_BF)
    r_lo = (r32 - r_hi.astype(_F32)).astype(_BF)
    x3, h3, route = _oproj(x2, mod_o, norm_odd_ffn[0].reshape(1, D), dil_w_out[0].astype(_BF), [o_d],
                           cfg["tm_o"], _F32, router=jnp.stack([r_hi, r_lo]))
    tm = cfg["tm_moe"]
    tile_e, tile_v, src, dst, n_rows = _route_plan(route.reshape(T, LANES), T, tm)
    y2 = _moe(h3, tile_e, tile_v, src, dst, moe_w_gate[0].astype(_BF),
              moe_w_up[0].astype(_BF), moe_w_down[0].astype(_BF), n_rows, tm, cfg["tf_moe"])
    return _combine(x3, y2, route, mod_o, cfg["tm_c"])
```

```python
import functools
import math

import jax
import jax.numpy as jnp
from jax import lax
from jax.experimental import pallas as pl
from jax.experimental.pallas import tpu as pltpu

_BF = jnp.bfloat16
_F32 = jnp.float32

GRID_W = 64
HEAD_DIM = 64
ROPE_THETA = 10000.0
EPS = 1e-6
MLA_HEADS = 8
MLA_Q_RANK = 256
MLA_KV_RANK = 128
MLA_NOPE = 64
MLA_ROPE = 32
MLA_V = 64
GQA_HEADS = 8
GQA_KV_HEADS = 2
DIL_PATTERNS = ((128, 1), (512, 4), (2048, 16))
N_EXPERTS = 8
TOP_K = 2
NEG = -1e30
LOG2E = math.log2(math.e)

LANES = 128
VMEM_LIMIT = 56 * 1024 * 1024


def _cparams(sem, vmem=VMEM_LIMIT):
    return pltpu.CompilerParams(dimension_semantics=sem, vmem_limit_bytes=vmem)


def _silu(x):
    return x / (1.0 + jnp.exp(-x))


def _modulate(x, g, shift, scale):
    ms = jnp.mean(x * x, axis=-1, keepdims=True)
    return x * lax.rsqrt(ms + EPS) * g * (1.0 + scale) + shift


def _norm(v, n, g):
    return v * lax.rsqrt(jnp.sum(v * v, axis=-1, keepdims=True) * (1.0 / n) + EPS) * g


SUBLANES = 8


def _rows_to_tiles(ref, x):
    tm = x.shape[0]
    for c in range(SUBLANES):
        ref[pl.ds(c, tm, stride=SUBLANES), :] = x[:, c * LANES:(c + 1) * LANES]


def _tiles_to_rows(ref, tm):
    return jnp.concatenate([ref[pl.ds(c, tm, stride=SUBLANES), :] for c in range(SUBLANES)], axis=1)


def _rope(x, c, sa, sb, k):
    return x * c + pltpu.roll(x, LANES - k, 1) * sa + pltpu.roll(x, k, 1) * sb


def _mod_kernel(c_ref, w_ref, b_ref, o_ref):
    sc = _silu(c_ref[...])
    o_ref[...] = jnp.dot(sc.astype(_BF), w_ref[...].astype(_BF),
                         preferred_element_type=_F32) + b_ref[...]


def _ada_mod(c, w, b):
    B, D = c.shape
    N = w.shape[1]
    tn = min(N, 1536)
    return pl.pallas_call(
        _mod_kernel,
        grid=(N // tn,),
        in_specs=[pl.BlockSpec((B, D), lambda j: (0, 0)),
                  pl.BlockSpec((D, tn), lambda j: (0, j)),
                  pl.BlockSpec((1, tn), lambda j: (0, j))],
        out_specs=pl.BlockSpec((B, tn), lambda j: (0, j)),
        out_shape=jax.ShapeDtypeStruct((B, N), _F32),
        compiler_params=_cparams(("arbitrary",)),
        name="ada_mod",
    )(c, w, b.reshape(1, N))


def _pre_even_kernel(x_ref, mod_ref, g_ref, w_in_ref, qn_ref, w_uq_ref, kvn_ref, w_uk_ref, w_uvt_ref,
                     w_vbt_ref, gains_ref, rope_ref, q_ref, k_ref, vt_ref, nq_scr, nk_scr):
    h = _modulate(x_ref[...], g_ref[...], mod_ref[0:1, :], mod_ref[1:2, :]).astype(_BF)
    y = jnp.dot(h, w_in_ref[...], preferred_element_type=_F32)
    cqn = _norm(y[:, 0:256], MLA_Q_RANK, qn_ref[...]).astype(_BF)
    qa = jnp.dot(cqn, w_uq_ref[...], preferred_element_type=_F32)
    ckvn = _norm(y[:, 256:384], MLA_KV_RANK, kvn_ref[...]).astype(_BF)
    kn = jnp.dot(ckvn, w_uk_ref[...], preferred_element_type=_F32)
    kpe = y[:, 384:512]
    nt_dims = (((1,), (1,)), ((), ()))
    vt_a = lax.dot_general(w_uvt_ref[...], ckvn, nt_dims, preferred_element_type=_F32).astype(_BF)
    vt_b = lax.dot_general(w_vbt_ref[...], h, nt_dims, preferred_element_type=_F32).astype(_BF)
    tm = x_ref.shape[0]
    ones_blk = jnp.where(lax.broadcasted_iota(jnp.int32, (LANES - MLA_V, tm), 0) == 0, 1.0, 0.0).astype(_BF)
    for hh in range(MLA_HEADS + GQA_KV_HEADS):
        src = vt_a[hh * MLA_V:(hh + 1) * MLA_V] if hh < MLA_HEADS else \
            vt_b[(hh - MLA_HEADS) * HEAD_DIM:(hh - MLA_HEADS + 1) * HEAD_DIM]
        vt_ref[hh * LANES:hh * LANES + MLA_V, :] = src
        vt_ref[hh * LANES + MLA_V:(hh + 1) * LANES, :] = ones_blk
    ca, sa = rope_ref[:, 0:128], rope_ref[:, 128:256]
    cb, sb = rope_ref[:, 256:384], rope_ref[:, 384:512]
    gqa, gka = gains_ref[0:1, :], gains_ref[1:2, :]
    gqb, gkb = gains_ref[2:3, :], gains_ref[3:4, :]
    na = MLA_NOPE + MLA_ROPE

    for hh in range(MLA_HEADS):
        sl = slice(hh * LANES, (hh + 1) * LANES)
        nq_scr[:, sl] = _norm(qa[:, sl], na, gqa)
        nk_scr[:, sl] = _norm(kn[:, sl] + kpe, na, gka)
    for hh in range(GQA_HEADS):
        src = slice(512 + hh * LANES, 512 + (hh + 1) * LANES)
        dst = slice((MLA_HEADS + hh) * LANES, (MLA_HEADS + hh + 1) * LANES)
        nq_scr[:, dst] = _norm(y[:, src], HEAD_DIM, gqb)
    for g in range(GQA_KV_HEADS):
        src = slice(1536 + g * LANES, 1536 + (g + 1) * LANES)
        dst = slice((MLA_HEADS + g) * LANES, (MLA_HEADS + g + 1) * LANES)
        nk_scr[:, dst] = _norm(y[:, src], HEAD_DIM, gkb)

    @pl.when(pl.program_id(1) >= 0)
    def _():
        def rope(ref, out, n_a, n_all, row):
            for hh in range(n_all):
                sl = slice(hh * LANES, (hh + 1) * LANES)
                c, s = (ca, sa) if hh < n_a else (cb, sb)
                b = gains_ref[row:row + 1, :] if hh < n_a else gains_ref[row + 2:row + 3, :]
                v = ref[:, sl]
                out[:, sl] = (v * c + pltpu.roll(v, 64, 1) * s + b).astype(_BF)

        rope(nq_scr, q_ref, MLA_HEADS, MLA_HEADS + GQA_HEADS, 4)
        rope(nk_scr, k_ref, MLA_HEADS, MLA_HEADS + GQA_KV_HEADS, 5)


def _pre_even(x, mod, g, w_in, qn, w_uq, kvn, w_uk, w_uvt, w_vbt, gains, rope, tm):
    B, S, D = x.shape
    nq = (MLA_HEADS + GQA_HEADS) * LANES
    nk = (MLA_HEADS + GQA_KV_HEADS) * LANES
    nvt = (MLA_HEADS + GQA_KV_HEADS) * LANES
    full = lambda a: pl.BlockSpec(a.shape, lambda b, i: (0,) * a.ndim)
    tok = lambda w: pl.BlockSpec((None, tm, w), lambda b, i: (b, i, 0))
    return pl.pallas_call(
        _pre_even_kernel,
        grid=(B, S // tm),
        in_specs=[tok(D),
                  pl.BlockSpec((None, 6, D), lambda b, i: (b, 0, 0)),
                  full(g), full(w_in), full(qn), full(w_uq), full(kvn), full(w_uk), full(w_uvt),
                  full(w_vbt), full(gains),
                  pl.BlockSpec((tm, rope.shape[1]), lambda b, i: (i, 0))],
        out_specs=[tok(nq), tok(nk), pl.BlockSpec((None, None, nvt, tm), lambda b, i: (b, i, 0, 0))],
        out_shape=[jax.ShapeDtypeStruct((B, S, nq), _BF), jax.ShapeDtypeStruct((B, S, nk), _BF),
                   jax.ShapeDtypeStruct((B, S // tm, nvt, tm), _BF)],
        scratch_shapes=[pltpu.VMEM((tm, nq), _F32), pltpu.VMEM((tm, nk), _F32)],
        compiler_params=_cparams(("parallel", "parallel")),
        name="pre_even",
    )(x, mod, g, w_in, qn, w_uq, kvn, w_uk, w_uvt, w_vbt, gains, rope)


SCORE_BOUND_MAX = 60.0


def _attn_kernel(fast_ref, q_ref, k_ref, vt_ref, o_ref, s_scr, *, shared_kv):
    @pl.when(fast_ref[0] == 1)
    def _():
        _attn_fixed_shift(q_ref, k_ref, vt_ref, o_ref, shared_kv=shared_kv)

    @pl.when(fast_ref[0] != 1)
    def _():
        _attn_running_max(q_ref, k_ref, vt_ref, o_ref, s_scr, shared_kv=shared_kv)


def _attn_fixed_shift(q_ref, k_ref, vt_ref, o_ref, *, shared_kv):
    tq = q_ref.shape[0]
    nk, _, tk = vt_ref.shape
    qs = [q_ref[:, hh * LANES:(hh + 1) * LANES] for hh in range(2)]

    def body(j, carry):
        r0 = pl.multiple_of(j * tk, tk)
        out = []
        for hh in range(2):
            kc = 0 if shared_kv else hh * LANES
            k = k_ref[pl.ds(r0, tk), kc:kc + LANES]
            st = lax.dot_general(k, qs[hh], (((1,), (1,)), ((), ())), preferred_element_type=_F32)
            p = jnp.exp2(st).astype(_BF)
            out.append(carry[hh] + jnp.dot(vt_ref[j, kc:kc + LANES, :], p, preferred_element_type=_F32))
        return tuple(out)

    z = jnp.zeros((LANES, tq), _F32)
    res = lax.fori_loop(0, nk, body, (z, z), unroll=2)
    nv = HEAD_DIM
    halves = [acc[0:nv, :] / acc[nv:nv + 1, :] for acc in res]
    o_ref[...] = jnp.concatenate(halves, axis=0).T.astype(o_ref.dtype)


def _attn_running_max(q_ref, k_ref, vt_ref, o_ref, s_scr, *, shared_kv):
    tq = q_ref.shape[0]
    nk, _, tk = vt_ref.shape
    qs = [q_ref[:, hh * LANES:(hh + 1) * LANES] for hh in range(2)]

    def scores(j, hh):
        r0 = j * tk if isinstance(j, int) else pl.multiple_of(j * tk, tk)
        kc = 0 if shared_kv else hh * LANES
        k = k_ref[pl.ds(r0, tk), kc:kc + LANES]
        return lax.dot_general(k, qs[hh], (((1,), (1,)), ((), ())), preferred_element_type=_F32)

    def put_scores(j, hh, slot):
        st = scores(j, hh)
        s_scr[hh, slot] = st
        return jnp.max(st, axis=0, keepdims=True)

    def step(j, hh, carry, cur, last=False):
        m, mc, acc = carry
        mc_next = mc if last else put_scores(j + 1, hh, 1 - cur)
        st = s_scr[hh, cur]
        m_new = jnp.maximum(m, mc)
        alpha = jnp.exp2(m - m_new)
        p = jnp.exp2(st - m_new).astype(_BF)
        vc = 0 if shared_kv else hh * LANES
        acc = alpha * acc + jnp.dot(vt_ref[j, vc:vc + LANES, :], p, preferred_element_type=_F32)
        return m_new, mc_next, acc

    def body(jj, carry):
        c = list(carry)
        for sub in range(2):
            for hh in range(2):
                c[hh] = step(2 * jj + sub, hh, c[hh], sub)
        return tuple(c)

    init = tuple((jnp.full((1, tq), NEG, _F32), put_scores(0, hh, 0), jnp.zeros((LANES, tq), _F32))
                 for hh in range(2))
    res = lax.fori_loop(0, nk // 2 - 1, body, init)
    res = list(res)
    for sub in range(2):
        for hh in range(2):
            res[hh] = step(nk - 2 + sub, hh, res[hh], sub, last=(sub == 1))
    nv = HEAD_DIM
    halves = [acc[0:nv, :] / acc[nv:nv + 1, :] for (_, _, acc) in res]
    o_ref[...] = jnp.concatenate(halves, axis=0).T.astype(o_ref.dtype)


def _attention(fast, q, k, vt, *, q_off, k_off, shared_kv, tq):
    B, S, _ = q.shape
    nk, _, tk = vt.shape[1:]
    assert nk % 2 == 0 and nk >= 2
    npairs = 4
    if shared_kv:
        kspec = pl.BlockSpec((None, S, LANES), lambda b, p, i, f: (b, 0, k_off + p // 2))
        vspec = pl.BlockSpec((None, nk, LANES, tk), lambda b, p, i, f: (b, 0, k_off + p // 2, 0))
    else:
        kspec = pl.BlockSpec((None, S, 2 * LANES), lambda b, p, i, f: (b, 0, k_off // 2 + p))
        vspec = pl.BlockSpec((None, nk, 2 * LANES, tk), lambda b, p, i, f: (b, 0, k_off // 2 + p, 0))
    grid_spec = pltpu.PrefetchScalarGridSpec(
        num_scalar_prefetch=1,
        grid=(B, npairs, S // tq),
        in_specs=[pl.BlockSpec((None, tq, 2 * LANES), lambda b, p, i, f: (b, i, q_off // 2 + p)),
                  kspec, vspec],
        out_specs=pl.BlockSpec((None, tq, LANES), lambda b, p, i, f: (b, i, p)),
        scratch_shapes=[pltpu.VMEM((2, 2, tk, tq), _F32)],
    )
    return pl.pallas_call(
        functools.partial(_attn_kernel, shared_kv=shared_kv),
        grid_spec=grid_spec,
        out_shape=jax.ShapeDtypeStruct((B, S, npairs * LANES), _BF),
        compiler_params=_cparams(("parallel", "parallel", "parallel")),
        name="attn_gqa" if shared_kv else "attn_mla",
    )(fast, q, k, vt)


def _oproj_kernel(*refs, n_in, route):
    x_ref, mod_ref, g_ref, w_ref = refs[0], refs[1], refs[2], refs[3]
    o_refs = refs[4:4 + n_in]
    pos = 4 + n_in
    if route:
        r_ref = refs[pos]
        pos += 1
    x_out, h_out = refs[pos], refs[pos + 1]
    y = None
    off = 0
    for o_ref in o_refs:
        if len(o_ref.shape) == 3:
            o = jnp.concatenate([o_ref[p] for p in range(o_ref.shape[0])], axis=1)
        else:
            o = o_ref[...]
        w = o.shape[1]
        t = jnp.dot(o, w_ref[off:off + w, :], preferred_element_type=_F32)
        y = t if y is None else y + t
        off += w
    x1 = x_ref[...] + mod_ref[2:3, :] * y
    x_out[...] = x1
    h = _modulate(x1, g_ref[...], mod_ref[3:4, :], mod_ref[4:5, :])
    if route:
        _rows_to_tiles(h_out, h)
    else:
        h_out[...] = h.astype(h_out.dtype)
    if route:
        route_out = refs[pos + 2]
        hh = h.astype(_BF)
        hl = (h - hh.astype(_F32)).astype(_BF)
        rh, rl = r_ref[0], r_ref[1]
        logits = (jnp.dot(hh, rh, preferred_element_type=_F32)
                  + jnp.dot(hh, rl, preferred_element_type=_F32)
                  + jnp.dot(hl, rh, preferred_element_type=_F32))
        tm = logits.shape[0]
        lane = lax.broadcasted_iota(jnp.int32, (tm, LANES), 1)
        lg = jnp.where(lane < N_EXPERTS, logits, NEG)
        m1 = jnp.max(lg, axis=-1, keepdims=True)
        lanef = lane.astype(_F32)
        i1 = jnp.min(jnp.where(lg == m1, lanef, float(LANES)), axis=-1, keepdims=True)
        lg2 = jnp.where(lanef == i1, NEG, lg)
        m2 = jnp.max(lg2, axis=-1, keepdims=True)
        i2 = jnp.min(jnp.where(lg2 == m2, lanef, float(LANES)), axis=-1, keepdims=True)
        e = jnp.exp(m2 - m1)
        g1 = 1.0 / (1.0 + e)
        g2 = e / (1.0 + e)
        route_out[...] = jnp.where(lane == 0, i1,
                                   jnp.where(lane == 1, i2,
                                             jnp.where(lane == 2, g1, jnp.where(lane == 3, g2, 0.0))))


def _oproj(x, mod, g, w, o_list, tm, h_dtype, router=None):
    B, S, D = x.shape
    n_in = len(o_list)
    route = router is not None
    tok = lambda wd: pl.BlockSpec((None, tm, wd), lambda b, i: (b, i, 0))
    in_specs = [tok(D), pl.BlockSpec((None, 6, D), lambda b, i: (b, 0, 0)),
                pl.BlockSpec((1, D), lambda b, i: (0, 0)),
                pl.BlockSpec(w.shape, lambda b, i: (0, 0))]
    for o in o_list:
        if o.ndim == 4:
            in_specs.append(pl.BlockSpec((None, o.shape[1], tm, LANES), lambda b, i: (b, 0, i, 0)))
        else:
            in_specs.append(tok(o.shape[2]))
    args = [x, mod, g, w] + list(o_list)
    out_specs = [tok(D), tok(D)]
    out_shape = [jax.ShapeDtypeStruct((B, S, D), _F32), jax.ShapeDtypeStruct((B, S, D), h_dtype)]
    if route:
        assert D == SUBLANES * LANES
        per_b = S // tm
        out_specs[1] = pl.BlockSpec((tm * SUBLANES, LANES), lambda b, i: (b * per_b + i, 0))
        out_shape[1] = jax.ShapeDtypeStruct((B * S * SUBLANES, LANES), _F32)
        in_specs.append(pl.BlockSpec(router.shape, lambda b, i: (0, 0, 0)))
        args.append(router)
        out_specs.append(tok(LANES))
        out_shape.append(jax.ShapeDtypeStruct((B, S, LANES), _F32))
    return pl.pallas_call(
        functools.partial(_oproj_kernel, n_in=n_in, route=route),
        grid=(B, S // tm),
        in_specs=in_specs, out_specs=out_specs, out_shape=out_shape,
        compiler_params=_cparams(("parallel", "parallel")),
        name="oproj_route" if route else "oproj",
    )(*args)


def _swiglu_hidden(x, wg_ref, wu_ref, a_scr, tf):
    F = wg_ref.shape[1]
    for c0 in range(0, F, tf):
        c1 = min(c0 + tf, F)
        g = jnp.dot(x, wg_ref[:, c0:c1], preferred_element_type=_F32)
        u = jnp.dot(x, wu_ref[:, c0:c1], preferred_element_type=_F32)
        a_scr[:, c0:c1] = (_silu(g) * u).astype(_BF)


def _ffn_kernel(x_ref, h_ref, mod_ref, modn_ref, gn_ref, wg_ref, wu_ref, wd_ref, x_out, h_out, a_scr, *, tf):
    _swiglu_hidden(h_ref[...], wg_ref, wu_ref, a_scr, tf)
    y = jnp.dot(a_scr[...], wd_ref[...], preferred_element_type=_F32)
    x2 = x_ref[...] + mod_ref[5:6, :] * y
    x_out[...] = x2
    h_out[...] = _modulate(x2, gn_ref[...], modn_ref[0:1, :], modn_ref[1:2, :]).astype(h_out.dtype)


def _ffn(x, h, mod, modn, gn, wg, wu, wd, tm, tf):
    B, S, D = x.shape
    T = B * S
    F = wg.shape[1]
    per_b = S // tm
    x2 = x.reshape(T, D)
    h2 = h.reshape(T, D)
    tok = pl.BlockSpec((tm, D), lambda i: (i, 0))
    modspec = pl.BlockSpec((None, 6, D), lambda i: (i // per_b, 0, 0))
    once = pl.Buffered(1)
    xo, ho = pl.pallas_call(
        functools.partial(_ffn_kernel, tf=tf),
        grid=(T // tm,),
        in_specs=[tok, tok, modspec, modspec, pl.BlockSpec((1, D), lambda i: (0, 0)),
                  pl.BlockSpec((D, F), lambda i: (0, 0), pipeline_mode=once),
                  pl.BlockSpec((D, F), lambda i: (0, 0), pipeline_mode=once),
                  pl.BlockSpec((F, D), lambda i: (0, 0), pipeline_mode=once)],
        out_specs=[tok, tok],
        out_shape=[jax.ShapeDtypeStruct((T, D), _F32), jax.ShapeDtypeStruct((T, D), _BF)],
        scratch_shapes=[pltpu.VMEM((tm, F), _BF)],
        compiler_params=_cparams(("parallel",)),
        name="ffn_dense",
    )(x2, h2, mod, modn, gn, wg, wu, wd)
    return xo.reshape(B, S, D), ho.reshape(B, S, D)


def _qkv_kernel(h_ref, w_ref, gains_ref, rope_ref, q_ref, k_ref, v_ref, n_scr):
    tm, D = h_ref.shape
    h = h_ref[...]
    lane = lax.broadcasted_iota(jnp.int32, (tm, LANES), 1)
    head_a = (lane & 63) < 32

    def head_pair_norm(v, g):
        sq = v * v
        s_a = jnp.sum(jnp.where(head_a, sq, 0.0), axis=-1, keepdims=True)
        s_b = jnp.sum(jnp.where(head_a, 0.0, sq), axis=-1, keepdims=True)
        r = lax.rsqrt(jnp.where(head_a, s_a, s_b) * (1.0 / HEAD_DIM) + EPS)
        return v * r * g

    y = jnp.dot(h, w_ref[...], preferred_element_type=_F32)
    for p in range(D // LANES):
        sl = slice(p * LANES, (p + 1) * LANES)
        n_scr[:, sl] = head_pair_norm(y[:, sl], gains_ref[0:1, :])
        n_scr[:, D + p * LANES:D + (p + 1) * LANES] = head_pair_norm(
            y[:, D + p * LANES:D + (p + 1) * LANES], gains_ref[1:2, :])
        v_ref[p] = y[:, 2 * D + p * LANES:2 * D + (p + 1) * LANES].astype(_BF)

    @pl.when(pl.program_id(1) >= 0)
    def _():
        c, s = rope_ref[:, 0:128], rope_ref[:, 128:256]
        for p in range(D // LANES):
            vq = n_scr[:, p * LANES:(p + 1) * LANES]
            vk = n_scr[:, D + p * LANES:D + (p + 1) * LANES]
            q_ref[p] = (vq * c + pltpu.roll(vq, 64, 1) * s).astype(_BF)
            k_ref[p] = (vk * c + pltpu.roll(vk, 64, 1) * s).astype(_BF)


def _qkv(h, w, gains, rope, tm):
    B, S, D = h.shape
    tok = pl.BlockSpec((None, tm, D), lambda b, i: (b, i, 0))
    hp = pl.BlockSpec((None, D // LANES, tm, LANES), lambda b, i: (b, 0, i, 0))
    sds = jax.ShapeDtypeStruct((B, D // LANES, S, LANES), _BF)
    return pl.pallas_call(
        _qkv_kernel,
        grid=(B, S // tm),
        in_specs=[tok, pl.BlockSpec(w.shape, lambda b, i: (0, 0)),
                  pl.BlockSpec((2, LANES), lambda b, i: (0, 0)),
                  pl.BlockSpec((tm, 2 * LANES), lambda b, i: (i, 0))],
        out_specs=[hp, hp, hp],
        out_shape=[sds, sds, sds],
        scratch_shapes=[pltpu.VMEM((tm, 2 * D), _F32)],
        compiler_params=_cparams(("parallel", "parallel")),
        name="qkv_dil",
    )(h, w, gains, rope)


DIL_GROUP = 4


def _dil_kernel(q_ref, k_ref, v_ref, o_ref, tmp, qp, kp, vp, acc, mm, ll, accp, mmp, llp, bias_scr, *, tl):
    S = q_ref.shape[0]
    ng = S // DIL_GROUP
    pitch = DIL_GROUP + 1

    def spread(dst, src):
        for j in range(DIL_GROUP):
            dst[pl.ds(j, ng, stride=pitch), :] = src[pl.ds(j, ng, stride=DIL_GROUP), :]

    for src, dst in ((q_ref, qp), (k_ref, kp), (v_ref, vp)):
        tmp[...] = src[...].astype(_F32)
        spread(dst, tmp)
    lane = lax.broadcasted_iota(jnp.int32, (tl, LANES), 1)
    lo = lane < 64
    head_a = (lane & 63) < 32
    for bi, (window, d) in enumerate(DIL_PATTERNS):
        L = S // d
        half = window // (2 * d)
        t = min(tl, L)
        W = min(t + 2 * half, L)
        nt = L // t
        if d > 1:
            assert d % DIL_GROUP == 0
        sd = d * pitch // DIL_GROUP

        def rows(r, first, n, d=d, sd=sd):
            if d == 1:
                return pl.ds(first, n)
            return pl.ds(r + r // DIL_GROUP + sd * first, n, stride=sd)

        assert t % half == 0 and W <= t + 2 * half
        qa = lax.broadcasted_iota(jnp.int32, (t, W), 0)
        kb = lax.broadcasted_iota(jnp.int32, (t, W), 1)
        for case in range(3):
            ok = jnp.abs(kb - qa - case * half) <= half
            bias_scr[case, 0:t, 0:W] = jnp.where(ok, 0.0, NEG)

        def body(idx, carry, bi=bi, d=d, L=L, half=half, t=t, W=W, nt=nt, rows=rows):
            r = idx // nt
            l0 = (idx % nt) * t
            start = jnp.clip(l0 - half, 0, L - W)
            if d == 1:
                qt = q_ref[pl.ds(pl.multiple_of(l0, t), t), :]
                kw = k_ref[pl.ds(pl.multiple_of(start, 64), W), :]
                vw = v_ref[pl.ds(pl.multiple_of(start, 64), W), :]
            else:
                qt = qp[rows(r, l0, t), :].astype(_BF)
                kw = kp[rows(r, start, W), :].astype(_BF)
                vw = vp[rows(r, start, W), :].astype(_BF)
            lo_t = lo[:t]
            qa_t = head_a[:t]
            zq = jnp.zeros_like(qt)
            q2 = jnp.concatenate([jnp.where(qa_t, qt, zq), jnp.where(qa_t, zq, qt)], axis=0)
            s = lax.dot_general(q2, kw, (((1,), (1,)), ((), ())), preferred_element_type=_F32)
            bias = bias_scr[(l0 - start) // half, 0:t, 0:W]
            s = s + jnp.concatenate([bias, bias], axis=0)
            m = jnp.max(s, axis=-1, keepdims=True)
            p = jnp.exp2(s - m)
            den = jnp.sum(p, axis=-1, keepdims=True)
            o2 = jnp.dot(p.astype(_BF), vw, preferred_element_type=_F32)
            o_new = jnp.where(lo_t, o2[:t], o2[t:])
            m_new = jnp.where(lo_t, m[:t], m[t:])
            l_new = jnp.where(lo_t, den[:t], den[t:])
            tok = rows(r, l0, t)
            if d == 1:
                assert bi == 0
                acc[tok, :] = o_new
                mm[tok, :] = m_new
                ll[tok, :] = l_new
            else:
                m_old = mmp[tok, :]
                mx = jnp.maximum(m_old, m_new)
                a_old = jnp.exp2(m_old - mx)
                a_new = jnp.exp2(m_new - mx)
                accp[tok, :] = accp[tok, :] * a_old + o_new * a_new
                llp[tok, :] = llp[tok, :] * a_old + l_new * a_new
                mmp[tok, :] = mx
            return carry

        lax.fori_loop(0, d * nt, body, 0, unroll=min(8, d * nt))
        if bi == 0:
            spread(accp, acc)
            spread(mmp, mm)
            spread(llp, ll)
    for j in range(DIL_GROUP):
        grp = pl.ds(j, ng, stride=pitch)
        tmp[pl.ds(j, ng, stride=DIL_GROUP), :] = accp[grp, :] / llp[grp, :]
    o_ref[...] = tmp[...].astype(o_ref.dtype)


def _dilated(q, k, v, tl):
    B, P, S, _ = q.shape
    spec = pl.BlockSpec((None, None, S, LANES), lambda b, p: (b, p, 0, 0))
    scr = pltpu.VMEM((S, LANES), _F32)
    scrp = pltpu.VMEM((S // DIL_GROUP * (DIL_GROUP + 1), LANES), _F32)
    return pl.pallas_call(
        functools.partial(_dil_kernel, tl=tl),
        grid=(B, P),
        in_specs=[spec, spec, spec],
        out_specs=spec,
        out_shape=jax.ShapeDtypeStruct((B, P, S, LANES), _BF),
        scratch_shapes=[scr, scrp, scrp, scrp, scr, scr, scr, scrp, scrp, scrp,
                        pltpu.VMEM((3, tl, tl + max(w // d for w, d in DIL_PATTERNS)), _F32)],
        compiler_params=_cparams(("parallel", "parallel")),
        name="dil_attn",
    )(q, k, v)


def _moe_kernel(te_ref, tv_ref, src_hbm, dst_hbm, h_hbm, wg_ref, wu_ref, wd_ref, out_hbm,
                xbuf, xbf, a_scr, ybuf, src_s0, src_s1, dst_s, gsem, ssem, isem, *, tf, dump0, n_dump):
    i = pl.program_id(0)
    nt = pl.num_programs(0)
    tm = xbf.shape[0]
    slot = i % 2

    def is_valid(t):
        return jnp.logical_and(jnp.logical_and(t >= 0, t < nt), tv_ref[jnp.clip(t, 0, nt - 1)] == 1)

    valid = is_valid(i)
    prev_valid = is_valid(i - 1)
    nxt_valid = is_valid(i + 1)

    src_bufs = (src_s0, src_s1)

    def src_copy(tile, par):
        return pltpu.make_async_copy(src_hbm.at[tile], src_bufs[par], isem.at[par])

    def dst_copy(tile):
        return pltpu.make_async_copy(dst_hbm.at[tile], dst_s, isem.at[2])

    def tile_at(first):
        return pl.ds(pl.multiple_of(first, SUBLANES), SUBLANES)

    def gather_row(r, par):
        return pltpu.make_async_copy(h_hbm.at[tile_at(src_bufs[par][r])],
                                     xbuf.at[par, tile_at(r * SUBLANES)], gsem.at[par])

    def scatter_row(r):
        return pltpu.make_async_copy(ybuf.at[tile_at(r * SUBLANES)], out_hbm.at[tile_at(dst_s[r])],
                                     ssem.at[0])

    def gather_wait(slot_):
        return pltpu.make_async_copy(h_hbm.at[pl.ds(0, tm * SUBLANES)], xbuf.at[slot_], gsem.at[slot_])

    def scatter_wait():
        return pltpu.make_async_copy(ybuf, out_hbm.at[pl.ds(0, tm * SUBLANES)], ssem.at[0])

    def for_rows(fn):
        def body(r, c):
            fn(r)
            return c
        lax.fori_loop(0, tm, body, 0, unroll=8)

    @pl.when(i == 0)
    def _():
        ybuf[...] = jnp.zeros_like(ybuf)
        for k in range(n_dump):
            pltpu.make_async_copy(ybuf, out_hbm.at[pl.ds((dump0 + k * tm) * SUBLANES, tm * SUBLANES)],
                                  ssem.at[0]).start()
        for k in range(n_dump):
            scatter_wait().wait()

    @pl.when(jnp.logical_and(i == 0, valid))
    def _():
        src_copy(0, 0).start()
        src_copy(0, 0).wait()
        for_rows(lambda r: gather_row(r, 0).start())

    @pl.when(jnp.logical_and(i == 0, nxt_valid))
    def _():
        src_copy(1, 1).start()

    @pl.when(valid)
    def _():
        dst_copy(i).start()

    for par in range(2):
        @pl.when(jnp.logical_and(nxt_valid, slot == par))
        def _(par=par):
            src_copy(i + 1, 1 - par).wait()
            for_rows(lambda r: gather_row(r, 1 - par).start())

    for par in range(2):
        @pl.when(jnp.logical_and(valid, slot == par))
        def _(par=par):
            gather_wait(par).wait()
            xbf[...] = _tiles_to_rows(xbuf.at[par], tm).astype(_BF)

    @pl.when(valid)
    def _():
        _swiglu_hidden(xbf[...], wg_ref, wu_ref, a_scr, tf)

    @pl.when(prev_valid)
    def _():
        scatter_wait().wait()

    @pl.when(valid)
    def _():
        _rows_to_tiles(ybuf, jnp.dot(a_scr[...], wd_ref[...], preferred_element_type=_F32))
        dst_copy(i).wait()
        for_rows(lambda r: scatter_row(r).start())

    for par in range(2):
        @pl.when(jnp.logical_and(is_valid(i + 2), slot == par))
        def _(par=par):
            src_copy(i + 2, par).start()

    @pl.when(jnp.logical_and(i == nt - 1, valid))
    def _():
        scatter_wait().wait()


def _moe(h, tile_e, tile_v, src, dst, wg, wu, wd, n_out_rows, tm, tf):
    T = h.shape[0] // SUBLANES
    D = wg.shape[1]
    nt = tile_e.shape[0]
    F = wg.shape[2]
    dump0 = T * TOP_K
    once = pl.Buffered(1)
    grid_spec = pltpu.PrefetchScalarGridSpec(
        num_scalar_prefetch=2,
        grid=(nt,),
        in_specs=[pl.BlockSpec(memory_space=pl.ANY), pl.BlockSpec(memory_space=pl.ANY),
                  pl.BlockSpec(memory_space=pl.ANY),
                  pl.BlockSpec((None, D, F), lambda i, te, tv: (te[i], 0, 0), pipeline_mode=once),
                  pl.BlockSpec((None, D, F), lambda i, te, tv: (te[i], 0, 0), pipeline_mode=once),
                  pl.BlockSpec((None, F, D), lambda i, te, tv: (te[i], 0, 0), pipeline_mode=once)],
        out_specs=pl.BlockSpec(memory_space=pl.ANY),
        scratch_shapes=[pltpu.VMEM((2, tm * SUBLANES, LANES), _F32), pltpu.VMEM((tm, D), _BF),
                        pltpu.VMEM((tm, F), _BF), pltpu.VMEM((tm * SUBLANES, LANES), _F32),
                        pltpu.SMEM((tm,), jnp.int32), pltpu.SMEM((tm,), jnp.int32),
                        pltpu.SMEM((tm,), jnp.int32),
                        pltpu.SemaphoreType.DMA((2,)), pltpu.SemaphoreType.DMA((1,)),
                        pltpu.SemaphoreType.DMA((3,))],
    )
    return pl.pallas_call(
        functools.partial(_moe_kernel, tf=tf, dump0=dump0, n_dump=(n_out_rows - dump0) // tm),
        grid_spec=grid_spec,
        out_shape=jax.ShapeDtypeStruct((n_out_rows * SUBLANES, LANES), _F32),
        compiler_params=_cparams(("arbitrary",)),
        name="moe_experts",
    )(tile_e, tile_v, src, dst, h, wg, wu, wd)


def _route_plan(route, T, tm):
    A = T * TOP_K
    e_flat = route[:, :TOP_K].astype(jnp.int32).reshape(A)
    order = jnp.argsort(e_flat, stable=True).astype(jnp.int32)
    counts = jnp.sum((e_flat[:, None] == jnp.arange(N_EXPERTS, dtype=jnp.int32)[None, :]).astype(jnp.int32), axis=0)
    starts = jnp.cumsum(counts) - counts
    pcounts = (counts + tm - 1) // tm * tm
    pends = jnp.cumsum(pcounts)
    pstarts = pends - pcounts
    nt = A // tm + N_EXPERTS
    tile0 = jnp.arange(nt, dtype=jnp.int32) * tm
    tile_v = (tile0 < pends[-1]).astype(jnp.int32)
    last_valid = jnp.maximum(pends[-1] // tm - 1, 0)
    tile_e_raw = jnp.minimum(jnp.searchsorted(pends, tile0, side="right"), N_EXPERTS - 1).astype(jnp.int32)
    tile_e = jnp.where(tile_v == 1, tile_e_raw, tile_e_raw[last_valid])
    r = jnp.arange(nt * tm, dtype=jnp.int32)
    e_r = jnp.repeat(tile_e, tm)
    within = r - pstarts[e_r]
    ok = jnp.logical_and(within < counts[e_r], jnp.repeat(tile_v, tm) == 1)
    a = order[jnp.clip(starts[e_r] + within, 0, A - 1)]
    src = jnp.where(ok, a // TOP_K, 0).astype(jnp.int32)
    dump = A + e_r * tm + jnp.clip(within - counts[e_r], 0, tm - 1)
    dst = jnp.where(ok, (a % TOP_K) * T + a // TOP_K, dump).astype(jnp.int32)
    return (tile_e, tile_v, (src * SUBLANES).reshape(nt, tm), (dst * SUBLANES).reshape(nt, tm),
            A + N_EXPERTS * tm)


def _combine_kernel(x_ref, y1_ref, y2_ref, r_ref, mod_ref, o_ref):
    tm = x_ref.shape[0]
    y = _tiles_to_rows(y1_ref, tm) * r_ref[:, 2:3] + _tiles_to_rows(y2_ref, tm) * r_ref[:, 3:4]
    o_ref[...] = x_ref[...] + mod_ref[5:6, :] * y


def _combine(x, y2, route, mod, tm):
    B, S, D = x.shape
    T = B * S
    per_b = S // tm
    out = pl.pallas_call(
        _combine_kernel,
        grid=(T // tm,),
        in_specs=[pl.BlockSpec((tm, D), lambda i: (i, 0)),
                  pl.BlockSpec((tm * SUBLANES, LANES), lambda i: (i, 0)),
                  pl.BlockSpec((tm * SUBLANES, LANES), lambda i: (T // tm + i, 0)),
                  pl.BlockSpec((tm, LANES), lambda i: (i, 0)),
                  pl.BlockSpec((None, 6, D), lambda i: (i // per_b, 0, 0))],
        out_specs=pl.BlockSpec((tm, D), lambda i: (i, 0)),
        out_shape=jax.ShapeDtypeStruct((T, D), _F32),
        compiler_params=_cparams(("parallel",)),
        name="moe_combine",
    )(x.reshape(T, D), y2, y2, route.reshape(T, LANES), mod)
    return out.reshape(B, S, D)


def _pad_cols(a, w):
    return jnp.pad(a, ((0, 0), (0, w - a.shape[1])))


def _rope_tables_even(S):
    pos = jnp.arange(S, dtype=jnp.int32)
    inv = ROPE_THETA ** (-jnp.arange(0, 32, 2, dtype=_F32) / 32)
    def cs(p):
        ang = p.astype(_F32)[:, None] * inv[None, :]
        return jnp.cos(ang), jnp.sin(ang)
    one = lambda w: jnp.ones((S, w), _F32)
    zero = lambda w: jnp.zeros((S, w), _F32)
    c, s = cs(pos)
    ca = jnp.concatenate([c, one(48), c, one(48)], 1)
    sa = jnp.concatenate([-s, zero(48), s, zero(48)], 1)
    cr, sr = cs(pos // GRID_W)
    cc, sc = cs(pos % GRID_W)
    cb = jnp.concatenate([cr, cc, one(32), cr, cc, one(32)], 1)
    sb = jnp.concatenate([-sr, -sc, zero(32), sr, sc, zero(32)], 1)
    return jnp.concatenate([ca, sa, cb, sb], 1)


def _slot_maps():
    r = MLA_ROPE // 2
    mla = ([MLA_NOPE + i for i in range(r)] + list(range(0, 64 - r))
           + [MLA_NOPE + r + i for i in range(r)] + list(range(64 - r, MLA_NOPE)))
    mla += [-1] * (LANES - len(mla))
    q = HEAD_DIM // 4
    gqa = (list(range(0, q)) + list(range(2 * q, 3 * q)) + [-1] * (64 - 2 * q)
           + list(range(q, 2 * q)) + list(range(3 * q, 4 * q)) + [-1] * (64 - 2 * q))
    return mla, gqa


def _to_slots(a, lane_map):
    idx = jnp.asarray([max(i, 0) for i in lane_map], jnp.int32)
    keep = jnp.asarray([1.0 if i >= 0 else 0.0 for i in lane_map], a.dtype)
    return jnp.take(a, idx, axis=-1) * keep


def _rope_tables_odd(S):
    pos = jnp.arange(S, dtype=_F32)
    inv = ROPE_THETA ** (-jnp.arange(0, HEAD_DIM, 2, dtype=_F32) / HEAD_DIM)
    ang = pos[:, None] * inv[None, :]
    c, s = jnp.cos(ang), jnp.sin(ang)
    return jnp.concatenate([c, c, c, c, -s, -s, s, s], 1)


def _tiles(S):
    return dict(tm_pre=min(S, 512), tq=min(S, 2048), tm_o=min(S, 512),
                tm_ffn=min(S, 512), tf_ffn=512, tm_qkv=min(S, 512),
                tl=128, tm_moe=min(S, 512), tf_moe=512, tm_c=min(S, 512))


def kernel(x, c, ada_even_w, ada_even_b, norm_even_mix, norm_even_ffn, even_w_in, mla_q_norm, mla_w_uq, mla_kv_norm, mla_w_ukv, mla_q_gain, mla_k_gain, gqa_q_gain, gqa_k_gain, even_w_out, ffn_w_gate, ffn_w_up, ffn_w_down, ada_odd_w, ada_odd_b, norm_odd_mix, norm_odd_ffn, dil_w_qkv, dil_q_gain, dil_k_gain, dil_w_out, moe_router, moe_w_gate, moe_w_up, moe_w_down):
    B, S, D = x.shape
    T = B * S
    cfg = _tiles(S)

    mod_e = _ada_mod(c, ada_even_w[0], ada_even_b[0]).reshape(B, 6, D)
    mod_o = _ada_mod(c, ada_odd_w[0], ada_odd_b[0]).reshape(B, 6, D)

    w = even_w_in[0]
    sp = [MLA_Q_RANK, MLA_Q_RANK + MLA_KV_RANK, MLA_Q_RANK + MLA_KV_RANK + MLA_ROPE]
    sp.append(sp[-1] + GQA_HEADS * HEAD_DIM)
    sp.append(sp[-1] + GQA_KV_HEADS * HEAD_DIM)
    w_cq, w_ckv, w_kpe = w[:, :sp[0]], w[:, sp[0]:sp[1]], w[:, sp[1]:sp[2]]
    w_qb, w_kb, w_vb = w[:, sp[2]:sp[3]], w[:, sp[3]:sp[4]], w[:, sp[4]:]
    mla_map, gqa_map = _slot_maps()
    na = MLA_NOPE + MLA_ROPE
    nope_only = [i if 0 <= i < MLA_NOPE else -1 for i in mla_map]
    rope_only = [i - MLA_NOPE if i >= MLA_NOPE else -1 for i in mla_map]
    gslots = lambda a, n: _to_slots(a.reshape(D, n, HEAD_DIM), gqa_map).reshape(D, n * LANES)
    w_in = jnp.concatenate([w_cq, w_ckv, _to_slots(w_kpe, rope_only), gslots(w_qb, GQA_HEADS),
                            gslots(w_kb, GQA_KV_HEADS)], axis=1).astype(_BF)
    w_uq = _to_slots(mla_w_uq[0].reshape(MLA_Q_RANK, MLA_HEADS, na), mla_map)
    w_uq = w_uq.reshape(MLA_Q_RANK, MLA_HEADS * LANES).astype(_BF)
    w_ukv = mla_w_ukv[0].reshape(MLA_KV_RANK, MLA_HEADS, MLA_NOPE + MLA_V)
    w_uk = _to_slots(w_ukv[:, :, :MLA_NOPE], nope_only).reshape(MLA_KV_RANK, MLA_HEADS * LANES).astype(_BF)
    w_uvt = w_ukv[:, :, MLA_NOPE:].reshape(MLA_KV_RANK, MLA_HEADS * MLA_V).T.astype(_BF)
    w_vbt = w_vb.T.astype(_BF)
    g_rows = [_to_slots(mla_q_gain[0], mla_map) * (na ** -0.5 * LOG2E), _to_slots(mla_k_gain[0], mla_map),
              _to_slots(gqa_q_gain[0], gqa_map) * (HEAD_DIM ** -0.5 * LOG2E), _to_slots(gqa_k_gain[0], gqa_map)]
    bound_a = na * jnp.max(jnp.abs(g_rows[0])) * jnp.max(jnp.abs(g_rows[1]))
    bound_b = HEAD_DIM * jnp.max(jnp.abs(g_rows[2])) * jnp.max(jnp.abs(g_rows[3]))
    fast_a, fast_b = bound_a <= SCORE_BOUND_MAX, bound_b <= SCORE_BOUND_MAX
    last_lane = jnp.zeros((LANES,), _F32).at[LANES - 1].set(1.0)
    g_rows += [last_lane, last_lane * jnp.where(fast_a, -bound_a, 0.0),
               last_lane, last_lane * jnp.where(fast_b, -bound_b, 0.0)]
    gains_e = jnp.stack(g_rows)
    rope_e = _rope_tables_even(S)
    q_all, k_all, vt_all = _pre_even(
        x, mod_e, norm_even_mix[0].reshape(1, D), w_in, mla_q_norm[0].reshape(1, -1), w_uq,
        mla_kv_norm[0].reshape(1, -1), w_uk, w_uvt, w_vbt, gains_e, rope_e, cfg["tm_pre"])
    as_flag = lambda f: f.astype(jnp.int32).reshape(1)
    o_a = _attention(as_flag(fast_a), q_all, k_all, vt_all, q_off=0, k_off=0, shared_kv=False, tq=cfg["tq"])
    o_b = _attention(as_flag(fast_b), q_all, k_all, vt_all, q_off=MLA_HEADS, k_off=MLA_HEADS,
                     shared_kv=True, tq=cfg["tq"])
    x1, h1 = _oproj(x, mod_e, norm_even_ffn[0].reshape(1, D), even_w_out[0].astype(_BF), [o_a, o_b],
                    cfg["tm_o"], _BF)
    x2, h2 = _ffn(x1, h1, mod_e, mod_o, norm_odd_mix[0].reshape(1, D), ffn_w_gate[0].astype(_BF),
                  ffn_w_up[0].astype(_BF), ffn_w_down[0].astype(_BF), cfg["tm_ffn"], cfg["tf_ffn"])

    hh = HEAD_DIM // 2
    pair = lambda v: jnp.concatenate([v[:hh], v[:hh], v[hh:], v[hh:]])
    gains_o = jnp.stack([pair(dil_q_gain[0]) * (HEAD_DIM ** -0.5 * LOG2E), pair(dil_k_gain[0])])
    wq, wk, wv = jnp.split(dil_w_qkv[0], 3, axis=1)
    perm = lambda w: w.reshape(D, D // LANES, 2, 2, hh).transpose(0, 1, 3, 2, 4).reshape(D, D)
    w_qkv = jnp.concatenate([perm(wq), perm(wk), wv], axis=1).astype(_BF)
    qd, kd, vd = _qkv(h2, w_qkv, gains_o, _rope_tables_odd(S), cfg["tm_qkv"])
    o_d = _dilated(qd, kd, vd, cfg["tl"])
    r32 = _pad_cols(moe_router[0], LANES)
    r_hi = r32.astype(_BF)
    r_lo = (r32 - r_hi.astype(_F32)).astype(_BF)
    x3, h3, route = _oproj(x2, mod_o, norm_odd_ffn[0].reshape(1, D), dil_w_out[0].astype(_BF), [o_d],
                           cfg["tm_o"], _F32, router=jnp.stack([r_hi, r_lo]))
    tm = cfg["tm_moe"]
    tile_e, tile_v, src, dst, n_rows = _route_plan(route.reshape(T, LANES), T, tm)
    y2 = _moe(h3, tile_e, tile_v, src, dst, moe_w_gate[0].astype(_BF),
              moe_w_up[0].astype(_BF), moe_w_down[0].astype(_BF), n_rows, tm, cfg["tf_moe"])
    return _combine(x3, y2, route, mod_o, cfg["tm_c"])
```

```python
import functools
import math

import jax
import jax.numpy as jnp
from jax import lax
from jax.experimental import pallas as pl
from jax.experimental.pallas import tpu as pltpu

_BF = jnp.bfloat16
_F32 = jnp.float32

GRID_W = 64
HEAD_DIM = 64
ROPE_THETA = 10000.0
EPS = 1e-6
MLA_HEADS = 8
MLA_Q_RANK = 256
MLA_KV_RANK = 128
MLA_NOPE = 64
MLA_ROPE = 32
MLA_V = 64
GQA_HEADS = 8
GQA_KV_HEADS = 2
DIL_PATTERNS = ((128, 1), (512, 4), (2048, 16))
N_EXPERTS = 8
TOP_K = 2
NEG = -1e30
LOG2E = math.log2(math.e)

LANES = 128
VMEM_LIMIT = 56 * 1024 * 1024


def _cparams(sem, vmem=VMEM_LIMIT):
    return pltpu.CompilerParams(dimension_semantics=sem, vmem_limit_bytes=vmem)


def _silu(x):
    return x / (1.0 + jnp.exp(-x))


def _modulate(x, g, shift, scale):
    ms = jnp.mean(x * x, axis=-1, keepdims=True)
    return x * lax.rsqrt(ms + EPS) * g * (1.0 + scale) + shift


def _norm(v, n, g):
    return v * lax.rsqrt(jnp.sum(v * v, axis=-1, keepdims=True) * (1.0 / n) + EPS) * g


SUBLANES = 8


def _rows_to_tiles(ref, x):
    tm = x.shape[0]
    for c in range(SUBLANES):
        ref[pl.ds(c, tm, stride=SUBLANES), :] = x[:, c * LANES:(c + 1) * LANES]


def _tiles_to_rows(ref, tm):
    return jnp.concatenate([ref[pl.ds(c, tm, stride=SUBLANES), :] for c in range(SUBLANES)], axis=1)


def _rope(x, c, sa, sb, k):
    return x * c + pltpu.roll(x, LANES - k, 1) * sa + pltpu.roll(x, k, 1) * sb


def _mod_kernel(c_ref, w_ref, b_ref, o_ref):
    sc = _silu(c_ref[...])
    o_ref[...] = jnp.dot(sc.astype(_BF), w_ref[...].astype(_BF),
                         preferred_element_type=_F32) + b_ref[...]


def _ada_mod(c, w, b):
    B, D = c.shape
    N = w.shape[1]
    tn = min(N, 1536)
    return pl.pallas_call(
        _mod_kernel,
        grid=(N // tn,),
        in_specs=[pl.BlockSpec((B, D), lambda j: (0, 0)),
                  pl.BlockSpec((D, tn), lambda j: (0, j)),
                  pl.BlockSpec((1, tn), lambda j: (0, j))],
        out_specs=pl.BlockSpec((B, tn), lambda j: (0, j)),
        out_shape=jax.ShapeDtypeStruct((B, N), _F32),
        compiler_params=_cparams(("arbitrary",)),
        name="ada_mod",
    )(c, w, b.reshape(1, N))


def _pre_even_kernel(x_ref, mod_ref, g_ref, w_in_ref, qn_ref, w_uq_ref, kvn_ref, w_uk_ref, w_uvt_ref,
                     w_vbt_ref, gains_ref, rope_ref, q_ref, k_ref, vt_ref, nq_scr, nk_scr):
    h = _modulate(x_ref[...], g_ref[...], mod_ref[0:1, :], mod_ref[1:2, :]).astype(_BF)
    y = jnp.dot(h, w_in_ref[...], preferred_element_type=_F32)
    cqn = _norm(y[:, 0:256], MLA_Q_RANK, qn_ref[...]).astype(_BF)
    qa = jnp.dot(cqn, w_uq_ref[...], preferred_element_type=_F32)
    ckvn = _norm(y[:, 256:384], MLA_KV_RANK, kvn_ref[...]).astype(_BF)
    kn = jnp.dot(ckvn, w_uk_ref[...], preferred_element_type=_F32)
    kpe = y[:, 384:512]
    nt_dims = (((1,), (1,)), ((), ()))
    vt_a = lax.dot_general(w_uvt_ref[...], ckvn, nt_dims, preferred_element_type=_F32).astype(_BF)
    vt_b = lax.dot_general(w_vbt_ref[...], h, nt_dims, preferred_element_type=_F32).astype(_BF)
    tm = x_ref.shape[0]
    ones_blk = jnp.where(lax.broadcasted_iota(jnp.int32, (LANES - MLA_V, tm), 0) == 0, 1.0, 0.0).astype(_BF)
    for hh in range(MLA_HEADS + GQA_KV_HEADS):
        src = vt_a[hh * MLA_V:(hh + 1) * MLA_V] if hh < MLA_HEADS else \
            vt_b[(hh - MLA_HEADS) * HEAD_DIM:(hh - MLA_HEADS + 1) * HEAD_DIM]
        vt_ref[hh * LANES:hh * LANES + MLA_V, :] = src
        vt_ref[hh * LANES + MLA_V:(hh + 1) * LANES, :] = ones_blk
    ca, sa = rope_ref[:, 0:128], rope_ref[:, 128:256]
    cb, sb = rope_ref[:, 256:384], rope_ref[:, 384:512]
    gqa, gka = gains_ref[0:1, :], gains_ref[1:2, :]
    gqb, gkb = gains_ref[2:3, :], gains_ref[3:4, :]
    na = MLA_NOPE + MLA_ROPE

    for hh in range(MLA_HEADS):
        sl = slice(hh * LANES, (hh + 1) * LANES)
        nq_scr[:, sl] = _norm(qa[:, sl], na, gqa)
        nk_scr[:, sl] = _norm(kn[:, sl] + kpe, na, gka)
    for hh in range(GQA_HEADS):
        src = slice(512 + hh * LANES, 512 + (hh + 1) * LANES)
        dst = slice((MLA_HEADS + hh) * LANES, (MLA_HEADS + hh + 1) * LANES)
        nq_scr[:, dst] = _norm(y[:, src], HEAD_DIM, gqb)
    for g in range(GQA_KV_HEADS):
        src = slice(1536 + g * LANES, 1536 + (g + 1) * LANES)
        dst = slice((MLA_HEADS + g) * LANES, (MLA_HEADS + g + 1) * LANES)
        nk_scr[:, dst] = _norm(y[:, src], HEAD_DIM, gkb)

    @pl.when(pl.program_id(1) >= 0)
    def _():
        def rope(ref, out, n_a, n_all, row):
            for hh in range(n_all):
                sl = slice(hh * LANES, (hh + 1) * LANES)
                c, s = (ca, sa) if hh < n_a else (cb, sb)
                b = gains_ref[row:row + 1, :] if hh < n_a else gains_ref[row + 2:row + 3, :]
                v = ref[:, sl]
                out[:, sl] = (v * c + pltpu.roll(v, 64, 1) * s + b).astype(_BF)

        rope(nq_scr, q_ref, MLA_HEADS, MLA_HEADS + GQA_HEADS, 4)
        rope(nk_scr, k_ref, MLA_HEADS, MLA_HEADS + GQA_KV_HEADS, 5)


def _pre_even(x, mod, g, w_in, qn, w_uq, kvn, w_uk, w_uvt, w_vbt, gains, rope, tm):
    B, S, D = x.shape
    nq = (MLA_HEADS + GQA_HEADS) * LANES
    nk = (MLA_HEADS + GQA_KV_HEADS) * LANES
    nvt = (MLA_HEADS + GQA_KV_HEADS) * LANES
    full = lambda a: pl.BlockSpec(a.shape, lambda b, i: (0,) * a.ndim)
    tok = lambda w: pl.BlockSpec((None, tm, w), lambda b, i: (b, i, 0))
    return pl.pallas_call(
        _pre_even_kernel,
        grid=(B, S // tm),
        in_specs=[tok(D),
                  pl.BlockSpec((None, 6, D), lambda b, i: (b, 0, 0)),
                  full(g), full(w_in), full(qn), full(w_uq), full(kvn), full(w_uk), full(w_uvt),
                  full(w_vbt), full(gains),
                  pl.BlockSpec((tm, rope.shape[1]), lambda b, i: (i, 0))],
        out_specs=[tok(nq), tok(nk), pl.BlockSpec((None, None, nvt, tm), lambda b, i: (b, i, 0, 0))],
        out_shape=[jax.ShapeDtypeStruct((B, S, nq), _BF), jax.ShapeDtypeStruct((B, S, nk), _BF),
                   jax.ShapeDtypeStruct((B, S // tm, nvt, tm), _BF)],
        scratch_shapes=[pltpu.VMEM((tm, nq), _F32), pltpu.VMEM((tm, nk), _F32)],
        compiler_params=_cparams(("parallel", "parallel")),
        name="pre_even",
    )(x, mod, g, w_in, qn, w_uq, kvn, w_uk, w_uvt, w_vbt, gains, rope)


SCORE_BOUND_MAX = 60.0


def _attn_kernel(fast_ref, q_ref, k_ref, vt_ref, o_ref, s_scr, *, shared_kv):
    @pl.when(fast_ref[0] == 1)
    def _():
        _attn_fixed_shift(q_ref, k_ref, vt_ref, o_ref, shared_kv=shared_kv)

    @pl.when(fast_ref[0] != 1)
    def _():
        _attn_running_max(q_ref, k_ref, vt_ref, o_ref, s_scr, shared_kv=shared_kv)


def _attn_fixed_shift(q_ref, k_ref, vt_ref, o_ref, *, shared_kv):
    tq = q_ref.shape[0]
    nk, _, tk = vt_ref.shape
    qs = [q_ref[:, hh * LANES:(hh + 1) * LANES] for hh in range(2)]

    def body(j, carry):
        r0 = pl.multiple_of(j * tk, tk)
        out = []
        for hh in range(2):
            kc = 0 if shared_kv else hh * LANES
            k = k_ref[pl.ds(r0, tk), kc:kc + LANES]
            st = lax.dot_general(k, qs[hh], (((1,), (1,)), ((), ())), preferred_element_type=_F32)
            p = jnp.exp2(st).astype(_BF)
            out.append(carry[hh] + jnp.dot(vt_ref[j, kc:kc + LANES, :], p, preferred_element_type=_F32))
        return tuple(out)

    z = jnp.zeros((LANES, tq), _F32)
    res = lax.fori_loop(0, nk, body, (z, z), unroll=2)
    nv = HEAD_DIM
    halves = [acc[0:nv, :] / acc[nv:nv + 1, :] for acc in res]
    o_ref[...] = jnp.concatenate(halves, axis=0).T.astype(o_ref.dtype)


def _attn_running_max(q_ref, k_ref, vt_ref, o_ref, s_scr, *, shared_kv):
    tq = q_ref.shape[0]
    nk, _, tk = vt_ref.shape
    qs = [q_ref[:, hh * LANES:(hh + 1) * LANES] for hh in range(2)]

    def scores(j, hh):
        r0 = j * tk if isinstance(j, int) else pl.multiple_of(j * tk, tk)
        kc = 0 if shared_kv else hh * LANES
        k = k_ref[pl.ds(r0, tk), kc:kc + LANES]
        return lax.dot_general(k, qs[hh], (((1,), (1,)), ((), ())), preferred_element_type=_F32)

    def put_scores(j, hh, slot):
        st = scores(j, hh)
        s_scr[hh, slot] = st
        return jnp.max(st, axis=0, keepdims=True)

    def step(j, hh, carry, cur, last=False):
        m, mc, acc = carry
        mc_next = mc if last else put_scores(j + 1, hh, 1 - cur)
        st = s_scr[hh, cur]
        m_new = jnp.maximum(m, mc)
        alpha = jnp.exp2(m - m_new)
        p = jnp.exp2(st - m_new).astype(_BF)
        vc = 0 if shared_kv else hh * LANES
        acc = alpha * acc + jnp.dot(vt_ref[j, vc:vc + LANES, :], p, preferred_element_type=_F32)
        return m_new, mc_next, acc

    def body(jj, carry):
        c = list(carry)
        for sub in range(2):
            for hh in range(2):
                c[hh] = step(2 * jj + sub, hh, c[hh], sub)
        return tuple(c)

    init = tuple((jnp.full((1, tq), NEG, _F32), put_scores(0, hh, 0), jnp.zeros((LANES, tq), _F32))
                 for hh in range(2))
    res = lax.fori_loop(0, nk // 2 - 1, body, init)
    res = list(res)
    for sub in range(2):
        for hh in range(2):
            res[hh] = step(nk - 2 + sub, hh, res[hh], sub, last=(sub == 1))
    nv = HEAD_DIM
    halves = [acc[0:nv, :] / acc[nv:nv + 1, :] for (_, _, acc) in res]
    o_ref[...] = jnp.concatenate(halves, axis=0).T.astype(o_ref.dtype)


def _attention(fast, q, k, vt, *, q_off, k_off, shared_kv, tq):
    B, S, _ = q.shape
    nk, _, tk = vt.shape[1:]
    assert nk % 2 == 0 and nk >= 2
    npairs = 4
    if shared_kv:
        kspec = pl.BlockSpec((None, S, LANES), lambda b, p, i, f: (b, 0, k_off + p // 2))
        vspec = pl.BlockSpec((None, nk, LANES, tk), lambda b, p, i, f: (b, 0, k_off + p // 2, 0))
    else:
        kspec = pl.BlockSpec((None, S, 2 * LANES), lambda b, p, i, f: (b, 0, k_off // 2 + p))
        vspec = pl.BlockSpec((None, nk, 2 * LANES, tk), lambda b, p, i, f: (b, 0, k_off // 2 + p, 0))
    grid_spec = pltpu.PrefetchScalarGridSpec(
        num_scalar_prefetch=1,
        grid=(B, npairs, S // tq),
        in_specs=[pl.BlockSpec((None, tq, 2 * LANES), lambda b, p, i, f: (b, i, q_off // 2 + p)),
                  kspec, vspec],
        out_specs=pl.BlockSpec((None, tq, LANES), lambda b, p, i, f: (b, i, p)),
        scratch_shapes=[pltpu.VMEM((2, 2, tk, tq), _F32)],
    )
    return pl.pallas_call(
        functools.partial(_attn_kernel, shared_kv=shared_kv),
        grid_spec=grid_spec,
        out_shape=jax.ShapeDtypeStruct((B, S, npairs * LANES), _BF),
        compiler_params=_cparams(("parallel", "parallel", "parallel")),
        name="attn_gqa" if shared_kv else "attn_mla",
    )(fast, q, k, vt)


def _oproj_kernel(*refs, n_in, route):
    x_ref, mod_ref, g_ref, w_ref = refs[0], refs[1], refs[2], refs[3]
    o_refs = refs[4:4 + n_in]
    pos = 4 + n_in
    if route:
        r_ref = refs[pos]
        pos += 1
    x_out, h_out = refs[pos], refs[pos + 1]
    y = None
    off = 0
    for o_ref in o_refs:
        if len(o_ref.shape) == 3:
            o = jnp.concatenate([o_ref[p] for p in range(o_ref.shape[0])], axis=1)
        else:
            o = o_ref[...]
        w = o.shape[1]
        t = jnp.dot(o, w_ref[off:off + w, :], preferred_element_type=_F32)
        y = t if y is None else y + t
        off += w
    x1 = x_ref[...] + mod_ref[2:3, :] * y
    x_out[...] = x1
    h = _modulate(x1, g_ref[...], mod_ref[3:4, :], mod_ref[4:5, :])
    if route:
        _rows_to_tiles(h_out, h)
    else:
        h_out[...] = h.astype(h_out.dtype)
    if route:
        route_out = refs[pos + 2]
        hh = h.astype(_BF)
        hl = (h - hh.astype(_F32)).astype(_BF)
        rh, rl = r_ref[0], r_ref[1]
        logits = (jnp.dot(hh, rh, preferred_element_type=_F32)
                  + jnp.dot(hh, rl, preferred_element_type=_F32)
                  + jnp.dot(hl, rh, preferred_element_type=_F32))
        tm = logits.shape[0]
        lane = lax.broadcasted_iota(jnp.int32, (tm, LANES), 1)
        lg = jnp.where(lane < N_EXPERTS, logits, NEG)
        m1 = jnp.max(lg, axis=-1, keepdims=True)
        lanef = lane.astype(_F32)
        i1 = jnp.min(jnp.where(lg == m1, lanef, float(LANES)), axis=-1, keepdims=True)
        lg2 = jnp.where(lanef == i1, NEG, lg)
        m2 = jnp.max(lg2, axis=-1, keepdims=True)
        i2 = jnp.min(jnp.where(lg2 == m2, lanef, float(LANES)), axis=-1, keepdims=True)
        e = jnp.exp(m2 - m1)
        g1 = 1.0 / (1.0 + e)
        g2 = e / (1.0 + e)
        route_out[...] = jnp.where(lane == 0, i1,
                                   jnp.where(lane == 1, i2,
                                             jnp.where(lane == 2, g1, jnp.where(lane == 3, g2, 0.0))))


def _oproj(x, mod, g, w, o_list, tm, h_dtype, router=None):
    B, S, D = x.shape
    n_in = len(o_list)
    route = router is not None
    tok = lambda wd: pl.BlockSpec((None, tm, wd), lambda b, i: (b, i, 0))
    in_specs = [tok(D), pl.BlockSpec((None, 6, D), lambda b, i: (b, 0, 0)),
                pl.BlockSpec((1, D), lambda b, i: (0, 0)),
                pl.BlockSpec(w.shape, lambda b, i: (0, 0))]
    for o in o_list:
        if o.ndim == 4:
            in_specs.append(pl.BlockSpec((None, o.shape[1], tm, LANES), lambda b, i: (b, 0, i, 0)))
        else:
            in_specs.append(tok(o.shape[2]))
    args = [x, mod, g, w] + list(o_list)
    out_specs = [tok(D), tok(D)]
    out_shape = [jax.ShapeDtypeStruct((B, S, D), _F32), jax.ShapeDtypeStruct((B, S, D), h_dtype)]
    if route:
        assert D == SUBLANES * LANES
        per_b = S // tm
        out_specs[1] = pl.BlockSpec((tm * SUBLANES, LANES), lambda b, i: (b * per_b + i, 0))
        out_shape[1] = jax.ShapeDtypeStruct((B * S * SUBLANES, LANES), _F32)
        in_specs.append(pl.BlockSpec(router.shape, lambda b, i: (0, 0, 0)))
        args.append(router)
        out_specs.append(tok(LANES))
        out_shape.append(jax.ShapeDtypeStruct((B, S, LANES), _F32))
    return pl.pallas_call(
        functools.partial(_oproj_kernel, n_in=n_in, route=route),
        grid=(B, S // tm),
        in_specs=in_specs, out_specs=out_specs, out_shape=out_shape,
        compiler_params=_cparams(("parallel", "parallel")),
        name="oproj_route" if route else "oproj",
    )(*args)


def _swiglu_hidden(x, wg_ref, wu_ref, a_scr, tf):
    F = wg_ref.shape[1]
    for c0 in range(0, F, tf):
        c1 = min(c0 + tf, F)
        g = jnp.dot(x, wg_ref[:, c0:c1], preferred_element_type=_F32)
        u = jnp.dot(x, wu_ref[:, c0:c1], preferred_element_type=_F32)
        a_scr[:, c0:c1] = (_silu(g) * u).astype(_BF)


def _ffn_kernel(x_ref, h_ref, mod_ref, modn_ref, gn_ref, wg_ref, wu_ref, wd_ref, x_out, h_out, a_scr, *, tf):
    _swiglu_hidden(h_ref[...], wg_ref, wu_ref, a_scr, tf)
    y = jnp.dot(a_scr[...], wd_ref[...], preferred_element_type=_F32)
    x2 = x_ref[...] + mod_ref[5:6, :] * y
    x_out[...] = x2
    h_out[...] = _modulate(x2, gn_ref[...], modn_ref[0:1, :], modn_ref[1:2, :]).astype(h_out.dtype)


def _ffn(x, h, mod, modn, gn, wg, wu, wd, tm, tf):
    B, S, D = x.shape
    T = B * S
    F = wg.shape[1]
    per_b = S // tm
    x2 = x.reshape(T, D)
    h2 = h.reshape(T, D)
    tok = pl.BlockSpec((tm, D), lambda i: (i, 0))
    modspec = pl.BlockSpec((None, 6, D), lambda i: (i // per_b, 0, 0))
    once = pl.Buffered(1)
    xo, ho = pl.pallas_call(
        functools.partial(_ffn_kernel, tf=tf),
        grid=(T // tm,),
        in_specs=[tok, tok, modspec, modspec, pl.BlockSpec((1, D), lambda i: (0, 0)),
                  pl.BlockSpec((D, F), lambda i: (0, 0), pipeline_mode=once),
                  pl.BlockSpec((D, F), lambda i: (0, 0), pipeline_mode=once),
                  pl.BlockSpec((F, D), lambda i: (0, 0), pipeline_mode=once)],
        out_specs=[tok, tok],
        out_shape=[jax.ShapeDtypeStruct((T, D), _F32), jax.ShapeDtypeStruct((T, D), _BF)],
        scratch_shapes=[pltpu.VMEM((tm, F), _BF)],
        compiler_params=_cparams(("parallel",)),
        name="ffn_dense",
    )(x2, h2, mod, modn, gn, wg, wu, wd)
    return xo.reshape(B, S, D), ho.reshape(B, S, D)


def _qkv_kernel(h_ref, w_ref, gains_ref, rope_ref, q_ref, k_ref, v_ref, n_scr):
    tm, D = h_ref.shape
    h = h_ref[...]
    lane = lax.broadcasted_iota(jnp.int32, (tm, LANES), 1)
    head_a = (lane & 63) < 32

    def head_pair_norm(v, g):
        sq = v * v
        s_a = jnp.sum(jnp.where(head_a, sq, 0.0), axis=-1, keepdims=True)
        s_b = jnp.sum(jnp.where(head_a, 0.0, sq), axis=-1, keepdims=True)
        r = lax.rsqrt(jnp.where(head_a, s_a, s_b) * (1.0 / HEAD_DIM) + EPS)
        return v * r * g

    y = jnp.dot(h, w_ref[...], preferred_element_type=_F32)
    for p in range(D // LANES):
        sl = slice(p * LANES, (p + 1) * LANES)
        n_scr[:, sl] = head_pair_norm(y[:, sl], gains_ref[0:1, :])
        n_scr[:, D + p * LANES:D + (p + 1) * LANES] = head_pair_norm(
            y[:, D + p * LANES:D + (p + 1) * LANES], gains_ref[1:2, :])
        v_ref[p] = y[:, 2 * D + p * LANES:2 * D + (p + 1) * LANES].astype(_BF)

    @pl.when(pl.program_id(1) >= 0)
    def _():
        c, s = rope_ref[:, 0:128], rope_ref[:, 128:256]
        for p in range(D // LANES):
            vq = n_scr[:, p * LANES:(p + 1) * LANES]
            vk = n_scr[:, D + p * LANES:D + (p + 1) * LANES]
            q_ref[p] = (vq * c + pltpu.roll(vq, 64, 1) * s).astype(_BF)
            k_ref[p] = (vk * c + pltpu.roll(vk, 64, 1) * s).astype(_BF)


def _qkv(h, w, gains, rope, tm):
    B, S, D = h.shape
    tok = pl.BlockSpec((None, tm, D), lambda b, i: (b, i, 0))
    hp = pl.BlockSpec((None, D // LANES, tm, LANES), lambda b, i: (b, 0, i, 0))
    sds = jax.ShapeDtypeStruct((B, D // LANES, S, LANES), _BF)
    return pl.pallas_call(
        _qkv_kernel,
        grid=(B, S // tm),
        in_specs=[tok, pl.BlockSpec(w.shape, lambda b, i: (0, 0)),
                  pl.BlockSpec((2, LANES), lambda b, i: (0, 0)),
                  pl.BlockSpec((tm, 2 * LANES), lambda b, i: (i, 0))],
        out_specs=[hp, hp, hp],
        out_shape=[sds, sds, sds],
        scratch_shapes=[pltpu.VMEM((tm, 2 * D), _F32)],
        compiler_params=_cparams(("parallel", "parallel")),
        name="qkv_dil",
    )(h, w, gains, rope)


DIL_GROUP = 4


def _dil_kernel(fast_ref, q_ref, k_ref, v_ref, shift_ref, o_ref, tmp, qp, kp, vp, acc, mm, ll, accp, mmp,
                llp, bias_scr, *, tl):
    @pl.when(fast_ref[0] == 1)
    def _():
        _dil_body(q_ref, k_ref, v_ref, shift_ref, o_ref, tmp, qp, kp, vp, acc, mm, ll, accp, mmp, llp,
                  bias_scr, tl=tl, fixed_shift=True)

    @pl.when(fast_ref[0] != 1)
    def _():
        _dil_body(q_ref, k_ref, v_ref, shift_ref, o_ref, tmp, qp, kp, vp, acc, mm, ll, accp, mmp, llp,
                  bias_scr, tl=tl, fixed_shift=False)


def _dil_body(q_ref, k_ref, v_ref, shift_ref, o_ref, tmp, qp, kp, vp, acc, mm, ll, accp, mmp, llp,
              bias_scr, *, tl, fixed_shift):
    S = q_ref.shape[0]
    ng = S // DIL_GROUP
    pitch = DIL_GROUP + 1

    def spread(dst, src):
        for j in range(DIL_GROUP):
            dst[pl.ds(j, ng, stride=pitch), :] = src[pl.ds(j, ng, stride=DIL_GROUP), :]

    for src, dst in ((q_ref, qp), (k_ref, kp), (v_ref, vp)):
        tmp[...] = src[...].astype(_F32)
        spread(dst, tmp)
    lane = lax.broadcasted_iota(jnp.int32, (tl, LANES), 1)
    lo = lane < 64
    head_a = (lane & 63) < 32
    for bi, (window, d) in enumerate(DIL_PATTERNS):
        L = S // d
        half = window // (2 * d)
        t = min(tl, L)
        W = min(t + 2 * half, L)
        nt = L // t
        if d > 1:
            assert d % DIL_GROUP == 0
        sd = d * pitch // DIL_GROUP

        def rows(r, first, n, d=d, sd=sd):
            if d == 1:
                return pl.ds(first, n)
            return pl.ds(r + r // DIL_GROUP + sd * first, n, stride=sd)

        assert t % half == 0 and W <= t + 2 * half
        qa = lax.broadcasted_iota(jnp.int32, (t, W), 0)
        kb = lax.broadcasted_iota(jnp.int32, (t, W), 1)
        inside = shift_ref[0:1, 0:W] if fixed_shift else 0.0
        for case in range(3):
            ok = jnp.abs(kb - qa - case * half) <= half
            bias_scr[case, 0:t, 0:W] = jnp.where(ok, inside, NEG)

        def body(idx, carry, bi=bi, d=d, L=L, half=half, t=t, W=W, nt=nt, rows=rows):
            r = idx // nt
            l0 = (idx % nt) * t
            start = jnp.clip(l0 - half, 0, L - W)
            if d == 1:
                qt = q_ref[pl.ds(pl.multiple_of(l0, t), t), :]
                kw = k_ref[pl.ds(pl.multiple_of(start, 64), W), :]
                vw = v_ref[pl.ds(pl.multiple_of(start, 64), W), :]
            else:
                qt = qp[rows(r, l0, t), :].astype(_BF)
                kw = kp[rows(r, start, W), :].astype(_BF)
                vw = vp[rows(r, start, W), :].astype(_BF)
            lo_t = lo[:t]
            qa_t = head_a[:t]
            zq = jnp.zeros_like(qt)
            q2 = jnp.concatenate([jnp.where(qa_t, qt, zq), jnp.where(qa_t, zq, qt)], axis=0)
            s = lax.dot_general(q2, kw, (((1,), (1,)), ((), ())), preferred_element_type=_F32)
            bias = bias_scr[(l0 - start) // half, 0:t, 0:W]
            s = s + jnp.concatenate([bias, bias], axis=0)
            if fixed_shift:
                p = jnp.exp2(s)
            else:
                m = jnp.max(s, axis=-1, keepdims=True)
                p = jnp.exp2(s - m)
                m_new = jnp.where(lo_t, m[:t], m[t:])
            den = jnp.sum(p, axis=-1, keepdims=True)
            o2 = jnp.dot(p.astype(_BF), vw, preferred_element_type=_F32)
            o_new = jnp.where(lo_t, o2[:t], o2[t:])
            l_new = jnp.where(lo_t, den[:t], den[t:])
            tok = rows(r, l0, t)
            if d == 1:
                assert bi == 0
                acc[tok, :] = o_new
                ll[tok, :] = l_new
                if not fixed_shift:
                    mm[tok, :] = m_new
            elif fixed_shift:
                accp[tok, :] = accp[tok, :] + o_new
                llp[tok, :] = llp[tok, :] + l_new
            else:
                m_old = mmp[tok, :]
                mx = jnp.maximum(m_old, m_new)
                a_old = jnp.exp2(m_old - mx)
                a_new = jnp.exp2(m_new - mx)
                accp[tok, :] = accp[tok, :] * a_old + o_new * a_new
                llp[tok, :] = llp[tok, :] * a_old + l_new * a_new
                mmp[tok, :] = mx
            return carry

        lax.fori_loop(0, d * nt, body, 0, unroll=min(8, d * nt))
        if bi == 0:
            spread(accp, acc)
            spread(llp, ll)
            if not fixed_shift:
                spread(mmp, mm)
    for j in range(DIL_GROUP):
        grp = pl.ds(j, ng, stride=pitch)
        tmp[pl.ds(j, ng, stride=DIL_GROUP), :] = accp[grp, :] / llp[grp, :]
    o_ref[...] = tmp[...].astype(o_ref.dtype)


def _dilated(fast, shift, q, k, v, tl):
    B, P, S, _ = q.shape
    wmax = tl + max(w // d for w, d in DIL_PATTERNS)
    spec = pl.BlockSpec((None, None, S, LANES), lambda b, p, f: (b, p, 0, 0))
    scr = pltpu.VMEM((S, LANES), _F32)
    scrp = pltpu.VMEM((S // DIL_GROUP * (DIL_GROUP + 1), LANES), _F32)
    grid_spec = pltpu.PrefetchScalarGridSpec(
        num_scalar_prefetch=1,
        grid=(B, P),
        in_specs=[spec, spec, spec, pl.BlockSpec((1, wmax), lambda b, p, f: (0, 0))],
        out_specs=spec,
        scratch_shapes=[scr, scrp, scrp, scrp, scr, scr, scr, scrp, scrp, scrp,
                        pltpu.VMEM((3, tl, wmax), _F32)],
    )
    return pl.pallas_call(
        functools.partial(_dil_kernel, tl=tl),
        grid_spec=grid_spec,
        out_shape=jax.ShapeDtypeStruct((B, P, S, LANES), _BF),
        compiler_params=_cparams(("parallel", "parallel")),
        name="dil_attn",
    )(fast, q, k, v, jnp.full((1, wmax), shift, _F32))


def _moe_kernel(te_ref, tv_ref, src_hbm, dst_hbm, h_hbm, wg_ref, wu_ref, wd_ref, out_hbm,
                xbuf, xbf, a_scr, ybuf, src_s0, src_s1, dst_s, gsem, ssem, isem, *, tf, dump0, n_dump):
    i = pl.program_id(0)
    nt = pl.num_programs(0)
    tm = xbf.shape[0]
    slot = i % 2

    def is_valid(t):
        return jnp.logical_and(jnp.logical_and(t >= 0, t < nt), tv_ref[jnp.clip(t, 0, nt - 1)] == 1)

    valid = is_valid(i)
    prev_valid = is_valid(i - 1)
    nxt_valid = is_valid(i + 1)

    src_bufs = (src_s0, src_s1)

    def src_copy(tile, par):
        return pltpu.make_async_copy(src_hbm.at[tile], src_bufs[par], isem.at[par])

    def dst_copy(tile):
        return pltpu.make_async_copy(dst_hbm.at[tile], dst_s, isem.at[2])

    def tile_at(first):
        return pl.ds(pl.multiple_of(first, SUBLANES), SUBLANES)

    def gather_row(r, par):
        return pltpu.make_async_copy(h_hbm.at[tile_at(src_bufs[par][r])],
                                     xbuf.at[par, tile_at(r * SUBLANES)], gsem.at[par])

    def scatter_row(r):
        return pltpu.make_async_copy(ybuf.at[tile_at(r * SUBLANES)], out_hbm.at[tile_at(dst_s[r])],
                                     ssem.at[0])

    def gather_wait(slot_):
        return pltpu.make_async_copy(h_hbm.at[pl.ds(0, tm * SUBLANES)], xbuf.at[slot_], gsem.at[slot_])

    def scatter_wait():
        return pltpu.make_async_copy(ybuf, out_hbm.at[pl.ds(0, tm * SUBLANES)], ssem.at[0])

    def for_rows(fn):
        def body(r, c):
            fn(r)
            return c
        lax.fori_loop(0, tm, body, 0, unroll=8)

    @pl.when(i == 0)
    def _():
        ybuf[...] = jnp.zeros_like(ybuf)
        for k in range(n_dump):
            pltpu.make_async_copy(ybuf, out_hbm.at[pl.ds((dump0 + k * tm) * SUBLANES, tm * SUBLANES)],
                                  ssem.at[0]).start()
        for k in range(n_dump):
            scatter_wait().wait()

    @pl.when(jnp.logical_and(i == 0, valid))
    def _():
        src_copy(0, 0).start()
        src_copy(0, 0).wait()
        for_rows(lambda r: gather_row(r, 0).start())

    @pl.when(jnp.logical_and(i == 0, nxt_valid))
    def _():
        src_copy(1, 1).start()

    @pl.when(valid)
    def _():
        dst_copy(i).start()

    for par in range(2):
        @pl.when(jnp.logical_and(nxt_valid, slot == par))
        def _(par=par):
            src_copy(i + 1, 1 - par).wait()
            for_rows(lambda r: gather_row(r, 1 - par).start())

    for par in range(2):
        @pl.when(jnp.logical_and(valid, slot == par))
        def _(par=par):
            gather_wait(par).wait()
            xbf[...] = _tiles_to_rows(xbuf.at[par], tm).astype(_BF)

    @pl.when(valid)
    def _():
        _swiglu_hidden(xbf[...], wg_ref, wu_ref, a_scr, tf)

    @pl.when(prev_valid)
    def _():
        scatter_wait().wait()

    @pl.when(valid)
    def _():
        _rows_to_tiles(ybuf, jnp.dot(a_scr[...], wd_ref[...], preferred_element_type=_F32))
        dst_copy(i).wait()
        for_rows(lambda r: scatter_row(r).start())

    for par in range(2):
        @pl.when(jnp.logical_and(is_valid(i + 2), slot == par))
        def _(par=par):
            src_copy(i + 2, par).start()

    @pl.when(jnp.logical_and(i == nt - 1, valid))
    def _():
        scatter_wait().wait()


def _moe(h, tile_e, tile_v, src, dst, wg, wu, wd, n_out_rows, tm, tf):
    T = h.shape[0] // SUBLANES
    D = wg.shape[1]
    nt = tile_e.shape[0]
    F = wg.shape[2]
    dump0 = T * TOP_K
    once = pl.Buffered(1)
    grid_spec = pltpu.PrefetchScalarGridSpec(
        num_scalar_prefetch=2,
        grid=(nt,),
        in_specs=[pl.BlockSpec(memory_space=pl.ANY), pl.BlockSpec(memory_space=pl.ANY),
                  pl.BlockSpec(memory_space=pl.ANY),
                  pl.BlockSpec((None, D, F), lambda i, te, tv: (te[i], 0, 0), pipeline_mode=once),
                  pl.BlockSpec((None, D, F), lambda i, te, tv: (te[i], 0, 0), pipeline_mode=once),
                  pl.BlockSpec((None, F, D), lambda i, te, tv: (te[i], 0, 0), pipeline_mode=once)],
        out_specs=pl.BlockSpec(memory_space=pl.ANY),
        scratch_shapes=[pltpu.VMEM((2, tm * SUBLANES, LANES), _F32), pltpu.VMEM((tm, D), _BF),
                        pltpu.VMEM((tm, F), _BF), pltpu.VMEM((tm * SUBLANES, LANES), _F32),
                        pltpu.SMEM((tm,), jnp.int32), pltpu.SMEM((tm,), jnp.int32),
                        pltpu.SMEM((tm,), jnp.int32),
                        pltpu.SemaphoreType.DMA((2,)), pltpu.SemaphoreType.DMA((1,)),
                        pltpu.SemaphoreType.DMA((3,))],
    )
    return pl.pallas_call(
        functools.partial(_moe_kernel, tf=tf, dump0=dump0, n_dump=(n_out_rows - dump0) // tm),
        grid_spec=grid_spec,
        out_shape=jax.ShapeDtypeStruct((n_out_rows * SUBLANES, LANES), _F32),
        compiler_params=_cparams(("arbitrary",)),
        name="moe_experts",
    )(tile_e, tile_v, src, dst, h, wg, wu, wd)


def _route_plan(route, T, tm):
    A = T * TOP_K
    e_flat = route[:, :TOP_K].astype(jnp.int32).reshape(A)
    order = jnp.argsort(e_flat, stable=True).astype(jnp.int32)
    counts = jnp.sum((e_flat[:, None] == jnp.arange(N_EXPERTS, dtype=jnp.int32)[None, :]).astype(jnp.int32), axis=0)
    starts = jnp.cumsum(counts) - counts
    pcounts = (counts + tm - 1) // tm * tm
    pends = jnp.cumsum(pcounts)
    pstarts = pends - pcounts
    nt = A // tm + N_EXPERTS
    tile0 = jnp.arange(nt, dtype=jnp.int32) * tm
    tile_v = (tile0 < pends[-1]).astype(jnp.int32)
    last_valid = jnp.maximum(pends[-1] // tm - 1, 0)
    tile_e_raw = jnp.minimum(jnp.searchsorted(pends, tile0, side="right"), N_EXPERTS - 1).astype(jnp.int32)
    tile_e = jnp.where(tile_v == 1, tile_e_raw, tile_e_raw[last_valid])
    r = jnp.arange(nt * tm, dtype=jnp.int32)
    e_r = jnp.repeat(tile_e, tm)
    within = r - pstarts[e_r]
    ok = jnp.logical_and(within < counts[e_r], jnp.repeat(tile_v, tm) == 1)
    a = order[jnp.clip(starts[e_r] + within, 0, A - 1)]
    src = jnp.where(ok, a // TOP_K, 0).astype(jnp.int32)
    dump = A + e_r * tm + jnp.clip(within - counts[e_r], 0, tm - 1)
    dst = jnp.where(ok, (a % TOP_K) * T + a // TOP_K, dump).astype(jnp.int32)
    return (tile_e, tile_v, (src * SUBLANES).reshape(nt, tm), (dst * SUBLANES).reshape(nt, tm),
            A + N_EXPERTS * tm)


def _combine_kernel(x_ref, y1_ref, y2_ref, r_ref, mod_ref, o_ref):
    tm = x_ref.shape[0]
    y = _tiles_to_rows(y1_ref, tm) * r_ref[:, 2:3] + _tiles_to_rows(y2_ref, tm) * r_ref[:, 3:4]
    o_ref[...] = x_ref[...] + mod_ref[5:6, :] * y


def _combine(x, y2, route, mod, tm):
    B, S, D = x.shape
    T = B * S
    per_b = S // tm
    out = pl.pallas_call(
        _combine_kernel,
        grid=(T // tm,),
        in_specs=[pl.BlockSpec((tm, D), lambda i: (i, 0)),
                  pl.BlockSpec((tm * SUBLANES, LANES), lambda i: (i, 0)),
                  pl.BlockSpec((tm * SUBLANES, LANES), lambda i: (T // tm + i, 0)),
                  pl.BlockSpec((tm, LANES), lambda i: (i, 0)),
                  pl.BlockSpec((None, 6, D), lambda i: (i // per_b, 0, 0))],
        out_specs=pl.BlockSpec((tm, D), lambda i: (i, 0)),
        out_shape=jax.ShapeDtypeStruct((T, D), _F32),
        compiler_params=_cparams(("parallel",)),
        name="moe_combine",
    )(x.reshape(T, D), y2, y2, route.reshape(T, LANES), mod)
    return out.reshape(B, S, D)


def _pad_cols(a, w):
    return jnp.pad(a, ((0, 0), (0, w - a.shape[1])))


def _rope_tables_even(S):
    pos = jnp.arange(S, dtype=jnp.int32)
    inv = ROPE_THETA ** (-jnp.arange(0, 32, 2, dtype=_F32) / 32)
    def cs(p):
        ang = p.astype(_F32)[:, None] * inv[None, :]
        return jnp.cos(ang), jnp.sin(ang)
    one = lambda w: jnp.ones((S, w), _F32)
    zero = lambda w: jnp.zeros((S, w), _F32)
    c, s = cs(pos)
    ca = jnp.concatenate([c, one(48), c, one(48)], 1)
    sa = jnp.concatenate([-s, zero(48), s, zero(48)], 1)
    cr, sr = cs(pos // GRID_W)
    cc, sc = cs(pos % GRID_W)
    cb = jnp.concatenate([cr, cc, one(32), cr, cc, one(32)], 1)
    sb = jnp.concatenate([-sr, -sc, zero(32), sr, sc, zero(32)], 1)
    return jnp.concatenate([ca, sa, cb, sb], 1)


def _slot_maps():
    r = MLA_ROPE // 2
    mla = ([MLA_NOPE + i for i in range(r)] + list(range(0, 64 - r))
           + [MLA_NOPE + r + i for i in range(r)] + list(range(64 - r, MLA_NOPE)))
    mla += [-1] * (LANES - len(mla))
    q = HEAD_DIM // 4
    gqa = (list(range(0, q)) + list(range(2 * q, 3 * q)) + [-1] * (64 - 2 * q)
           + list(range(q, 2 * q)) + list(range(3 * q, 4 * q)) + [-1] * (64 - 2 * q))
    return mla, gqa


def _to_slots(a, lane_map):
    idx = jnp.asarray([max(i, 0) for i in lane_map], jnp.int32)
    keep = jnp.asarray([1.0 if i >= 0 else 0.0 for i in lane_map], a.dtype)
    return jnp.take(a, idx, axis=-1) * keep


def _rope_tables_odd(S):
    pos = jnp.arange(S, dtype=_F32)
    inv = ROPE_THETA ** (-jnp.arange(0, HEAD_DIM, 2, dtype=_F32) / HEAD_DIM)
    ang = pos[:, None] * inv[None, :]
    c, s = jnp.cos(ang), jnp.sin(ang)
    return jnp.concatenate([c, c, c, c, -s, -s, s, s], 1)


def _tiles(S):
    return dict(tm_pre=min(S, 512), tq=min(S, 2048), tm_o=min(S, 512),
                tm_ffn=min(S, 512), tf_ffn=512, tm_qkv=min(S, 512),
                tl=128, tm_moe=min(S, 512), tf_moe=512, tm_c=min(S, 512))


def kernel(x, c, ada_even_w, ada_even_b, norm_even_mix, norm_even_ffn, even_w_in, mla_q_norm, mla_w_uq, mla_kv_norm, mla_w_ukv, mla_q_gain, mla_k_gain, gqa_q_gain, gqa_k_gain, even_w_out, ffn_w_gate, ffn_w_up, ffn_w_down, ada_odd_w, ada_odd_b, norm_odd_mix, norm_odd_ffn, dil_w_qkv, dil_q_gain, dil_k_gain, dil_w_out, moe_router, moe_w_gate, moe_w_up, moe_w_down):
    B, S, D = x.shape
    T = B * S
    cfg = _tiles(S)

    mod_e = _ada_mod(c, ada_even_w[0], ada_even_b[0]).reshape(B, 6, D)
    mod_o = _ada_mod(c, ada_odd_w[0], ada_odd_b[0]).reshape(B, 6, D)

    w = even_w_in[0]
    sp = [MLA_Q_RANK, MLA_Q_RANK + MLA_KV_RANK, MLA_Q_RANK + MLA_KV_RANK + MLA_ROPE]
    sp.append(sp[-1] + GQA_HEADS * HEAD_DIM)
    sp.append(sp[-1] + GQA_KV_HEADS * HEAD_DIM)
    w_cq, w_ckv, w_kpe = w[:, :sp[0]], w[:, sp[0]:sp[1]], w[:, sp[1]:sp[2]]
    w_qb, w_kb, w_vb = w[:, sp[2]:sp[3]], w[:, sp[3]:sp[4]], w[:, sp[4]:]
    mla_map, gqa_map = _slot_maps()
    na = MLA_NOPE + MLA_ROPE
    nope_only = [i if 0 <= i < MLA_NOPE else -1 for i in mla_map]
    rope_only = [i - MLA_NOPE if i >= MLA_NOPE else -1 for i in mla_map]
    gslots = lambda a, n: _to_slots(a.reshape(D, n, HEAD_DIM), gqa_map).reshape(D, n * LANES)
    w_in = jnp.concatenate([w_cq, w_ckv, _to_slots(w_kpe, rope_only), gslots(w_qb, GQA_HEADS),
                            gslots(w_kb, GQA_KV_HEADS)], axis=1).astype(_BF)
    w_uq = _to_slots(mla_w_uq[0].reshape(MLA_Q_RANK, MLA_HEADS, na), mla_map)
    w_uq = w_uq.reshape(MLA_Q_RANK, MLA_HEADS * LANES).astype(_BF)
    w_ukv = mla_w_ukv[0].reshape(MLA_KV_RANK, MLA_HEADS, MLA_NOPE + MLA_V)
    w_uk = _to_slots(w_ukv[:, :, :MLA_NOPE], nope_only).reshape(MLA_KV_RANK, MLA_HEADS * LANES).astype(_BF)
    w_uvt = w_ukv[:, :, MLA_NOPE:].reshape(MLA_KV_RANK, MLA_HEADS * MLA_V).T.astype(_BF)
    w_vbt = w_vb.T.astype(_BF)
    g_rows = [_to_slots(mla_q_gain[0], mla_map) * (na ** -0.5 * LOG2E), _to_slots(mla_k_gain[0], mla_map),
              _to_slots(gqa_q_gain[0], gqa_map) * (HEAD_DIM ** -0.5 * LOG2E), _to_slots(gqa_k_gain[0], gqa_map)]
    bound_a = na * jnp.max(jnp.abs(g_rows[0])) * jnp.max(jnp.abs(g_rows[1]))
    bound_b = HEAD_DIM * jnp.max(jnp.abs(g_rows[2])) * jnp.max(jnp.abs(g_rows[3]))
    fast_a, fast_b = bound_a <= SCORE_BOUND_MAX, bound_b <= SCORE_BOUND_MAX
    last_lane = jnp.zeros((LANES,), _F32).at[LANES - 1].set(1.0)
    g_rows += [last_lane, last_lane * jnp.where(fast_a, -bound_a, 0.0),
               last_lane, last_lane * jnp.where(fast_b, -bound_b, 0.0)]
    gains_e = jnp.stack(g_rows)
    rope_e = _rope_tables_even(S)
    q_all, k_all, vt_all = _pre_even(
        x, mod_e, norm_even_mix[0].reshape(1, D), w_in, mla_q_norm[0].reshape(1, -1), w_uq,
        mla_kv_norm[0].reshape(1, -1), w_uk, w_uvt, w_vbt, gains_e, rope_e, cfg["tm_pre"])
    as_flag = lambda f: f.astype(jnp.int32).reshape(1)
    o_a = _attention(as_flag(fast_a), q_all, k_all, vt_all, q_off=0, k_off=0, shared_kv=False, tq=cfg["tq"])
    o_b = _attention(as_flag(fast_b), q_all, k_all, vt_all, q_off=MLA_HEADS, k_off=MLA_HEADS,
                     shared_kv=True, tq=cfg["tq"])
    x1, h1 = _oproj(x, mod_e, norm_even_ffn[0].reshape(1, D), even_w_out[0].astype(_BF), [o_a, o_b],
                    cfg["tm_o"], _BF)
    x2, h2 = _ffn(x1, h1, mod_e, mod_o, norm_odd_mix[0].reshape(1, D), ffn_w_gate[0].astype(_BF),
                  ffn_w_up[0].astype(_BF), ffn_w_down[0].astype(_BF), cfg["tm_ffn"], cfg["tf_ffn"])

    hh = HEAD_DIM // 2
    pair = lambda v: jnp.concatenate([v[:hh], v[:hh], v[hh:], v[hh:]])
    gains_o = jnp.stack([pair(dil_q_gain[0]) * (HEAD_DIM ** -0.5 * LOG2E), pair(dil_k_gain[0])])
    wq, wk, wv = jnp.split(dil_w_qkv[0], 3, axis=1)
    perm = lambda w: w.reshape(D, D // LANES, 2, 2, hh).transpose(0, 1, 3, 2, 4).reshape(D, D)
    w_qkv = jnp.concatenate([perm(wq), perm(wk), wv], axis=1).astype(_BF)
    qd, kd, vd = _qkv(h2, w_qkv, gains_o, _rope_tables_odd(S), cfg["tm_qkv"])
    bound_d = HEAD_DIM * jnp.max(jnp.abs(gains_o[0])) * jnp.max(jnp.abs(gains_o[1]))
    fast_d = bound_d <= SCORE_BOUND_MAX
    o_d = _dilated(as_flag(fast_d), jnp.where(fast_d, -bound_d, 0.0), qd, kd, vd, cfg["tl"])
    r32 = _pad_cols(moe_router[0], LANES)
    r_hi = r32.astype(_BF)
    r_lo = (r32 - r_hi.astype(_F32)).astype(_BF)
    x3, h3, route = _oproj(x2, mod_o, norm_odd_ffn[0].reshape(1, D), dil_w_out[0].astype(_BF), [o_d],
                           cfg["tm_o"], _F32, router=jnp.stack([r_hi, r_lo]))
    tm = cfg["tm_moe"]
    tile_e, tile_v, src, dst, n_rows = _route_plan(route.reshape(T, LANES), T, tm)
    y2 = _moe(h3, tile_e, tile_v, src, dst, moe_w_gate[0].astype(_BF),
              moe_w_up[0].astype(_BF), moe_w_down[0].astype(_BF), n_rows, tm, cfg["tf_moe"])
    return _combine(x3, y2, route, mod_o, cfg["tm_c"])
```

```python
import functools
import math

import jax
import jax.numpy as jnp
from jax import lax
from jax.experimental import pallas as pl
from jax.experimental.pallas import tpu as pltpu

_BF = jnp.bfloat16
_F32 = jnp.float32

GRID_W = 64
HEAD_DIM = 64
ROPE_THETA = 10000.0
EPS = 1e-6
MLA_HEADS = 8
MLA_Q_RANK = 256
MLA_KV_RANK = 128
MLA_NOPE = 64
MLA_ROPE = 32
MLA_V = 64
GQA_HEADS = 8
GQA_KV_HEADS = 2
DIL_PATTERNS = ((128, 1), (512, 4), (2048, 16))
N_EXPERTS = 8
TOP_K = 2
NEG = -1e30
LOG2E = math.log2(math.e)

LANES = 128
VMEM_LIMIT = 56 * 1024 * 1024


def _cparams(sem, vmem=VMEM_LIMIT):
    return pltpu.CompilerParams(dimension_semantics=sem, vmem_limit_bytes=vmem)


def _silu(x):
    return x / (1.0 + jnp.exp(-x))


def _modulate(x, g, shift, scale):
    ms = jnp.mean(x * x, axis=-1, keepdims=True)
    return x * lax.rsqrt(ms + EPS) * g * (1.0 + scale) + shift


def _norm(v, n, g):
    return v * lax.rsqrt(jnp.sum(v * v, axis=-1, keepdims=True) * (1.0 / n) + EPS) * g


SUBLANES = 8


def _rows_to_tiles(ref, x):
    tm = x.shape[0]
    for c in range(SUBLANES):
        ref[pl.ds(c, tm, stride=SUBLANES), :] = x[:, c * LANES:(c + 1) * LANES]


def _tiles_to_rows(ref, tm):
    return jnp.concatenate([ref[pl.ds(c, tm, stride=SUBLANES), :] for c in range(SUBLANES)], axis=1)


def _rope(x, c, sa, sb, k):
    return x * c + pltpu.roll(x, LANES - k, 1) * sa + pltpu.roll(x, k, 1) * sb


def _mod_kernel(c_ref, w_ref, b_ref, o_ref):
    sc = _silu(c_ref[...])
    o_ref[...] = jnp.dot(sc.astype(_BF), w_ref[...].astype(_BF),
                         preferred_element_type=_F32) + b_ref[...]


def _ada_mod(c, w, b):
    B, D = c.shape
    N = w.shape[1]
    tn = min(N, 1536)
    return pl.pallas_call(
        _mod_kernel,
        grid=(N // tn,),
        in_specs=[pl.BlockSpec((B, D), lambda j: (0, 0)),
                  pl.BlockSpec((D, tn), lambda j: (0, j)),
                  pl.BlockSpec((1, tn), lambda j: (0, j))],
        out_specs=pl.BlockSpec((B, tn), lambda j: (0, j)),
        out_shape=jax.ShapeDtypeStruct((B, N), _F32),
        compiler_params=_cparams(("arbitrary",)),
        name="ada_mod",
    )(c, w, b.reshape(1, N))


def _pre_even_kernel(x_ref, mod_ref, g_ref, w_in_ref, qn_ref, w_uq_ref, kvn_ref, w_uk_ref, w_uvt_ref,
                     w_vbt_ref, gains_ref, rope_ref, q_ref, k_ref, vt_ref, nq_scr, nk_scr):
    h = _modulate(x_ref[...], g_ref[...], mod_ref[0:1, :], mod_ref[1:2, :]).astype(_BF)
    y = jnp.dot(h, w_in_ref[...], preferred_element_type=_F32)
    cqn = _norm(y[:, 0:256], MLA_Q_RANK, qn_ref[...]).astype(_BF)
    qa = jnp.dot(cqn, w_uq_ref[...], preferred_element_type=_F32)
    ckvn = _norm(y[:, 256:384], MLA_KV_RANK, kvn_ref[...]).astype(_BF)
    kn = jnp.dot(ckvn, w_uk_ref[...], preferred_element_type=_F32)
    kpe = y[:, 384:512]
    nt_dims = (((1,), (1,)), ((), ()))
    vt_a = lax.dot_general(w_uvt_ref[...], ckvn, nt_dims, preferred_element_type=_F32).astype(_BF)
    vt_b = lax.dot_general(w_vbt_ref[...], h, nt_dims, preferred_element_type=_F32).astype(_BF)
    tm = x_ref.shape[0]
    ones_blk = jnp.where(lax.broadcasted_iota(jnp.int32, (LANES - MLA_V, tm), 0) == 0, 1.0, 0.0).astype(_BF)
    for hh in range(MLA_HEADS + GQA_KV_HEADS):
        src = vt_a[hh * MLA_V:(hh + 1) * MLA_V] if hh < MLA_HEADS else \
            vt_b[(hh - MLA_HEADS) * HEAD_DIM:(hh - MLA_HEADS + 1) * HEAD_DIM]
        vt_ref[hh * LANES:hh * LANES + MLA_V, :] = src
        vt_ref[hh * LANES + MLA_V:(hh + 1) * LANES, :] = ones_blk
    ca, sa = rope_ref[:, 0:128], rope_ref[:, 128:256]
    cb, sb = rope_ref[:, 256:384], rope_ref[:, 384:512]
    gqa, gka = gains_ref[0:1, :], gains_ref[1:2, :]
    gqb, gkb = gains_ref[2:3, :], gains_ref[3:4, :]
    na = MLA_NOPE + MLA_ROPE

    for hh in range(MLA_HEADS):
        sl = slice(hh * LANES, (hh + 1) * LANES)
        nq_scr[:, sl] = _norm(qa[:, sl], na, gqa)
        nk_scr[:, sl] = _norm(kn[:, sl] + kpe, na, gka)
    for hh in range(GQA_HEADS):
        src = slice(512 + hh * LANES, 512 + (hh + 1) * LANES)
        dst = slice((MLA_HEADS + hh) * LANES, (MLA_HEADS + hh + 1) * LANES)
        nq_scr[:, dst] = _norm(y[:, src], HEAD_DIM, gqb)
    for g in range(GQA_KV_HEADS):
        src = slice(1536 + g * LANES, 1536 + (g + 1) * LANES)
        dst = slice((MLA_HEADS + g) * LANES, (MLA_HEADS + g + 1) * LANES)
        nk_scr[:, dst] = _norm(y[:, src], HEAD_DIM, gkb)

    @pl.when(pl.program_id(1) >= 0)
    def _():
        def rope(ref, out, n_a, n_all, row):
            for hh in range(n_all):
                sl = slice(hh * LANES, (hh + 1) * LANES)
                c, s = (ca, sa) if hh < n_a else (cb, sb)
                b = gains_ref[row:row + 1, :] if hh < n_a else gains_ref[row + 2:row + 3, :]
                v = ref[:, sl]
                out[:, sl] = (v * c + pltpu.roll(v, 64, 1) * s + b).astype(_BF)

        rope(nq_scr, q_ref, MLA_HEADS, MLA_HEADS + GQA_HEADS, 4)
        rope(nk_scr, k_ref, MLA_HEADS, MLA_HEADS + GQA_KV_HEADS, 5)


def _pre_even(x, mod, g, w_in, qn, w_uq, kvn, w_uk, w_uvt, w_vbt, gains, rope, tm):
    B, S, D = x.shape
    nq = (MLA_HEADS + GQA_HEADS) * LANES
    nk = (MLA_HEADS + GQA_KV_HEADS) * LANES
    nvt = (MLA_HEADS + GQA_KV_HEADS) * LANES
    full = lambda a: pl.BlockSpec(a.shape, lambda b, i: (0,) * a.ndim)
    tok = lambda w: pl.BlockSpec((None, tm, w), lambda b, i: (b, i, 0))
    return pl.pallas_call(
        _pre_even_kernel,
        grid=(B, S // tm),
        in_specs=[tok(D),
                  pl.BlockSpec((None, 6, D), lambda b, i: (b, 0, 0)),
                  full(g), full(w_in), full(qn), full(w_uq), full(kvn), full(w_uk), full(w_uvt),
                  full(w_vbt), full(gains),
                  pl.BlockSpec((tm, rope.shape[1]), lambda b, i: (i, 0))],
        out_specs=[tok(nq), tok(nk), pl.BlockSpec((None, None, nvt, tm), lambda b, i: (b, i, 0, 0))],
        out_shape=[jax.ShapeDtypeStruct((B, S, nq), _BF), jax.ShapeDtypeStruct((B, S, nk), _BF),
                   jax.ShapeDtypeStruct((B, S // tm, nvt, tm), _BF)],
        scratch_shapes=[pltpu.VMEM((tm, nq), _F32), pltpu.VMEM((tm, nk), _F32)],
        compiler_params=_cparams(("parallel", "parallel")),
        name="pre_even",
    )(x, mod, g, w_in, qn, w_uq, kvn, w_uk, w_uvt, w_vbt, gains, rope)


SCORE_BOUND_MAX = 60.0


def _attn_kernel(fast_ref, q_ref, k_ref, vt_ref, o_ref, s_scr, *, shared_kv):
    @pl.when(fast_ref[0] == 1)
    def _():
        _attn_fixed_shift(q_ref, k_ref, vt_ref, o_ref, shared_kv=shared_kv)

    @pl.when(fast_ref[0] != 1)
    def _():
        _attn_running_max(q_ref, k_ref, vt_ref, o_ref, s_scr, shared_kv=shared_kv)


def _attn_fixed_shift(q_ref, k_ref, vt_ref, o_ref, *, shared_kv):
    tq = q_ref.shape[0]
    nk, _, tk = vt_ref.shape
    qs = [q_ref[:, hh * LANES:(hh + 1) * LANES] for hh in range(2)]

    def body(j, carry):
        r0 = pl.multiple_of(j * tk, tk)
        out = []
        for hh in range(2):
            kc = 0 if shared_kv else hh * LANES
            k = k_ref[pl.ds(r0, tk), kc:kc + LANES]
            st = lax.dot_general(k, qs[hh], (((1,), (1,)), ((), ())), preferred_element_type=_F32)
            p = jnp.exp2(st).astype(_BF)
            out.append(carry[hh] + jnp.dot(vt_ref[j, kc:kc + LANES, :], p, preferred_element_type=_F32))
        return tuple(out)

    z = jnp.zeros((LANES, tq), _F32)
    res = lax.fori_loop(0, nk, body, (z, z), unroll=2)
    nv = HEAD_DIM
    halves = [acc[0:nv, :] / acc[nv:nv + 1, :] for acc in res]
    o_ref[...] = jnp.concatenate(halves, axis=0).T.astype(o_ref.dtype)


def _attn_running_max(q_ref, k_ref, vt_ref, o_ref, s_scr, *, shared_kv):
    tq = q_ref.shape[0]
    nk, _, tk = vt_ref.shape
    qs = [q_ref[:, hh * LANES:(hh + 1) * LANES] for hh in range(2)]

    def scores(j, hh):
        r0 = j * tk if isinstance(j, int) else pl.multiple_of(j * tk, tk)
        kc = 0 if shared_kv else hh * LANES
        k = k_ref[pl.ds(r0, tk), kc:kc + LANES]
        return lax.dot_general(k, qs[hh], (((1,), (1,)), ((), ())), preferred_element_type=_F32)

    def put_scores(j, hh, slot):
        st = scores(j, hh)
        s_scr[hh, slot] = st
        return jnp.max(st, axis=0, keepdims=True)

    def step(j, hh, carry, cur, last=False):
        m, mc, acc = carry
        mc_next = mc if last else put_scores(j + 1, hh, 1 - cur)
        st = s_scr[hh, cur]
        m_new = jnp.maximum(m, mc)
        alpha = jnp.exp2(m - m_new)
        p = jnp.exp2(st - m_new).astype(_BF)
        vc = 0 if shared_kv else hh * LANES
        acc = alpha * acc + jnp.dot(vt_ref[j, vc:vc + LANES, :], p, preferred_element_type=_F32)
        return m_new, mc_next, acc

    def body(jj, carry):
        c = list(carry)
        for sub in range(2):
            for hh in range(2):
                c[hh] = step(2 * jj + sub, hh, c[hh], sub)
        return tuple(c)

    init = tuple((jnp.full((1, tq), NEG, _F32), put_scores(0, hh, 0), jnp.zeros((LANES, tq), _F32))
                 for hh in range(2))
    res = lax.fori_loop(0, nk // 2 - 1, body, init)
    res = list(res)
    for sub in range(2):
        for hh in range(2):
            res[hh] = step(nk - 2 + sub, hh, res[hh], sub, last=(sub == 1))
    nv = HEAD_DIM
    halves = [acc[0:nv, :] / acc[nv:nv + 1, :] for (_, _, acc) in res]
    o_ref[...] = jnp.concatenate(halves, axis=0).T.astype(o_ref.dtype)


def _attention(fast, q, k, vt, *, q_off, k_off, shared_kv, tq):
    B, S, _ = q.shape
    nk, _, tk = vt.shape[1:]
    assert nk % 2 == 0 and nk >= 2
    npairs = 4
    if shared_kv:
        kspec = pl.BlockSpec((None, S, LANES), lambda b, p, i, f: (b, 0, k_off + p // 2))
        vspec = pl.BlockSpec((None, nk, LANES, tk), lambda b, p, i, f: (b, 0, k_off + p // 2, 0))
    else:
        kspec = pl.BlockSpec((None, S, 2 * LANES), lambda b, p, i, f: (b, 0, k_off // 2 + p))
        vspec = pl.BlockSpec((None, nk, 2 * LANES, tk), lambda b, p, i, f: (b, 0, k_off // 2 + p, 0))
    grid_spec = pltpu.PrefetchScalarGridSpec(
        num_scalar_prefetch=1,
        grid=(B, npairs, S // tq),
        in_specs=[pl.BlockSpec((None, tq, 2 * LANES), lambda b, p, i, f: (b, i, q_off // 2 + p)),
                  kspec, vspec],
        out_specs=pl.BlockSpec((None, tq, LANES), lambda b, p, i, f: (b, i, p)),
        scratch_shapes=[pltpu.VMEM((2, 2, tk, tq), _F32)],
    )
    return pl.pallas_call(
        functools.partial(_attn_kernel, shared_kv=shared_kv),
        grid_spec=grid_spec,
        out_shape=jax.ShapeDtypeStruct((B, S, npairs * LANES), _BF),
        compiler_params=_cparams(("parallel", "parallel", "parallel")),
        name="attn_gqa" if shared_kv else "attn_mla",
    )(fast, q, k, vt)


def _oproj_kernel(*refs, n_in, route):
    x_ref, mod_ref, g_ref, w_ref = refs[0], refs[1], refs[2], refs[3]
    o_refs = refs[4:4 + n_in]
    pos = 4 + n_in
    if route:
        r_ref = refs[pos]
        pos += 1
    x_out, h_out = refs[pos], refs[pos + 1]
    y = None
    off = 0
    for o_ref in o_refs:
        if len(o_ref.shape) == 3:
            o = jnp.concatenate([o_ref[p] for p in range(o_ref.shape[0])], axis=1)
        else:
            o = o_ref[...]
        w = o.shape[1]
        t = jnp.dot(o, w_ref[off:off + w, :], preferred_element_type=_F32)
        y = t if y is None else y + t
        off += w
    x1 = x_ref[...] + mod_ref[2:3, :] * y
    x_out[...] = x1
    h = _modulate(x1, g_ref[...], mod_ref[3:4, :], mod_ref[4:5, :])
    if route:
        _rows_to_tiles(h_out, h)
    else:
        h_out[...] = h.astype(h_out.dtype)
    if route:
        route_out = refs[pos + 2]
        hh = h.astype(_BF)
        hl = (h - hh.astype(_F32)).astype(_BF)
        rh, rl = r_ref[0], r_ref[1]
        logits = (jnp.dot(hh, rh, preferred_element_type=_F32)
                  + jnp.dot(hh, rl, preferred_element_type=_F32)
                  + jnp.dot(hl, rh, preferred_element_type=_F32))
        tm = logits.shape[0]
        lane = lax.broadcasted_iota(jnp.int32, (tm, LANES), 1)
        lg = jnp.where(lane < N_EXPERTS, logits, NEG)
        m1 = jnp.max(lg, axis=-1, keepdims=True)
        lanef = lane.astype(_F32)
        i1 = jnp.min(jnp.where(lg == m1, lanef, float(LANES)), axis=-1, keepdims=True)
        lg2 = jnp.where(lanef == i1, NEG, lg)
        m2 = jnp.max(lg2, axis=-1, keepdims=True)
        i2 = jnp.min(jnp.where(lg2 == m2, lanef, float(LANES)), axis=-1, keepdims=True)
        e = jnp.exp(m2 - m1)
        g1 = 1.0 / (1.0 + e)
        g2 = e / (1.0 + e)
        route_out[...] = jnp.where(lane == 0, i1,
                                   jnp.where(lane == 1, i2,
                                             jnp.where(lane == 2, g1, jnp.where(lane == 3, g2, 0.0))))


def _oproj(x, mod, g, w, o_list, tm, h_dtype, router=None):
    B, S, D = x.shape
    n_in = len(o_list)
    route = router is not None
    tok = lambda wd: pl.BlockSpec((None, tm, wd), lambda b, i: (b, i, 0))
    in_specs = [tok(D), pl.BlockSpec((None, 6, D), lambda b, i: (b, 0, 0)),
                pl.BlockSpec((1, D), lambda b, i: (0, 0)),
                pl.BlockSpec(w.shape, lambda b, i: (0, 0))]
    for o in o_list:
        if o.ndim == 4:
            in_specs.append(pl.BlockSpec((None, o.shape[1], tm, LANES), lambda b, i: (b, 0, i, 0)))
        else:
            in_specs.append(tok(o.shape[2]))
    args = [x, mod, g, w] + list(o_list)
    out_specs = [tok(D), tok(D)]
    out_shape = [jax.ShapeDtypeStruct((B, S, D), _F32), jax.ShapeDtypeStruct((B, S, D), h_dtype)]
    if route:
        assert D == SUBLANES * LANES
        per_b = S // tm
        out_specs[1] = pl.BlockSpec((tm * SUBLANES, LANES), lambda b, i: (b * per_b + i, 0))
        out_shape[1] = jax.ShapeDtypeStruct((B * S * SUBLANES, LANES), _F32)
        in_specs.append(pl.BlockSpec(router.shape, lambda b, i: (0, 0, 0)))
        args.append(router)
        out_specs.append(tok(LANES))
        out_shape.append(jax.ShapeDtypeStruct((B, S, LANES), _F32))
    return pl.pallas_call(
        functools.partial(_oproj_kernel, n_in=n_in, route=route),
        grid=(B, S // tm),
        in_specs=in_specs, out_specs=out_specs, out_shape=out_shape,
        compiler_params=_cparams(("parallel", "parallel")),
        name="oproj_route" if route else "oproj",
    )(*args)


def _swiglu_hidden(x, wg_ref, wu_ref, a_scr, tf, after_chunk=None):
    F = wg_ref.shape[1]
    for c0 in range(0, F, tf):
        c1 = min(c0 + tf, F)
        g = jnp.dot(x, wg_ref[:, c0:c1], preferred_element_type=_F32)
        u = jnp.dot(x, wu_ref[:, c0:c1], preferred_element_type=_F32)
        a_scr[:, c0:c1] = (_silu(g) * u).astype(_BF)
        if after_chunk is not None:
            after_chunk(c0 // tf)


def _ffn_kernel(x_ref, h_ref, mod_ref, modn_ref, gn_ref, wg_ref, wu_ref, wd_ref, x_out, h_out, a_scr, *, tf):
    _swiglu_hidden(h_ref[...], wg_ref, wu_ref, a_scr, tf)
    y = jnp.dot(a_scr[...], wd_ref[...], preferred_element_type=_F32)
    x2 = x_ref[...] + mod_ref[5:6, :] * y
    x_out[...] = x2
    h_out[...] = _modulate(x2, gn_ref[...], modn_ref[0:1, :], modn_ref[1:2, :]).astype(h_out.dtype)


def _ffn(x, h, mod, modn, gn, wg, wu, wd, tm, tf):
    B, S, D = x.shape
    T = B * S
    F = wg.shape[1]
    per_b = S // tm
    x2 = x.reshape(T, D)
    h2 = h.reshape(T, D)
    tok = pl.BlockSpec((tm, D), lambda i: (i, 0))
    modspec = pl.BlockSpec((None, 6, D), lambda i: (i // per_b, 0, 0))
    once = pl.Buffered(1)
    xo, ho = pl.pallas_call(
        functools.partial(_ffn_kernel, tf=tf),
        grid=(T // tm,),
        in_specs=[tok, tok, modspec, modspec, pl.BlockSpec((1, D), lambda i: (0, 0)),
                  pl.BlockSpec((D, F), lambda i: (0, 0), pipeline_mode=once),
                  pl.BlockSpec((D, F), lambda i: (0, 0), pipeline_mode=once),
                  pl.BlockSpec((F, D), lambda i: (0, 0), pipeline_mode=once)],
        out_specs=[tok, tok],
        out_shape=[jax.ShapeDtypeStruct((T, D), _F32), jax.ShapeDtypeStruct((T, D), _BF)],
        scratch_shapes=[pltpu.VMEM((tm, F), _BF)],
        compiler_params=_cparams(("parallel",)),
        name="ffn_dense",
    )(x2, h2, mod, modn, gn, wg, wu, wd)
    return xo.reshape(B, S, D), ho.reshape(B, S, D)


def _qkv_kernel(h_ref, w_ref, gains_ref, rope_ref, q_ref, k_ref, v_ref, n_scr):
    tm, D = h_ref.shape
    h = h_ref[...]
    lane = lax.broadcasted_iota(jnp.int32, (tm, LANES), 1)
    head_a = (lane & 63) < 32

    def head_pair_norm(v, g):
        sq = v * v
        s_a = jnp.sum(jnp.where(head_a, sq, 0.0), axis=-1, keepdims=True)
        s_b = jnp.sum(jnp.where(head_a, 0.0, sq), axis=-1, keepdims=True)
        r = lax.rsqrt(jnp.where(head_a, s_a, s_b) * (1.0 / HEAD_DIM) + EPS)
        return v * r * g

    y = jnp.dot(h, w_ref[...], preferred_element_type=_F32)
    for p in range(D // LANES):
        sl = slice(p * LANES, (p + 1) * LANES)
        n_scr[:, sl] = head_pair_norm(y[:, sl], gains_ref[0:1, :])
        n_scr[:, D + p * LANES:D + (p + 1) * LANES] = head_pair_norm(
            y[:, D + p * LANES:D + (p + 1) * LANES], gains_ref[1:2, :])
        v_ref[p] = y[:, 2 * D + p * LANES:2 * D + (p + 1) * LANES].astype(_BF)

    @pl.when(pl.program_id(1) >= 0)
    def _():
        c, s = rope_ref[:, 0:128], rope_ref[:, 128:256]
        for p in range(D // LANES):
            vq = n_scr[:, p * LANES:(p + 1) * LANES]
            vk = n_scr[:, D + p * LANES:D + (p + 1) * LANES]
            q_ref[p] = (vq * c + pltpu.roll(vq, 64, 1) * s).astype(_BF)
            k_ref[p] = (vk * c + pltpu.roll(vk, 64, 1) * s).astype(_BF)


def _qkv(h, w, gains, rope, tm):
    B, S, D = h.shape
    tok = pl.BlockSpec((None, tm, D), lambda b, i: (b, i, 0))
    hp = pl.BlockSpec((None, D // LANES, tm, LANES), lambda b, i: (b, 0, i, 0))
    sds = jax.ShapeDtypeStruct((B, D // LANES, S, LANES), _BF)
    return pl.pallas_call(
        _qkv_kernel,
        grid=(B, S // tm),
        in_specs=[tok, pl.BlockSpec(w.shape, lambda b, i: (0, 0)),
                  pl.BlockSpec((2, LANES), lambda b, i: (0, 0)),
                  pl.BlockSpec((tm, 2 * LANES), lambda b, i: (i, 0))],
        out_specs=[hp, hp, hp],
        out_shape=[sds, sds, sds],
        scratch_shapes=[pltpu.VMEM((tm, 2 * D), _F32)],
        compiler_params=_cparams(("parallel", "parallel")),
        name="qkv_dil",
    )(h, w, gains, rope)


DIL_GROUP = 4


def _dil_kernel(fast_ref, q_ref, k_ref, v_ref, shift_ref, o_ref, tmp, qp, kp, vp, acc, mm, ll, accp, mmp,
                llp, bias_scr, *, tl):
    @pl.when(fast_ref[0] == 1)
    def _():
        _dil_body(q_ref, k_ref, v_ref, shift_ref, o_ref, tmp, qp, kp, vp, acc, mm, ll, accp, mmp, llp,
                  bias_scr, tl=tl, fixed_shift=True)

    @pl.when(fast_ref[0] != 1)
    def _():
        _dil_body(q_ref, k_ref, v_ref, shift_ref, o_ref, tmp, qp, kp, vp, acc, mm, ll, accp, mmp, llp,
                  bias_scr, tl=tl, fixed_shift=False)


def _dil_body(q_ref, k_ref, v_ref, shift_ref, o_ref, tmp, qp, kp, vp, acc, mm, ll, accp, mmp, llp,
              bias_scr, *, tl, fixed_shift):
    S = q_ref.shape[0]
    ng = S // DIL_GROUP
    pitch = DIL_GROUP + 1

    def spread(dst, src):
        for j in range(DIL_GROUP):
            dst[pl.ds(j, ng, stride=pitch), :] = src[pl.ds(j, ng, stride=DIL_GROUP), :]

    for src, dst in ((q_ref, qp), (k_ref, kp), (v_ref, vp)):
        tmp[...] = src[...].astype(_F32)
        spread(dst, tmp)
    lane = lax.broadcasted_iota(jnp.int32, (tl, LANES), 1)
    lo = lane < 64
    head_a = (lane & 63) < 32
    for bi, (window, d) in enumerate(DIL_PATTERNS):
        L = S // d
        half = window // (2 * d)
        t = min(tl, L)
        W = min(t + 2 * half, L)
        nt = L // t
        if d > 1:
            assert d % DIL_GROUP == 0
        sd = d * pitch // DIL_GROUP

        def rows(r, first, n, d=d, sd=sd):
            if d == 1:
                return pl.ds(first, n)
            return pl.ds(r + r // DIL_GROUP + sd * first, n, stride=sd)

        assert t % half == 0 and W <= t + 2 * half
        qa = lax.broadcasted_iota(jnp.int32, (t, W), 0)
        kb = lax.broadcasted_iota(jnp.int32, (t, W), 1)
        inside = shift_ref[0:1, 0:W] if fixed_shift else 0.0
        for case in range(3):
            ok = jnp.abs(kb - qa - case * half) <= half
            bias_scr[case, 0:t, 0:W] = jnp.where(ok, inside, NEG)

        def body(idx, carry, bi=bi, d=d, L=L, half=half, t=t, W=W, nt=nt, rows=rows):
            r = idx // nt
            l0 = (idx % nt) * t
            start = jnp.clip(l0 - half, 0, L - W)
            if d == 1:
                qt = q_ref[pl.ds(pl.multiple_of(l0, t), t), :]
                kw = k_ref[pl.ds(pl.multiple_of(start, 64), W), :]
                vw = v_ref[pl.ds(pl.multiple_of(start, 64), W), :]
            else:
                qt = qp[rows(r, l0, t), :].astype(_BF)
                kw = kp[rows(r, start, W), :].astype(_BF)
                vw = vp[rows(r, start, W), :].astype(_BF)
            lo_t = lo[:t]
            qa_t = head_a[:t]
            zq = jnp.zeros_like(qt)
            q2 = jnp.concatenate([jnp.where(qa_t, qt, zq), jnp.where(qa_t, zq, qt)], axis=0)
            s = lax.dot_general(q2, kw, (((1,), (1,)), ((), ())), preferred_element_type=_F32)
            bias = bias_scr[(l0 - start) // half, 0:t, 0:W]
            s = s + jnp.concatenate([bias, bias], axis=0)
            if fixed_shift:
                p = jnp.exp2(s)
            else:
                m = jnp.max(s, axis=-1, keepdims=True)
                p = jnp.exp2(s - m)
                m_new = jnp.where(lo_t, m[:t], m[t:])
            den = jnp.sum(p, axis=-1, keepdims=True)
            o2 = jnp.dot(p.astype(_BF), vw, preferred_element_type=_F32)
            o_new = jnp.where(lo_t, o2[:t], o2[t:])
            l_new = jnp.where(lo_t, den[:t], den[t:])
            tok = rows(r, l0, t)
            if d == 1:
                assert bi == 0
                acc[tok, :] = o_new
                ll[tok, :] = l_new
                if not fixed_shift:
                    mm[tok, :] = m_new
            elif fixed_shift:
                accp[tok, :] = accp[tok, :] + o_new
                llp[tok, :] = llp[tok, :] + l_new
            else:
                m_old = mmp[tok, :]
                mx = jnp.maximum(m_old, m_new)
                a_old = jnp.exp2(m_old - mx)
                a_new = jnp.exp2(m_new - mx)
                accp[tok, :] = accp[tok, :] * a_old + o_new * a_new
                llp[tok, :] = llp[tok, :] * a_old + l_new * a_new
                mmp[tok, :] = mx
            return carry

        lax.fori_loop(0, d * nt, body, 0, unroll=min(8, d * nt))
        if bi == 0:
            spread(accp, acc)
            spread(llp, ll)
            if not fixed_shift:
                spread(mmp, mm)
    for j in range(DIL_GROUP):
        grp = pl.ds(j, ng, stride=pitch)
        tmp[pl.ds(j, ng, stride=DIL_GROUP), :] = accp[grp, :] / llp[grp, :]
    o_ref[...] = tmp[...].astype(o_ref.dtype)


def _dilated(fast, shift, q, k, v, tl):
    B, P, S, _ = q.shape
    wmax = tl + max(w // d for w, d in DIL_PATTERNS)
    spec = pl.BlockSpec((None, None, S, LANES), lambda b, p, f: (b, p, 0, 0))
    scr = pltpu.VMEM((S, LANES), _F32)
    scrp = pltpu.VMEM((S // DIL_GROUP * (DIL_GROUP + 1), LANES), _F32)
    grid_spec = pltpu.PrefetchScalarGridSpec(
        num_scalar_prefetch=1,
        grid=(B, P),
        in_specs=[spec, spec, spec, pl.BlockSpec((1, wmax), lambda b, p, f: (0, 0))],
        out_specs=spec,
        scratch_shapes=[scr, scrp, scrp, scrp, scr, scr, scr, scrp, scrp, scrp,
                        pltpu.VMEM((3, tl, wmax), _F32)],
    )
    return pl.pallas_call(
        functools.partial(_dil_kernel, tl=tl),
        grid_spec=grid_spec,
        out_shape=jax.ShapeDtypeStruct((B, P, S, LANES), _BF),
        compiler_params=_cparams(("parallel", "parallel")),
        name="dil_attn",
    )(fast, q, k, v, jnp.full((1, wmax), shift, _F32))


def _moe_kernel(te_ref, tv_ref, src_hbm, dst_hbm, h_hbm, wg_ref, wu_ref, wd_ref, out_hbm,
                xbuf, xbf, a_scr, ybuf, src_s0, src_s1, dst_s0, dst_s1, gsem, ssem, isem,
                *, tf, dump0, n_dump):
    i = pl.program_id(0)
    nt = pl.num_programs(0)
    tm = xbf.shape[0]
    slot = i % 2

    def is_valid(t):
        return jnp.logical_and(jnp.logical_and(t >= 0, t < nt), tv_ref[jnp.clip(t, 0, nt - 1)] == 1)

    def both(a, b):
        return jnp.logical_and(a, b)

    valid = is_valid(i)
    prev_valid = is_valid(i - 1)

    src_bufs = (src_s0, src_s1)
    dst_bufs = (dst_s0, dst_s1)

    def src_copy(tile, par):
        return pltpu.make_async_copy(src_hbm.at[tile], src_bufs[par], isem.at[par])

    def dst_copy(tile, par):
        return pltpu.make_async_copy(dst_hbm.at[tile], dst_bufs[par], isem.at[2 + par])

    def tile_at(first):
        return pl.ds(pl.multiple_of(first, SUBLANES), SUBLANES)

    def gather_row(r, par):
        return pltpu.make_async_copy(h_hbm.at[tile_at(src_bufs[par][r])],
                                     xbuf.at[par, tile_at(r * SUBLANES)], gsem.at[par])

    def scatter_row(r, par):
        return pltpu.make_async_copy(ybuf.at[par, tile_at(r * SUBLANES)],
                                     out_hbm.at[tile_at(dst_bufs[par][r])], ssem.at[par])

    def gather_wait(par):
        return pltpu.make_async_copy(h_hbm.at[pl.ds(0, tm * SUBLANES)], xbuf.at[par], gsem.at[par])

    def scatter_wait(par):
        return pltpu.make_async_copy(ybuf.at[par], out_hbm.at[pl.ds(0, tm * SUBLANES)], ssem.at[par])

    def for_rows(fn):
        def body(r, c):
            fn(r)
            return c
        lax.fori_loop(0, tm, body, 0, unroll=8)

    @pl.when(i == 0)
    def _():
        ybuf[...] = jnp.zeros_like(ybuf)
        for k in range(n_dump):
            pltpu.make_async_copy(ybuf.at[1], out_hbm.at[pl.ds((dump0 + k * tm) * SUBLANES, tm * SUBLANES)],
                                  ssem.at[1]).start()
        for k in range(n_dump):
            scatter_wait(1).wait()

    @pl.when(both(i == 0, valid))
    def _():
        src_copy(0, 0).start()
        src_copy(0, 0).wait()
        for_rows(lambda r: gather_row(r, 0).start())
        src_copy(1, 1).start()

    def phase1(par, with_scatter):
        xbf[...] = _tiles_to_rows(xbuf.at[par], tm).astype(_BF)
        n_chunk = -(-wg_ref.shape[1] // tf)
        per = -(-tm // n_chunk)

        def after(c):
            for r in range(c * per, min((c + 1) * per, tm)):
                gather_row(r, 1 - par).start()
                if with_scatter:
                    scatter_row(r, 1 - par).start()

        _swiglu_hidden(xbf[...], wg_ref, wu_ref, a_scr, tf, after)

    for par in range(2):
        here = slot == par

        @pl.when(both(valid, here))
        def _(par=par):
            dst_copy(i, par).start()
            src_copy(i + 1, 1 - par).wait()

        @pl.when(both(prev_valid, here))
        def _(par=par):
            dst_copy(i - 1, 1 - par).wait()

        @pl.when(both(jnp.logical_or(both(i == 0, valid), prev_valid), here))
        def _(par=par):
            gather_wait(par).wait()

        @pl.when(both(both(valid, prev_valid), here))
        def _(par=par):
            phase1(par, True)

        if par == 0:
            @pl.when(both(both(valid, jnp.logical_not(prev_valid)), here))
            def _():
                phase1(0, False)

        @pl.when(both(both(jnp.logical_not(valid), prev_valid), here))
        def _(par=par):
            for_rows(lambda r: scatter_row(r, 1 - par).start())

        @pl.when(both(is_valid(i - 2), here))
        def _(par=par):
            scatter_wait(par).wait()

        @pl.when(both(valid, here))
        def _(par=par):
            _rows_to_tiles(ybuf.at[par], jnp.dot(a_scr[...], wd_ref[...], preferred_element_type=_F32))

        @pl.when(both(is_valid(i + 1), here))
        def _(par=par):
            src_copy(i + 2, par).start()


def _moe(h, tile_e, tile_v, src, dst, wg, wu, wd, n_out_rows, tm, tf):
    T = h.shape[0] // SUBLANES
    D = wg.shape[1]
    nt = tile_e.shape[0]
    F = wg.shape[2]
    dump0 = T * TOP_K
    once = pl.Buffered(1)
    grid_spec = pltpu.PrefetchScalarGridSpec(
        num_scalar_prefetch=2,
        grid=(nt,),
        in_specs=[pl.BlockSpec(memory_space=pl.ANY), pl.BlockSpec(memory_space=pl.ANY),
                  pl.BlockSpec(memory_space=pl.ANY),
                  pl.BlockSpec((None, D, F), lambda i, te, tv: (te[i], 0, 0), pipeline_mode=once),
                  pl.BlockSpec((None, D, F), lambda i, te, tv: (te[i], 0, 0), pipeline_mode=once),
                  pl.BlockSpec((None, F, D), lambda i, te, tv: (te[i], 0, 0), pipeline_mode=once)],
        out_specs=pl.BlockSpec(memory_space=pl.ANY),
        scratch_shapes=[pltpu.VMEM((2, tm * SUBLANES, LANES), _F32), pltpu.VMEM((tm, D), _BF),
                        pltpu.VMEM((tm, F), _BF), pltpu.VMEM((2, tm * SUBLANES, LANES), _F32),
                        pltpu.SMEM((tm,), jnp.int32), pltpu.SMEM((tm,), jnp.int32),
                        pltpu.SMEM((tm,), jnp.int32), pltpu.SMEM((tm,), jnp.int32),
                        pltpu.SemaphoreType.DMA((2,)), pltpu.SemaphoreType.DMA((2,)),
                        pltpu.SemaphoreType.DMA((4,))],
    )
    return pl.pallas_call(
        functools.partial(_moe_kernel, tf=tf, dump0=dump0, n_dump=(n_out_rows - dump0) // tm),
        grid_spec=grid_spec,
        out_shape=jax.ShapeDtypeStruct((n_out_rows * SUBLANES, LANES), _F32),
        compiler_params=_cparams(("arbitrary",)),
        name="moe_experts",
    )(tile_e, tile_v, src, dst, h, wg, wu, wd)


def _route_plan(route, T, tm):
    A = T * TOP_K
    e_flat = route[:, :TOP_K].astype(jnp.int32).reshape(A)
    order = jnp.argsort(e_flat, stable=True).astype(jnp.int32)
    counts = jnp.sum((e_flat[:, None] == jnp.arange(N_EXPERTS, dtype=jnp.int32)[None, :]).astype(jnp.int32), axis=0)
    starts = jnp.cumsum(counts) - counts
    pcounts = (counts + tm - 1) // tm * tm
    pends = jnp.cumsum(pcounts)
    pstarts = pends - pcounts
    nt = A // tm + N_EXPERTS + 2
    tile0 = jnp.arange(nt, dtype=jnp.int32) * tm
    tile_v = (tile0 < pends[-1]).astype(jnp.int32)
    last_valid = jnp.maximum(pends[-1] // tm - 1, 0)
    tile_e_raw = jnp.minimum(jnp.searchsorted(pends, tile0, side="right"), N_EXPERTS - 1).astype(jnp.int32)
    tile_e = jnp.where(tile_v == 1, tile_e_raw, tile_e_raw[last_valid])
    r = jnp.arange(nt * tm, dtype=jnp.int32)
    e_r = jnp.repeat(tile_e, tm)
    within = r - pstarts[e_r]
    ok = jnp.logical_and(within < counts[e_r], jnp.repeat(tile_v, tm) == 1)
    a = order[jnp.clip(starts[e_r] + within, 0, A - 1)]
    src = jnp.where(ok, a // TOP_K, 0).astype(jnp.int32)
    dump = A + e_r * tm + jnp.clip(within - counts[e_r], 0, tm - 1)
    dst = jnp.where(ok, (a % TOP_K) * T + a // TOP_K, dump).astype(jnp.int32)
    return (tile_e, tile_v, (src * SUBLANES).reshape(nt, tm), (dst * SUBLANES).reshape(nt, tm),
            A + N_EXPERTS * tm)


def _combine_kernel(x_ref, y1_ref, y2_ref, r_ref, mod_ref, o_ref):
    tm = x_ref.shape[0]
    y = _tiles_to_rows(y1_ref, tm) * r_ref[:, 2:3] + _tiles_to_rows(y2_ref, tm) * r_ref[:, 3:4]
    o_ref[...] = x_ref[...] + mod_ref[5:6, :] * y


def _combine(x, y2, route, mod, tm):
    B, S, D = x.shape
    T = B * S
    per_b = S // tm
    out = pl.pallas_call(
        _combine_kernel,
        grid=(T // tm,),
        in_specs=[pl.BlockSpec((tm, D), lambda i: (i, 0)),
                  pl.BlockSpec((tm * SUBLANES, LANES), lambda i: (i, 0)),
                  pl.BlockSpec((tm * SUBLANES, LANES), lambda i: (T // tm + i, 0)),
                  pl.BlockSpec((tm, LANES), lambda i: (i, 0)),
                  pl.BlockSpec((None, 6, D), lambda i: (i // per_b, 0, 0))],
        out_specs=pl.BlockSpec((tm, D), lambda i: (i, 0)),
        out_shape=jax.ShapeDtypeStruct((T, D), _F32),
        compiler_params=_cparams(("parallel",)),
        name="moe_combine",
    )(x.reshape(T, D), y2, y2, route.reshape(T, LANES), mod)
    return out.reshape(B, S, D)


def _pad_cols(a, w):
    return jnp.pad(a, ((0, 0), (0, w - a.shape[1])))


def _rope_tables_even(S):
    pos = jnp.arange(S, dtype=jnp.int32)
    inv = ROPE_THETA ** (-jnp.arange(0, 32, 2, dtype=_F32) / 32)
    def cs(p):
        ang = p.astype(_F32)[:, None] * inv[None, :]
        return jnp.cos(ang), jnp.sin(ang)
    one = lambda w: jnp.ones((S, w), _F32)
    zero = lambda w: jnp.zeros((S, w), _F32)
    c, s = cs(pos)
    ca = jnp.concatenate([c, one(48), c, one(48)], 1)
    sa = jnp.concatenate([-s, zero(48), s, zero(48)], 1)
    cr, sr = cs(pos // GRID_W)
    cc, sc = cs(pos % GRID_W)
    cb = jnp.concatenate([cr, cc, one(32), cr, cc, one(32)], 1)
    sb = jnp.concatenate([-sr, -sc, zero(32), sr, sc, zero(32)], 1)
    return jnp.concatenate([ca, sa, cb, sb], 1)


def _slot_maps():
    r = MLA_ROPE // 2
    mla = ([MLA_NOPE + i for i in range(r)] + list(range(0, 64 - r))
           + [MLA_NOPE + r + i for i in range(r)] + list(range(64 - r, MLA_NOPE)))
    mla += [-1] * (LANES - len(mla))
    q = HEAD_DIM // 4
    gqa = (list(range(0, q)) + list(range(2 * q, 3 * q)) + [-1] * (64 - 2 * q)
           + list(range(q, 2 * q)) + list(range(3 * q, 4 * q)) + [-1] * (64 - 2 * q))
    return mla, gqa


def _to_slots(a, lane_map):
    idx = jnp.asarray([max(i, 0) for i in lane_map], jnp.int32)
    keep = jnp.asarray([1.0 if i >= 0 else 0.0 for i in lane_map], a.dtype)
    return jnp.take(a, idx, axis=-1) * keep


def _rope_tables_odd(S):
    pos = jnp.arange(S, dtype=_F32)
    inv = ROPE_THETA ** (-jnp.arange(0, HEAD_DIM, 2, dtype=_F32) / HEAD_DIM)
    ang = pos[:, None] * inv[None, :]
    c, s = jnp.cos(ang), jnp.sin(ang)
    return jnp.concatenate([c, c, c, c, -s, -s, s, s], 1)


def _tiles(S):
    return dict(tm_pre=min(S, 512), tq=min(S, 2048), tm_o=min(S, 512),
                tm_ffn=min(S, 512), tf_ffn=512, tm_qkv=min(S, 512),
                tl=128, tm_moe=min(S, 512), tf_moe=512, tm_c=min(S, 512))


def kernel(x, c, ada_even_w, ada_even_b, norm_even_mix, norm_even_ffn, even_w_in, mla_q_norm, mla_w_uq, mla_kv_norm, mla_w_ukv, mla_q_gain, mla_k_gain, gqa_q_gain, gqa_k_gain, even_w_out, ffn_w_gate, ffn_w_up, ffn_w_down, ada_odd_w, ada_odd_b, norm_odd_mix, norm_odd_ffn, dil_w_qkv, dil_q_gain, dil_k_gain, dil_w_out, moe_router, moe_w_gate, moe_w_up, moe_w_down):
    B, S, D = x.shape
    T = B * S
    cfg = _tiles(S)

    mod_e = _ada_mod(c, ada_even_w[0], ada_even_b[0]).reshape(B, 6, D)
    mod_o = _ada_mod(c, ada_odd_w[0], ada_odd_b[0]).reshape(B, 6, D)

    w = even_w_in[0]
    sp = [MLA_Q_RANK, MLA_Q_RANK + MLA_KV_RANK, MLA_Q_RANK + MLA_KV_RANK + MLA_ROPE]
    sp.append(sp[-1] + GQA_HEADS * HEAD_DIM)
    sp.append(sp[-1] + GQA_KV_HEADS * HEAD_DIM)
    w_cq, w_ckv, w_kpe = w[:, :sp[0]], w[:, sp[0]:sp[1]], w[:, sp[1]:sp[2]]
    w_qb, w_kb, w_vb = w[:, sp[2]:sp[3]], w[:, sp[3]:sp[4]], w[:, sp[4]:]
    mla_map, gqa_map = _slot_maps()
    na = MLA_NOPE + MLA_ROPE
    nope_only = [i if 0 <= i < MLA_NOPE else -1 for i in mla_map]
    rope_only = [i - MLA_NOPE if i >= MLA_NOPE else -1 for i in mla_map]
    gslots = lambda a, n: _to_slots(a.reshape(D, n, HEAD_DIM), gqa_map).reshape(D, n * LANES)
    w_in = jnp.concatenate([w_cq, w_ckv, _to_slots(w_kpe, rope_only), gslots(w_qb, GQA_HEADS),
                            gslots(w_kb, GQA_KV_HEADS)], axis=1).astype(_BF)
    w_uq = _to_slots(mla_w_uq[0].reshape(MLA_Q_RANK, MLA_HEADS, na), mla_map)
    w_uq = w_uq.reshape(MLA_Q_RANK, MLA_HEADS * LANES).astype(_BF)
    w_ukv = mla_w_ukv[0].reshape(MLA_KV_RANK, MLA_HEADS, MLA_NOPE + MLA_V)
    w_uk = _to_slots(w_ukv[:, :, :MLA_NOPE], nope_only).reshape(MLA_KV_RANK, MLA_HEADS * LANES).astype(_BF)
    w_uvt = w_ukv[:, :, MLA_NOPE:].reshape(MLA_KV_RANK, MLA_HEADS * MLA_V).T.astype(_BF)
    w_vbt = w_vb.T.astype(_BF)
    g_rows = [_to_slots(mla_q_gain[0], mla_map) * (na ** -0.5 * LOG2E), _to_slots(mla_k_gain[0], mla_map),
              _to_slots(gqa_q_gain[0], gqa_map) * (HEAD_DIM ** -0.5 * LOG2E), _to_slots(gqa_k_gain[0], gqa_map)]
    bound_a = na * jnp.max(jnp.abs(g_rows[0])) * jnp.max(jnp.abs(g_rows[1]))
    bound_b = HEAD_DIM * jnp.max(jnp.abs(g_rows[2])) * jnp.max(jnp.abs(g_rows[3]))
    fast_a, fast_b = bound_a <= SCORE_BOUND_MAX, bound_b <= SCORE_BOUND_MAX
    last_lane = jnp.zeros((LANES,), _F32).at[LANES - 1].set(1.0)
    g_rows += [last_lane, last_lane * jnp.where(fast_a, -bound_a, 0.0),
               last_lane, last_lane * jnp.where(fast_b, -bound_b, 0.0)]
    gains_e = jnp.stack(g_rows)
    rope_e = _rope_tables_even(S)
    q_all, k_all, vt_all = _pre_even(
        x, mod_e, norm_even_mix[0].reshape(1, D), w_in, mla_q_norm[0].reshape(1, -1), w_uq,
        mla_kv_norm[0].reshape(1, -1), w_uk, w_uvt, w_vbt, gains_e, rope_e, cfg["tm_pre"])
    as_flag = lambda f: f.astype(jnp.int32).reshape(1)
    o_a = _attention(as_flag(fast_a), q_all, k_all, vt_all, q_off=0, k_off=0, shared_kv=False, tq=cfg["tq"])
    o_b = _attention(as_flag(fast_b), q_all, k_all, vt_all, q_off=MLA_HEADS, k_off=MLA_HEADS,
                     shared_kv=True, tq=cfg["tq"])
    x1, h1 = _oproj(x, mod_e, norm_even_ffn[0].reshape(1, D), even_w_out[0].astype(_BF), [o_a, o_b],
                    cfg["tm_o"], _BF)
    x2, h2 = _ffn(x1, h1, mod_e, mod_o, norm_odd_mix[0].reshape(1, D), ffn_w_gate[0].astype(_BF),
                  ffn_w_up[0].astype(_BF), ffn_w_down[0].astype(_BF), cfg["tm_ffn"], cfg["tf_ffn"])

    hh = HEAD_DIM // 2
    pair = lambda v: jnp.concatenate([v[:hh], v[:hh], v[hh:], v[hh:]])
    gains_o = jnp.stack([pair(dil_q_gain[0]) * (HEAD_DIM ** -0.5 * LOG2E), pair(dil_k_gain[0])])
    wq, wk, wv = jnp.split(dil_w_qkv[0], 3, axis=1)
    perm = lambda w: w.reshape(D, D // LANES, 2, 2, hh).transpose(0, 1, 3, 2, 4).reshape(D, D)
    w_qkv = jnp.concatenate([perm(wq), perm(wk), wv], axis=1).astype(_BF)
    qd, kd, vd = _qkv(h2, w_qkv, gains_o, _rope_tables_odd(S), cfg["tm_qkv"])
    bound_d = HEAD_DIM * jnp.max(jnp.abs(gains_o[0])) * jnp.max(jnp.abs(gains_o[1]))
    fast_d = bound_d <= SCORE_BOUND_MAX
    o_d = _dilated(as_flag(fast_d), jnp.where(fast_d, -bound_d, 0.0), qd, kd, vd, cfg["tl"])
    r32 = _pad_cols(moe_router[0], LANES)
    r_hi = r32.astype(_BF)
    r_lo = (r32 - r_hi.astype(_F32)).astype(_BF)
    x3, h3, route = _oproj(x2, mod_o, norm_odd_ffn[0].reshape(1, D), dil_w_out[0].astype(_BF), [o_d],
                           cfg["tm_o"], _F32, router=jnp.stack([r_hi, r_lo]))
    tm = cfg["tm_moe"]
    tile_e, tile_v, src, dst, n_rows = _route_plan(route.reshape(T, LANES), T, tm)
    y2 = _moe(h3, tile_e, tile_v, src, dst, moe_w_gate[0].astype(_BF),
              moe_w_up[0].astype(_BF), moe_w_down[0].astype(_BF), n_rows, tm, cfg["tf_moe"])
    return _combine(x3, y2, route, mod_o, cfg["tm_c"])
```

```python
import functools
import math

import jax
import jax.numpy as jnp
from jax import lax
from jax.experimental import pallas as pl
from jax.experimental.pallas import tpu as pltpu

_BF = jnp.bfloat16
_F32 = jnp.float32

GRID_W = 64
HEAD_DIM = 64
ROPE_THETA = 10000.0
EPS = 1e-6
MLA_HEADS = 8
MLA_Q_RANK = 256
MLA_KV_RANK = 128
MLA_NOPE = 64
MLA_ROPE = 32
MLA_V = 64
GQA_HEADS = 8
GQA_KV_HEADS = 2
DIL_PATTERNS = ((128, 1), (512, 4), (2048, 16))
N_EXPERTS = 8
TOP_K = 2
NEG = -1e30
LOG2E = math.log2(math.e)

LANES = 128
VMEM_LIMIT = 56 * 1024 * 1024


def _cparams(sem, vmem=VMEM_LIMIT):
    return pltpu.CompilerParams(dimension_semantics=sem, vmem_limit_bytes=vmem)


def _silu(x):
    return x / (1.0 + jnp.exp(-x))


def _modulate(x, g, shift, scale):
    ms = jnp.mean(x * x, axis=-1, keepdims=True)
    return x * lax.rsqrt(ms + EPS) * g * (1.0 + scale) + shift


def _norm(v, n, g):
    return v * lax.rsqrt(jnp.sum(v * v, axis=-1, keepdims=True) * (1.0 / n) + EPS) * g


SUBLANES = 8


def _rows_to_tiles(ref, x):
    tm = x.shape[0]
    for c in range(SUBLANES):
        ref[pl.ds(c, tm, stride=SUBLANES), :] = x[:, c * LANES:(c + 1) * LANES]


def _tiles_to_rows(ref, tm):
    return jnp.concatenate([ref[pl.ds(c, tm, stride=SUBLANES), :] for c in range(SUBLANES)], axis=1)


def _rope(x, c, sa, sb, k):
    return x * c + pltpu.roll(x, LANES - k, 1) * sa + pltpu.roll(x, k, 1) * sb


def _mod_kernel(c_ref, w_ref, b_ref, o_ref):
    sc = _silu(c_ref[...])
    o_ref[...] = jnp.dot(sc.astype(_BF), w_ref[...].astype(_BF),
                         preferred_element_type=_F32) + b_ref[...]


def _ada_mod(c, w, b):
    B, D = c.shape
    N = w.shape[1]
    tn = min(N, 1536)
    return pl.pallas_call(
        _mod_kernel,
        grid=(N // tn,),
        in_specs=[pl.BlockSpec((B, D), lambda j: (0, 0)),
                  pl.BlockSpec((D, tn), lambda j: (0, j)),
                  pl.BlockSpec((1, tn), lambda j: (0, j))],
        out_specs=pl.BlockSpec((B, tn), lambda j: (0, j)),
        out_shape=jax.ShapeDtypeStruct((B, N), _F32),
        compiler_params=_cparams(("arbitrary",)),
        name="ada_mod",
    )(c, w, b.reshape(1, N))


def _pre_even_kernel(x_ref, mod_ref, g_ref, w_in_ref, qn_ref, w_uq_ref, kvn_ref, w_uk_ref, w_uvt_ref,
                     w_vbt_ref, gains_ref, rope_ref, q_ref, k_ref, vt_ref, nq_scr, nk_scr):
    h = _modulate(x_ref[...], g_ref[...], mod_ref[0:1, :], mod_ref[1:2, :]).astype(_BF)
    y = jnp.dot(h, w_in_ref[...], preferred_element_type=_F32)
    cqn = _norm(y[:, 0:256], MLA_Q_RANK, qn_ref[...]).astype(_BF)
    qa = jnp.dot(cqn, w_uq_ref[...], preferred_element_type=_F32)
    ckvn = _norm(y[:, 256:384], MLA_KV_RANK, kvn_ref[...]).astype(_BF)
    kn = jnp.dot(ckvn, w_uk_ref[...], preferred_element_type=_F32)
    kpe = y[:, 384:512]
    nt_dims = (((1,), (1,)), ((), ()))
    vt_a = lax.dot_general(w_uvt_ref[...], ckvn, nt_dims, preferred_element_type=_F32).astype(_BF)
    vt_b = lax.dot_general(w_vbt_ref[...], h, nt_dims, preferred_element_type=_F32).astype(_BF)
    tm = x_ref.shape[0]
    ones_blk = jnp.where(lax.broadcasted_iota(jnp.int32, (LANES - MLA_V, tm), 0) == 0, 1.0, 0.0).astype(_BF)
    for hh in range(MLA_HEADS + GQA_KV_HEADS):
        src = vt_a[hh * MLA_V:(hh + 1) * MLA_V] if hh < MLA_HEADS else \
            vt_b[(hh - MLA_HEADS) * HEAD_DIM:(hh - MLA_HEADS + 1) * HEAD_DIM]
        vt_ref[hh * LANES:hh * LANES + MLA_V, :] = src
        vt_ref[hh * LANES + MLA_V:(hh + 1) * LANES, :] = ones_blk
    ca, sa = rope_ref[:, 0:128], rope_ref[:, 128:256]
    cb, sb = rope_ref[:, 256:384], rope_ref[:, 384:512]
    gqa, gka = gains_ref[0:1, :], gains_ref[1:2, :]
    gqb, gkb = gains_ref[2:3, :], gains_ref[3:4, :]
    na = MLA_NOPE + MLA_ROPE

    for hh in range(MLA_HEADS):
        sl = slice(hh * LANES, (hh + 1) * LANES)
        nq_scr[:, sl] = _norm(qa[:, sl], na, gqa)
        nk_scr[:, sl] = _norm(kn[:, sl] + kpe, na, gka)
    for hh in range(GQA_HEADS):
        src = slice(512 + hh * LANES, 512 + (hh + 1) * LANES)
        dst = slice((MLA_HEADS + hh) * LANES, (MLA_HEADS + hh + 1) * LANES)
        nq_scr[:, dst] = _norm(y[:, src], HEAD_DIM, gqb)
    for g in range(GQA_KV_HEADS):
        src = slice(1536 + g * LANES, 1536 + (g + 1) * LANES)
        dst = slice((MLA_HEADS + g) * LANES, (MLA_HEADS + g + 1) * LANES)
        nk_scr[:, dst] = _norm(y[:, src], HEAD_DIM, gkb)

    @pl.when(pl.program_id(1) >= 0)
    def _():
        def rope(ref, out, n_a, n_all, row):
            for hh in range(n_all):
                sl = slice(hh * LANES, (hh + 1) * LANES)
                c, s = (ca, sa) if hh < n_a else (cb, sb)
                b = gains_ref[row:row + 1, :] if hh < n_a else gains_ref[row + 2:row + 3, :]
                v = ref[:, sl]
                out[:, sl] = (v * c + pltpu.roll(v, 64, 1) * s + b).astype(_BF)

        rope(nq_scr, q_ref, MLA_HEADS, MLA_HEADS + GQA_HEADS, 4)
        rope(nk_scr, k_ref, MLA_HEADS, MLA_HEADS + GQA_KV_HEADS, 5)


def _pre_even(x, mod, g, w_in, qn, w_uq, kvn, w_uk, w_uvt, w_vbt, gains, rope, tm):
    B, S, D = x.shape
    nq = (MLA_HEADS + GQA_HEADS) * LANES
    nk = (MLA_HEADS + GQA_KV_HEADS) * LANES
    nvt = (MLA_HEADS + GQA_KV_HEADS) * LANES
    full = lambda a: pl.BlockSpec(a.shape, lambda b, i: (0,) * a.ndim)
    tok = lambda w: pl.BlockSpec((None, tm, w), lambda b, i: (b, i, 0))
    return pl.pallas_call(
        _pre_even_kernel,
        grid=(B, S // tm),
        in_specs=[tok(D),
                  pl.BlockSpec((None, 6, D), lambda b, i: (b, 0, 0)),
                  full(g), full(w_in), full(qn), full(w_uq), full(kvn), full(w_uk), full(w_uvt),
                  full(w_vbt), full(gains),
                  pl.BlockSpec((tm, rope.shape[1]), lambda b, i: (i, 0))],
        out_specs=[tok(nq), tok(nk), pl.BlockSpec((None, None, nvt, tm), lambda b, i: (b, i, 0, 0))],
        out_shape=[jax.ShapeDtypeStruct((B, S, nq), _BF), jax.ShapeDtypeStruct((B, S, nk), _BF),
                   jax.ShapeDtypeStruct((B, S // tm, nvt, tm), _BF)],
        scratch_shapes=[pltpu.VMEM((tm, nq), _F32), pltpu.VMEM((tm, nk), _F32)],
        compiler_params=_cparams(("parallel", "parallel")),
        name="pre_even",
    )(x, mod, g, w_in, qn, w_uq, kvn, w_uk, w_uvt, w_vbt, gains, rope)


SCORE_BOUND_MAX = 60.0


def _attn_kernel(fast_ref, q_ref, k_ref, vt_ref, o_ref, s_scr, *, shared_kv):
    @pl.when(fast_ref[0] == 1)
    def _():
        _attn_fixed_shift(q_ref, k_ref, vt_ref, o_ref, shared_kv=shared_kv)

    @pl.when(fast_ref[0] != 1)
    def _():
        _attn_running_max(q_ref, k_ref, vt_ref, o_ref, s_scr, shared_kv=shared_kv)


def _attn_fixed_shift(q_ref, k_ref, vt_ref, o_ref, *, shared_kv):
    tq = q_ref.shape[0]
    nk, _, tk = vt_ref.shape
    qs = [q_ref[:, hh * LANES:(hh + 1) * LANES] for hh in range(2)]

    def body(j, carry):
        r0 = pl.multiple_of(j * tk, tk)
        out = []
        for hh in range(2):
            kc = 0 if shared_kv else hh * LANES
            k = k_ref[pl.ds(r0, tk), kc:kc + LANES]
            st = lax.dot_general(k, qs[hh], (((1,), (1,)), ((), ())), preferred_element_type=_F32)
            p = jnp.exp2(st).astype(_BF)
            out.append(carry[hh] + jnp.dot(vt_ref[j, kc:kc + LANES, :], p, preferred_element_type=_F32))
        return tuple(out)

    z = jnp.zeros((LANES, tq), _F32)
    res = lax.fori_loop(0, nk, body, (z, z), unroll=2)
    nv = HEAD_DIM
    halves = [acc[0:nv, :] / acc[nv:nv + 1, :] for acc in res]
    o_ref[...] = jnp.concatenate(halves, axis=0).T.astype(o_ref.dtype)


def _attn_running_max(q_ref, k_ref, vt_ref, o_ref, s_scr, *, shared_kv):
    tq = q_ref.shape[0]
    nk, _, tk = vt_ref.shape
    qs = [q_ref[:, hh * LANES:(hh + 1) * LANES] for hh in range(2)]

    def scores(j, hh):
        r0 = j * tk if isinstance(j, int) else pl.multiple_of(j * tk, tk)
        kc = 0 if shared_kv else hh * LANES
        k = k_ref[pl.ds(r0, tk), kc:kc + LANES]
        return lax.dot_general(k, qs[hh], (((1,), (1,)), ((), ())), preferred_element_type=_F32)

    def put_scores(j, hh, slot):
        st = scores(j, hh)
        s_scr[hh, slot] = st
        return jnp.max(st, axis=0, keepdims=True)

    def step(j, hh, carry, cur, last=False):
        m, mc, acc = carry
        mc_next = mc if last else put_scores(j + 1, hh, 1 - cur)
        st = s_scr[hh, cur]
        m_new = jnp.maximum(m, mc)
        alpha = jnp.exp2(m - m_new)
        p = jnp.exp2(st - m_new).astype(_BF)
        vc = 0 if shared_kv else hh * LANES
        acc = alpha * acc + jnp.dot(vt_ref[j, vc:vc + LANES, :], p, preferred_element_type=_F32)
        return m_new, mc_next, acc

    def body(jj, carry):
        c = list(carry)
        for sub in range(2):
            for hh in range(2):
                c[hh] = step(2 * jj + sub, hh, c[hh], sub)
        return tuple(c)

    init = tuple((jnp.full((1, tq), NEG, _F32), put_scores(0, hh, 0), jnp.zeros((LANES, tq), _F32))
                 for hh in range(2))
    res = lax.fori_loop(0, nk // 2 - 1, body, init)
    res = list(res)
    for sub in range(2):
        for hh in range(2):
            res[hh] = step(nk - 2 + sub, hh, res[hh], sub, last=(sub == 1))
    nv = HEAD_DIM
    halves = [acc[0:nv, :] / acc[nv:nv + 1, :] for (_, _, acc) in res]
    o_ref[...] = jnp.concatenate(halves, axis=0).T.astype(o_ref.dtype)


def _attention(fast, q, k, vt, *, q_off, k_off, shared_kv, tq):
    B, S, _ = q.shape
    nk, _, tk = vt.shape[1:]
    assert nk % 2 == 0 and nk >= 2
    npairs = 4
    if shared_kv:
        kspec = pl.BlockSpec((None, S, LANES), lambda b, p, i, f: (b, 0, k_off + p // 2))
        vspec = pl.BlockSpec((None, nk, LANES, tk), lambda b, p, i, f: (b, 0, k_off + p // 2, 0))
    else:
        kspec = pl.BlockSpec((None, S, 2 * LANES), lambda b, p, i, f: (b, 0, k_off // 2 + p))
        vspec = pl.BlockSpec((None, nk, 2 * LANES, tk), lambda b, p, i, f: (b, 0, k_off // 2 + p, 0))
    grid_spec = pltpu.PrefetchScalarGridSpec(
        num_scalar_prefetch=1,
        grid=(B, npairs, S // tq),
        in_specs=[pl.BlockSpec((None, tq, 2 * LANES), lambda b, p, i, f: (b, i, q_off // 2 + p)),
                  kspec, vspec],
        out_specs=pl.BlockSpec((None, tq, LANES), lambda b, p, i, f: (b, i, p)),
        scratch_shapes=[pltpu.VMEM((2, 2, tk, tq), _F32)],
    )
    return pl.pallas_call(
        functools.partial(_attn_kernel, shared_kv=shared_kv),
        grid_spec=grid_spec,
        out_shape=jax.ShapeDtypeStruct((B, S, npairs * LANES), _BF),
        compiler_params=_cparams(("parallel", "parallel", "parallel")),
        name="attn_gqa" if shared_kv else "attn_mla",
    )(fast, q, k, vt)


def _oproj_kernel(*refs, n_in, route):
    x_ref, mod_ref, g_ref, w_ref = refs[0], refs[1], refs[2], refs[3]
    o_refs = refs[4:4 + n_in]
    pos = 4 + n_in
    if route:
        r_ref = refs[pos]
        pos += 1
    x_out, h_out = refs[pos], refs[pos + 1]
    y = None
    off = 0
    for o_ref in o_refs:
        if len(o_ref.shape) == 3:
            o = jnp.concatenate([o_ref[p] for p in range(o_ref.shape[0])], axis=1)
        else:
            o = o_ref[...]
        w = o.shape[1]
        t = jnp.dot(o, w_ref[off:off + w, :], preferred_element_type=_F32)
        y = t if y is None else y + t
        off += w
    x1 = x_ref[...] + mod_ref[2:3, :] * y
    x_out[...] = x1
    h = _modulate(x1, g_ref[...], mod_ref[3:4, :], mod_ref[4:5, :])
    if route:
        _rows_to_tiles(h_out, h)
    else:
        h_out[...] = h.astype(h_out.dtype)
    if route:
        route_out = refs[pos + 2]
        hh = h.astype(_BF)
        hl = (h - hh.astype(_F32)).astype(_BF)
        rh, rl = r_ref[0], r_ref[1]
        logits = (jnp.dot(hh, rh, preferred_element_type=_F32)
                  + jnp.dot(hh, rl, preferred_element_type=_F32)
                  + jnp.dot(hl, rh, preferred_element_type=_F32))
        tm = logits.shape[0]
        lane = lax.broadcasted_iota(jnp.int32, (tm, LANES), 1)
        lg = jnp.where(lane < N_EXPERTS, logits, NEG)
        m1 = jnp.max(lg, axis=-1, keepdims=True)
        lanef = lane.astype(_F32)
        i1 = jnp.min(jnp.where(lg == m1, lanef, float(LANES)), axis=-1, keepdims=True)
        lg2 = jnp.where(lanef == i1, NEG, lg)
        m2 = jnp.max(lg2, axis=-1, keepdims=True)
        i2 = jnp.min(jnp.where(lg2 == m2, lanef, float(LANES)), axis=-1, keepdims=True)
        e = jnp.exp(m2 - m1)
        g1 = 1.0 / (1.0 + e)
        g2 = e / (1.0 + e)
        route_out[...] = jnp.where(lane == 0, i1,
                                   jnp.where(lane == 1, i2,
                                             jnp.where(lane == 2, g1, jnp.where(lane == 3, g2, 0.0))))


def _oproj(x, mod, g, w, o_list, tm, h_dtype, router=None):
    B, S, D = x.shape
    n_in = len(o_list)
    route = router is not None
    tok = lambda wd: pl.BlockSpec((None, tm, wd), lambda b, i: (b, i, 0))
    in_specs = [tok(D), pl.BlockSpec((None, 6, D), lambda b, i: (b, 0, 0)),
                pl.BlockSpec((1, D), lambda b, i: (0, 0)),
                pl.BlockSpec(w.shape, lambda b, i: (0, 0))]
    for o in o_list:
        if o.ndim == 4:
            in_specs.append(pl.BlockSpec((None, o.shape[1], tm, LANES), lambda b, i: (b, 0, i, 0)))
        else:
            in_specs.append(tok(o.shape[2]))
    args = [x, mod, g, w] + list(o_list)
    out_specs = [tok(D), tok(D)]
    out_shape = [jax.ShapeDtypeStruct((B, S, D), _F32), jax.ShapeDtypeStruct((B, S, D), h_dtype)]
    if route:
        assert D == SUBLANES * LANES
        per_b = S // tm
        out_specs[1] = pl.BlockSpec((tm * SUBLANES, LANES), lambda b, i: (b * per_b + i, 0))
        out_shape[1] = jax.ShapeDtypeStruct((B * S * SUBLANES, LANES), _F32)
        in_specs.append(pl.BlockSpec(router.shape, lambda b, i: (0, 0, 0)))
        args.append(router)
        out_specs.append(tok(LANES))
        out_shape.append(jax.ShapeDtypeStruct((B, S, LANES), _F32))
    return pl.pallas_call(
        functools.partial(_oproj_kernel, n_in=n_in, route=route),
        grid=(B, S // tm),
        in_specs=in_specs, out_specs=out_specs, out_shape=out_shape,
        compiler_params=_cparams(("parallel", "parallel")),
        name="oproj_route" if route else "oproj",
    )(*args)


def _swiglu_hidden(x, wg_ref, wu_ref, a_scr, tf, after_chunk=None):
    F = wg_ref.shape[1]
    for c0 in range(0, F, tf):
        c1 = min(c0 + tf, F)
        g = jnp.dot(x, wg_ref[:, c0:c1], preferred_element_type=_F32)
        u = jnp.dot(x, wu_ref[:, c0:c1], preferred_element_type=_F32)
        a_scr[:, c0:c1] = (_silu(g) * u).astype(_BF)
        if after_chunk is not None:
            after_chunk(c0 // tf)


def _ffn_kernel(x_ref, oa_ref, ob_ref, wo_ref, mod_ref, g_ref, modn_ref, gn_ref, wg_ref, wu_ref, wd_ref,
                x_out, h_out, a_scr, *, tf):
    wa = oa_ref.shape[1]
    y = (jnp.dot(oa_ref[...], wo_ref[0:wa, :], preferred_element_type=_F32)
         + jnp.dot(ob_ref[...], wo_ref[wa:, :], preferred_element_type=_F32))
    x1 = x_ref[...] + mod_ref[2:3, :] * y
    h = _modulate(x1, g_ref[...], mod_ref[3:4, :], mod_ref[4:5, :]).astype(_BF)
    _swiglu_hidden(h, wg_ref, wu_ref, a_scr, tf)
    y = jnp.dot(a_scr[...], wd_ref[...], preferred_element_type=_F32)
    x2 = x1 + mod_ref[5:6, :] * y
    x_out[...] = x2
    h_out[...] = _modulate(x2, gn_ref[...], modn_ref[0:1, :], modn_ref[1:2, :]).astype(h_out.dtype)


def _ffn(x, o_a, o_b, w_out, mod, g, modn, gn, wg, wu, wd, tm, tf):
    B, S, D = x.shape
    T = B * S
    F = wg.shape[1]
    per_b = S // tm
    flat = lambda a: a.reshape(T, a.shape[2])
    tok = lambda w: pl.BlockSpec((tm, w), lambda i: (i, 0))
    modspec = pl.BlockSpec((None, 6, D), lambda i: (i // per_b, 0, 0))
    row = pl.BlockSpec((1, D), lambda i: (0, 0))
    once = pl.Buffered(1)
    res = lambda a: pl.BlockSpec(a.shape, lambda i: (0, 0), pipeline_mode=once)
    xo, ho = pl.pallas_call(
        functools.partial(_ffn_kernel, tf=tf),
        grid=(T // tm,),
        in_specs=[tok(D), tok(o_a.shape[2]), tok(o_b.shape[2]), res(w_out), modspec, row, modspec, row,
                  res(wg), res(wu), res(wd)],
        out_specs=[tok(D), tok(D)],
        out_shape=[jax.ShapeDtypeStruct((T, D), _F32), jax.ShapeDtypeStruct((T, D), _BF)],
        scratch_shapes=[pltpu.VMEM((tm, F), _BF)],
        compiler_params=_cparams(("parallel",)),
        name="ffn_dense",
    )(flat(x), flat(o_a), flat(o_b), w_out, mod, g, modn, gn, wg, wu, wd)
    return xo.reshape(B, S, D), ho.reshape(B, S, D)


def _qkv_kernel(h_ref, w_ref, gains_ref, rope_ref, q_ref, k_ref, v_ref, n_scr):
    tm, D = h_ref.shape
    h = h_ref[...]
    lane = lax.broadcasted_iota(jnp.int32, (tm, LANES), 1)
    head_a = (lane & 63) < 32

    def head_pair_norm(v, g):
        sq = v * v
        s_a = jnp.sum(jnp.where(head_a, sq, 0.0), axis=-1, keepdims=True)
        s_b = jnp.sum(jnp.where(head_a, 0.0, sq), axis=-1, keepdims=True)
        r = lax.rsqrt(jnp.where(head_a, s_a, s_b) * (1.0 / HEAD_DIM) + EPS)
        return v * r * g

    y = jnp.dot(h, w_ref[...], preferred_element_type=_F32)
    for p in range(D // LANES):
        sl = slice(p * LANES, (p + 1) * LANES)
        n_scr[:, sl] = head_pair_norm(y[:, sl], gains_ref[0:1, :])
        n_scr[:, D + p * LANES:D + (p + 1) * LANES] = head_pair_norm(
            y[:, D + p * LANES:D + (p + 1) * LANES], gains_ref[1:2, :])
        v_ref[p] = y[:, 2 * D + p * LANES:2 * D + (p + 1) * LANES].astype(_BF)

    @pl.when(pl.program_id(1) >= 0)
    def _():
        c, s = rope_ref[:, 0:128], rope_ref[:, 128:256]
        for p in range(D // LANES):
            vq = n_scr[:, p * LANES:(p + 1) * LANES]
            vk = n_scr[:, D + p * LANES:D + (p + 1) * LANES]
            q_ref[p] = (vq * c + pltpu.roll(vq, 64, 1) * s).astype(_BF)
            k_ref[p] = (vk * c + pltpu.roll(vk, 64, 1) * s).astype(_BF)


def _qkv(h, w, gains, rope, tm):
    B, S, D = h.shape
    tok = pl.BlockSpec((None, tm, D), lambda b, i: (b, i, 0))
    hp = pl.BlockSpec((None, D // LANES, tm, LANES), lambda b, i: (b, 0, i, 0))
    sds = jax.ShapeDtypeStruct((B, D // LANES, S, LANES), _BF)
    return pl.pallas_call(
        _qkv_kernel,
        grid=(B, S // tm),
        in_specs=[tok, pl.BlockSpec(w.shape, lambda b, i: (0, 0)),
                  pl.BlockSpec((2, LANES), lambda b, i: (0, 0)),
                  pl.BlockSpec((tm, 2 * LANES), lambda b, i: (i, 0))],
        out_specs=[hp, hp, hp],
        out_shape=[sds, sds, sds],
        scratch_shapes=[pltpu.VMEM((tm, 2 * D), _F32)],
        compiler_params=_cparams(("parallel", "parallel")),
        name="qkv_dil",
    )(h, w, gains, rope)


DIL_GROUP = 4


def _dil_kernel(fast_ref, q_ref, k_ref, v_ref, shift_ref, o_ref, tmp, qp, kp, vp, acc, mm, ll, accp, mmp,
                llp, bias_scr, *, tl):
    @pl.when(fast_ref[0] == 1)
    def _():
        _dil_body(q_ref, k_ref, v_ref, shift_ref, o_ref, tmp, qp, kp, vp, acc, mm, ll, accp, mmp, llp,
                  bias_scr, tl=tl, fixed_shift=True)

    @pl.when(fast_ref[0] != 1)
    def _():
        _dil_body(q_ref, k_ref, v_ref, shift_ref, o_ref, tmp, qp, kp, vp, acc, mm, ll, accp, mmp, llp,
                  bias_scr, tl=tl, fixed_shift=False)


def _dil_body(q_ref, k_ref, v_ref, shift_ref, o_ref, tmp, qp, kp, vp, acc, mm, ll, accp, mmp, llp,
              bias_scr, *, tl, fixed_shift):
    S = q_ref.shape[0]
    ng = S // DIL_GROUP
    pitch = DIL_GROUP + 1

    def spread(dst, src):
        for j in range(DIL_GROUP):
            dst[pl.ds(j, ng, stride=pitch), :] = src[pl.ds(j, ng, stride=DIL_GROUP), :]

    for src, dst in ((q_ref, qp), (k_ref, kp), (v_ref, vp)):
        tmp[...] = src[...].astype(_F32)
        spread(dst, tmp)
    lane = lax.broadcasted_iota(jnp.int32, (tl, LANES), 1)
    lo = lane < 64
    head_a = (lane & 63) < 32
    for bi, (window, d) in enumerate(DIL_PATTERNS):
        L = S // d
        half = window // (2 * d)
        t = min(tl, L)
        W = min(t + 2 * half, L)
        nt = L // t
        if d > 1:
            assert d % DIL_GROUP == 0
        sd = d * pitch // DIL_GROUP

        def rows(r, first, n, d=d, sd=sd):
            if d == 1:
                return pl.ds(first, n)
            return pl.ds(r + r // DIL_GROUP + sd * first, n, stride=sd)

        assert t % half == 0 and W <= t + 2 * half
        qa = lax.broadcasted_iota(jnp.int32, (t, W), 0)
        kb = lax.broadcasted_iota(jnp.int32, (t, W), 1)
        inside = shift_ref[0:1, 0:W] if fixed_shift else 0.0
        for case in range(3):
            ok = jnp.abs(kb - qa - case * half) <= half
            bias_scr[case, 0:t, 0:W] = jnp.where(ok, inside, NEG)

        def body(idx, carry, bi=bi, d=d, L=L, half=half, t=t, W=W, nt=nt, rows=rows):
            r = idx // nt
            l0 = (idx % nt) * t
            start = jnp.clip(l0 - half, 0, L - W)
            if d == 1:
                qt = q_ref[pl.ds(pl.multiple_of(l0, t), t), :]
                kw = k_ref[pl.ds(pl.multiple_of(start, 64), W), :]
                vw = v_ref[pl.ds(pl.multiple_of(start, 64), W), :]
            else:
                qt = qp[rows(r, l0, t), :].astype(_BF)
                kw = kp[rows(r, start, W), :].astype(_BF)
                vw = vp[rows(r, start, W), :].astype(_BF)
            lo_t = lo[:t]
            qa_t = head_a[:t]
            zq = jnp.zeros_like(qt)
            q2 = jnp.concatenate([jnp.where(qa_t, qt, zq), jnp.where(qa_t, zq, qt)], axis=0)
            s = lax.dot_general(q2, kw, (((1,), (1,)), ((), ())), preferred_element_type=_F32)
            bias = bias_scr[(l0 - start) // half, 0:t, 0:W]
            s = s + jnp.concatenate([bias, bias], axis=0)
            if fixed_shift:
                p = jnp.exp2(s)
            else:
                m = jnp.max(s, axis=-1, keepdims=True)
                p = jnp.exp2(s - m)
                m_new = jnp.where(lo_t, m[:t], m[t:])
            den = jnp.sum(p, axis=-1, keepdims=True)
            o2 = jnp.dot(p.astype(_BF), vw, preferred_element_type=_F32)
            o_new = jnp.where(lo_t, o2[:t], o2[t:])
            l_new = jnp.where(lo_t, den[:t], den[t:])
            tok = rows(r, l0, t)
            if d == 1:
                assert bi == 0
                acc[tok, :] = o_new
                ll[tok, :] = l_new
                if not fixed_shift:
                    mm[tok, :] = m_new
            elif fixed_shift:
                accp[tok, :] = accp[tok, :] + o_new
                llp[tok, :] = llp[tok, :] + l_new
            else:
                m_old = mmp[tok, :]
                mx = jnp.maximum(m_old, m_new)
                a_old = jnp.exp2(m_old - mx)
                a_new = jnp.exp2(m_new - mx)
                accp[tok, :] = accp[tok, :] * a_old + o_new * a_new
                llp[tok, :] = llp[tok, :] * a_old + l_new * a_new
                mmp[tok, :] = mx
            return carry

        lax.fori_loop(0, d * nt, body, 0, unroll=min(8, d * nt))
        if bi == 0:
            spread(accp, acc)
            spread(llp, ll)
            if not fixed_shift:
                spread(mmp, mm)
    for j in range(DIL_GROUP):
        grp = pl.ds(j, ng, stride=pitch)
        tmp[pl.ds(j, ng, stride=DIL_GROUP), :] = accp[grp, :] / llp[grp, :]
    o_ref[...] = tmp[...].astype(o_ref.dtype)


def _dilated(fast, shift, q, k, v, tl):
    B, P, S, _ = q.shape
    wmax = tl + max(w // d for w, d in DIL_PATTERNS)
    spec = pl.BlockSpec((None, None, S, LANES), lambda b, p, f: (b, p, 0, 0))
    scr = pltpu.VMEM((S, LANES), _F32)
    scrp = pltpu.VMEM((S // DIL_GROUP * (DIL_GROUP + 1), LANES), _F32)
    grid_spec = pltpu.PrefetchScalarGridSpec(
        num_scalar_prefetch=1,
        grid=(B, P),
        in_specs=[spec, spec, spec, pl.BlockSpec((1, wmax), lambda b, p, f: (0, 0))],
        out_specs=spec,
        scratch_shapes=[scr, scrp, scrp, scrp, scr, scr, scr, scrp, scrp, scrp,
                        pltpu.VMEM((3, tl, wmax), _F32)],
    )
    return pl.pallas_call(
        functools.partial(_dil_kernel, tl=tl),
        grid_spec=grid_spec,
        out_shape=jax.ShapeDtypeStruct((B, P, S, LANES), _BF),
        compiler_params=_cparams(("parallel", "parallel")),
        name="dil_attn",
    )(fast, q, k, v, jnp.full((1, wmax), shift, _F32))


def _moe_kernel(te_ref, tv_ref, src_hbm, dst_hbm, h_hbm, wg_ref, wu_ref, wd_ref, out_hbm,
                xbuf, xbf, a_scr, ybuf, src_s0, src_s1, dst_s0, dst_s1, gsem, ssem, isem,
                *, tf, dump0, n_dump):
    i = pl.program_id(0)
    nt = pl.num_programs(0)
    tm = xbf.shape[0]
    slot = i % 2

    def is_valid(t):
        return jnp.logical_and(jnp.logical_and(t >= 0, t < nt), tv_ref[jnp.clip(t, 0, nt - 1)] == 1)

    def both(a, b):
        return jnp.logical_and(a, b)

    valid = is_valid(i)
    prev_valid = is_valid(i - 1)

    src_bufs = (src_s0, src_s1)
    dst_bufs = (dst_s0, dst_s1)

    def src_copy(tile, par):
        return pltpu.make_async_copy(src_hbm.at[tile], src_bufs[par], isem.at[par])

    def dst_copy(tile, par):
        return pltpu.make_async_copy(dst_hbm.at[tile], dst_bufs[par], isem.at[2 + par])

    def tile_at(first):
        return pl.ds(pl.multiple_of(first, SUBLANES), SUBLANES)

    def gather_row(r, par):
        return pltpu.make_async_copy(h_hbm.at[tile_at(src_bufs[par][r])],
                                     xbuf.at[par, tile_at(r * SUBLANES)], gsem.at[par])

    def scatter_row(r, par):
        return pltpu.make_async_copy(ybuf.at[par, tile_at(r * SUBLANES)],
                                     out_hbm.at[tile_at(dst_bufs[par][r])], ssem.at[par])

    def gather_wait(par):
        return pltpu.make_async_copy(h_hbm.at[pl.ds(0, tm * SUBLANES)], xbuf.at[par], gsem.at[par])

    def scatter_wait(par):
        return pltpu.make_async_copy(ybuf.at[par], out_hbm.at[pl.ds(0, tm * SUBLANES)], ssem.at[par])

    def for_rows(fn):
        def body(r, c):
            fn(r)
            return c
        lax.fori_loop(0, tm, body, 0, unroll=8)

    @pl.when(i == 0)
    def _():
        ybuf[...] = jnp.zeros_like(ybuf)
        for k in range(n_dump):
            pltpu.make_async_copy(ybuf.at[1], out_hbm.at[pl.ds((dump0 + k * tm) * SUBLANES, tm * SUBLANES)],
                                  ssem.at[1]).start()
        for k in range(n_dump):
            scatter_wait(1).wait()

    @pl.when(both(i == 0, valid))
    def _():
        src_copy(0, 0).start()
        src_copy(0, 0).wait()
        for_rows(lambda r: gather_row(r, 0).start())
        src_copy(1, 1).start()

    def phase1(par, with_scatter):
        xbf[...] = _tiles_to_rows(xbuf.at[par], tm).astype(_BF)
        n_chunk = -(-wg_ref.shape[1] // tf)
        per = -(-tm // n_chunk)

        def after(c):
            for r in range(c * per, min((c + 1) * per, tm)):
                gather_row(r, 1 - par).start()
                if with_scatter:
                    scatter_row(r, 1 - par).start()

        _swiglu_hidden(xbf[...], wg_ref, wu_ref, a_scr, tf, after)

    for par in range(2):
        here = slot == par

        @pl.when(both(valid, here))
        def _(par=par):
            dst_copy(i, par).start()
            src_copy(i + 1, 1 - par).wait()

        @pl.when(both(prev_valid, here))
        def _(par=par):
            dst_copy(i - 1, 1 - par).wait()

        @pl.when(both(jnp.logical_or(both(i == 0, valid), prev_valid), here))
        def _(par=par):
            gather_wait(par).wait()

        @pl.when(both(both(valid, prev_valid), here))
        def _(par=par):
            phase1(par, True)

        if par == 0:
            @pl.when(both(both(valid, jnp.logical_not(prev_valid)), here))
            def _():
                phase1(0, False)

        @pl.when(both(both(jnp.logical_not(valid), prev_valid), here))
        def _(par=par):
            for_rows(lambda r: scatter_row(r, 1 - par).start())

        @pl.when(both(is_valid(i - 2), here))
        def _(par=par):
            scatter_wait(par).wait()

        @pl.when(both(valid, here))
        def _(par=par):
            _rows_to_tiles(ybuf.at[par], jnp.dot(a_scr[...], wd_ref[...], preferred_element_type=_F32))

        @pl.when(both(is_valid(i + 1), here))
        def _(par=par):
            src_copy(i + 2, par).start()


def _moe(h, tile_e, tile_v, src, dst, wg, wu, wd, n_out_rows, tm, tf):
    T = h.shape[0] // SUBLANES
    D = wg.shape[1]
    nt = tile_e.shape[0]
    F = wg.shape[2]
    dump0 = T * TOP_K
    once = pl.Buffered(1)
    grid_spec = pltpu.PrefetchScalarGridSpec(
        num_scalar_prefetch=2,
        grid=(nt,),
        in_specs=[pl.BlockSpec(memory_space=pl.ANY), pl.BlockSpec(memory_space=pl.ANY),
                  pl.BlockSpec(memory_space=pl.ANY),
                  pl.BlockSpec((None, D, F), lambda i, te, tv: (te[i], 0, 0), pipeline_mode=once),
                  pl.BlockSpec((None, D, F), lambda i, te, tv: (te[i], 0, 0), pipeline_mode=once),
                  pl.BlockSpec((None, F, D), lambda i, te, tv: (te[i], 0, 0), pipeline_mode=once)],
        out_specs=pl.BlockSpec(memory_space=pl.ANY),
        scratch_shapes=[pltpu.VMEM((2, tm * SUBLANES, LANES), _F32), pltpu.VMEM((tm, D), _BF),
                        pltpu.VMEM((tm, F), _BF), pltpu.VMEM((2, tm * SUBLANES, LANES), _F32),
                        pltpu.SMEM((tm,), jnp.int32), pltpu.SMEM((tm,), jnp.int32),
                        pltpu.SMEM((tm,), jnp.int32), pltpu.SMEM((tm,), jnp.int32),
                        pltpu.SemaphoreType.DMA((2,)), pltpu.SemaphoreType.DMA((2,)),
                        pltpu.SemaphoreType.DMA((4,))],
    )
    return pl.pallas_call(
        functools.partial(_moe_kernel, tf=tf, dump0=dump0, n_dump=(n_out_rows - dump0) // tm),
        grid_spec=grid_spec,
        out_shape=jax.ShapeDtypeStruct((n_out_rows * SUBLANES, LANES), _F32),
        compiler_params=_cparams(("arbitrary",)),
        name="moe_experts",
    )(tile_e, tile_v, src, dst, h, wg, wu, wd)


def _route_plan(route, T, tm):
    A = T * TOP_K
    e_flat = route[:, :TOP_K].astype(jnp.int32).reshape(A)
    order = jnp.argsort(e_flat, stable=True).astype(jnp.int32)
    counts = jnp.sum((e_flat[:, None] == jnp.arange(N_EXPERTS, dtype=jnp.int32)[None, :]).astype(jnp.int32), axis=0)
    starts = jnp.cumsum(counts) - counts
    pcounts = (counts + tm - 1) // tm * tm
    pends = jnp.cumsum(pcounts)
    pstarts = pends - pcounts
    nt = A // tm + N_EXPERTS + 2
    tile0 = jnp.arange(nt, dtype=jnp.int32) * tm
    tile_v = (tile0 < pends[-1]).astype(jnp.int32)
    last_valid = jnp.maximum(pends[-1] // tm - 1, 0)
    tile_e_raw = jnp.minimum(jnp.searchsorted(pends, tile0, side="right"), N_EXPERTS - 1).astype(jnp.int32)
    tile_e = jnp.where(tile_v == 1, tile_e_raw, tile_e_raw[last_valid])
    r = jnp.arange(nt * tm, dtype=jnp.int32)
    e_r = jnp.repeat(tile_e, tm)
    within = r - pstarts[e_r]
    ok = jnp.logical_and(within < counts[e_r], jnp.repeat(tile_v, tm) == 1)
    a = order[jnp.clip(starts[e_r] + within, 0, A - 1)]
    src = jnp.where(ok, a // TOP_K, 0).astype(jnp.int32)
    dump = A + e_r * tm + jnp.clip(within - counts[e_r], 0, tm - 1)
    dst = jnp.where(ok, (a % TOP_K) * T + a // TOP_K, dump).astype(jnp.int32)
    return (tile_e, tile_v, (src * SUBLANES).reshape(nt, tm), (dst * SUBLANES).reshape(nt, tm),
            A + N_EXPERTS * tm)


def _combine_kernel(x_ref, y1_ref, y2_ref, r_ref, mod_ref, o_ref):
    tm = x_ref.shape[0]
    y = _tiles_to_rows(y1_ref, tm) * r_ref[:, 2:3] + _tiles_to_rows(y2_ref, tm) * r_ref[:, 3:4]
    o_ref[...] = x_ref[...] + mod_ref[5:6, :] * y


def _combine(x, y2, route, mod, tm):
    B, S, D = x.shape
    T = B * S
    per_b = S // tm
    out = pl.pallas_call(
        _combine_kernel,
        grid=(T // tm,),
        in_specs=[pl.BlockSpec((tm, D), lambda i: (i, 0)),
                  pl.BlockSpec((tm * SUBLANES, LANES), lambda i: (i, 0)),
                  pl.BlockSpec((tm * SUBLANES, LANES), lambda i: (T // tm + i, 0)),
                  pl.BlockSpec((tm, LANES), lambda i: (i, 0)),
                  pl.BlockSpec((None, 6, D), lambda i: (i // per_b, 0, 0))],
        out_specs=pl.BlockSpec((tm, D), lambda i: (i, 0)),
        out_shape=jax.ShapeDtypeStruct((T, D), _F32),
        compiler_params=_cparams(("parallel",)),
        name="moe_combine",
    )(x.reshape(T, D), y2, y2, route.reshape(T, LANES), mod)
    return out.reshape(B, S, D)


def _pad_cols(a, w):
    return jnp.pad(a, ((0, 0), (0, w - a.shape[1])))


def _rope_tables_even(S):
    pos = jnp.arange(S, dtype=jnp.int32)
    inv = ROPE_THETA ** (-jnp.arange(0, 32, 2, dtype=_F32) / 32)
    def cs(p):
        ang = p.astype(_F32)[:, None] * inv[None, :]
        return jnp.cos(ang), jnp.sin(ang)
    one = lambda w: jnp.ones((S, w), _F32)
    zero = lambda w: jnp.zeros((S, w), _F32)
    c, s = cs(pos)
    ca = jnp.concatenate([c, one(48), c, one(48)], 1)
    sa = jnp.concatenate([-s, zero(48), s, zero(48)], 1)
    cr, sr = cs(pos // GRID_W)
    cc, sc = cs(pos % GRID_W)
    cb = jnp.concatenate([cr, cc, one(32), cr, cc, one(32)], 1)
    sb = jnp.concatenate([-sr, -sc, zero(32), sr, sc, zero(32)], 1)
    return jnp.concatenate([ca, sa, cb, sb], 1)


def _slot_maps():
    r = MLA_ROPE // 2
    mla = ([MLA_NOPE + i for i in range(r)] + list(range(0, 64 - r))
           + [MLA_NOPE + r + i for i in range(r)] + list(range(64 - r, MLA_NOPE)))
    mla += [-1] * (LANES - len(mla))
    q = HEAD_DIM // 4
    gqa = (list(range(0, q)) + list(range(2 * q, 3 * q)) + [-1] * (64 - 2 * q)
           + list(range(q, 2 * q)) + list(range(3 * q, 4 * q)) + [-1] * (64 - 2 * q))
    return mla, gqa


def _to_slots(a, lane_map):
    idx = jnp.asarray([max(i, 0) for i in lane_map], jnp.int32)
    keep = jnp.asarray([1.0 if i >= 0 else 0.0 for i in lane_map], a.dtype)
    return jnp.take(a, idx, axis=-1) * keep


def _rope_tables_odd(S):
    pos = jnp.arange(S, dtype=_F32)
    inv = ROPE_THETA ** (-jnp.arange(0, HEAD_DIM, 2, dtype=_F32) / HEAD_DIM)
    ang = pos[:, None] * inv[None, :]
    c, s = jnp.cos(ang), jnp.sin(ang)
    return jnp.concatenate([c, c, c, c, -s, -s, s, s], 1)


def _tiles(S):
    return dict(tm_pre=min(S, 512), tq=min(S, 2048), tm_o=min(S, 512),
                tm_ffn=min(S, 512), tf_ffn=512, tm_qkv=min(S, 512),
                tl=128, tm_moe=min(S, 512), tf_moe=512, tm_c=min(S, 512))


def kernel(x, c, ada_even_w, ada_even_b, norm_even_mix, norm_even_ffn, even_w_in, mla_q_norm, mla_w_uq, mla_kv_norm, mla_w_ukv, mla_q_gain, mla_k_gain, gqa_q_gain, gqa_k_gain, even_w_out, ffn_w_gate, ffn_w_up, ffn_w_down, ada_odd_w, ada_odd_b, norm_odd_mix, norm_odd_ffn, dil_w_qkv, dil_q_gain, dil_k_gain, dil_w_out, moe_router, moe_w_gate, moe_w_up, moe_w_down):
    B, S, D = x.shape
    T = B * S
    cfg = _tiles(S)

    mod_e = _ada_mod(c, ada_even_w[0], ada_even_b[0]).reshape(B, 6, D)
    mod_o = _ada_mod(c, ada_odd_w[0], ada_odd_b[0]).reshape(B, 6, D)

    w = even_w_in[0]
    sp = [MLA_Q_RANK, MLA_Q_RANK + MLA_KV_RANK, MLA_Q_RANK + MLA_KV_RANK + MLA_ROPE]
    sp.append(sp[-1] + GQA_HEADS * HEAD_DIM)
    sp.append(sp[-1] + GQA_KV_HEADS * HEAD_DIM)
    w_cq, w_ckv, w_kpe = w[:, :sp[0]], w[:, sp[0]:sp[1]], w[:, sp[1]:sp[2]]
    w_qb, w_kb, w_vb = w[:, sp[2]:sp[3]], w[:, sp[3]:sp[4]], w[:, sp[4]:]
    mla_map, gqa_map = _slot_maps()
    na = MLA_NOPE + MLA_ROPE
    nope_only = [i if 0 <= i < MLA_NOPE else -1 for i in mla_map]
    rope_only = [i - MLA_NOPE if i >= MLA_NOPE else -1 for i in mla_map]
    gslots = lambda a, n: _to_slots(a.reshape(D, n, HEAD_DIM), gqa_map).reshape(D, n * LANES)
    w_in = jnp.concatenate([w_cq, w_ckv, _to_slots(w_kpe, rope_only), gslots(w_qb, GQA_HEADS),
                            gslots(w_kb, GQA_KV_HEADS)], axis=1).astype(_BF)
    w_uq = _to_slots(mla_w_uq[0].reshape(MLA_Q_RANK, MLA_HEADS, na), mla_map)
    w_uq = w_uq.reshape(MLA_Q_RANK, MLA_HEADS * LANES).astype(_BF)
    w_ukv = mla_w_ukv[0].reshape(MLA_KV_RANK, MLA_HEADS, MLA_NOPE + MLA_V)
    w_uk = _to_slots(w_ukv[:, :, :MLA_NOPE], nope_only).reshape(MLA_KV_RANK, MLA_HEADS * LANES).astype(_BF)
    w_uvt = w_ukv[:, :, MLA_NOPE:].reshape(MLA_KV_RANK, MLA_HEADS * MLA_V).T.astype(_BF)
    w_vbt = w_vb.T.astype(_BF)
    g_rows = [_to_slots(mla_q_gain[0], mla_map) * (na ** -0.5 * LOG2E), _to_slots(mla_k_gain[0], mla_map),
              _to_slots(gqa_q_gain[0], gqa_map) * (HEAD_DIM ** -0.5 * LOG2E), _to_slots(gqa_k_gain[0], gqa_map)]
    bound_a = na * jnp.max(jnp.abs(g_rows[0])) * jnp.max(jnp.abs(g_rows[1]))
    bound_b = HEAD_DIM * jnp.max(jnp.abs(g_rows[2])) * jnp.max(jnp.abs(g_rows[3]))
    fast_a, fast_b = bound_a <= SCORE_BOUND_MAX, bound_b <= SCORE_BOUND_MAX
    last_lane = jnp.zeros((LANES,), _F32).at[LANES - 1].set(1.0)
    g_rows += [last_lane, last_lane * jnp.where(fast_a, -bound_a, 0.0),
               last_lane, last_lane * jnp.where(fast_b, -bound_b, 0.0)]
    gains_e = jnp.stack(g_rows)
    rope_e = _rope_tables_even(S)
    q_all, k_all, vt_all = _pre_even(
        x, mod_e, norm_even_mix[0].reshape(1, D), w_in, mla_q_norm[0].reshape(1, -1), w_uq,
        mla_kv_norm[0].reshape(1, -1), w_uk, w_uvt, w_vbt, gains_e, rope_e, cfg["tm_pre"])
    as_flag = lambda f: f.astype(jnp.int32).reshape(1)
    o_a = _attention(as_flag(fast_a), q_all, k_all, vt_all, q_off=0, k_off=0, shared_kv=False, tq=cfg["tq"])
    o_b = _attention(as_flag(fast_b), q_all, k_all, vt_all, q_off=MLA_HEADS, k_off=MLA_HEADS,
                     shared_kv=True, tq=cfg["tq"])
    x2, h2 = _ffn(x, o_a, o_b, even_w_out[0].astype(_BF), mod_e, norm_even_ffn[0].reshape(1, D), mod_o,
                  norm_odd_mix[0].reshape(1, D), ffn_w_gate[0].astype(_BF), ffn_w_up[0].astype(_BF),
                  ffn_w_down[0].astype(_BF), cfg["tm_ffn"], cfg["tf_ffn"])

    hh = HEAD_DIM // 2
    pair = lambda v: jnp.concatenate([v[:hh], v[:hh], v[hh:], v[hh:]])
    gains_o = jnp.stack([pair(dil_q_gain[0]) * (HEAD_DIM ** -0.5 * LOG2E), pair(dil_k_gain[0])])
    wq, wk, wv = jnp.split(dil_w_qkv[0], 3, axis=1)
    perm = lambda w: w.reshape(D, D // LANES, 2, 2, hh).transpose(0, 1, 3, 2, 4).reshape(D, D)
    w_qkv = jnp.concatenate([perm(wq), perm(wk), wv], axis=1).astype(_BF)
    qd, kd, vd = _qkv(h2, w_qkv, gains_o, _rope_tables_odd(S), cfg["tm_qkv"])
    bound_d = HEAD_DIM * jnp.max(jnp.abs(gains_o[0])) * jnp.max(jnp.abs(gains_o[1]))
    fast_d = bound_d <= SCORE_BOUND_MAX
    o_d = _dilated(as_flag(fast_d), jnp.where(fast_d, -bound_d, 0.0), qd, kd, vd, cfg["tl"])
    r32 = _pad_cols(moe_router[0], LANES)
    r_hi = r32.astype(_BF)
    r_lo = (r32 - r_hi.astype(_F32)).astype(_BF)
    x3, h3, route = _oproj(x2, mod_o, norm_odd_ffn[0].reshape(1, D), dil_w_out[0].astype(_BF), [o_d],
                           cfg["tm_o"], _F32, router=jnp.stack([r_hi, r_lo]))
    tm = cfg["tm_moe"]
    tile_e, tile_v, src, dst, n_rows = _route_plan(route.reshape(T, LANES), T, tm)
    y2 = _moe(h3, tile_e, tile_v, src, dst, moe_w_gate[0].astype(_BF),
              moe_w_up[0].astype(_BF), moe_w_down[0].astype(_BF), n_rows, tm, cfg["tf_moe"])
    return _combine(x3, y2, route, mod_o, cfg["tm_c"])
```

```python
import functools
import math

import jax
import jax.numpy as jnp
from jax import lax
from jax.experimental import pallas as pl
from jax.experimental.pallas import tpu as pltpu

_BF = jnp.bfloat16
_F32 = jnp.float32

GRID_W = 64
HEAD_DIM = 64
ROPE_THETA = 10000.0
EPS = 1e-6
MLA_HEADS = 8
MLA_Q_RANK = 256
MLA_KV_RANK = 128
MLA_NOPE = 64
MLA_ROPE = 32
MLA_V = 64
GQA_HEADS = 8
GQA_KV_HEADS = 2
DIL_PATTERNS = ((128, 1), (512, 4), (2048, 16))
N_EXPERTS = 8
TOP_K = 2
NEG = -1e30
LOG2E = math.log2(math.e)

LANES = 128
VMEM_LIMIT = 56 * 1024 * 1024


def _cparams(sem, vmem=VMEM_LIMIT):
    return pltpu.CompilerParams(dimension_semantics=sem, vmem_limit_bytes=vmem)


def _silu(x):
    return x / (1.0 + jnp.exp(-x))


def _modulate(x, g, shift, scale):
    ms = jnp.mean(x * x, axis=-1, keepdims=True)
    return x * lax.rsqrt(ms + EPS) * g * (1.0 + scale) + shift


def _norm(v, n, g):
    return v * lax.rsqrt(jnp.sum(v * v, axis=-1, keepdims=True) * (1.0 / n) + EPS) * g


SUBLANES = 8


def _rows_to_tiles(ref, x):
    tm = x.shape[0]
    for c in range(SUBLANES):
        ref[pl.ds(c, tm, stride=SUBLANES), :] = x[:, c * LANES:(c + 1) * LANES]


def _tiles_to_rows(ref, tm):
    return jnp.concatenate([ref[pl.ds(c, tm, stride=SUBLANES), :] for c in range(SUBLANES)], axis=1)


def _rope(x, c, sa, sb, k):
    return x * c + pltpu.roll(x, LANES - k, 1) * sa + pltpu.roll(x, k, 1) * sb


def _mod_kernel(c_ref, w_ref, b_ref, o_ref):
    sc = _silu(c_ref[...])
    o_ref[...] = jnp.dot(sc.astype(_BF), w_ref[...].astype(_BF),
                         preferred_element_type=_F32) + b_ref[...]


def _ada_mod(c, w, b):
    B, D = c.shape
    N = w.shape[1]
    tn = min(N, 1536)
    return pl.pallas_call(
        _mod_kernel,
        grid=(N // tn,),
        in_specs=[pl.BlockSpec((B, D), lambda j: (0, 0)),
                  pl.BlockSpec((D, tn), lambda j: (0, j)),
                  pl.BlockSpec((1, tn), lambda j: (0, j))],
        out_specs=pl.BlockSpec((B, tn), lambda j: (0, j)),
        out_shape=jax.ShapeDtypeStruct((B, N), _F32),
        compiler_params=_cparams(("arbitrary",)),
        name="ada_mod",
    )(c, w, b.reshape(1, N))


def _pre_even_kernel(x_ref, mod_ref, g_ref, w_in_ref, qn_ref, w_uq_ref, kvn_ref, w_uk_ref, w_uvt_ref,
                     w_vbt_ref, gains_ref, rope_ref, q_ref, k_ref, vt_ref, nq_scr, nk_scr):
    h = _modulate(x_ref[...], g_ref[...], mod_ref[0:1, :], mod_ref[1:2, :]).astype(_BF)
    y = jnp.dot(h, w_in_ref[...], preferred_element_type=_F32)
    cqn = _norm(y[:, 0:256], MLA_Q_RANK, qn_ref[...]).astype(_BF)
    qa = jnp.dot(cqn, w_uq_ref[...], preferred_element_type=_F32)
    ckvn = _norm(y[:, 256:384], MLA_KV_RANK, kvn_ref[...]).astype(_BF)
    kn = jnp.dot(ckvn, w_uk_ref[...], preferred_element_type=_F32)
    kpe = y[:, 384:512]
    nt_dims = (((1,), (1,)), ((), ()))
    vt_a = lax.dot_general(w_uvt_ref[...], ckvn, nt_dims, preferred_element_type=_F32).astype(_BF)
    vt_b = lax.dot_general(w_vbt_ref[...], h, nt_dims, preferred_element_type=_F32).astype(_BF)
    tm = x_ref.shape[0]
    ones_blk = jnp.where(lax.broadcasted_iota(jnp.int32, (LANES - MLA_V, tm), 0) == 0, 1.0, 0.0).astype(_BF)
    for hh in range(MLA_HEADS + GQA_KV_HEADS):
        src = vt_a[hh * MLA_V:(hh + 1) * MLA_V] if hh < MLA_HEADS else \
            vt_b[(hh - MLA_HEADS) * HEAD_DIM:(hh - MLA_HEADS + 1) * HEAD_DIM]
        vt_ref[hh * LANES:hh * LANES + MLA_V, :] = src
        vt_ref[hh * LANES + MLA_V:(hh + 1) * LANES, :] = ones_blk
    ca, sa = rope_ref[:, 0:128], rope_ref[:, 128:256]
    cb, sb = rope_ref[:, 256:384], rope_ref[:, 384:512]
    gqa, gka = gains_ref[0:1, :], gains_ref[1:2, :]
    gqb, gkb = gains_ref[2:3, :], gains_ref[3:4, :]
    na = MLA_NOPE + MLA_ROPE

    for hh in range(MLA_HEADS):
        sl = slice(hh * LANES, (hh + 1) * LANES)
        nq_scr[:, sl] = _norm(qa[:, sl], na, gqa)
        nk_scr[:, sl] = _norm(kn[:, sl] + kpe, na, gka)
    for hh in range(GQA_HEADS):
        src = slice(512 + hh * LANES, 512 + (hh + 1) * LANES)
        dst = slice((MLA_HEADS + hh) * LANES, (MLA_HEADS + hh + 1) * LANES)
        nq_scr[:, dst] = _norm(y[:, src], HEAD_DIM, gqb)
    for g in range(GQA_KV_HEADS):
        src = slice(1536 + g * LANES, 1536 + (g + 1) * LANES)
        dst = slice((MLA_HEADS + g) * LANES, (MLA_HEADS + g + 1) * LANES)
        nk_scr[:, dst] = _norm(y[:, src], HEAD_DIM, gkb)

    @pl.when(pl.program_id(1) >= 0)
    def _():
        def rope(ref, out, n_a, n_all, row):
            for hh in range(n_all):
                sl = slice(hh * LANES, (hh + 1) * LANES)
                c, s = (ca, sa) if hh < n_a else (cb, sb)
                b = gains_ref[row:row + 1, :] if hh < n_a else gains_ref[row + 2:row + 3, :]
                v = ref[:, sl]
                out[:, sl] = (v * c + pltpu.roll(v, 64, 1) * s + b).astype(_BF)

        rope(nq_scr, q_ref, MLA_HEADS, MLA_HEADS + GQA_HEADS, 4)
        rope(nk_scr, k_ref, MLA_HEADS, MLA_HEADS + GQA_KV_HEADS, 5)


def _pre_even(x, mod, g, w_in, qn, w_uq, kvn, w_uk, w_uvt, w_vbt, gains, rope, tm):
    B, S, D = x.shape
    nq = (MLA_HEADS + GQA_HEADS) * LANES
    nk = (MLA_HEADS + GQA_KV_HEADS) * LANES
    nvt = (MLA_HEADS + GQA_KV_HEADS) * LANES
    full = lambda a: pl.BlockSpec(a.shape, lambda b, i: (0,) * a.ndim)
    tok = lambda w: pl.BlockSpec((None, tm, w), lambda b, i: (b, i, 0))
    return pl.pallas_call(
        _pre_even_kernel,
        grid=(B, S // tm),
        in_specs=[tok(D),
                  pl.BlockSpec((None, 6, D), lambda b, i: (b, 0, 0)),
                  full(g), full(w_in), full(qn), full(w_uq), full(kvn), full(w_uk), full(w_uvt),
                  full(w_vbt), full(gains),
                  pl.BlockSpec((tm, rope.shape[1]), lambda b, i: (i, 0))],
        out_specs=[tok(nq), tok(nk), pl.BlockSpec((None, None, nvt, tm), lambda b, i: (b, i, 0, 0))],
        out_shape=[jax.ShapeDtypeStruct((B, S, nq), _BF), jax.ShapeDtypeStruct((B, S, nk), _BF),
                   jax.ShapeDtypeStruct((B, S // tm, nvt, tm), _BF)],
        scratch_shapes=[pltpu.VMEM((tm, nq), _F32), pltpu.VMEM((tm, nk), _F32)],
        compiler_params=_cparams(("parallel", "parallel")),
        name="pre_even",
    )(x, mod, g, w_in, qn, w_uq, kvn, w_uk, w_uvt, w_vbt, gains, rope)


SCORE_BOUND_MAX = 60.0


def _attn_kernel(fast_ref, q_ref, k_ref, vt_ref, o_ref, s_scr, *, shared_kv):
    @pl.when(fast_ref[0] == 1)
    def _():
        _attn_fixed_shift(q_ref, k_ref, vt_ref, o_ref, shared_kv=shared_kv)

    @pl.when(fast_ref[0] != 1)
    def _():
        _attn_running_max(q_ref, k_ref, vt_ref, o_ref, s_scr, shared_kv=shared_kv)


def _attn_fixed_shift(q_ref, k_ref, vt_ref, o_ref, *, shared_kv):
    tq = q_ref.shape[0]
    nk, _, tk = vt_ref.shape
    qs = [q_ref[:, hh * LANES:(hh + 1) * LANES] for hh in range(2)]

    def body(j, carry):
        r0 = pl.multiple_of(j * tk, tk)
        out = []
        for hh in range(2):
            kc = 0 if shared_kv else hh * LANES
            k = k_ref[pl.ds(r0, tk), kc:kc + LANES]
            st = lax.dot_general(k, qs[hh], (((1,), (1,)), ((), ())), preferred_element_type=_F32)
            p = jnp.exp2(st).astype(_BF)
            out.append(carry[hh] + jnp.dot(vt_ref[j, kc:kc + LANES, :], p, preferred_element_type=_F32))
        return tuple(out)

    z = jnp.zeros((LANES, tq), _F32)
    res = lax.fori_loop(0, nk, body, (z, z), unroll=2)
    nv = HEAD_DIM
    halves = [acc[0:nv, :] / acc[nv:nv + 1, :] for acc in res]
    o_ref[...] = jnp.concatenate(halves, axis=0).T.astype(o_ref.dtype)


def _attn_running_max(q_ref, k_ref, vt_ref, o_ref, s_scr, *, shared_kv):
    tq = q_ref.shape[0]
    nk, _, tk = vt_ref.shape
    qs = [q_ref[:, hh * LANES:(hh + 1) * LANES] for hh in range(2)]

    def scores(j, hh):
        r0 = j * tk if isinstance(j, int) else pl.multiple_of(j * tk, tk)
        kc = 0 if shared_kv else hh * LANES
        k = k_ref[pl.ds(r0, tk), kc:kc + LANES]
        return lax.dot_general(k, qs[hh], (((1,), (1,)), ((), ())), preferred_element_type=_F32)

    def put_scores(j, hh, slot):
        st = scores(j, hh)
        s_scr[hh, slot] = st
        return jnp.max(st, axis=0, keepdims=True)

    def step(j, hh, carry, cur, last=False):
        m, mc, acc = carry
        mc_next = mc if last else put_scores(j + 1, hh, 1 - cur)
        st = s_scr[hh, cur]
        m_new = jnp.maximum(m, mc)
        alpha = jnp.exp2(m - m_new)
        p = jnp.exp2(st - m_new).astype(_BF)
        vc = 0 if shared_kv else hh * LANES
        acc = alpha * acc + jnp.dot(vt_ref[j, vc:vc + LANES, :], p, preferred_element_type=_F32)
        return m_new, mc_next, acc

    def body(jj, carry):
        c = list(carry)
        for sub in range(2):
            for hh in range(2):
                c[hh] = step(2 * jj + sub, hh, c[hh], sub)
        return tuple(c)

    init = tuple((jnp.full((1, tq), NEG, _F32), put_scores(0, hh, 0), jnp.zeros((LANES, tq), _F32))
                 for hh in range(2))
    res = lax.fori_loop(0, nk // 2 - 1, body, init)
    res = list(res)
    for sub in range(2):
        for hh in range(2):
            res[hh] = step(nk - 2 + sub, hh, res[hh], sub, last=(sub == 1))
    nv = HEAD_DIM
    halves = [acc[0:nv, :] / acc[nv:nv + 1, :] for (_, _, acc) in res]
    o_ref[...] = jnp.concatenate(halves, axis=0).T.astype(o_ref.dtype)


def _attention(fast, q, k, vt, *, q_off, k_off, shared_kv, tq):
    B, S, _ = q.shape
    nk, _, tk = vt.shape[1:]
    assert nk % 2 == 0 and nk >= 2
    npairs = 4
    if shared_kv:
        kspec = pl.BlockSpec((None, S, LANES), lambda b, p, i, f: (b, 0, k_off + p // 2))
        vspec = pl.BlockSpec((None, nk, LANES, tk), lambda b, p, i, f: (b, 0, k_off + p // 2, 0))
    else:
        kspec = pl.BlockSpec((None, S, 2 * LANES), lambda b, p, i, f: (b, 0, k_off // 2 + p))
        vspec = pl.BlockSpec((None, nk, 2 * LANES, tk), lambda b, p, i, f: (b, 0, k_off // 2 + p, 0))
    grid_spec = pltpu.PrefetchScalarGridSpec(
        num_scalar_prefetch=1,
        grid=(B, npairs, S // tq),
        in_specs=[pl.BlockSpec((None, tq, 2 * LANES), lambda b, p, i, f: (b, i, q_off // 2 + p)),
                  kspec, vspec],
        out_specs=pl.BlockSpec((None, tq, LANES), lambda b, p, i, f: (b, i, p)),
        scratch_shapes=[pltpu.VMEM((2, 2, tk, tq), _F32)],
    )
    return pl.pallas_call(
        functools.partial(_attn_kernel, shared_kv=shared_kv),
        grid_spec=grid_spec,
        out_shape=jax.ShapeDtypeStruct((B, S, npairs * LANES), _BF),
        compiler_params=_cparams(("parallel", "parallel", "parallel")),
        name="attn_gqa" if shared_kv else "attn_mla",
    )(fast, q, k, vt)


def _oproj_kernel(*refs, n_in, route):
    x_ref, mod_ref, g_ref, w_ref = refs[0], refs[1], refs[2], refs[3]
    o_refs = refs[4:4 + n_in]
    pos = 4 + n_in
    if route:
        r_ref = refs[pos]
        pos += 1
    x_out, h_out = refs[pos], refs[pos + 1]
    y = None
    off = 0
    for o_ref in o_refs:
        if len(o_ref.shape) == 3:
            o = jnp.concatenate([o_ref[p] for p in range(o_ref.shape[0])], axis=1)
        else:
            o = o_ref[...]
        w = o.shape[1]
        t = jnp.dot(o, w_ref[off:off + w, :], preferred_element_type=_F32)
        y = t if y is None else y + t
        off += w
    x1 = x_ref[...] + mod_ref[2:3, :] * y
    x_out[...] = x1
    h = _modulate(x1, g_ref[...], mod_ref[3:4, :], mod_ref[4:5, :])
    if route:
        _rows_to_tiles(h_out, h)
    else:
        h_out[...] = h.astype(h_out.dtype)
    if route:
        route_out = refs[pos + 2]
        hh = h.astype(_BF)
        hl = (h - hh.astype(_F32)).astype(_BF)
        rh, rl = r_ref[0], r_ref[1]
        logits = (jnp.dot(hh, rh, preferred_element_type=_F32)
                  + jnp.dot(hh, rl, preferred_element_type=_F32)
                  + jnp.dot(hl, rh, preferred_element_type=_F32))
        tm = logits.shape[0]
        lane = lax.broadcasted_iota(jnp.int32, (tm, LANES), 1)
        lg = jnp.where(lane < N_EXPERTS, logits, NEG)
        m1 = jnp.max(lg, axis=-1, keepdims=True)
        lanef = lane.astype(_F32)
        i1 = jnp.min(jnp.where(lg == m1, lanef, float(LANES)), axis=-1, keepdims=True)
        lg2 = jnp.where(lanef == i1, NEG, lg)
        m2 = jnp.max(lg2, axis=-1, keepdims=True)
        i2 = jnp.min(jnp.where(lg2 == m2, lanef, float(LANES)), axis=-1, keepdims=True)
        e = jnp.exp(m2 - m1)
        g1 = 1.0 / (1.0 + e)
        g2 = e / (1.0 + e)
        route_out[...] = jnp.where(lane == 0, i1,
                                   jnp.where(lane == 1, i2,
                                             jnp.where(lane == 2, g1, jnp.where(lane == 3, g2, 0.0))))


def _oproj(x, mod, g, w, o_list, tm, h_dtype, router=None):
    B, S, D = x.shape
    n_in = len(o_list)
    route = router is not None
    tok = lambda wd: pl.BlockSpec((None, tm, wd), lambda b, i: (b, i, 0))
    in_specs = [tok(D), pl.BlockSpec((None, 6, D), lambda b, i: (b, 0, 0)),
                pl.BlockSpec((1, D), lambda b, i: (0, 0)),
                pl.BlockSpec(w.shape, lambda b, i: (0, 0))]
    for o in o_list:
        if o.ndim == 4:
            in_specs.append(pl.BlockSpec((None, o.shape[1], tm, LANES), lambda b, i: (b, 0, i, 0)))
        else:
            in_specs.append(tok(o.shape[2]))
    args = [x, mod, g, w] + list(o_list)
    out_specs = [tok(D), tok(D)]
    out_shape = [jax.ShapeDtypeStruct((B, S, D), _F32), jax.ShapeDtypeStruct((B, S, D), h_dtype)]
    if route:
        assert D == SUBLANES * LANES
        per_b = S // tm
        out_specs[1] = pl.BlockSpec((tm * SUBLANES, LANES), lambda b, i: (b * per_b + i, 0))
        out_shape[1] = jax.ShapeDtypeStruct((B * S * SUBLANES, LANES), _F32)
        in_specs.append(pl.BlockSpec(router.shape, lambda b, i: (0, 0, 0)))
        args.append(router)
        out_specs.append(tok(LANES))
        out_shape.append(jax.ShapeDtypeStruct((B, S, LANES), _F32))
    return pl.pallas_call(
        functools.partial(_oproj_kernel, n_in=n_in, route=route),
        grid=(B, S // tm),
        in_specs=in_specs, out_specs=out_specs, out_shape=out_shape,
        compiler_params=_cparams(("parallel", "parallel")),
        name="oproj_route" if route else "oproj",
    )(*args)


def _swiglu_hidden(x, wg_ref, wu_ref, a_scr, tf, after_chunk=None):
    F = wg_ref.shape[1]
    for c0 in range(0, F, tf):
        c1 = min(c0 + tf, F)
        g = jnp.dot(x, wg_ref[:, c0:c1], preferred_element_type=_F32)
        u = jnp.dot(x, wu_ref[:, c0:c1], preferred_element_type=_F32)
        a_scr[:, c0:c1] = (_silu(g) * u).astype(_BF)
        if after_chunk is not None:
            after_chunk(c0 // tf)


def _ffn_kernel(x_ref, oa_ref, ob_ref, wo_ref, mod_ref, g_ref, modn_ref, gn_ref, wg_ref, wu_ref, wd_ref,
                x_out, h_out, a_scr, *, tf):
    wa = oa_ref.shape[1]
    y = (jnp.dot(oa_ref[...], wo_ref[0:wa, :], preferred_element_type=_F32)
         + jnp.dot(ob_ref[...], wo_ref[wa:, :], preferred_element_type=_F32))
    x1 = x_ref[...] + mod_ref[2:3, :] * y
    h = _modulate(x1, g_ref[...], mod_ref[3:4, :], mod_ref[4:5, :]).astype(_BF)
    _swiglu_hidden(h, wg_ref, wu_ref, a_scr, tf)
    y = jnp.dot(a_scr[...], wd_ref[...], preferred_element_type=_F32)
    x2 = x1 + mod_ref[5:6, :] * y
    x_out[...] = x2
    h_out[...] = _modulate(x2, gn_ref[...], modn_ref[0:1, :], modn_ref[1:2, :]).astype(h_out.dtype)


def _ffn(x, o_a, o_b, w_out, mod, g, modn, gn, wg, wu, wd, tm, tf):
    B, S, D = x.shape
    T = B * S
    F = wg.shape[1]
    per_b = S // tm
    flat = lambda a: a.reshape(T, a.shape[2])
    tok = lambda w: pl.BlockSpec((tm, w), lambda i: (i, 0))
    modspec = pl.BlockSpec((None, 6, D), lambda i: (i // per_b, 0, 0))
    row = pl.BlockSpec((1, D), lambda i: (0, 0))
    once = pl.Buffered(1)
    res = lambda a: pl.BlockSpec(a.shape, lambda i: (0, 0), pipeline_mode=once)
    xo, ho = pl.pallas_call(
        functools.partial(_ffn_kernel, tf=tf),
        grid=(T // tm,),
        in_specs=[tok(D), tok(o_a.shape[2]), tok(o_b.shape[2]), res(w_out), modspec, row, modspec, row,
                  res(wg), res(wu), res(wd)],
        out_specs=[tok(D), tok(D)],
        out_shape=[jax.ShapeDtypeStruct((T, D), _F32), jax.ShapeDtypeStruct((T, D), _BF)],
        scratch_shapes=[pltpu.VMEM((tm, F), _BF)],
        compiler_params=_cparams(("parallel",)),
        name="ffn_dense",
    )(flat(x), flat(o_a), flat(o_b), w_out, mod, g, modn, gn, wg, wu, wd)
    return xo.reshape(B, S, D), ho.reshape(B, S, D)


def _qkv_kernel(h_ref, w_ref, gains_ref, rope_ref, q_ref, k_ref, v_ref, n_scr):
    tm, D = h_ref.shape
    h = h_ref[...]
    lane = lax.broadcasted_iota(jnp.int32, (tm, LANES), 1)
    head_a = (lane & 63) < 32

    def head_pair_norm(v, g):
        sq = v * v
        s_a = jnp.sum(jnp.where(head_a, sq, 0.0), axis=-1, keepdims=True)
        s_b = jnp.sum(jnp.where(head_a, 0.0, sq), axis=-1, keepdims=True)
        r = lax.rsqrt(jnp.where(head_a, s_a, s_b) * (1.0 / HEAD_DIM) + EPS)
        return v * r * g

    y = jnp.dot(h, w_ref[...], preferred_element_type=_F32)
    for p in range(D // LANES):
        sl = slice(p * LANES, (p + 1) * LANES)
        n_scr[:, sl] = head_pair_norm(y[:, sl], gains_ref[0:1, :])
        n_scr[:, D + p * LANES:D + (p + 1) * LANES] = head_pair_norm(
            y[:, D + p * LANES:D + (p + 1) * LANES], gains_ref[1:2, :])
        v_ref[p] = y[:, 2 * D + p * LANES:2 * D + (p + 1) * LANES].astype(_BF)

    @pl.when(pl.program_id(1) >= 0)
    def _():
        c, s = rope_ref[:, 0:128], rope_ref[:, 128:256]
        for p in range(D // LANES):
            vq = n_scr[:, p * LANES:(p + 1) * LANES]
            vk = n_scr[:, D + p * LANES:D + (p + 1) * LANES]
            q_ref[p] = (vq * c + pltpu.roll(vq, 64, 1) * s).astype(_BF)
            k_ref[p] = (vk * c + pltpu.roll(vk, 64, 1) * s).astype(_BF)


def _qkv(h, w, gains, rope, tm):
    B, S, D = h.shape
    tok = pl.BlockSpec((None, tm, D), lambda b, i: (b, i, 0))
    hp = pl.BlockSpec((None, D // LANES, tm, LANES), lambda b, i: (b, 0, i, 0))
    sds = jax.ShapeDtypeStruct((B, D // LANES, S, LANES), _BF)
    return pl.pallas_call(
        _qkv_kernel,
        grid=(B, S // tm),
        in_specs=[tok, pl.BlockSpec(w.shape, lambda b, i: (0, 0)),
                  pl.BlockSpec((2, LANES), lambda b, i: (0, 0)),
                  pl.BlockSpec((tm, 2 * LANES), lambda b, i: (i, 0))],
        out_specs=[hp, hp, hp],
        out_shape=[sds, sds, sds],
        scratch_shapes=[pltpu.VMEM((tm, 2 * D), _F32)],
        compiler_params=_cparams(("parallel", "parallel")),
        name="qkv_dil",
    )(h, w, gains, rope)


DIL_GROUP = 4


def _dil_kernel(fast_ref, q_ref, k_ref, v_ref, shift_ref, o_ref, tmp, qp, kp, vp, acc, mm, ll, accp, mmp,
                llp, bias_scr, *, tl):
    @pl.when(fast_ref[0] == 1)
    def _():
        _dil_body(q_ref, k_ref, v_ref, shift_ref, o_ref, tmp, qp, kp, vp, acc, mm, ll, accp, mmp, llp,
                  bias_scr, tl=tl, fixed_shift=True)

    @pl.when(fast_ref[0] != 1)
    def _():
        _dil_body(q_ref, k_ref, v_ref, shift_ref, o_ref, tmp, qp, kp, vp, acc, mm, ll, accp, mmp, llp,
                  bias_scr, tl=tl, fixed_shift=False)


def _dil_body(q_ref, k_ref, v_ref, shift_ref, o_ref, tmp, qp, kp, vp, acc, mm, ll, accp, mmp, llp,
              bias_scr, *, tl, fixed_shift):
    S = q_ref.shape[0]
    ng = S // DIL_GROUP
    pitch = DIL_GROUP + 1

    def spread(dst, src):
        for j in range(DIL_GROUP):
            dst[pl.ds(j, ng, stride=pitch), :] = src[pl.ds(j, ng, stride=DIL_GROUP), :]

    for src, dst in ((q_ref, qp), (k_ref, kp), (v_ref, vp)):
        tmp[...] = src[...].astype(_F32)
        spread(dst, tmp)
    lane = lax.broadcasted_iota(jnp.int32, (tl, LANES), 1)
    lo = lane < 64
    head_a = (lane & 63) < 32
    for bi, (window, d) in enumerate(DIL_PATTERNS):
        L = S // d
        half = window // (2 * d)
        t = min(tl, L)
        W = min(t + 2 * half, L)
        nt = L // t
        if d > 1:
            assert d % DIL_GROUP == 0
        sd = d * pitch // DIL_GROUP

        def rows(r, first, n, d=d, sd=sd):
            if d == 1:
                return pl.ds(first, n)
            return pl.ds(r + r // DIL_GROUP + sd * first, n, stride=sd)

        assert t % half == 0 and W <= t + 2 * half
        qa = lax.broadcasted_iota(jnp.int32, (t, W), 0)
        kb = lax.broadcasted_iota(jnp.int32, (t, W), 1)
        inside = shift_ref[0:1, 0:W] if fixed_shift else 0.0
        for case in range(3):
            ok = jnp.abs(kb - qa - case * half) <= half
            bias_scr[case, 0:t, 0:W] = jnp.where(ok, inside, NEG)

        def body(idx, carry, bi=bi, d=d, L=L, half=half, t=t, W=W, nt=nt, rows=rows):
            r = idx // nt
            l0 = (idx % nt) * t
            start = jnp.clip(l0 - half, 0, L - W)
            if d == 1:
                qt = q_ref[pl.ds(pl.multiple_of(l0, t), t), :]
                kw = k_ref[pl.ds(pl.multiple_of(start, 64), W), :]
                vw = v_ref[pl.ds(pl.multiple_of(start, 64), W), :]
            else:
                qt = qp[rows(r, l0, t), :].astype(_BF)
                kw = kp[rows(r, start, W), :].astype(_BF)
                vw = vp[rows(r, start, W), :].astype(_BF)
            lo_t = lo[:t]
            qa_t = head_a[:t]
            zq = jnp.zeros_like(qt)
            q2 = jnp.concatenate([jnp.where(qa_t, qt, zq), jnp.where(qa_t, zq, qt)], axis=0)
            s = lax.dot_general(q2, kw, (((1,), (1,)), ((), ())), preferred_element_type=_F32)
            bias = bias_scr[(l0 - start) // half, 0:t, 0:W]
            s = s + jnp.concatenate([bias, bias], axis=0)
            if fixed_shift:
                p = jnp.exp2(s)
            else:
                m = jnp.max(s, axis=-1, keepdims=True)
                p = jnp.exp2(s - m)
                m_new = jnp.where(lo_t, m[:t], m[t:])
            den = jnp.sum(p, axis=-1, keepdims=True)
            o2 = jnp.dot(p.astype(_BF), vw, preferred_element_type=_F32)
            o_new = jnp.where(lo_t, o2[:t], o2[t:])
            l_new = jnp.where(lo_t, den[:t], den[t:])
            tok = rows(r, l0, t)
            if d == 1:
                assert bi == 0
                acc[tok, :] = o_new
                ll[tok, :] = l_new
                if not fixed_shift:
                    mm[tok, :] = m_new
            elif fixed_shift:
                accp[tok, :] = accp[tok, :] + o_new
                llp[tok, :] = llp[tok, :] + l_new
            else:
                m_old = mmp[tok, :]
                mx = jnp.maximum(m_old, m_new)
                a_old = jnp.exp2(m_old - mx)
                a_new = jnp.exp2(m_new - mx)
                accp[tok, :] = accp[tok, :] * a_old + o_new * a_new
                llp[tok, :] = llp[tok, :] * a_old + l_new * a_new
                mmp[tok, :] = mx
            return carry

        lax.fori_loop(0, d * nt, body, 0, unroll=min(32 if fixed_shift else 8, d * nt))
        if bi == 0:
            spread(accp, acc)
            spread(llp, ll)
            if not fixed_shift:
                spread(mmp, mm)
    for j in range(DIL_GROUP):
        grp = pl.ds(j, ng, stride=pitch)
        tmp[pl.ds(j, ng, stride=DIL_GROUP), :] = accp[grp, :] / llp[grp, :]
    o_ref[...] = tmp[...].astype(o_ref.dtype)


def _dilated(fast, shift, q, k, v, tl):
    B, P, S, _ = q.shape
    wmax = tl + max(w // d for w, d in DIL_PATTERNS)
    spec = pl.BlockSpec((None, None, S, LANES), lambda b, p, f: (b, p, 0, 0))
    scr = pltpu.VMEM((S, LANES), _F32)
    scrp = pltpu.VMEM((S // DIL_GROUP * (DIL_GROUP + 1), LANES), _F32)
    grid_spec = pltpu.PrefetchScalarGridSpec(
        num_scalar_prefetch=1,
        grid=(B, P),
        in_specs=[spec, spec, spec, pl.BlockSpec((1, wmax), lambda b, p, f: (0, 0))],
        out_specs=spec,
        scratch_shapes=[scr, scrp, scrp, scrp, scr, scr, scr, scrp, scrp, scrp,
                        pltpu.VMEM((3, tl, wmax), _F32)],
    )
    return pl.pallas_call(
        functools.partial(_dil_kernel, tl=tl),
        grid_spec=grid_spec,
        out_shape=jax.ShapeDtypeStruct((B, P, S, LANES), _BF),
        compiler_params=_cparams(("parallel", "parallel")),
        name="dil_attn",
    )(fast, q, k, v, jnp.full((1, wmax), shift, _F32))


def _moe_kernel(te_ref, tv_ref, src_hbm, dst_hbm, h_hbm, wg_ref, wu_ref, wd_ref, out_hbm,
                xbuf, xbf, a_scr, ybuf, src_s0, src_s1, dst_s0, dst_s1, gsem, ssem, isem,
                *, tf, dump0, n_dump):
    i = pl.program_id(0)
    nt = pl.num_programs(0)
    tm = xbf.shape[0]
    slot = i % 2

    def is_valid(t):
        return jnp.logical_and(jnp.logical_and(t >= 0, t < nt), tv_ref[jnp.clip(t, 0, nt - 1)] == 1)

    def both(a, b):
        return jnp.logical_and(a, b)

    valid = is_valid(i)
    prev_valid = is_valid(i - 1)

    src_bufs = (src_s0, src_s1)
    dst_bufs = (dst_s0, dst_s1)

    def src_copy(tile, par):
        return pltpu.make_async_copy(src_hbm.at[tile], src_bufs[par], isem.at[par])

    def dst_copy(tile, par):
        return pltpu.make_async_copy(dst_hbm.at[tile], dst_bufs[par], isem.at[2 + par])

    def tile_at(first):
        return pl.ds(pl.multiple_of(first, SUBLANES), SUBLANES)

    def gather_row(r, par):
        return pltpu.make_async_copy(h_hbm.at[tile_at(src_bufs[par][r])],
                                     xbuf.at[par, tile_at(r * SUBLANES)], gsem.at[par])

    def scatter_row(r, par):
        return pltpu.make_async_copy(ybuf.at[par, tile_at(r * SUBLANES)],
                                     out_hbm.at[tile_at(dst_bufs[par][r])], ssem.at[par])

    def gather_wait(par):
        return pltpu.make_async_copy(h_hbm.at[pl.ds(0, tm * SUBLANES)], xbuf.at[par], gsem.at[par])

    def scatter_wait(par):
        return pltpu.make_async_copy(ybuf.at[par], out_hbm.at[pl.ds(0, tm * SUBLANES)], ssem.at[par])

    def for_rows(fn):
        def body(r, c):
            fn(r)
            return c
        lax.fori_loop(0, tm, body, 0, unroll=8)

    @pl.when(i == 0)
    def _():
        ybuf[...] = jnp.zeros_like(ybuf)
        for k in range(n_dump):
            pltpu.make_async_copy(ybuf.at[1], out_hbm.at[pl.ds((dump0 + k * tm) * SUBLANES, tm * SUBLANES)],
                                  ssem.at[1]).start()
        for k in range(n_dump):
            scatter_wait(1).wait()

    @pl.when(both(i == 0, valid))
    def _():
        src_copy(0, 0).start()
        src_copy(0, 0).wait()
        for_rows(lambda r: gather_row(r, 0).start())
        src_copy(1, 1).start()

    def phase1(par, with_scatter):
        xbf[...] = _tiles_to_rows(xbuf.at[par], tm).astype(_BF)
        n_chunk = -(-wg_ref.shape[1] // tf)
        per = -(-tm // n_chunk)

        def after(c):
            for r in range(c * per, min((c + 1) * per, tm)):
                gather_row(r, 1 - par).start()
                if with_scatter:
                    scatter_row(r, 1 - par).start()

        _swiglu_hidden(xbf[...], wg_ref, wu_ref, a_scr, tf, after)

    for par in range(2):
        here = slot == par

        @pl.when(both(valid, here))
        def _(par=par):
            dst_copy(i, par).start()
            src_copy(i + 1, 1 - par).wait()

        @pl.when(both(prev_valid, here))
        def _(par=par):
            dst_copy(i - 1, 1 - par).wait()

        @pl.when(both(jnp.logical_or(both(i == 0, valid), prev_valid), here))
        def _(par=par):
            gather_wait(par).wait()

        @pl.when(both(both(valid, prev_valid), here))
        def _(par=par):
            phase1(par, True)

        if par == 0:
            @pl.when(both(both(valid, jnp.logical_not(prev_valid)), here))
            def _():
                phase1(0, False)

        @pl.when(both(both(jnp.logical_not(valid), prev_valid), here))
        def _(par=par):
            for_rows(lambda r: scatter_row(r, 1 - par).start())

        @pl.when(both(is_valid(i - 2), here))
        def _(par=par):
            scatter_wait(par).wait()

        @pl.when(both(valid, here))
        def _(par=par):
            _rows_to_tiles(ybuf.at[par], jnp.dot(a_scr[...], wd_ref[...], preferred_element_type=_F32))

        @pl.when(both(is_valid(i + 1), here))
        def _(par=par):
            src_copy(i + 2, par).start()


def _moe(h, tile_e, tile_v, src, dst, wg, wu, wd, n_out_rows, tm, tf):
    T = h.shape[0] // SUBLANES
    D = wg.shape[1]
    nt = tile_e.shape[0]
    F = wg.shape[2]
    dump0 = T * TOP_K
    once = pl.Buffered(1)
    grid_spec = pltpu.PrefetchScalarGridSpec(
        num_scalar_prefetch=2,
        grid=(nt,),
        in_specs=[pl.BlockSpec(memory_space=pl.ANY), pl.BlockSpec(memory_space=pl.ANY),
                  pl.BlockSpec(memory_space=pl.ANY),
                  pl.BlockSpec((None, D, F), lambda i, te, tv: (te[i], 0, 0), pipeline_mode=once),
                  pl.BlockSpec((None, D, F), lambda i, te, tv: (te[i], 0, 0), pipeline_mode=once),
                  pl.BlockSpec((None, F, D), lambda i, te, tv: (te[i], 0, 0), pipeline_mode=once)],
        out_specs=pl.BlockSpec(memory_space=pl.ANY),
        scratch_shapes=[pltpu.VMEM((2, tm * SUBLANES, LANES), _F32), pltpu.VMEM((tm, D), _BF),
                        pltpu.VMEM((tm, F), _BF), pltpu.VMEM((2, tm * SUBLANES, LANES), _F32),
                        pltpu.SMEM((tm,), jnp.int32), pltpu.SMEM((tm,), jnp.int32),
                        pltpu.SMEM((tm,), jnp.int32), pltpu.SMEM((tm,), jnp.int32),
                        pltpu.SemaphoreType.DMA((2,)), pltpu.SemaphoreType.DMA((2,)),
                        pltpu.SemaphoreType.DMA((4,))],
    )
    return pl.pallas_call(
        functools.partial(_moe_kernel, tf=tf, dump0=dump0, n_dump=(n_out_rows - dump0) // tm),
        grid_spec=grid_spec,
        out_shape=jax.ShapeDtypeStruct((n_out_rows * SUBLANES, LANES), _F32),
        compiler_params=_cparams(("arbitrary",)),
        name="moe_experts",
    )(tile_e, tile_v, src, dst, h, wg, wu, wd)


def _route_plan(route, T, tm):
    A = T * TOP_K
    e_flat = route[:, :TOP_K].astype(jnp.int32).reshape(A)
    order = jnp.argsort(e_flat, stable=True).astype(jnp.int32)
    counts = jnp.sum((e_flat[:, None] == jnp.arange(N_EXPERTS, dtype=jnp.int32)[None, :]).astype(jnp.int32), axis=0)
    starts = jnp.cumsum(counts) - counts
    pcounts = (counts + tm - 1) // tm * tm
    pends = jnp.cumsum(pcounts)
    pstarts = pends - pcounts
    nt = A // tm + N_EXPERTS + 2
    tile0 = jnp.arange(nt, dtype=jnp.int32) * tm
    tile_v = (tile0 < pends[-1]).astype(jnp.int32)
    last_valid = jnp.maximum(pends[-1] // tm - 1, 0)
    tile_e_raw = jnp.minimum(jnp.searchsorted(pends, tile0, side="right"), N_EXPERTS - 1).astype(jnp.int32)
    tile_e = jnp.where(tile_v == 1, tile_e_raw, tile_e_raw[last_valid])
    r = jnp.arange(nt * tm, dtype=jnp.int32)
    e_r = jnp.repeat(tile_e, tm)
    within = r - pstarts[e_r]
    ok = jnp.logical_and(within < counts[e_r], jnp.repeat(tile_v, tm) == 1)
    a = order[jnp.clip(starts[e_r] + within, 0, A - 1)]
    src = jnp.where(ok, a // TOP_K, 0).astype(jnp.int32)
    dump = A + e_r * tm + jnp.clip(within - counts[e_r], 0, tm - 1)
    dst = jnp.where(ok, (a % TOP_K) * T + a // TOP_K, dump).astype(jnp.int32)
    return (tile_e, tile_v, (src * SUBLANES).reshape(nt, tm), (dst * SUBLANES).reshape(nt, tm),
            A + N_EXPERTS * tm)


def _combine_kernel(x_ref, y1_ref, y2_ref, r_ref, mod_ref, o_ref):
    tm = x_ref.shape[0]
    y = _tiles_to_rows(y1_ref, tm) * r_ref[:, 2:3] + _tiles_to_rows(y2_ref, tm) * r_ref[:, 3:4]
    o_ref[...] = x_ref[...] + mod_ref[5:6, :] * y


def _combine(x, y2, route, mod, tm):
    B, S, D = x.shape
    T = B * S
    per_b = S // tm
    out = pl.pallas_call(
        _combine_kernel,
        grid=(T // tm,),
        in_specs=[pl.BlockSpec((tm, D), lambda i: (i, 0)),
                  pl.BlockSpec((tm * SUBLANES, LANES), lambda i: (i, 0)),
                  pl.BlockSpec((tm * SUBLANES, LANES), lambda i: (T // tm + i, 0)),
                  pl.BlockSpec((tm, LANES), lambda i: (i, 0)),
                  pl.BlockSpec((None, 6, D), lambda i: (i // per_b, 0, 0))],
        out_specs=pl.BlockSpec((tm, D), lambda i: (i, 0)),
        out_shape=jax.ShapeDtypeStruct((T, D), _F32),
        compiler_params=_cparams(("parallel",)),
        name="moe_combine",
    )(x.reshape(T, D), y2, y2, route.reshape(T, LANES), mod)
    return out.reshape(B, S, D)


def _pad_cols(a, w):
    return jnp.pad(a, ((0, 0), (0, w - a.shape[1])))


def _rope_tables_even(S):
    pos = jnp.arange(S, dtype=jnp.int32)
    inv = ROPE_THETA ** (-jnp.arange(0, 32, 2, dtype=_F32) / 32)
    def cs(p):
        ang = p.astype(_F32)[:, None] * inv[None, :]
        return jnp.cos(ang), jnp.sin(ang)
    one = lambda w: jnp.ones((S, w), _F32)
    zero = lambda w: jnp.zeros((S, w), _F32)
    c, s = cs(pos)
    ca = jnp.concatenate([c, one(48), c, one(48)], 1)
    sa = jnp.concatenate([-s, zero(48), s, zero(48)], 1)
    cr, sr = cs(pos // GRID_W)
    cc, sc = cs(pos % GRID_W)
    cb = jnp.concatenate([cr, cc, one(32), cr, cc, one(32)], 1)
    sb = jnp.concatenate([-sr, -sc, zero(32), sr, sc, zero(32)], 1)
    return jnp.concatenate([ca, sa, cb, sb], 1)


def _slot_maps():
    r = MLA_ROPE // 2
    mla = ([MLA_NOPE + i for i in range(r)] + list(range(0, 64 - r))
           + [MLA_NOPE + r + i for i in range(r)] + list(range(64 - r, MLA_NOPE)))
    mla += [-1] * (LANES - len(mla))
    q = HEAD_DIM // 4
    gqa = (list(range(0, q)) + list(range(2 * q, 3 * q)) + [-1] * (64 - 2 * q)
           + list(range(q, 2 * q)) + list(range(3 * q, 4 * q)) + [-1] * (64 - 2 * q))
    return mla, gqa


def _to_slots(a, lane_map):
    idx = jnp.asarray([max(i, 0) for i in lane_map], jnp.int32)
    keep = jnp.asarray([1.0 if i >= 0 else 0.0 for i in lane_map], a.dtype)
    return jnp.take(a, idx, axis=-1) * keep


def _rope_tables_odd(S):
    pos = jnp.arange(S, dtype=_F32)
    inv = ROPE_THETA ** (-jnp.arange(0, HEAD_DIM, 2, dtype=_F32) / HEAD_DIM)
    ang = pos[:, None] * inv[None, :]
    c, s = jnp.cos(ang), jnp.sin(ang)
    return jnp.concatenate([c, c, c, c, -s, -s, s, s], 1)


def _tiles(S):
    return dict(tm_pre=min(S, 512), tq=min(S, 2048), tm_o=min(S, 512),
                tm_ffn=min(S, 512), tf_ffn=512, tm_qkv=min(S, 512),
                tl=128, tm_moe=min(S, 512), tf_moe=512, tm_c=min(S, 512))


def kernel(x, c, ada_even_w, ada_even_b, norm_even_mix, norm_even_ffn, even_w_in, mla_q_norm, mla_w_uq, mla_kv_norm, mla_w_ukv, mla_q_gain, mla_k_gain, gqa_q_gain, gqa_k_gain, even_w_out, ffn_w_gate, ffn_w_up, ffn_w_down, ada_odd_w, ada_odd_b, norm_odd_mix, norm_odd_ffn, dil_w_qkv, dil_q_gain, dil_k_gain, dil_w_out, moe_router, moe_w_gate, moe_w_up, moe_w_down):
    B, S, D = x.shape
    T = B * S
    cfg = _tiles(S)

    mod_e = _ada_mod(c, ada_even_w[0], ada_even_b[0]).reshape(B, 6, D)
    mod_o = _ada_mod(c, ada_odd_w[0], ada_odd_b[0]).reshape(B, 6, D)

    w = even_w_in[0]
    sp = [MLA_Q_RANK, MLA_Q_RANK + MLA_KV_RANK, MLA_Q_RANK + MLA_KV_RANK + MLA_ROPE]
    sp.append(sp[-1] + GQA_HEADS * HEAD_DIM)
    sp.append(sp[-1] + GQA_KV_HEADS * HEAD_DIM)
    w_cq, w_ckv, w_kpe = w[:, :sp[0]], w[:, sp[0]:sp[1]], w[:, sp[1]:sp[2]]
    w_qb, w_kb, w_vb = w[:, sp[2]:sp[3]], w[:, sp[3]:sp[4]], w[:, sp[4]:]
    mla_map, gqa_map = _slot_maps()
    na = MLA_NOPE + MLA_ROPE
    nope_only = [i if 0 <= i < MLA_NOPE else -1 for i in mla_map]
    rope_only = [i - MLA_NOPE if i >= MLA_NOPE else -1 for i in mla_map]
    gslots = lambda a, n: _to_slots(a.reshape(D, n, HEAD_DIM), gqa_map).reshape(D, n * LANES)
    w_in = jnp.concatenate([w_cq, w_ckv, _to_slots(w_kpe, rope_only), gslots(w_qb, GQA_HEADS),
                            gslots(w_kb, GQA_KV_HEADS)], axis=1).astype(_BF)
    w_uq = _to_slots(mla_w_uq[0].reshape(MLA_Q_RANK, MLA_HEADS, na), mla_map)
    w_uq = w_uq.reshape(MLA_Q_RANK, MLA_HEADS * LANES).astype(_BF)
    w_ukv = mla_w_ukv[0].reshape(MLA_KV_RANK, MLA_HEADS, MLA_NOPE + MLA_V)
    w_uk = _to_slots(w_ukv[:, :, :MLA_NOPE], nope_only).reshape(MLA_KV_RANK, MLA_HEADS * LANES).astype(_BF)
    w_uvt = w_ukv[:, :, MLA_NOPE:].reshape(MLA_KV_RANK, MLA_HEADS * MLA_V).T.astype(_BF)
    w_vbt = w_vb.T.astype(_BF)
    g_rows = [_to_slots(mla_q_gain[0], mla_map) * (na ** -0.5 * LOG2E), _to_slots(mla_k_gain[0], mla_map),
              _to_slots(gqa_q_gain[0], gqa_map) * (HEAD_DIM ** -0.5 * LOG2E), _to_slots(gqa_k_gain[0], gqa_map)]
    bound_a = na * jnp.max(jnp.abs(g_rows[0])) * jnp.max(jnp.abs(g_rows[1]))
    bound_b = HEAD_DIM * jnp.max(jnp.abs(g_rows[2])) * jnp.max(jnp.abs(g_rows[3]))
    fast_a, fast_b = bound_a <= SCORE_BOUND_MAX, bound_b <= SCORE_BOUND_MAX
    last_lane = jnp.zeros((LANES,), _F32).at[LANES - 1].set(1.0)
    g_rows += [last_lane, last_lane * jnp.where(fast_a, -bound_a, 0.0),
               last_lane, last_lane * jnp.where(fast_b, -bound_b, 0.0)]
    gains_e = jnp.stack(g_rows)
    rope_e = _rope_tables_even(S)
    q_all, k_all, vt_all = _pre_even(
        x, mod_e, norm_even_mix[0].reshape(1, D), w_in, mla_q_norm[0].reshape(1, -1), w_uq,
        mla_kv_norm[0].reshape(1, -1), w_uk, w_uvt, w_vbt, gains_e, rope_e, cfg["tm_pre"])
    as_flag = lambda f: f.astype(jnp.int32).reshape(1)
    o_a = _attention(as_flag(fast_a), q_all, k_all, vt_all, q_off=0, k_off=0, shared_kv=False, tq=cfg["tq"])
    o_b = _attention(as_flag(fast_b), q_all, k_all, vt_all, q_off=MLA_HEADS, k_off=MLA_HEADS,
                     shared_kv=True, tq=cfg["tq"])
    x2, h2 = _ffn(x, o_a, o_b, even_w_out[0].astype(_BF), mod_e, norm_even_ffn[0].reshape(1, D), mod_o,
                  norm_odd_mix[0].reshape(1, D), ffn_w_gate[0].astype(_BF), ffn_w_up[0].astype(_BF),
                  ffn_w_down[0].astype(_BF), cfg["tm_ffn"], cfg["tf_ffn"])

    hh = HEAD_DIM // 2
    pair = lambda v: jnp.concatenate([v[:hh], v[:hh], v[hh:], v[hh:]])
    gains_o = jnp.stack([pair(dil_q_gain[0]) * (HEAD_DIM ** -0.5 * LOG2E), pair(dil_k_gain[0])])
    wq, wk, wv = jnp.split(dil_w_qkv[0], 3, axis=1)
    perm = lambda w: w.reshape(D, D // LANES, 2, 2, hh).transpose(0, 1, 3, 2, 4).reshape(D, D)
    w_qkv = jnp.concatenate([perm(wq), perm(wk), wv], axis=1).astype(_BF)
    qd, kd, vd = _qkv(h2, w_qkv, gains_o, _rope_tables_odd(S), cfg["tm_qkv"])
    bound_d = HEAD_DIM * jnp.max(jnp.abs(gains_o[0])) * jnp.max(jnp.abs(gains_o[1]))
    fast_d = bound_d <= SCORE_BOUND_MAX
    o_d = _dilated(as_flag(fast_d), jnp.where(fast_d, -bound_d, 0.0), qd, kd, vd, cfg["tl"])
    r32 = _pad_cols(moe_router[0], LANES)
    r_hi = r32.astype(_BF)
    r_lo = (r32 - r_hi.astype(_F32)).astype(_BF)
    x3, h3, route = _oproj(x2, mod_o, norm_odd_ffn[0].reshape(1, D), dil_w_out[0].astype(_BF), [o_d],
                           cfg["tm_o"], _F32, router=jnp.stack([r_hi, r_lo]))
    tm = cfg["tm_moe"]
    tile_e, tile_v, src, dst, n_rows = _route_plan(route.reshape(T, LANES), T, tm)
    y2 = _moe(h3, tile_e, tile_v, src, dst, moe_w_gate[0].astype(_BF),
              moe_w_up[0].astype(_BF), moe_w_down[0].astype(_BF), n_rows, tm, cfg["tf_moe"])
    return _combine(x3, y2, route, mod_o, cfg["tm_c"])
```

```python
import functools
import math

import jax
import jax.numpy as jnp
from jax import lax
from jax.experimental import pallas as pl
from jax.experimental.pallas import tpu as pltpu

_BF = jnp.bfloat16
_F32 = jnp.float32

GRID_W = 64
HEAD_DIM = 64
ROPE_THETA = 10000.0
EPS = 1e-6
MLA_HEADS = 8
MLA_Q_RANK = 256
MLA_KV_RANK = 128
MLA_NOPE = 64
MLA_ROPE = 32
MLA_V = 64
GQA_HEADS = 8
GQA_KV_HEADS = 2
DIL_PATTERNS = ((128, 1), (512, 4), (2048, 16))
N_EXPERTS = 8
TOP_K = 2
NEG = -1e30
LOG2E = math.log2(math.e)

LANES = 128
VMEM_LIMIT = 56 * 1024 * 1024


def _cparams(sem, vmem=VMEM_LIMIT):
    return pltpu.CompilerParams(dimension_semantics=sem, vmem_limit_bytes=vmem)


def _silu(x):
    return x / (1.0 + jnp.exp(-x))


def _modulate(x, g, shift, scale):
    ms = jnp.mean(x * x, axis=-1, keepdims=True)
    return x * lax.rsqrt(ms + EPS) * g * (1.0 + scale) + shift


def _norm(v, n, g):
    return v * lax.rsqrt(jnp.sum(v * v, axis=-1, keepdims=True) * (1.0 / n) + EPS) * g


SUBLANES = 8


def _rows_to_tiles(ref, x):
    tm = x.shape[0]
    for c in range(SUBLANES):
        ref[pl.ds(c, tm, stride=SUBLANES), :] = x[:, c * LANES:(c + 1) * LANES]


def _tiles_to_rows(ref, tm):
    return jnp.concatenate([ref[pl.ds(c, tm, stride=SUBLANES), :] for c in range(SUBLANES)], axis=1)


def _rope(x, c, sa, sb, k):
    return x * c + pltpu.roll(x, LANES - k, 1) * sa + pltpu.roll(x, k, 1) * sb


def _mod_kernel(c_ref, w_ref, b_ref, o_ref):
    sc = _silu(c_ref[...])
    o_ref[...] = jnp.dot(sc.astype(_BF), w_ref[...].astype(_BF),
                         preferred_element_type=_F32) + b_ref[...]


def _ada_mod(c, w, b):
    B, D = c.shape
    N = w.shape[1]
    tn = min(N, 1536)
    return pl.pallas_call(
        _mod_kernel,
        grid=(N // tn,),
        in_specs=[pl.BlockSpec((B, D), lambda j: (0, 0)),
                  pl.BlockSpec((D, tn), lambda j: (0, j)),
                  pl.BlockSpec((1, tn), lambda j: (0, j))],
        out_specs=pl.BlockSpec((B, tn), lambda j: (0, j)),
        out_shape=jax.ShapeDtypeStruct((B, N), _F32),
        compiler_params=_cparams(("arbitrary",)),
        name="ada_mod",
    )(c, w, b.reshape(1, N))


def _pre_even_kernel(x_ref, mod_ref, g_ref, w_in_ref, qn_ref, w_uq_ref, kvn_ref, w_uk_ref, w_uvt_ref,
                     w_vbt_ref, gains_ref, rope_ref, q_ref, k_ref, vt_ref, nq_scr, nk_scr):
    h = _modulate(x_ref[...], g_ref[...], mod_ref[0:1, :], mod_ref[1:2, :]).astype(_BF)
    y = jnp.dot(h, w_in_ref[...], preferred_element_type=_F32)
    cqn = _norm(y[:, 0:256], MLA_Q_RANK, qn_ref[...]).astype(_BF)
    qa = jnp.dot(cqn, w_uq_ref[...], preferred_element_type=_F32)
    ckvn = _norm(y[:, 256:384], MLA_KV_RANK, kvn_ref[...]).astype(_BF)
    kn = jnp.dot(ckvn, w_uk_ref[...], preferred_element_type=_F32)
    kpe = y[:, 384:512]
    nt_dims = (((1,), (1,)), ((), ()))
    vt_a = lax.dot_general(w_uvt_ref[...], ckvn, nt_dims, preferred_element_type=_F32).astype(_BF)
    vt_b = lax.dot_general(w_vbt_ref[...], h, nt_dims, preferred_element_type=_F32).astype(_BF)
    tm = x_ref.shape[0]
    ones_blk = jnp.where(lax.broadcasted_iota(jnp.int32, (LANES - MLA_V, tm), 0) == 0, 1.0, 0.0).astype(_BF)
    for hh in range(MLA_HEADS + GQA_KV_HEADS):
        src = vt_a[hh * MLA_V:(hh + 1) * MLA_V] if hh < MLA_HEADS else \
            vt_b[(hh - MLA_HEADS) * HEAD_DIM:(hh - MLA_HEADS + 1) * HEAD_DIM]
        vt_ref[hh * LANES:hh * LANES + MLA_V, :] = src
        vt_ref[hh * LANES + MLA_V:(hh + 1) * LANES, :] = ones_blk
    ca, sa = rope_ref[:, 0:128], rope_ref[:, 128:256]
    cb, sb = rope_ref[:, 256:384], rope_ref[:, 384:512]
    gqa, gka = gains_ref[0:1, :], gains_ref[1:2, :]
    gqb, gkb = gains_ref[2:3, :], gains_ref[3:4, :]
    na = MLA_NOPE + MLA_ROPE

    for hh in range(MLA_HEADS):
        sl = slice(hh * LANES, (hh + 1) * LANES)
        nq_scr[:, sl] = _norm(qa[:, sl], na, gqa)
        nk_scr[:, sl] = _norm(kn[:, sl] + kpe, na, gka)
    for hh in range(GQA_HEADS):
        src = slice(512 + hh * LANES, 512 + (hh + 1) * LANES)
        dst = slice((MLA_HEADS + hh) * LANES, (MLA_HEADS + hh + 1) * LANES)
        nq_scr[:, dst] = _norm(y[:, src], HEAD_DIM, gqb)
    for g in range(GQA_KV_HEADS):
        src = slice(1536 + g * LANES, 1536 + (g + 1) * LANES)
        dst = slice((MLA_HEADS + g) * LANES, (MLA_HEADS + g + 1) * LANES)
        nk_scr[:, dst] = _norm(y[:, src], HEAD_DIM, gkb)

    @pl.when(pl.program_id(1) >= 0)
    def _():
        def rope(ref, out, n_a, n_all, row):
            for hh in range(n_all):
                sl = slice(hh * LANES, (hh + 1) * LANES)
                c, s = (ca, sa) if hh < n_a else (cb, sb)
                b = gains_ref[row:row + 1, :] if hh < n_a else gains_ref[row + 2:row + 3, :]
                v = ref[:, sl]
                out[:, sl] = (v * c + pltpu.roll(v, 64, 1) * s + b).astype(_BF)

        rope(nq_scr, q_ref, MLA_HEADS, MLA_HEADS + GQA_HEADS, 4)
        rope(nk_scr, k_ref, MLA_HEADS, MLA_HEADS + GQA_KV_HEADS, 5)


def _pre_even(x, mod, g, w_in, qn, w_uq, kvn, w_uk, w_uvt, w_vbt, gains, rope, tm):
    B, S, D = x.shape
    nq = (MLA_HEADS + GQA_HEADS) * LANES
    nk = (MLA_HEADS + GQA_KV_HEADS) * LANES
    nvt = (MLA_HEADS + GQA_KV_HEADS) * LANES
    full = lambda a: pl.BlockSpec(a.shape, lambda b, i: (0,) * a.ndim)
    tok = lambda w: pl.BlockSpec((None, tm, w), lambda b, i: (b, i, 0))
    return pl.pallas_call(
        _pre_even_kernel,
        grid=(B, S // tm),
        in_specs=[tok(D),
                  pl.BlockSpec((None, 6, D), lambda b, i: (b, 0, 0)),
                  full(g), full(w_in), full(qn), full(w_uq), full(kvn), full(w_uk), full(w_uvt),
                  full(w_vbt), full(gains),
                  pl.BlockSpec((tm, rope.shape[1]), lambda b, i: (i, 0))],
        out_specs=[tok(nq), tok(nk), pl.BlockSpec((None, None, nvt, tm), lambda b, i: (b, i, 0, 0))],
        out_shape=[jax.ShapeDtypeStruct((B, S, nq), _BF), jax.ShapeDtypeStruct((B, S, nk), _BF),
                   jax.ShapeDtypeStruct((B, S // tm, nvt, tm), _BF)],
        scratch_shapes=[pltpu.VMEM((tm, nq), _F32), pltpu.VMEM((tm, nk), _F32)],
        compiler_params=_cparams(("parallel", "parallel")),
        name="pre_even",
    )(x, mod, g, w_in, qn, w_uq, kvn, w_uk, w_uvt, w_vbt, gains, rope)


SCORE_BOUND_MAX = 60.0


def _attn_kernel(fast_ref, q_ref, k_ref, vt_ref, o_ref, s_scr, *, shared_kv):
    @pl.when(fast_ref[0] == 1)
    def _():
        _attn_fixed_shift(q_ref, k_ref, vt_ref, o_ref, shared_kv=shared_kv)

    @pl.when(fast_ref[0] != 1)
    def _():
        _attn_running_max(q_ref, k_ref, vt_ref, o_ref, s_scr, shared_kv=shared_kv)


def _attn_fixed_shift(q_ref, k_ref, vt_ref, o_ref, *, shared_kv):
    tq = q_ref.shape[0]
    nk, _, tk = vt_ref.shape
    qs = [q_ref[:, hh * LANES:(hh + 1) * LANES] for hh in range(2)]

    def body(j, carry):
        r0 = pl.multiple_of(j * tk, tk)
        out = []
        for hh in range(2):
            kc = 0 if shared_kv else hh * LANES
            k = k_ref[pl.ds(r0, tk), kc:kc + LANES]
            st = lax.dot_general(k, qs[hh], (((1,), (1,)), ((), ())), preferred_element_type=_F32)
            p = jnp.exp2(st).astype(_BF)
            out.append(carry[hh] + jnp.dot(vt_ref[j, kc:kc + LANES, :], p, preferred_element_type=_F32))
        return tuple(out)

    z = jnp.zeros((LANES, tq), _F32)
    res = lax.fori_loop(0, nk, body, (z, z), unroll=True)
    nv = HEAD_DIM
    halves = [acc[0:nv, :] / acc[nv:nv + 1, :] for acc in res]
    o_ref[...] = jnp.concatenate(halves, axis=0).T.astype(o_ref.dtype)


def _attn_running_max(q_ref, k_ref, vt_ref, o_ref, s_scr, *, shared_kv):
    tq = q_ref.shape[0]
    nk, _, tk = vt_ref.shape
    qs = [q_ref[:, hh * LANES:(hh + 1) * LANES] for hh in range(2)]

    def scores(j, hh):
        r0 = j * tk if isinstance(j, int) else pl.multiple_of(j * tk, tk)
        kc = 0 if shared_kv else hh * LANES
        k = k_ref[pl.ds(r0, tk), kc:kc + LANES]
        return lax.dot_general(k, qs[hh], (((1,), (1,)), ((), ())), preferred_element_type=_F32)

    def put_scores(j, hh, slot):
        st = scores(j, hh)
        s_scr[hh, slot] = st
        return jnp.max(st, axis=0, keepdims=True)

    def step(j, hh, carry, cur, last=False):
        m, mc, acc = carry
        mc_next = mc if last else put_scores(j + 1, hh, 1 - cur)
        st = s_scr[hh, cur]
        m_new = jnp.maximum(m, mc)
        alpha = jnp.exp2(m - m_new)
        p = jnp.exp2(st - m_new).astype(_BF)
        vc = 0 if shared_kv else hh * LANES
        acc = alpha * acc + jnp.dot(vt_ref[j, vc:vc + LANES, :], p, preferred_element_type=_F32)
        return m_new, mc_next, acc

    def body(jj, carry):
        c = list(carry)
        for sub in range(2):
            for hh in range(2):
                c[hh] = step(2 * jj + sub, hh, c[hh], sub)
        return tuple(c)

    init = tuple((jnp.full((1, tq), NEG, _F32), put_scores(0, hh, 0), jnp.zeros((LANES, tq), _F32))
                 for hh in range(2))
    res = lax.fori_loop(0, nk // 2 - 1, body, init)
    res = list(res)
    for sub in range(2):
        for hh in range(2):
            res[hh] = step(nk - 2 + sub, hh, res[hh], sub, last=(sub == 1))
    nv = HEAD_DIM
    halves = [acc[0:nv, :] / acc[nv:nv + 1, :] for (_, _, acc) in res]
    o_ref[...] = jnp.concatenate(halves, axis=0).T.astype(o_ref.dtype)


def _attention(fast, q, k, vt, *, q_off, k_off, shared_kv, tq):
    B, S, _ = q.shape
    nk, _, tk = vt.shape[1:]
    assert nk % 2 == 0 and nk >= 2
    npairs = 4
    if shared_kv:
        kspec = pl.BlockSpec((None, S, LANES), lambda b, p, i, f: (b, 0, k_off + p // 2))
        vspec = pl.BlockSpec((None, nk, LANES, tk), lambda b, p, i, f: (b, 0, k_off + p // 2, 0))
    else:
        kspec = pl.BlockSpec((None, S, 2 * LANES), lambda b, p, i, f: (b, 0, k_off // 2 + p))
        vspec = pl.BlockSpec((None, nk, 2 * LANES, tk), lambda b, p, i, f: (b, 0, k_off // 2 + p, 0))
    grid_spec = pltpu.PrefetchScalarGridSpec(
        num_scalar_prefetch=1,
        grid=(B, npairs, S // tq),
        in_specs=[pl.BlockSpec((None, tq, 2 * LANES), lambda b, p, i, f: (b, i, q_off // 2 + p)),
                  kspec, vspec],
        out_specs=pl.BlockSpec((None, tq, LANES), lambda b, p, i, f: (b, i, p)),
        scratch_shapes=[pltpu.VMEM((2, 2, tk, tq), _F32)],
    )
    return pl.pallas_call(
        functools.partial(_attn_kernel, shared_kv=shared_kv),
        grid_spec=grid_spec,
        out_shape=jax.ShapeDtypeStruct((B, S, npairs * LANES), _BF),
        compiler_params=_cparams(("parallel", "parallel", "parallel")),
        name="attn_gqa" if shared_kv else "attn_mla",
    )(fast, q, k, vt)


def _oproj_kernel(*refs, n_in, route):
    x_ref, mod_ref, g_ref, w_ref = refs[0], refs[1], refs[2], refs[3]
    o_refs = refs[4:4 + n_in]
    pos = 4 + n_in
    if route:
        r_ref = refs[pos]
        pos += 1
    x_out, h_out = refs[pos], refs[pos + 1]
    y = None
    off = 0
    for o_ref in o_refs:
        if len(o_ref.shape) == 3:
            o = jnp.concatenate([o_ref[p] for p in range(o_ref.shape[0])], axis=1)
        else:
            o = o_ref[...]
        w = o.shape[1]
        t = jnp.dot(o, w_ref[off:off + w, :], preferred_element_type=_F32)
        y = t if y is None else y + t
        off += w
    x1 = x_ref[...] + mod_ref[2:3, :] * y
    x_out[...] = x1
    h = _modulate(x1, g_ref[...], mod_ref[3:4, :], mod_ref[4:5, :])
    if route:
        _rows_to_tiles(h_out, h)
    else:
        h_out[...] = h.astype(h_out.dtype)
    if route:
        route_out = refs[pos + 2]
        hh = h.astype(_BF)
        hl = (h - hh.astype(_F32)).astype(_BF)
        rh, rl = r_ref[0], r_ref[1]
        logits = (jnp.dot(hh, rh, preferred_element_type=_F32)
                  + jnp.dot(hh, rl, preferred_element_type=_F32)
                  + jnp.dot(hl, rh, preferred_element_type=_F32))
        tm = logits.shape[0]
        lane = lax.broadcasted_iota(jnp.int32, (tm, LANES), 1)
        lg = jnp.where(lane < N_EXPERTS, logits, NEG)
        m1 = jnp.max(lg, axis=-1, keepdims=True)
        lanef = lane.astype(_F32)
        i1 = jnp.min(jnp.where(lg == m1, lanef, float(LANES)), axis=-1, keepdims=True)
        lg2 = jnp.where(lanef == i1, NEG, lg)
        m2 = jnp.max(lg2, axis=-1, keepdims=True)
        i2 = jnp.min(jnp.where(lg2 == m2, lanef, float(LANES)), axis=-1, keepdims=True)
        e = jnp.exp(m2 - m1)
        g1 = 1.0 / (1.0 + e)
        g2 = e / (1.0 + e)
        route_out[...] = jnp.where(lane == 0, i1,
                                   jnp.where(lane == 1, i2,
                                             jnp.where(lane == 2, g1, jnp.where(lane == 3, g2, 0.0))))


def _oproj(x, mod, g, w, o_list, tm, h_dtype, router=None):
    B, S, D = x.shape
    n_in = len(o_list)
    route = router is not None
    tok = lambda wd: pl.BlockSpec((None, tm, wd), lambda b, i: (b, i, 0))
    in_specs = [tok(D), pl.BlockSpec((None, 6, D), lambda b, i: (b, 0, 0)),
                pl.BlockSpec((1, D), lambda b, i: (0, 0)),
                pl.BlockSpec(w.shape, lambda b, i: (0, 0))]
    for o in o_list:
        if o.ndim == 4:
            in_specs.append(pl.BlockSpec((None, o.shape[1], tm, LANES), lambda b, i: (b, 0, i, 0)))
        else:
            in_specs.append(tok(o.shape[2]))
    args = [x, mod, g, w] + list(o_list)
    out_specs = [tok(D), tok(D)]
    out_shape = [jax.ShapeDtypeStruct((B, S, D), _F32), jax.ShapeDtypeStruct((B, S, D), h_dtype)]
    if route:
        assert D == SUBLANES * LANES
        per_b = S // tm
        out_specs[1] = pl.BlockSpec((tm * SUBLANES, LANES), lambda b, i: (b * per_b + i, 0))
        out_shape[1] = jax.ShapeDtypeStruct((B * S * SUBLANES, LANES), _F32)
        in_specs.append(pl.BlockSpec(router.shape, lambda b, i: (0, 0, 0)))
        args.append(router)
        out_specs.append(tok(LANES))
        out_shape.append(jax.ShapeDtypeStruct((B, S, LANES), _F32))
    return pl.pallas_call(
        functools.partial(_oproj_kernel, n_in=n_in, route=route),
        grid=(B, S // tm),
        in_specs=in_specs, out_specs=out_specs, out_shape=out_shape,
        compiler_params=_cparams(("parallel", "parallel")),
        name="oproj_route" if route else "oproj",
    )(*args)


def _swiglu_hidden(x, wg_ref, wu_ref, a_scr, tf, after_chunk=None):
    F = wg_ref.shape[1]
    for c0 in range(0, F, tf):
        c1 = min(c0 + tf, F)
        g = jnp.dot(x, wg_ref[:, c0:c1], preferred_element_type=_F32)
        u = jnp.dot(x, wu_ref[:, c0:c1], preferred_element_type=_F32)
        a_scr[:, c0:c1] = (_silu(g) * u).astype(_BF)
        if after_chunk is not None:
            after_chunk(c0 // tf)


def _ffn_kernel(x_ref, oa_ref, ob_ref, wo_ref, mod_ref, g_ref, modn_ref, gn_ref, wg_ref, wu_ref, wd_ref,
                x_out, h_out, a_scr, *, tf):
    wa = oa_ref.shape[1]
    y = (jnp.dot(oa_ref[...], wo_ref[0:wa, :], preferred_element_type=_F32)
         + jnp.dot(ob_ref[...], wo_ref[wa:, :], preferred_element_type=_F32))
    x1 = x_ref[...] + mod_ref[2:3, :] * y
    h = _modulate(x1, g_ref[...], mod_ref[3:4, :], mod_ref[4:5, :]).astype(_BF)
    _swiglu_hidden(h, wg_ref, wu_ref, a_scr, tf)
    y = jnp.dot(a_scr[...], wd_ref[...], preferred_element_type=_F32)
    x2 = x1 + mod_ref[5:6, :] * y
    x_out[...] = x2
    h_out[...] = _modulate(x2, gn_ref[...], modn_ref[0:1, :], modn_ref[1:2, :]).astype(h_out.dtype)


def _ffn(x, o_a, o_b, w_out, mod, g, modn, gn, wg, wu, wd, tm, tf):
    B, S, D = x.shape
    T = B * S
    F = wg.shape[1]
    per_b = S // tm
    flat = lambda a: a.reshape(T, a.shape[2])
    tok = lambda w: pl.BlockSpec((tm, w), lambda i: (i, 0))
    modspec = pl.BlockSpec((None, 6, D), lambda i: (i // per_b, 0, 0))
    row = pl.BlockSpec((1, D), lambda i: (0, 0))
    once = pl.Buffered(1)
    res = lambda a: pl.BlockSpec(a.shape, lambda i: (0, 0), pipeline_mode=once)
    xo, ho = pl.pallas_call(
        functools.partial(_ffn_kernel, tf=tf),
        grid=(T // tm,),
        in_specs=[tok(D), tok(o_a.shape[2]), tok(o_b.shape[2]), res(w_out), modspec, row, modspec, row,
                  res(wg), res(wu), res(wd)],
        out_specs=[tok(D), tok(D)],
        out_shape=[jax.ShapeDtypeStruct((T, D), _F32), jax.ShapeDtypeStruct((T, D), _BF)],
        scratch_shapes=[pltpu.VMEM((tm, F), _BF)],
        compiler_params=_cparams(("parallel",)),
        name="ffn_dense",
    )(flat(x), flat(o_a), flat(o_b), w_out, mod, g, modn, gn, wg, wu, wd)
    return xo.reshape(B, S, D), ho.reshape(B, S, D)


def _qkv_kernel(h_ref, w_ref, gains_ref, rope_ref, q_ref, k_ref, v_ref, n_scr):
    tm, D = h_ref.shape
    h = h_ref[...]
    lane = lax.broadcasted_iota(jnp.int32, (tm, LANES), 1)
    head_a = (lane & 63) < 32

    def head_pair_norm(v, g):
        sq = v * v
        s_a = jnp.sum(jnp.where(head_a, sq, 0.0), axis=-1, keepdims=True)
        s_b = jnp.sum(jnp.where(head_a, 0.0, sq), axis=-1, keepdims=True)
        r = lax.rsqrt(jnp.where(head_a, s_a, s_b) * (1.0 / HEAD_DIM) + EPS)
        return v * r * g

    y = jnp.dot(h, w_ref[...], preferred_element_type=_F32)
    for p in range(D // LANES):
        sl = slice(p * LANES, (p + 1) * LANES)
        n_scr[:, sl] = head_pair_norm(y[:, sl], gains_ref[0:1, :])
        n_scr[:, D + p * LANES:D + (p + 1) * LANES] = head_pair_norm(
            y[:, D + p * LANES:D + (p + 1) * LANES], gains_ref[1:2, :])
        v_ref[p] = y[:, 2 * D + p * LANES:2 * D + (p + 1) * LANES].astype(_BF)

    @pl.when(pl.program_id(1) >= 0)
    def _():
        c, s = rope_ref[:, 0:128], rope_ref[:, 128:256]
        for p in range(D // LANES):
            vq = n_scr[:, p * LANES:(p + 1) * LANES]
            vk = n_scr[:, D + p * LANES:D + (p + 1) * LANES]
            q_ref[p] = (vq * c + pltpu.roll(vq, 64, 1) * s).astype(_BF)
            k_ref[p] = (vk * c + pltpu.roll(vk, 64, 1) * s).astype(_BF)


def _qkv(h, w, gains, rope, tm):
    B, S, D = h.shape
    tok = pl.BlockSpec((None, tm, D), lambda b, i: (b, i, 0))
    hp = pl.BlockSpec((None, D // LANES, tm, LANES), lambda b, i: (b, 0, i, 0))
    sds = jax.ShapeDtypeStruct((B, D // LANES, S, LANES), _BF)
    return pl.pallas_call(
        _qkv_kernel,
        grid=(B, S // tm),
        in_specs=[tok, pl.BlockSpec(w.shape, lambda b, i: (0, 0)),
                  pl.BlockSpec((2, LANES), lambda b, i: (0, 0)),
                  pl.BlockSpec((tm, 2 * LANES), lambda b, i: (i, 0))],
        out_specs=[hp, hp, hp],
        out_shape=[sds, sds, sds],
        scratch_shapes=[pltpu.VMEM((tm, 2 * D), _F32)],
        compiler_params=_cparams(("parallel", "parallel")),
        name="qkv_dil",
    )(h, w, gains, rope)


DIL_GROUP = 4


def _dil_kernel(fast_ref, q_ref, k_ref, v_ref, shift_ref, o_ref, tmp, qp, kp, vp, acc, mm, ll, accp, mmp,
                llp, bias_scr, *, tl):
    @pl.when(fast_ref[0] == 1)
    def _():
        _dil_body(q_ref, k_ref, v_ref, shift_ref, o_ref, tmp, qp, kp, vp, acc, mm, ll, accp, mmp, llp,
                  bias_scr, tl=tl, fixed_shift=True)

    @pl.when(fast_ref[0] != 1)
    def _():
        _dil_body(q_ref, k_ref, v_ref, shift_ref, o_ref, tmp, qp, kp, vp, acc, mm, ll, accp, mmp, llp,
                  bias_scr, tl=tl, fixed_shift=False)


def _dil_body(q_ref, k_ref, v_ref, shift_ref, o_ref, tmp, qp, kp, vp, acc, mm, ll, accp, mmp, llp,
              bias_scr, *, tl, fixed_shift):
    S = q_ref.shape[0]
    ng = S // DIL_GROUP
    pitch = DIL_GROUP + 1

    def spread(dst, src):
        for j in range(DIL_GROUP):
            dst[pl.ds(j, ng, stride=pitch), :] = src[pl.ds(j, ng, stride=DIL_GROUP), :]

    for src, dst in ((q_ref, qp), (k_ref, kp), (v_ref, vp)):
        tmp[...] = src[...].astype(_F32)
        spread(dst, tmp)
    lane = lax.broadcasted_iota(jnp.int32, (tl, LANES), 1)
    lo = lane < 64
    head_a = (lane & 63) < 32
    for bi, (window, d) in enumerate(DIL_PATTERNS):
        L = S // d
        half = window // (2 * d)
        t = min(tl, L)
        W = min(t + 2 * half, L)
        nt = L // t
        if d > 1:
            assert d % DIL_GROUP == 0
        sd = d * pitch // DIL_GROUP

        def rows(r, first, n, d=d, sd=sd):
            if d == 1:
                return pl.ds(first, n)
            return pl.ds(r + r // DIL_GROUP + sd * first, n, stride=sd)

        assert t % half == 0 and W <= t + 2 * half
        qa = lax.broadcasted_iota(jnp.int32, (t, W), 0)
        kb = lax.broadcasted_iota(jnp.int32, (t, W), 1)
        inside = shift_ref[0:1, 0:W] if fixed_shift else 0.0
        for case in range(3):
            ok = jnp.abs(kb - qa - case * half) <= half
            bias_scr[case, 0:t, 0:W] = jnp.where(ok, inside, NEG)

        def body(idx, carry, bi=bi, d=d, L=L, half=half, t=t, W=W, nt=nt, rows=rows):
            r = idx // nt
            l0 = (idx % nt) * t
            start = jnp.clip(l0 - half, 0, L - W)
            if d == 1:
                qt = q_ref[pl.ds(pl.multiple_of(l0, t), t), :]
                kw = k_ref[pl.ds(pl.multiple_of(start, 64), W), :]
                vw = v_ref[pl.ds(pl.multiple_of(start, 64), W), :]
            else:
                qt = qp[rows(r, l0, t), :].astype(_BF)
                kw = kp[rows(r, start, W), :].astype(_BF)
                vw = vp[rows(r, start, W), :].astype(_BF)
            lo_t = lo[:t]
            qa_t = head_a[:t]
            zq = jnp.zeros_like(qt)
            q2 = jnp.concatenate([jnp.where(qa_t, qt, zq), jnp.where(qa_t, zq, qt)], axis=0)
            s = lax.dot_general(q2, kw, (((1,), (1,)), ((), ())), preferred_element_type=_F32)
            bias = bias_scr[(l0 - start) // half, 0:t, 0:W]
            s = s + jnp.concatenate([bias, bias], axis=0)
            if fixed_shift:
                p = jnp.exp2(s)
            else:
                m = jnp.max(s, axis=-1, keepdims=True)
                p = jnp.exp2(s - m)
                m_new = jnp.where(lo_t, m[:t], m[t:])
            den = jnp.sum(p, axis=-1, keepdims=True)
            o2 = jnp.dot(p.astype(_BF), vw, preferred_element_type=_F32)
            o_new = jnp.where(lo_t, o2[:t], o2[t:])
            l_new = jnp.where(lo_t, den[:t], den[t:])
            tok = rows(r, l0, t)
            if d == 1:
                assert bi == 0
                acc[tok, :] = o_new
                ll[tok, :] = l_new
                if not fixed_shift:
                    mm[tok, :] = m_new
            elif fixed_shift:
                accp[tok, :] = accp[tok, :] + o_new
                llp[tok, :] = llp[tok, :] + l_new
            else:
                m_old = mmp[tok, :]
                mx = jnp.maximum(m_old, m_new)
                a_old = jnp.exp2(m_old - mx)
                a_new = jnp.exp2(m_new - mx)
                accp[tok, :] = accp[tok, :] * a_old + o_new * a_new
                llp[tok, :] = llp[tok, :] * a_old + l_new * a_new
                mmp[tok, :] = mx
            return carry

        lax.fori_loop(0, d * nt, body, 0, unroll=min(32 if fixed_shift else 8, d * nt))
        if bi == 0:
            spread(accp, acc)
            spread(llp, ll)
            if not fixed_shift:
                spread(mmp, mm)
    for j in range(DIL_GROUP):
        grp = pl.ds(j, ng, stride=pitch)
        tmp[pl.ds(j, ng, stride=DIL_GROUP), :] = accp[grp, :] / llp[grp, :]
    o_ref[...] = tmp[...].astype(o_ref.dtype)


def _dilated(fast, shift, q, k, v, tl):
    B, P, S, _ = q.shape
    wmax = tl + max(w // d for w, d in DIL_PATTERNS)
    spec = pl.BlockSpec((None, None, S, LANES), lambda b, p, f: (b, p, 0, 0))
    scr = pltpu.VMEM((S, LANES), _F32)
    scrp = pltpu.VMEM((S // DIL_GROUP * (DIL_GROUP + 1), LANES), _F32)
    grid_spec = pltpu.PrefetchScalarGridSpec(
        num_scalar_prefetch=1,
        grid=(B, P),
        in_specs=[spec, spec, spec, pl.BlockSpec((1, wmax), lambda b, p, f: (0, 0))],
        out_specs=spec,
        scratch_shapes=[scr, scrp, scrp, scrp, scr, scr, scr, scrp, scrp, scrp,
                        pltpu.VMEM((3, tl, wmax), _F32)],
    )
    return pl.pallas_call(
        functools.partial(_dil_kernel, tl=tl),
        grid_spec=grid_spec,
        out_shape=jax.ShapeDtypeStruct((B, P, S, LANES), _BF),
        compiler_params=_cparams(("parallel", "parallel")),
        name="dil_attn",
    )(fast, q, k, v, jnp.full((1, wmax), shift, _F32))


def _moe_kernel(te_ref, tv_ref, src_hbm, dst_hbm, h_hbm, wg_ref, wu_ref, wd_ref, out_hbm,
                xbuf, xbf, a_scr, ybuf, src_s0, src_s1, dst_s0, dst_s1, gsem, ssem, isem,
                *, tf, dump0, n_dump):
    i = pl.program_id(0)
    nt = pl.num_programs(0)
    tm = xbf.shape[0]
    slot = i % 2

    def is_valid(t):
        return jnp.logical_and(jnp.logical_and(t >= 0, t < nt), tv_ref[jnp.clip(t, 0, nt - 1)] == 1)

    def both(a, b):
        return jnp.logical_and(a, b)

    valid = is_valid(i)
    prev_valid = is_valid(i - 1)

    src_bufs = (src_s0, src_s1)
    dst_bufs = (dst_s0, dst_s1)

    def src_copy(tile, par):
        return pltpu.make_async_copy(src_hbm.at[tile], src_bufs[par], isem.at[par])

    def dst_copy(tile, par):
        return pltpu.make_async_copy(dst_hbm.at[tile], dst_bufs[par], isem.at[2 + par])

    def tile_at(first):
        return pl.ds(pl.multiple_of(first, SUBLANES), SUBLANES)

    def gather_row(r, par):
        return pltpu.make_async_copy(h_hbm.at[tile_at(src_bufs[par][r])],
                                     xbuf.at[par, tile_at(r * SUBLANES)], gsem.at[par])

    def scatter_row(r, par):
        return pltpu.make_async_copy(ybuf.at[par, tile_at(r * SUBLANES)],
                                     out_hbm.at[tile_at(dst_bufs[par][r])], ssem.at[par])

    def gather_wait(par):
        return pltpu.make_async_copy(h_hbm.at[pl.ds(0, tm * SUBLANES)], xbuf.at[par], gsem.at[par])

    def scatter_wait(par):
        return pltpu.make_async_copy(ybuf.at[par], out_hbm.at[pl.ds(0, tm * SUBLANES)], ssem.at[par])

    def for_rows(fn):
        def body(r, c):
            fn(r)
            return c
        lax.fori_loop(0, tm, body, 0, unroll=8)

    @pl.when(i == 0)
    def _():
        ybuf[...] = jnp.zeros_like(ybuf)
        for k in range(n_dump):
            pltpu.make_async_copy(ybuf.at[1], out_hbm.at[pl.ds((dump0 + k * tm) * SUBLANES, tm * SUBLANES)],
                                  ssem.at[1]).start()
        for k in range(n_dump):
            scatter_wait(1).wait()

    @pl.when(both(i == 0, valid))
    def _():
        src_copy(0, 0).start()
        src_copy(0, 0).wait()
        for_rows(lambda r: gather_row(r, 0).start())
        src_copy(1, 1).start()

    def phase1(par, with_scatter):
        xbf[...] = _tiles_to_rows(xbuf.at[par], tm).astype(_BF)
        n_chunk = -(-wg_ref.shape[1] // tf)
        per = -(-tm // n_chunk)

        def after(c):
            for r in range(c * per, min((c + 1) * per, tm)):
                gather_row(r, 1 - par).start()
                if with_scatter:
                    scatter_row(r, 1 - par).start()

        _swiglu_hidden(xbf[...], wg_ref, wu_ref, a_scr, tf, after)

    for par in range(2):
        here = slot == par

        @pl.when(both(valid, here))
        def _(par=par):
            dst_copy(i, par).start()
            src_copy(i + 1, 1 - par).wait()

        @pl.when(both(prev_valid, here))
        def _(par=par):
            dst_copy(i - 1, 1 - par).wait()

        @pl.when(both(jnp.logical_or(both(i == 0, valid), prev_valid), here))
        def _(par=par):
            gather_wait(par).wait()

        @pl.when(both(both(valid, prev_valid), here))
        def _(par=par):
            phase1(par, True)

        if par == 0:
            @pl.when(both(both(valid, jnp.logical_not(prev_valid)), here))
            def _():
                phase1(0, False)

        @pl.when(both(both(jnp.logical_not(valid), prev_valid), here))
        def _(par=par):
            for_rows(lambda r: scatter_row(r, 1 - par).start())

        @pl.when(both(is_valid(i - 2), here))
        def _(par=par):
            scatter_wait(par).wait()

        @pl.when(both(valid, here))
        def _(par=par):
            _rows_to_tiles(ybuf.at[par], jnp.dot(a_scr[...], wd_ref[...], preferred_element_type=_F32))

        @pl.when(both(is_valid(i + 1), here))
        def _(par=par):
            src_copy(i + 2, par).start()


def _moe(h, tile_e, tile_v, src, dst, wg, wu, wd, n_out_rows, tm, tf):
    T = h.shape[0] // SUBLANES
    D = wg.shape[1]
    nt = tile_e.shape[0]
    F = wg.shape[2]
    dump0 = T * TOP_K
    once = pl.Buffered(1)
    grid_spec = pltpu.PrefetchScalarGridSpec(
        num_scalar_prefetch=2,
        grid=(nt,),
        in_specs=[pl.BlockSpec(memory_space=pl.ANY), pl.BlockSpec(memory_space=pl.ANY),
                  pl.BlockSpec(memory_space=pl.ANY),
                  pl.BlockSpec((None, D, F), lambda i, te, tv: (te[i], 0, 0), pipeline_mode=once),
                  pl.BlockSpec((None, D, F), lambda i, te, tv: (te[i], 0, 0), pipeline_mode=once),
                  pl.BlockSpec((None, F, D), lambda i, te, tv: (te[i], 0, 0), pipeline_mode=once)],
        out_specs=pl.BlockSpec(memory_space=pl.ANY),
        scratch_shapes=[pltpu.VMEM((2, tm * SUBLANES, LANES), _F32), pltpu.VMEM((tm, D), _BF),
                        pltpu.VMEM((tm, F), _BF), pltpu.VMEM((2, tm * SUBLANES, LANES), _F32),
                        pltpu.SMEM((tm,), jnp.int32), pltpu.SMEM((tm,), jnp.int32),
                        pltpu.SMEM((tm,), jnp.int32), pltpu.SMEM((tm,), jnp.int32),
                        pltpu.SemaphoreType.DMA((2,)), pltpu.SemaphoreType.DMA((2,)),
                        pltpu.SemaphoreType.DMA((4,))],
    )
    return pl.pallas_call(
        functools.partial(_moe_kernel, tf=tf, dump0=dump0, n_dump=(n_out_rows - dump0) // tm),
        grid_spec=grid_spec,
        out_shape=jax.ShapeDtypeStruct((n_out_rows * SUBLANES, LANES), _F32),
        compiler_params=_cparams(("arbitrary",)),
        name="moe_experts",
    )(tile_e, tile_v, src, dst, h, wg, wu, wd)


def _route_plan(route, T, tm):
    A = T * TOP_K
    e_flat = route[:, :TOP_K].astype(jnp.int32).reshape(A)
    order = jnp.argsort(e_flat, stable=True).astype(jnp.int32)
    counts = jnp.sum((e_flat[:, None] == jnp.arange(N_EXPERTS, dtype=jnp.int32)[None, :]).astype(jnp.int32), axis=0)
    starts = jnp.cumsum(counts) - counts
    pcounts = (counts + tm - 1) // tm * tm
    pends = jnp.cumsum(pcounts)
    pstarts = pends - pcounts
    nt = A // tm + N_EXPERTS + 2
    tile0 = jnp.arange(nt, dtype=jnp.int32) * tm
    tile_v = (tile0 < pends[-1]).astype(jnp.int32)
    last_valid = jnp.maximum(pends[-1] // tm - 1, 0)
    tile_e_raw = jnp.minimum(jnp.searchsorted(pends, tile0, side="right"), N_EXPERTS - 1).astype(jnp.int32)
    tile_e = jnp.where(tile_v == 1, tile_e_raw, tile_e_raw[last_valid])
    r = jnp.arange(nt * tm, dtype=jnp.int32)
    e_r = jnp.repeat(tile_e, tm)
    within = r - pstarts[e_r]
    ok = jnp.logical_and(within < counts[e_r], jnp.repeat(tile_v, tm) == 1)
    a = order[jnp.clip(starts[e_r] + within, 0, A - 1)]
    src = jnp.where(ok, a // TOP_K, 0).astype(jnp.int32)
    dump = A + e_r * tm + jnp.clip(within - counts[e_r], 0, tm - 1)
    dst = jnp.where(ok, (a % TOP_K) * T + a // TOP_K, dump).astype(jnp.int32)
    return (tile_e, tile_v, (src * SUBLANES).reshape(nt, tm), (dst * SUBLANES).reshape(nt, tm),
            A + N_EXPERTS * tm)


def _combine_kernel(x_ref, y1_ref, y2_ref, r_ref, mod_ref, o_ref):
    tm = x_ref.shape[0]
    y = _tiles_to_rows(y1_ref, tm) * r_ref[:, 2:3] + _tiles_to_rows(y2_ref, tm) * r_ref[:, 3:4]
    o_ref[...] = x_ref[...] + mod_ref[5:6, :] * y


def _combine(x, y2, route, mod, tm):
    B, S, D = x.shape
    T = B * S
    per_b = S // tm
    out = pl.pallas_call(
        _combine_kernel,
        grid=(T // tm,),
        in_specs=[pl.BlockSpec((tm, D), lambda i: (i, 0)),
                  pl.BlockSpec((tm * SUBLANES, LANES), lambda i: (i, 0)),
                  pl.BlockSpec((tm * SUBLANES, LANES), lambda i: (T // tm + i, 0)),
                  pl.BlockSpec((tm, LANES), lambda i: (i, 0)),
                  pl.BlockSpec((None, 6, D), lambda i: (i // per_b, 0, 0))],
        out_specs=pl.BlockSpec((tm, D), lambda i: (i, 0)),
        out_shape=jax.ShapeDtypeStruct((T, D), _F32),
        compiler_params=_cparams(("parallel",)),
        name="moe_combine",
    )(x.reshape(T, D), y2, y2, route.reshape(T, LANES), mod)
    return out.reshape(B, S, D)


def _pad_cols(a, w):
    return jnp.pad(a, ((0, 0), (0, w - a.shape[1])))


def _rope_tables_even(S):
    pos = jnp.arange(S, dtype=jnp.int32)
    inv = ROPE_THETA ** (-jnp.arange(0, 32, 2, dtype=_F32) / 32)
    def cs(p):
        ang = p.astype(_F32)[:, None] * inv[None, :]
        return jnp.cos(ang), jnp.sin(ang)
    one = lambda w: jnp.ones((S, w), _F32)
    zero = lambda w: jnp.zeros((S, w), _F32)
    c, s = cs(pos)
    ca = jnp.concatenate([c, one(48), c, one(48)], 1)
    sa = jnp.concatenate([-s, zero(48), s, zero(48)], 1)
    cr, sr = cs(pos // GRID_W)
    cc, sc = cs(pos % GRID_W)
    cb = jnp.concatenate([cr, cc, one(32), cr, cc, one(32)], 1)
    sb = jnp.concatenate([-sr, -sc, zero(32), sr, sc, zero(32)], 1)
    return jnp.concatenate([ca, sa, cb, sb], 1)


def _slot_maps():
    r = MLA_ROPE // 2
    mla = ([MLA_NOPE + i for i in range(r)] + list(range(0, 64 - r))
           + [MLA_NOPE + r + i for i in range(r)] + list(range(64 - r, MLA_NOPE)))
    mla += [-1] * (LANES - len(mla))
    q = HEAD_DIM // 4
    gqa = (list(range(0, q)) + list(range(2 * q, 3 * q)) + [-1] * (64 - 2 * q)
           + list(range(q, 2 * q)) + list(range(3 * q, 4 * q)) + [-1] * (64 - 2 * q))
    return mla, gqa


def _to_slots(a, lane_map):
    idx = jnp.asarray([max(i, 0) for i in lane_map], jnp.int32)
    keep = jnp.asarray([1.0 if i >= 0 else 0.0 for i in lane_map], a.dtype)
    return jnp.take(a, idx, axis=-1) * keep


def _rope_tables_odd(S):
    pos = jnp.arange(S, dtype=_F32)
    inv = ROPE_THETA ** (-jnp.arange(0, HEAD_DIM, 2, dtype=_F32) / HEAD_DIM)
    ang = pos[:, None] * inv[None, :]
    c, s = jnp.cos(ang), jnp.sin(ang)
    return jnp.concatenate([c, c, c, c, -s, -s, s, s], 1)


def _tiles(S):
    return dict(tm_pre=min(S, 512), tq=min(S, 2048), tm_o=min(S, 512),
                tm_ffn=min(S, 512), tf_ffn=512, tm_qkv=min(S, 512),
                tl=128, tm_moe=min(S, 512), tf_moe=512, tm_c=min(S, 512))


def kernel(x, c, ada_even_w, ada_even_b, norm_even_mix, norm_even_ffn, even_w_in, mla_q_norm, mla_w_uq, mla_kv_norm, mla_w_ukv, mla_q_gain, mla_k_gain, gqa_q_gain, gqa_k_gain, even_w_out, ffn_w_gate, ffn_w_up, ffn_w_down, ada_odd_w, ada_odd_b, norm_odd_mix, norm_odd_ffn, dil_w_qkv, dil_q_gain, dil_k_gain, dil_w_out, moe_router, moe_w_gate, moe_w_up, moe_w_down):
    B, S, D = x.shape
    T = B * S
    cfg = _tiles(S)

    mod_e = _ada_mod(c, ada_even_w[0], ada_even_b[0]).reshape(B, 6, D)
    mod_o = _ada_mod(c, ada_odd_w[0], ada_odd_b[0]).reshape(B, 6, D)

    w = even_w_in[0]
    sp = [MLA_Q_RANK, MLA_Q_RANK + MLA_KV_RANK, MLA_Q_RANK + MLA_KV_RANK + MLA_ROPE]
    sp.append(sp[-1] + GQA_HEADS * HEAD_DIM)
    sp.append(sp[-1] + GQA_KV_HEADS * HEAD_DIM)
    w_cq, w_ckv, w_kpe = w[:, :sp[0]], w[:, sp[0]:sp[1]], w[:, sp[1]:sp[2]]
    w_qb, w_kb, w_vb = w[:, sp[2]:sp[3]], w[:, sp[3]:sp[4]], w[:, sp[4]:]
    mla_map, gqa_map = _slot_maps()
    na = MLA_NOPE + MLA_ROPE
    nope_only = [i if 0 <= i < MLA_NOPE else -1 for i in mla_map]
    rope_only = [i - MLA_NOPE if i >= MLA_NOPE else -1 for i in mla_map]
    gslots = lambda a, n: _to_slots(a.reshape(D, n, HEAD_DIM), gqa_map).reshape(D, n * LANES)
    w_in = jnp.concatenate([w_cq, w_ckv, _to_slots(w_kpe, rope_only), gslots(w_qb, GQA_HEADS),
                            gslots(w_kb, GQA_KV_HEADS)], axis=1).astype(_BF)
    w_uq = _to_slots(mla_w_uq[0].reshape(MLA_Q_RANK, MLA_HEADS, na), mla_map)
    w_uq = w_uq.reshape(MLA_Q_RANK, MLA_HEADS * LANES).astype(_BF)
    w_ukv = mla_w_ukv[0].reshape(MLA_KV_RANK, MLA_HEADS, MLA_NOPE + MLA_V)
    w_uk = _to_slots(w_ukv[:, :, :MLA_NOPE], nope_only).reshape(MLA_KV_RANK, MLA_HEADS * LANES).astype(_BF)
    w_uvt = w_ukv[:, :, MLA_NOPE:].reshape(MLA_KV_RANK, MLA_HEADS * MLA_V).T.astype(_BF)
    w_vbt = w_vb.T.astype(_BF)
    g_rows = [_to_slots(mla_q_gain[0], mla_map) * (na ** -0.5 * LOG2E), _to_slots(mla_k_gain[0], mla_map),
              _to_slots(gqa_q_gain[0], gqa_map) * (HEAD_DIM ** -0.5 * LOG2E), _to_slots(gqa_k_gain[0], gqa_map)]
    bound_a = na * jnp.max(jnp.abs(g_rows[0])) * jnp.max(jnp.abs(g_rows[1]))
    bound_b = HEAD_DIM * jnp.max(jnp.abs(g_rows[2])) * jnp.max(jnp.abs(g_rows[3]))
    fast_a, fast_b = bound_a <= SCORE_BOUND_MAX, bound_b <= SCORE_BOUND_MAX
    last_lane = jnp.zeros((LANES,), _F32).at[LANES - 1].set(1.0)
    g_rows += [last_lane, last_lane * jnp.where(fast_a, -bound_a, 0.0),
               last_lane, last_lane * jnp.where(fast_b, -bound_b, 0.0)]
    gains_e = jnp.stack(g_rows)
    rope_e = _rope_tables_even(S)
    q_all, k_all, vt_all = _pre_even(
        x, mod_e, norm_even_mix[0].reshape(1, D), w_in, mla_q_norm[0].reshape(1, -1), w_uq,
        mla_kv_norm[0].reshape(1, -1), w_uk, w_uvt, w_vbt, gains_e, rope_e, cfg["tm_pre"])
    as_flag = lambda f: f.astype(jnp.int32).reshape(1)
    o_a = _attention(as_flag(fast_a), q_all, k_all, vt_all, q_off=0, k_off=0, shared_kv=False, tq=cfg["tq"])
    o_b = _attention(as_flag(fast_b), q_all, k_all, vt_all, q_off=MLA_HEADS, k_off=MLA_HEADS,
                     shared_kv=True, tq=cfg["tq"])
    x2, h2 = _ffn(x, o_a, o_b, even_w_out[0].astype(_BF), mod_e, norm_even_ffn[0].reshape(1, D), mod_o,
                  norm_odd_mix[0].reshape(1, D), ffn_w_gate[0].astype(_BF), ffn_w_up[0].astype(_BF),
                  ffn_w_down[0].astype(_BF), cfg["tm_ffn"], cfg["tf_ffn"])

    hh = HEAD_DIM // 2
    pair = lambda v: jnp.concatenate([v[:hh], v[:hh], v[hh:], v[hh:]])
    gains_o = jnp.stack([pair(dil_q_gain[0]) * (HEAD_DIM ** -0.5 * LOG2E), pair(dil_k_gain[0])])
    wq, wk, wv = jnp.split(dil_w_qkv[0], 3, axis=1)
    perm = lambda w: w.reshape(D, D // LANES, 2, 2, hh).transpose(0, 1, 3, 2, 4).reshape(D, D)
    w_qkv = jnp.concatenate([perm(wq), perm(wk), wv], axis=1).astype(_BF)
    qd, kd, vd = _qkv(h2, w_qkv, gains_o, _rope_tables_odd(S), cfg["tm_qkv"])
    bound_d = HEAD_DIM * jnp.max(jnp.abs(gains_o[0])) * jnp.max(jnp.abs(gains_o[1]))
    fast_d = bound_d <= SCORE_BOUND_MAX
    o_d = _dilated(as_flag(fast_d), jnp.where(fast_d, -bound_d, 0.0), qd, kd, vd, cfg["tl"])
    r32 = _pad_cols(moe_router[0], LANES)
    r_hi = r32.astype(_BF)
    r_lo = (r32 - r_hi.astype(_F32)).astype(_BF)
    x3, h3, route = _oproj(x2, mod_o, norm_odd_ffn[0].reshape(1, D), dil_w_out[0].astype(_BF), [o_d],
                           cfg["tm_o"], _F32, router=jnp.stack([r_hi, r_lo]))
    tm = cfg["tm_moe"]
    tile_e, tile_v, src, dst, n_rows = _route_plan(route.reshape(T, LANES), T, tm)
    y2 = _moe(h3, tile_e, tile_v, src, dst, moe_w_gate[0].astype(_BF),
              moe_w_up[0].astype(_BF), moe_w_down[0].astype(_BF), n_rows, tm, cfg["tf_moe"])
    return _combine(x3, y2, route, mod_o, cfg["tm_c"])
```

```python
import functools
import math

import jax
import jax.numpy as jnp
from jax import lax
from jax.experimental import pallas as pl
from jax.experimental.pallas import tpu as pltpu

_BF = jnp.bfloat16
_F32 = jnp.float32

GRID_W = 64
HEAD_DIM = 64
ROPE_THETA = 10000.0
EPS = 1e-6
MLA_HEADS = 8
MLA_Q_RANK = 256
MLA_KV_RANK = 128
MLA_NOPE = 64
MLA_ROPE = 32
MLA_V = 64
GQA_HEADS = 8
GQA_KV_HEADS = 2
DIL_PATTERNS = ((128, 1), (512, 4), (2048, 16))
N_EXPERTS = 8
TOP_K = 2
NEG = -1e30
LOG2E = math.log2(math.e)

LANES = 128
VMEM_LIMIT = 56 * 1024 * 1024


def _cparams(sem, vmem=VMEM_LIMIT):
    return pltpu.CompilerParams(dimension_semantics=sem, vmem_limit_bytes=vmem)


def _silu(x):
    return x / (1.0 + jnp.exp(-x))


def _modulate(x, g, shift, scale):
    ms = jnp.mean(x * x, axis=-1, keepdims=True)
    return x * lax.rsqrt(ms + EPS) * g * (1.0 + scale) + shift


def _norm(v, n, g):
    return v * lax.rsqrt(jnp.sum(v * v, axis=-1, keepdims=True) * (1.0 / n) + EPS) * g


SUBLANES = 8


def _rows_to_tiles(ref, x):
    tm = x.shape[0]
    for c in range(SUBLANES):
        ref[pl.ds(c, tm, stride=SUBLANES), :] = x[:, c * LANES:(c + 1) * LANES]


def _tiles_to_rows(ref, tm):
    return jnp.concatenate([ref[pl.ds(c, tm, stride=SUBLANES), :] for c in range(SUBLANES)], axis=1)


def _rope(x, c, sa, sb, k):
    return x * c + pltpu.roll(x, LANES - k, 1) * sa + pltpu.roll(x, k, 1) * sb


def _mod_kernel(c_ref, w_ref, b_ref, o_ref):
    sc = _silu(c_ref[...])
    o_ref[...] = jnp.dot(sc.astype(_BF), w_ref[...].astype(_BF),
                         preferred_element_type=_F32) + b_ref[...]


def _ada_mod(c, w, b):
    B, D = c.shape
    N = w.shape[1]
    tn = min(N, 1536)
    return pl.pallas_call(
        _mod_kernel,
        grid=(N // tn,),
        in_specs=[pl.BlockSpec((B, D), lambda j: (0, 0)),
                  pl.BlockSpec((D, tn), lambda j: (0, j)),
                  pl.BlockSpec((1, tn), lambda j: (0, j))],
        out_specs=pl.BlockSpec((B, tn), lambda j: (0, j)),
        out_shape=jax.ShapeDtypeStruct((B, N), _F32),
        compiler_params=_cparams(("arbitrary",)),
        name="ada_mod",
    )(c, w, b.reshape(1, N))


def _pre_even_kernel(x_ref, mod_ref, g_ref, w_in_ref, qn_ref, w_uq_ref, kvn_ref, w_uk_ref, w_uvt_ref,
                     w_vbt_ref, gains_ref, rope_ref, q_ref, k_ref, vt_ref, nq_scr, nk_scr):
    h = _modulate(x_ref[...], g_ref[...], mod_ref[0:1, :], mod_ref[1:2, :]).astype(_BF)
    y = jnp.dot(h, w_in_ref[...], preferred_element_type=_F32)
    cqn = _norm(y[:, 0:256], MLA_Q_RANK, qn_ref[...]).astype(_BF)
    qa = jnp.dot(cqn, w_uq_ref[...], preferred_element_type=_F32)
    ckvn = _norm(y[:, 256:384], MLA_KV_RANK, kvn_ref[...]).astype(_BF)
    kn = jnp.dot(ckvn, w_uk_ref[...], preferred_element_type=_F32)
    kpe = y[:, 384:512]
    nt_dims = (((1,), (1,)), ((), ()))
    vt_a = lax.dot_general(w_uvt_ref[...], ckvn, nt_dims, preferred_element_type=_F32).astype(_BF)
    vt_b = lax.dot_general(w_vbt_ref[...], h, nt_dims, preferred_element_type=_F32).astype(_BF)
    tm = x_ref.shape[0]
    ones_blk = jnp.where(lax.broadcasted_iota(jnp.int32, (LANES - MLA_V, tm), 0) == 0, 1.0, 0.0).astype(_BF)
    for hh in range(MLA_HEADS + GQA_KV_HEADS):
        src = vt_a[hh * MLA_V:(hh + 1) * MLA_V] if hh < MLA_HEADS else \
            vt_b[(hh - MLA_HEADS) * HEAD_DIM:(hh - MLA_HEADS + 1) * HEAD_DIM]
        vt_ref[hh * LANES:hh * LANES + MLA_V, :] = src
        vt_ref[hh * LANES + MLA_V:(hh + 1) * LANES, :] = ones_blk
    ca, sa = rope_ref[:, 0:128], rope_ref[:, 128:256]
    cb, sb = rope_ref[:, 256:384], rope_ref[:, 384:512]
    gqa, gka = gains_ref[0:1, :], gains_ref[1:2, :]
    gqb, gkb = gains_ref[2:3, :], gains_ref[3:4, :]
    na = MLA_NOPE + MLA_ROPE

    for hh in range(MLA_HEADS):
        sl = slice(hh * LANES, (hh + 1) * LANES)
        nq_scr[:, sl] = _norm(qa[:, sl], na, gqa)
        nk_scr[:, sl] = _norm(kn[:, sl] + kpe, na, gka)
    for hh in range(GQA_HEADS):
        src = slice(512 + hh * LANES, 512 + (hh + 1) * LANES)
        dst = slice((MLA_HEADS + hh) * LANES, (MLA_HEADS + hh + 1) * LANES)
        nq_scr[:, dst] = _norm(y[:, src], HEAD_DIM, gqb)
    for g in range(GQA_KV_HEADS):
        src = slice(1536 + g * LANES, 1536 + (g + 1) * LANES)
        dst = slice((MLA_HEADS + g) * LANES, (MLA_HEADS + g + 1) * LANES)
        nk_scr[:, dst] = _norm(y[:, src], HEAD_DIM, gkb)

    @pl.when(pl.program_id(1) >= 0)
    def _():
        def rope(ref, out, n_a, n_all, row):
            for hh in range(n_all):
                sl = slice(hh * LANES, (hh + 1) * LANES)
                c, s = (ca, sa) if hh < n_a else (cb, sb)
                b = gains_ref[row:row + 1, :] if hh < n_a else gains_ref[row + 2:row + 3, :]
                v = ref[:, sl]
                out[:, sl] = (v * c + pltpu.roll(v, 64, 1) * s + b).astype(_BF)

        rope(nq_scr, q_ref, MLA_HEADS, MLA_HEADS + GQA_HEADS, 4)
        rope(nk_scr, k_ref, MLA_HEADS, MLA_HEADS + GQA_KV_HEADS, 5)


def _pre_even(x, mod, g, w_in, qn, w_uq, kvn, w_uk, w_uvt, w_vbt, gains, rope, tm):
    B, S, D = x.shape
    nq = (MLA_HEADS + GQA_HEADS) * LANES
    nk = (MLA_HEADS + GQA_KV_HEADS) * LANES
    nvt = (MLA_HEADS + GQA_KV_HEADS) * LANES
    full = lambda a: pl.BlockSpec(a.shape, lambda b, i: (0,) * a.ndim)
    tok = lambda w: pl.BlockSpec((None, tm, w), lambda b, i: (b, i, 0))
    return pl.pallas_call(
        _pre_even_kernel,
        grid=(B, S // tm),
        in_specs=[tok(D),
                  pl.BlockSpec((None, 6, D), lambda b, i: (b, 0, 0)),
                  full(g), full(w_in), full(qn), full(w_uq), full(kvn), full(w_uk), full(w_uvt),
                  full(w_vbt), full(gains),
                  pl.BlockSpec((tm, rope.shape[1]), lambda b, i: (i, 0))],
        out_specs=[tok(nq), tok(nk), pl.BlockSpec((None, None, nvt, tm), lambda b, i: (b, i, 0, 0))],
        out_shape=[jax.ShapeDtypeStruct((B, S, nq), _BF), jax.ShapeDtypeStruct((B, S, nk), _BF),
                   jax.ShapeDtypeStruct((B, S // tm, nvt, tm), _BF)],
        scratch_shapes=[pltpu.VMEM((tm, nq), _F32), pltpu.VMEM((tm, nk), _F32)],
        compiler_params=_cparams(("parallel", "parallel")),
        name="pre_even",
    )(x, mod, g, w_in, qn, w_uq, kvn, w_uk, w_uvt, w_vbt, gains, rope)


SCORE_BOUND_MAX = 60.0


def _attn_kernel(fast_ref, q_ref, k_ref, vt_ref, o_ref, s_scr, *, shared_kv):
    @pl.when(fast_ref[0] == 1)
    def _():
        _attn_fixed_shift(q_ref, k_ref, vt_ref, o_ref, shared_kv=shared_kv)

    @pl.when(fast_ref[0] != 1)
    def _():
        _attn_running_max(q_ref, k_ref, vt_ref, o_ref, s_scr, shared_kv=shared_kv)


def _attn_fixed_shift(q_ref, k_ref, vt_ref, o_ref, *, shared_kv):
    tq = q_ref.shape[0]
    nk, _, tk = vt_ref.shape
    qs = [q_ref[:, hh * LANES:(hh + 1) * LANES] for hh in range(2)]

    def body(j, carry):
        r0 = pl.multiple_of(j * tk, tk)
        out = []
        for hh in range(2):
            kc = 0 if shared_kv else hh * LANES
            k = k_ref[pl.ds(r0, tk), kc:kc + LANES]
            st = lax.dot_general(k, qs[hh], (((1,), (1,)), ((), ())), preferred_element_type=_F32)
            p = jnp.exp2(st).astype(_BF)
            out.append(carry[hh] + jnp.dot(vt_ref[j, kc:kc + LANES, :], p, preferred_element_type=_F32))
        return tuple(out)

    z = jnp.zeros((LANES, tq), _F32)
    res = lax.fori_loop(0, nk, body, (z, z), unroll=True)
    nv = HEAD_DIM
    halves = [acc[0:nv, :] / acc[nv:nv + 1, :] for acc in res]
    o_ref[...] = jnp.concatenate(halves, axis=0).T.astype(o_ref.dtype)


def _attn_running_max(q_ref, k_ref, vt_ref, o_ref, s_scr, *, shared_kv):
    tq = q_ref.shape[0]
    nk, _, tk = vt_ref.shape
    qs = [q_ref[:, hh * LANES:(hh + 1) * LANES] for hh in range(2)]

    def scores(j, hh):
        r0 = j * tk if isinstance(j, int) else pl.multiple_of(j * tk, tk)
        kc = 0 if shared_kv else hh * LANES
        k = k_ref[pl.ds(r0, tk), kc:kc + LANES]
        return lax.dot_general(k, qs[hh], (((1,), (1,)), ((), ())), preferred_element_type=_F32)

    def put_scores(j, hh, slot):
        st = scores(j, hh)
        s_scr[hh, slot] = st
        return jnp.max(st, axis=0, keepdims=True)

    def step(j, hh, carry, cur, last=False):
        m, mc, acc = carry
        mc_next = mc if last else put_scores(j + 1, hh, 1 - cur)
        st = s_scr[hh, cur]
        m_new = jnp.maximum(m, mc)
        alpha = jnp.exp2(m - m_new)
        p = jnp.exp2(st - m_new).astype(_BF)
        vc = 0 if shared_kv else hh * LANES
        acc = alpha * acc + jnp.dot(vt_ref[j, vc:vc + LANES, :], p, preferred_element_type=_F32)
        return m_new, mc_next, acc

    def body(jj, carry):
        c = list(carry)
        for sub in range(2):
            for hh in range(2):
                c[hh] = step(2 * jj + sub, hh, c[hh], sub)
        return tuple(c)

    init = tuple((jnp.full((1, tq), NEG, _F32), put_scores(0, hh, 0), jnp.zeros((LANES, tq), _F32))
                 for hh in range(2))
    res = lax.fori_loop(0, nk // 2 - 1, body, init)
    res = list(res)
    for sub in range(2):
        for hh in range(2):
            res[hh] = step(nk - 2 + sub, hh, res[hh], sub, last=(sub == 1))
    nv = HEAD_DIM
    halves = [acc[0:nv, :] / acc[nv:nv + 1, :] for (_, _, acc) in res]
    o_ref[...] = jnp.concatenate(halves, axis=0).T.astype(o_ref.dtype)


def _attention(fast, q, k, vt, *, q_off, k_off, shared_kv, tq):
    B, S, _ = q.shape
    nk, _, tk = vt.shape[1:]
    assert nk % 2 == 0 and nk >= 2
    npairs = 4
    if shared_kv:
        kspec = pl.BlockSpec((None, S, LANES), lambda b, p, i, f: (b, 0, k_off + p // 2))
        vspec = pl.BlockSpec((None, nk, LANES, tk), lambda b, p, i, f: (b, 0, k_off + p // 2, 0))
    else:
        kspec = pl.BlockSpec((None, S, 2 * LANES), lambda b, p, i, f: (b, 0, k_off // 2 + p))
        vspec = pl.BlockSpec((None, nk, 2 * LANES, tk), lambda b, p, i, f: (b, 0, k_off // 2 + p, 0))
    grid_spec = pltpu.PrefetchScalarGridSpec(
        num_scalar_prefetch=1,
        grid=(B, npairs, S // tq),
        in_specs=[pl.BlockSpec((None, tq, 2 * LANES), lambda b, p, i, f: (b, i, q_off // 2 + p)),
                  kspec, vspec],
        out_specs=pl.BlockSpec((None, tq, LANES), lambda b, p, i, f: (b, i, p)),
        scratch_shapes=[pltpu.VMEM((2, 2, tk, tq), _F32)],
    )
    return pl.pallas_call(
        functools.partial(_attn_kernel, shared_kv=shared_kv),
        grid_spec=grid_spec,
        out_shape=jax.ShapeDtypeStruct((B, S, npairs * LANES), _BF),
        compiler_params=_cparams(("parallel", "parallel", "parallel")),
        name="attn_gqa" if shared_kv else "attn_mla",
    )(fast, q, k, vt)


def _oproj_kernel(*refs, n_in, route):
    x_ref, mod_ref, g_ref, w_ref = refs[0], refs[1], refs[2], refs[3]
    o_refs = refs[4:4 + n_in]
    pos = 4 + n_in
    if route:
        r_ref = refs[pos]
        pos += 1
    x_out, h_out = refs[pos], refs[pos + 1]
    y = None
    off = 0
    for o_ref in o_refs:
        if len(o_ref.shape) == 3:
            o = jnp.concatenate([o_ref[p] for p in range(o_ref.shape[0])], axis=1)
        else:
            o = o_ref[...]
        w = o.shape[1]
        t = jnp.dot(o, w_ref[off:off + w, :], preferred_element_type=_F32)
        y = t if y is None else y + t
        off += w
    x1 = x_ref[...] + mod_ref[2:3, :] * y
    x_out[...] = x1
    h = _modulate(x1, g_ref[...], mod_ref[3:4, :], mod_ref[4:5, :])
    if route:
        _rows_to_tiles(h_out, h)
    else:
        h_out[...] = h.astype(h_out.dtype)
    if route:
        route_out = refs[pos + 2]
        hh = h.astype(_BF)
        hl = (h - hh.astype(_F32)).astype(_BF)
        rh, rl = r_ref[0], r_ref[1]
        logits = (jnp.dot(hh, rh, preferred_element_type=_F32)
                  + jnp.dot(hh, rl, preferred_element_type=_F32)
                  + jnp.dot(hl, rh, preferred_element_type=_F32))
        tm = logits.shape[0]
        lane = lax.broadcasted_iota(jnp.int32, (tm, LANES), 1)
        lg = jnp.where(lane < N_EXPERTS, logits, NEG)
        m1 = jnp.max(lg, axis=-1, keepdims=True)
        lanef = lane.astype(_F32)
        i1 = jnp.min(jnp.where(lg == m1, lanef, float(LANES)), axis=-1, keepdims=True)
        lg2 = jnp.where(lanef == i1, NEG, lg)
        m2 = jnp.max(lg2, axis=-1, keepdims=True)
        i2 = jnp.min(jnp.where(lg2 == m2, lanef, float(LANES)), axis=-1, keepdims=True)
        e = jnp.exp(m2 - m1)
        g1 = 1.0 / (1.0 + e)
        g2 = e / (1.0 + e)
        route_out[...] = jnp.where(lane == 0, i1,
                                   jnp.where(lane == 1, i2,
                                             jnp.where(lane == 2, g1, jnp.where(lane == 3, g2, 0.0))))


def _oproj(x, mod, g, w, o_list, tm, h_dtype, router=None):
    B, S, D = x.shape
    n_in = len(o_list)
    route = router is not None
    tok = lambda wd: pl.BlockSpec((None, tm, wd), lambda b, i: (b, i, 0))
    in_specs = [tok(D), pl.BlockSpec((None, 6, D), lambda b, i: (b, 0, 0)),
                pl.BlockSpec((1, D), lambda b, i: (0, 0)),
                pl.BlockSpec(w.shape, lambda b, i: (0, 0))]
    for o in o_list:
        if o.ndim == 4:
            in_specs.append(pl.BlockSpec((None, o.shape[1], tm, LANES), lambda b, i: (b, 0, i, 0)))
        else:
            in_specs.append(tok(o.shape[2]))
    args = [x, mod, g, w] + list(o_list)
    out_specs = [tok(D), tok(D)]
    out_shape = [jax.ShapeDtypeStruct((B, S, D), _F32), jax.ShapeDtypeStruct((B, S, D), h_dtype)]
    if route:
        assert D == SUBLANES * LANES
        per_b = S // tm
        out_specs[1] = pl.BlockSpec((tm * SUBLANES, LANES), lambda b, i: (b * per_b + i, 0))
        out_shape[1] = jax.ShapeDtypeStruct((B * S * SUBLANES, LANES), _F32)
        in_specs.append(pl.BlockSpec(router.shape, lambda b, i: (0, 0, 0)))
        args.append(router)
        out_specs.append(tok(LANES))
        out_shape.append(jax.ShapeDtypeStruct((B, S, LANES), _F32))
    return pl.pallas_call(
        functools.partial(_oproj_kernel, n_in=n_in, route=route),
        grid=(B, S // tm),
        in_specs=in_specs, out_specs=out_specs, out_shape=out_shape,
        compiler_params=_cparams(("parallel", "parallel")),
        name="oproj_route" if route else "oproj",
    )(*args)


def _swiglu_hidden(x, wg_ref, wu_ref, a_scr, tf, after_chunk=None):
    F = wg_ref.shape[1]
    for c0 in range(0, F, tf):
        c1 = min(c0 + tf, F)
        g = jnp.dot(x, wg_ref[:, c0:c1], preferred_element_type=_F32)
        u = jnp.dot(x, wu_ref[:, c0:c1], preferred_element_type=_F32)
        a_scr[:, c0:c1] = (_silu(g) * u).astype(_BF)
        if after_chunk is not None:
            after_chunk(c0 // tf)


def _ffn_kernel(x_ref, oa_ref, ob_ref, wo_ref, mod_ref, g_ref, modn_ref, gn_ref, wg_ref, wu_ref, wd_ref,
                x_out, h_out, a_scr, *, tf):
    wa = oa_ref.shape[1]
    y = (jnp.dot(oa_ref[...], wo_ref[0:wa, :], preferred_element_type=_F32)
         + jnp.dot(ob_ref[...], wo_ref[wa:, :], preferred_element_type=_F32))
    x1 = x_ref[...] + mod_ref[2:3, :] * y
    h = _modulate(x1, g_ref[...], mod_ref[3:4, :], mod_ref[4:5, :]).astype(_BF)
    _swiglu_hidden(h, wg_ref, wu_ref, a_scr, tf)
    y = jnp.dot(a_scr[...], wd_ref[...], preferred_element_type=_F32)
    x2 = x1 + mod_ref[5:6, :] * y
    x_out[...] = x2
    h_out[...] = _modulate(x2, gn_ref[...], modn_ref[0:1, :], modn_ref[1:2, :]).astype(h_out.dtype)


def _ffn(x, o_a, o_b, w_out, mod, g, modn, gn, wg, wu, wd, tm, tf):
    B, S, D = x.shape
    T = B * S
    F = wg.shape[1]
    per_b = S // tm
    flat = lambda a: a.reshape(T, a.shape[2])
    tok = lambda w: pl.BlockSpec((tm, w), lambda i: (i, 0))
    modspec = pl.BlockSpec((None, 6, D), lambda i: (i // per_b, 0, 0))
    row = pl.BlockSpec((1, D), lambda i: (0, 0))
    once = pl.Buffered(1)
    res = lambda a: pl.BlockSpec(a.shape, lambda i: (0, 0), pipeline_mode=once)
    xo, ho = pl.pallas_call(
        functools.partial(_ffn_kernel, tf=tf),
        grid=(T // tm,),
        in_specs=[tok(D), tok(o_a.shape[2]), tok(o_b.shape[2]), res(w_out), modspec, row, modspec, row,
                  res(wg), res(wu), res(wd)],
        out_specs=[tok(D), tok(D)],
        out_shape=[jax.ShapeDtypeStruct((T, D), _F32), jax.ShapeDtypeStruct((T, D), _BF)],
        scratch_shapes=[pltpu.VMEM((tm, F), _BF)],
        compiler_params=_cparams(("parallel",)),
        name="ffn_dense",
    )(flat(x), flat(o_a), flat(o_b), w_out, mod, g, modn, gn, wg, wu, wd)
    return xo.reshape(B, S, D), ho.reshape(B, S, D)


def _qkv_kernel(h_ref, w_ref, gains_ref, rope_ref, q_ref, k_ref, v_ref, n_scr):
    tm, D = h_ref.shape
    h = h_ref[...]
    lane = lax.broadcasted_iota(jnp.int32, (tm, LANES), 1)
    head_a = (lane & 63) < 32

    def head_pair_norm(v, g):
        sq = v * v
        s_a = jnp.sum(jnp.where(head_a, sq, 0.0), axis=-1, keepdims=True)
        s_b = jnp.sum(jnp.where(head_a, 0.0, sq), axis=-1, keepdims=True)
        r = lax.rsqrt(jnp.where(head_a, s_a, s_b) * (1.0 / HEAD_DIM) + EPS)
        return v * r * g

    y = jnp.dot(h, w_ref[...], preferred_element_type=_F32)
    for p in range(D // LANES):
        sl = slice(p * LANES, (p + 1) * LANES)
        n_scr[:, sl] = head_pair_norm(y[:, sl], gains_ref[0:1, :])
        n_scr[:, D + p * LANES:D + (p + 1) * LANES] = head_pair_norm(
            y[:, D + p * LANES:D + (p + 1) * LANES], gains_ref[1:2, :])
        v_ref[p] = y[:, 2 * D + p * LANES:2 * D + (p + 1) * LANES].astype(_BF)

    @pl.when(pl.program_id(1) >= 0)
    def _():
        c, s = rope_ref[:, 0:128], rope_ref[:, 128:256]
        for p in range(D // LANES):
            vq = n_scr[:, p * LANES:(p + 1) * LANES]
            vk = n_scr[:, D + p * LANES:D + (p + 1) * LANES]
            q_ref[p] = (vq * c + pltpu.roll(vq, 64, 1) * s).astype(_BF)
            k_ref[p] = (vk * c + pltpu.roll(vk, 64, 1) * s).astype(_BF)


def _qkv(h, w, gains, rope, tm):
    B, S, D = h.shape
    tok = pl.BlockSpec((None, tm, D), lambda b, i: (b, i, 0))
    hp = pl.BlockSpec((None, D // LANES, tm, LANES), lambda b, i: (b, 0, i, 0))
    sds = jax.ShapeDtypeStruct((B, D // LANES, S, LANES), _BF)
    return pl.pallas_call(
        _qkv_kernel,
        grid=(B, S // tm),
        in_specs=[tok, pl.BlockSpec(w.shape, lambda b, i: (0, 0)),
                  pl.BlockSpec((2, LANES), lambda b, i: (0, 0)),
                  pl.BlockSpec((tm, 2 * LANES), lambda b, i: (i, 0))],
        out_specs=[hp, hp, hp],
        out_shape=[sds, sds, sds],
        scratch_shapes=[pltpu.VMEM((tm, 2 * D), _F32)],
        compiler_params=_cparams(("parallel", "parallel")),
        name="qkv_dil",
    )(h, w, gains, rope)


DIL_GROUP = 4


def _dil_kernel(fast_ref, q_ref, k_ref, v_ref, shift_ref, o_ref, tmp, qp, kp, vp, acc, mm, ll, accp, mmp,
                llp, bias_scr, *, tl):
    @pl.when(fast_ref[0] == 1)
    def _():
        _dil_body(q_ref, k_ref, v_ref, shift_ref, o_ref, tmp, qp, kp, vp, acc, mm, ll, accp, mmp, llp,
                  bias_scr, tl=tl, fixed_shift=True)

    @pl.when(fast_ref[0] != 1)
    def _():
        _dil_body(q_ref, k_ref, v_ref, shift_ref, o_ref, tmp, qp, kp, vp, acc, mm, ll, accp, mmp, llp,
                  bias_scr, tl=tl, fixed_shift=False)


def _dil_body(q_ref, k_ref, v_ref, shift_ref, o_ref, tmp, qp, kp, vp, acc, mm, ll, accp, mmp, llp,
              bias_scr, *, tl, fixed_shift):
    S = q_ref.shape[0]
    ng = S // DIL_GROUP
    pitch = DIL_GROUP + 1

    def spread(dst, src):
        for j in range(DIL_GROUP):
            dst[pl.ds(j, ng, stride=pitch), :] = src[pl.ds(j, ng, stride=DIL_GROUP), :]

    for src, dst in ((q_ref, qp), (k_ref, kp), (v_ref, vp)):
        tmp[...] = src[...].astype(_F32)
        spread(dst, tmp)
    lane = lax.broadcasted_iota(jnp.int32, (tl, LANES), 1)
    lo = lane < 64
    head_a = (lane & 63) < 32
    for bi, (window, d) in enumerate(DIL_PATTERNS):
        L = S // d
        half = window // (2 * d)
        t = min(tl, L)
        W = min(t + 2 * half, L)
        nt = L // t
        if d > 1:
            assert d % DIL_GROUP == 0
        sd = d * pitch // DIL_GROUP

        def rows(r, first, n, d=d, sd=sd):
            if d == 1:
                return pl.ds(first, n)
            return pl.ds(r + r // DIL_GROUP + sd * first, n, stride=sd)

        assert t % half == 0 and W <= t + 2 * half
        qa = lax.broadcasted_iota(jnp.int32, (t, W), 0)
        kb = lax.broadcasted_iota(jnp.int32, (t, W), 1)
        inside = shift_ref[0:1, 0:W] if fixed_shift else 0.0
        for case in range(3):
            ok = jnp.abs(kb - qa - case * half) <= half
            bias_scr[case, 0:t, 0:W] = jnp.where(ok, inside, NEG)

        def body(idx, carry, bi=bi, d=d, L=L, half=half, t=t, W=W, nt=nt, rows=rows):
            r = idx // nt
            l0 = (idx % nt) * t
            start = jnp.clip(l0 - half, 0, L - W)
            if d == 1:
                qt = q_ref[pl.ds(pl.multiple_of(l0, t), t), :]
                kw = k_ref[pl.ds(pl.multiple_of(start, 64), W), :]
                vw = v_ref[pl.ds(pl.multiple_of(start, 64), W), :]
            else:
                qt = qp[rows(r, l0, t), :].astype(_BF)
                kw = kp[rows(r, start, W), :].astype(_BF)
                vw = vp[rows(r, start, W), :].astype(_BF)
            lo_t = lo[:t]
            qa_t = head_a[:t]
            zq = jnp.zeros_like(qt)
            q2 = jnp.concatenate([jnp.where(qa_t, qt, zq), jnp.where(qa_t, zq, qt)], axis=0)
            s = lax.dot_general(q2, kw, (((1,), (1,)), ((), ())), preferred_element_type=_F32)
            bias = bias_scr[(l0 - start) // half, 0:t, 0:W]
            s = s + jnp.concatenate([bias, bias], axis=0)
            if fixed_shift:
                p = jnp.exp2(s)
            else:
                m = jnp.max(s, axis=-1, keepdims=True)
                p = jnp.exp2(s - m)
                m_new = jnp.where(lo_t, m[:t], m[t:])
            den = jnp.sum(p, axis=-1, keepdims=True)
            o2 = jnp.dot(p.astype(_BF), vw, preferred_element_type=_F32)
            o_new = jnp.where(lo_t, o2[:t], o2[t:])
            l_new = jnp.where(lo_t, den[:t], den[t:])
            tok = rows(r, l0, t)
            if d == 1:
                assert bi == 0
                acc[tok, :] = o_new
                ll[tok, :] = l_new
                if not fixed_shift:
                    mm[tok, :] = m_new
            elif fixed_shift:
                accp[tok, :] = accp[tok, :] + o_new
                llp[tok, :] = llp[tok, :] + l_new
            else:
                m_old = mmp[tok, :]
                mx = jnp.maximum(m_old, m_new)
                a_old = jnp.exp2(m_old - mx)
                a_new = jnp.exp2(m_new - mx)
                accp[tok, :] = accp[tok, :] * a_old + o_new * a_new
                llp[tok, :] = llp[tok, :] * a_old + l_new * a_new
                mmp[tok, :] = mx
            return carry

        lax.fori_loop(0, d * nt, body, 0, unroll=min(32 if fixed_shift else 8, d * nt))
        if bi == 0:
            spread(accp, acc)
            spread(llp, ll)
            if not fixed_shift:
                spread(mmp, mm)
    for j in range(DIL_GROUP):
        grp = pl.ds(j, ng, stride=pitch)
        tmp[pl.ds(j, ng, stride=DIL_GROUP), :] = accp[grp, :] / llp[grp, :]
    o_ref[...] = tmp[...].astype(o_ref.dtype)


def _dilated(fast, shift, q, k, v, tl):
    B, P, S, _ = q.shape
    wmax = tl + max(w // d for w, d in DIL_PATTERNS)
    spec = pl.BlockSpec((None, None, S, LANES), lambda b, p, f: (b, p, 0, 0))
    scr = pltpu.VMEM((S, LANES), _F32)
    scrp = pltpu.VMEM((S // DIL_GROUP * (DIL_GROUP + 1), LANES), _F32)
    grid_spec = pltpu.PrefetchScalarGridSpec(
        num_scalar_prefetch=1,
        grid=(B, P),
        in_specs=[spec, spec, spec, pl.BlockSpec((1, wmax), lambda b, p, f: (0, 0))],
        out_specs=spec,
        scratch_shapes=[scr, scrp, scrp, scrp, scr, scr, scr, scrp, scrp, scrp,
                        pltpu.VMEM((3, tl, wmax), _F32)],
    )
    return pl.pallas_call(
        functools.partial(_dil_kernel, tl=tl),
        grid_spec=grid_spec,
        out_shape=jax.ShapeDtypeStruct((B, P, S, LANES), _BF),
        compiler_params=_cparams(("parallel", "parallel")),
        name="dil_attn",
    )(fast, q, k, v, jnp.full((1, wmax), shift, _F32))


def _moe_kernel(te_ref, tv_ref, src_hbm, dst_hbm, h_hbm, wg_ref, wu_ref, wd_ref, out_hbm,
                xbuf, xbf, a_scr, ybuf, src_s0, src_s1, dst_s0, dst_s1, gsem, ssem, isem,
                *, tf, dump0, n_dump):
    i = pl.program_id(0)
    nt = pl.num_programs(0)
    tm = xbf.shape[0]
    slot = i % 2

    def is_valid(t):
        return jnp.logical_and(jnp.logical_and(t >= 0, t < nt), tv_ref[jnp.clip(t, 0, nt - 1)] == 1)

    def both(a, b):
        return jnp.logical_and(a, b)

    valid = is_valid(i)
    prev_valid = is_valid(i - 1)

    src_bufs = (src_s0, src_s1)
    dst_bufs = (dst_s0, dst_s1)

    def src_copy(tile, par):
        return pltpu.make_async_copy(src_hbm.at[tile], src_bufs[par], isem.at[par])

    def dst_copy(tile, par):
        return pltpu.make_async_copy(dst_hbm.at[tile], dst_bufs[par], isem.at[2 + par])

    def tile_at(first):
        return pl.ds(pl.multiple_of(first, SUBLANES), SUBLANES)

    def gather_row(r, par):
        return pltpu.make_async_copy(h_hbm.at[tile_at(src_bufs[par][r])],
                                     xbuf.at[par, tile_at(r * SUBLANES)], gsem.at[par])

    def scatter_row(r, par):
        return pltpu.make_async_copy(ybuf.at[par, tile_at(r * SUBLANES)],
                                     out_hbm.at[tile_at(dst_bufs[par][r])], ssem.at[par])

    def gather_wait(par):
        return pltpu.make_async_copy(h_hbm.at[pl.ds(0, tm * SUBLANES)], xbuf.at[par], gsem.at[par])

    def scatter_wait(par):
        return pltpu.make_async_copy(ybuf.at[par], out_hbm.at[pl.ds(0, tm * SUBLANES)], ssem.at[par])

    def for_rows(fn):
        def body(r, c):
            fn(r)
            return c
        lax.fori_loop(0, tm, body, 0, unroll=8)

    @pl.when(i == 0)
    def _():
        ybuf[...] = jnp.zeros_like(ybuf)
        for k in range(n_dump):
            pltpu.make_async_copy(ybuf.at[1], out_hbm.at[pl.ds((dump0 + k * tm) * SUBLANES, tm * SUBLANES)],
                                  ssem.at[1]).start()
        for k in range(n_dump):
            scatter_wait(1).wait()

    @pl.when(both(i == 0, valid))
    def _():
        src_copy(0, 0).start()
        src_copy(0, 0).wait()
        for_rows(lambda r: gather_row(r, 0).start())
        src_copy(1, 1).start()

    def phase1(par, with_scatter):
        xbf[...] = _tiles_to_rows(xbuf.at[par], tm).astype(_BF)
        n_chunk = -(-wg_ref.shape[1] // tf)
        per = -(-tm // n_chunk)

        def after(c):
            for r in range(c * per, min((c + 1) * per, tm)):
                gather_row(r, 1 - par).start()
                if with_scatter:
                    scatter_row(r, 1 - par).start()

        _swiglu_hidden(xbf[...], wg_ref, wu_ref, a_scr, tf, after)
        _rows_to_tiles(ybuf.at[par], jnp.dot(a_scr[...], wd_ref[...], preferred_element_type=_F32))

    for par in range(2):
        here = slot == par

        @pl.when(both(valid, here))
        def _(par=par):
            dst_copy(i, par).start()
            src_copy(i + 1, 1 - par).wait()

        @pl.when(both(prev_valid, here))
        def _(par=par):
            dst_copy(i - 1, 1 - par).wait()

        @pl.when(both(jnp.logical_or(both(i == 0, valid), prev_valid), here))
        def _(par=par):
            gather_wait(par).wait()

        @pl.when(both(is_valid(i - 2), here))
        def _(par=par):
            scatter_wait(par).wait()

        @pl.when(both(both(valid, prev_valid), here))
        def _(par=par):
            phase1(par, True)

        if par == 0:
            @pl.when(both(both(valid, jnp.logical_not(prev_valid)), here))
            def _():
                phase1(0, False)

        @pl.when(both(both(jnp.logical_not(valid), prev_valid), here))
        def _(par=par):
            for_rows(lambda r: scatter_row(r, 1 - par).start())

        @pl.when(both(is_valid(i + 1), here))
        def _(par=par):
            src_copy(i + 2, par).start()


def _moe(h, tile_e, tile_v, src, dst, wg, wu, wd, n_out_rows, tm, tf):
    T = h.shape[0] // SUBLANES
    D = wg.shape[1]
    nt = tile_e.shape[0]
    F = wg.shape[2]
    dump0 = T * TOP_K
    once = pl.Buffered(1)
    grid_spec = pltpu.PrefetchScalarGridSpec(
        num_scalar_prefetch=2,
        grid=(nt,),
        in_specs=[pl.BlockSpec(memory_space=pl.ANY), pl.BlockSpec(memory_space=pl.ANY),
                  pl.BlockSpec(memory_space=pl.ANY),
                  pl.BlockSpec((None, D, F), lambda i, te, tv: (te[i], 0, 0), pipeline_mode=once),
                  pl.BlockSpec((None, D, F), lambda i, te, tv: (te[i], 0, 0), pipeline_mode=once),
                  pl.BlockSpec((None, F, D), lambda i, te, tv: (te[i], 0, 0), pipeline_mode=once)],
        out_specs=pl.BlockSpec(memory_space=pl.ANY),
        scratch_shapes=[pltpu.VMEM((2, tm * SUBLANES, LANES), _F32), pltpu.VMEM((tm, D), _BF),
                        pltpu.VMEM((tm, F), _BF), pltpu.VMEM((2, tm * SUBLANES, LANES), _F32),
                        pltpu.SMEM((tm,), jnp.int32), pltpu.SMEM((tm,), jnp.int32),
                        pltpu.SMEM((tm,), jnp.int32), pltpu.SMEM((tm,), jnp.int32),
                        pltpu.SemaphoreType.DMA((2,)), pltpu.SemaphoreType.DMA((2,)),
                        pltpu.SemaphoreType.DMA((4,))],
    )
    return pl.pallas_call(
        functools.partial(_moe_kernel, tf=tf, dump0=dump0, n_dump=(n_out_rows - dump0) // tm),
        grid_spec=grid_spec,
        out_shape=jax.ShapeDtypeStruct((n_out_rows * SUBLANES, LANES), _F32),
        compiler_params=_cparams(("arbitrary",)),
        name="moe_experts",
    )(tile_e, tile_v, src, dst, h, wg, wu, wd)


def _route_plan(route, T, tm):
    A = T * TOP_K
    e_flat = route[:, :TOP_K].astype(jnp.int32).reshape(A)
    order = jnp.argsort(e_flat, stable=True).astype(jnp.int32)
    counts = jnp.sum((e_flat[:, None] == jnp.arange(N_EXPERTS, dtype=jnp.int32)[None, :]).astype(jnp.int32), axis=0)
    starts = jnp.cumsum(counts) - counts
    pcounts = (counts + tm - 1) // tm * tm
    pends = jnp.cumsum(pcounts)
    pstarts = pends - pcounts
    nt = A // tm + N_EXPERTS + 2
    tile0 = jnp.arange(nt, dtype=jnp.int32) * tm
    tile_v = (tile0 < pends[-1]).astype(jnp.int32)
    last_valid = jnp.maximum(pends[-1] // tm - 1, 0)
    tile_e_raw = jnp.minimum(jnp.searchsorted(pends, tile0, side="right"), N_EXPERTS - 1).astype(jnp.int32)
    tile_e = jnp.where(tile_v == 1, tile_e_raw, tile_e_raw[last_valid])
    r = jnp.arange(nt * tm, dtype=jnp.int32)
    e_r = jnp.repeat(tile_e, tm)
    within = r - pstarts[e_r]
    ok = jnp.logical_and(within < counts[e_r], jnp.repeat(tile_v, tm) == 1)
    a = order[jnp.clip(starts[e_r] + within, 0, A - 1)]
    src = jnp.where(ok, a // TOP_K, 0).astype(jnp.int32)
    dump = A + e_r * tm + jnp.clip(within - counts[e_r], 0, tm - 1)
    dst = jnp.where(ok, (a % TOP_K) * T + a // TOP_K, dump).astype(jnp.int32)
    return (tile_e, tile_v, (src * SUBLANES).reshape(nt, tm), (dst * SUBLANES).reshape(nt, tm),
            A + N_EXPERTS * tm)


def _combine_kernel(x_ref, y1_ref, y2_ref, r_ref, mod_ref, o_ref):
    tm = x_ref.shape[0]
    y = _tiles_to_rows(y1_ref, tm) * r_ref[:, 2:3] + _tiles_to_rows(y2_ref, tm) * r_ref[:, 3:4]
    o_ref[...] = x_ref[...] + mod_ref[5:6, :] * y


def _combine(x, y2, route, mod, tm):
    B, S, D = x.shape
    T = B * S
    per_b = S // tm
    out = pl.pallas_call(
        _combine_kernel,
        grid=(T // tm,),
        in_specs=[pl.BlockSpec((tm, D), lambda i: (i, 0)),
                  pl.BlockSpec((tm * SUBLANES, LANES), lambda i: (i, 0)),
                  pl.BlockSpec((tm * SUBLANES, LANES), lambda i: (T // tm + i, 0)),
                  pl.BlockSpec((tm, LANES), lambda i: (i, 0)),
                  pl.BlockSpec((None, 6, D), lambda i: (i // per_b, 0, 0))],
        out_specs=pl.BlockSpec((tm, D), lambda i: (i, 0)),
        out_shape=jax.ShapeDtypeStruct((T, D), _F32),
        compiler_params=_cparams(("parallel",)),
        name="moe_combine",
    )(x.reshape(T, D), y2, y2, route.reshape(T, LANES), mod)
    return out.reshape(B, S, D)


def _pad_cols(a, w):
    return jnp.pad(a, ((0, 0), (0, w - a.shape[1])))


def _rope_tables_even(S):
    pos = jnp.arange(S, dtype=jnp.int32)
    inv = ROPE_THETA ** (-jnp.arange(0, 32, 2, dtype=_F32) / 32)
    def cs(p):
        ang = p.astype(_F32)[:, None] * inv[None, :]
        return jnp.cos(ang), jnp.sin(ang)
    one = lambda w: jnp.ones((S, w), _F32)
    zero = lambda w: jnp.zeros((S, w), _F32)
    c, s = cs(pos)
    ca = jnp.concatenate([c, one(48), c, one(48)], 1)
    sa = jnp.concatenate([-s, zero(48), s, zero(48)], 1)
    cr, sr = cs(pos // GRID_W)
    cc, sc = cs(pos % GRID_W)
    cb = jnp.concatenate([cr, cc, one(32), cr, cc, one(32)], 1)
    sb = jnp.concatenate([-sr, -sc, zero(32), sr, sc, zero(32)], 1)
    return jnp.concatenate([ca, sa, cb, sb], 1)


def _slot_maps():
    r = MLA_ROPE // 2
    mla = ([MLA_NOPE + i for i in range(r)] + list(range(0, 64 - r))
           + [MLA_NOPE + r + i for i in range(r)] + list(range(64 - r, MLA_NOPE)))
    mla += [-1] * (LANES - len(mla))
    q = HEAD_DIM // 4
    gqa = (list(range(0, q)) + list(range(2 * q, 3 * q)) + [-1] * (64 - 2 * q)
           + list(range(q, 2 * q)) + list(range(3 * q, 4 * q)) + [-1] * (64 - 2 * q))
    return mla, gqa


def _to_slots(a, lane_map):
    idx = jnp.asarray([max(i, 0) for i in lane_map], jnp.int32)
    keep = jnp.asarray([1.0 if i >= 0 else 0.0 for i in lane_map], a.dtype)
    return jnp.take(a, idx, axis=-1) * keep


def _rope_tables_odd(S):
    pos = jnp.arange(S, dtype=_F32)
    inv = ROPE_THETA ** (-jnp.arange(0, HEAD_DIM, 2, dtype=_F32) / HEAD_DIM)
    ang = pos[:, None] * inv[None, :]
    c, s = jnp.cos(ang), jnp.sin(ang)
    return jnp.concatenate([c, c, c, c, -s, -s, s, s], 1)


def _tiles(S):
    return dict(tm_pre=min(S, 512), tq=min(S, 2048), tm_o=min(S, 512),
                tm_ffn=min(S, 512), tf_ffn=512, tm_qkv=min(S, 512),
                tl=128, tm_moe=min(S, 512), tf_moe=512, tm_c=min(S, 512))


def kernel(x, c, ada_even_w, ada_even_b, norm_even_mix, norm_even_ffn, even_w_in, mla_q_norm, mla_w_uq, mla_kv_norm, mla_w_ukv, mla_q_gain, mla_k_gain, gqa_q_gain, gqa_k_gain, even_w_out, ffn_w_gate, ffn_w_up, ffn_w_down, ada_odd_w, ada_odd_b, norm_odd_mix, norm_odd_ffn, dil_w_qkv, dil_q_gain, dil_k_gain, dil_w_out, moe_router, moe_w_gate, moe_w_up, moe_w_down):
    B, S, D = x.shape
    T = B * S
    cfg = _tiles(S)

    mod_e = _ada_mod(c, ada_even_w[0], ada_even_b[0]).reshape(B, 6, D)
    mod_o = _ada_mod(c, ada_odd_w[0], ada_odd_b[0]).reshape(B, 6, D)

    w = even_w_in[0]
    sp = [MLA_Q_RANK, MLA_Q_RANK + MLA_KV_RANK, MLA_Q_RANK + MLA_KV_RANK + MLA_ROPE]
    sp.append(sp[-1] + GQA_HEADS * HEAD_DIM)
    sp.append(sp[-1] + GQA_KV_HEADS * HEAD_DIM)
    w_cq, w_ckv, w_kpe = w[:, :sp[0]], w[:, sp[0]:sp[1]], w[:, sp[1]:sp[2]]
    w_qb, w_kb, w_vb = w[:, sp[2]:sp[3]], w[:, sp[3]:sp[4]], w[:, sp[4]:]
    mla_map, gqa_map = _slot_maps()
    na = MLA_NOPE + MLA_ROPE
    nope_only = [i if 0 <= i < MLA_NOPE else -1 for i in mla_map]
    rope_only = [i - MLA_NOPE if i >= MLA_NOPE else -1 for i in mla_map]
    gslots = lambda a, n: _to_slots(a.reshape(D, n, HEAD_DIM), gqa_map).reshape(D, n * LANES)
    w_in = jnp.concatenate([w_cq, w_ckv, _to_slots(w_kpe, rope_only), gslots(w_qb, GQA_HEADS),
                            gslots(w_kb, GQA_KV_HEADS)], axis=1).astype(_BF)
    w_uq = _to_slots(mla_w_uq[0].reshape(MLA_Q_RANK, MLA_HEADS, na), mla_map)
    w_uq = w_uq.reshape(MLA_Q_RANK, MLA_HEADS * LANES).astype(_BF)
    w_ukv = mla_w_ukv[0].reshape(MLA_KV_RANK, MLA_HEADS, MLA_NOPE + MLA_V)
    w_uk = _to_slots(w_ukv[:, :, :MLA_NOPE], nope_only).reshape(MLA_KV_RANK, MLA_HEADS * LANES).astype(_BF)
    w_uvt = w_ukv[:, :, MLA_NOPE:].reshape(MLA_KV_RANK, MLA_HEADS * MLA_V).T.astype(_BF)
    w_vbt = w_vb.T.astype(_BF)
    g_rows = [_to_slots(mla_q_gain[0], mla_map) * (na ** -0.5 * LOG2E), _to_slots(mla_k_gain[0], mla_map),
              _to_slots(gqa_q_gain[0], gqa_map) * (HEAD_DIM ** -0.5 * LOG2E), _to_slots(gqa_k_gain[0], gqa_map)]
    bound_a = na * jnp.max(jnp.abs(g_rows[0])) * jnp.max(jnp.abs(g_rows[1]))
    bound_b = HEAD_DIM * jnp.max(jnp.abs(g_rows[2])) * jnp.max(jnp.abs(g_rows[3]))
    fast_a, fast_b = bound_a <= SCORE_BOUND_MAX, bound_b <= SCORE_BOUND_MAX
    last_lane = jnp.zeros((LANES,), _F32).at[LANES - 1].set(1.0)
    g_rows += [last_lane, last_lane * jnp.where(fast_a, -bound_a, 0.0),
               last_lane, last_lane * jnp.where(fast_b, -bound_b, 0.0)]
    gains_e = jnp.stack(g_rows)
    rope_e = _rope_tables_even(S)
    q_all, k_all, vt_all = _pre_even(
        x, mod_e, norm_even_mix[0].reshape(1, D), w_in, mla_q_norm[0].reshape(1, -1), w_uq,
        mla_kv_norm[0].reshape(1, -1), w_uk, w_uvt, w_vbt, gains_e, rope_e, cfg["tm_pre"])
    as_flag = lambda f: f.astype(jnp.int32).reshape(1)
    o_a = _attention(as_flag(fast_a), q_all, k_all, vt_all, q_off=0, k_off=0, shared_kv=False, tq=cfg["tq"])
    o_b = _attention(as_flag(fast_b), q_all, k_all, vt_all, q_off=MLA_HEADS, k_off=MLA_HEADS,
                     shared_kv=True, tq=cfg["tq"])
    x2, h2 = _ffn(x, o_a, o_b, even_w_out[0].astype(_BF), mod_e, norm_even_ffn[0].reshape(1, D), mod_o,
                  norm_odd_mix[0].reshape(1, D), ffn_w_gate[0].astype(_BF), ffn_w_up[0].astype(_BF),
                  ffn_w_down[0].astype(_BF), cfg["tm_ffn"], cfg["tf_ffn"])

    hh = HEAD_DIM // 2
    pair = lambda v: jnp.concatenate([v[:hh], v[:hh], v[hh:], v[hh:]])
    gains_o = jnp.stack([pair(dil_q_gain[0]) * (HEAD_DIM ** -0.5 * LOG2E), pair(dil_k_gain[0])])
    wq, wk, wv = jnp.split(dil_w_qkv[0], 3, axis=1)
    perm = lambda w: w.reshape(D, D // LANES, 2, 2, hh).transpose(0, 1, 3, 2, 4).reshape(D, D)
    w_qkv = jnp.concatenate([perm(wq), perm(wk), wv], axis=1).astype(_BF)
    qd, kd, vd = _qkv(h2, w_qkv, gains_o, _rope_tables_odd(S), cfg["tm_qkv"])
    bound_d = HEAD_DIM * jnp.max(jnp.abs(gains_o[0])) * jnp.max(jnp.abs(gains_o[1]))
    fast_d = bound_d <= SCORE_BOUND_MAX
    o_d = _dilated(as_flag(fast_d), jnp.where(fast_d, -bound_d, 0.0), qd, kd, vd, cfg["tl"])
    r32 = _pad_cols(moe_router[0], LANES)
    r_hi = r32.astype(_BF)
    r_lo = (r32 - r_hi.astype(_F32)).astype(_BF)
    x3, h3, route = _oproj(x2, mod_o, norm_odd_ffn[0].reshape(1, D), dil_w_out[0].astype(_BF), [o_d],
                           cfg["tm_o"], _F32, router=jnp.stack([r_hi, r_lo]))
    tm = cfg["tm_moe"]
    tile_e, tile_v, src, dst, n_rows = _route_plan(route.reshape(T, LANES), T, tm)
    y2 = _moe(h3, tile_e, tile_v, src, dst, moe_w_gate[0].astype(_BF),
              moe_w_up[0].astype(_BF), moe_w_down[0].astype(_BF), n_rows, tm, cfg["tf_moe"])
    return _combine(x3, y2, route, mod_o, cfg["tm_c"])
```

```python
import functools
import math

import jax
import jax.numpy as jnp
from jax import lax
from jax.experimental import pallas as pl
from jax.experimental.pallas import tpu as pltpu

_BF = jnp.bfloat16
_F32 = jnp.float32

GRID_W = 64
HEAD_DIM = 64
ROPE_THETA = 10000.0
EPS = 1e-6
MLA_HEADS = 8
MLA_Q_RANK = 256
MLA_KV_RANK = 128
MLA_NOPE = 64
MLA_ROPE = 32
MLA_V = 64
GQA_HEADS = 8
GQA_KV_HEADS = 2
DIL_PATTERNS = ((128, 1), (512, 4), (2048, 16))
N_EXPERTS = 8
TOP_K = 2
NEG = -1e30
LOG2E = math.log2(math.e)

LANES = 128
VMEM_LIMIT = 56 * 1024 * 1024


def _cparams(sem, vmem=VMEM_LIMIT):
    return pltpu.CompilerParams(dimension_semantics=sem, vmem_limit_bytes=vmem)


def _silu(x):
    return x / (1.0 + jnp.exp(-x))


def _modulate(x, g, shift, scale):
    ms = jnp.mean(x * x, axis=-1, keepdims=True)
    return x * lax.rsqrt(ms + EPS) * g * (1.0 + scale) + shift


def _norm(v, n, g):
    return v * lax.rsqrt(jnp.sum(v * v, axis=-1, keepdims=True) * (1.0 / n) + EPS) * g


SUBLANES = 8


def _rows_to_tiles(ref, x, row0=0):
    tm = x.shape[0]
    for c in range(SUBLANES):
        ref[pl.ds(row0 * SUBLANES + c, tm, stride=SUBLANES), :] = x[:, c * LANES:(c + 1) * LANES]


def _tiles_to_rows(ref, tm):
    return jnp.concatenate([ref[pl.ds(c, tm, stride=SUBLANES), :] for c in range(SUBLANES)], axis=1)


def _rope(x, c, sa, sb, k):
    return x * c + pltpu.roll(x, LANES - k, 1) * sa + pltpu.roll(x, k, 1) * sb


def _mod_kernel(c_ref, w_ref, b_ref, o_ref):
    sc = _silu(c_ref[...])
    o_ref[...] = jnp.dot(sc.astype(_BF), w_ref[...].astype(_BF),
                         preferred_element_type=_F32) + b_ref[...]


def _ada_mod(c, w, b):
    B, D = c.shape
    N = w.shape[1]
    tn = min(N, 1536)
    return pl.pallas_call(
        _mod_kernel,
        grid=(N // tn,),
        in_specs=[pl.BlockSpec((B, D), lambda j: (0, 0)),
                  pl.BlockSpec((D, tn), lambda j: (0, j)),
                  pl.BlockSpec((1, tn), lambda j: (0, j))],
        out_specs=pl.BlockSpec((B, tn), lambda j: (0, j)),
        out_shape=jax.ShapeDtypeStruct((B, N), _F32),
        compiler_params=_cparams(("arbitrary",)),
        name="ada_mod",
    )(c, w, b.reshape(1, N))


def _pre_even_kernel(x_ref, mod_ref, g_ref, w_in_ref, qn_ref, w_uq_ref, kvn_ref, w_uk_ref, w_uvt_ref,
                     w_vbt_ref, gains_ref, rope_ref, q_ref, k_ref, vt_ref, nq_scr, nk_scr):
    h = _modulate(x_ref[...], g_ref[...], mod_ref[0:1, :], mod_ref[1:2, :]).astype(_BF)
    y = jnp.dot(h, w_in_ref[...], preferred_element_type=_F32)
    cqn = _norm(y[:, 0:256], MLA_Q_RANK, qn_ref[...]).astype(_BF)
    qa = jnp.dot(cqn, w_uq_ref[...], preferred_element_type=_F32)
    ckvn = _norm(y[:, 256:384], MLA_KV_RANK, kvn_ref[...]).astype(_BF)
    kn = jnp.dot(ckvn, w_uk_ref[...], preferred_element_type=_F32)
    kpe = y[:, 384:512]
    nt_dims = (((1,), (1,)), ((), ()))
    vt_a = lax.dot_general(w_uvt_ref[...], ckvn, nt_dims, preferred_element_type=_F32).astype(_BF)
    vt_b = lax.dot_general(w_vbt_ref[...], h, nt_dims, preferred_element_type=_F32).astype(_BF)
    tm = x_ref.shape[0]
    ones_blk = jnp.where(lax.broadcasted_iota(jnp.int32, (LANES - MLA_V, tm), 0) == 0, 1.0, 0.0).astype(_BF)
    for hh in range(MLA_HEADS + GQA_KV_HEADS):
        src = vt_a[hh * MLA_V:(hh + 1) * MLA_V] if hh < MLA_HEADS else \
            vt_b[(hh - MLA_HEADS) * HEAD_DIM:(hh - MLA_HEADS + 1) * HEAD_DIM]
        vt_ref[hh * LANES:hh * LANES + MLA_V, :] = src
        vt_ref[hh * LANES + MLA_V:(hh + 1) * LANES, :] = ones_blk
    ca, sa = rope_ref[:, 0:128], rope_ref[:, 128:256]
    cb, sb = rope_ref[:, 256:384], rope_ref[:, 384:512]
    gqa, gka = gains_ref[0:1, :], gains_ref[1:2, :]
    gqb, gkb = gains_ref[2:3, :], gains_ref[3:4, :]
    na = MLA_NOPE + MLA_ROPE

    for hh in range(MLA_HEADS):
        sl = slice(hh * LANES, (hh + 1) * LANES)
        nq_scr[:, sl] = _norm(qa[:, sl], na, gqa)
        nk_scr[:, sl] = _norm(kn[:, sl] + kpe, na, gka)
    for hh in range(GQA_HEADS):
        src = slice(512 + hh * LANES, 512 + (hh + 1) * LANES)
        dst = slice((MLA_HEADS + hh) * LANES, (MLA_HEADS + hh + 1) * LANES)
        nq_scr[:, dst] = _norm(y[:, src], HEAD_DIM, gqb)
    for g in range(GQA_KV_HEADS):
        src = slice(1536 + g * LANES, 1536 + (g + 1) * LANES)
        dst = slice((MLA_HEADS + g) * LANES, (MLA_HEADS + g + 1) * LANES)
        nk_scr[:, dst] = _norm(y[:, src], HEAD_DIM, gkb)

    @pl.when(pl.program_id(1) >= 0)
    def _():
        def rope(ref, out, n_a, n_all, row):
            for hh in range(n_all):
                sl = slice(hh * LANES, (hh + 1) * LANES)
                c, s = (ca, sa) if hh < n_a else (cb, sb)
                b = gains_ref[row:row + 1, :] if hh < n_a else gains_ref[row + 2:row + 3, :]
                v = ref[:, sl]
                out[:, sl] = (v * c + pltpu.roll(v, 64, 1) * s + b).astype(_BF)

        rope(nq_scr, q_ref, MLA_HEADS, MLA_HEADS + GQA_HEADS, 4)
        rope(nk_scr, k_ref, MLA_HEADS, MLA_HEADS + GQA_KV_HEADS, 5)


def _pre_even(x, mod, g, w_in, qn, w_uq, kvn, w_uk, w_uvt, w_vbt, gains, rope, tm):
    B, S, D = x.shape
    nq = (MLA_HEADS + GQA_HEADS) * LANES
    nk = (MLA_HEADS + GQA_KV_HEADS) * LANES
    nvt = (MLA_HEADS + GQA_KV_HEADS) * LANES
    full = lambda a: pl.BlockSpec(a.shape, lambda b, i: (0,) * a.ndim)
    tok = lambda w: pl.BlockSpec((None, tm, w), lambda b, i: (b, i, 0))
    return pl.pallas_call(
        _pre_even_kernel,
        grid=(B, S // tm),
        in_specs=[tok(D),
                  pl.BlockSpec((None, 6, D), lambda b, i: (b, 0, 0)),
                  full(g), full(w_in), full(qn), full(w_uq), full(kvn), full(w_uk), full(w_uvt),
                  full(w_vbt), full(gains),
                  pl.BlockSpec((tm, rope.shape[1]), lambda b, i: (i, 0))],
        out_specs=[tok(nq), tok(nk), pl.BlockSpec((None, None, nvt, tm), lambda b, i: (b, i, 0, 0))],
        out_shape=[jax.ShapeDtypeStruct((B, S, nq), _BF), jax.ShapeDtypeStruct((B, S, nk), _BF),
                   jax.ShapeDtypeStruct((B, S // tm, nvt, tm), _BF)],
        scratch_shapes=[pltpu.VMEM((tm, nq), _F32), pltpu.VMEM((tm, nk), _F32)],
        compiler_params=_cparams(("parallel", "parallel")),
        name="pre_even",
    )(x, mod, g, w_in, qn, w_uq, kvn, w_uk, w_uvt, w_vbt, gains, rope)


SCORE_BOUND_MAX = 60.0


def _attn_kernel(fast_ref, q_ref, k_ref, vt_ref, o_ref, s_scr, *, shared_kv):
    @pl.when(fast_ref[0] == 1)
    def _():
        _attn_fixed_shift(q_ref, k_ref, vt_ref, o_ref, shared_kv=shared_kv)

    @pl.when(fast_ref[0] != 1)
    def _():
        _attn_running_max(q_ref, k_ref, vt_ref, o_ref, s_scr, shared_kv=shared_kv)


def _attn_fixed_shift(q_ref, k_ref, vt_ref, o_ref, *, shared_kv):
    tq = q_ref.shape[0]
    nk, _, tk = vt_ref.shape
    qs = [q_ref[:, hh * LANES:(hh + 1) * LANES] for hh in range(2)]

    def body(j, carry):
        r0 = pl.multiple_of(j * tk, tk)
        out = []
        for hh in range(2):
            kc = 0 if shared_kv else hh * LANES
            k = k_ref[pl.ds(r0, tk), kc:kc + LANES]
            st = lax.dot_general(k, qs[hh], (((1,), (1,)), ((), ())), preferred_element_type=_F32)
            p = jnp.exp2(st).astype(_BF)
            out.append(carry[hh] + jnp.dot(vt_ref[j, kc:kc + LANES, :], p, preferred_element_type=_F32))
        return tuple(out)

    z = jnp.zeros((LANES, tq), _F32)
    res = lax.fori_loop(0, nk, body, (z, z), unroll=True)
    nv = HEAD_DIM
    halves = [acc[0:nv, :] / acc[nv:nv + 1, :] for acc in res]
    o_ref[...] = jnp.concatenate(halves, axis=0).T.astype(o_ref.dtype)


def _attn_running_max(q_ref, k_ref, vt_ref, o_ref, s_scr, *, shared_kv):
    tq = q_ref.shape[0]
    nk, _, tk = vt_ref.shape
    qs = [q_ref[:, hh * LANES:(hh + 1) * LANES] for hh in range(2)]

    def scores(j, hh):
        r0 = j * tk if isinstance(j, int) else pl.multiple_of(j * tk, tk)
        kc = 0 if shared_kv else hh * LANES
        k = k_ref[pl.ds(r0, tk), kc:kc + LANES]
        return lax.dot_general(k, qs[hh], (((1,), (1,)), ((), ())), preferred_element_type=_F32)

    def put_scores(j, hh, slot):
        st = scores(j, hh)
        s_scr[hh, slot] = st
        return jnp.max(st, axis=0, keepdims=True)

    def step(j, hh, carry, cur, last=False):
        m, mc, acc = carry
        mc_next = mc if last else put_scores(j + 1, hh, 1 - cur)
        st = s_scr[hh, cur]
        m_new = jnp.maximum(m, mc)
        alpha = jnp.exp2(m - m_new)
        p = jnp.exp2(st - m_new).astype(_BF)
        vc = 0 if shared_kv else hh * LANES
        acc = alpha * acc + jnp.dot(vt_ref[j, vc:vc + LANES, :], p, preferred_element_type=_F32)
        return m_new, mc_next, acc

    def body(jj, carry):
        c = list(carry)
        for sub in range(2):
            for hh in range(2):
                c[hh] = step(2 * jj + sub, hh, c[hh], sub)
        return tuple(c)

    init = tuple((jnp.full((1, tq), NEG, _F32), put_scores(0, hh, 0), jnp.zeros((LANES, tq), _F32))
                 for hh in range(2))
    res = lax.fori_loop(0, nk // 2 - 1, body, init)
    res = list(res)
    for sub in range(2):
        for hh in range(2):
            res[hh] = step(nk - 2 + sub, hh, res[hh], sub, last=(sub == 1))
    nv = HEAD_DIM
    halves = [acc[0:nv, :] / acc[nv:nv + 1, :] for (_, _, acc) in res]
    o_ref[...] = jnp.concatenate(halves, axis=0).T.astype(o_ref.dtype)


def _attention(fast, q, k, vt, *, q_off, k_off, shared_kv, tq):
    B, S, _ = q.shape
    nk, _, tk = vt.shape[1:]
    assert nk % 2 == 0 and nk >= 2
    npairs = 4
    if shared_kv:
        kspec = pl.BlockSpec((None, S, LANES), lambda b, p, i, f: (b, 0, k_off + p // 2))
        vspec = pl.BlockSpec((None, nk, LANES, tk), lambda b, p, i, f: (b, 0, k_off + p // 2, 0))
    else:
        kspec = pl.BlockSpec((None, S, 2 * LANES), lambda b, p, i, f: (b, 0, k_off // 2 + p))
        vspec = pl.BlockSpec((None, nk, 2 * LANES, tk), lambda b, p, i, f: (b, 0, k_off // 2 + p, 0))
    grid_spec = pltpu.PrefetchScalarGridSpec(
        num_scalar_prefetch=1,
        grid=(B, npairs, S // tq),
        in_specs=[pl.BlockSpec((None, tq, 2 * LANES), lambda b, p, i, f: (b, i, q_off // 2 + p)),
                  kspec, vspec],
        out_specs=pl.BlockSpec((None, tq, LANES), lambda b, p, i, f: (b, i, p)),
        scratch_shapes=[pltpu.VMEM((2, 2, tk, tq), _F32)],
    )
    return pl.pallas_call(
        functools.partial(_attn_kernel, shared_kv=shared_kv),
        grid_spec=grid_spec,
        out_shape=jax.ShapeDtypeStruct((B, S, npairs * LANES), _BF),
        compiler_params=_cparams(("parallel", "parallel", "parallel")),
        name="attn_gqa" if shared_kv else "attn_mla",
    )(fast, q, k, vt)


def _oproj_kernel(*refs, n_in, route, sub):
    x_ref, mod_ref, g_ref, w_ref = refs[0], refs[1], refs[2], refs[3]
    o_refs = refs[4:4 + n_in]
    pos = 4 + n_in
    if route:
        r_ref = refs[pos]
        pos += 1
    x_out, h_out = refs[pos], refs[pos + 1]
    route_out = refs[pos + 2] if route else None
    tm_all = x_ref.shape[0]
    for r0 in range(0, tm_all, sub):
        _oproj_rows(slice(r0, r0 + sub), r0, x_ref, mod_ref, g_ref, w_ref, o_refs,
                    r_ref if route else None, x_out, h_out, route_out)


def _oproj_rows(rows, r0, x_ref, mod_ref, g_ref, w_ref, o_refs, r_ref, x_out, h_out, route_out):
    route = r_ref is not None
    y = None
    off = 0
    for o_ref in o_refs:
        if len(o_ref.shape) == 3:
            o = jnp.concatenate([o_ref[p, rows, :] for p in range(o_ref.shape[0])], axis=1)
        else:
            o = o_ref[rows, :]
        w = o.shape[1]
        t = jnp.dot(o, w_ref[off:off + w, :], preferred_element_type=_F32)
        y = t if y is None else y + t
        off += w
    x1 = x_ref[rows, :] + mod_ref[2:3, :] * y
    x_out[rows, :] = x1
    h = _modulate(x1, g_ref[...], mod_ref[3:4, :], mod_ref[4:5, :])
    if route:
        _rows_to_tiles(h_out, h, r0)
    else:
        h_out[rows, :] = h.astype(h_out.dtype)
    if route:
        hh = h.astype(_BF)
        hl = (h - hh.astype(_F32)).astype(_BF)
        rh, rl = r_ref[0], r_ref[1]
        logits = (jnp.dot(hh, rh, preferred_element_type=_F32)
                  + jnp.dot(hh, rl, preferred_element_type=_F32)
                  + jnp.dot(hl, rh, preferred_element_type=_F32))
        tm = logits.shape[0]
        lane = lax.broadcasted_iota(jnp.int32, (tm, LANES), 1)
        lg = jnp.where(lane < N_EXPERTS, logits, NEG)
        m1 = jnp.max(lg, axis=-1, keepdims=True)
        lanef = lane.astype(_F32)
        i1 = jnp.min(jnp.where(lg == m1, lanef, float(LANES)), axis=-1, keepdims=True)
        lg2 = jnp.where(lanef == i1, NEG, lg)
        m2 = jnp.max(lg2, axis=-1, keepdims=True)
        i2 = jnp.min(jnp.where(lg2 == m2, lanef, float(LANES)), axis=-1, keepdims=True)
        e = jnp.exp(m2 - m1)
        g1 = 1.0 / (1.0 + e)
        g2 = e / (1.0 + e)
        route_out[rows, :] = jnp.where(lane == 0, i1,
                                   jnp.where(lane == 1, i2,
                                             jnp.where(lane == 2, g1, jnp.where(lane == 3, g2, 0.0))))


def _oproj(x, mod, g, w, o_list, tm, h_dtype, router=None):
    B, S, D = x.shape
    n_in = len(o_list)
    route = router is not None
    tok = lambda wd: pl.BlockSpec((None, tm, wd), lambda b, i: (b, i, 0))
    in_specs = [tok(D), pl.BlockSpec((None, 6, D), lambda b, i: (b, 0, 0)),
                pl.BlockSpec((1, D), lambda b, i: (0, 0)),
                pl.BlockSpec(w.shape, lambda b, i: (0, 0))]
    for o in o_list:
        if o.ndim == 4:
            in_specs.append(pl.BlockSpec((None, o.shape[1], tm, LANES), lambda b, i: (b, 0, i, 0)))
        else:
            in_specs.append(tok(o.shape[2]))
    args = [x, mod, g, w] + list(o_list)
    out_specs = [tok(D), tok(D)]
    out_shape = [jax.ShapeDtypeStruct((B, S, D), _F32), jax.ShapeDtypeStruct((B, S, D), h_dtype)]
    if route:
        assert D == SUBLANES * LANES
        per_b = S // tm
        out_specs[1] = pl.BlockSpec((tm * SUBLANES, LANES), lambda b, i: (b * per_b + i, 0))
        out_shape[1] = jax.ShapeDtypeStruct((B * S * SUBLANES, LANES), _F32)
        in_specs.append(pl.BlockSpec(router.shape, lambda b, i: (0, 0, 0)))
        args.append(router)
        out_specs.append(tok(LANES))
        out_shape.append(jax.ShapeDtypeStruct((B, S, LANES), _F32))
    return pl.pallas_call(
        functools.partial(_oproj_kernel, n_in=n_in, route=route, sub=min(tm, 128)),
        grid=(B, S // tm),
        in_specs=in_specs, out_specs=out_specs, out_shape=out_shape,
        compiler_params=_cparams(("parallel", "parallel")),
        name="oproj_route" if route else "oproj",
    )(*args)


def _swiglu_hidden(x, wg_ref, wu_ref, a_scr, tf, after_chunk=None):
    F = wg_ref.shape[1]
    for c0 in range(0, F, tf):
        c1 = min(c0 + tf, F)
        g = jnp.dot(x, wg_ref[:, c0:c1], preferred_element_type=_F32)
        u = jnp.dot(x, wu_ref[:, c0:c1], preferred_element_type=_F32)
        a_scr[:, c0:c1] = (_silu(g) * u).astype(_BF)
        if after_chunk is not None:
            after_chunk(c0 // tf)


def _ffn_kernel(x_ref, oa_ref, ob_ref, wo_ref, mod_ref, g_ref, modn_ref, gn_ref, wg_ref, wu_ref, wd_ref,
                x_out, h_out, a_scr, *, tf):
    wa = oa_ref.shape[1]
    y = (jnp.dot(oa_ref[...], wo_ref[0:wa, :], preferred_element_type=_F32)
         + jnp.dot(ob_ref[...], wo_ref[wa:, :], preferred_element_type=_F32))
    x1 = x_ref[...] + mod_ref[2:3, :] * y
    h = _modulate(x1, g_ref[...], mod_ref[3:4, :], mod_ref[4:5, :]).astype(_BF)
    _swiglu_hidden(h, wg_ref, wu_ref, a_scr, tf)
    y = jnp.dot(a_scr[...], wd_ref[...], preferred_element_type=_F32)
    x2 = x1 + mod_ref[5:6, :] * y
    x_out[...] = x2
    h_out[...] = _modulate(x2, gn_ref[...], modn_ref[0:1, :], modn_ref[1:2, :]).astype(h_out.dtype)


def _ffn(x, o_a, o_b, w_out, mod, g, modn, gn, wg, wu, wd, tm, tf):
    B, S, D = x.shape
    T = B * S
    F = wg.shape[1]
    per_b = S // tm
    flat = lambda a: a.reshape(T, a.shape[2])
    tok = lambda w: pl.BlockSpec((tm, w), lambda i: (i, 0))
    modspec = pl.BlockSpec((None, 6, D), lambda i: (i // per_b, 0, 0))
    row = pl.BlockSpec((1, D), lambda i: (0, 0))
    once = pl.Buffered(1)
    res = lambda a: pl.BlockSpec(a.shape, lambda i: (0, 0), pipeline_mode=once)
    xo, ho = pl.pallas_call(
        functools.partial(_ffn_kernel, tf=tf),
        grid=(T // tm,),
        in_specs=[tok(D), tok(o_a.shape[2]), tok(o_b.shape[2]), res(w_out), modspec, row, modspec, row,
                  res(wg), res(wu), res(wd)],
        out_specs=[tok(D), tok(D)],
        out_shape=[jax.ShapeDtypeStruct((T, D), _F32), jax.ShapeDtypeStruct((T, D), _BF)],
        scratch_shapes=[pltpu.VMEM((tm, F), _BF)],
        compiler_params=_cparams(("parallel",)),
        name="ffn_dense",
    )(flat(x), flat(o_a), flat(o_b), w_out, mod, g, modn, gn, wg, wu, wd)
    return xo.reshape(B, S, D), ho.reshape(B, S, D)


def _qkv_kernel(h_ref, w_ref, gains_ref, rope_ref, q_ref, k_ref, v_ref, n_scr):
    tm, D = h_ref.shape
    h = h_ref[...]
    lane = lax.broadcasted_iota(jnp.int32, (tm, LANES), 1)
    head_a = (lane & 63) < 32

    def head_pair_norm(v, g):
        sq = v * v
        s_a = jnp.sum(jnp.where(head_a, sq, 0.0), axis=-1, keepdims=True)
        s_b = jnp.sum(jnp.where(head_a, 0.0, sq), axis=-1, keepdims=True)
        r = lax.rsqrt(jnp.where(head_a, s_a, s_b) * (1.0 / HEAD_DIM) + EPS)
        return v * r * g

    y = jnp.dot(h, w_ref[...], preferred_element_type=_F32)
    for p in range(D // LANES):
        sl = slice(p * LANES, (p + 1) * LANES)
        n_scr[:, sl] = head_pair_norm(y[:, sl], gains_ref[0:1, :])
        n_scr[:, D + p * LANES:D + (p + 1) * LANES] = head_pair_norm(
            y[:, D + p * LANES:D + (p + 1) * LANES], gains_ref[1:2, :])
        v_ref[p] = y[:, 2 * D + p * LANES:2 * D + (p + 1) * LANES].astype(_BF)

    @pl.when(pl.program_id(1) >= 0)
    def _():
        c, s = rope_ref[:, 0:128], rope_ref[:, 128:256]
        for p in range(D // LANES):
            vq = n_scr[:, p * LANES:(p + 1) * LANES]
            vk = n_scr[:, D + p * LANES:D + (p + 1) * LANES]
            q_ref[p] = (vq * c + pltpu.roll(vq, 64, 1) * s).astype(_BF)
            k_ref[p] = (vk * c + pltpu.roll(vk, 64, 1) * s).astype(_BF)


def _qkv(h, w, gains, rope, tm):
    B, S, D = h.shape
    tok = pl.BlockSpec((None, tm, D), lambda b, i: (b, i, 0))
    hp = pl.BlockSpec((None, D // LANES, tm, LANES), lambda b, i: (b, 0, i, 0))
    sds = jax.ShapeDtypeStruct((B, D // LANES, S, LANES), _BF)
    return pl.pallas_call(
        _qkv_kernel,
        grid=(B, S // tm),
        in_specs=[tok, pl.BlockSpec(w.shape, lambda b, i: (0, 0)),
                  pl.BlockSpec((2, LANES), lambda b, i: (0, 0)),
                  pl.BlockSpec((tm, 2 * LANES), lambda b, i: (i, 0))],
        out_specs=[hp, hp, hp],
        out_shape=[sds, sds, sds],
        scratch_shapes=[pltpu.VMEM((tm, 2 * D), _F32)],
        compiler_params=_cparams(("parallel", "parallel")),
        name="qkv_dil",
    )(h, w, gains, rope)


DIL_GROUP = 4


def _dil_kernel(fast_ref, q_ref, k_ref, v_ref, shift_ref, o_ref, tmp, qp, kp, vp, acc, mm, ll, accp, mmp,
                llp, bias_scr, *, tl):
    @pl.when(fast_ref[0] == 1)
    def _():
        _dil_body(q_ref, k_ref, v_ref, shift_ref, o_ref, tmp, qp, kp, vp, acc, mm, ll, accp, mmp, llp,
                  bias_scr, tl=tl, fixed_shift=True)

    @pl.when(fast_ref[0] != 1)
    def _():
        _dil_body(q_ref, k_ref, v_ref, shift_ref, o_ref, tmp, qp, kp, vp, acc, mm, ll, accp, mmp, llp,
                  bias_scr, tl=tl, fixed_shift=False)


def _dil_body(q_ref, k_ref, v_ref, shift_ref, o_ref, tmp, qp, kp, vp, acc, mm, ll, accp, mmp, llp,
              bias_scr, *, tl, fixed_shift):
    S = q_ref.shape[0]
    ng = S // DIL_GROUP
    pitch = DIL_GROUP + 1

    def spread(dst, src):
        for j in range(DIL_GROUP):
            dst[pl.ds(j, ng, stride=pitch), :] = src[pl.ds(j, ng, stride=DIL_GROUP), :]

    for src, dst in ((q_ref, qp), (k_ref, kp), (v_ref, vp)):
        tmp[...] = src[...].astype(_F32)
        spread(dst, tmp)
    lane = lax.broadcasted_iota(jnp.int32, (tl, LANES), 1)
    lo = lane < 64
    head_a = (lane & 63) < 32
    for bi, (window, d) in enumerate(DIL_PATTERNS):
        L = S // d
        half = window // (2 * d)
        t = min(tl, L)
        W = min(t + 2 * half, L)
        nt = L // t
        if d > 1:
            assert d % DIL_GROUP == 0
        sd = d * pitch // DIL_GROUP

        def rows(r, first, n, d=d, sd=sd):
            if d == 1:
                return pl.ds(first, n)
            return pl.ds(r + r // DIL_GROUP + sd * first, n, stride=sd)

        assert t % half == 0 and W <= t + 2 * half
        qa = lax.broadcasted_iota(jnp.int32, (t, W), 0)
        kb = lax.broadcasted_iota(jnp.int32, (t, W), 1)
        inside = shift_ref[0:1, 0:W] if fixed_shift else 0.0
        for case in range(3):
            ok = jnp.abs(kb - qa - case * half) <= half
            bias_scr[case, 0:t, 0:W] = jnp.where(ok, inside, NEG)

        def body(idx, carry, bi=bi, d=d, L=L, half=half, t=t, W=W, nt=nt, rows=rows):
            r = idx // nt
            l0 = (idx % nt) * t
            start = jnp.clip(l0 - half, 0, L - W)
            if d == 1:
                qt = q_ref[pl.ds(pl.multiple_of(l0, t), t), :]
                kw = k_ref[pl.ds(pl.multiple_of(start, 64), W), :]
                vw = v_ref[pl.ds(pl.multiple_of(start, 64), W), :]
            else:
                qt = qp[rows(r, l0, t), :].astype(_BF)
                kw = kp[rows(r, start, W), :].astype(_BF)
                vw = vp[rows(r, start, W), :].astype(_BF)
            lo_t = lo[:t]
            qa_t = head_a[:t]
            zq = jnp.zeros_like(qt)
            q2 = jnp.concatenate([jnp.where(qa_t, qt, zq), jnp.where(qa_t, zq, qt)], axis=0)
            s = lax.dot_general(q2, kw, (((1,), (1,)), ((), ())), preferred_element_type=_F32)
            bias = bias_scr[(l0 - start) // half, 0:t, 0:W]
            s = s + jnp.concatenate([bias, bias], axis=0)
            if fixed_shift:
                p = jnp.exp2(s)
            else:
                m = jnp.max(s, axis=-1, keepdims=True)
                p = jnp.exp2(s - m)
                m_new = jnp.where(lo_t, m[:t], m[t:])
            den = jnp.sum(p, axis=-1, keepdims=True)
            o2 = jnp.dot(p.astype(_BF), vw, preferred_element_type=_F32)
            o_new = jnp.where(lo_t, o2[:t], o2[t:])
            l_new = jnp.where(lo_t, den[:t], den[t:])
            tok = rows(r, l0, t)
            if d == 1:
                assert bi == 0
                acc[tok, :] = o_new
                ll[tok, :] = l_new
                if not fixed_shift:
                    mm[tok, :] = m_new
            elif fixed_shift:
                accp[tok, :] = accp[tok, :] + o_new
                llp[tok, :] = llp[tok, :] + l_new
            else:
                m_old = mmp[tok, :]
                mx = jnp.maximum(m_old, m_new)
                a_old = jnp.exp2(m_old - mx)
                a_new = jnp.exp2(m_new - mx)
                accp[tok, :] = accp[tok, :] * a_old + o_new * a_new
                llp[tok, :] = llp[tok, :] * a_old + l_new * a_new
                mmp[tok, :] = mx
            return carry

        lax.fori_loop(0, d * nt, body, 0, unroll=min(32 if fixed_shift else 8, d * nt))
        if bi == 0:
            spread(accp, acc)
            spread(llp, ll)
            if not fixed_shift:
                spread(mmp, mm)
    for j in range(DIL_GROUP):
        grp = pl.ds(j, ng, stride=pitch)
        tmp[pl.ds(j, ng, stride=DIL_GROUP), :] = accp[grp, :] / llp[grp, :]
    o_ref[...] = tmp[...].astype(o_ref.dtype)


def _dilated(fast, shift, q, k, v, tl):
    B, P, S, _ = q.shape
    wmax = tl + max(w // d for w, d in DIL_PATTERNS)
    spec = pl.BlockSpec((None, None, S, LANES), lambda b, p, f: (b, p, 0, 0))
    scr = pltpu.VMEM((S, LANES), _F32)
    scrp = pltpu.VMEM((S // DIL_GROUP * (DIL_GROUP + 1), LANES), _F32)
    grid_spec = pltpu.PrefetchScalarGridSpec(
        num_scalar_prefetch=1,
        grid=(B, P),
        in_specs=[spec, spec, spec, pl.BlockSpec((1, wmax), lambda b, p, f: (0, 0))],
        out_specs=spec,
        scratch_shapes=[scr, scrp, scrp, scrp, scr, scr, scr, scrp, scrp, scrp,
                        pltpu.VMEM((3, tl, wmax), _F32)],
    )
    return pl.pallas_call(
        functools.partial(_dil_kernel, tl=tl),
        grid_spec=grid_spec,
        out_shape=jax.ShapeDtypeStruct((B, P, S, LANES), _BF),
        compiler_params=_cparams(("parallel", "parallel")),
        name="dil_attn",
    )(fast, q, k, v, jnp.full((1, wmax), shift, _F32))


def _moe_kernel(te_ref, tv_ref, src_hbm, dst_hbm, h_hbm, wg_ref, wu_ref, wd_ref, out_hbm,
                xbuf, xbf, a_scr, ybuf, src_s0, src_s1, dst_s0, dst_s1, gsem, ssem, isem,
                *, tf, dump0, n_dump):
    i = pl.program_id(0)
    nt = pl.num_programs(0)
    tm = xbf.shape[0]
    slot = i % 2

    def is_valid(t):
        return jnp.logical_and(jnp.logical_and(t >= 0, t < nt), tv_ref[jnp.clip(t, 0, nt - 1)] == 1)

    def both(a, b):
        return jnp.logical_and(a, b)

    valid = is_valid(i)
    prev_valid = is_valid(i - 1)

    src_bufs = (src_s0, src_s1)
    dst_bufs = (dst_s0, dst_s1)

    def src_copy(tile, par):
        return pltpu.make_async_copy(src_hbm.at[tile], src_bufs[par], isem.at[par])

    def dst_copy(tile, par):
        return pltpu.make_async_copy(dst_hbm.at[tile], dst_bufs[par], isem.at[2 + par])

    def tile_at(first):
        return pl.ds(pl.multiple_of(first, SUBLANES), SUBLANES)

    def gather_row(r, par):
        return pltpu.make_async_copy(h_hbm.at[tile_at(src_bufs[par][r])],
                                     xbuf.at[par, tile_at(r * SUBLANES)], gsem.at[par])

    def scatter_row(r, par):
        return pltpu.make_async_copy(ybuf.at[par, tile_at(r * SUBLANES)],
                                     out_hbm.at[tile_at(dst_bufs[par][r])], ssem.at[par])

    def gather_wait(par):
        return pltpu.make_async_copy(h_hbm.at[pl.ds(0, tm * SUBLANES)], xbuf.at[par], gsem.at[par])

    def scatter_wait(par):
        return pltpu.make_async_copy(ybuf.at[par], out_hbm.at[pl.ds(0, tm * SUBLANES)], ssem.at[par])

    def for_rows(fn):
        def body(r, c):
            fn(r)
            return c
        lax.fori_loop(0, tm, body, 0, unroll=8)

    @pl.when(i == 0)
    def _():
        ybuf[...] = jnp.zeros_like(ybuf)
        for k in range(n_dump):
            pltpu.make_async_copy(ybuf.at[1], out_hbm.at[pl.ds((dump0 + k * tm) * SUBLANES, tm * SUBLANES)],
                                  ssem.at[1]).start()
        for k in range(n_dump):
            scatter_wait(1).wait()

    @pl.when(both(i == 0, valid))
    def _():
        src_copy(0, 0).start()
        src_copy(0, 0).wait()
        for_rows(lambda r: gather_row(r, 0).start())
        src_copy(1, 1).start()

    def phase1(par, with_scatter):
        xbf[...] = _tiles_to_rows(xbuf.at[par], tm).astype(_BF)
        n_chunk = -(-wg_ref.shape[1] // tf)
        per = -(-tm // n_chunk)

        def after(c):
            for r in range(c * per, min((c + 1) * per, tm)):
                gather_row(r, 1 - par).start()
                if with_scatter:
                    scatter_row(r, 1 - par).start()

        _swiglu_hidden(xbf[...], wg_ref, wu_ref, a_scr, tf, after)
        _rows_to_tiles(ybuf.at[par], jnp.dot(a_scr[...], wd_ref[...], preferred_element_type=_F32))

    for par in range(2):
        here = slot == par

        @pl.when(both(valid, here))
        def _(par=par):
            dst_copy(i, par).start()
            src_copy(i + 1, 1 - par).wait()

        @pl.when(both(prev_valid, here))
        def _(par=par):
            dst_copy(i - 1, 1 - par).wait()

        @pl.when(both(jnp.logical_or(both(i == 0, valid), prev_valid), here))
        def _(par=par):
            gather_wait(par).wait()

        @pl.when(both(is_valid(i - 2), here))
        def _(par=par):
            scatter_wait(par).wait()

        @pl.when(both(both(valid, prev_valid), here))
        def _(par=par):
            phase1(par, True)

        if par == 0:
            @pl.when(both(both(valid, jnp.logical_not(prev_valid)), here))
            def _():
                phase1(0, False)

        @pl.when(both(both(jnp.logical_not(valid), prev_valid), here))
        def _(par=par):
            for_rows(lambda r: scatter_row(r, 1 - par).start())

        @pl.when(both(is_valid(i + 1), here))
        def _(par=par):
            src_copy(i + 2, par).start()


def _moe(h, tile_e, tile_v, src, dst, wg, wu, wd, n_out_rows, tm, tf):
    T = h.shape[0] // SUBLANES
    D = wg.shape[1]
    nt = tile_e.shape[0]
    F = wg.shape[2]
    dump0 = T * TOP_K
    once = pl.Buffered(1)
    grid_spec = pltpu.PrefetchScalarGridSpec(
        num_scalar_prefetch=2,
        grid=(nt,),
        in_specs=[pl.BlockSpec(memory_space=pl.ANY), pl.BlockSpec(memory_space=pl.ANY),
                  pl.BlockSpec(memory_space=pl.ANY),
                  pl.BlockSpec((None, D, F), lambda i, te, tv: (te[i], 0, 0), pipeline_mode=once),
                  pl.BlockSpec((None, D, F), lambda i, te, tv: (te[i], 0, 0), pipeline_mode=once),
                  pl.BlockSpec((None, F, D), lambda i, te, tv: (te[i], 0, 0), pipeline_mode=once)],
        out_specs=pl.BlockSpec(memory_space=pl.ANY),
        scratch_shapes=[pltpu.VMEM((2, tm * SUBLANES, LANES), _F32), pltpu.VMEM((tm, D), _BF),
                        pltpu.VMEM((tm, F), _BF), pltpu.VMEM((2, tm * SUBLANES, LANES), _F32),
                        pltpu.SMEM((tm,), jnp.int32), pltpu.SMEM((tm,), jnp.int32),
                        pltpu.SMEM((tm,), jnp.int32), pltpu.SMEM((tm,), jnp.int32),
                        pltpu.SemaphoreType.DMA((2,)), pltpu.SemaphoreType.DMA((2,)),
                        pltpu.SemaphoreType.DMA((4,))],
    )
    return pl.pallas_call(
        functools.partial(_moe_kernel, tf=tf, dump0=dump0, n_dump=(n_out_rows - dump0) // tm),
        grid_spec=grid_spec,
        out_shape=jax.ShapeDtypeStruct((n_out_rows * SUBLANES, LANES), _F32),
        compiler_params=_cparams(("arbitrary",)),
        name="moe_experts",
    )(tile_e, tile_v, src, dst, h, wg, wu, wd)


def _route_plan(route, T, tm):
    A = T * TOP_K
    e_flat = route[:, :TOP_K].astype(jnp.int32).reshape(A)
    order = jnp.argsort(e_flat, stable=True).astype(jnp.int32)
    counts = jnp.sum((e_flat[:, None] == jnp.arange(N_EXPERTS, dtype=jnp.int32)[None, :]).astype(jnp.int32), axis=0)
    starts = jnp.cumsum(counts) - counts
    pcounts = (counts + tm - 1) // tm * tm
    pends = jnp.cumsum(pcounts)
    pstarts = pends - pcounts
    nt = A // tm + N_EXPERTS + 2
    tile0 = jnp.arange(nt, dtype=jnp.int32) * tm
    tile_v = (tile0 < pends[-1]).astype(jnp.int32)
    last_valid = jnp.maximum(pends[-1] // tm - 1, 0)
    tile_e_raw = jnp.minimum(jnp.searchsorted(pends, tile0, side="right"), N_EXPERTS - 1).astype(jnp.int32)
    tile_e = jnp.where(tile_v == 1, tile_e_raw, tile_e_raw[last_valid])
    r = jnp.arange(nt * tm, dtype=jnp.int32)
    e_r = jnp.repeat(tile_e, tm)
    within = r - pstarts[e_r]
    ok = jnp.logical_and(within < counts[e_r], jnp.repeat(tile_v, tm) == 1)
    a = order[jnp.clip(starts[e_r] + within, 0, A - 1)]
    src = jnp.where(ok, a // TOP_K, 0).astype(jnp.int32)
    dump = A + e_r * tm + jnp.clip(within - counts[e_r], 0, tm - 1)
    dst = jnp.where(ok, (a % TOP_K) * T + a // TOP_K, dump).astype(jnp.int32)
    return (tile_e, tile_v, (src * SUBLANES).reshape(nt, tm), (dst * SUBLANES).reshape(nt, tm),
            A + N_EXPERTS * tm)


def _combine_kernel(x_ref, y1_ref, y2_ref, r_ref, mod_ref, o_ref):
    tm = x_ref.shape[0]
    y = _tiles_to_rows(y1_ref, tm) * r_ref[:, 2:3] + _tiles_to_rows(y2_ref, tm) * r_ref[:, 3:4]
    o_ref[...] = x_ref[...] + mod_ref[5:6, :] * y


def _combine(x, y2, route, mod, tm):
    B, S, D = x.shape
    T = B * S
    per_b = S // tm
    out = pl.pallas_call(
        _combine_kernel,
        grid=(T // tm,),
        in_specs=[pl.BlockSpec((tm, D), lambda i: (i, 0)),
                  pl.BlockSpec((tm * SUBLANES, LANES), lambda i: (i, 0)),
                  pl.BlockSpec((tm * SUBLANES, LANES), lambda i: (T // tm + i, 0)),
                  pl.BlockSpec((tm, LANES), lambda i: (i, 0)),
                  pl.BlockSpec((None, 6, D), lambda i: (i // per_b, 0, 0))],
        out_specs=pl.BlockSpec((tm, D), lambda i: (i, 0)),
        out_shape=jax.ShapeDtypeStruct((T, D), _F32),
        compiler_params=_cparams(("parallel",)),
        name="moe_combine",
    )(x.reshape(T, D), y2, y2, route.reshape(T, LANES), mod)
    return out.reshape(B, S, D)


def _pad_cols(a, w):
    return jnp.pad(a, ((0, 0), (0, w - a.shape[1])))


def _rope_tables_even(S):
    pos = jnp.arange(S, dtype=jnp.int32)
    inv = ROPE_THETA ** (-jnp.arange(0, 32, 2, dtype=_F32) / 32)
    def cs(p):
        ang = p.astype(_F32)[:, None] * inv[None, :]
        return jnp.cos(ang), jnp.sin(ang)
    one = lambda w: jnp.ones((S, w), _F32)
    zero = lambda w: jnp.zeros((S, w), _F32)
    c, s = cs(pos)
    ca = jnp.concatenate([c, one(48), c, one(48)], 1)
    sa = jnp.concatenate([-s, zero(48), s, zero(48)], 1)
    cr, sr = cs(pos // GRID_W)
    cc, sc = cs(pos % GRID_W)
    cb = jnp.concatenate([cr, cc, one(32), cr, cc, one(32)], 1)
    sb = jnp.concatenate([-sr, -sc, zero(32), sr, sc, zero(32)], 1)
    return jnp.concatenate([ca, sa, cb, sb], 1)


def _slot_maps():
    r = MLA_ROPE // 2
    mla = ([MLA_NOPE + i for i in range(r)] + list(range(0, 64 - r))
           + [MLA_NOPE + r + i for i in range(r)] + list(range(64 - r, MLA_NOPE)))
    mla += [-1] * (LANES - len(mla))
    q = HEAD_DIM // 4
    gqa = (list(range(0, q)) + list(range(2 * q, 3 * q)) + [-1] * (64 - 2 * q)
           + list(range(q, 2 * q)) + list(range(3 * q, 4 * q)) + [-1] * (64 - 2 * q))
    return mla, gqa


def _to_slots(a, lane_map):
    idx = jnp.asarray([max(i, 0) for i in lane_map], jnp.int32)
    keep = jnp.asarray([1.0 if i >= 0 else 0.0 for i in lane_map], a.dtype)
    return jnp.take(a, idx, axis=-1) * keep


def _rope_tables_odd(S):
    pos = jnp.arange(S, dtype=_F32)
    inv = ROPE_THETA ** (-jnp.arange(0, HEAD_DIM, 2, dtype=_F32) / HEAD_DIM)
    ang = pos[:, None] * inv[None, :]
    c, s = jnp.cos(ang), jnp.sin(ang)
    return jnp.concatenate([c, c, c, c, -s, -s, s, s], 1)


def _tiles(S):
    return dict(tm_pre=min(S, 512), tq=min(S, 2048), tm_o=min(S, 512),
                tm_ffn=min(S, 512), tf_ffn=512, tm_qkv=min(S, 512),
                tl=128, tm_moe=min(S, 512), tf_moe=512, tm_c=min(S, 512))


def kernel(x, c, ada_even_w, ada_even_b, norm_even_mix, norm_even_ffn, even_w_in, mla_q_norm, mla_w_uq, mla_kv_norm, mla_w_ukv, mla_q_gain, mla_k_gain, gqa_q_gain, gqa_k_gain, even_w_out, ffn_w_gate, ffn_w_up, ffn_w_down, ada_odd_w, ada_odd_b, norm_odd_mix, norm_odd_ffn, dil_w_qkv, dil_q_gain, dil_k_gain, dil_w_out, moe_router, moe_w_gate, moe_w_up, moe_w_down):
    B, S, D = x.shape
    T = B * S
    cfg = _tiles(S)

    mod_e = _ada_mod(c, ada_even_w[0], ada_even_b[0]).reshape(B, 6, D)
    mod_o = _ada_mod(c, ada_odd_w[0], ada_odd_b[0]).reshape(B, 6, D)

    w = even_w_in[0]
    sp = [MLA_Q_RANK, MLA_Q_RANK + MLA_KV_RANK, MLA_Q_RANK + MLA_KV_RANK + MLA_ROPE]
    sp.append(sp[-1] + GQA_HEADS * HEAD_DIM)
    sp.append(sp[-1] + GQA_KV_HEADS * HEAD_DIM)
    w_cq, w_ckv, w_kpe = w[:, :sp[0]], w[:, sp[0]:sp[1]], w[:, sp[1]:sp[2]]
    w_qb, w_kb, w_vb = w[:, sp[2]:sp[3]], w[:, sp[3]:sp[4]], w[:, sp[4]:]
    mla_map, gqa_map = _slot_maps()
    na = MLA_NOPE + MLA_ROPE
    nope_only = [i if 0 <= i < MLA_NOPE else -1 for i in mla_map]
    rope_only = [i - MLA_NOPE if i >= MLA_NOPE else -1 for i in mla_map]
    gslots = lambda a, n: _to_slots(a.reshape(D, n, HEAD_DIM), gqa_map).reshape(D, n * LANES)
    w_in = jnp.concatenate([w_cq, w_ckv, _to_slots(w_kpe, rope_only), gslots(w_qb, GQA_HEADS),
                            gslots(w_kb, GQA_KV_HEADS)], axis=1).astype(_BF)
    w_uq = _to_slots(mla_w_uq[0].reshape(MLA_Q_RANK, MLA_HEADS, na), mla_map)
    w_uq = w_uq.reshape(MLA_Q_RANK, MLA_HEADS * LANES).astype(_BF)
    w_ukv = mla_w_ukv[0].reshape(MLA_KV_RANK, MLA_HEADS, MLA_NOPE + MLA_V)
    w_uk = _to_slots(w_ukv[:, :, :MLA_NOPE], nope_only).reshape(MLA_KV_RANK, MLA_HEADS * LANES).astype(_BF)
    w_uvt = w_ukv[:, :, MLA_NOPE:].reshape(MLA_KV_RANK, MLA_HEADS * MLA_V).T.astype(_BF)
    w_vbt = w_vb.T.astype(_BF)
    g_rows = [_to_slots(mla_q_gain[0], mla_map) * (na ** -0.5 * LOG2E), _to_slots(mla_k_gain[0], mla_map),
              _to_slots(gqa_q_gain[0], gqa_map) * (HEAD_DIM ** -0.5 * LOG2E), _to_slots(gqa_k_gain[0], gqa_map)]
    bound_a = na * jnp.max(jnp.abs(g_rows[0])) * jnp.max(jnp.abs(g_rows[1]))
    bound_b = HEAD_DIM * jnp.max(jnp.abs(g_rows[2])) * jnp.max(jnp.abs(g_rows[3]))
    fast_a, fast_b = bound_a <= SCORE_BOUND_MAX, bound_b <= SCORE_BOUND_MAX
    last_lane = jnp.zeros((LANES,), _F32).at[LANES - 1].set(1.0)
    g_rows += [last_lane, last_lane * jnp.where(fast_a, -bound_a, 0.0),
               last_lane, last_lane * jnp.where(fast_b, -bound_b, 0.0)]
    gains_e = jnp.stack(g_rows)
    rope_e = _rope_tables_even(S)
    q_all, k_all, vt_all = _pre_even(
        x, mod_e, norm_even_mix[0].reshape(1, D), w_in, mla_q_norm[0].reshape(1, -1), w_uq,
        mla_kv_norm[0].reshape(1, -1), w_uk, w_uvt, w_vbt, gains_e, rope_e, cfg["tm_pre"])
    as_flag = lambda f: f.astype(jnp.int32).reshape(1)
    o_a = _attention(as_flag(fast_a), q_all, k_all, vt_all, q_off=0, k_off=0, shared_kv=False, tq=cfg["tq"])
    o_b = _attention(as_flag(fast_b), q_all, k_all, vt_all, q_off=MLA_HEADS, k_off=MLA_HEADS,
                     shared_kv=True, tq=cfg["tq"])
    x2, h2 = _ffn(x, o_a, o_b, even_w_out[0].astype(_BF), mod_e, norm_even_ffn[0].reshape(1, D), mod_o,
                  norm_odd_mix[0].reshape(1, D), ffn_w_gate[0].astype(_BF), ffn_w_up[0].astype(_BF),
                  ffn_w_down[0].astype(_BF), cfg["tm_ffn"], cfg["tf_ffn"])

    hh = HEAD_DIM // 2
    pair = lambda v: jnp.concatenate([v[:hh], v[:hh], v[hh:], v[hh:]])
    gains_o = jnp.stack([pair(dil_q_gain[0]) * (HEAD_DIM ** -0.5 * LOG2E), pair(dil_k_gain[0])])
    wq, wk, wv = jnp.split(dil_w_qkv[0], 3, axis=1)
    perm = lambda w: w.reshape(D, D // LANES, 2, 2, hh).transpose(0, 1, 3, 2, 4).reshape(D, D)
    w_qkv = jnp.concatenate([perm(wq), perm(wk), wv], axis=1).astype(_BF)
    qd, kd, vd = _qkv(h2, w_qkv, gains_o, _rope_tables_odd(S), cfg["tm_qkv"])
    bound_d = HEAD_DIM * jnp.max(jnp.abs(gains_o[0])) * jnp.max(jnp.abs(gains_o[1]))
    fast_d = bound_d <= SCORE_BOUND_MAX
    o_d = _dilated(as_flag(fast_d), jnp.where(fast_d, -bound_d, 0.0), qd, kd, vd, cfg["tl"])
    r32 = _pad_cols(moe_router[0], LANES)
    r_hi = r32.astype(_BF)
    r_lo = (r32 - r_hi.astype(_F32)).astype(_BF)
    x3, h3, route = _oproj(x2, mod_o, norm_odd_ffn[0].reshape(1, D), dil_w_out[0].astype(_BF), [o_d],
                           cfg["tm_o"], _F32, router=jnp.stack([r_hi, r_lo]))
    tm = cfg["tm_moe"]
    tile_e, tile_v, src, dst, n_rows = _route_plan(route.reshape(T, LANES), T, tm)
    y2 = _moe(h3, tile_e, tile_v, src, dst, moe_w_gate[0].astype(_BF),
              moe_w_up[0].astype(_BF), moe_w_down[0].astype(_BF), n_rows, tm, cfg["tf_moe"])
    return _combine(x3, y2, route, mod_o, cfg["tm_c"])
```

```python
import functools
import math

import jax
import jax.numpy as jnp
from jax import lax
from jax.experimental import pallas as pl
from jax.experimental.pallas import tpu as pltpu

_BF = jnp.bfloat16
_F32 = jnp.float32

GRID_W = 64
HEAD_DIM = 64
ROPE_THETA = 10000.0
EPS = 1e-6
MLA_HEADS = 8
MLA_Q_RANK = 256
MLA_KV_RANK = 128
MLA_NOPE = 64
MLA_ROPE = 32
MLA_V = 64
GQA_HEADS = 8
GQA_KV_HEADS = 2
DIL_PATTERNS = ((128, 1), (512, 4), (2048, 16))
N_EXPERTS = 8
TOP_K = 2
NEG = -1e30
LOG2E = math.log2(math.e)

LANES = 128
VMEM_LIMIT = 56 * 1024 * 1024


def _cparams(sem, vmem=VMEM_LIMIT):
    return pltpu.CompilerParams(dimension_semantics=sem, vmem_limit_bytes=vmem)


def _silu(x):
    return x / (1.0 + jnp.exp(-x))


def _modulate(x, g, shift, scale):
    ms = jnp.mean(x * x, axis=-1, keepdims=True)
    return x * lax.rsqrt(ms + EPS) * g * (1.0 + scale) + shift


def _norm(v, n, g):
    return v * lax.rsqrt(jnp.sum(v * v, axis=-1, keepdims=True) * (1.0 / n) + EPS) * g


SUBLANES = 8


def _rows_to_tiles(ref, x, row0=0):
    tm = x.shape[0]
    for c in range(SUBLANES):
        ref[pl.ds(row0 * SUBLANES + c, tm, stride=SUBLANES), :] = x[:, c * LANES:(c + 1) * LANES]


def _tiles_to_rows(ref, tm):
    return jnp.concatenate([ref[pl.ds(c, tm, stride=SUBLANES), :] for c in range(SUBLANES)], axis=1)


def _mod_kernel(c_ref, w_ref, b_ref, o_ref):
    sc = _silu(c_ref[...])
    o_ref[...] = jnp.dot(sc.astype(_BF), w_ref[...].astype(_BF),
                         preferred_element_type=_F32) + b_ref[...]


def _ada_mod(c, w, b):
    B, D = c.shape
    N = w.shape[1]
    tn = min(N, 1536)
    return pl.pallas_call(
        _mod_kernel,
        grid=(N // tn,),
        in_specs=[pl.BlockSpec((B, D), lambda j: (0, 0)),
                  pl.BlockSpec((D, tn), lambda j: (0, j)),
                  pl.BlockSpec((1, tn), lambda j: (0, j))],
        out_specs=pl.BlockSpec((B, tn), lambda j: (0, j)),
        out_shape=jax.ShapeDtypeStruct((B, N), _F32),
        compiler_params=_cparams(("arbitrary",)),
        name="ada_mod",
    )(c, w, b.reshape(1, N))


def _pre_even_kernel(x_ref, mod_ref, g_ref, w_in_ref, qn_ref, w_uq_ref, kvn_ref, w_uk_ref, w_uvt_ref,
                     w_vbt_ref, gains_ref, rope_ref, q_ref, k_ref, vt_ref, nq_scr, nk_scr):
    h = _modulate(x_ref[...], g_ref[...], mod_ref[0:1, :], mod_ref[1:2, :]).astype(_BF)
    y = jnp.dot(h, w_in_ref[...], preferred_element_type=_F32)
    cqn = _norm(y[:, 0:256], MLA_Q_RANK, qn_ref[...]).astype(_BF)
    qa = jnp.dot(cqn, w_uq_ref[...], preferred_element_type=_F32)
    ckvn = _norm(y[:, 256:384], MLA_KV_RANK, kvn_ref[...]).astype(_BF)
    kn = jnp.dot(ckvn, w_uk_ref[...], preferred_element_type=_F32)
    kpe = y[:, 384:512]
    nt_dims = (((1,), (1,)), ((), ()))
    vt_a = lax.dot_general(w_uvt_ref[...], ckvn, nt_dims, preferred_element_type=_F32).astype(_BF)
    vt_b = lax.dot_general(w_vbt_ref[...], h, nt_dims, preferred_element_type=_F32).astype(_BF)
    tm = x_ref.shape[0]
    ones_blk = jnp.where(lax.broadcasted_iota(jnp.int32, (LANES - MLA_V, tm), 0) == 0, 1.0, 0.0).astype(_BF)
    for hh in range(MLA_HEADS + GQA_KV_HEADS):
        src = vt_a[hh * MLA_V:(hh + 1) * MLA_V] if hh < MLA_HEADS else \
            vt_b[(hh - MLA_HEADS) * HEAD_DIM:(hh - MLA_HEADS + 1) * HEAD_DIM]
        vt_ref[hh * LANES:hh * LANES + MLA_V, :] = src
        vt_ref[hh * LANES + MLA_V:(hh + 1) * LANES, :] = ones_blk
    ca, sa = rope_ref[:, 0:128], rope_ref[:, 128:256]
    cb, sb = rope_ref[:, 256:384], rope_ref[:, 384:512]
    gqa, gka = gains_ref[0:1, :], gains_ref[1:2, :]
    gqb, gkb = gains_ref[2:3, :], gains_ref[3:4, :]
    na = MLA_NOPE + MLA_ROPE

    for hh in range(MLA_HEADS):
        sl = slice(hh * LANES, (hh + 1) * LANES)
        nq_scr[:, sl] = _norm(qa[:, sl], na, gqa)
        nk_scr[:, sl] = _norm(kn[:, sl] + kpe, na, gka)
    for hh in range(GQA_HEADS):
        src = slice(512 + hh * LANES, 512 + (hh + 1) * LANES)
        dst = slice((MLA_HEADS + hh) * LANES, (MLA_HEADS + hh + 1) * LANES)
        nq_scr[:, dst] = _norm(y[:, src], HEAD_DIM, gqb)
    for g in range(GQA_KV_HEADS):
        src = slice(1536 + g * LANES, 1536 + (g + 1) * LANES)
        dst = slice((MLA_HEADS + g) * LANES, (MLA_HEADS + g + 1) * LANES)
        nk_scr[:, dst] = _norm(y[:, src], HEAD_DIM, gkb)

    @pl.when(pl.program_id(1) >= 0)
    def _():
        def rope(ref, out, n_a, n_all, row):
            for hh in range(n_all):
                sl = slice(hh * LANES, (hh + 1) * LANES)
                c, s = (ca, sa) if hh < n_a else (cb, sb)
                b = gains_ref[row:row + 1, :] if hh < n_a else gains_ref[row + 2:row + 3, :]
                v = ref[:, sl]
                out[:, sl] = (v * c + pltpu.roll(v, 64, 1) * s + b).astype(_BF)

        rope(nq_scr, q_ref, MLA_HEADS, MLA_HEADS + GQA_HEADS, 4)
        rope(nk_scr, k_ref, MLA_HEADS, MLA_HEADS + GQA_KV_HEADS, 5)


def _pre_even(x, mod, g, w_in, qn, w_uq, kvn, w_uk, w_uvt, w_vbt, gains, rope, tm):
    B, S, D = x.shape
    nq = (MLA_HEADS + GQA_HEADS) * LANES
    nk = (MLA_HEADS + GQA_KV_HEADS) * LANES
    nvt = (MLA_HEADS + GQA_KV_HEADS) * LANES
    full = lambda a: pl.BlockSpec(a.shape, lambda b, i: (0,) * a.ndim)
    tok = lambda w: pl.BlockSpec((None, tm, w), lambda b, i: (b, i, 0))
    return pl.pallas_call(
        _pre_even_kernel,
        grid=(B, S // tm),
        in_specs=[tok(D),
                  pl.BlockSpec((None, 6, D), lambda b, i: (b, 0, 0)),
                  full(g), full(w_in), full(qn), full(w_uq), full(kvn), full(w_uk), full(w_uvt),
                  full(w_vbt), full(gains),
                  pl.BlockSpec((tm, rope.shape[1]), lambda b, i: (i, 0))],
        out_specs=[tok(nq), tok(nk), pl.BlockSpec((None, None, nvt, tm), lambda b, i: (b, i, 0, 0))],
        out_shape=[jax.ShapeDtypeStruct((B, S, nq), _BF), jax.ShapeDtypeStruct((B, S, nk), _BF),
                   jax.ShapeDtypeStruct((B, S // tm, nvt, tm), _BF)],
        scratch_shapes=[pltpu.VMEM((tm, nq), _F32), pltpu.VMEM((tm, nk), _F32)],
        compiler_params=_cparams(("parallel", "parallel")),
        name="pre_even",
    )(x, mod, g, w_in, qn, w_uq, kvn, w_uk, w_uvt, w_vbt, gains, rope)


SCORE_BOUND_MAX = 60.0


def _attn_kernel(fast_ref, q_ref, k_ref, vt_ref, o_ref, s_scr, *, shared_kv):
    @pl.when(fast_ref[0] == 1)
    def _():
        _attn_fixed_shift(q_ref, k_ref, vt_ref, o_ref, shared_kv=shared_kv)

    @pl.when(fast_ref[0] != 1)
    def _():
        _attn_running_max(q_ref, k_ref, vt_ref, o_ref, s_scr, shared_kv=shared_kv)


def _attn_fixed_shift(q_ref, k_ref, vt_ref, o_ref, *, shared_kv):
    tq = q_ref.shape[0]
    nk, _, tk = vt_ref.shape
    qs = [q_ref[:, hh * LANES:(hh + 1) * LANES] for hh in range(2)]

    def body(j, carry):
        r0 = pl.multiple_of(j * tk, tk)
        out = []
        for hh in range(2):
            kc = 0 if shared_kv else hh * LANES
            k = k_ref[pl.ds(r0, tk), kc:kc + LANES]
            st = lax.dot_general(k, qs[hh], (((1,), (1,)), ((), ())), preferred_element_type=_F32)
            p = jnp.exp2(st).astype(_BF)
            out.append(carry[hh] + jnp.dot(vt_ref[j, kc:kc + LANES, :], p, preferred_element_type=_F32))
        return tuple(out)

    z = jnp.zeros((LANES, tq), _F32)
    res = lax.fori_loop(0, nk, body, (z, z), unroll=True)
    nv = HEAD_DIM
    halves = [acc[0:nv, :] / acc[nv:nv + 1, :] for acc in res]
    o_ref[...] = jnp.concatenate(halves, axis=0).T.astype(o_ref.dtype)


def _attn_running_max(q_ref, k_ref, vt_ref, o_ref, s_scr, *, shared_kv):
    tq = q_ref.shape[0]
    nk, _, tk = vt_ref.shape
    qs = [q_ref[:, hh * LANES:(hh + 1) * LANES] for hh in range(2)]

    def scores(j, hh):
        r0 = j * tk if isinstance(j, int) else pl.multiple_of(j * tk, tk)
        kc = 0 if shared_kv else hh * LANES
        k = k_ref[pl.ds(r0, tk), kc:kc + LANES]
        return lax.dot_general(k, qs[hh], (((1,), (1,)), ((), ())), preferred_element_type=_F32)

    def put_scores(j, hh, slot):
        st = scores(j, hh)
        s_scr[hh, slot] = st
        return jnp.max(st, axis=0, keepdims=True)

    def step(j, hh, carry, cur, last=False):
        m, mc, acc = carry
        mc_next = mc if last else put_scores(j + 1, hh, 1 - cur)
        st = s_scr[hh, cur]
        m_new = jnp.maximum(m, mc)
        alpha = jnp.exp2(m - m_new)
        p = jnp.exp2(st - m_new).astype(_BF)
        vc = 0 if shared_kv else hh * LANES
        acc = alpha * acc + jnp.dot(vt_ref[j, vc:vc + LANES, :], p, preferred_element_type=_F32)
        return m_new, mc_next, acc

    def body(jj, carry):
        c = list(carry)
        for sub in range(2):
            for hh in range(2):
                c[hh] = step(2 * jj + sub, hh, c[hh], sub)
        return tuple(c)

    init = tuple((jnp.full((1, tq), NEG, _F32), put_scores(0, hh, 0), jnp.zeros((LANES, tq), _F32))
                 for hh in range(2))
    res = lax.fori_loop(0, nk // 2 - 1, body, init)
    res = list(res)
    for sub in range(2):
        for hh in range(2):
            res[hh] = step(nk - 2 + sub, hh, res[hh], sub, last=(sub == 1))
    nv = HEAD_DIM
    halves = [acc[0:nv, :] / acc[nv:nv + 1, :] for (_, _, acc) in res]
    o_ref[...] = jnp.concatenate(halves, axis=0).T.astype(o_ref.dtype)


def _attention(fast, q, k, vt, *, q_off, k_off, shared_kv, tq):
    B, S, _ = q.shape
    nk, _, tk = vt.shape[1:]
    assert nk % 2 == 0 and nk >= 2
    npairs = 4
    if shared_kv:
        kspec = pl.BlockSpec((None, S, LANES), lambda b, p, i, f: (b, 0, k_off + p // 2))
        vspec = pl.BlockSpec((None, nk, LANES, tk), lambda b, p, i, f: (b, 0, k_off + p // 2, 0))
    else:
        kspec = pl.BlockSpec((None, S, 2 * LANES), lambda b, p, i, f: (b, 0, k_off // 2 + p))
        vspec = pl.BlockSpec((None, nk, 2 * LANES, tk), lambda b, p, i, f: (b, 0, k_off // 2 + p, 0))
    grid_spec = pltpu.PrefetchScalarGridSpec(
        num_scalar_prefetch=1,
        grid=(B, npairs, S // tq),
        in_specs=[pl.BlockSpec((None, tq, 2 * LANES), lambda b, p, i, f: (b, i, q_off // 2 + p)),
                  kspec, vspec],
        out_specs=pl.BlockSpec((None, tq, LANES), lambda b, p, i, f: (b, i, p)),
        scratch_shapes=[pltpu.VMEM((2, 2, tk, tq), _F32)],
    )
    return pl.pallas_call(
        functools.partial(_attn_kernel, shared_kv=shared_kv),
        grid_spec=grid_spec,
        out_shape=jax.ShapeDtypeStruct((B, S, npairs * LANES), _BF),
        compiler_params=_cparams(("parallel", "parallel", "parallel")),
        name="attn_gqa" if shared_kv else "attn_mla",
    )(fast, q, k, vt)


def _oproj_route_kernel(x_ref, mod_ref, g_ref, w_ref, o_ref, r_ref, x_out, h_out, route_out, *, sub):
    for r0 in range(0, x_ref.shape[0], sub):
        rows = slice(r0, r0 + sub)
        o = jnp.concatenate([o_ref[p, rows, :] for p in range(o_ref.shape[0])], axis=1)
        y = jnp.dot(o, w_ref[...], preferred_element_type=_F32)
        x1 = x_ref[rows, :] + mod_ref[2:3, :] * y
        x_out[rows, :] = x1
        h = _modulate(x1, g_ref[...], mod_ref[3:4, :], mod_ref[4:5, :])
        _rows_to_tiles(h_out, h, r0)
        hh = h.astype(_BF)
        hl = (h - hh.astype(_F32)).astype(_BF)
        rh, rl = r_ref[0], r_ref[1]
        logits = (jnp.dot(hh, rh, preferred_element_type=_F32)
                  + jnp.dot(hh, rl, preferred_element_type=_F32)
                  + jnp.dot(hl, rh, preferred_element_type=_F32))
        tm = logits.shape[0]
        lane = lax.broadcasted_iota(jnp.int32, (tm, LANES), 1)
        lg = jnp.where(lane < N_EXPERTS, logits, NEG)
        m1 = jnp.max(lg, axis=-1, keepdims=True)
        lanef = lane.astype(_F32)
        i1 = jnp.min(jnp.where(lg == m1, lanef, float(LANES)), axis=-1, keepdims=True)
        lg2 = jnp.where(lanef == i1, NEG, lg)
        m2 = jnp.max(lg2, axis=-1, keepdims=True)
        i2 = jnp.min(jnp.where(lg2 == m2, lanef, float(LANES)), axis=-1, keepdims=True)
        e = jnp.exp(m2 - m1)
        g1 = 1.0 / (1.0 + e)
        g2 = e / (1.0 + e)
        route_out[rows, :] = jnp.where(lane == 0, i1,
                                   jnp.where(lane == 1, i2,
                                             jnp.where(lane == 2, g1, jnp.where(lane == 3, g2, 0.0))))


def _oproj_route(x, mod, g, w, o, router, tm):
    B, S, D = x.shape
    assert D == SUBLANES * LANES
    per_b = S // tm
    tok = lambda wd: pl.BlockSpec((None, tm, wd), lambda b, i: (b, i, 0))
    return pl.pallas_call(
        functools.partial(_oproj_route_kernel, sub=min(tm, 128)),
        grid=(B, S // tm),
        in_specs=[tok(D), pl.BlockSpec((None, 6, D), lambda b, i: (b, 0, 0)),
                  pl.BlockSpec((1, D), lambda b, i: (0, 0)),
                  pl.BlockSpec(w.shape, lambda b, i: (0, 0)),
                  pl.BlockSpec((None, o.shape[1], tm, LANES), lambda b, i: (b, 0, i, 0)),
                  pl.BlockSpec(router.shape, lambda b, i: (0, 0, 0))],
        out_specs=[tok(D), pl.BlockSpec((tm * SUBLANES, LANES), lambda b, i: (b * per_b + i, 0)), tok(LANES)],
        out_shape=[jax.ShapeDtypeStruct((B, S, D), _F32),
                   jax.ShapeDtypeStruct((B * S * SUBLANES, LANES), _F32),
                   jax.ShapeDtypeStruct((B, S, LANES), _F32)],
        compiler_params=_cparams(("parallel", "parallel")),
        name="oproj_route",
    )(x, mod, g, w, o, router)


def _swiglu_hidden(x, wg_ref, wu_ref, a_scr, tf, after_chunk=None):
    F = wg_ref.shape[1]
    for c0 in range(0, F, tf):
        c1 = min(c0 + tf, F)
        g = jnp.dot(x, wg_ref[:, c0:c1], preferred_element_type=_F32)
        u = jnp.dot(x, wu_ref[:, c0:c1], preferred_element_type=_F32)
        a_scr[:, c0:c1] = (_silu(g) * u).astype(_BF)
        if after_chunk is not None:
            after_chunk(c0 // tf)


def _ffn_kernel(x_ref, oa_ref, ob_ref, wo_ref, mod_ref, g_ref, modn_ref, gn_ref, wg_ref, wu_ref, wd_ref,
                x_out, h_out, a_scr, *, tf):
    wa = oa_ref.shape[1]
    y = (jnp.dot(oa_ref[...], wo_ref[0:wa, :], preferred_element_type=_F32)
         + jnp.dot(ob_ref[...], wo_ref[wa:, :], preferred_element_type=_F32))
    x1 = x_ref[...] + mod_ref[2:3, :] * y
    h = _modulate(x1, g_ref[...], mod_ref[3:4, :], mod_ref[4:5, :]).astype(_BF)
    _swiglu_hidden(h, wg_ref, wu_ref, a_scr, tf)
    y = jnp.dot(a_scr[...], wd_ref[...], preferred_element_type=_F32)
    x2 = x1 + mod_ref[5:6, :] * y
    x_out[...] = x2
    h_out[...] = _modulate(x2, gn_ref[...], modn_ref[0:1, :], modn_ref[1:2, :]).astype(h_out.dtype)


def _ffn(x, o_a, o_b, w_out, mod, g, modn, gn, wg, wu, wd, tm, tf):
    B, S, D = x.shape
    T = B * S
    F = wg.shape[1]
    per_b = S // tm
    flat = lambda a: a.reshape(T, a.shape[2])
    tok = lambda w: pl.BlockSpec((tm, w), lambda i: (i, 0))
    modspec = pl.BlockSpec((None, 6, D), lambda i: (i // per_b, 0, 0))
    row = pl.BlockSpec((1, D), lambda i: (0, 0))
    once = pl.Buffered(1)
    res = lambda a: pl.BlockSpec(a.shape, lambda i: (0, 0), pipeline_mode=once)
    xo, ho = pl.pallas_call(
        functools.partial(_ffn_kernel, tf=tf),
        grid=(T // tm,),
        in_specs=[tok(D), tok(o_a.shape[2]), tok(o_b.shape[2]), res(w_out), modspec, row, modspec, row,
                  res(wg), res(wu), res(wd)],
        out_specs=[tok(D), tok(D)],
        out_shape=[jax.ShapeDtypeStruct((T, D), _F32), jax.ShapeDtypeStruct((T, D), _BF)],
        scratch_shapes=[pltpu.VMEM((tm, F), _BF)],
        compiler_params=_cparams(("parallel",)),
        name="ffn_dense",
    )(flat(x), flat(o_a), flat(o_b), w_out, mod, g, modn, gn, wg, wu, wd)
    return xo.reshape(B, S, D), ho.reshape(B, S, D)


def _qkv_kernel(h_ref, w_ref, gains_ref, rope_ref, q_ref, k_ref, v_ref, n_scr):
    tm, D = h_ref.shape
    h = h_ref[...]
    lane = lax.broadcasted_iota(jnp.int32, (tm, LANES), 1)
    head_a = (lane & 63) < 32

    def head_pair_norm(v, g):
        sq = v * v
        s_a = jnp.sum(jnp.where(head_a, sq, 0.0), axis=-1, keepdims=True)
        s_b = jnp.sum(jnp.where(head_a, 0.0, sq), axis=-1, keepdims=True)
        r = lax.rsqrt(jnp.where(head_a, s_a, s_b) * (1.0 / HEAD_DIM) + EPS)
        return v * r * g

    y = jnp.dot(h, w_ref[...], preferred_element_type=_F32)
    for p in range(D // LANES):
        sl = slice(p * LANES, (p + 1) * LANES)
        n_scr[:, sl] = head_pair_norm(y[:, sl], gains_ref[0:1, :])
        n_scr[:, D + p * LANES:D + (p + 1) * LANES] = head_pair_norm(
            y[:, D + p * LANES:D + (p + 1) * LANES], gains_ref[1:2, :])
        v_ref[p] = y[:, 2 * D + p * LANES:2 * D + (p + 1) * LANES].astype(_BF)

    @pl.when(pl.program_id(1) >= 0)
    def _():
        c, s = rope_ref[:, 0:128], rope_ref[:, 128:256]
        for p in range(D // LANES):
            vq = n_scr[:, p * LANES:(p + 1) * LANES]
            vk = n_scr[:, D + p * LANES:D + (p + 1) * LANES]
            q_ref[p] = (vq * c + pltpu.roll(vq, 64, 1) * s).astype(_BF)
            k_ref[p] = (vk * c + pltpu.roll(vk, 64, 1) * s).astype(_BF)


def _qkv(h, w, gains, rope, tm):
    B, S, D = h.shape
    tok = pl.BlockSpec((None, tm, D), lambda b, i: (b, i, 0))
    hp = pl.BlockSpec((None, D // LANES, tm, LANES), lambda b, i: (b, 0, i, 0))
    sds = jax.ShapeDtypeStruct((B, D // LANES, S, LANES), _BF)
    return pl.pallas_call(
        _qkv_kernel,
        grid=(B, S // tm),
        in_specs=[tok, pl.BlockSpec(w.shape, lambda b, i: (0, 0)),
                  pl.BlockSpec((2, LANES), lambda b, i: (0, 0)),
                  pl.BlockSpec((tm, 2 * LANES), lambda b, i: (i, 0))],
        out_specs=[hp, hp, hp],
        out_shape=[sds, sds, sds],
        scratch_shapes=[pltpu.VMEM((tm, 2 * D), _F32)],
        compiler_params=_cparams(("parallel", "parallel")),
        name="qkv_dil",
    )(h, w, gains, rope)


DIL_GROUP = 4


def _dil_kernel(fast_ref, q_ref, k_ref, v_ref, shift_ref, o_ref, tmp, qp, kp, vp, acc, mm, ll, accp, mmp,
                llp, bias_scr, *, tl):
    @pl.when(fast_ref[0] == 1)
    def _():
        _dil_body(q_ref, k_ref, v_ref, shift_ref, o_ref, tmp, qp, kp, vp, acc, mm, ll, accp, mmp, llp,
                  bias_scr, tl=tl, fixed_shift=True)

    @pl.when(fast_ref[0] != 1)
    def _():
        _dil_body(q_ref, k_ref, v_ref, shift_ref, o_ref, tmp, qp, kp, vp, acc, mm, ll, accp, mmp, llp,
                  bias_scr, tl=tl, fixed_shift=False)


def _dil_body(q_ref, k_ref, v_ref, shift_ref, o_ref, tmp, qp, kp, vp, acc, mm, ll, accp, mmp, llp,
              bias_scr, *, tl, fixed_shift):
    S = q_ref.shape[0]
    ng = S // DIL_GROUP
    pitch = DIL_GROUP + 1

    def spread(dst, src):
        for j in range(DIL_GROUP):
            dst[pl.ds(j, ng, stride=pitch), :] = src[pl.ds(j, ng, stride=DIL_GROUP), :]

    for src, dst in ((q_ref, qp), (k_ref, kp), (v_ref, vp)):
        tmp[...] = src[...].astype(_F32)
        spread(dst, tmp)
    lane = lax.broadcasted_iota(jnp.int32, (tl, LANES), 1)
    lo = lane < 64
    head_a = (lane & 63) < 32
    for bi, (window, d) in enumerate(DIL_PATTERNS):
        L = S // d
        half = window // (2 * d)
        t = min(tl, L)
        W = min(t + 2 * half, L)
        nt = L // t
        if d > 1:
            assert d % DIL_GROUP == 0
        sd = d * pitch // DIL_GROUP

        def rows(r, first, n, d=d, sd=sd):
            if d == 1:
                return pl.ds(first, n)
            return pl.ds(r + r // DIL_GROUP + sd * first, n, stride=sd)

        assert t % half == 0 and W <= t + 2 * half
        qa = lax.broadcasted_iota(jnp.int32, (t, W), 0)
        kb = lax.broadcasted_iota(jnp.int32, (t, W), 1)
        inside = shift_ref[0:1, 0:W] if fixed_shift else 0.0
        for case in range(3):
            ok = jnp.abs(kb - qa - case * half) <= half
            bias_scr[case, 0:t, 0:W] = jnp.where(ok, inside, NEG)

        def body(idx, carry, bi=bi, d=d, L=L, half=half, t=t, W=W, nt=nt, rows=rows):
            r = idx // nt
            l0 = (idx % nt) * t
            start = jnp.clip(l0 - half, 0, L - W)
            if d == 1:
                qt = q_ref[pl.ds(pl.multiple_of(l0, t), t), :]
                kw = k_ref[pl.ds(pl.multiple_of(start, 64), W), :]
                vw = v_ref[pl.ds(pl.multiple_of(start, 64), W), :]
            else:
                qt = qp[rows(r, l0, t), :].astype(_BF)
                kw = kp[rows(r, start, W), :].astype(_BF)
                vw = vp[rows(r, start, W), :].astype(_BF)
            lo_t = lo[:t]
            qa_t = head_a[:t]
            zq = jnp.zeros_like(qt)
            q2 = jnp.concatenate([jnp.where(qa_t, qt, zq), jnp.where(qa_t, zq, qt)], axis=0)
            s = lax.dot_general(q2, kw, (((1,), (1,)), ((), ())), preferred_element_type=_F32)
            bias = bias_scr[(l0 - start) // half, 0:t, 0:W]
            s = s + jnp.concatenate([bias, bias], axis=0)
            if fixed_shift:
                p = jnp.exp2(s)
            else:
                m = jnp.max(s, axis=-1, keepdims=True)
                p = jnp.exp2(s - m)
                m_new = jnp.where(lo_t, m[:t], m[t:])
            den = jnp.sum(p, axis=-1, keepdims=True)
            o2 = jnp.dot(p.astype(_BF), vw, preferred_element_type=_F32)
            o_new = jnp.where(lo_t, o2[:t], o2[t:])
            l_new = jnp.where(lo_t, den[:t], den[t:])
            tok = rows(r, l0, t)
            if d == 1:
                assert bi == 0
                acc[tok, :] = o_new
                ll[tok, :] = l_new
                if not fixed_shift:
                    mm[tok, :] = m_new
            elif fixed_shift:
                accp[tok, :] = accp[tok, :] + o_new
                llp[tok, :] = llp[tok, :] + l_new
            else:
                m_old = mmp[tok, :]
                mx = jnp.maximum(m_old, m_new)
                a_old = jnp.exp2(m_old - mx)
                a_new = jnp.exp2(m_new - mx)
                accp[tok, :] = accp[tok, :] * a_old + o_new * a_new
                llp[tok, :] = llp[tok, :] * a_old + l_new * a_new
                mmp[tok, :] = mx
            return carry

        lax.fori_loop(0, d * nt, body, 0, unroll=min(32 if fixed_shift else 8, d * nt))
        if bi == 0:
            spread(accp, acc)
            spread(llp, ll)
            if not fixed_shift:
                spread(mmp, mm)
    for j in range(DIL_GROUP):
        grp = pl.ds(j, ng, stride=pitch)
        tmp[pl.ds(j, ng, stride=DIL_GROUP), :] = accp[grp, :] / llp[grp, :]
    o_ref[...] = tmp[...].astype(o_ref.dtype)


def _dilated(fast, shift, q, k, v, tl):
    B, P, S, _ = q.shape
    wmax = tl + max(w // d for w, d in DIL_PATTERNS)
    spec = pl.BlockSpec((None, None, S, LANES), lambda b, p, f: (b, p, 0, 0))
    scr = pltpu.VMEM((S, LANES), _F32)
    scrp = pltpu.VMEM((S // DIL_GROUP * (DIL_GROUP + 1), LANES), _F32)
    grid_spec = pltpu.PrefetchScalarGridSpec(
        num_scalar_prefetch=1,
        grid=(B, P),
        in_specs=[spec, spec, spec, pl.BlockSpec((1, wmax), lambda b, p, f: (0, 0))],
        out_specs=spec,
        scratch_shapes=[scr, scrp, scrp, scrp, scr, scr, scr, scrp, scrp, scrp,
                        pltpu.VMEM((3, tl, wmax), _F32)],
    )
    return pl.pallas_call(
        functools.partial(_dil_kernel, tl=tl),
        grid_spec=grid_spec,
        out_shape=jax.ShapeDtypeStruct((B, P, S, LANES), _BF),
        compiler_params=_cparams(("parallel", "parallel")),
        name="dil_attn",
    )(fast, q, k, v, jnp.full((1, wmax), shift, _F32))


def _moe_kernel(te_ref, tv_ref, src_hbm, dst_hbm, h_hbm, wg_ref, wu_ref, wd_ref, out_hbm,
                xbuf, xbf, a_scr, ybuf, src_s0, src_s1, dst_s0, dst_s1, gsem, ssem, isem,
                *, tf, dump0, n_dump):
    i = pl.program_id(0)
    nt = pl.num_programs(0)
    tm = xbf.shape[0]
    slot = i % 2

    def is_valid(t):
        return jnp.logical_and(jnp.logical_and(t >= 0, t < nt), tv_ref[jnp.clip(t, 0, nt - 1)] == 1)

    def both(a, b):
        return jnp.logical_and(a, b)

    valid = is_valid(i)
    prev_valid = is_valid(i - 1)

    src_bufs = (src_s0, src_s1)
    dst_bufs = (dst_s0, dst_s1)

    def src_copy(tile, par):
        return pltpu.make_async_copy(src_hbm.at[tile], src_bufs[par], isem.at[par])

    def dst_copy(tile, par):
        return pltpu.make_async_copy(dst_hbm.at[tile], dst_bufs[par], isem.at[2 + par])

    def tile_at(first):
        return pl.ds(pl.multiple_of(first, SUBLANES), SUBLANES)

    def gather_row(r, par):
        return pltpu.make_async_copy(h_hbm.at[tile_at(src_bufs[par][r])],
                                     xbuf.at[par, tile_at(r * SUBLANES)], gsem.at[par])

    def scatter_row(r, par):
        return pltpu.make_async_copy(ybuf.at[par, tile_at(r * SUBLANES)],
                                     out_hbm.at[tile_at(dst_bufs[par][r])], ssem.at[par])

    def gather_wait(par):
        return pltpu.make_async_copy(h_hbm.at[pl.ds(0, tm * SUBLANES)], xbuf.at[par], gsem.at[par])

    def scatter_wait(par):
        return pltpu.make_async_copy(ybuf.at[par], out_hbm.at[pl.ds(0, tm * SUBLANES)], ssem.at[par])

    def for_rows(fn):
        def body(r, c):
            fn(r)
            return c
        lax.fori_loop(0, tm, body, 0, unroll=8)

    @pl.when(i == 0)
    def _():
        ybuf[...] = jnp.zeros_like(ybuf)
        for k in range(n_dump):
            pltpu.make_async_copy(ybuf.at[1], out_hbm.at[pl.ds((dump0 + k * tm) * SUBLANES, tm * SUBLANES)],
                                  ssem.at[1]).start()
        for k in range(n_dump):
            scatter_wait(1).wait()

    @pl.when(both(i == 0, valid))
    def _():
        src_copy(0, 0).start()
        src_copy(0, 0).wait()
        for_rows(lambda r: gather_row(r, 0).start())
        src_copy(1, 1).start()

    def phase1(par, with_scatter):
        xbf[...] = _tiles_to_rows(xbuf.at[par], tm).astype(_BF)
        n_chunk = -(-wg_ref.shape[1] // tf)
        per = -(-tm // n_chunk)

        def after(c):
            for r in range(c * per, min((c + 1) * per, tm)):
                gather_row(r, 1 - par).start()
                if with_scatter:
                    scatter_row(r, 1 - par).start()

        _swiglu_hidden(xbf[...], wg_ref, wu_ref, a_scr, tf, after)
        _rows_to_tiles(ybuf.at[par], jnp.dot(a_scr[...], wd_ref[...], preferred_element_type=_F32))

    for par in range(2):
        here = slot == par

        @pl.when(both(valid, here))
        def _(par=par):
            dst_copy(i, par).start()
            src_copy(i + 1, 1 - par).wait()

        @pl.when(both(prev_valid, here))
        def _(par=par):
            dst_copy(i - 1, 1 - par).wait()

        @pl.when(both(jnp.logical_or(both(i == 0, valid), prev_valid), here))
        def _(par=par):
            gather_wait(par).wait()

        @pl.when(both(is_valid(i - 2), here))
        def _(par=par):
            scatter_wait(par).wait()

        @pl.when(both(both(valid, prev_valid), here))
        def _(par=par):
            phase1(par, True)

        if par == 0:
            @pl.when(both(both(valid, jnp.logical_not(prev_valid)), here))
            def _():
                phase1(0, False)

        @pl.when(both(both(jnp.logical_not(valid), prev_valid), here))
        def _(par=par):
            for_rows(lambda r: scatter_row(r, 1 - par).start())

        @pl.when(both(is_valid(i + 1), here))
        def _(par=par):
            src_copy(i + 2, par).start()


def _moe(h, tile_e, tile_v, src, dst, wg, wu, wd, n_out_rows, tm, tf):
    T = h.shape[0] // SUBLANES
    D = wg.shape[1]
    nt = tile_e.shape[0]
    F = wg.shape[2]
    dump0 = T * TOP_K
    once = pl.Buffered(1)
    grid_spec = pltpu.PrefetchScalarGridSpec(
        num_scalar_prefetch=2,
        grid=(nt,),
        in_specs=[pl.BlockSpec(memory_space=pl.ANY), pl.BlockSpec(memory_space=pl.ANY),
                  pl.BlockSpec(memory_space=pl.ANY),
                  pl.BlockSpec((None, D, F), lambda i, te, tv: (te[i], 0, 0), pipeline_mode=once),
                  pl.BlockSpec((None, D, F), lambda i, te, tv: (te[i], 0, 0), pipeline_mode=once),
                  pl.BlockSpec((None, F, D), lambda i, te, tv: (te[i], 0, 0), pipeline_mode=once)],
        out_specs=pl.BlockSpec(memory_space=pl.ANY),
        scratch_shapes=[pltpu.VMEM((2, tm * SUBLANES, LANES), _F32), pltpu.VMEM((tm, D), _BF),
                        pltpu.VMEM((tm, F), _BF), pltpu.VMEM((2, tm * SUBLANES, LANES), _F32),
                        pltpu.SMEM((tm,), jnp.int32), pltpu.SMEM((tm,), jnp.int32),
                        pltpu.SMEM((tm,), jnp.int32), pltpu.SMEM((tm,), jnp.int32),
                        pltpu.SemaphoreType.DMA((2,)), pltpu.SemaphoreType.DMA((2,)),
                        pltpu.SemaphoreType.DMA((4,))],
    )
    return pl.pallas_call(
        functools.partial(_moe_kernel, tf=tf, dump0=dump0, n_dump=(n_out_rows - dump0) // tm),
        grid_spec=grid_spec,
        out_shape=jax.ShapeDtypeStruct((n_out_rows * SUBLANES, LANES), _F32),
        compiler_params=_cparams(("arbitrary",)),
        name="moe_experts",
    )(tile_e, tile_v, src, dst, h, wg, wu, wd)


def _route_plan(route, T, tm):
    A = T * TOP_K
    e_flat = route[:, :TOP_K].astype(jnp.int32).reshape(A)
    order = jnp.argsort(e_flat, stable=True).astype(jnp.int32)
    counts = jnp.sum((e_flat[:, None] == jnp.arange(N_EXPERTS, dtype=jnp.int32)[None, :]).astype(jnp.int32), axis=0)
    starts = jnp.cumsum(counts) - counts
    pcounts = (counts + tm - 1) // tm * tm
    pends = jnp.cumsum(pcounts)
    pstarts = pends - pcounts
    nt = A // tm + N_EXPERTS + 2
    tile0 = jnp.arange(nt, dtype=jnp.int32) * tm
    tile_v = (tile0 < pends[-1]).astype(jnp.int32)
    last_valid = jnp.maximum(pends[-1] // tm - 1, 0)
    tile_e_raw = jnp.minimum(jnp.searchsorted(pends, tile0, side="right"), N_EXPERTS - 1).astype(jnp.int32)
    tile_e = jnp.where(tile_v == 1, tile_e_raw, tile_e_raw[last_valid])
    r = jnp.arange(nt * tm, dtype=jnp.int32)
    e_r = jnp.repeat(tile_e, tm)
    within = r - pstarts[e_r]
    ok = jnp.logical_and(within < counts[e_r], jnp.repeat(tile_v, tm) == 1)
    a = order[jnp.clip(starts[e_r] + within, 0, A - 1)]
    src = jnp.where(ok, a // TOP_K, 0).astype(jnp.int32)
    dump = A + e_r * tm + jnp.clip(within - counts[e_r], 0, tm - 1)
    dst = jnp.where(ok, (a % TOP_K) * T + a // TOP_K, dump).astype(jnp.int32)
    return (tile_e, tile_v, (src * SUBLANES).reshape(nt, tm), (dst * SUBLANES).reshape(nt, tm),
            A + N_EXPERTS * tm)


def _combine_kernel(x_ref, y1_ref, y2_ref, r_ref, mod_ref, o_ref):
    tm = x_ref.shape[0]
    y = _tiles_to_rows(y1_ref, tm) * r_ref[:, 2:3] + _tiles_to_rows(y2_ref, tm) * r_ref[:, 3:4]
    o_ref[...] = x_ref[...] + mod_ref[5:6, :] * y


def _combine(x, y2, route, mod, tm):
    B, S, D = x.shape
    T = B * S
    per_b = S // tm
    out = pl.pallas_call(
        _combine_kernel,
        grid=(T // tm,),
        in_specs=[pl.BlockSpec((tm, D), lambda i: (i, 0)),
                  pl.BlockSpec((tm * SUBLANES, LANES), lambda i: (i, 0)),
                  pl.BlockSpec((tm * SUBLANES, LANES), lambda i: (T // tm + i, 0)),
                  pl.BlockSpec((tm, LANES), lambda i: (i, 0)),
                  pl.BlockSpec((None, 6, D), lambda i: (i // per_b, 0, 0))],
        out_specs=pl.BlockSpec((tm, D), lambda i: (i, 0)),
        out_shape=jax.ShapeDtypeStruct((T, D), _F32),
        compiler_params=_cparams(("parallel",)),
        name="moe_combine",
    )(x.reshape(T, D), y2, y2, route.reshape(T, LANES), mod)
    return out.reshape(B, S, D)


def _pad_cols(a, w):
    return jnp.pad(a, ((0, 0), (0, w - a.shape[1])))


def _rope_tables_even(S):
    pos = jnp.arange(S, dtype=jnp.int32)
    inv = ROPE_THETA ** (-jnp.arange(0, 32, 2, dtype=_F32) / 32)
    def cs(p):
        ang = p.astype(_F32)[:, None] * inv[None, :]
        return jnp.cos(ang), jnp.sin(ang)
    one = lambda w: jnp.ones((S, w), _F32)
    zero = lambda w: jnp.zeros((S, w), _F32)
    c, s = cs(pos)
    ca = jnp.concatenate([c, one(48), c, one(48)], 1)
    sa = jnp.concatenate([-s, zero(48), s, zero(48)], 1)
    cr, sr = cs(pos // GRID_W)
    cc, sc = cs(pos % GRID_W)
    cb = jnp.concatenate([cr, cc, one(32), cr, cc, one(32)], 1)
    sb = jnp.concatenate([-sr, -sc, zero(32), sr, sc, zero(32)], 1)
    return jnp.concatenate([ca, sa, cb, sb], 1)


def _slot_maps():
    r = MLA_ROPE // 2
    mla = ([MLA_NOPE + i for i in range(r)] + list(range(0, 64 - r))
           + [MLA_NOPE + r + i for i in range(r)] + list(range(64 - r, MLA_NOPE)))
    mla += [-1] * (LANES - len(mla))
    q = HEAD_DIM // 4
    gqa = (list(range(0, q)) + list(range(2 * q, 3 * q)) + [-1] * (64 - 2 * q)
           + list(range(q, 2 * q)) + list(range(3 * q, 4 * q)) + [-1] * (64 - 2 * q))
    return mla, gqa


def _to_slots(a, lane_map):
    idx = jnp.asarray([max(i, 0) for i in lane_map], jnp.int32)
    keep = jnp.asarray([1.0 if i >= 0 else 0.0 for i in lane_map], a.dtype)
    return jnp.take(a, idx, axis=-1) * keep


def _rope_tables_odd(S):
    pos = jnp.arange(S, dtype=_F32)
    inv = ROPE_THETA ** (-jnp.arange(0, HEAD_DIM, 2, dtype=_F32) / HEAD_DIM)
    ang = pos[:, None] * inv[None, :]
    c, s = jnp.cos(ang), jnp.sin(ang)
    return jnp.concatenate([c, c, c, c, -s, -s, s, s], 1)


def _tiles(S):
    return dict(tm_pre=min(S, 512), tq=min(S, 2048), tm_o=min(S, 512),
                tm_ffn=min(S, 512), tf_ffn=512, tm_qkv=min(S, 512),
                tl=128, tm_moe=min(S, 512), tf_moe=512, tm_c=min(S, 512))


def kernel(x, c, ada_even_w, ada_even_b, norm_even_mix, norm_even_ffn, even_w_in, mla_q_norm, mla_w_uq, mla_kv_norm, mla_w_ukv, mla_q_gain, mla_k_gain, gqa_q_gain, gqa_k_gain, even_w_out, ffn_w_gate, ffn_w_up, ffn_w_down, ada_odd_w, ada_odd_b, norm_odd_mix, norm_odd_ffn, dil_w_qkv, dil_q_gain, dil_k_gain, dil_w_out, moe_router, moe_w_gate, moe_w_up, moe_w_down):
    B, S, D = x.shape
    T = B * S
    cfg = _tiles(S)

    mod_e = _ada_mod(c, ada_even_w[0], ada_even_b[0]).reshape(B, 6, D)
    mod_o = _ada_mod(c, ada_odd_w[0], ada_odd_b[0]).reshape(B, 6, D)

    w = even_w_in[0]
    sp = [MLA_Q_RANK, MLA_Q_RANK + MLA_KV_RANK, MLA_Q_RANK + MLA_KV_RANK + MLA_ROPE]
    sp.append(sp[-1] + GQA_HEADS * HEAD_DIM)
    sp.append(sp[-1] + GQA_KV_HEADS * HEAD_DIM)
    w_cq, w_ckv, w_kpe = w[:, :sp[0]], w[:, sp[0]:sp[1]], w[:, sp[1]:sp[2]]
    w_qb, w_kb, w_vb = w[:, sp[2]:sp[3]], w[:, sp[3]:sp[4]], w[:, sp[4]:]
    mla_map, gqa_map = _slot_maps()
    na = MLA_NOPE + MLA_ROPE
    nope_only = [i if 0 <= i < MLA_NOPE else -1 for i in mla_map]
    rope_only = [i - MLA_NOPE if i >= MLA_NOPE else -1 for i in mla_map]
    gslots = lambda a, n: _to_slots(a.reshape(D, n, HEAD_DIM), gqa_map).reshape(D, n * LANES)
    w_in = jnp.concatenate([w_cq, w_ckv, _to_slots(w_kpe, rope_only), gslots(w_qb, GQA_HEADS),
                            gslots(w_kb, GQA_KV_HEADS)], axis=1).astype(_BF)
    w_uq = _to_slots(mla_w_uq[0].reshape(MLA_Q_RANK, MLA_HEADS, na), mla_map)
    w_uq = w_uq.reshape(MLA_Q_RANK, MLA_HEADS * LANES).astype(_BF)
    w_ukv = mla_w_ukv[0].reshape(MLA_KV_RANK, MLA_HEADS, MLA_NOPE + MLA_V)
    w_uk = _to_slots(w_ukv[:, :, :MLA_NOPE], nope_only).reshape(MLA_KV_RANK, MLA_HEADS * LANES).astype(_BF)
    w_uvt = w_ukv[:, :, MLA_NOPE:].reshape(MLA_KV_RANK, MLA_HEADS * MLA_V).T.astype(_BF)
    w_vbt = w_vb.T.astype(_BF)
    g_rows = [_to_slots(mla_q_gain[0], mla_map) * (na ** -0.5 * LOG2E), _to_slots(mla_k_gain[0], mla_map),
              _to_slots(gqa_q_gain[0], gqa_map) * (HEAD_DIM ** -0.5 * LOG2E), _to_slots(gqa_k_gain[0], gqa_map)]
    bound_a = na * jnp.max(jnp.abs(g_rows[0])) * jnp.max(jnp.abs(g_rows[1]))
    bound_b = HEAD_DIM * jnp.max(jnp.abs(g_rows[2])) * jnp.max(jnp.abs(g_rows[3]))
    fast_a, fast_b = bound_a <= SCORE_BOUND_MAX, bound_b <= SCORE_BOUND_MAX
    last_lane = jnp.zeros((LANES,), _F32).at[LANES - 1].set(1.0)
    g_rows += [last_lane, last_lane * jnp.where(fast_a, -bound_a, 0.0),
               last_lane, last_lane * jnp.where(fast_b, -bound_b, 0.0)]
    gains_e = jnp.stack(g_rows)
    rope_e = _rope_tables_even(S)
    q_all, k_all, vt_all = _pre_even(
        x, mod_e, norm_even_mix[0].reshape(1, D), w_in, mla_q_norm[0].reshape(1, -1), w_uq,
        mla_kv_norm[0].reshape(1, -1), w_uk, w_uvt, w_vbt, gains_e, rope_e, cfg["tm_pre"])
    as_flag = lambda f: f.astype(jnp.int32).reshape(1)
    o_a = _attention(as_flag(fast_a), q_all, k_all, vt_all, q_off=0, k_off=0, shared_kv=False, tq=cfg["tq"])
    o_b = _attention(as_flag(fast_b), q_all, k_all, vt_all, q_off=MLA_HEADS, k_off=MLA_HEADS,
                     shared_kv=True, tq=cfg["tq"])
    x2, h2 = _ffn(x, o_a, o_b, even_w_out[0].astype(_BF), mod_e, norm_even_ffn[0].reshape(1, D), mod_o,
                  norm_odd_mix[0].reshape(1, D), ffn_w_gate[0].astype(_BF), ffn_w_up[0].astype(_BF),
                  ffn_w_down[0].astype(_BF), cfg["tm_ffn"], cfg["tf_ffn"])

    hh = HEAD_DIM // 2
    pair = lambda v: jnp.concatenate([v[:hh], v[:hh], v[hh:], v[hh:]])
    gains_o = jnp.stack([pair(dil_q_gain[0]) * (HEAD_DIM ** -0.5 * LOG2E), pair(dil_k_gain[0])])
    wq, wk, wv = jnp.split(dil_w_qkv[0], 3, axis=1)
    perm = lambda w: w.reshape(D, D // LANES, 2, 2, hh).transpose(0, 1, 3, 2, 4).reshape(D, D)
    w_qkv = jnp.concatenate([perm(wq), perm(wk), wv], axis=1).astype(_BF)
    qd, kd, vd = _qkv(h2, w_qkv, gains_o, _rope_tables_odd(S), cfg["tm_qkv"])
    bound_d = HEAD_DIM * jnp.max(jnp.abs(gains_o[0])) * jnp.max(jnp.abs(gains_o[1]))
    fast_d = bound_d <= SCORE_BOUND_MAX
    o_d = _dilated(as_flag(fast_d), jnp.where(fast_d, -bound_d, 0.0), qd, kd, vd, cfg["tl"])
    r32 = _pad_cols(moe_router[0], LANES)
    r_hi = r32.astype(_BF)
    r_lo = (r32 - r_hi.astype(_F32)).astype(_BF)
    x3, h3, route = _oproj_route(x2, mod_o, norm_odd_ffn[0].reshape(1, D), dil_w_out[0].astype(_BF), o_d,
                                 jnp.stack([r_hi, r_lo]), cfg["tm_o"])
    tm = cfg["tm_moe"]
    tile_e, tile_v, src, dst, n_rows = _route_plan(route.reshape(T, LANES), T, tm)
    y2 = _moe(h3, tile_e, tile_v, src, dst, moe_w_gate[0].astype(_BF),
              moe_w_up[0].astype(_BF), moe_w_down[0].astype(_BF), n_rows, tm, cfg["tf_moe"])
    return _combine(x3, y2, route, mod_o, cfg["tm_c"])
```

```python
import functools
import math

import jax
import jax.numpy as jnp
from jax import lax
from jax.experimental import pallas as pl
from jax.experimental.pallas import tpu as pltpu

_BF = jnp.bfloat16
_F32 = jnp.float32

GRID_W = 64
HEAD_DIM = 64
ROPE_THETA = 10000.0
EPS = 1e-6
MLA_HEADS = 8
MLA_Q_RANK = 256
MLA_KV_RANK = 128
MLA_NOPE = 64
MLA_ROPE = 32
MLA_V = 64
GQA_HEADS = 8
GQA_KV_HEADS = 2
DIL_PATTERNS = ((128, 1), (512, 4), (2048, 16))
N_EXPERTS = 8
TOP_K = 2
NEG = -1e30
LOG2E = math.log2(math.e)

LANES = 128
VMEM_LIMIT = 56 * 1024 * 1024


def _cparams(sem, vmem=VMEM_LIMIT):
    return pltpu.CompilerParams(dimension_semantics=sem, vmem_limit_bytes=vmem)


def _silu(x):
    return x / (1.0 + jnp.exp(-x))


def _modulate(x, g, shift, scale):
    ms = jnp.mean(x * x, axis=-1, keepdims=True)
    return x * lax.rsqrt(ms + EPS) * g * (1.0 + scale) + shift


def _norm(v, n, g):
    return v * lax.rsqrt(jnp.sum(v * v, axis=-1, keepdims=True) * (1.0 / n) + EPS) * g


SUBLANES = 8


def _rows_to_tiles(ref, x, row0=0):
    tm = x.shape[0]
    for c in range(SUBLANES):
        ref[pl.ds(row0 * SUBLANES + c, tm, stride=SUBLANES), :] = x[:, c * LANES:(c + 1) * LANES]


def _tiles_to_rows(ref, tm):
    return jnp.concatenate([ref[pl.ds(c, tm, stride=SUBLANES), :] for c in range(SUBLANES)], axis=1)


def _mod_kernel(c_ref, w_ref, b_ref, o_ref):
    sc = _silu(c_ref[...])
    o_ref[...] = jnp.dot(sc.astype(_BF), w_ref[...].astype(_BF),
                         preferred_element_type=_F32) + b_ref[...]


def _ada_mod(c, w, b):
    B, D = c.shape
    N = w.shape[1]
    tn = min(N, 1536)
    return pl.pallas_call(
        _mod_kernel,
        grid=(N // tn,),
        in_specs=[pl.BlockSpec((B, D), lambda j: (0, 0)),
                  pl.BlockSpec((D, tn), lambda j: (0, j)),
                  pl.BlockSpec((1, tn), lambda j: (0, j))],
        out_specs=pl.BlockSpec((B, tn), lambda j: (0, j)),
        out_shape=jax.ShapeDtypeStruct((B, N), _F32),
        compiler_params=_cparams(("arbitrary",)),
        name="ada_mod",
    )(c, w, b.reshape(1, N))


def _pre_even_kernel(x_ref, mod_ref, g_ref, w_in_ref, qn_ref, w_uq_ref, kvn_ref, w_uk_ref, w_uvt_ref,
                     w_vbt_ref, gains_ref, rope_ref, q_ref, k_ref, vt_ref, nq_scr, nk_scr):
    h = _modulate(x_ref[...], g_ref[...], mod_ref[0:1, :], mod_ref[1:2, :]).astype(_BF)
    y = jnp.dot(h, w_in_ref[...], preferred_element_type=_F32)
    cqn = _norm(y[:, 0:256], MLA_Q_RANK, qn_ref[...]).astype(_BF)
    qa = jnp.dot(cqn, w_uq_ref[...], preferred_element_type=_F32)
    ckvn = _norm(y[:, 256:384], MLA_KV_RANK, kvn_ref[...]).astype(_BF)
    kn = jnp.dot(ckvn, w_uk_ref[...], preferred_element_type=_F32)
    kpe = y[:, 384:512]
    nt_dims = (((1,), (1,)), ((), ()))
    vt_a = lax.dot_general(w_uvt_ref[...], ckvn, nt_dims, preferred_element_type=_F32).astype(_BF)
    vt_b = lax.dot_general(w_vbt_ref[...], h, nt_dims, preferred_element_type=_F32).astype(_BF)
    tm = x_ref.shape[0]
    ones_blk = jnp.where(lax.broadcasted_iota(jnp.int32, (LANES - MLA_V, tm), 0) == 0, 1.0, 0.0).astype(_BF)
    for hh in range(MLA_HEADS + GQA_KV_HEADS):
        src = vt_a[hh * MLA_V:(hh + 1) * MLA_V] if hh < MLA_HEADS else \
            vt_b[(hh - MLA_HEADS) * HEAD_DIM:(hh - MLA_HEADS + 1) * HEAD_DIM]
        vt_ref[hh * LANES:hh * LANES + MLA_V, :] = src
        vt_ref[hh * LANES + MLA_V:(hh + 1) * LANES, :] = ones_blk
    ca, sa = rope_ref[:, 0:128], rope_ref[:, 128:256]
    cb, sb = rope_ref[:, 256:384], rope_ref[:, 384:512]
    gqa, gka = gains_ref[0:1, :], gains_ref[1:2, :]
    gqb, gkb = gains_ref[2:3, :], gains_ref[3:4, :]
    na = MLA_NOPE + MLA_ROPE

    for hh in range(MLA_HEADS):
        sl = slice(hh * LANES, (hh + 1) * LANES)
        nq_scr[:, sl] = _norm(qa[:, sl], na, gqa)
        nk_scr[:, sl] = _norm(kn[:, sl] + kpe, na, gka)
    for hh in range(GQA_HEADS):
        src = slice(512 + hh * LANES, 512 + (hh + 1) * LANES)
        dst = slice((MLA_HEADS + hh) * LANES, (MLA_HEADS + hh + 1) * LANES)
        nq_scr[:, dst] = _norm(y[:, src], HEAD_DIM, gqb)
    for g in range(GQA_KV_HEADS):
        src = slice(1536 + g * LANES, 1536 + (g + 1) * LANES)
        dst = slice((MLA_HEADS + g) * LANES, (MLA_HEADS + g + 1) * LANES)
        nk_scr[:, dst] = _norm(y[:, src], HEAD_DIM, gkb)

    @pl.when(pl.program_id(1) >= 0)
    def _():
        def rope(ref, out, n_a, n_all, row):
            for hh in range(n_all):
                sl = slice(hh * LANES, (hh + 1) * LANES)
                c, s = (ca, sa) if hh < n_a else (cb, sb)
                b = gains_ref[row:row + 1, :] if hh < n_a else gains_ref[row + 2:row + 3, :]
                v = ref[:, sl]
                out[:, sl] = (v * c + pltpu.roll(v, 64, 1) * s + b).astype(_BF)

        rope(nq_scr, q_ref, MLA_HEADS, MLA_HEADS + GQA_HEADS, 4)
        rope(nk_scr, k_ref, MLA_HEADS, MLA_HEADS + GQA_KV_HEADS, 5)


def _pre_even(x, mod, g, w_in, qn, w_uq, kvn, w_uk, w_uvt, w_vbt, gains, rope, tm):
    B, S, D = x.shape
    nq = (MLA_HEADS + GQA_HEADS) * LANES
    nk = (MLA_HEADS + GQA_KV_HEADS) * LANES
    nvt = (MLA_HEADS + GQA_KV_HEADS) * LANES
    full = lambda a: pl.BlockSpec(a.shape, lambda b, i: (0,) * a.ndim)
    tok = lambda w: pl.BlockSpec((None, tm, w), lambda b, i: (b, i, 0))
    return pl.pallas_call(
        _pre_even_kernel,
        grid=(B, S // tm),
        in_specs=[tok(D),
                  pl.BlockSpec((None, 6, D), lambda b, i: (b, 0, 0)),
                  full(g), full(w_in), full(qn), full(w_uq), full(kvn), full(w_uk), full(w_uvt),
                  full(w_vbt), full(gains),
                  pl.BlockSpec((tm, rope.shape[1]), lambda b, i: (i, 0))],
        out_specs=[tok(nq), tok(nk), pl.BlockSpec((None, None, nvt, tm), lambda b, i: (b, i, 0, 0))],
        out_shape=[jax.ShapeDtypeStruct((B, S, nq), _BF), jax.ShapeDtypeStruct((B, S, nk), _BF),
                   jax.ShapeDtypeStruct((B, S // tm, nvt, tm), _BF)],
        scratch_shapes=[pltpu.VMEM((tm, nq), _F32), pltpu.VMEM((tm, nk), _F32)],
        compiler_params=_cparams(("parallel", "parallel")),
        name="pre_even",
    )(x, mod, g, w_in, qn, w_uq, kvn, w_uk, w_uvt, w_vbt, gains, rope)


SCORE_BOUND_MAX = 60.0


def _attn_kernel(fast_ref, q_ref, k_ref, vt_ref, o_ref, s_scr, *, shared_kv):
    @pl.when(fast_ref[0] == 1)
    def _():
        _attn_fixed_shift(q_ref, k_ref, vt_ref, o_ref, shared_kv=shared_kv)

    @pl.when(fast_ref[0] != 1)
    def _():
        _attn_running_max(q_ref, k_ref, vt_ref, o_ref, s_scr, shared_kv=shared_kv)


def _attn_fixed_shift(q_ref, k_ref, vt_ref, o_ref, *, shared_kv):
    tq = q_ref.shape[0]
    nk, _, tk = vt_ref.shape
    qs = [q_ref[:, hh * LANES:(hh + 1) * LANES] for hh in range(2)]

    def body(j, carry):
        r0 = pl.multiple_of(j * tk, tk)
        out = []
        for hh in range(2):
            kc = 0 if shared_kv else hh * LANES
            k = k_ref[pl.ds(r0, tk), kc:kc + LANES]
            st = lax.dot_general(k, qs[hh], (((1,), (1,)), ((), ())), preferred_element_type=_F32)
            p = jnp.exp2(st).astype(_BF)
            out.append(carry[hh] + jnp.dot(vt_ref[j, kc:kc + LANES, :], p, preferred_element_type=_F32))
        return tuple(out)

    z = jnp.zeros((LANES, tq), _F32)
    res = lax.fori_loop(0, nk, body, (z, z), unroll=True)
    nv = HEAD_DIM
    halves = [acc[0:nv, :] / acc[nv:nv + 1, :] for acc in res]
    o_ref[...] = jnp.concatenate(halves, axis=0).T.astype(o_ref.dtype)


def _attn_running_max(q_ref, k_ref, vt_ref, o_ref, s_scr, *, shared_kv):
    tq = q_ref.shape[0]
    nk, _, tk = vt_ref.shape
    qs = [q_ref[:, hh * LANES:(hh + 1) * LANES] for hh in range(2)]

    def scores(j, hh):
        r0 = j * tk if isinstance(j, int) else pl.multiple_of(j * tk, tk)
        kc = 0 if shared_kv else hh * LANES
        k = k_ref[pl.ds(r0, tk), kc:kc + LANES]
        return lax.dot_general(k, qs[hh], (((1,), (1,)), ((), ())), preferred_element_type=_F32)

    def put_scores(j, hh, slot):
        st = scores(j, hh)
        s_scr[hh, slot] = st
        return jnp.max(st, axis=0, keepdims=True)

    def step(j, hh, carry, cur, last=False):
        m, mc, acc = carry
        mc_next = mc if last else put_scores(j + 1, hh, 1 - cur)
        st = s_scr[hh, cur]
        m_new = jnp.maximum(m, mc)
        alpha = jnp.exp2(m - m_new)
        p = jnp.exp2(st - m_new).astype(_BF)
        vc = 0 if shared_kv else hh * LANES
        acc = alpha * acc + jnp.dot(vt_ref[j, vc:vc + LANES, :], p, preferred_element_type=_F32)
        return m_new, mc_next, acc

    def body(jj, carry):
        c = list(carry)
        for sub in range(2):
            for hh in range(2):
                c[hh] = step(2 * jj + sub, hh, c[hh], sub)
        return tuple(c)

    init = tuple((jnp.full((1, tq), NEG, _F32), put_scores(0, hh, 0), jnp.zeros((LANES, tq), _F32))
                 for hh in range(2))
    res = lax.fori_loop(0, nk // 2 - 1, body, init)
    res = list(res)
    for sub in range(2):
        for hh in range(2):
            res[hh] = step(nk - 2 + sub, hh, res[hh], sub, last=(sub == 1))
    nv = HEAD_DIM
    halves = [acc[0:nv, :] / acc[nv:nv + 1, :] for (_, _, acc) in res]
    o_ref[...] = jnp.concatenate(halves, axis=0).T.astype(o_ref.dtype)


def _attention(fast, q, k, vt, *, q_off, k_off, shared_kv, tq):
    B, S, _ = q.shape
    nk, _, tk = vt.shape[1:]
    assert nk % 2 == 0 and nk >= 2
    npairs = 4
    if shared_kv:
        kspec = pl.BlockSpec((None, S, LANES), lambda b, p, i, f: (b, 0, k_off + p // 2))
        vspec = pl.BlockSpec((None, nk, LANES, tk), lambda b, p, i, f: (b, 0, k_off + p // 2, 0))
    else:
        kspec = pl.BlockSpec((None, S, 2 * LANES), lambda b, p, i, f: (b, 0, k_off // 2 + p))
        vspec = pl.BlockSpec((None, nk, 2 * LANES, tk), lambda b, p, i, f: (b, 0, k_off // 2 + p, 0))
    grid_spec = pltpu.PrefetchScalarGridSpec(
        num_scalar_prefetch=1,
        grid=(B, npairs, S // tq),
        in_specs=[pl.BlockSpec((None, tq, 2 * LANES), lambda b, p, i, f: (b, i, q_off // 2 + p)),
                  kspec, vspec],
        out_specs=pl.BlockSpec((None, tq, LANES), lambda b, p, i, f: (b, i, p)),
        scratch_shapes=[pltpu.VMEM((2, 2, tk, tq), _F32)],
    )
    return pl.pallas_call(
        functools.partial(_attn_kernel, shared_kv=shared_kv),
        grid_spec=grid_spec,
        out_shape=jax.ShapeDtypeStruct((B, S, npairs * LANES), _BF),
        compiler_params=_cparams(("parallel", "parallel", "parallel")),
        name="attn_gqa" if shared_kv else "attn_mla",
    )(fast, q, k, vt)


def _oproj_route_kernel(x_ref, mod_ref, g_ref, w_ref, o_ref, r_ref, x_out, h_out, route_out, *, sub):
    for r0 in range(0, x_ref.shape[0], sub):
        rows = slice(r0, r0 + sub)
        o = jnp.concatenate([o_ref[p, rows, :] for p in range(o_ref.shape[0])], axis=1)
        y = jnp.dot(o, w_ref[...], preferred_element_type=_F32)
        x1 = x_ref[rows, :] + mod_ref[2:3, :] * y
        x_out[rows, :] = x1
        h = _modulate(x1, g_ref[...], mod_ref[3:4, :], mod_ref[4:5, :])
        _rows_to_tiles(h_out, h, r0)
        hh = h.astype(_BF)
        hl = (h - hh.astype(_F32)).astype(_BF)
        rh, rl = r_ref[0], r_ref[1]
        logits = (jnp.dot(hh, rh, preferred_element_type=_F32)
                  + jnp.dot(hh, rl, preferred_element_type=_F32)
                  + jnp.dot(hl, rh, preferred_element_type=_F32))
        tm = logits.shape[0]
        lane = lax.broadcasted_iota(jnp.int32, (tm, LANES), 1)
        lg = jnp.where(lane < N_EXPERTS, logits, NEG)
        m1 = jnp.max(lg, axis=-1, keepdims=True)
        lanef = lane.astype(_F32)
        i1 = jnp.min(jnp.where(lg == m1, lanef, float(LANES)), axis=-1, keepdims=True)
        lg2 = jnp.where(lanef == i1, NEG, lg)
        m2 = jnp.max(lg2, axis=-1, keepdims=True)
        i2 = jnp.min(jnp.where(lg2 == m2, lanef, float(LANES)), axis=-1, keepdims=True)
        e = jnp.exp(m2 - m1)
        g1 = 1.0 / (1.0 + e)
        g2 = e / (1.0 + e)
        route_out[rows, :] = jnp.where(lane == 0, i1,
                                   jnp.where(lane == 1, i2,
                                             jnp.where(lane == 2, g1, jnp.where(lane == 3, g2, 0.0))))


def _oproj_route(x, mod, g, w, o, router, tm):
    B, S, D = x.shape
    assert D == SUBLANES * LANES
    per_b = S // tm
    tok = lambda wd: pl.BlockSpec((None, tm, wd), lambda b, i: (b, i, 0))
    return pl.pallas_call(
        functools.partial(_oproj_route_kernel, sub=min(tm, 128)),
        grid=(B, S // tm),
        in_specs=[tok(D), pl.BlockSpec((None, 6, D), lambda b, i: (b, 0, 0)),
                  pl.BlockSpec((1, D), lambda b, i: (0, 0)),
                  pl.BlockSpec(w.shape, lambda b, i: (0, 0)),
                  pl.BlockSpec((None, o.shape[1], tm, LANES), lambda b, i: (b, 0, i, 0)),
                  pl.BlockSpec(router.shape, lambda b, i: (0, 0, 0))],
        out_specs=[tok(D), pl.BlockSpec((tm * SUBLANES, LANES), lambda b, i: (b * per_b + i, 0)), tok(LANES)],
        out_shape=[jax.ShapeDtypeStruct((B, S, D), _F32),
                   jax.ShapeDtypeStruct((B * S * SUBLANES, LANES), _F32),
                   jax.ShapeDtypeStruct((B, S, LANES), _F32)],
        compiler_params=_cparams(("parallel", "parallel")),
        name="oproj_route",
    )(x, mod, g, w, o, router)


def _swiglu_hidden(x, wg_ref, wu_ref, a_scr, tf, after_chunk=None):
    F = wg_ref.shape[1]
    for c0 in range(0, F, tf):
        c1 = min(c0 + tf, F)
        g = jnp.dot(x, wg_ref[:, c0:c1], preferred_element_type=_F32)
        u = jnp.dot(x, wu_ref[:, c0:c1], preferred_element_type=_F32)
        a_scr[:, c0:c1] = (_silu(g) * u).astype(_BF)
        if after_chunk is not None:
            after_chunk(c0 // tf)


def _ffn_kernel(x_ref, oa_ref, ob_ref, wo_ref, mod_ref, g_ref, modn_ref, gn_ref, wg_ref, wu_ref, wd_ref,
                x_out, h_out, a_scr, *, tf):
    wa = oa_ref.shape[1]
    y = (jnp.dot(oa_ref[...], wo_ref[0:wa, :], preferred_element_type=_F32)
         + jnp.dot(ob_ref[...], wo_ref[wa:, :], preferred_element_type=_F32))
    x1 = x_ref[...] + mod_ref[2:3, :] * y
    h = _modulate(x1, g_ref[...], mod_ref[3:4, :], mod_ref[4:5, :]).astype(_BF)
    _swiglu_hidden(h, wg_ref, wu_ref, a_scr, tf)
    y = jnp.dot(a_scr[...], wd_ref[...], preferred_element_type=_F32)
    x2 = x1 + mod_ref[5:6, :] * y
    x_out[...] = x2
    h_out[...] = _modulate(x2, gn_ref[...], modn_ref[0:1, :], modn_ref[1:2, :]).astype(h_out.dtype)


def _ffn(x, o_a, o_b, w_out, mod, g, modn, gn, wg, wu, wd, tm, tf):
    B, S, D = x.shape
    T = B * S
    F = wg.shape[1]
    per_b = S // tm
    flat = lambda a: a.reshape(T, a.shape[2])
    tok = lambda w: pl.BlockSpec((tm, w), lambda i: (i, 0))
    modspec = pl.BlockSpec((None, 6, D), lambda i: (i // per_b, 0, 0))
    row = pl.BlockSpec((1, D), lambda i: (0, 0))
    once = pl.Buffered(1)
    res = lambda a: pl.BlockSpec(a.shape, lambda i: (0, 0), pipeline_mode=once)
    xo, ho = pl.pallas_call(
        functools.partial(_ffn_kernel, tf=tf),
        grid=(T // tm,),
        in_specs=[tok(D), tok(o_a.shape[2]), tok(o_b.shape[2]), res(w_out), modspec, row, modspec, row,
                  res(wg), res(wu), res(wd)],
        out_specs=[tok(D), tok(D)],
        out_shape=[jax.ShapeDtypeStruct((T, D), _F32), jax.ShapeDtypeStruct((T, D), _BF)],
        scratch_shapes=[pltpu.VMEM((tm, F), _BF)],
        compiler_params=_cparams(("parallel",)),
        name="ffn_dense",
    )(flat(x), flat(o_a), flat(o_b), w_out, mod, g, modn, gn, wg, wu, wd)
    return xo.reshape(B, S, D), ho.reshape(B, S, D)


def _qkv_kernel(h_ref, w_ref, gains_ref, rope_ref, q_ref, k_ref, v_ref, n_scr):
    tm, D = h_ref.shape
    h = h_ref[...]
    lane = lax.broadcasted_iota(jnp.int32, (tm, LANES), 1)
    head_a = (lane & 63) < 32

    def head_pair_norm(v, g):
        sq = v * v
        s_a = jnp.sum(jnp.where(head_a, sq, 0.0), axis=-1, keepdims=True)
        s_b = jnp.sum(jnp.where(head_a, 0.0, sq), axis=-1, keepdims=True)
        r = lax.rsqrt(jnp.where(head_a, s_a, s_b) * (1.0 / HEAD_DIM) + EPS)
        return v * r * g

    y = jnp.dot(h, w_ref[...], preferred_element_type=_F32)
    for p in range(D // LANES):
        sl = slice(p * LANES, (p + 1) * LANES)
        n_scr[:, sl] = head_pair_norm(y[:, sl], gains_ref[0:1, :])
        n_scr[:, D + p * LANES:D + (p + 1) * LANES] = head_pair_norm(
            y[:, D + p * LANES:D + (p + 1) * LANES], gains_ref[1:2, :])
        v_ref[p] = y[:, 2 * D + p * LANES:2 * D + (p + 1) * LANES].astype(_BF)

    @pl.when(pl.program_id(1) >= 0)
    def _():
        c, s = rope_ref[:, 0:128], rope_ref[:, 128:256]
        for p in range(D // LANES):
            vq = n_scr[:, p * LANES:(p + 1) * LANES]
            vk = n_scr[:, D + p * LANES:D + (p + 1) * LANES]
            q_ref[p] = (vq * c + pltpu.roll(vq, 64, 1) * s).astype(_BF)
            k_ref[p] = (vk * c + pltpu.roll(vk, 64, 1) * s).astype(_BF)


def _qkv(h, w, gains, rope, tm):
    B, S, D = h.shape
    tok = pl.BlockSpec((None, tm, D), lambda b, i: (b, i, 0))
    hp = pl.BlockSpec((None, D // LANES, tm, LANES), lambda b, i: (b, 0, i, 0))
    sds = jax.ShapeDtypeStruct((B, D // LANES, S, LANES), _BF)
    return pl.pallas_call(
        _qkv_kernel,
        grid=(B, S // tm),
        in_specs=[tok, pl.BlockSpec(w.shape, lambda b, i: (0, 0)),
                  pl.BlockSpec((2, LANES), lambda b, i: (0, 0)),
                  pl.BlockSpec((tm, 2 * LANES), lambda b, i: (i, 0))],
        out_specs=[hp, hp, hp],
        out_shape=[sds, sds, sds],
        scratch_shapes=[pltpu.VMEM((tm, 2 * D), _F32)],
        compiler_params=_cparams(("parallel", "parallel")),
        name="qkv_dil",
    )(h, w, gains, rope)


DIL_GROUP = 4


def _dil_kernel(fast_ref, q_ref, k_ref, v_ref, shift_ref, o_ref, tmp, qp, kp, vp, acc, mm, ll, accp, mmp,
                llp, bias_scr, *, tl):
    @pl.when(fast_ref[0] == 1)
    def _():
        _dil_body(q_ref, k_ref, v_ref, shift_ref, o_ref, tmp, qp, kp, vp, acc, mm, ll, accp, mmp, llp,
                  bias_scr, tl=tl, fixed_shift=True)

    @pl.when(fast_ref[0] != 1)
    def _():
        _dil_body(q_ref, k_ref, v_ref, shift_ref, o_ref, tmp, qp, kp, vp, acc, mm, ll, accp, mmp, llp,
                  bias_scr, tl=tl, fixed_shift=False)


def _dil_body(q_ref, k_ref, v_ref, shift_ref, o_ref, tmp, qp, kp, vp, acc, mm, ll, accp, mmp, llp,
              bias_scr, *, tl, fixed_shift):
    S = q_ref.shape[0]
    ng = S // DIL_GROUP
    pitch = DIL_GROUP + 1

    def spread(dst, src):
        for j in range(DIL_GROUP):
            dst[pl.ds(j, ng, stride=pitch), :] = src[pl.ds(j, ng, stride=DIL_GROUP), :]

    for src, dst in ((q_ref, qp), (k_ref, kp), (v_ref, vp)):
        tmp[...] = src[...].astype(_F32)
        spread(dst, tmp)
    lane = lax.broadcasted_iota(jnp.int32, (tl, LANES), 1)
    lo = lane < 64
    head_a = (lane & 63) < 32
    for bi, (window, d) in enumerate(DIL_PATTERNS):
        L = S // d
        half = window // (2 * d)
        t = min(tl, L)
        W = min(t + 2 * half, L)
        nt = L // t
        if d > 1:
            assert d % DIL_GROUP == 0
        sd = d * pitch // DIL_GROUP

        def rows(r, first, n, d=d, sd=sd):
            if d == 1:
                return pl.ds(first, n)
            return pl.ds(r + r // DIL_GROUP + sd * first, n, stride=sd)

        assert t % half == 0 and W <= t + 2 * half
        qa = lax.broadcasted_iota(jnp.int32, (t, W), 0)
        kb = lax.broadcasted_iota(jnp.int32, (t, W), 1)
        inside = shift_ref[0:1, 0:W] if fixed_shift else 0.0
        for case in range(3):
            ok = jnp.abs(kb - qa - case * half) <= half
            bias_scr[case, 0:t, 0:W] = jnp.where(ok, inside, NEG)

        def body(idx, carry, bi=bi, d=d, L=L, half=half, t=t, W=W, nt=nt, rows=rows):
            r = idx // nt
            l0 = (idx % nt) * t
            start = jnp.clip(l0 - half, 0, L - W)
            if d == 1:
                qt = q_ref[pl.ds(pl.multiple_of(l0, t), t), :]
                kw = k_ref[pl.ds(pl.multiple_of(start, 64), W), :]
                vw = v_ref[pl.ds(pl.multiple_of(start, 64), W), :]
            else:
                qt = qp[rows(r, l0, t), :].astype(_BF)
                kw = kp[rows(r, start, W), :].astype(_BF)
                vw = vp[rows(r, start, W), :].astype(_BF)
            lo_t = lo[:t]
            qa_t = head_a[:t]
            zq = jnp.zeros_like(qt)
            q2 = jnp.concatenate([jnp.where(qa_t, qt, zq), jnp.where(qa_t, zq, qt)], axis=0)
            s = lax.dot_general(q2, kw, (((1,), (1,)), ((), ())), preferred_element_type=_F32)
            bias = bias_scr[(l0 - start) // half, 0:t, 0:W]
            s = s + jnp.concatenate([bias, bias], axis=0)
            if fixed_shift:
                p = jnp.exp2(s)
            else:
                m = jnp.max(s, axis=-1, keepdims=True)
                p = jnp.exp2(s - m)
                m_new = jnp.where(lo_t, m[:t], m[t:])
            den = jnp.sum(p, axis=-1, keepdims=True)
            o2 = jnp.dot(p.astype(_BF), vw, preferred_element_type=_F32)
            o_new = jnp.where(lo_t, o2[:t], o2[t:])
            l_new = jnp.where(lo_t, den[:t], den[t:])
            tok = rows(r, l0, t)
            if d == 1:
                assert bi == 0
                acc[tok, :] = o_new
                ll[tok, :] = l_new
                if not fixed_shift:
                    mm[tok, :] = m_new
            elif fixed_shift:
                accp[tok, :] = accp[tok, :] + o_new
                llp[tok, :] = llp[tok, :] + l_new
            else:
                m_old = mmp[tok, :]
                mx = jnp.maximum(m_old, m_new)
                a_old = jnp.exp2(m_old - mx)
                a_new = jnp.exp2(m_new - mx)
                accp[tok, :] = accp[tok, :] * a_old + o_new * a_new
                llp[tok, :] = llp[tok, :] * a_old + l_new * a_new
                mmp[tok, :] = mx
            return carry

        lax.fori_loop(0, d * nt, body, 0, unroll=min(32 if fixed_shift else 8, d * nt))
        if bi == 0:
            spread(accp, acc)
            spread(llp, ll)
            if not fixed_shift:
                spread(mmp, mm)
    for j in range(DIL_GROUP):
        grp = pl.ds(j, ng, stride=pitch)
        tmp[pl.ds(j, ng, stride=DIL_GROUP), :] = accp[grp, :] / llp[grp, :]
    o_ref[...] = tmp[...].astype(o_ref.dtype)


def _dilated(fast, shift, q, k, v, tl):
    B, P, S, _ = q.shape
    wmax = tl + max(w // d for w, d in DIL_PATTERNS)
    spec = pl.BlockSpec((None, None, S, LANES), lambda b, p, f: (b, p, 0, 0))
    scr = pltpu.VMEM((S, LANES), _F32)
    scrp = pltpu.VMEM((S // DIL_GROUP * (DIL_GROUP + 1), LANES), _F32)
    grid_spec = pltpu.PrefetchScalarGridSpec(
        num_scalar_prefetch=1,
        grid=(B, P),
        in_specs=[spec, spec, spec, pl.BlockSpec((1, wmax), lambda b, p, f: (0, 0))],
        out_specs=spec,
        scratch_shapes=[scr, scrp, scrp, scrp, scr, scr, scr, scrp, scrp, scrp,
                        pltpu.VMEM((3, tl, wmax), _F32)],
    )
    return pl.pallas_call(
        functools.partial(_dil_kernel, tl=tl),
        grid_spec=grid_spec,
        out_shape=jax.ShapeDtypeStruct((B, P, S, LANES), _BF),
        compiler_params=_cparams(("parallel", "parallel")),
        name="dil_attn",
    )(fast, q, k, v, jnp.full((1, wmax), shift, _F32))


def _moe_kernel(te_ref, tv_ref, src_hbm, dst_hbm, h_hbm, wg_ref, wu_ref, wd_ref, out_hbm,
                xbuf, xbf, a_scr, ybuf, src_s0, src_s1, dst_s0, dst_s1, gsem, ssem, isem,
                *, tf, dump0, n_dump):
    i = pl.program_id(0)
    nt = pl.num_programs(0)
    tm = xbf.shape[0]
    slot = i % 2

    def is_valid(t):
        return jnp.logical_and(jnp.logical_and(t >= 0, t < nt), tv_ref[jnp.clip(t, 0, nt - 1)] == 1)

    def both(a, b):
        return jnp.logical_and(a, b)

    valid = is_valid(i)
    prev_valid = is_valid(i - 1)

    src_bufs = (src_s0, src_s1)
    dst_bufs = (dst_s0, dst_s1)

    def src_copy(tile, par):
        return pltpu.make_async_copy(src_hbm.at[tile], src_bufs[par], isem.at[par])

    def dst_copy(tile, par):
        return pltpu.make_async_copy(dst_hbm.at[tile], dst_bufs[par], isem.at[2 + par])

    def tile_at(first):
        return pl.ds(pl.multiple_of(first, SUBLANES), SUBLANES)

    def gather_row(r, par):
        return pltpu.make_async_copy(h_hbm.at[tile_at(src_bufs[par][r])],
                                     xbuf.at[par, tile_at(r * SUBLANES)], gsem.at[par])

    def scatter_row(r, par):
        return pltpu.make_async_copy(ybuf.at[par, tile_at(r * SUBLANES)],
                                     out_hbm.at[tile_at(dst_bufs[par][r])], ssem.at[par])

    def gather_wait(par):
        return pltpu.make_async_copy(h_hbm.at[pl.ds(0, tm * SUBLANES)], xbuf.at[par], gsem.at[par])

    def scatter_wait(par):
        return pltpu.make_async_copy(ybuf.at[par], out_hbm.at[pl.ds(0, tm * SUBLANES)], ssem.at[par])

    def for_rows(fn):
        def body(r, c):
            fn(r)
            return c
        lax.fori_loop(0, tm, body, 0, unroll=8)

    @pl.when(i == 0)
    def _():
        ybuf[...] = jnp.zeros_like(ybuf)
        for k in range(n_dump):
            pltpu.make_async_copy(ybuf.at[1], out_hbm.at[pl.ds((dump0 + k * tm) * SUBLANES, tm * SUBLANES)],
                                  ssem.at[1]).start()
        for k in range(n_dump):
            scatter_wait(1).wait()

    @pl.when(both(i == 0, valid))
    def _():
        src_copy(0, 0).start()
        src_copy(0, 0).wait()
        for_rows(lambda r: gather_row(r, 0).start())
        src_copy(1, 1).start()

    def phase1(par, with_scatter):
        xbf[...] = _tiles_to_rows(xbuf.at[par], tm).astype(_BF)
        n_chunk = -(-wg_ref.shape[1] // tf)
        per = -(-tm // n_chunk)

        def after(c):
            for r in range(c * per, min((c + 1) * per, tm)):
                gather_row(r, 1 - par).start()
                if with_scatter:
                    scatter_row(r, 1 - par).start()

        _swiglu_hidden(xbf[...], wg_ref, wu_ref, a_scr, tf, after)
        _rows_to_tiles(ybuf.at[par], jnp.dot(a_scr[...], wd_ref[...], preferred_element_type=_F32))

    for par in range(2):
        here = slot == par

        @pl.when(both(valid, here))
        def _(par=par):
            dst_copy(i, par).start()
            src_copy(i + 1, 1 - par).wait()

        @pl.when(both(prev_valid, here))
        def _(par=par):
            dst_copy(i - 1, 1 - par).wait()

        @pl.when(both(jnp.logical_or(both(i == 0, valid), prev_valid), here))
        def _(par=par):
            gather_wait(par).wait()

        @pl.when(both(is_valid(i - 2), here))
        def _(par=par):
            scatter_wait(par).wait()

        @pl.when(both(both(valid, prev_valid), here))
        def _(par=par):
            phase1(par, True)

        if par == 0:
            @pl.when(both(both(valid, jnp.logical_not(prev_valid)), here))
            def _():
                phase1(0, False)

        @pl.when(both(both(jnp.logical_not(valid), prev_valid), here))
        def _(par=par):
            for_rows(lambda r: scatter_row(r, 1 - par).start())

        @pl.when(both(is_valid(i + 1), here))
        def _(par=par):
            src_copy(i + 2, par).start()


def _moe(h, tile_e, tile_v, src, dst, wg, wu, wd, n_out_rows, tm, tf):
    T = h.shape[0] // SUBLANES
    D = wg.shape[1]
    nt = tile_e.shape[0]
    F = wg.shape[2]
    dump0 = T * TOP_K
    once = pl.Buffered(1)
    grid_spec = pltpu.PrefetchScalarGridSpec(
        num_scalar_prefetch=2,
        grid=(nt,),
        in_specs=[pl.BlockSpec(memory_space=pl.ANY), pl.BlockSpec(memory_space=pl.ANY),
                  pl.BlockSpec(memory_space=pl.ANY),
                  pl.BlockSpec((None, D, F), lambda i, te, tv: (te[i], 0, 0), pipeline_mode=once),
                  pl.BlockSpec((None, D, F), lambda i, te, tv: (te[i], 0, 0), pipeline_mode=once),
                  pl.BlockSpec((None, F, D), lambda i, te, tv: (te[i], 0, 0), pipeline_mode=once)],
        out_specs=pl.BlockSpec(memory_space=pl.ANY),
        scratch_shapes=[pltpu.VMEM((2, tm * SUBLANES, LANES), _F32), pltpu.VMEM((tm, D), _BF),
                        pltpu.VMEM((tm, F), _BF), pltpu.VMEM((2, tm * SUBLANES, LANES), _F32),
                        pltpu.SMEM((tm,), jnp.int32), pltpu.SMEM((tm,), jnp.int32),
                        pltpu.SMEM((tm,), jnp.int32), pltpu.SMEM((tm,), jnp.int32),
                        pltpu.SemaphoreType.DMA((2,)), pltpu.SemaphoreType.DMA((2,)),
                        pltpu.SemaphoreType.DMA((4,))],
    )
    return pl.pallas_call(
        functools.partial(_moe_kernel, tf=tf, dump0=dump0, n_dump=(n_out_rows - dump0) // tm),
        grid_spec=grid_spec,
        out_shape=jax.ShapeDtypeStruct((n_out_rows * SUBLANES, LANES), _F32),
        compiler_params=_cparams(("arbitrary",)),
        name="moe_experts",
    )(tile_e, tile_v, src, dst, h, wg, wu, wd)


def _route_plan(route, T, tm):
    A = T * TOP_K
    e_flat = route[:, :TOP_K].astype(jnp.int32).reshape(A)
    order = jnp.argsort(e_flat, stable=True).astype(jnp.int32)
    counts = jnp.sum((e_flat[:, None] == jnp.arange(N_EXPERTS, dtype=jnp.int32)[None, :]).astype(jnp.int32), axis=0)
    starts = jnp.cumsum(counts) - counts
    pcounts = (counts + tm - 1) // tm * tm
    pends = jnp.cumsum(pcounts)
    pstarts = pends - pcounts
    nt = A // tm + N_EXPERTS + 2
    tile0 = jnp.arange(nt, dtype=jnp.int32) * tm
    tile_v = (tile0 < pends[-1]).astype(jnp.int32)
    last_valid = jnp.maximum(pends[-1] // tm - 1, 0)
    tile_e_raw = jnp.minimum(jnp.searchsorted(pends, tile0, side="right"), N_EXPERTS - 1).astype(jnp.int32)
    tile_e = jnp.where(tile_v == 1, tile_e_raw, tile_e_raw[last_valid])
    r = jnp.arange(nt * tm, dtype=jnp.int32)
    e_r = jnp.repeat(tile_e, tm)
    within = r - pstarts[e_r]
    ok = jnp.logical_and(within < counts[e_r], jnp.repeat(tile_v, tm) == 1)
    a = order[jnp.clip(starts[e_r] + within, 0, A - 1)]
    src = jnp.where(ok, a // TOP_K, 0).astype(jnp.int32)
    dump = A + e_r * tm + jnp.clip(within - counts[e_r], 0, tm - 1)
    dst = jnp.where(ok, (a % TOP_K) * T + a // TOP_K, dump).astype(jnp.int32)
    return (tile_e, tile_v, (src * SUBLANES).reshape(nt, tm), (dst * SUBLANES).reshape(nt, tm),
            A + N_EXPERTS * tm)


def _combine_kernel(x_ref, y1_ref, y2_ref, r_ref, mod_ref, o_ref):
    tm = x_ref.shape[0]
    y = _tiles_to_rows(y1_ref, tm) * r_ref[:, 2:3] + _tiles_to_rows(y2_ref, tm) * r_ref[:, 3:4]
    o_ref[...] = x_ref[...] + mod_ref[5:6, :] * y


def _combine(x, y2, route, mod, tm):
    B, S, D = x.shape
    T = B * S
    per_b = S // tm
    out = pl.pallas_call(
        _combine_kernel,
        grid=(T // tm,),
        in_specs=[pl.BlockSpec((tm, D), lambda i: (i, 0)),
                  pl.BlockSpec((tm * SUBLANES, LANES), lambda i: (i, 0)),
                  pl.BlockSpec((tm * SUBLANES, LANES), lambda i: (T // tm + i, 0)),
                  pl.BlockSpec((tm, LANES), lambda i: (i, 0)),
                  pl.BlockSpec((None, 6, D), lambda i: (i // per_b, 0, 0))],
        out_specs=pl.BlockSpec((tm, D), lambda i: (i, 0)),
        out_shape=jax.ShapeDtypeStruct((T, D), _F32),
        compiler_params=_cparams(("parallel",)),
        name="moe_combine",
    )(x.reshape(T, D), y2, y2, route.reshape(T, LANES), mod)
    return out.reshape(B, S, D)


def _pad_cols(a, w):
    return jnp.pad(a, ((0, 0), (0, w - a.shape[1])))


def _rope_tables_even(S):
    pos = jnp.arange(S, dtype=jnp.int32)
    inv = ROPE_THETA ** (-jnp.arange(0, 32, 2, dtype=_F32) / 32)
    def cs(p):
        ang = p.astype(_F32)[:, None] * inv[None, :]
        return jnp.cos(ang), jnp.sin(ang)
    one = lambda w: jnp.ones((S, w), _F32)
    zero = lambda w: jnp.zeros((S, w), _F32)
    c, s = cs(pos)
    ca = jnp.concatenate([c, one(48), c, one(48)], 1)
    sa = jnp.concatenate([-s, zero(48), s, zero(48)], 1)
    cr, sr = cs(pos // GRID_W)
    cc, sc = cs(pos % GRID_W)
    cb = jnp.concatenate([cr, cc, one(32), cr, cc, one(32)], 1)
    sb = jnp.concatenate([-sr, -sc, zero(32), sr, sc, zero(32)], 1)
    return jnp.concatenate([ca, sa, cb, sb], 1)


def _slot_maps():
    r = MLA_ROPE // 2
    mla = ([MLA_NOPE + i for i in range(r)] + list(range(0, 64 - r))
           + [MLA_NOPE + r + i for i in range(r)] + list(range(64 - r, MLA_NOPE)))
    mla += [-1] * (LANES - len(mla))
    q = HEAD_DIM // 4
    gqa = (list(range(0, q)) + list(range(2 * q, 3 * q)) + [-1] * (64 - 2 * q)
           + list(range(q, 2 * q)) + list(range(3 * q, 4 * q)) + [-1] * (64 - 2 * q))
    return mla, gqa


def _to_slots(a, lane_map):
    idx = jnp.asarray([max(i, 0) for i in lane_map], jnp.int32)
    keep = jnp.asarray([1.0 if i >= 0 else 0.0 for i in lane_map], a.dtype)
    return jnp.take(a, idx, axis=-1) * keep


def _rope_tables_odd(S):
    pos = jnp.arange(S, dtype=_F32)
    inv = ROPE_THETA ** (-jnp.arange(0, HEAD_DIM, 2, dtype=_F32) / HEAD_DIM)
    ang = pos[:, None] * inv[None, :]
    c, s = jnp.cos(ang), jnp.sin(ang)
    return jnp.concatenate([c, c, c, c, -s, -s, s, s], 1)


def _tiles(S):
    return dict(tm_pre=min(S, 512), tq=min(S, 2048), tm_o=min(S, 512),
                tm_ffn=min(S, 512), tf_ffn=256, tm_qkv=min(S, 512),
                tl=128, tm_moe=min(S, 512), tf_moe=256, tm_c=min(S, 512))


def kernel(x, c, ada_even_w, ada_even_b, norm_even_mix, norm_even_ffn, even_w_in, mla_q_norm, mla_w_uq, mla_kv_norm, mla_w_ukv, mla_q_gain, mla_k_gain, gqa_q_gain, gqa_k_gain, even_w_out, ffn_w_gate, ffn_w_up, ffn_w_down, ada_odd_w, ada_odd_b, norm_odd_mix, norm_odd_ffn, dil_w_qkv, dil_q_gain, dil_k_gain, dil_w_out, moe_router, moe_w_gate, moe_w_up, moe_w_down):
    B, S, D = x.shape
    T = B * S
    cfg = _tiles(S)

    mod_e = _ada_mod(c, ada_even_w[0], ada_even_b[0]).reshape(B, 6, D)
    mod_o = _ada_mod(c, ada_odd_w[0], ada_odd_b[0]).reshape(B, 6, D)

    w = even_w_in[0]
    sp = [MLA_Q_RANK, MLA_Q_RANK + MLA_KV_RANK, MLA_Q_RANK + MLA_KV_RANK + MLA_ROPE]
    sp.append(sp[-1] + GQA_HEADS * HEAD_DIM)
    sp.append(sp[-1] + GQA_KV_HEADS * HEAD_DIM)
    w_cq, w_ckv, w_kpe = w[:, :sp[0]], w[:, sp[0]:sp[1]], w[:, sp[1]:sp[2]]
    w_qb, w_kb, w_vb = w[:, sp[2]:sp[3]], w[:, sp[3]:sp[4]], w[:, sp[4]:]
    mla_map, gqa_map = _slot_maps()
    na = MLA_NOPE + MLA_ROPE
    nope_only = [i if 0 <= i < MLA_NOPE else -1 for i in mla_map]
    rope_only = [i - MLA_NOPE if i >= MLA_NOPE else -1 for i in mla_map]
    gslots = lambda a, n: _to_slots(a.reshape(D, n, HEAD_DIM), gqa_map).reshape(D, n * LANES)
    w_in = jnp.concatenate([w_cq, w_ckv, _to_slots(w_kpe, rope_only), gslots(w_qb, GQA_HEADS),
                            gslots(w_kb, GQA_KV_HEADS)], axis=1).astype(_BF)
    w_uq = _to_slots(mla_w_uq[0].reshape(MLA_Q_RANK, MLA_HEADS, na), mla_map)
    w_uq = w_uq.reshape(MLA_Q_RANK, MLA_HEADS * LANES).astype(_BF)
    w_ukv = mla_w_ukv[0].reshape(MLA_KV_RANK, MLA_HEADS, MLA_NOPE + MLA_V)
    w_uk = _to_slots(w_ukv[:, :, :MLA_NOPE], nope_only).reshape(MLA_KV_RANK, MLA_HEADS * LANES).astype(_BF)
    w_uvt = w_ukv[:, :, MLA_NOPE:].reshape(MLA_KV_RANK, MLA_HEADS * MLA_V).T.astype(_BF)
    w_vbt = w_vb.T.astype(_BF)
    g_rows = [_to_slots(mla_q_gain[0], mla_map) * (na ** -0.5 * LOG2E), _to_slots(mla_k_gain[0], mla_map),
              _to_slots(gqa_q_gain[0], gqa_map) * (HEAD_DIM ** -0.5 * LOG2E), _to_slots(gqa_k_gain[0], gqa_map)]
    bound_a = na * jnp.max(jnp.abs(g_rows[0])) * jnp.max(jnp.abs(g_rows[1]))
    bound_b = HEAD_DIM * jnp.max(jnp.abs(g_rows[2])) * jnp.max(jnp.abs(g_rows[3]))
    fast_a, fast_b = bound_a <= SCORE_BOUND_MAX, bound_b <= SCORE_BOUND_MAX
    last_lane = jnp.zeros((LANES,), _F32).at[LANES - 1].set(1.0)
    g_rows += [last_lane, last_lane * jnp.where(fast_a, -bound_a, 0.0),
               last_lane, last_lane * jnp.where(fast_b, -bound_b, 0.0)]
    gains_e = jnp.stack(g_rows)
    rope_e = _rope_tables_even(S)
    q_all, k_all, vt_all = _pre_even(
        x, mod_e, norm_even_mix[0].reshape(1, D), w_in, mla_q_norm[0].reshape(1, -1), w_uq,
        mla_kv_norm[0].reshape(1, -1), w_uk, w_uvt, w_vbt, gains_e, rope_e, cfg["tm_pre"])
    as_flag = lambda f: f.astype(jnp.int32).reshape(1)
    o_a = _attention(as_flag(fast_a), q_all, k_all, vt_all, q_off=0, k_off=0, shared_kv=False, tq=cfg["tq"])
    o_b = _attention(as_flag(fast_b), q_all, k_all, vt_all, q_off=MLA_HEADS, k_off=MLA_HEADS,
                     shared_kv=True, tq=cfg["tq"])
    x2, h2 = _ffn(x, o_a, o_b, even_w_out[0].astype(_BF), mod_e, norm_even_ffn[0].reshape(1, D), mod_o,
                  norm_odd_mix[0].reshape(1, D), ffn_w_gate[0].astype(_BF), ffn_w_up[0].astype(_BF),
                  ffn_w_down[0].astype(_BF), cfg["tm_ffn"], cfg["tf_ffn"])

    hh = HEAD_DIM // 2
    pair = lambda v: jnp.concatenate([v[:hh], v[:hh], v[hh:], v[hh:]])
    gains_o = jnp.stack([pair(dil_q_gain[0]) * (HEAD_DIM ** -0.5 * LOG2E), pair(dil_k_gain[0])])
    wq, wk, wv = jnp.split(dil_w_qkv[0], 3, axis=1)
    perm = lambda w: w.reshape(D, D // LANES, 2, 2, hh).transpose(0, 1, 3, 2, 4).reshape(D, D)
    w_qkv = jnp.concatenate([perm(wq), perm(wk), wv], axis=1).astype(_BF)
    qd, kd, vd = _qkv(h2, w_qkv, gains_o, _rope_tables_odd(S), cfg["tm_qkv"])
    bound_d = HEAD_DIM * jnp.max(jnp.abs(gains_o[0])) * jnp.max(jnp.abs(gains_o[1]))
    fast_d = bound_d <= SCORE_BOUND_MAX
    o_d = _dilated(as_flag(fast_d), jnp.where(fast_d, -bound_d, 0.0), qd, kd, vd, cfg["tl"])
    r32 = _pad_cols(moe_router[0], LANES)
    r_hi = r32.astype(_BF)
    r_lo = (r32 - r_hi.astype(_F32)).astype(_BF)
    x3, h3, route = _oproj_route(x2, mod_o, norm_odd_ffn[0].reshape(1, D), dil_w_out[0].astype(_BF), o_d,
                                 jnp.stack([r_hi, r_lo]), cfg["tm_o"])
    tm = cfg["tm_moe"]
    tile_e, tile_v, src, dst, n_rows = _route_plan(route.reshape(T, LANES), T, tm)
    y2 = _moe(h3, tile_e, tile_v, src, dst, moe_w_gate[0].astype(_BF),
              moe_w_up[0].astype(_BF), moe_w_down[0].astype(_BF), n_rows, tm, cfg["tf_moe"])
    return _combine(x3, y2, route, mod_o, cfg["tm_c"])
```

```python
import functools
import math

import jax
import jax.numpy as jnp
from jax import lax
from jax.experimental import pallas as pl
from jax.experimental.pallas import tpu as pltpu

_BF = jnp.bfloat16
_F32 = jnp.float32

GRID_W = 64
HEAD_DIM = 64
ROPE_THETA = 10000.0
EPS = 1e-6
MLA_HEADS = 8
MLA_Q_RANK = 256
MLA_KV_RANK = 128
MLA_NOPE = 64
MLA_ROPE = 32
MLA_V = 64
GQA_HEADS = 8
GQA_KV_HEADS = 2
DIL_PATTERNS = ((128, 1), (512, 4), (2048, 16))
N_EXPERTS = 8
TOP_K = 2
NEG = -1e30
LOG2E = math.log2(math.e)

LANES = 128
VMEM_LIMIT = 56 * 1024 * 1024


def _cparams(sem, vmem=VMEM_LIMIT):
    return pltpu.CompilerParams(dimension_semantics=sem, vmem_limit_bytes=vmem)


def _silu(x):
    return x / (1.0 + jnp.exp(-x))


def _modulate(x, g, shift, scale):
    ms = jnp.mean(x * x, axis=-1, keepdims=True)
    return x * lax.rsqrt(ms + EPS) * g * (1.0 + scale) + shift


def _norm(v, n, g):
    return v * lax.rsqrt(jnp.sum(v * v, axis=-1, keepdims=True) * (1.0 / n) + EPS) * g


SUBLANES = 8


def _rows_to_tiles(ref, x, row0=0):
    tm = x.shape[0]
    for c in range(SUBLANES):
        ref[pl.ds(row0 * SUBLANES + c, tm, stride=SUBLANES), :] = x[:, c * LANES:(c + 1) * LANES]


def _tiles_to_rows(ref, tm):
    return jnp.concatenate([ref[pl.ds(c, tm, stride=SUBLANES), :] for c in range(SUBLANES)], axis=1)


def _mod_kernel(c_ref, w_ref, b_ref, o_ref):
    sc = _silu(c_ref[...])
    o_ref[...] = jnp.dot(sc.astype(_BF), w_ref[...].astype(_BF),
                         preferred_element_type=_F32) + b_ref[...]


def _ada_mod(c, w, b):
    B, D = c.shape
    N = w.shape[1]
    tn = min(N, 1536)
    return pl.pallas_call(
        _mod_kernel,
        grid=(N // tn,),
        in_specs=[pl.BlockSpec((B, D), lambda j: (0, 0)),
                  pl.BlockSpec((D, tn), lambda j: (0, j)),
                  pl.BlockSpec((1, tn), lambda j: (0, j))],
        out_specs=pl.BlockSpec((B, tn), lambda j: (0, j)),
        out_shape=jax.ShapeDtypeStruct((B, N), _F32),
        compiler_params=_cparams(("arbitrary",)),
        name="ada_mod",
    )(c, w, b.reshape(1, N))


def _pre_even_kernel(x_ref, mod_ref, g_ref, w_in_ref, qn_ref, w_uq_ref, kvn_ref, w_uk_ref, w_uvt_ref,
                     w_vbt_ref, gains_ref, rope_ref, q_ref, k_ref, vt_ref, nq_scr, nk_scr):
    h = _modulate(x_ref[...], g_ref[...], mod_ref[0:1, :], mod_ref[1:2, :]).astype(_BF)
    y = jnp.dot(h, w_in_ref[...], preferred_element_type=_F32)
    cqn = _norm(y[:, 0:256], MLA_Q_RANK, qn_ref[...]).astype(_BF)
    qa = jnp.dot(cqn, w_uq_ref[...], preferred_element_type=_F32)
    ckvn = _norm(y[:, 256:384], MLA_KV_RANK, kvn_ref[...]).astype(_BF)
    kn = jnp.dot(ckvn, w_uk_ref[...], preferred_element_type=_F32)
    kpe = y[:, 384:512]
    nt_dims = (((1,), (1,)), ((), ()))
    vt_a = lax.dot_general(w_uvt_ref[...], ckvn, nt_dims, preferred_element_type=_F32).astype(_BF)
    vt_b = lax.dot_general(w_vbt_ref[...], h, nt_dims, preferred_element_type=_F32).astype(_BF)
    tm = x_ref.shape[0]
    ones_blk = jnp.where(lax.broadcasted_iota(jnp.int32, (LANES - MLA_V, tm), 0) == 0, 1.0, 0.0).astype(_BF)
    for hh in range(MLA_HEADS + GQA_KV_HEADS):
        src = vt_a[hh * MLA_V:(hh + 1) * MLA_V] if hh < MLA_HEADS else \
            vt_b[(hh - MLA_HEADS) * HEAD_DIM:(hh - MLA_HEADS + 1) * HEAD_DIM]
        vt_ref[hh * LANES:hh * LANES + MLA_V, :] = src
        vt_ref[hh * LANES + MLA_V:(hh + 1) * LANES, :] = ones_blk
    ca, sa = rope_ref[:, 0:128], rope_ref[:, 128:256]
    cb, sb = rope_ref[:, 256:384], rope_ref[:, 384:512]
    gqa, gka = gains_ref[0:1, :], gains_ref[1:2, :]
    gqb, gkb = gains_ref[2:3, :], gains_ref[3:4, :]
    na = MLA_NOPE + MLA_ROPE

    for hh in range(MLA_HEADS):
        sl = slice(hh * LANES, (hh + 1) * LANES)
        nq_scr[:, sl] = _norm(qa[:, sl], na, gqa)
        nk_scr[:, sl] = _norm(kn[:, sl] + kpe, na, gka)
    for hh in range(GQA_HEADS):
        src = slice(512 + hh * LANES, 512 + (hh + 1) * LANES)
        dst = slice((MLA_HEADS + hh) * LANES, (MLA_HEADS + hh + 1) * LANES)
        nq_scr[:, dst] = _norm(y[:, src], HEAD_DIM, gqb)
    for g in range(GQA_KV_HEADS):
        src = slice(1536 + g * LANES, 1536 + (g + 1) * LANES)
        dst = slice((MLA_HEADS + g) * LANES, (MLA_HEADS + g + 1) * LANES)
        nk_scr[:, dst] = _norm(y[:, src], HEAD_DIM, gkb)

    @pl.when(pl.program_id(1) >= 0)
    def _():
        def rope(ref, out, n_a, n_all, row):
            for hh in range(n_all):
                sl = slice(hh * LANES, (hh + 1) * LANES)
                c, s = (ca, sa) if hh < n_a else (cb, sb)
                b = gains_ref[row:row + 1, :] if hh < n_a else gains_ref[row + 2:row + 3, :]
                v = ref[:, sl]
                out[:, sl] = (v * c + pltpu.roll(v, 64, 1) * s + b).astype(_BF)

        rope(nq_scr, q_ref, MLA_HEADS, MLA_HEADS + GQA_HEADS, 4)
        rope(nk_scr, k_ref, MLA_HEADS, MLA_HEADS + GQA_KV_HEADS, 5)


def _pre_even(x, mod, g, w_in, qn, w_uq, kvn, w_uk, w_uvt, w_vbt, gains, rope, tm):
    B, S, D = x.shape
    nq = (MLA_HEADS + GQA_HEADS) * LANES
    nk = (MLA_HEADS + GQA_KV_HEADS) * LANES
    nvt = (MLA_HEADS + GQA_KV_HEADS) * LANES
    full = lambda a: pl.BlockSpec(a.shape, lambda b, i: (0,) * a.ndim)
    tok = lambda w: pl.BlockSpec((None, tm, w), lambda b, i: (b, i, 0))
    return pl.pallas_call(
        _pre_even_kernel,
        grid=(B, S // tm),
        in_specs=[tok(D),
                  pl.BlockSpec((None, 6, D), lambda b, i: (b, 0, 0)),
                  full(g), full(w_in), full(qn), full(w_uq), full(kvn), full(w_uk), full(w_uvt),
                  full(w_vbt), full(gains),
                  pl.BlockSpec((tm, rope.shape[1]), lambda b, i: (i, 0))],
        out_specs=[tok(nq), tok(nk), pl.BlockSpec((None, None, nvt, tm), lambda b, i: (b, i, 0, 0))],
        out_shape=[jax.ShapeDtypeStruct((B, S, nq), _BF), jax.ShapeDtypeStruct((B, S, nk), _BF),
                   jax.ShapeDtypeStruct((B, S // tm, nvt, tm), _BF)],
        scratch_shapes=[pltpu.VMEM((tm, nq), _F32), pltpu.VMEM((tm, nk), _F32)],
        compiler_params=_cparams(("parallel", "parallel")),
        name="pre_even",
    )(x, mod, g, w_in, qn, w_uq, kvn, w_uk, w_uvt, w_vbt, gains, rope)


SCORE_BOUND_MAX = 60.0


def _attn_kernel(fast_ref, q_ref, k_ref, vt_ref, o_ref, s_scr, *, shared_kv):
    @pl.when(fast_ref[0] == 1)
    def _():
        _attn_fixed_shift(q_ref, k_ref, vt_ref, o_ref, shared_kv=shared_kv)

    @pl.when(fast_ref[0] != 1)
    def _():
        _attn_running_max(q_ref, k_ref, vt_ref, o_ref, s_scr, shared_kv=shared_kv)


def _attn_fixed_shift(q_ref, k_ref, vt_ref, o_ref, *, shared_kv):
    tq = q_ref.shape[0]
    nk, _, tk = vt_ref.shape
    qs = [q_ref[:, hh * LANES:(hh + 1) * LANES] for hh in range(2)]

    def body(j, carry):
        r0 = pl.multiple_of(j * tk, tk)
        out = []
        for hh in range(2):
            kc = 0 if shared_kv else hh * LANES
            k = k_ref[pl.ds(r0, tk), kc:kc + LANES]
            st = lax.dot_general(k, qs[hh], (((1,), (1,)), ((), ())), preferred_element_type=_F32)
            p = jnp.exp2(st).astype(_BF)
            out.append(carry[hh] + jnp.dot(vt_ref[j, kc:kc + LANES, :], p, preferred_element_type=_F32))
        return tuple(out)

    z = jnp.zeros((LANES, tq), _F32)
    res = lax.fori_loop(0, nk, body, (z, z), unroll=True)
    nv = HEAD_DIM
    halves = [acc[0:nv, :] / acc[nv:nv + 1, :] for acc in res]
    o_ref[...] = jnp.concatenate(halves, axis=0).T.astype(o_ref.dtype)


def _attn_running_max(q_ref, k_ref, vt_ref, o_ref, s_scr, *, shared_kv):
    tq = q_ref.shape[0]
    nk, _, tk = vt_ref.shape
    qs = [q_ref[:, hh * LANES:(hh + 1) * LANES] for hh in range(2)]

    def scores(j, hh):
        r0 = j * tk if isinstance(j, int) else pl.multiple_of(j * tk, tk)
        kc = 0 if shared_kv else hh * LANES
        k = k_ref[pl.ds(r0, tk), kc:kc + LANES]
        return lax.dot_general(k, qs[hh], (((1,), (1,)), ((), ())), preferred_element_type=_F32)

    def put_scores(j, hh, slot):
        st = scores(j, hh)
        s_scr[hh, slot] = st
        return jnp.max(st, axis=0, keepdims=True)

    def step(j, hh, carry, cur, last=False):
        m, mc, acc = carry
        mc_next = mc if last else put_scores(j + 1, hh, 1 - cur)
        st = s_scr[hh, cur]
        m_new = jnp.maximum(m, mc)
        alpha = jnp.exp2(m - m_new)
        p = jnp.exp2(st - m_new).astype(_BF)
        vc = 0 if shared_kv else hh * LANES
        acc = alpha * acc + jnp.dot(vt_ref[j, vc:vc + LANES, :], p, preferred_element_type=_F32)
        return m_new, mc_next, acc

    def body(jj, carry):
        c = list(carry)
        for sub in range(2):
            for hh in range(2):
                c[hh] = step(2 * jj + sub, hh, c[hh], sub)
        return tuple(c)

    init = tuple((jnp.full((1, tq), NEG, _F32), put_scores(0, hh, 0), jnp.zeros((LANES, tq), _F32))
                 for hh in range(2))
    res = lax.fori_loop(0, nk // 2 - 1, body, init)
    res = list(res)
    for sub in range(2):
        for hh in range(2):
            res[hh] = step(nk - 2 + sub, hh, res[hh], sub, last=(sub == 1))
    nv = HEAD_DIM
    halves = [acc[0:nv, :] / acc[nv:nv + 1, :] for (_, _, acc) in res]
    o_ref[...] = jnp.concatenate(halves, axis=0).T.astype(o_ref.dtype)


def _attention(fast, q, k, vt, *, q_off, k_off, shared_kv, tq):
    B, S, _ = q.shape
    nk, _, tk = vt.shape[1:]
    assert nk % 2 == 0 and nk >= 2
    npairs = 4
    if shared_kv:
        kspec = pl.BlockSpec((None, S, LANES), lambda b, p, i, f: (b, 0, k_off + p // 2))
        vspec = pl.BlockSpec((None, nk, LANES, tk), lambda b, p, i, f: (b, 0, k_off + p // 2, 0))
    else:
        kspec = pl.BlockSpec((None, S, 2 * LANES), lambda b, p, i, f: (b, 0, k_off // 2 + p))
        vspec = pl.BlockSpec((None, nk, 2 * LANES, tk), lambda b, p, i, f: (b, 0, k_off // 2 + p, 0))
    grid_spec = pltpu.PrefetchScalarGridSpec(
        num_scalar_prefetch=1,
        grid=(B, npairs, S // tq),
        in_specs=[pl.BlockSpec((None, tq, 2 * LANES), lambda b, p, i, f: (b, i, q_off // 2 + p)),
                  kspec, vspec],
        out_specs=pl.BlockSpec((None, tq, LANES), lambda b, p, i, f: (b, i, p)),
        scratch_shapes=[pltpu.VMEM((2, 2, tk, tq), _F32)],
    )
    return pl.pallas_call(
        functools.partial(_attn_kernel, shared_kv=shared_kv),
        grid_spec=grid_spec,
        out_shape=jax.ShapeDtypeStruct((B, S, npairs * LANES), _BF),
        compiler_params=_cparams(("parallel", "parallel", "parallel")),
        name="attn_gqa" if shared_kv else "attn_mla",
    )(fast, q, k, vt)


def _oproj_route_kernel(x_ref, mod_ref, g_ref, w_ref, o_ref, r_ref, x_out, h_out, route_out, *, sub):
    for r0 in range(0, x_ref.shape[0], sub):
        rows = slice(r0, r0 + sub)
        o = jnp.concatenate([o_ref[p, rows, :] for p in range(o_ref.shape[0])], axis=1)
        y = jnp.dot(o, w_ref[...], preferred_element_type=_F32)
        x1 = x_ref[rows, :] + mod_ref[2:3, :] * y
        x_out[rows, :] = x1
        h = _modulate(x1, g_ref[...], mod_ref[3:4, :], mod_ref[4:5, :])
        _rows_to_tiles(h_out, h, r0)
        hh = h.astype(_BF)
        hl = (h - hh.astype(_F32)).astype(_BF)
        rh, rl = r_ref[0], r_ref[1]
        logits = (jnp.dot(hh, rh, preferred_element_type=_F32)
                  + jnp.dot(hh, rl, preferred_element_type=_F32)
                  + jnp.dot(hl, rh, preferred_element_type=_F32))
        tm = logits.shape[0]
        lane = lax.broadcasted_iota(jnp.int32, (tm, LANES), 1)
        lg = jnp.where(lane < N_EXPERTS, logits, NEG)
        m1 = jnp.max(lg, axis=-1, keepdims=True)
        lanef = lane.astype(_F32)
        i1 = jnp.min(jnp.where(lg == m1, lanef, float(LANES)), axis=-1, keepdims=True)
        lg2 = jnp.where(lanef == i1, NEG, lg)
        m2 = jnp.max(lg2, axis=-1, keepdims=True)
        i2 = jnp.min(jnp.where(lg2 == m2, lanef, float(LANES)), axis=-1, keepdims=True)
        e = jnp.exp(m2 - m1)
        g1 = 1.0 / (1.0 + e)
        g2 = e / (1.0 + e)
        route_out[rows, :] = jnp.where(lane == 0, i1,
                                   jnp.where(lane == 1, i2,
                                             jnp.where(lane == 2, g1, jnp.where(lane == 3, g2, 0.0))))


def _oproj_route(x, mod, g, w, o, router, tm):
    B, S, D = x.shape
    assert D == SUBLANES * LANES
    per_b = S // tm
    tok = lambda wd: pl.BlockSpec((None, tm, wd), lambda b, i: (b, i, 0))
    return pl.pallas_call(
        functools.partial(_oproj_route_kernel, sub=min(tm, 128)),
        grid=(B, S // tm),
        in_specs=[tok(D), pl.BlockSpec((None, 6, D), lambda b, i: (b, 0, 0)),
                  pl.BlockSpec((1, D), lambda b, i: (0, 0)),
                  pl.BlockSpec(w.shape, lambda b, i: (0, 0)),
                  pl.BlockSpec((None, o.shape[1], tm, LANES), lambda b, i: (b, 0, i, 0)),
                  pl.BlockSpec(router.shape, lambda b, i: (0, 0, 0))],
        out_specs=[tok(D), pl.BlockSpec((tm * SUBLANES, LANES), lambda b, i: (b * per_b + i, 0)), tok(LANES)],
        out_shape=[jax.ShapeDtypeStruct((B, S, D), _F32),
                   jax.ShapeDtypeStruct((B * S * SUBLANES, LANES), _F32),
                   jax.ShapeDtypeStruct((B, S, LANES), _F32)],
        compiler_params=_cparams(("parallel", "parallel")),
        name="oproj_route",
    )(x, mod, g, w, o, router)


def _swiglu_hidden(x, wg_ref, wu_ref, a_scr, tf, after_chunk=None):
    F = wg_ref.shape[1]
    for c0 in range(0, F, tf):
        c1 = min(c0 + tf, F)
        g = jnp.dot(x, wg_ref[:, c0:c1], preferred_element_type=_F32)
        u = jnp.dot(x, wu_ref[:, c0:c1], preferred_element_type=_F32)
        a_scr[:, c0:c1] = (_silu(g) * u).astype(_BF)
        if after_chunk is not None:
            after_chunk(c0 // tf)


def _ffn_kernel(x_ref, oa_ref, ob_ref, wo_ref, mod_ref, g_ref, modn_ref, gn_ref, wg_ref, wu_ref, wd_ref,
                x_out, h_out, a_scr, *, tf):
    wa = oa_ref.shape[1]
    y = (jnp.dot(oa_ref[...], wo_ref[0:wa, :], preferred_element_type=_F32)
         + jnp.dot(ob_ref[...], wo_ref[wa:, :], preferred_element_type=_F32))
    x1 = x_ref[...] + mod_ref[2:3, :] * y
    h = _modulate(x1, g_ref[...], mod_ref[3:4, :], mod_ref[4:5, :]).astype(_BF)
    _swiglu_hidden(h, wg_ref, wu_ref, a_scr, tf)
    y = jnp.dot(a_scr[...], wd_ref[...], preferred_element_type=_F32)
    x2 = x1 + mod_ref[5:6, :] * y
    x_out[...] = x2
    h_out[...] = _modulate(x2, gn_ref[...], modn_ref[0:1, :], modn_ref[1:2, :]).astype(h_out.dtype)


def _ffn(x, o_a, o_b, w_out, mod, g, modn, gn, wg, wu, wd, tm, tf):
    B, S, D = x.shape
    T = B * S
    F = wg.shape[1]
    per_b = S // tm
    flat = lambda a: a.reshape(T, a.shape[2])
    tok = lambda w: pl.BlockSpec((tm, w), lambda i: (i, 0))
    modspec = pl.BlockSpec((None, 6, D), lambda i: (i // per_b, 0, 0))
    row = pl.BlockSpec((1, D), lambda i: (0, 0))
    once = pl.Buffered(1)
    res = lambda a: pl.BlockSpec(a.shape, lambda i: (0, 0), pipeline_mode=once)
    xo, ho = pl.pallas_call(
        functools.partial(_ffn_kernel, tf=tf),
        grid=(T // tm,),
        in_specs=[tok(D), tok(o_a.shape[2]), tok(o_b.shape[2]), res(w_out), modspec, row, modspec, row,
                  res(wg), res(wu), res(wd)],
        out_specs=[tok(D), tok(D)],
        out_shape=[jax.ShapeDtypeStruct((T, D), _F32), jax.ShapeDtypeStruct((T, D), _BF)],
        scratch_shapes=[pltpu.VMEM((tm, F), _BF)],
        compiler_params=_cparams(("parallel",)),
        name="ffn_dense",
    )(flat(x), flat(o_a), flat(o_b), w_out, mod, g, modn, gn, wg, wu, wd)
    return xo.reshape(B, S, D), ho.reshape(B, S, D)


def _qkv_kernel(h_ref, w_ref, gains_ref, rope_ref, q_ref, k_ref, v_ref, n_scr):
    tm, D = h_ref.shape
    h = h_ref[...]
    lane = lax.broadcasted_iota(jnp.int32, (tm, LANES), 1)
    head_a = (lane & 63) < 32

    def head_pair_norm(v, g):
        sq = v * v
        s_a = jnp.sum(jnp.where(head_a, sq, 0.0), axis=-1, keepdims=True)
        s_b = jnp.sum(jnp.where(head_a, 0.0, sq), axis=-1, keepdims=True)
        r = lax.rsqrt(jnp.where(head_a, s_a, s_b) * (1.0 / HEAD_DIM) + EPS)
        return v * r * g

    y = jnp.dot(h, w_ref[...], preferred_element_type=_F32)
    for p in range(D // LANES):
        sl = slice(p * LANES, (p + 1) * LANES)
        n_scr[:, sl] = head_pair_norm(y[:, sl], gains_ref[0:1, :])
        n_scr[:, D + p * LANES:D + (p + 1) * LANES] = head_pair_norm(
            y[:, D + p * LANES:D + (p + 1) * LANES], gains_ref[1:2, :])
        v_ref[p] = y[:, 2 * D + p * LANES:2 * D + (p + 1) * LANES].astype(_BF)

    @pl.when(pl.program_id(1) >= 0)
    def _():
        c, s = rope_ref[:, 0:128], rope_ref[:, 128:256]
        for p in range(D // LANES):
            vq = n_scr[:, p * LANES:(p + 1) * LANES]
            vk = n_scr[:, D + p * LANES:D + (p + 1) * LANES]
            q_ref[p] = (vq * c + pltpu.roll(vq, 64, 1) * s).astype(_BF)
            k_ref[p] = (vk * c + pltpu.roll(vk, 64, 1) * s).astype(_BF)


def _qkv(h, w, gains, rope, tm):
    B, S, D = h.shape
    tok = pl.BlockSpec((None, tm, D), lambda b, i: (b, i, 0))
    hp = pl.BlockSpec((None, D // LANES, tm, LANES), lambda b, i: (b, 0, i, 0))
    sds = jax.ShapeDtypeStruct((B, D // LANES, S, LANES), _BF)
    return pl.pallas_call(
        _qkv_kernel,
        grid=(B, S // tm),
        in_specs=[tok, pl.BlockSpec(w.shape, lambda b, i: (0, 0)),
                  pl.BlockSpec((2, LANES), lambda b, i: (0, 0)),
                  pl.BlockSpec((tm, 2 * LANES), lambda b, i: (i, 0))],
        out_specs=[hp, hp, hp],
        out_shape=[sds, sds, sds],
        scratch_shapes=[pltpu.VMEM((tm, 2 * D), _F32)],
        compiler_params=_cparams(("parallel", "parallel")),
        name="qkv_dil",
    )(h, w, gains, rope)


DIL_GROUP = 4


def _dil_kernel(fast_ref, q_ref, k_ref, v_ref, shift_ref, o_ref, tmp, qp, kp, vp, acc, mm, ll, accp, mmp,
                llp, bias_scr, *, tl):
    @pl.when(fast_ref[0] == 1)
    def _():
        _dil_body(q_ref, k_ref, v_ref, shift_ref, o_ref, tmp, qp, kp, vp, acc, mm, ll, accp, mmp, llp,
                  bias_scr, tl=tl, fixed_shift=True)

    @pl.when(fast_ref[0] != 1)
    def _():
        _dil_body(q_ref, k_ref, v_ref, shift_ref, o_ref, tmp, qp, kp, vp, acc, mm, ll, accp, mmp, llp,
                  bias_scr, tl=tl, fixed_shift=False)


def _dil_body(q_ref, k_ref, v_ref, shift_ref, o_ref, tmp, qp, kp, vp, acc, mm, ll, accp, mmp, llp,
              bias_scr, *, tl, fixed_shift):
    S = q_ref.shape[0]
    ng = S // DIL_GROUP
    pitch = DIL_GROUP + 1

    def spread(dst, src):
        for j in range(DIL_GROUP):
            dst[pl.ds(j, ng, stride=pitch), :] = src[pl.ds(j, ng, stride=DIL_GROUP), :]

    for src, dst in ((q_ref, qp), (k_ref, kp), (v_ref, vp)):
        tmp[...] = src[...].astype(_F32)
        spread(dst, tmp)
    lane = lax.broadcasted_iota(jnp.int32, (tl, LANES), 1)
    lo = lane < 64
    head_a = (lane & 63) < 32
    for bi, (window, d) in enumerate(DIL_PATTERNS):
        L = S // d
        half = window // (2 * d)
        t = min(tl, L)
        W = min(t + 2 * half, L)
        nt = L // t
        if d > 1:
            assert d % DIL_GROUP == 0
        sd = d * pitch // DIL_GROUP

        def rows(r, first, n, d=d, sd=sd):
            if d == 1:
                return pl.ds(first, n)
            return pl.ds(r + r // DIL_GROUP + sd * first, n, stride=sd)

        assert t % half == 0 and W <= t + 2 * half
        qa = lax.broadcasted_iota(jnp.int32, (t, W), 0)
        kb = lax.broadcasted_iota(jnp.int32, (t, W), 1)
        inside = shift_ref[0:1, 0:W] if fixed_shift else 0.0
        for case in range(3):
            ok = jnp.abs(kb - qa - case * half) <= half
            bias_scr[case, 0:t, 0:W] = jnp.where(ok, inside, NEG)

        def body(idx, carry, bi=bi, d=d, L=L, half=half, t=t, W=W, nt=nt, rows=rows):
            r = idx // nt
            l0 = (idx % nt) * t
            start = jnp.clip(l0 - half, 0, L - W)
            if d == 1:
                qt = q_ref[pl.ds(pl.multiple_of(l0, t), t), :]
                kw = k_ref[pl.ds(pl.multiple_of(start, 64), W), :]
                vw = v_ref[pl.ds(pl.multiple_of(start, 64), W), :]
            else:
                qt = qp[rows(r, l0, t), :].astype(_BF)
                kw = kp[rows(r, start, W), :].astype(_BF)
                vw = vp[rows(r, start, W), :].astype(_BF)
            lo_t = lo[:t]
            qa_t = head_a[:t]
            zq = jnp.zeros_like(qt)
            q2 = jnp.concatenate([jnp.where(qa_t, qt, zq), jnp.where(qa_t, zq, qt)], axis=0)
            s = lax.dot_general(q2, kw, (((1,), (1,)), ((), ())), preferred_element_type=_F32)
            bias = bias_scr[(l0 - start) // half, 0:t, 0:W]
            s = s + jnp.concatenate([bias, bias], axis=0)
            if fixed_shift:
                p = jnp.exp2(s)
            else:
                m = jnp.max(s, axis=-1, keepdims=True)
                p = jnp.exp2(s - m)
                m_new = jnp.where(lo_t, m[:t], m[t:])
            den = jnp.sum(p, axis=-1, keepdims=True)
            o2 = jnp.dot(p.astype(_BF), vw, preferred_element_type=_F32)
            o_new = jnp.where(lo_t, o2[:t], o2[t:])
            l_new = jnp.where(lo_t, den[:t], den[t:])
            tok = rows(r, l0, t)
            if d == 1:
                assert bi == 0
                acc[tok, :] = o_new
                ll[tok, :] = l_new
                if not fixed_shift:
                    mm[tok, :] = m_new
            elif fixed_shift:
                accp[tok, :] = accp[tok, :] + o_new
                llp[tok, :] = llp[tok, :] + l_new
            else:
                m_old = mmp[tok, :]
                mx = jnp.maximum(m_old, m_new)
                a_old = jnp.exp2(m_old - mx)
                a_new = jnp.exp2(m_new - mx)
                accp[tok, :] = accp[tok, :] * a_old + o_new * a_new
                llp[tok, :] = llp[tok, :] * a_old + l_new * a_new
                mmp[tok, :] = mx
            return carry

        lax.fori_loop(0, d * nt, body, 0, unroll=min(32 if fixed_shift else 8, d * nt))
        if bi == 0:
            spread(accp, acc)
            spread(llp, ll)
            if not fixed_shift:
                spread(mmp, mm)
    for j in range(DIL_GROUP):
        grp = pl.ds(j, ng, stride=pitch)
        tmp[pl.ds(j, ng, stride=DIL_GROUP), :] = accp[grp, :] / llp[grp, :]
    o_ref[...] = tmp[...].astype(o_ref.dtype)


def _dilated(fast, shift, q, k, v, tl):
    B, P, S, _ = q.shape
    wmax = tl + max(w // d for w, d in DIL_PATTERNS)
    spec = pl.BlockSpec((None, None, S, LANES), lambda b, p, f: (b, p, 0, 0))
    scr = pltpu.VMEM((S, LANES), _F32)
    scrp = pltpu.VMEM((S // DIL_GROUP * (DIL_GROUP + 1), LANES), _F32)
    grid_spec = pltpu.PrefetchScalarGridSpec(
        num_scalar_prefetch=1,
        grid=(B, P),
        in_specs=[spec, spec, spec, pl.BlockSpec((1, wmax), lambda b, p, f: (0, 0))],
        out_specs=spec,
        scratch_shapes=[scr, scrp, scrp, scrp, scr, scr, scr, scrp, scrp, scrp,
                        pltpu.VMEM((3, tl, wmax), _F32)],
    )
    return pl.pallas_call(
        functools.partial(_dil_kernel, tl=tl),
        grid_spec=grid_spec,
        out_shape=jax.ShapeDtypeStruct((B, P, S, LANES), _BF),
        compiler_params=_cparams(("parallel", "parallel")),
        name="dil_attn",
    )(fast, q, k, v, jnp.full((1, wmax), shift, _F32))


def _moe_kernel(te_ref, tv_ref, src_hbm, dst_hbm, h_hbm, wg_ref, wu_ref, wd_ref, out_hbm,
                xbuf, xbf, a_scr, ybuf, src_s0, src_s1, dst_s0, dst_s1, gsem, ssem, isem,
                *, tf, dump0, n_dump):
    i = pl.program_id(0)
    nt = pl.num_programs(0)
    tm = xbf.shape[0]
    slot = i % 2

    def is_valid(t):
        return jnp.logical_and(jnp.logical_and(t >= 0, t < nt), tv_ref[jnp.clip(t, 0, nt - 1)] == 1)

    def both(a, b):
        return jnp.logical_and(a, b)

    valid = is_valid(i)
    prev_valid = is_valid(i - 1)

    src_bufs = (src_s0, src_s1)
    dst_bufs = (dst_s0, dst_s1)

    def src_copy(tile, par):
        return pltpu.make_async_copy(src_hbm.at[tile], src_bufs[par], isem.at[par])

    def dst_copy(tile, par):
        return pltpu.make_async_copy(dst_hbm.at[tile], dst_bufs[par], isem.at[2 + par])

    def tile_at(first):
        return pl.ds(pl.multiple_of(first, SUBLANES), SUBLANES)

    def gather_row(r, par):
        return pltpu.make_async_copy(h_hbm.at[tile_at(src_bufs[par][r])],
                                     xbuf.at[par, tile_at(r * SUBLANES)], gsem.at[par])

    def scatter_row(r, par):
        return pltpu.make_async_copy(ybuf.at[par, tile_at(r * SUBLANES)],
                                     out_hbm.at[tile_at(dst_bufs[par][r])], ssem.at[par])

    def gather_wait(par):
        return pltpu.make_async_copy(h_hbm.at[pl.ds(0, tm * SUBLANES)], xbuf.at[par], gsem.at[par])

    def scatter_wait(par):
        return pltpu.make_async_copy(ybuf.at[par], out_hbm.at[pl.ds(0, tm * SUBLANES)], ssem.at[par])

    def for_rows(fn):
        def body(r, c):
            fn(r)
            return c
        lax.fori_loop(0, tm, body, 0, unroll=8)

    @pl.when(i == 0)
    def _():
        ybuf[...] = jnp.zeros_like(ybuf)
        for k in range(n_dump):
            pltpu.make_async_copy(ybuf.at[1], out_hbm.at[pl.ds((dump0 + k * tm) * SUBLANES, tm * SUBLANES)],
                                  ssem.at[1]).start()
        for k in range(n_dump):
            scatter_wait(1).wait()

    @pl.when(both(i == 0, valid))
    def _():
        src_copy(0, 0).start()
        src_copy(0, 0).wait()
        for_rows(lambda r: gather_row(r, 0).start())
        src_copy(1, 1).start()

    def phase1(par, with_scatter):
        xbf[...] = _tiles_to_rows(xbuf.at[par], tm).astype(_BF)
        n_chunk = -(-wg_ref.shape[1] // tf)
        per = -(-tm // n_chunk)

        def after(c):
            for r in range(c * per, min((c + 1) * per, tm)):
                gather_row(r, 1 - par).start()
                if with_scatter:
                    scatter_row(r, 1 - par).start()

        _swiglu_hidden(xbf[...], wg_ref, wu_ref, a_scr, tf, after)
        _rows_to_tiles(ybuf.at[par], jnp.dot(a_scr[...], wd_ref[...], preferred_element_type=_F32))

    for par in range(2):
        here = slot == par

        @pl.when(both(valid, here))
        def _(par=par):
            dst_copy(i, par).start()
            src_copy(i + 1, 1 - par).wait()

        @pl.when(both(prev_valid, here))
        def _(par=par):
            dst_copy(i - 1, 1 - par).wait()

        @pl.when(both(jnp.logical_or(both(i == 0, valid), prev_valid), here))
        def _(par=par):
            gather_wait(par).wait()

        @pl.when(both(is_valid(i - 2), here))
        def _(par=par):
            scatter_wait(par).wait()

        @pl.when(both(both(valid, prev_valid), here))
        def _(par=par):
            phase1(par, True)

        if par == 0:
            @pl.when(both(both(valid, jnp.logical_not(prev_valid)), here))
            def _():
                phase1(0, False)

        @pl.when(both(both(jnp.logical_not(valid), prev_valid), here))
        def _(par=par):
            for_rows(lambda r: scatter_row(r, 1 - par).start())

        @pl.when(both(is_valid(i + 1), here))
        def _(par=par):
            src_copy(i + 2, par).start()


def _moe(h, tile_e, tile_v, src, dst, wg, wu, wd, n_out_rows, tm, tf):
    T = h.shape[0] // SUBLANES
    D = wg.shape[1]
    nt = tile_e.shape[0]
    F = wg.shape[2]
    dump0 = T * TOP_K
    once = pl.Buffered(1)
    grid_spec = pltpu.PrefetchScalarGridSpec(
        num_scalar_prefetch=2,
        grid=(nt,),
        in_specs=[pl.BlockSpec(memory_space=pl.ANY), pl.BlockSpec(memory_space=pl.ANY),
                  pl.BlockSpec(memory_space=pl.ANY),
                  pl.BlockSpec((None, D, F), lambda i, te, tv: (te[i], 0, 0), pipeline_mode=once),
                  pl.BlockSpec((None, D, F), lambda i, te, tv: (te[i], 0, 0), pipeline_mode=once),
                  pl.BlockSpec((None, F, D), lambda i, te, tv: (te[i], 0, 0), pipeline_mode=once)],
        out_specs=pl.BlockSpec(memory_space=pl.ANY),
        scratch_shapes=[pltpu.VMEM((2, tm * SUBLANES, LANES), _F32), pltpu.VMEM((tm, D), _BF),
                        pltpu.VMEM((tm, F), _BF), pltpu.VMEM((2, tm * SUBLANES, LANES), _F32),
                        pltpu.SMEM((tm,), jnp.int32), pltpu.SMEM((tm,), jnp.int32),
                        pltpu.SMEM((tm,), jnp.int32), pltpu.SMEM((tm,), jnp.int32),
                        pltpu.SemaphoreType.DMA((2,)), pltpu.SemaphoreType.DMA((2,)),
                        pltpu.SemaphoreType.DMA((4,))],
    )
    return pl.pallas_call(
        functools.partial(_moe_kernel, tf=tf, dump0=dump0, n_dump=(n_out_rows - dump0) // tm),
        grid_spec=grid_spec,
        out_shape=jax.ShapeDtypeStruct((n_out_rows * SUBLANES, LANES), _F32),
        compiler_params=_cparams(("arbitrary",)),
        name="moe_experts",
    )(tile_e, tile_v, src, dst, h, wg, wu, wd)


def _route_plan(route, T, tm):
    A = T * TOP_K
    e_flat = route[:, :TOP_K].astype(jnp.int32).reshape(A)
    order = jnp.argsort(e_flat, stable=True).astype(jnp.int32)
    experts = jnp.arange(N_EXPERTS, dtype=jnp.int32)
    counts = jnp.sum((e_flat[None, :] == experts[:, None]).astype(jnp.int32), axis=1)
    starts = jnp.cumsum(counts) - counts
    pcounts = (counts + tm - 1) // tm * tm
    pends = jnp.cumsum(pcounts)
    pstarts = pends - pcounts
    nt = A // tm + N_EXPERTS + 2
    tile0 = jnp.arange(nt, dtype=jnp.int32) * tm
    tile_v = (tile0 < pends[-1]).astype(jnp.int32)
    last_valid = jnp.maximum(pends[-1] // tm - 1, 0)
    tile_e_raw = jnp.minimum(jnp.sum((pends[None, :] <= tile0[:, None]).astype(jnp.int32), axis=1),
                             N_EXPERTS - 1)
    tile_e = jnp.where(tile_v == 1, tile_e_raw, tile_e_raw[last_valid])
    r = jnp.arange(nt * tm, dtype=jnp.int32)
    e_r = jnp.repeat(tile_e, tm)
    within = r - pstarts[e_r]
    ok = jnp.logical_and(within < counts[e_r], jnp.repeat(tile_v, tm) == 1)
    a = order[jnp.clip(starts[e_r] + within, 0, A - 1)]
    src = jnp.where(ok, a // TOP_K, 0).astype(jnp.int32)
    dump = A + e_r * tm + jnp.clip(within - counts[e_r], 0, tm - 1)
    dst = jnp.where(ok, (a % TOP_K) * T + a // TOP_K, dump).astype(jnp.int32)
    return (tile_e, tile_v, (src * SUBLANES).reshape(nt, tm), (dst * SUBLANES).reshape(nt, tm),
            A + N_EXPERTS * tm)


def _combine_kernel(x_ref, y1_ref, y2_ref, r_ref, mod_ref, o_ref):
    tm = x_ref.shape[0]
    y = _tiles_to_rows(y1_ref, tm) * r_ref[:, 2:3] + _tiles_to_rows(y2_ref, tm) * r_ref[:, 3:4]
    o_ref[...] = x_ref[...] + mod_ref[5:6, :] * y


def _combine(x, y2, route, mod, tm):
    B, S, D = x.shape
    T = B * S
    per_b = S // tm
    out = pl.pallas_call(
        _combine_kernel,
        grid=(T // tm,),
        in_specs=[pl.BlockSpec((tm, D), lambda i: (i, 0)),
                  pl.BlockSpec((tm * SUBLANES, LANES), lambda i: (i, 0)),
                  pl.BlockSpec((tm * SUBLANES, LANES), lambda i: (T // tm + i, 0)),
                  pl.BlockSpec((tm, LANES), lambda i: (i, 0)),
                  pl.BlockSpec((None, 6, D), lambda i: (i // per_b, 0, 0))],
        out_specs=pl.BlockSpec((tm, D), lambda i: (i, 0)),
        out_shape=jax.ShapeDtypeStruct((T, D), _F32),
        compiler_params=_cparams(("parallel",)),
        name="moe_combine",
    )(x.reshape(T, D), y2, y2, route.reshape(T, LANES), mod)
    return out.reshape(B, S, D)


def _pad_cols(a, w):
    return jnp.pad(a, ((0, 0), (0, w - a.shape[1])))


def _rope_tables_even(S):
    pos = jnp.arange(S, dtype=jnp.int32)
    inv = ROPE_THETA ** (-jnp.arange(0, 32, 2, dtype=_F32) / 32)
    def cs(p):
        ang = p.astype(_F32)[:, None] * inv[None, :]
        return jnp.cos(ang), jnp.sin(ang)
    one = lambda w: jnp.ones((S, w), _F32)
    zero = lambda w: jnp.zeros((S, w), _F32)
    c, s = cs(pos)
    ca = jnp.concatenate([c, one(48), c, one(48)], 1)
    sa = jnp.concatenate([-s, zero(48), s, zero(48)], 1)
    cr, sr = cs(pos // GRID_W)
    cc, sc = cs(pos % GRID_W)
    cb = jnp.concatenate([cr, cc, one(32), cr, cc, one(32)], 1)
    sb = jnp.concatenate([-sr, -sc, zero(32), sr, sc, zero(32)], 1)
    return jnp.concatenate([ca, sa, cb, sb], 1)


def _slot_maps():
    r = MLA_ROPE // 2
    mla = ([MLA_NOPE + i for i in range(r)] + list(range(0, 64 - r))
           + [MLA_NOPE + r + i for i in range(r)] + list(range(64 - r, MLA_NOPE)))
    mla += [-1] * (LANES - len(mla))
    q = HEAD_DIM // 4
    gqa = (list(range(0, q)) + list(range(2 * q, 3 * q)) + [-1] * (64 - 2 * q)
           + list(range(q, 2 * q)) + list(range(3 * q, 4 * q)) + [-1] * (64 - 2 * q))
    return mla, gqa


def _to_slots(a, lane_map):
    idx = jnp.asarray([max(i, 0) for i in lane_map], jnp.int32)
    keep = jnp.asarray([1.0 if i >= 0 else 0.0 for i in lane_map], a.dtype)
    return jnp.take(a, idx, axis=-1) * keep


def _rope_tables_odd(S):
    pos = jnp.arange(S, dtype=_F32)
    inv = ROPE_THETA ** (-jnp.arange(0, HEAD_DIM, 2, dtype=_F32) / HEAD_DIM)
    ang = pos[:, None] * inv[None, :]
    c, s = jnp.cos(ang), jnp.sin(ang)
    return jnp.concatenate([c, c, c, c, -s, -s, s, s], 1)


def _tiles(S):
    return dict(tm_pre=min(S, 512), tq=min(S, 2048), tm_o=min(S, 512),
                tm_ffn=min(S, 512), tf_ffn=256, tm_qkv=min(S, 512),
                tl=128, tm_moe=min(S, 512), tf_moe=256, tm_c=min(S, 512))


def kernel(x, c, ada_even_w, ada_even_b, norm_even_mix, norm_even_ffn, even_w_in, mla_q_norm, mla_w_uq, mla_kv_norm, mla_w_ukv, mla_q_gain, mla_k_gain, gqa_q_gain, gqa_k_gain, even_w_out, ffn_w_gate, ffn_w_up, ffn_w_down, ada_odd_w, ada_odd_b, norm_odd_mix, norm_odd_ffn, dil_w_qkv, dil_q_gain, dil_k_gain, dil_w_out, moe_router, moe_w_gate, moe_w_up, moe_w_down):
    B, S, D = x.shape
    T = B * S
    cfg = _tiles(S)

    mod_e = _ada_mod(c, ada_even_w[0], ada_even_b[0]).reshape(B, 6, D)
    mod_o = _ada_mod(c, ada_odd_w[0], ada_odd_b[0]).reshape(B, 6, D)

    w = even_w_in[0]
    sp = [MLA_Q_RANK, MLA_Q_RANK + MLA_KV_RANK, MLA_Q_RANK + MLA_KV_RANK + MLA_ROPE]
    sp.append(sp[-1] + GQA_HEADS * HEAD_DIM)
    sp.append(sp[-1] + GQA_KV_HEADS * HEAD_DIM)
    w_cq, w_ckv, w_kpe = w[:, :sp[0]], w[:, sp[0]:sp[1]], w[:, sp[1]:sp[2]]
    w_qb, w_kb, w_vb = w[:, sp[2]:sp[3]], w[:, sp[3]:sp[4]], w[:, sp[4]:]
    mla_map, gqa_map = _slot_maps()
    na = MLA_NOPE + MLA_ROPE
    nope_only = [i if 0 <= i < MLA_NOPE else -1 for i in mla_map]
    rope_only = [i - MLA_NOPE if i >= MLA_NOPE else -1 for i in mla_map]
    gslots = lambda a, n: _to_slots(a.reshape(D, n, HEAD_DIM), gqa_map).reshape(D, n * LANES)
    w_in = jnp.concatenate([w_cq, w_ckv, _to_slots(w_kpe, rope_only), gslots(w_qb, GQA_HEADS),
                            gslots(w_kb, GQA_KV_HEADS)], axis=1).astype(_BF)
    w_uq = _to_slots(mla_w_uq[0].reshape(MLA_Q_RANK, MLA_HEADS, na), mla_map)
    w_uq = w_uq.reshape(MLA_Q_RANK, MLA_HEADS * LANES).astype(_BF)
    w_ukv = mla_w_ukv[0].reshape(MLA_KV_RANK, MLA_HEADS, MLA_NOPE + MLA_V)
    w_uk = _to_slots(w_ukv[:, :, :MLA_NOPE], nope_only).reshape(MLA_KV_RANK, MLA_HEADS * LANES).astype(_BF)
    w_uvt = w_ukv[:, :, MLA_NOPE:].reshape(MLA_KV_RANK, MLA_HEADS * MLA_V).T.astype(_BF)
    w_vbt = w_vb.T.astype(_BF)
    g_rows = [_to_slots(mla_q_gain[0], mla_map) * (na ** -0.5 * LOG2E), _to_slots(mla_k_gain[0], mla_map),
              _to_slots(gqa_q_gain[0], gqa_map) * (HEAD_DIM ** -0.5 * LOG2E), _to_slots(gqa_k_gain[0], gqa_map)]
    bound_a = na * jnp.max(jnp.abs(g_rows[0])) * jnp.max(jnp.abs(g_rows[1]))
    bound_b = HEAD_DIM * jnp.max(jnp.abs(g_rows[2])) * jnp.max(jnp.abs(g_rows[3]))
    fast_a, fast_b = bound_a <= SCORE_BOUND_MAX, bound_b <= SCORE_BOUND_MAX
    last_lane = jnp.zeros((LANES,), _F32).at[LANES - 1].set(1.0)
    g_rows += [last_lane, last_lane * jnp.where(fast_a, -bound_a, 0.0),
               last_lane, last_lane * jnp.where(fast_b, -bound_b, 0.0)]
    gains_e = jnp.stack(g_rows)
    rope_e = _rope_tables_even(S)
    q_all, k_all, vt_all = _pre_even(
        x, mod_e, norm_even_mix[0].reshape(1, D), w_in, mla_q_norm[0].reshape(1, -1), w_uq,
        mla_kv_norm[0].reshape(1, -1), w_uk, w_uvt, w_vbt, gains_e, rope_e, cfg["tm_pre"])
    as_flag = lambda f: f.astype(jnp.int32).reshape(1)
    o_a = _attention(as_flag(fast_a), q_all, k_all, vt_all, q_off=0, k_off=0, shared_kv=False, tq=cfg["tq"])
    o_b = _attention(as_flag(fast_b), q_all, k_all, vt_all, q_off=MLA_HEADS, k_off=MLA_HEADS,
                     shared_kv=True, tq=cfg["tq"])
    x2, h2 = _ffn(x, o_a, o_b, even_w_out[0].astype(_BF), mod_e, norm_even_ffn[0].reshape(1, D), mod_o,
                  norm_odd_mix[0].reshape(1, D), ffn_w_gate[0].astype(_BF), ffn_w_up[0].astype(_BF),
                  ffn_w_down[0].astype(_BF), cfg["tm_ffn"], cfg["tf_ffn"])

    hh = HEAD_DIM // 2
    pair = lambda v: jnp.concatenate([v[:hh], v[:hh], v[hh:], v[hh:]])
    gains_o = jnp.stack([pair(dil_q_gain[0]) * (HEAD_DIM ** -0.5 * LOG2E), pair(dil_k_gain[0])])
    wq, wk, wv = jnp.split(dil_w_qkv[0], 3, axis=1)
    perm = lambda w: w.reshape(D, D // LANES, 2, 2, hh).transpose(0, 1, 3, 2, 4).reshape(D, D)
    w_qkv = jnp.concatenate([perm(wq), perm(wk), wv], axis=1).astype(_BF)
    qd, kd, vd = _qkv(h2, w_qkv, gains_o, _rope_tables_odd(S), cfg["tm_qkv"])
    bound_d = HEAD_DIM * jnp.max(jnp.abs(gains_o[0])) * jnp.max(jnp.abs(gains_o[1]))
    fast_d = bound_d <= SCORE_BOUND_MAX
    o_d = _dilated(as_flag(fast_d), jnp.where(fast_d, -bound_d, 0.0), qd, kd, vd, cfg["tl"])
    r32 = _pad_cols(moe_router[0], LANES)
    r_hi = r32.astype(_BF)
    r_lo = (r32 - r_hi.astype(_F32)).astype(_BF)
    x3, h3, route = _oproj_route(x2, mod_o, norm_odd_ffn[0].reshape(1, D), dil_w_out[0].astype(_BF), o_d,
                                 jnp.stack([r_hi, r_lo]), cfg["tm_o"])
    tm = cfg["tm_moe"]
    tile_e, tile_v, src, dst, n_rows = _route_plan(route.reshape(T, LANES), T, tm)
    y2 = _moe(h3, tile_e, tile_v, src, dst, moe_w_gate[0].astype(_BF),
              moe_w_up[0].astype(_BF), moe_w_down[0].astype(_BF), n_rows, tm, cfg["tf_moe"])
    return _combine(x3, y2, route, mod_o, cfg["tm_c"])
```

```python
import functools
import math

import jax
import jax.numpy as jnp
from jax import lax
from jax.experimental import pallas as pl
from jax.experimental.pallas import tpu as pltpu

_BF = jnp.bfloat16
_F32 = jnp.float32

GRID_W = 64
HEAD_DIM = 64
ROPE_THETA = 10000.0
EPS = 1e-6
MLA_HEADS = 8
MLA_Q_RANK = 256
MLA_KV_RANK = 128
MLA_NOPE = 64
MLA_ROPE = 32
MLA_V = 64
GQA_HEADS = 8
GQA_KV_HEADS = 2
DIL_PATTERNS = ((128, 1), (512, 4), (2048, 16))
N_EXPERTS = 8
TOP_K = 2
NEG = -1e30
LOG2E = math.log2(math.e)

LANES = 128
VMEM_LIMIT = 56 * 1024 * 1024


def _cparams(sem, vmem=VMEM_LIMIT):
    return pltpu.CompilerParams(dimension_semantics=sem, vmem_limit_bytes=vmem)


def _silu(x):
    return x / (1.0 + jnp.exp(-x))


def _modulate(x, g, shift, scale):
    ms = jnp.mean(x * x, axis=-1, keepdims=True)
    return x * lax.rsqrt(ms + EPS) * g * (1.0 + scale) + shift


def _norm(v, n, g):
    return v * lax.rsqrt(jnp.sum(v * v, axis=-1, keepdims=True) * (1.0 / n) + EPS) * g


SUBLANES = 8


def _rows_to_tiles(ref, x, row0=0):
    tm = x.shape[0]
    for c in range(SUBLANES):
        ref[pl.ds(row0 * SUBLANES + c, tm, stride=SUBLANES), :] = x[:, c * LANES:(c + 1) * LANES]


def _tiles_to_rows(ref, tm):
    return jnp.concatenate([ref[pl.ds(c, tm, stride=SUBLANES), :] for c in range(SUBLANES)], axis=1)


def _mod_kernel(c_ref, w_ref, b_ref, o_ref):
    sc = _silu(c_ref[...])
    o_ref[...] = jnp.dot(sc.astype(_BF), w_ref[...].astype(_BF),
                         preferred_element_type=_F32) + b_ref[...]


def _ada_mod(c, w, b):
    B, D = c.shape
    N = w.shape[1]
    tn = min(N, 1536)
    return pl.pallas_call(
        _mod_kernel,
        grid=(N // tn,),
        in_specs=[pl.BlockSpec((B, D), lambda j: (0, 0)),
                  pl.BlockSpec((D, tn), lambda j: (0, j)),
                  pl.BlockSpec((1, tn), lambda j: (0, j))],
        out_specs=pl.BlockSpec((B, tn), lambda j: (0, j)),
        out_shape=jax.ShapeDtypeStruct((B, N), _F32),
        compiler_params=_cparams(("arbitrary",)),
        name="ada_mod",
    )(c, w, b.reshape(1, N))


def _pre_even_kernel(x_ref, mod_ref, g_ref, w_in_ref, qn_ref, w_uq_ref, kvn_ref, w_uk_ref, w_uvt_ref,
                     w_vbt_ref, gains_ref, rope_ref, q_ref, k_ref, vt_ref, nq_scr, nk_scr):
    h = _modulate(x_ref[...], g_ref[...], mod_ref[0:1, :], mod_ref[1:2, :]).astype(_BF)
    y = jnp.dot(h, w_in_ref[...], preferred_element_type=_F32)
    cqn = _norm(y[:, 0:256], MLA_Q_RANK, qn_ref[...]).astype(_BF)
    qa = jnp.dot(cqn, w_uq_ref[...], preferred_element_type=_F32)
    ckvn = _norm(y[:, 256:384], MLA_KV_RANK, kvn_ref[...]).astype(_BF)
    kn = jnp.dot(ckvn, w_uk_ref[...], preferred_element_type=_F32)
    kpe = y[:, 384:512]
    nt_dims = (((1,), (1,)), ((), ()))
    vt_a = lax.dot_general(w_uvt_ref[...], ckvn, nt_dims, preferred_element_type=_F32).astype(_BF)
    vt_b = lax.dot_general(w_vbt_ref[...], h, nt_dims, preferred_element_type=_F32).astype(_BF)
    tm = x_ref.shape[0]
    ones_blk = jnp.where(lax.broadcasted_iota(jnp.int32, (LANES - MLA_V, tm), 0) == 0, 1.0, 0.0).astype(_BF)
    for hh in range(MLA_HEADS + GQA_KV_HEADS):
        src = vt_a[hh * MLA_V:(hh + 1) * MLA_V] if hh < MLA_HEADS else \
            vt_b[(hh - MLA_HEADS) * HEAD_DIM:(hh - MLA_HEADS + 1) * HEAD_DIM]
        vt_ref[hh * LANES:hh * LANES + MLA_V, :] = src
        vt_ref[hh * LANES + MLA_V:(hh + 1) * LANES, :] = ones_blk
    ca, sa = rope_ref[:, 0:128], rope_ref[:, 128:256]
    cb, sb = rope_ref[:, 256:384], rope_ref[:, 384:512]
    gqa, gka = gains_ref[0:1, :], gains_ref[1:2, :]
    gqb, gkb = gains_ref[2:3, :], gains_ref[3:4, :]
    na = MLA_NOPE + MLA_ROPE

    for hh in range(MLA_HEADS):
        sl = slice(hh * LANES, (hh + 1) * LANES)
        nq_scr[:, sl] = _norm(qa[:, sl], na, gqa)
        nk_scr[:, sl] = _norm(kn[:, sl] + kpe, na, gka)
    for hh in range(GQA_HEADS):
        src = slice(512 + hh * LANES, 512 + (hh + 1) * LANES)
        dst = slice((MLA_HEADS + hh) * LANES, (MLA_HEADS + hh + 1) * LANES)
        nq_scr[:, dst] = _norm(y[:, src], HEAD_DIM, gqb)
    for g in range(GQA_KV_HEADS):
        src = slice(1536 + g * LANES, 1536 + (g + 1) * LANES)
        dst = slice((MLA_HEADS + g) * LANES, (MLA_HEADS + g + 1) * LANES)
        nk_scr[:, dst] = _norm(y[:, src], HEAD_DIM, gkb)

    @pl.when(pl.program_id(1) >= 0)
    def _():
        def rope(ref, out, n_a, n_all, row):
            for hh in range(n_all):
                sl = slice(hh * LANES, (hh + 1) * LANES)
                c, s = (ca, sa) if hh < n_a else (cb, sb)
                b = gains_ref[row:row + 1, :] if hh < n_a else gains_ref[row + 2:row + 3, :]
                v = ref[:, sl]
                out[:, sl] = (v * c + pltpu.roll(v, 64, 1) * s + b).astype(_BF)

        rope(nq_scr, q_ref, MLA_HEADS, MLA_HEADS + GQA_HEADS, 4)
        rope(nk_scr, k_ref, MLA_HEADS, MLA_HEADS + GQA_KV_HEADS, 5)


def _pre_even(x, mod, g, w_in, qn, w_uq, kvn, w_uk, w_uvt, w_vbt, gains, rope, tm):
    B, S, D = x.shape
    nq = (MLA_HEADS + GQA_HEADS) * LANES
    nk = (MLA_HEADS + GQA_KV_HEADS) * LANES
    nvt = (MLA_HEADS + GQA_KV_HEADS) * LANES
    full = lambda a: pl.BlockSpec(a.shape, lambda b, i: (0,) * a.ndim)
    tok = lambda w: pl.BlockSpec((None, tm, w), lambda b, i: (b, i, 0))
    return pl.pallas_call(
        _pre_even_kernel,
        grid=(B, S // tm),
        in_specs=[tok(D),
                  pl.BlockSpec((None, 6, D), lambda b, i: (b, 0, 0)),
                  full(g), full(w_in), full(qn), full(w_uq), full(kvn), full(w_uk), full(w_uvt),
                  full(w_vbt), full(gains),
                  pl.BlockSpec((tm, rope.shape[1]), lambda b, i: (i, 0))],
        out_specs=[tok(nq), tok(nk), pl.BlockSpec((None, None, nvt, tm), lambda b, i: (b, i, 0, 0))],
        out_shape=[jax.ShapeDtypeStruct((B, S, nq), _BF), jax.ShapeDtypeStruct((B, S, nk), _BF),
                   jax.ShapeDtypeStruct((B, S // tm, nvt, tm), _BF)],
        scratch_shapes=[pltpu.VMEM((tm, nq), _F32), pltpu.VMEM((tm, nk), _F32)],
        compiler_params=_cparams(("parallel", "parallel")),
        name="pre_even",
    )(x, mod, g, w_in, qn, w_uq, kvn, w_uk, w_uvt, w_vbt, gains, rope)


SCORE_BOUND_MAX = 60.0


def _attn_kernel(fast_ref, q_ref, k_ref, vt_ref, o_ref, s_scr, *, shared_kv):
    @pl.when(fast_ref[0] == 1)
    def _():
        _attn_fixed_shift(q_ref, k_ref, vt_ref, o_ref, shared_kv=shared_kv)

    @pl.when(fast_ref[0] != 1)
    def _():
        _attn_running_max(q_ref, k_ref, vt_ref, o_ref, s_scr, shared_kv=shared_kv)


def _attn_fixed_shift(q_ref, k_ref, vt_ref, o_ref, *, shared_kv):
    tq = q_ref.shape[0]
    nk, _, tk = vt_ref.shape
    qs = [q_ref[:, hh * LANES:(hh + 1) * LANES] for hh in range(2)]

    def body(j, carry):
        r0 = pl.multiple_of(j * tk, tk)
        out = []
        for hh in range(2):
            kc = 0 if shared_kv else hh * LANES
            k = k_ref[pl.ds(r0, tk), kc:kc + LANES]
            st = lax.dot_general(k, qs[hh], (((1,), (1,)), ((), ())), preferred_element_type=_F32)
            p = jnp.exp2(st).astype(_BF)
            out.append(carry[hh] + jnp.dot(vt_ref[j, kc:kc + LANES, :], p, preferred_element_type=_F32))
        return tuple(out)

    z = jnp.zeros((LANES, tq), _F32)
    res = lax.fori_loop(0, nk, body, (z, z), unroll=True)
    nv = HEAD_DIM
    halves = [acc[0:nv, :] / acc[nv:nv + 1, :] for acc in res]
    o_ref[...] = jnp.concatenate(halves, axis=0).T.astype(o_ref.dtype)


def _attn_running_max(q_ref, k_ref, vt_ref, o_ref, s_scr, *, shared_kv):
    tq = q_ref.shape[0]
    nk, _, tk = vt_ref.shape
    qs = [q_ref[:, hh * LANES:(hh + 1) * LANES] for hh in range(2)]

    def scores(j, hh):
        r0 = j * tk if isinstance(j, int) else pl.multiple_of(j * tk, tk)
        kc = 0 if shared_kv else hh * LANES
        k = k_ref[pl.ds(r0, tk), kc:kc + LANES]
        return lax.dot_general(k, qs[hh], (((1,), (1,)), ((), ())), preferred_element_type=_F32)

    def put_scores(j, hh, slot):
        st = scores(j, hh)
        s_scr[hh, slot] = st
        return jnp.max(st, axis=0, keepdims=True)

    def step(j, hh, carry, cur, last=False):
        m, mc, acc = carry
        mc_next = mc if last else put_scores(j + 1, hh, 1 - cur)
        st = s_scr[hh, cur]
        m_new = jnp.maximum(m, mc)
        alpha = jnp.exp2(m - m_new)
        p = jnp.exp2(st - m_new).astype(_BF)
        vc = 0 if shared_kv else hh * LANES
        acc = alpha * acc + jnp.dot(vt_ref[j, vc:vc + LANES, :], p, preferred_element_type=_F32)
        return m_new, mc_next, acc

    def body(jj, carry):
        c = list(carry)
        for sub in range(2):
            for hh in range(2):
                c[hh] = step(2 * jj + sub, hh, c[hh], sub)
        return tuple(c)

    init = tuple((jnp.full((1, tq), NEG, _F32), put_scores(0, hh, 0), jnp.zeros((LANES, tq), _F32))
                 for hh in range(2))
    res = lax.fori_loop(0, nk // 2 - 1, body, init)
    res = list(res)
    for sub in range(2):
        for hh in range(2):
            res[hh] = step(nk - 2 + sub, hh, res[hh], sub, last=(sub == 1))
    nv = HEAD_DIM
    halves = [acc[0:nv, :] / acc[nv:nv + 1, :] for (_, _, acc) in res]
    o_ref[...] = jnp.concatenate(halves, axis=0).T.astype(o_ref.dtype)


def _attention(fast, q, k, vt, *, q_off, k_off, shared_kv, tq):
    B, S, _ = q.shape
    nk, _, tk = vt.shape[1:]
    assert nk % 2 == 0 and nk >= 2
    npairs = 4
    if shared_kv:
        kspec = pl.BlockSpec((None, S, LANES), lambda b, p, i, f: (b, 0, k_off + p // 2))
        vspec = pl.BlockSpec((None, nk, LANES, tk), lambda b, p, i, f: (b, 0, k_off + p // 2, 0))
    else:
        kspec = pl.BlockSpec((None, S, 2 * LANES), lambda b, p, i, f: (b, 0, k_off // 2 + p))
        vspec = pl.BlockSpec((None, nk, 2 * LANES, tk), lambda b, p, i, f: (b, 0, k_off // 2 + p, 0))
    grid_spec = pltpu.PrefetchScalarGridSpec(
        num_scalar_prefetch=1,
        grid=(B, npairs, S // tq),
        in_specs=[pl.BlockSpec((None, tq, 2 * LANES), lambda b, p, i, f: (b, i, q_off // 2 + p)),
                  kspec, vspec],
        out_specs=pl.BlockSpec((None, tq, LANES), lambda b, p, i, f: (b, i, p)),
        scratch_shapes=[pltpu.VMEM((2, 2, tk, tq), _F32)],
    )
    return pl.pallas_call(
        functools.partial(_attn_kernel, shared_kv=shared_kv),
        grid_spec=grid_spec,
        out_shape=jax.ShapeDtypeStruct((B, S, npairs * LANES), _BF),
        compiler_params=_cparams(("parallel", "parallel", "parallel")),
        name="attn_gqa" if shared_kv else "attn_mla",
    )(fast, q, k, vt)


def _oproj_route_kernel(x_ref, mod_ref, g_ref, w_ref, o_ref, r_ref, x_out, h_out, route_out, *, sub):
    for r0 in range(0, x_ref.shape[0], sub):
        rows = slice(r0, r0 + sub)
        o = jnp.concatenate([o_ref[p, rows, :] for p in range(o_ref.shape[0])], axis=1)
        y = jnp.dot(o, w_ref[...], preferred_element_type=_F32)
        x1 = x_ref[rows, :] + mod_ref[2:3, :] * y
        x_out[rows, :] = x1
        h = _modulate(x1, g_ref[...], mod_ref[3:4, :], mod_ref[4:5, :])
        _rows_to_tiles(h_out, h, r0)
        hh = h.astype(_BF)
        hl = (h - hh.astype(_F32)).astype(_BF)
        rh, rl = r_ref[0], r_ref[1]
        logits = (jnp.dot(hh, rh, preferred_element_type=_F32)
                  + jnp.dot(hh, rl, preferred_element_type=_F32)
                  + jnp.dot(hl, rh, preferred_element_type=_F32))
        tm = logits.shape[0]
        lane = lax.broadcasted_iota(jnp.int32, (tm, LANES), 1)
        lg = jnp.where(lane < N_EXPERTS, logits, NEG)
        m1 = jnp.max(lg, axis=-1, keepdims=True)
        lanef = lane.astype(_F32)
        i1 = jnp.min(jnp.where(lg == m1, lanef, float(LANES)), axis=-1, keepdims=True)
        lg2 = jnp.where(lanef == i1, NEG, lg)
        m2 = jnp.max(lg2, axis=-1, keepdims=True)
        i2 = jnp.min(jnp.where(lg2 == m2, lanef, float(LANES)), axis=-1, keepdims=True)
        e = jnp.exp(m2 - m1)
        g1 = 1.0 / (1.0 + e)
        g2 = e / (1.0 + e)
        route_out[rows, :] = jnp.where(lane == 0, i1,
                                   jnp.where(lane == 1, i2,
                                             jnp.where(lane == 2, g1, jnp.where(lane == 3, g2, 0.0))))


def _oproj_route(x, mod, g, w, o, router, tm):
    B, S, D = x.shape
    assert D == SUBLANES * LANES
    per_b = S // tm
    tok = lambda wd: pl.BlockSpec((None, tm, wd), lambda b, i: (b, i, 0))
    return pl.pallas_call(
        functools.partial(_oproj_route_kernel, sub=min(tm, 256)),
        grid=(B, S // tm),
        in_specs=[tok(D), pl.BlockSpec((None, 6, D), lambda b, i: (b, 0, 0)),
                  pl.BlockSpec((1, D), lambda b, i: (0, 0)),
                  pl.BlockSpec(w.shape, lambda b, i: (0, 0)),
                  pl.BlockSpec((None, o.shape[1], tm, LANES), lambda b, i: (b, 0, i, 0)),
                  pl.BlockSpec(router.shape, lambda b, i: (0, 0, 0))],
        out_specs=[tok(D), pl.BlockSpec((tm * SUBLANES, LANES), lambda b, i: (b * per_b + i, 0)), tok(LANES)],
        out_shape=[jax.ShapeDtypeStruct((B, S, D), _F32),
                   jax.ShapeDtypeStruct((B * S * SUBLANES, LANES), _F32),
                   jax.ShapeDtypeStruct((B, S, LANES), _F32)],
        compiler_params=_cparams(("parallel", "parallel")),
        name="oproj_route",
    )(x, mod, g, w, o, router)


def _swiglu_hidden(x, wg_ref, wu_ref, a_scr, tf, after_chunk=None):
    F = wg_ref.shape[1]
    for c0 in range(0, F, tf):
        c1 = min(c0 + tf, F)
        g = jnp.dot(x, wg_ref[:, c0:c1], preferred_element_type=_F32)
        u = jnp.dot(x, wu_ref[:, c0:c1], preferred_element_type=_F32)
        a_scr[:, c0:c1] = (_silu(g) * u).astype(_BF)
        if after_chunk is not None:
            after_chunk(c0 // tf)


def _ffn_kernel(x_ref, oa_ref, ob_ref, wo_ref, mod_ref, g_ref, modn_ref, gn_ref, wg_ref, wu_ref, wd_ref,
                x_out, h_out, a_scr, *, tf):
    wa = oa_ref.shape[1]
    y = (jnp.dot(oa_ref[...], wo_ref[0:wa, :], preferred_element_type=_F32)
         + jnp.dot(ob_ref[...], wo_ref[wa:, :], preferred_element_type=_F32))
    x1 = x_ref[...] + mod_ref[2:3, :] * y
    h = _modulate(x1, g_ref[...], mod_ref[3:4, :], mod_ref[4:5, :]).astype(_BF)
    _swiglu_hidden(h, wg_ref, wu_ref, a_scr, tf)
    y = jnp.dot(a_scr[...], wd_ref[...], preferred_element_type=_F32)
    x2 = x1 + mod_ref[5:6, :] * y
    x_out[...] = x2
    h_out[...] = _modulate(x2, gn_ref[...], modn_ref[0:1, :], modn_ref[1:2, :]).astype(h_out.dtype)


def _ffn(x, o_a, o_b, w_out, mod, g, modn, gn, wg, wu, wd, tm, tf):
    B, S, D = x.shape
    T = B * S
    F = wg.shape[1]
    per_b = S // tm
    flat = lambda a: a.reshape(T, a.shape[2])
    tok = lambda w: pl.BlockSpec((tm, w), lambda i: (i, 0))
    modspec = pl.BlockSpec((None, 6, D), lambda i: (i // per_b, 0, 0))
    row = pl.BlockSpec((1, D), lambda i: (0, 0))
    once = pl.Buffered(1)
    res = lambda a: pl.BlockSpec(a.shape, lambda i: (0, 0), pipeline_mode=once)
    xo, ho = pl.pallas_call(
        functools.partial(_ffn_kernel, tf=tf),
        grid=(T // tm,),
        in_specs=[tok(D), tok(o_a.shape[2]), tok(o_b.shape[2]), res(w_out), modspec, row, modspec, row,
                  res(wg), res(wu), res(wd)],
        out_specs=[tok(D), tok(D)],
        out_shape=[jax.ShapeDtypeStruct((T, D), _F32), jax.ShapeDtypeStruct((T, D), _BF)],
        scratch_shapes=[pltpu.VMEM((tm, F), _BF)],
        compiler_params=_cparams(("parallel",)),
        name="ffn_dense",
    )(flat(x), flat(o_a), flat(o_b), w_out, mod, g, modn, gn, wg, wu, wd)
    return xo.reshape(B, S, D), ho.reshape(B, S, D)


def _qkv_kernel(h_ref, w_ref, gains_ref, rope_ref, q_ref, k_ref, v_ref, n_scr):
    tm, D = h_ref.shape
    h = h_ref[...]
    lane = lax.broadcasted_iota(jnp.int32, (tm, LANES), 1)
    head_a = (lane & 63) < 32

    def head_pair_norm(v, g):
        sq = v * v
        s_a = jnp.sum(jnp.where(head_a, sq, 0.0), axis=-1, keepdims=True)
        s_b = jnp.sum(jnp.where(head_a, 0.0, sq), axis=-1, keepdims=True)
        r = lax.rsqrt(jnp.where(head_a, s_a, s_b) * (1.0 / HEAD_DIM) + EPS)
        return v * r * g

    y = jnp.dot(h, w_ref[...], preferred_element_type=_F32)
    for p in range(D // LANES):
        sl = slice(p * LANES, (p + 1) * LANES)
        n_scr[:, sl] = head_pair_norm(y[:, sl], gains_ref[0:1, :])
        n_scr[:, D + p * LANES:D + (p + 1) * LANES] = head_pair_norm(
            y[:, D + p * LANES:D + (p + 1) * LANES], gains_ref[1:2, :])
        v_ref[p] = y[:, 2 * D + p * LANES:2 * D + (p + 1) * LANES].astype(_BF)

    @pl.when(pl.program_id(1) >= 0)
    def _():
        c, s = rope_ref[:, 0:128], rope_ref[:, 128:256]
        for p in range(D // LANES):
            vq = n_scr[:, p * LANES:(p + 1) * LANES]
            vk = n_scr[:, D + p * LANES:D + (p + 1) * LANES]
            q_ref[p] = (vq * c + pltpu.roll(vq, 64, 1) * s).astype(_BF)
            k_ref[p] = (vk * c + pltpu.roll(vk, 64, 1) * s).astype(_BF)


def _qkv(h, w, gains, rope, tm):
    B, S, D = h.shape
    tok = pl.BlockSpec((None, tm, D), lambda b, i: (b, i, 0))
    hp = pl.BlockSpec((None, D // LANES, tm, LANES), lambda b, i: (b, 0, i, 0))
    sds = jax.ShapeDtypeStruct((B, D // LANES, S, LANES), _BF)
    return pl.pallas_call(
        _qkv_kernel,
        grid=(B, S // tm),
        in_specs=[tok, pl.BlockSpec(w.shape, lambda b, i: (0, 0)),
                  pl.BlockSpec((2, LANES), lambda b, i: (0, 0)),
                  pl.BlockSpec((tm, 2 * LANES), lambda b, i: (i, 0))],
        out_specs=[hp, hp, hp],
        out_shape=[sds, sds, sds],
        scratch_shapes=[pltpu.VMEM((tm, 2 * D), _F32)],
        compiler_params=_cparams(("parallel", "parallel")),
        name="qkv_dil",
    )(h, w, gains, rope)


DIL_GROUP = 4


def _dil_kernel(fast_ref, q_ref, k_ref, v_ref, shift_ref, o_ref, tmp, qp, kp, vp, acc, mm, ll, accp, mmp,
                llp, bias_scr, *, tl):
    @pl.when(fast_ref[0] == 1)
    def _():
        _dil_body(q_ref, k_ref, v_ref, shift_ref, o_ref, tmp, qp, kp, vp, acc, mm, ll, accp, mmp, llp,
                  bias_scr, tl=tl, fixed_shift=True)

    @pl.when(fast_ref[0] != 1)
    def _():
        _dil_body(q_ref, k_ref, v_ref, shift_ref, o_ref, tmp, qp, kp, vp, acc, mm, ll, accp, mmp, llp,
                  bias_scr, tl=tl, fixed_shift=False)


def _dil_body(q_ref, k_ref, v_ref, shift_ref, o_ref, tmp, qp, kp, vp, acc, mm, ll, accp, mmp, llp,
              bias_scr, *, tl, fixed_shift):
    S = q_ref.shape[0]
    ng = S // DIL_GROUP
    pitch = DIL_GROUP + 1

    def spread(dst, src):
        for j in range(DIL_GROUP):
            dst[pl.ds(j, ng, stride=pitch), :] = src[pl.ds(j, ng, stride=DIL_GROUP), :]

    for src, dst in ((q_ref, qp), (k_ref, kp), (v_ref, vp)):
        tmp[...] = src[...].astype(_F32)
        spread(dst, tmp)
    lane = lax.broadcasted_iota(jnp.int32, (tl, LANES), 1)
    lo = lane < 64
    head_a = (lane & 63) < 32
    for bi, (window, d) in enumerate(DIL_PATTERNS):
        L = S // d
        half = window // (2 * d)
        t = min(tl, L)
        W = min(t + 2 * half, L)
        nt = L // t
        if d > 1:
            assert d % DIL_GROUP == 0
        sd = d * pitch // DIL_GROUP

        def rows(r, first, n, d=d, sd=sd):
            if d == 1:
                return pl.ds(first, n)
            return pl.ds(r + r // DIL_GROUP + sd * first, n, stride=sd)

        assert t % half == 0 and W <= t + 2 * half
        qa = lax.broadcasted_iota(jnp.int32, (t, W), 0)
        kb = lax.broadcasted_iota(jnp.int32, (t, W), 1)
        inside = shift_ref[0:1, 0:W] if fixed_shift else 0.0
        for case in range(3):
            ok = jnp.abs(kb - qa - case * half) <= half
            bias_scr[case, 0:t, 0:W] = jnp.where(ok, inside, NEG)

        def body(idx, carry, bi=bi, d=d, L=L, half=half, t=t, W=W, nt=nt, rows=rows):
            r = idx // nt
            l0 = (idx % nt) * t
            start = jnp.clip(l0 - half, 0, L - W)
            if d == 1:
                qt = q_ref[pl.ds(pl.multiple_of(l0, t), t), :]
                kw = k_ref[pl.ds(pl.multiple_of(start, 64), W), :]
                vw = v_ref[pl.ds(pl.multiple_of(start, 64), W), :]
            else:
                qt = qp[rows(r, l0, t), :].astype(_BF)
                kw = kp[rows(r, start, W), :].astype(_BF)
                vw = vp[rows(r, start, W), :].astype(_BF)
            lo_t = lo[:t]
            qa_t = head_a[:t]
            zq = jnp.zeros_like(qt)
            q2 = jnp.concatenate([jnp.where(qa_t, qt, zq), jnp.where(qa_t, zq, qt)], axis=0)
            s = lax.dot_general(q2, kw, (((1,), (1,)), ((), ())), preferred_element_type=_F32)
            bias = bias_scr[(l0 - start) // half, 0:t, 0:W]
            s = s + jnp.concatenate([bias, bias], axis=0)
            if fixed_shift:
                p = jnp.exp2(s)
            else:
                m = jnp.max(s, axis=-1, keepdims=True)
                p = jnp.exp2(s - m)
                m_new = jnp.where(lo_t, m[:t], m[t:])
            den = jnp.sum(p, axis=-1, keepdims=True)
            o2 = jnp.dot(p.astype(_BF), vw, preferred_element_type=_F32)
            o_new = jnp.where(lo_t, o2[:t], o2[t:])
            l_new = jnp.where(lo_t, den[:t], den[t:])
            tok = rows(r, l0, t)
            if d == 1:
                assert bi == 0
                acc[tok, :] = o_new
                ll[tok, :] = l_new
                if not fixed_shift:
                    mm[tok, :] = m_new
            elif fixed_shift:
                accp[tok, :] = accp[tok, :] + o_new
                llp[tok, :] = llp[tok, :] + l_new
            else:
                m_old = mmp[tok, :]
                mx = jnp.maximum(m_old, m_new)
                a_old = jnp.exp2(m_old - mx)
                a_new = jnp.exp2(m_new - mx)
                accp[tok, :] = accp[tok, :] * a_old + o_new * a_new
                llp[tok, :] = llp[tok, :] * a_old + l_new * a_new
                mmp[tok, :] = mx
            return carry

        lax.fori_loop(0, d * nt, body, 0, unroll=min(32 if fixed_shift else 8, d * nt))
        if bi == 0:
            spread(accp, acc)
            spread(llp, ll)
            if not fixed_shift:
                spread(mmp, mm)
    for j in range(DIL_GROUP):
        grp = pl.ds(j, ng, stride=pitch)
        tmp[pl.ds(j, ng, stride=DIL_GROUP), :] = accp[grp, :] / llp[grp, :]
    o_ref[...] = tmp[...].astype(o_ref.dtype)


def _dilated(fast, shift, q, k, v, tl):
    B, P, S, _ = q.shape
    wmax = tl + max(w // d for w, d in DIL_PATTERNS)
    spec = pl.BlockSpec((None, None, S, LANES), lambda b, p, f: (b, p, 0, 0))
    scr = pltpu.VMEM((S, LANES), _F32)
    scrp = pltpu.VMEM((S // DIL_GROUP * (DIL_GROUP + 1), LANES), _F32)
    grid_spec = pltpu.PrefetchScalarGridSpec(
        num_scalar_prefetch=1,
        grid=(B, P),
        in_specs=[spec, spec, spec, pl.BlockSpec((1, wmax), lambda b, p, f: (0, 0))],
        out_specs=spec,
        scratch_shapes=[scr, scrp, scrp, scrp, scr, scr, scr, scrp, scrp, scrp,
                        pltpu.VMEM((3, tl, wmax), _F32)],
    )
    return pl.pallas_call(
        functools.partial(_dil_kernel, tl=tl),
        grid_spec=grid_spec,
        out_shape=jax.ShapeDtypeStruct((B, P, S, LANES), _BF),
        compiler_params=_cparams(("parallel", "parallel")),
        name="dil_attn",
    )(fast, q, k, v, jnp.full((1, wmax), shift, _F32))


def _moe_kernel(te_ref, tv_ref, src_hbm, dst_hbm, h_hbm, wg_ref, wu_ref, wd_ref, out_hbm,
                xbuf, xbf, a_scr, ybuf, src_s0, src_s1, dst_s0, dst_s1, gsem, ssem, isem,
                *, tf, dump0, n_dump):
    i = pl.program_id(0)
    nt = pl.num_programs(0)
    tm = xbf.shape[0]
    slot = i % 2

    def is_valid(t):
        return jnp.logical_and(jnp.logical_and(t >= 0, t < nt), tv_ref[jnp.clip(t, 0, nt - 1)] == 1)

    def both(a, b):
        return jnp.logical_and(a, b)

    valid = is_valid(i)
    prev_valid = is_valid(i - 1)

    src_bufs = (src_s0, src_s1)
    dst_bufs = (dst_s0, dst_s1)

    def src_copy(tile, par):
        return pltpu.make_async_copy(src_hbm.at[tile], src_bufs[par], isem.at[par])

    def dst_copy(tile, par):
        return pltpu.make_async_copy(dst_hbm.at[tile], dst_bufs[par], isem.at[2 + par])

    def tile_at(first):
        return pl.ds(pl.multiple_of(first, SUBLANES), SUBLANES)

    def gather_row(r, par):
        return pltpu.make_async_copy(h_hbm.at[tile_at(src_bufs[par][r])],
                                     xbuf.at[par, tile_at(r * SUBLANES)], gsem.at[par])

    def scatter_row(r, par):
        return pltpu.make_async_copy(ybuf.at[par, tile_at(r * SUBLANES)],
                                     out_hbm.at[tile_at(dst_bufs[par][r])], ssem.at[par])

    def gather_wait(par):
        return pltpu.make_async_copy(h_hbm.at[pl.ds(0, tm * SUBLANES)], xbuf.at[par], gsem.at[par])

    def scatter_wait(par):
        return pltpu.make_async_copy(ybuf.at[par], out_hbm.at[pl.ds(0, tm * SUBLANES)], ssem.at[par])

    def for_rows(fn):
        def body(r, c):
            fn(r)
            return c
        lax.fori_loop(0, tm, body, 0, unroll=8)

    @pl.when(i == 0)
    def _():
        ybuf[...] = jnp.zeros_like(ybuf)
        for k in range(n_dump):
            pltpu.make_async_copy(ybuf.at[1], out_hbm.at[pl.ds((dump0 + k * tm) * SUBLANES, tm * SUBLANES)],
                                  ssem.at[1]).start()
        for k in range(n_dump):
            scatter_wait(1).wait()

    @pl.when(both(i == 0, valid))
    def _():
        src_copy(0, 0).start()
        src_copy(0, 0).wait()
        for_rows(lambda r: gather_row(r, 0).start())
        src_copy(1, 1).start()

    def phase1(par, with_scatter):
        xbf[...] = _tiles_to_rows(xbuf.at[par], tm).astype(_BF)
        n_chunk = -(-wg_ref.shape[1] // tf)
        per = -(-tm // n_chunk)

        def after(c):
            for r in range(c * per, min((c + 1) * per, tm)):
                gather_row(r, 1 - par).start()
                if with_scatter:
                    scatter_row(r, 1 - par).start()

        _swiglu_hidden(xbf[...], wg_ref, wu_ref, a_scr, tf, after)
        _rows_to_tiles(ybuf.at[par], jnp.dot(a_scr[...], wd_ref[...], preferred_element_type=_F32))

    for par in range(2):
        here = slot == par

        @pl.when(both(valid, here))
        def _(par=par):
            dst_copy(i, par).start()
            src_copy(i + 1, 1 - par).wait()

        @pl.when(both(prev_valid, here))
        def _(par=par):
            dst_copy(i - 1, 1 - par).wait()

        @pl.when(both(jnp.logical_or(both(i == 0, valid), prev_valid), here))
        def _(par=par):
            gather_wait(par).wait()

        @pl.when(both(is_valid(i - 2), here))
        def _(par=par):
            scatter_wait(par).wait()

        @pl.when(both(both(valid, prev_valid), here))
        def _(par=par):
            phase1(par, True)

        if par == 0:
            @pl.when(both(both(valid, jnp.logical_not(prev_valid)), here))
            def _():
                phase1(0, False)

        @pl.when(both(both(jnp.logical_not(valid), prev_valid), here))
        def _(par=par):
            for_rows(lambda r: scatter_row(r, 1 - par).start())

        @pl.when(both(is_valid(i + 1), here))
        def _(par=par):
            src_copy(i + 2, par).start()


def _moe(h, tile_e, tile_v, src, dst, wg, wu, wd, n_out_rows, tm, tf):
    T = h.shape[0] // SUBLANES
    D = wg.shape[1]
    nt = tile_e.shape[0]
    F = wg.shape[2]
    dump0 = T * TOP_K
    once = pl.Buffered(1)
    grid_spec = pltpu.PrefetchScalarGridSpec(
        num_scalar_prefetch=2,
        grid=(nt,),
        in_specs=[pl.BlockSpec(memory_space=pl.ANY), pl.BlockSpec(memory_space=pl.ANY),
                  pl.BlockSpec(memory_space=pl.ANY),
                  pl.BlockSpec((None, D, F), lambda i, te, tv: (te[i], 0, 0), pipeline_mode=once),
                  pl.BlockSpec((None, D, F), lambda i, te, tv: (te[i], 0, 0), pipeline_mode=once),
                  pl.BlockSpec((None, F, D), lambda i, te, tv: (te[i], 0, 0), pipeline_mode=once)],
        out_specs=pl.BlockSpec(memory_space=pl.ANY),
        scratch_shapes=[pltpu.VMEM((2, tm * SUBLANES, LANES), _F32), pltpu.VMEM((tm, D), _BF),
                        pltpu.VMEM((tm, F), _BF), pltpu.VMEM((2, tm * SUBLANES, LANES), _F32),
                        pltpu.SMEM((tm,), jnp.int32), pltpu.SMEM((tm,), jnp.int32),
                        pltpu.SMEM((tm,), jnp.int32), pltpu.SMEM((tm,), jnp.int32),
                        pltpu.SemaphoreType.DMA((2,)), pltpu.SemaphoreType.DMA((2,)),
                        pltpu.SemaphoreType.DMA((4,))],
    )
    return pl.pallas_call(
        functools.partial(_moe_kernel, tf=tf, dump0=dump0, n_dump=(n_out_rows - dump0) // tm),
        grid_spec=grid_spec,
        out_shape=jax.ShapeDtypeStruct((n_out_rows * SUBLANES, LANES), _F32),
        compiler_params=_cparams(("arbitrary",)),
        name="moe_experts",
    )(tile_e, tile_v, src, dst, h, wg, wu, wd)


def _route_plan(route, T, tm):
    A = T * TOP_K
    e_flat = route[:, :TOP_K].astype(jnp.int32).reshape(A)
    order = jnp.argsort(e_flat, stable=True).astype(jnp.int32)
    experts = jnp.arange(N_EXPERTS, dtype=jnp.int32)
    counts = jnp.sum((e_flat[None, :] == experts[:, None]).astype(jnp.int32), axis=1)
    starts = jnp.cumsum(counts) - counts
    pcounts = (counts + tm - 1) // tm * tm
    pends = jnp.cumsum(pcounts)
    pstarts = pends - pcounts
    nt = A // tm + N_EXPERTS + 2
    tile0 = jnp.arange(nt, dtype=jnp.int32) * tm
    tile_v = (tile0 < pends[-1]).astype(jnp.int32)
    last_valid = jnp.maximum(pends[-1] // tm - 1, 0)
    tile_e_raw = jnp.minimum(jnp.sum((pends[None, :] <= tile0[:, None]).astype(jnp.int32), axis=1),
                             N_EXPERTS - 1)
    tile_e = jnp.where(tile_v == 1, tile_e_raw, tile_e_raw[last_valid])
    r = jnp.arange(nt * tm, dtype=jnp.int32)
    e_r = jnp.repeat(tile_e, tm)
    within = r - pstarts[e_r]
    ok = jnp.logical_and(within < counts[e_r], jnp.repeat(tile_v, tm) == 1)
    a = order[jnp.clip(starts[e_r] + within, 0, A - 1)]
    src = jnp.where(ok, a // TOP_K, 0).astype(jnp.int32)
    dump = A + e_r * tm + jnp.clip(within - counts[e_r], 0, tm - 1)
    dst = jnp.where(ok, (a % TOP_K) * T + a // TOP_K, dump).astype(jnp.int32)
    return (tile_e, tile_v, (src * SUBLANES).reshape(nt, tm), (dst * SUBLANES).reshape(nt, tm),
            A + N_EXPERTS * tm)


def _combine_kernel(x_ref, y1_ref, y2_ref, r_ref, mod_ref, o_ref):
    tm = x_ref.shape[0]
    y = _tiles_to_rows(y1_ref, tm) * r_ref[:, 2:3] + _tiles_to_rows(y2_ref, tm) * r_ref[:, 3:4]
    o_ref[...] = x_ref[...] + mod_ref[5:6, :] * y


def _combine(x, y2, route, mod, tm):
    B, S, D = x.shape
    T = B * S
    per_b = S // tm
    out = pl.pallas_call(
        _combine_kernel,
        grid=(T // tm,),
        in_specs=[pl.BlockSpec((tm, D), lambda i: (i, 0)),
                  pl.BlockSpec((tm * SUBLANES, LANES), lambda i: (i, 0)),
                  pl.BlockSpec((tm * SUBLANES, LANES), lambda i: (T // tm + i, 0)),
                  pl.BlockSpec((tm, LANES), lambda i: (i, 0)),
                  pl.BlockSpec((None, 6, D), lambda i: (i // per_b, 0, 0))],
        out_specs=pl.BlockSpec((tm, D), lambda i: (i, 0)),
        out_shape=jax.ShapeDtypeStruct((T, D), _F32),
        compiler_params=_cparams(("parallel",)),
        name="moe_combine",
    )(x.reshape(T, D), y2, y2, route.reshape(T, LANES), mod)
    return out.reshape(B, S, D)


def _pad_cols(a, w):
    return jnp.pad(a, ((0, 0), (0, w - a.shape[1])))


def _rope_tables_even(S):
    pos = jnp.arange(S, dtype=jnp.int32)
    inv = ROPE_THETA ** (-jnp.arange(0, 32, 2, dtype=_F32) / 32)
    def cs(p):
        ang = p.astype(_F32)[:, None] * inv[None, :]
        return jnp.cos(ang), jnp.sin(ang)
    one = lambda w: jnp.ones((S, w), _F32)
    zero = lambda w: jnp.zeros((S, w), _F32)
    c, s = cs(pos)
    ca = jnp.concatenate([c, one(48), c, one(48)], 1)
    sa = jnp.concatenate([-s, zero(48), s, zero(48)], 1)
    cr, sr = cs(pos // GRID_W)
    cc, sc = cs(pos % GRID_W)
    cb = jnp.concatenate([cr, cc, one(32), cr, cc, one(32)], 1)
    sb = jnp.concatenate([-sr, -sc, zero(32), sr, sc, zero(32)], 1)
    return jnp.concatenate([ca, sa, cb, sb], 1)


def _slot_maps():
    r = MLA_ROPE // 2
    mla = ([MLA_NOPE + i for i in range(r)] + list(range(0, 64 - r))
           + [MLA_NOPE + r + i for i in range(r)] + list(range(64 - r, MLA_NOPE)))
    mla += [-1] * (LANES - len(mla))
    q = HEAD_DIM // 4
    gqa = (list(range(0, q)) + list(range(2 * q, 3 * q)) + [-1] * (64 - 2 * q)
           + list(range(q, 2 * q)) + list(range(3 * q, 4 * q)) + [-1] * (64 - 2 * q))
    return mla, gqa


def _to_slots(a, lane_map):
    idx = jnp.asarray([max(i, 0) for i in lane_map], jnp.int32)
    keep = jnp.asarray([1.0 if i >= 0 else 0.0 for i in lane_map], a.dtype)
    return jnp.take(a, idx, axis=-1) * keep


def _rope_tables_odd(S):
    pos = jnp.arange(S, dtype=_F32)
    inv = ROPE_THETA ** (-jnp.arange(0, HEAD_DIM, 2, dtype=_F32) / HEAD_DIM)
    ang = pos[:, None] * inv[None, :]
    c, s = jnp.cos(ang), jnp.sin(ang)
    return jnp.concatenate([c, c, c, c, -s, -s, s, s], 1)


def _tiles(S):
    return dict(tm_pre=min(S, 512), tq=min(S, 2048), tm_o=min(S, 512),
                tm_ffn=min(S, 512), tf_ffn=256, tm_qkv=min(S, 512),
                tl=128, tm_moe=min(S, 512), tf_moe=256, tm_c=min(S, 512))


def kernel(x, c, ada_even_w, ada_even_b, norm_even_mix, norm_even_ffn, even_w_in, mla_q_norm, mla_w_uq, mla_kv_norm, mla_w_ukv, mla_q_gain, mla_k_gain, gqa_q_gain, gqa_k_gain, even_w_out, ffn_w_gate, ffn_w_up, ffn_w_down, ada_odd_w, ada_odd_b, norm_odd_mix, norm_odd_ffn, dil_w_qkv, dil_q_gain, dil_k_gain, dil_w_out, moe_router, moe_w_gate, moe_w_up, moe_w_down):
    B, S, D = x.shape
    T = B * S
    cfg = _tiles(S)

    mod_e = _ada_mod(c, ada_even_w[0], ada_even_b[0]).reshape(B, 6, D)
    mod_o = _ada_mod(c, ada_odd_w[0], ada_odd_b[0]).reshape(B, 6, D)

    w = even_w_in[0]
    sp = [MLA_Q_RANK, MLA_Q_RANK + MLA_KV_RANK, MLA_Q_RANK + MLA_KV_RANK + MLA_ROPE]
    sp.append(sp[-1] + GQA_HEADS * HEAD_DIM)
    sp.append(sp[-1] + GQA_KV_HEADS * HEAD_DIM)
    w_cq, w_ckv, w_kpe = w[:, :sp[0]], w[:, sp[0]:sp[1]], w[:, sp[1]:sp[2]]
    w_qb, w_kb, w_vb = w[:, sp[2]:sp[3]], w[:, sp[3]:sp[4]], w[:, sp[4]:]
    mla_map, gqa_map = _slot_maps()
    na = MLA_NOPE + MLA_ROPE
    nope_only = [i if 0 <= i < MLA_NOPE else -1 for i in mla_map]
    rope_only = [i - MLA_NOPE if i >= MLA_NOPE else -1 for i in mla_map]
    gslots = lambda a, n: _to_slots(a.reshape(D, n, HEAD_DIM), gqa_map).reshape(D, n * LANES)
    w_in = jnp.concatenate([w_cq, w_ckv, _to_slots(w_kpe, rope_only), gslots(w_qb, GQA_HEADS),
                            gslots(w_kb, GQA_KV_HEADS)], axis=1).astype(_BF)
    w_uq = _to_slots(mla_w_uq[0].reshape(MLA_Q_RANK, MLA_HEADS, na), mla_map)
    w_uq = w_uq.reshape(MLA_Q_RANK, MLA_HEADS * LANES).astype(_BF)
    w_ukv = mla_w_ukv[0].reshape(MLA_KV_RANK, MLA_HEADS, MLA_NOPE + MLA_V)
    w_uk = _to_slots(w_ukv[:, :, :MLA_NOPE], nope_only).reshape(MLA_KV_RANK, MLA_HEADS * LANES).astype(_BF)
    w_uvt = w_ukv[:, :, MLA_NOPE:].reshape(MLA_KV_RANK, MLA_HEADS * MLA_V).T.astype(_BF)
    w_vbt = w_vb.T.astype(_BF)
    g_rows = [_to_slots(mla_q_gain[0], mla_map) * (na ** -0.5 * LOG2E), _to_slots(mla_k_gain[0], mla_map),
              _to_slots(gqa_q_gain[0], gqa_map) * (HEAD_DIM ** -0.5 * LOG2E), _to_slots(gqa_k_gain[0], gqa_map)]
    bound_a = na * jnp.max(jnp.abs(g_rows[0])) * jnp.max(jnp.abs(g_rows[1]))
    bound_b = HEAD_DIM * jnp.max(jnp.abs(g_rows[2])) * jnp.max(jnp.abs(g_rows[3]))
    fast_a, fast_b = bound_a <= SCORE_BOUND_MAX, bound_b <= SCORE_BOUND_MAX
    last_lane = jnp.zeros((LANES,), _F32).at[LANES - 1].set(1.0)
    g_rows += [last_lane, last_lane * jnp.where(fast_a, -bound_a, 0.0),
               last_lane, last_lane * jnp.where(fast_b, -bound_b, 0.0)]
    gains_e = jnp.stack(g_rows)
    rope_e = _rope_tables_even(S)
    q_all, k_all, vt_all = _pre_even(
        x, mod_e, norm_even_mix[0].reshape(1, D), w_in, mla_q_norm[0].reshape(1, -1), w_uq,
        mla_kv_norm[0].reshape(1, -1), w_uk, w_uvt, w_vbt, gains_e, rope_e, cfg["tm_pre"])
    as_flag = lambda f: f.astype(jnp.int32).reshape(1)
    o_a = _attention(as_flag(fast_a), q_all, k_all, vt_all, q_off=0, k_off=0, shared_kv=False, tq=cfg["tq"])
    o_b = _attention(as_flag(fast_b), q_all, k_all, vt_all, q_off=MLA_HEADS, k_off=MLA_HEADS,
                     shared_kv=True, tq=cfg["tq"])
    x2, h2 = _ffn(x, o_a, o_b, even_w_out[0].astype(_BF), mod_e, norm_even_ffn[0].reshape(1, D), mod_o,
                  norm_odd_mix[0].reshape(1, D), ffn_w_gate[0].astype(_BF), ffn_w_up[0].astype(_BF),
                  ffn_w_down[0].astype(_BF), cfg["tm_ffn"], cfg["tf_ffn"])

    hh = HEAD_DIM // 2
    pair = lambda v: jnp.concatenate([v[:hh], v[:hh], v[hh:], v[hh:]])
    gains_o = jnp.stack([pair(dil_q_gain[0]) * (HEAD_DIM ** -0.5 * LOG2E), pair(dil_k_gain[0])])
    wq, wk, wv = jnp.split(dil_w_qkv[0], 3, axis=1)
    perm = lambda w: w.reshape(D, D // LANES, 2, 2, hh).transpose(0, 1, 3, 2, 4).reshape(D, D)
    w_qkv = jnp.concatenate([perm(wq), perm(wk), wv], axis=1).astype(_BF)
    qd, kd, vd = _qkv(h2, w_qkv, gains_o, _rope_tables_odd(S), cfg["tm_qkv"])
    bound_d = HEAD_DIM * jnp.max(jnp.abs(gains_o[0])) * jnp.max(jnp.abs(gains_o[1]))
    fast_d = bound_d <= SCORE_BOUND_MAX
    o_d = _dilated(as_flag(fast_d), jnp.where(fast_d, -bound_d, 0.0), qd, kd, vd, cfg["tl"])
    r32 = _pad_cols(moe_router[0], LANES)
    r_hi = r32.astype(_BF)
    r_lo = (r32 - r_hi.astype(_F32)).astype(_BF)
    x3, h3, route = _oproj_route(x2, mod_o, norm_odd_ffn[0].reshape(1, D), dil_w_out[0].astype(_BF), o_d,
                                 jnp.stack([r_hi, r_lo]), cfg["tm_o"])
    tm = cfg["tm_moe"]
    tile_e, tile_v, src, dst, n_rows = _route_plan(route.reshape(T, LANES), T, tm)
    y2 = _moe(h3, tile_e, tile_v, src, dst, moe_w_gate[0].astype(_BF),
              moe_w_up[0].astype(_BF), moe_w_down[0].astype(_BF), n_rows, tm, cfg["tf_moe"])
    return _combine(x3, y2, route, mod_o, cfg["tm_c"])
```

```python
import functools
import math

import jax
import jax.numpy as jnp
from jax import lax
from jax.experimental import pallas as pl
from jax.experimental.pallas import tpu as pltpu

_BF = jnp.bfloat16
_F32 = jnp.float32

GRID_W = 64
HEAD_DIM = 64
ROPE_THETA = 10000.0
EPS = 1e-6
MLA_HEADS = 8
MLA_Q_RANK = 256
MLA_KV_RANK = 128
MLA_NOPE = 64
MLA_ROPE = 32
MLA_V = 64
GQA_HEADS = 8
GQA_KV_HEADS = 2
DIL_PATTERNS = ((128, 1), (512, 4), (2048, 16))
N_EXPERTS = 8
TOP_K = 2
NEG = -1e30
LOG2E = math.log2(math.e)

LANES = 128
VMEM_LIMIT = 56 * 1024 * 1024


def _cparams(sem, vmem=VMEM_LIMIT):
    return pltpu.CompilerParams(dimension_semantics=sem, vmem_limit_bytes=vmem)


def _silu(x):
    return x / (1.0 + jnp.exp(-x))


def _modulate(x, g, shift, scale):
    ms = jnp.mean(x * x, axis=-1, keepdims=True)
    return x * lax.rsqrt(ms + EPS) * g * (1.0 + scale) + shift


def _norm(v, n, g):
    return v * lax.rsqrt(jnp.sum(v * v, axis=-1, keepdims=True) * (1.0 / n) + EPS) * g


SUBLANES = 8


def _rows_to_tiles(ref, x, row0=0):
    tm = x.shape[0]
    for c in range(SUBLANES):
        ref[pl.ds(row0 * SUBLANES + c, tm, stride=SUBLANES), :] = x[:, c * LANES:(c + 1) * LANES]


def _tiles_to_rows(ref, tm):
    return jnp.concatenate([ref[pl.ds(c, tm, stride=SUBLANES), :] for c in range(SUBLANES)], axis=1)


def _mod_kernel(c_ref, w_ref, b_ref, o_ref):
    sc = _silu(c_ref[...])
    o_ref[...] = jnp.dot(sc.astype(_BF), w_ref[...].astype(_BF),
                         preferred_element_type=_F32) + b_ref[...]


def _ada_mod(c, w, b):
    B, D = c.shape
    N = w.shape[1]
    tn = min(N, 1536)
    return pl.pallas_call(
        _mod_kernel,
        grid=(N // tn,),
        in_specs=[pl.BlockSpec((B, D), lambda j: (0, 0)),
                  pl.BlockSpec((D, tn), lambda j: (0, j)),
                  pl.BlockSpec((1, tn), lambda j: (0, j))],
        out_specs=pl.BlockSpec((B, tn), lambda j: (0, j)),
        out_shape=jax.ShapeDtypeStruct((B, N), _F32),
        compiler_params=_cparams(("arbitrary",)),
        name="ada_mod",
    )(c, w, b.reshape(1, N))


def _pre_even_kernel(x_ref, mod_ref, g_ref, w_in_ref, qn_ref, w_uq_ref, kvn_ref, w_uk_ref, w_uvt_ref,
                     w_vbt_ref, gains_ref, rope_ref, q_ref, k_ref, vt_ref, nq_scr, nk_scr):
    h = _modulate(x_ref[...], g_ref[...], mod_ref[0:1, :], mod_ref[1:2, :]).astype(_BF)
    y = jnp.dot(h, w_in_ref[...], preferred_element_type=_F32)
    cqn = _norm(y[:, 0:256], MLA_Q_RANK, qn_ref[...]).astype(_BF)
    qa = jnp.dot(cqn, w_uq_ref[...], preferred_element_type=_F32)
    ckvn = _norm(y[:, 256:384], MLA_KV_RANK, kvn_ref[...]).astype(_BF)
    kn = jnp.dot(ckvn, w_uk_ref[...], preferred_element_type=_F32)
    kpe = y[:, 384:512]
    nt_dims = (((1,), (1,)), ((), ()))
    vt_a = lax.dot_general(w_uvt_ref[...], ckvn, nt_dims, preferred_element_type=_F32).astype(_BF)
    vt_b = lax.dot_general(w_vbt_ref[...], h, nt_dims, preferred_element_type=_F32).astype(_BF)
    tm = x_ref.shape[0]
    ones_blk = jnp.where(lax.broadcasted_iota(jnp.int32, (LANES - MLA_V, tm), 0) == 0, 1.0, 0.0).astype(_BF)
    for hh in range(MLA_HEADS + GQA_KV_HEADS):
        src = vt_a[hh * MLA_V:(hh + 1) * MLA_V] if hh < MLA_HEADS else \
            vt_b[(hh - MLA_HEADS) * HEAD_DIM:(hh - MLA_HEADS + 1) * HEAD_DIM]
        vt_ref[hh * LANES:hh * LANES + MLA_V, :] = src
        vt_ref[hh * LANES + MLA_V:(hh + 1) * LANES, :] = ones_blk
    ca, sa = rope_ref[:, 0:128], rope_ref[:, 128:256]
    cb, sb = rope_ref[:, 256:384], rope_ref[:, 384:512]
    gqa, gka = gains_ref[0:1, :], gains_ref[1:2, :]
    gqb, gkb = gains_ref[2:3, :], gains_ref[3:4, :]
    na = MLA_NOPE + MLA_ROPE

    for hh in range(MLA_HEADS):
        sl = slice(hh * LANES, (hh + 1) * LANES)
        nq_scr[:, sl] = _norm(qa[:, sl], na, gqa)
        nk_scr[:, sl] = _norm(kn[:, sl] + kpe, na, gka)
    for hh in range(GQA_HEADS):
        src = slice(512 + hh * LANES, 512 + (hh + 1) * LANES)
        dst = slice((MLA_HEADS + hh) * LANES, (MLA_HEADS + hh + 1) * LANES)
        nq_scr[:, dst] = _norm(y[:, src], HEAD_DIM, gqb)
    for g in range(GQA_KV_HEADS):
        src = slice(1536 + g * LANES, 1536 + (g + 1) * LANES)
        dst = slice((MLA_HEADS + g) * LANES, (MLA_HEADS + g + 1) * LANES)
        nk_scr[:, dst] = _norm(y[:, src], HEAD_DIM, gkb)

    @pl.when(pl.program_id(1) >= 0)
    def _():
        def rope(ref, out, n_a, n_all, row):
            for hh in range(n_all):
                sl = slice(hh * LANES, (hh + 1) * LANES)
                c, s = (ca, sa) if hh < n_a else (cb, sb)
                b = gains_ref[row:row + 1, :] if hh < n_a else gains_ref[row + 2:row + 3, :]
                v = ref[:, sl]
                out[:, sl] = (v * c + pltpu.roll(v, 64, 1) * s + b).astype(_BF)

        rope(nq_scr, q_ref, MLA_HEADS, MLA_HEADS + GQA_HEADS, 4)
        rope(nk_scr, k_ref, MLA_HEADS, MLA_HEADS + GQA_KV_HEADS, 5)


def _pre_even(x, mod, g, w_in, qn, w_uq, kvn, w_uk, w_uvt, w_vbt, gains, rope, tm):
    B, S, D = x.shape
    nq = (MLA_HEADS + GQA_HEADS) * LANES
    nk = (MLA_HEADS + GQA_KV_HEADS) * LANES
    nvt = (MLA_HEADS + GQA_KV_HEADS) * LANES
    full = lambda a: pl.BlockSpec(a.shape, lambda b, i: (0,) * a.ndim)
    tok = lambda w: pl.BlockSpec((None, tm, w), lambda b, i: (b, i, 0))
    return pl.pallas_call(
        _pre_even_kernel,
        grid=(B, S // tm),
        in_specs=[tok(D),
                  pl.BlockSpec((None, 6, D), lambda b, i: (b, 0, 0)),
                  full(g), full(w_in), full(qn), full(w_uq), full(kvn), full(w_uk), full(w_uvt),
                  full(w_vbt), full(gains),
                  pl.BlockSpec((tm, rope.shape[1]), lambda b, i: (i, 0))],
        out_specs=[tok(nq), tok(nk), pl.BlockSpec((None, None, nvt, tm), lambda b, i: (b, i, 0, 0))],
        out_shape=[jax.ShapeDtypeStruct((B, S, nq), _BF), jax.ShapeDtypeStruct((B, S, nk), _BF),
                   jax.ShapeDtypeStruct((B, S // tm, nvt, tm), _BF)],
        scratch_shapes=[pltpu.VMEM((tm, nq), _F32), pltpu.VMEM((tm, nk), _F32)],
        compiler_params=_cparams(("parallel", "parallel")),
        name="pre_even",
    )(x, mod, g, w_in, qn, w_uq, kvn, w_uk, w_uvt, w_vbt, gains, rope)


SCORE_BOUND_MAX = 60.0


def _attn_kernel(fast_ref, q_ref, k_ref, vt_ref, o_ref, s_scr, *, shared_kv):
    @pl.when(fast_ref[0] == 1)
    def _():
        _attn_fixed_shift(q_ref, k_ref, vt_ref, o_ref, shared_kv=shared_kv)

    @pl.when(fast_ref[0] != 1)
    def _():
        _attn_running_max(q_ref, k_ref, vt_ref, o_ref, s_scr, shared_kv=shared_kv)


def _attn_fixed_shift(q_ref, k_ref, vt_ref, o_ref, *, shared_kv):
    tq = q_ref.shape[0]
    nk, _, tk = vt_ref.shape
    qs = [q_ref[:, hh * LANES:(hh + 1) * LANES] for hh in range(2)]

    def body(j, carry):
        r0 = pl.multiple_of(j * tk, tk)
        out = []
        for hh in range(2):
            kc = 0 if shared_kv else hh * LANES
            k = k_ref[pl.ds(r0, tk), kc:kc + LANES]
            st = lax.dot_general(k, qs[hh], (((1,), (1,)), ((), ())), preferred_element_type=_F32)
            p = jnp.exp2(st).astype(_BF)
            out.append(carry[hh] + jnp.dot(vt_ref[j, kc:kc + LANES, :], p, preferred_element_type=_F32))
        return tuple(out)

    z = jnp.zeros((LANES, tq), _F32)
    res = lax.fori_loop(0, nk, body, (z, z), unroll=True)
    nv = HEAD_DIM
    halves = [acc[0:nv, :] / acc[nv:nv + 1, :] for acc in res]
    o_ref[...] = jnp.concatenate(halves, axis=0).T.astype(o_ref.dtype)


def _attn_running_max(q_ref, k_ref, vt_ref, o_ref, s_scr, *, shared_kv):
    tq = q_ref.shape[0]
    nk, _, tk = vt_ref.shape
    qs = [q_ref[:, hh * LANES:(hh + 1) * LANES] for hh in range(2)]

    def scores(j, hh):
        r0 = j * tk if isinstance(j, int) else pl.multiple_of(j * tk, tk)
        kc = 0 if shared_kv else hh * LANES
        k = k_ref[pl.ds(r0, tk), kc:kc + LANES]
        return lax.dot_general(k, qs[hh], (((1,), (1,)), ((), ())), preferred_element_type=_F32)

    def put_scores(j, hh, slot):
        st = scores(j, hh)
        s_scr[hh, slot] = st
        return jnp.max(st, axis=0, keepdims=True)

    def step(j, hh, carry, cur, last=False):
        m, mc, acc = carry
        mc_next = mc if last else put_scores(j + 1, hh, 1 - cur)
        st = s_scr[hh, cur]
        m_new = jnp.maximum(m, mc)
        alpha = jnp.exp2(m - m_new)
        p = jnp.exp2(st - m_new).astype(_BF)
        vc = 0 if shared_kv else hh * LANES
        acc = alpha * acc + jnp.dot(vt_ref[j, vc:vc + LANES, :], p, preferred_element_type=_F32)
        return m_new, mc_next, acc

    def body(jj, carry):
        c = list(carry)
        for sub in range(2):
            for hh in range(2):
                c[hh] = step(2 * jj + sub, hh, c[hh], sub)
        return tuple(c)

    init = tuple((jnp.full((1, tq), NEG, _F32), put_scores(0, hh, 0), jnp.zeros((LANES, tq), _F32))
                 for hh in range(2))
    res = lax.fori_loop(0, nk // 2 - 1, body, init)
    res = list(res)
    for sub in range(2):
        for hh in range(2):
            res[hh] = step(nk - 2 + sub, hh, res[hh], sub, last=(sub == 1))
    nv = HEAD_DIM
    halves = [acc[0:nv, :] / acc[nv:nv + 1, :] for (_, _, acc) in res]
    o_ref[...] = jnp.concatenate(halves, axis=0).T.astype(o_ref.dtype)


def _attention(fast, q, k, vt, *, q_off, k_off, shared_kv, tq):
    B, S, _ = q.shape
    nk, _, tk = vt.shape[1:]
    assert nk % 2 == 0 and nk >= 2
    npairs = 4
    if shared_kv:
        kspec = pl.BlockSpec((None, S, LANES), lambda b, p, i, f: (b, 0, k_off + p // 2))
        vspec = pl.BlockSpec((None, nk, LANES, tk), lambda b, p, i, f: (b, 0, k_off + p // 2, 0))
    else:
        kspec = pl.BlockSpec((None, S, 2 * LANES), lambda b, p, i, f: (b, 0, k_off // 2 + p))
        vspec = pl.BlockSpec((None, nk, 2 * LANES, tk), lambda b, p, i, f: (b, 0, k_off // 2 + p, 0))
    grid_spec = pltpu.PrefetchScalarGridSpec(
        num_scalar_prefetch=1,
        grid=(B, npairs, S // tq),
        in_specs=[pl.BlockSpec((None, tq, 2 * LANES), lambda b, p, i, f: (b, i, q_off // 2 + p)),
                  kspec, vspec],
        out_specs=pl.BlockSpec((None, tq, LANES), lambda b, p, i, f: (b, i, p)),
        scratch_shapes=[pltpu.VMEM((2, 2, tk, tq), _F32)],
    )
    return pl.pallas_call(
        functools.partial(_attn_kernel, shared_kv=shared_kv),
        grid_spec=grid_spec,
        out_shape=jax.ShapeDtypeStruct((B, S, npairs * LANES), _BF),
        compiler_params=_cparams(("parallel", "parallel", "parallel")),
        name="attn_gqa" if shared_kv else "attn_mla",
    )(fast, q, k, vt)


def _oproj_route_kernel(x_ref, mod_ref, g_ref, w_ref, o_ref, r_ref, x_out, h_out, route_out, *, sub):
    for r0 in range(0, x_ref.shape[0], sub):
        rows = slice(r0, r0 + sub)
        o = jnp.concatenate([o_ref[p, rows, :] for p in range(o_ref.shape[0])], axis=1)
        y = jnp.dot(o, w_ref[...], preferred_element_type=_F32)
        x1 = x_ref[rows, :] + mod_ref[2:3, :] * y
        x_out[rows, :] = x1
        h = _modulate(x1, g_ref[...], mod_ref[3:4, :], mod_ref[4:5, :])
        _rows_to_tiles(h_out, h, r0)
        hh = h.astype(_BF)
        hl = (h - hh.astype(_F32)).astype(_BF)
        rh, rl = r_ref[0], r_ref[1]
        logits = (jnp.dot(hh, rh, preferred_element_type=_F32)
                  + jnp.dot(hh, rl, preferred_element_type=_F32)
                  + jnp.dot(hl, rh, preferred_element_type=_F32))
        tm = logits.shape[0]
        lane = lax.broadcasted_iota(jnp.int32, (tm, LANES), 1)
        lg = jnp.where(lane < N_EXPERTS, logits, NEG)
        m1 = jnp.max(lg, axis=-1, keepdims=True)
        lanef = lane.astype(_F32)
        i1 = jnp.min(jnp.where(lg == m1, lanef, float(LANES)), axis=-1, keepdims=True)
        lg2 = jnp.where(lanef == i1, NEG, lg)
        m2 = jnp.max(lg2, axis=-1, keepdims=True)
        i2 = jnp.min(jnp.where(lg2 == m2, lanef, float(LANES)), axis=-1, keepdims=True)
        e = jnp.exp(m2 - m1)
        g1 = 1.0 / (1.0 + e)
        g2 = e / (1.0 + e)
        route_out[rows, :] = jnp.where(lane == 0, i1,
                                   jnp.where(lane == 1, i2,
                                             jnp.where(lane == 2, g1, jnp.where(lane == 3, g2, 0.0))))


def _oproj_route(x, mod, g, w, o, router, tm):
    B, S, D = x.shape
    assert D == SUBLANES * LANES
    per_b = S // tm
    tok = lambda wd: pl.BlockSpec((None, tm, wd), lambda b, i: (b, i, 0))
    return pl.pallas_call(
        functools.partial(_oproj_route_kernel, sub=min(tm, 256)),
        grid=(B, S // tm),
        in_specs=[tok(D), pl.BlockSpec((None, 6, D), lambda b, i: (b, 0, 0)),
                  pl.BlockSpec((1, D), lambda b, i: (0, 0)),
                  pl.BlockSpec(w.shape, lambda b, i: (0, 0)),
                  pl.BlockSpec((None, o.shape[1], tm, LANES), lambda b, i: (b, 0, i, 0)),
                  pl.BlockSpec(router.shape, lambda b, i: (0, 0, 0))],
        out_specs=[tok(D), pl.BlockSpec((tm * SUBLANES, LANES), lambda b, i: (b * per_b + i, 0)), tok(LANES)],
        out_shape=[jax.ShapeDtypeStruct((B, S, D), _F32),
                   jax.ShapeDtypeStruct((B * S * SUBLANES, LANES), _F32),
                   jax.ShapeDtypeStruct((B, S, LANES), _F32)],
        compiler_params=_cparams(("parallel", "parallel")),
        name="oproj_route",
    )(x, mod, g, w, o, router)


def _swiglu_hidden(x, wg_ref, wu_ref, a_scr, tf, after_chunk=None):
    F = wg_ref.shape[1]
    for c0 in range(0, F, tf):
        c1 = min(c0 + tf, F)
        g = jnp.dot(x, wg_ref[:, c0:c1], preferred_element_type=_F32)
        u = jnp.dot(x, wu_ref[:, c0:c1], preferred_element_type=_F32)
        a_scr[:, c0:c1] = (_silu(g) * u).astype(_BF)
        if after_chunk is not None:
            after_chunk(c0 // tf)


def _ffn_kernel(x_ref, oa_ref, ob_ref, wo_ref, mod_ref, g_ref, modn_ref, gn_ref, wg_ref, wu_ref, wd_ref,
                x_out, h_out, a_scr, *, tf):
    wa = oa_ref.shape[1]
    y = (jnp.dot(oa_ref[...], wo_ref[0:wa, :], preferred_element_type=_F32)
         + jnp.dot(ob_ref[...], wo_ref[wa:, :], preferred_element_type=_F32))
    x1 = x_ref[...] + mod_ref[2:3, :] * y
    h = _modulate(x1, g_ref[...], mod_ref[3:4, :], mod_ref[4:5, :]).astype(_BF)
    _swiglu_hidden(h, wg_ref, wu_ref, a_scr, tf)
    y = jnp.dot(a_scr[...], wd_ref[...], preferred_element_type=_F32)
    x2 = x1 + mod_ref[5:6, :] * y
    x_out[...] = x2
    h_out[...] = _modulate(x2, gn_ref[...], modn_ref[0:1, :], modn_ref[1:2, :]).astype(h_out.dtype)


def _ffn(x, o_a, o_b, w_out, mod, g, modn, gn, wg, wu, wd, tm, tf):
    B, S, D = x.shape
    T = B * S
    F = wg.shape[1]
    per_b = S // tm
    flat = lambda a: a.reshape(T, a.shape[2])
    tok = lambda w: pl.BlockSpec((tm, w), lambda i: (i, 0))
    modspec = pl.BlockSpec((None, 6, D), lambda i: (i // per_b, 0, 0))
    row = pl.BlockSpec((1, D), lambda i: (0, 0))
    once = pl.Buffered(1)
    res = lambda a: pl.BlockSpec(a.shape, lambda i: (0, 0), pipeline_mode=once)
    xo, ho = pl.pallas_call(
        functools.partial(_ffn_kernel, tf=tf),
        grid=(T // tm,),
        in_specs=[tok(D), tok(o_a.shape[2]), tok(o_b.shape[2]), res(w_out), modspec, row, modspec, row,
                  res(wg), res(wu), res(wd)],
        out_specs=[tok(D), tok(D)],
        out_shape=[jax.ShapeDtypeStruct((T, D), _F32), jax.ShapeDtypeStruct((T, D), _BF)],
        scratch_shapes=[pltpu.VMEM((tm, F), _BF)],
        compiler_params=_cparams(("parallel",)),
        name="ffn_dense",
    )(flat(x), flat(o_a), flat(o_b), w_out, mod, g, modn, gn, wg, wu, wd)
    return xo.reshape(B, S, D), ho.reshape(B, S, D)


def _qkv_kernel(h_ref, w_ref, gains_ref, rope_ref, q_ref, k_ref, v_ref, n_scr):
    tm, D = h_ref.shape
    h = h_ref[...]
    lane = lax.broadcasted_iota(jnp.int32, (tm, LANES), 1)
    head_a = (lane & 63) < 32

    def head_pair_norm(v, g):
        sq = v * v
        s_a = jnp.sum(jnp.where(head_a, sq, 0.0), axis=-1, keepdims=True)
        s_b = jnp.sum(jnp.where(head_a, 0.0, sq), axis=-1, keepdims=True)
        r = lax.rsqrt(jnp.where(head_a, s_a, s_b) * (1.0 / HEAD_DIM) + EPS)
        return v * r * g

    y = jnp.dot(h, w_ref[...], preferred_element_type=_F32)
    for p in range(D // LANES):
        sl = slice(p * LANES, (p + 1) * LANES)
        n_scr[:, sl] = head_pair_norm(y[:, sl], gains_ref[0:1, :])
        n_scr[:, D + p * LANES:D + (p + 1) * LANES] = head_pair_norm(
            y[:, D + p * LANES:D + (p + 1) * LANES], gains_ref[1:2, :])
        v_ref[p] = y[:, 2 * D + p * LANES:2 * D + (p + 1) * LANES].astype(_BF)

    @pl.when(pl.program_id(1) >= 0)
    def _():
        c, s = rope_ref[:, 0:128], rope_ref[:, 128:256]
        for p in range(D // LANES):
            vq = n_scr[:, p * LANES:(p + 1) * LANES]
            vk = n_scr[:, D + p * LANES:D + (p + 1) * LANES]
            q_ref[p] = (vq * c + pltpu.roll(vq, 64, 1) * s).astype(_BF)
            k_ref[p] = (vk * c + pltpu.roll(vk, 64, 1) * s).astype(_BF)


def _qkv(h, w, gains, rope, tm):
    B, S, D = h.shape
    tok = pl.BlockSpec((None, tm, D), lambda b, i: (b, i, 0))
    hp = pl.BlockSpec((None, D // LANES, tm, LANES), lambda b, i: (b, 0, i, 0))
    sds = jax.ShapeDtypeStruct((B, D // LANES, S, LANES), _BF)
    return pl.pallas_call(
        _qkv_kernel,
        grid=(B, S // tm),
        in_specs=[tok, pl.BlockSpec(w.shape, lambda b, i: (0, 0)),
                  pl.BlockSpec((2, LANES), lambda b, i: (0, 0)),
                  pl.BlockSpec((tm, 2 * LANES), lambda b, i: (i, 0))],
        out_specs=[hp, hp, hp],
        out_shape=[sds, sds, sds],
        scratch_shapes=[pltpu.VMEM((tm, 2 * D), _F32)],
        compiler_params=_cparams(("parallel", "parallel")),
        name="qkv_dil",
    )(h, w, gains, rope)


DIL_GROUP = 4


def _dil_kernel(fast_ref, q_ref, k_ref, v_ref, shift_ref, o_ref, tmp, qp, kp, vp, acc, mm, ll, accp, mmp,
                llp, bias_scr, *, tl):
    @pl.when(fast_ref[0] == 1)
    def _():
        _dil_body(q_ref, k_ref, v_ref, shift_ref, o_ref, tmp, qp, kp, vp, acc, mm, ll, accp, mmp, llp,
                  bias_scr, tl=tl, fixed_shift=True)

    @pl.when(fast_ref[0] != 1)
    def _():
        _dil_body(q_ref, k_ref, v_ref, shift_ref, o_ref, tmp, qp, kp, vp, acc, mm, ll, accp, mmp, llp,
                  bias_scr, tl=tl, fixed_shift=False)


def _dil_body(q_ref, k_ref, v_ref, shift_ref, o_ref, tmp, qp, kp, vp, acc, mm, ll, accp, mmp, llp,
              bias_scr, *, tl, fixed_shift):
    S = q_ref.shape[0]
    ng = S // DIL_GROUP
    pitch = DIL_GROUP + 1

    def spread(dst, src):
        for j in range(DIL_GROUP):
            dst[pl.ds(j, ng, stride=pitch), :] = src[pl.ds(j, ng, stride=DIL_GROUP), :]

    for src, dst in ((q_ref, qp), (k_ref, kp), (v_ref, vp)):
        tmp[...] = src[...].astype(_F32)
        spread(dst, tmp)
    lane = lax.broadcasted_iota(jnp.int32, (tl, LANES), 1)
    lo = lane < 64
    head_a = (lane & 63) < 32
    for bi, (window, d) in enumerate(DIL_PATTERNS):
        L = S // d
        half = window // (2 * d)
        t = min(tl, L)
        W = min(t + 2 * half, L)
        nt = L // t
        if d > 1:
            assert d % DIL_GROUP == 0
        sd = d * pitch // DIL_GROUP

        def rows(r, first, n, d=d, sd=sd):
            if d == 1:
                return pl.ds(first, n)
            return pl.ds(r + r // DIL_GROUP + sd * first, n, stride=sd)

        assert t % half == 0 and W <= t + 2 * half
        qa = lax.broadcasted_iota(jnp.int32, (t, W), 0)
        kb = lax.broadcasted_iota(jnp.int32, (t, W), 1)
        inside = shift_ref[0:1, 0:W] if fixed_shift else 0.0
        for case in range(3):
            ok = jnp.abs(kb - qa - case * half) <= half
            bias_scr[case, 0:t, 0:W] = jnp.where(ok, inside, NEG)

        def body(idx, carry, bi=bi, d=d, L=L, half=half, t=t, W=W, nt=nt, rows=rows):
            r = idx // nt
            l0 = (idx % nt) * t
            start = jnp.clip(l0 - half, 0, L - W)
            if d == 1:
                qt = q_ref[pl.ds(pl.multiple_of(l0, t), t), :]
                kw = k_ref[pl.ds(pl.multiple_of(start, 64), W), :]
                vw = v_ref[pl.ds(pl.multiple_of(start, 64), W), :]
            else:
                qt = qp[rows(r, l0, t), :].astype(_BF)
                kw = kp[rows(r, start, W), :].astype(_BF)
                vw = vp[rows(r, start, W), :].astype(_BF)
            lo_t = lo[:t]
            qa_t = head_a[:t]
            zq = jnp.zeros_like(qt)
            q2 = jnp.concatenate([jnp.where(qa_t, qt, zq), jnp.where(qa_t, zq, qt)], axis=0)
            s = lax.dot_general(q2, kw, (((1,), (1,)), ((), ())), preferred_element_type=_F32)
            bias = bias_scr[(l0 - start) // half, 0:t, 0:W]
            s = s + jnp.concatenate([bias, bias], axis=0)
            if fixed_shift:
                p = jnp.exp2(s)
            else:
                m = jnp.max(s, axis=-1, keepdims=True)
                p = jnp.exp2(s - m)
                m_new = jnp.where(lo_t, m[:t], m[t:])
            den = jnp.sum(p, axis=-1, keepdims=True)
            o2 = jnp.dot(p.astype(_BF), vw, preferred_element_type=_F32)
            o_new = jnp.where(lo_t, o2[:t], o2[t:])
            l_new = jnp.where(lo_t, den[:t], den[t:])
            tok = rows(r, l0, t)
            if d == 1:
                assert bi == 0
                acc[tok, :] = o_new
                ll[tok, :] = l_new
                if not fixed_shift:
                    mm[tok, :] = m_new
            elif fixed_shift:
                accp[tok, :] = accp[tok, :] + o_new
                llp[tok, :] = llp[tok, :] + l_new
            else:
                m_old = mmp[tok, :]
                mx = jnp.maximum(m_old, m_new)
                a_old = jnp.exp2(m_old - mx)
                a_new = jnp.exp2(m_new - mx)
                accp[tok, :] = accp[tok, :] * a_old + o_new * a_new
                llp[tok, :] = llp[tok, :] * a_old + l_new * a_new
                mmp[tok, :] = mx
            return carry

        lax.fori_loop(0, d * nt, body, 0, unroll=min(32 if fixed_shift else 8, d * nt))
        if bi == 0:
            spread(accp, acc)
            spread(llp, ll)
            if not fixed_shift:
                spread(mmp, mm)
    for j in range(DIL_GROUP):
        grp = pl.ds(j, ng, stride=pitch)
        tmp[pl.ds(j, ng, stride=DIL_GROUP), :] = accp[grp, :] / llp[grp, :]
    o_ref[...] = tmp[...].astype(o_ref.dtype)


def _dilated(fast, shift, q, k, v, tl):
    B, P, S, _ = q.shape
    wmax = tl + max(w // d for w, d in DIL_PATTERNS)
    spec = pl.BlockSpec((None, None, S, LANES), lambda b, p, f: (b, p, 0, 0))
    scr = pltpu.VMEM((S, LANES), _F32)
    scrp = pltpu.VMEM((S // DIL_GROUP * (DIL_GROUP + 1), LANES), _F32)
    grid_spec = pltpu.PrefetchScalarGridSpec(
        num_scalar_prefetch=1,
        grid=(B, P),
        in_specs=[spec, spec, spec, pl.BlockSpec((1, wmax), lambda b, p, f: (0, 0))],
        out_specs=spec,
        scratch_shapes=[scr, scrp, scrp, scrp, scr, scr, scr, scrp, scrp, scrp,
                        pltpu.VMEM((3, tl, wmax), _F32)],
    )
    return pl.pallas_call(
        functools.partial(_dil_kernel, tl=tl),
        grid_spec=grid_spec,
        out_shape=jax.ShapeDtypeStruct((B, P, S, LANES), _BF),
        compiler_params=_cparams(("parallel", "parallel")),
        name="dil_attn",
    )(fast, q, k, v, jnp.full((1, wmax), shift, _F32))


def _moe_kernel(te_ref, tv_ref, src_hbm, dst_hbm, h_hbm, wg_ref, wu_ref, wd_ref, out_hbm,
                xbuf, xbf, a_scr, ybuf, src_s0, src_s1, dst_s0, dst_s1, gsem, ssem, isem,
                *, tf, dump0, n_dump):
    i = pl.program_id(0)
    nt = pl.num_programs(0)
    tm = xbf.shape[0]
    slot = i % 2

    def is_valid(t):
        return jnp.logical_and(jnp.logical_and(t >= 0, t < nt), tv_ref[jnp.clip(t, 0, nt - 1)] == 1)

    def both(a, b):
        return jnp.logical_and(a, b)

    valid = is_valid(i)
    prev_valid = is_valid(i - 1)

    src_bufs = (src_s0, src_s1)
    dst_bufs = (dst_s0, dst_s1)

    def src_copy(tile, par):
        return pltpu.make_async_copy(src_hbm.at[tile], src_bufs[par], isem.at[par])

    def dst_copy(tile, par):
        return pltpu.make_async_copy(dst_hbm.at[tile], dst_bufs[par], isem.at[2 + par])

    def tile_at(first):
        return pl.ds(pl.multiple_of(first, SUBLANES), SUBLANES)

    def gather_row(r, par):
        return pltpu.make_async_copy(h_hbm.at[tile_at(src_bufs[par][r])],
                                     xbuf.at[par, tile_at(r * SUBLANES)], gsem.at[par])

    def scatter_row(r, par):
        return pltpu.make_async_copy(ybuf.at[par, tile_at(r * SUBLANES)],
                                     out_hbm.at[tile_at(dst_bufs[par][r])], ssem.at[par])

    def gather_wait(par):
        return pltpu.make_async_copy(h_hbm.at[pl.ds(0, tm * SUBLANES)], xbuf.at[par], gsem.at[par])

    def scatter_wait(par):
        return pltpu.make_async_copy(ybuf.at[par], out_hbm.at[pl.ds(0, tm * SUBLANES)], ssem.at[par])

    def for_rows(fn):
        def body(r, c):
            fn(r)
            return c
        lax.fori_loop(0, tm, body, 0, unroll=8)

    @pl.when(i == 0)
    def _():
        ybuf[...] = jnp.zeros_like(ybuf)
        for k in range(n_dump):
            pltpu.make_async_copy(ybuf.at[1], out_hbm.at[pl.ds((dump0 + k * tm) * SUBLANES, tm * SUBLANES)],
                                  ssem.at[1]).start()
        for k in range(n_dump):
            scatter_wait(1).wait()

    @pl.when(both(i == 0, valid))
    def _():
        src_copy(0, 0).start()
        src_copy(0, 0).wait()
        for_rows(lambda r: gather_row(r, 0).start())
        src_copy(1, 1).start()

    def phase1(par, with_scatter):
        xbf[...] = _tiles_to_rows(xbuf.at[par], tm).astype(_BF)
        n_chunk = -(-wg_ref.shape[1] // tf)
        per = -(-tm // n_chunk)

        def after(c):
            for r in range(c * per, min((c + 1) * per, tm)):
                gather_row(r, 1 - par).start(priority=r % 2)
                if with_scatter:
                    scatter_row(r, 1 - par).start(priority=(r + 1) % 2)

        _swiglu_hidden(xbf[...], wg_ref, wu_ref, a_scr, tf, after)
        _rows_to_tiles(ybuf.at[par], jnp.dot(a_scr[...], wd_ref[...], preferred_element_type=_F32))

    for par in range(2):
        here = slot == par

        @pl.when(both(valid, here))
        def _(par=par):
            dst_copy(i, par).start()
            src_copy(i + 1, 1 - par).wait()

        @pl.when(both(prev_valid, here))
        def _(par=par):
            dst_copy(i - 1, 1 - par).wait()

        @pl.when(both(jnp.logical_or(both(i == 0, valid), prev_valid), here))
        def _(par=par):
            gather_wait(par).wait()

        @pl.when(both(is_valid(i - 2), here))
        def _(par=par):
            scatter_wait(par).wait()

        @pl.when(both(both(valid, prev_valid), here))
        def _(par=par):
            phase1(par, True)

        if par == 0:
            @pl.when(both(both(valid, jnp.logical_not(prev_valid)), here))
            def _():
                phase1(0, False)

        @pl.when(both(both(jnp.logical_not(valid), prev_valid), here))
        def _(par=par):
            for_rows(lambda r: scatter_row(r, 1 - par).start())

        @pl.when(both(is_valid(i + 1), here))
        def _(par=par):
            src_copy(i + 2, par).start()


def _moe(h, tile_e, tile_v, src, dst, wg, wu, wd, n_out_rows, tm, tf):
    T = h.shape[0] // SUBLANES
    D = wg.shape[1]
    nt = tile_e.shape[0]
    F = wg.shape[2]
    dump0 = T * TOP_K
    once = pl.Buffered(1)
    grid_spec = pltpu.PrefetchScalarGridSpec(
        num_scalar_prefetch=2,
        grid=(nt,),
        in_specs=[pl.BlockSpec(memory_space=pl.ANY), pl.BlockSpec(memory_space=pl.ANY),
                  pl.BlockSpec(memory_space=pl.ANY),
                  pl.BlockSpec((None, D, F), lambda i, te, tv: (te[i], 0, 0), pipeline_mode=once),
                  pl.BlockSpec((None, D, F), lambda i, te, tv: (te[i], 0, 0), pipeline_mode=once),
                  pl.BlockSpec((None, F, D), lambda i, te, tv: (te[i], 0, 0), pipeline_mode=once)],
        out_specs=pl.BlockSpec(memory_space=pl.ANY),
        scratch_shapes=[pltpu.VMEM((2, tm * SUBLANES, LANES), _F32), pltpu.VMEM((tm, D), _BF),
                        pltpu.VMEM((tm, F), _BF), pltpu.VMEM((2, tm * SUBLANES, LANES), _F32),
                        pltpu.SMEM((tm,), jnp.int32), pltpu.SMEM((tm,), jnp.int32),
                        pltpu.SMEM((tm,), jnp.int32), pltpu.SMEM((tm,), jnp.int32),
                        pltpu.SemaphoreType.DMA((2,)), pltpu.SemaphoreType.DMA((2,)),
                        pltpu.SemaphoreType.DMA((4,))],
    )
    return pl.pallas_call(
        functools.partial(_moe_kernel, tf=tf, dump0=dump0, n_dump=(n_out_rows - dump0) // tm),
        grid_spec=grid_spec,
        out_shape=jax.ShapeDtypeStruct((n_out_rows * SUBLANES, LANES), _F32),
        compiler_params=_cparams(("arbitrary",)),
        name="moe_experts",
    )(tile_e, tile_v, src, dst, h, wg, wu, wd)


def _route_plan(route, T, tm):
    A = T * TOP_K
    e_flat = route[:, :TOP_K].astype(jnp.int32).reshape(A)
    order = jnp.argsort(e_flat, stable=True).astype(jnp.int32)
    experts = jnp.arange(N_EXPERTS, dtype=jnp.int32)
    counts = jnp.sum((e_flat[None, :] == experts[:, None]).astype(jnp.int32), axis=1)
    starts = jnp.cumsum(counts) - counts
    pcounts = (counts + tm - 1) // tm * tm
    pends = jnp.cumsum(pcounts)
    pstarts = pends - pcounts
    nt = A // tm + N_EXPERTS + 2
    tile0 = jnp.arange(nt, dtype=jnp.int32) * tm
    tile_v = (tile0 < pends[-1]).astype(jnp.int32)
    last_valid = jnp.maximum(pends[-1] // tm - 1, 0)
    tile_e_raw = jnp.minimum(jnp.sum((pends[None, :] <= tile0[:, None]).astype(jnp.int32), axis=1),
                             N_EXPERTS - 1)
    tile_e = jnp.where(tile_v == 1, tile_e_raw, tile_e_raw[last_valid])
    r = jnp.arange(nt * tm, dtype=jnp.int32)
    e_r = jnp.repeat(tile_e, tm)
    within = r - pstarts[e_r]
    ok = jnp.logical_and(within < counts[e_r], jnp.repeat(tile_v, tm) == 1)
    a = order[jnp.clip(starts[e_r] + within, 0, A - 1)]
    src = jnp.where(ok, a // TOP_K, 0).astype(jnp.int32)
    dump = A + e_r * tm + jnp.clip(within - counts[e_r], 0, tm - 1)
    dst = jnp.where(ok, (a % TOP_K) * T + a // TOP_K, dump).astype(jnp.int32)
    return (tile_e, tile_v, (src * SUBLANES).reshape(nt, tm), (dst * SUBLANES).reshape(nt, tm),
            A + N_EXPERTS * tm)


def _combine_kernel(x_ref, y1_ref, y2_ref, r_ref, mod_ref, o_ref):
    tm = x_ref.shape[0]
    y = _tiles_to_rows(y1_ref, tm) * r_ref[:, 2:3] + _tiles_to_rows(y2_ref, tm) * r_ref[:, 3:4]
    o_ref[...] = x_ref[...] + mod_ref[5:6, :] * y


def _combine(x, y2, route, mod, tm):
    B, S, D = x.shape
    T = B * S
    per_b = S // tm
    out = pl.pallas_call(
        _combine_kernel,
        grid=(T // tm,),
        in_specs=[pl.BlockSpec((tm, D), lambda i: (i, 0)),
                  pl.BlockSpec((tm * SUBLANES, LANES), lambda i: (i, 0)),
                  pl.BlockSpec((tm * SUBLANES, LANES), lambda i: (T // tm + i, 0)),
                  pl.BlockSpec((tm, LANES), lambda i: (i, 0)),
                  pl.BlockSpec((None, 6, D), lambda i: (i // per_b, 0, 0))],
        out_specs=pl.BlockSpec((tm, D), lambda i: (i, 0)),
        out_shape=jax.ShapeDtypeStruct((T, D), _F32),
        compiler_params=_cparams(("parallel",)),
        name="moe_combine",
    )(x.reshape(T, D), y2, y2, route.reshape(T, LANES), mod)
    return out.reshape(B, S, D)


def _pad_cols(a, w):
    return jnp.pad(a, ((0, 0), (0, w - a.shape[1])))


def _rope_tables_even(S):
    pos = jnp.arange(S, dtype=jnp.int32)
    inv = ROPE_THETA ** (-jnp.arange(0, 32, 2, dtype=_F32) / 32)
    def cs(p):
        ang = p.astype(_F32)[:, None] * inv[None, :]
        return jnp.cos(ang), jnp.sin(ang)
    one = lambda w: jnp.ones((S, w), _F32)
    zero = lambda w: jnp.zeros((S, w), _F32)
    c, s = cs(pos)
    ca = jnp.concatenate([c, one(48), c, one(48)], 1)
    sa = jnp.concatenate([-s, zero(48), s, zero(48)], 1)
    cr, sr = cs(pos // GRID_W)
    cc, sc = cs(pos % GRID_W)
    cb = jnp.concatenate([cr, cc, one(32), cr, cc, one(32)], 1)
    sb = jnp.concatenate([-sr, -sc, zero(32), sr, sc, zero(32)], 1)
    return jnp.concatenate([ca, sa, cb, sb], 1)


def _slot_maps():
    r = MLA_ROPE // 2
    mla = ([MLA_NOPE + i for i in range(r)] + list(range(0, 64 - r))
           + [MLA_NOPE + r + i for i in range(r)] + list(range(64 - r, MLA_NOPE)))
    mla += [-1] * (LANES - len(mla))
    q = HEAD_DIM // 4
    gqa = (list(range(0, q)) + list(range(2 * q, 3 * q)) + [-1] * (64 - 2 * q)
           + list(range(q, 2 * q)) + list(range(3 * q, 4 * q)) + [-1] * (64 - 2 * q))
    return mla, gqa


def _to_slots(a, lane_map):
    idx = jnp.asarray([max(i, 0) for i in lane_map], jnp.int32)
    keep = jnp.asarray([1.0 if i >= 0 else 0.0 for i in lane_map], a.dtype)
    return jnp.take(a, idx, axis=-1) * keep


def _rope_tables_odd(S):
    pos = jnp.arange(S, dtype=_F32)
    inv = ROPE_THETA ** (-jnp.arange(0, HEAD_DIM, 2, dtype=_F32) / HEAD_DIM)
    ang = pos[:, None] * inv[None, :]
    c, s = jnp.cos(ang), jnp.sin(ang)
    return jnp.concatenate([c, c, c, c, -s, -s, s, s], 1)


def _tiles(S):
    return dict(tm_pre=min(S, 512), tq=min(S, 2048), tm_o=min(S, 512),
                tm_ffn=min(S, 512), tf_ffn=256, tm_qkv=min(S, 512),
                tl=128, tm_moe=min(S, 512), tf_moe=256, tm_c=min(S, 512))


def kernel(x, c, ada_even_w, ada_even_b, norm_even_mix, norm_even_ffn, even_w_in, mla_q_norm, mla_w_uq, mla_kv_norm, mla_w_ukv, mla_q_gain, mla_k_gain, gqa_q_gain, gqa_k_gain, even_w_out, ffn_w_gate, ffn_w_up, ffn_w_down, ada_odd_w, ada_odd_b, norm_odd_mix, norm_odd_ffn, dil_w_qkv, dil_q_gain, dil_k_gain, dil_w_out, moe_router, moe_w_gate, moe_w_up, moe_w_down):
    B, S, D = x.shape
    T = B * S
    cfg = _tiles(S)

    mod_e = _ada_mod(c, ada_even_w[0], ada_even_b[0]).reshape(B, 6, D)
    mod_o = _ada_mod(c, ada_odd_w[0], ada_odd_b[0]).reshape(B, 6, D)

    w = even_w_in[0]
    sp = [MLA_Q_RANK, MLA_Q_RANK + MLA_KV_RANK, MLA_Q_RANK + MLA_KV_RANK + MLA_ROPE]
    sp.append(sp[-1] + GQA_HEADS * HEAD_DIM)
    sp.append(sp[-1] + GQA_KV_HEADS * HEAD_DIM)
    w_cq, w_ckv, w_kpe = w[:, :sp[0]], w[:, sp[0]:sp[1]], w[:, sp[1]:sp[2]]
    w_qb, w_kb, w_vb = w[:, sp[2]:sp[3]], w[:, sp[3]:sp[4]], w[:, sp[4]:]
    mla_map, gqa_map = _slot_maps()
    na = MLA_NOPE + MLA_ROPE
    nope_only = [i if 0 <= i < MLA_NOPE else -1 for i in mla_map]
    rope_only = [i - MLA_NOPE if i >= MLA_NOPE else -1 for i in mla_map]
    gslots = lambda a, n: _to_slots(a.reshape(D, n, HEAD_DIM), gqa_map).reshape(D, n * LANES)
    w_in = jnp.concatenate([w_cq, w_ckv, _to_slots(w_kpe, rope_only), gslots(w_qb, GQA_HEADS),
                            gslots(w_kb, GQA_KV_HEADS)], axis=1).astype(_BF)
    w_uq = _to_slots(mla_w_uq[0].reshape(MLA_Q_RANK, MLA_HEADS, na), mla_map)
    w_uq = w_uq.reshape(MLA_Q_RANK, MLA_HEADS * LANES).astype(_BF)
    w_ukv = mla_w_ukv[0].reshape(MLA_KV_RANK, MLA_HEADS, MLA_NOPE + MLA_V)
    w_uk = _to_slots(w_ukv[:, :, :MLA_NOPE], nope_only).reshape(MLA_KV_RANK, MLA_HEADS * LANES).astype(_BF)
    w_uvt = w_ukv[:, :, MLA_NOPE:].reshape(MLA_KV_RANK, MLA_HEADS * MLA_V).T.astype(_BF)
    w_vbt = w_vb.T.astype(_BF)
    g_rows = [_to_slots(mla_q_gain[0], mla_map) * (na ** -0.5 * LOG2E), _to_slots(mla_k_gain[0], mla_map),
              _to_slots(gqa_q_gain[0], gqa_map) * (HEAD_DIM ** -0.5 * LOG2E), _to_slots(gqa_k_gain[0], gqa_map)]
    bound_a = na * jnp.max(jnp.abs(g_rows[0])) * jnp.max(jnp.abs(g_rows[1]))
    bound_b = HEAD_DIM * jnp.max(jnp.abs(g_rows[2])) * jnp.max(jnp.abs(g_rows[3]))
    fast_a, fast_b = bound_a <= SCORE_BOUND_MAX, bound_b <= SCORE_BOUND_MAX
    last_lane = jnp.zeros((LANES,), _F32).at[LANES - 1].set(1.0)
    g_rows += [last_lane, last_lane * jnp.where(fast_a, -bound_a, 0.0),
               last_lane, last_lane * jnp.where(fast_b, -bound_b, 0.0)]
    gains_e = jnp.stack(g_rows)
    rope_e = _rope_tables_even(S)
    q_all, k_all, vt_all = _pre_even(
        x, mod_e, norm_even_mix[0].reshape(1, D), w_in, mla_q_norm[0].reshape(1, -1), w_uq,
        mla_kv_norm[0].reshape(1, -1), w_uk, w_uvt, w_vbt, gains_e, rope_e, cfg["tm_pre"])
    as_flag = lambda f: f.astype(jnp.int32).reshape(1)
    o_a = _attention(as_flag(fast_a), q_all, k_all, vt_all, q_off=0, k_off=0, shared_kv=False, tq=cfg["tq"])
    o_b = _attention(as_flag(fast_b), q_all, k_all, vt_all, q_off=MLA_HEADS, k_off=MLA_HEADS,
                     shared_kv=True, tq=cfg["tq"])
    x2, h2 = _ffn(x, o_a, o_b, even_w_out[0].astype(_BF), mod_e, norm_even_ffn[0].reshape(1, D), mod_o,
                  norm_odd_mix[0].reshape(1, D), ffn_w_gate[0].astype(_BF), ffn_w_up[0].astype(_BF),
                  ffn_w_down[0].astype(_BF), cfg["tm_ffn"], cfg["tf_ffn"])

    hh = HEAD_DIM // 2
    pair = lambda v: jnp.concatenate([v[:hh], v[:hh], v[hh:], v[hh:]])
    gains_o = jnp.stack([pair(dil_q_gain[0]) * (HEAD_DIM ** -0.5 * LOG2E), pair(dil_k_gain[0])])
    wq, wk, wv = jnp.split(dil_w_qkv[0], 3, axis=1)
    perm = lambda w: w.reshape(D, D // LANES, 2, 2, hh).transpose(0, 1, 3, 2, 4).reshape(D, D)
    w_qkv = jnp.concatenate([perm(wq), perm(wk), wv], axis=1).astype(_BF)
    qd, kd, vd = _qkv(h2, w_qkv, gains_o, _rope_tables_odd(S), cfg["tm_qkv"])
    bound_d = HEAD_DIM * jnp.max(jnp.abs(gains_o[0])) * jnp.max(jnp.abs(gains_o[1]))
    fast_d = bound_d <= SCORE_BOUND_MAX
    o_d = _dilated(as_flag(fast_d), jnp.where(fast_d, -bound_d, 0.0), qd, kd, vd, cfg["tl"])
    r32 = _pad_cols(moe_router[0], LANES)
    r_hi = r32.astype(_BF)
    r_lo = (r32 - r_hi.astype(_F32)).astype(_BF)
    x3, h3, route = _oproj_route(x2, mod_o, norm_odd_ffn[0].reshape(1, D), dil_w_out[0].astype(_BF), o_d,
                                 jnp.stack([r_hi, r_lo]), cfg["tm_o"])
    tm = cfg["tm_moe"]
    tile_e, tile_v, src, dst, n_rows = _route_plan(route.reshape(T, LANES), T, tm)
    y2 = _moe(h3, tile_e, tile_v, src, dst, moe_w_gate[0].astype(_BF),
              moe_w_up[0].astype(_BF), moe_w_down[0].astype(_BF), n_rows, tm, cfg["tf_moe"])
    return _combine(x3, y2, route, mod_o, cfg["tm_c"])
```
